```python
import math
import jax, jax.numpy as jnp
from jax import lax
import numpy as np

D_MODEL = 1024
BATCH = 8
SEQ = 4096
DEPTH = 2

CHUNK = 64
D_FF = 2816
N_BRANCH = 4
BRANCH_W = D_MODEL // 2
POOL_WINDOWS = (2, 4, 8, 16)
N_POOL_GROUPS = len(POOL_WINDOWS)
POOL_GROUP_W = BRANCH_W // N_POOL_GROUPS
SCONV_K = 3
CCONV_K = 31
SGU_BLOCK = 128
SGU_HEADS = 4
SGU_HEAD_W = BRANCH_W // SGU_HEADS
COLS_A = BRANCH_W
COLS_B = 3 * BRANCH_W
COLS_C = 2 * BRANCH_W
COLS_D = 2 * BRANCH_W
COLS_G = N_BRANCH * D_MODEL
IN_COLS = COLS_A + COLS_B + COLS_C + COLS_D + COLS_G
SPLITS = (COLS_A, COLS_A + COLS_B, COLS_A + COLS_B + COLS_C, COLS_A + COLS_B + COLS_C + COLS_D)
EPS = 1e-6

kernel_name = "hybrid_gated_parallel_mixers_macaron"


def rms_norm(x, g):
    x32 = x.astype(jnp.float32)
    y = x32 * lax.rsqrt(jnp.mean(x32 * x32, axis=-1, keepdims=True) + EPS)
    return (y * g.astype(jnp.float32)).astype(x.dtype)


def layer_norm(x, g, b):
    x32 = x.astype(jnp.float32)
    mu = jnp.mean(x32, axis=-1, keepdims=True)
    var = jnp.mean(jnp.square(x32 - mu), axis=-1, keepdims=True)
    y = (x32 - mu) * lax.rsqrt(var + EPS)
    return (y * g.astype(jnp.float32) + b.astype(jnp.float32)).astype(x.dtype)


def swiglu_half(x, g, w13, w2):
    h = rms_norm(x, g)
    a, b = jnp.split(h @ w13, 2, axis=-1)
    return x + 0.5 * ((jax.nn.silu(a) * b) @ w2)


def causal_dwconv(x, w):
    k, c = w.shape
    return lax.conv_general_dilated(
        x, w[:, None, :].astype(x.dtype), window_strides=(1,), padding=[(k - 1, 0)],
        dimension_numbers=("NWC", "WIO", "NWC"), feature_group_count=c)


def pool_mixer(a, pool_w, pool_scale):
    bn, s, _ = a.shape
    a32 = a.astype(jnp.float32).reshape(bn, s, N_POOL_GROUPS, POOL_GROUP_W)
    cs = jnp.cumsum(a32, axis=1)
    t = jnp.arange(s)
    outs = []
    for gi, win in enumerate(POOL_WINDOWS):
        c = cs[:, :, gi]
        lag = jnp.pad(c, ((0, 0), (win, 0), (0, 0)))[:, :s]
        cnt = jnp.minimum(t + 1, win).astype(jnp.float32)[None, :, None]
        outs.append((c - lag) / cnt - a32[:, :, gi])
    d = jnp.stack(outs, axis=2).astype(a.dtype)
    y = jnp.einsum('bsgc,gcd->bsgd', d, pool_w).reshape(bn, s, BRANCH_W)
    return y * pool_scale


def short_conv_mixer(p, sconv_w):
    xin, bg, cg = jnp.split(p, 3, axis=-1)
    return bg * causal_dwconv(cg * xin, sconv_w)


def conformer_conv_mixer(p, cconv_w, ln_g, ln_b):
    a, b = jnp.split(p, 2, axis=-1)
    y = a * jax.nn.sigmoid(b)
    y = causal_dwconv(y, cconv_w)
    y = layer_norm(y, ln_g, ln_b)
    return jax.nn.silu(y)


def spatial_gating_mixer(p, ln_g, ln_b, sgu_w, sgu_b):
    u, v = jnp.split(jax.nn.gelu(p), 2, axis=-1)
    v = layer_norm(v, ln_g, ln_b)
    bn, s, _ = v.shape
    v = v.reshape(bn, s // SGU_BLOCK, SGU_BLOCK, SGU_HEADS, SGU_HEAD_W)
    i = jnp.arange(SGU_BLOCK)
    mask = (i[None, :] // CHUNK) <= (i[:, None] // CHUNK)
    w = jnp.where(mask[None], sgu_w, jnp.zeros((), sgu_w.dtype))
    z = jnp.einsum('hij,bnjhc->bnihc', w, v) + sgu_b.T[None, None, :, :, None]
    return u * z.reshape(bn, s, BRANCH_W)


def _fwd_setup_inputs(seed: int = 0) -> dict:
    key = jax.random.key(seed)
    ks = jax.random.split(key, 24)
    f32 = jnp.float32

    def nrm(k, shape, scale):
        return jax.random.normal(k, shape, f32) * scale

    def gain(k, shape):
        return 1.0 + 0.05 * jax.random.normal(k, shape, f32)

    L = DEPTH
    return {
        "x": jax.random.normal(ks[0], (BATCH, SEQ, D_MODEL), f32),
        "ffn1_norm": gain(ks[1], (L, D_MODEL)),
        "ffn1_w13": nrm(ks[2], (L, D_MODEL, 2 * D_FF), D_MODEL ** -0.5),
        "ffn1_w2": nrm(ks[3], (L, D_FF, D_MODEL), D_FF ** -0.5),
        "mix_norm": gain(ks[4], (L, D_MODEL)),
        "w_in": nrm(ks[5], (L, D_MODEL, IN_COLS), D_MODEL ** -0.5),
        "pool_w": nrm(ks[6], (L, N_POOL_GROUPS, POOL_GROUP_W, POOL_GROUP_W), POOL_GROUP_W ** -0.5),
        "pool_scale": gain(ks[7], (L, BRANCH_W)),
        "sconv_w": nrm(ks[8], (L, SCONV_K, BRANCH_W), SCONV_K ** -0.5),
        "cconv_w": nrm(ks[9], (L, CCONV_K, BRANCH_W), CCONV_K ** -0.5),
        "cconv_ln_g": gain(ks[10], (L, BRANCH_W)),
        "cconv_ln_b": nrm(ks[11], (L, BRANCH_W), 0.02),
        "sgu_ln_g": gain(ks[12], (L, BRANCH_W)),
        "sgu_ln_b": nrm(ks[13], (L, BRANCH_W), 0.02),
        "sgu_w": nrm(ks[14], (L, SGU_HEADS, SGU_BLOCK, SGU_BLOCK), SGU_BLOCK ** -0.5),
        "sgu_b": gain(ks[15], (L, SGU_HEADS, SGU_BLOCK)),
        "w_up": nrm(ks[16], (L, N_BRANCH, BRANCH_W, D_MODEL), BRANCH_W ** -0.5),
        "w_out": nrm(ks[17], (L, D_MODEL, D_MODEL), D_MODEL ** -0.5),
        "ffn2_norm": gain(ks[18], (L, D_MODEL)),
        "ffn2_w13": nrm(ks[19], (L, D_MODEL, 2 * D_FF), D_MODEL ** -0.5),
        "ffn2_w2": nrm(ks[20], (L, D_FF, D_MODEL), D_FF ** -0.5),
        "final_norm": gain(ks[21], (D_MODEL,)),
    }


def _fwd_reference(x, ffn1_norm, ffn1_w13, ffn1_w2, mix_norm, w_in, pool_w, pool_scale,
              sconv_w, cconv_w, cconv_ln_g, cconv_ln_b, sgu_ln_g, sgu_ln_b, sgu_w, sgu_b,
              w_up, w_out, ffn2_norm, ffn2_w13, ffn2_w2, final_norm):
    bn, s, d = x.shape
    for l in range(DEPTH):
        x = swiglu_half(x, ffn1_norm[l], ffn1_w13[l], ffn1_w2[l])
        h = rms_norm(x, mix_norm[l])
        proj = h @ w_in[l]
        pa, pb, pc, pd, pg = jnp.split(proj, SPLITS, axis=-1)
        ya = pool_mixer(pa, pool_w[l], pool_scale[l])
        yb = short_conv_mixer(pb, sconv_w[l])
        yc = conformer_conv_mixer(pc, cconv_w[l], cconv_ln_g[l], cconv_ln_b[l])
        yd = spatial_gating_mixer(pd, sgu_ln_g[l], sgu_ln_b[l], sgu_w[l], sgu_b[l])
        y = jnp.stack([ya, yb, yc, yd], axis=2)
        up = jnp.einsum('bsgc,gcd->bsgd', y, w_up[l])
        gates = jax.nn.sigmoid(pg.reshape(bn, s, N_BRANCH, d))
        merged = jnp.sum(gates * up, axis=2)
        x = x + merged @ w_out[l]
        x = swiglu_half(x, ffn2_norm[l], ffn2_w13[l], ffn2_w2[l])
    return rms_norm(x, final_norm)


import jax as _jax
import jax.numpy as _jnp

TWIN_FORMAT = 'train_step'
FWD_PARAMS = ['x', 'ffn1_norm', 'ffn1_w13', 'ffn1_w2', 'mix_norm', 'w_in', 'pool_w', 'pool_scale', 'sconv_w', 'cconv_w', 'cconv_ln_g', 'cconv_ln_b', 'sgu_ln_g', 'sgu_ln_b', 'sgu_w', 'sgu_b', 'w_up', 'w_out', 'ffn2_norm', 'ffn2_w13', 'ffn2_w2', 'final_norm']
TWIN_WEIGHTS = ['ffn1_norm', 'ffn1_w13', 'ffn1_w2', 'mix_norm', 'w_in', 'pool_w', 'pool_scale', 'sconv_w', 'cconv_w', 'cconv_ln_g', 'cconv_ln_b', 'sgu_ln_g', 'sgu_ln_b', 'sgu_w', 'sgu_b', 'w_up', 'w_out', 'ffn2_norm', 'ffn2_w13', 'ffn2_w2', 'final_norm']
TWIN_DIFF_INPUT = 'x'
TWIN_INPUTS = ['x', 'ffn1_norm', 'ffn1_w13', 'ffn1_w2', 'mix_norm', 'w_in', 'pool_w', 'pool_scale', 'sconv_w', 'cconv_w', 'cconv_ln_g', 'cconv_ln_b', 'sgu_ln_g', 'sgu_ln_b', 'sgu_w', 'sgu_b', 'w_up', 'w_out', 'ffn2_norm', 'ffn2_w13', 'ffn2_w2', 'final_norm', 'loss_target', 'm_ffn1_norm', 'm_ffn1_w13', 'm_ffn1_w2', 'm_mix_norm', 'm_w_in', 'm_pool_w', 'm_pool_scale', 'm_sconv_w', 'm_cconv_w', 'm_cconv_ln_g', 'm_cconv_ln_b', 'm_sgu_ln_g', 'm_sgu_ln_b', 'm_sgu_w', 'm_sgu_b', 'm_w_up', 'm_w_out', 'm_ffn2_norm', 'm_ffn2_w13', 'm_ffn2_w2', 'm_final_norm', 'v_ffn1_norm', 'v_ffn1_w13', 'v_ffn1_w2', 'v_mix_norm', 'v_w_in', 'v_pool_w', 'v_pool_scale', 'v_sconv_w', 'v_cconv_w', 'v_cconv_ln_g', 'v_cconv_ln_b', 'v_sgu_ln_g', 'v_sgu_ln_b', 'v_sgu_w', 'v_sgu_b', 'v_w_up', 'v_w_out', 'v_ffn2_norm', 'v_ffn2_w13', 'v_ffn2_w2', 'v_final_norm']
TWIN_OUTPUTS = ['loss', 'grad_x', 'grad_ffn1_norm', 'grad_ffn1_w13', 'grad_ffn1_w2', 'grad_mix_norm', 'grad_w_in', 'grad_pool_w', 'grad_pool_scale', 'grad_sconv_w', 'grad_cconv_w', 'grad_cconv_ln_g', 'grad_cconv_ln_b', 'grad_sgu_ln_g', 'grad_sgu_ln_b', 'grad_sgu_w', 'grad_sgu_b', 'grad_w_up', 'grad_w_out', 'grad_ffn2_norm', 'grad_ffn2_w13', 'grad_ffn2_w2', 'grad_final_norm', 'delta_ffn1_norm', 'delta_ffn1_w13', 'delta_ffn1_w2', 'delta_mix_norm', 'delta_w_in', 'delta_pool_w', 'delta_pool_scale', 'delta_sconv_w', 'delta_cconv_w', 'delta_cconv_ln_g', 'delta_cconv_ln_b', 'delta_sgu_ln_g', 'delta_sgu_ln_b', 'delta_sgu_w', 'delta_sgu_b', 'delta_w_up', 'delta_w_out', 'delta_ffn2_norm', 'delta_ffn2_w13', 'delta_ffn2_w2', 'delta_final_norm', 'new_m_ffn1_norm', 'new_m_ffn1_w13', 'new_m_ffn1_w2', 'new_m_mix_norm', 'new_m_w_in', 'new_m_pool_w', 'new_m_pool_scale', 'new_m_sconv_w', 'new_m_cconv_w', 'new_m_cconv_ln_g', 'new_m_cconv_ln_b', 'new_m_sgu_ln_g', 'new_m_sgu_ln_b', 'new_m_sgu_w', 'new_m_sgu_b', 'new_m_w_up', 'new_m_w_out', 'new_m_ffn2_norm', 'new_m_ffn2_w13', 'new_m_ffn2_w2', 'new_m_final_norm', 'new_v_ffn1_norm', 'new_v_ffn1_w13', 'new_v_ffn1_w2', 'new_v_mix_norm', 'new_v_w_in', 'new_v_pool_w', 'new_v_pool_scale', 'new_v_sconv_w', 'new_v_cconv_w', 'new_v_cconv_ln_g', 'new_v_cconv_ln_b', 'new_v_sgu_ln_g', 'new_v_sgu_ln_b', 'new_v_sgu_w', 'new_v_sgu_b', 'new_v_w_up', 'new_v_w_out', 'new_v_ffn2_norm', 'new_v_ffn2_w13', 'new_v_ffn2_w2', 'new_v_final_norm']
TWIN_LEAF_KINDS = {'loss': 'loss', 'grad_x': 'grad_x', 'grad_ffn1_norm': 'grad_w', 'grad_ffn1_w13': 'grad_w', 'grad_ffn1_w2': 'grad_w', 'grad_mix_norm': 'grad_w', 'grad_w_in': 'grad_w', 'grad_pool_w': 'grad_w', 'grad_pool_scale': 'grad_w', 'grad_sconv_w': 'grad_w', 'grad_cconv_w': 'grad_w', 'grad_cconv_ln_g': 'grad_w', 'grad_cconv_ln_b': 'grad_w', 'grad_sgu_ln_g': 'grad_w', 'grad_sgu_ln_b': 'grad_w', 'grad_sgu_w': 'grad_w', 'grad_sgu_b': 'grad_w', 'grad_w_up': 'grad_w', 'grad_w_out': 'grad_w', 'grad_ffn2_norm': 'grad_w', 'grad_ffn2_w13': 'grad_w', 'grad_ffn2_w2': 'grad_w', 'grad_final_norm': 'grad_w', 'delta_ffn1_norm': 'delta_w', 'delta_ffn1_w13': 'delta_w', 'delta_ffn1_w2': 'delta_w', 'delta_mix_norm': 'delta_w', 'delta_w_in': 'delta_w', 'delta_pool_w': 'delta_w', 'delta_pool_scale': 'delta_w', 'delta_sconv_w': 'delta_w', 'delta_cconv_w': 'delta_w', 'delta_cconv_ln_g': 'delta_w', 'delta_cconv_ln_b': 'delta_w', 'delta_sgu_ln_g': 'delta_w', 'delta_sgu_ln_b': 'delta_w', 'delta_sgu_w': 'delta_w', 'delta_sgu_b': 'delta_w', 'delta_w_up': 'delta_w', 'delta_w_out': 'delta_w', 'delta_ffn2_norm': 'delta_w', 'delta_ffn2_w13': 'delta_w', 'delta_ffn2_w2': 'delta_w', 'delta_final_norm': 'delta_w', 'new_m_ffn1_norm': 'new_m', 'new_m_ffn1_w13': 'new_m', 'new_m_ffn1_w2': 'new_m', 'new_m_mix_norm': 'new_m', 'new_m_w_in': 'new_m', 'new_m_pool_w': 'new_m', 'new_m_pool_scale': 'new_m', 'new_m_sconv_w': 'new_m', 'new_m_cconv_w': 'new_m', 'new_m_cconv_ln_g': 'new_m', 'new_m_cconv_ln_b': 'new_m', 'new_m_sgu_ln_g': 'new_m', 'new_m_sgu_ln_b': 'new_m', 'new_m_sgu_w': 'new_m', 'new_m_sgu_b': 'new_m', 'new_m_w_up': 'new_m', 'new_m_w_out': 'new_m', 'new_m_ffn2_norm': 'new_m', 'new_m_ffn2_w13': 'new_m', 'new_m_ffn2_w2': 'new_m', 'new_m_final_norm': 'new_m', 'new_v_ffn1_norm': 'new_v', 'new_v_ffn1_w13': 'new_v', 'new_v_ffn1_w2': 'new_v', 'new_v_mix_norm': 'new_v', 'new_v_w_in': 'new_v', 'new_v_pool_w': 'new_v', 'new_v_pool_scale': 'new_v', 'new_v_sconv_w': 'new_v', 'new_v_cconv_w': 'new_v', 'new_v_cconv_ln_g': 'new_v', 'new_v_cconv_ln_b': 'new_v', 'new_v_sgu_ln_g': 'new_v', 'new_v_sgu_ln_b': 'new_v', 'new_v_sgu_w': 'new_v', 'new_v_sgu_b': 'new_v', 'new_v_w_up': 'new_v', 'new_v_w_out': 'new_v', 'new_v_ffn2_norm': 'new_v', 'new_v_ffn2_w13': 'new_v', 'new_v_ffn2_w2': 'new_v', 'new_v_final_norm': 'new_v'}


def _forward(args):
    return _fwd_reference(*[args[k] for k in FWD_PARAMS])


def _output_shape():
    out = _jax.eval_shape(lambda: _forward(_fwd_setup_inputs(0)))
    return out.shape, out.dtype

N_MICROBATCH = 1
ADAM_LR = 0.001
ADAM_B1 = 0.9
ADAM_B2 = 0.999
ADAM_EPS = 1e-08
ADAM_WD = 0.01
ADAM_STEP = 10
PER_EXAMPLE_BATCH_AXIS = {'x': 0, 'loss_target': 0}
SHARED_INPUTS = []
_WEIGHT_DTYPES = {'ffn1_norm': _jnp.float32, 'ffn1_w13': _jnp.float32, 'ffn1_w2': _jnp.float32, 'mix_norm': _jnp.float32, 'w_in': _jnp.float32, 'pool_w': _jnp.float32, 'pool_scale': _jnp.float32, 'sconv_w': _jnp.float32, 'cconv_w': _jnp.float32, 'cconv_ln_g': _jnp.float32, 'cconv_ln_b': _jnp.float32, 'sgu_ln_g': _jnp.float32, 'sgu_ln_b': _jnp.float32, 'sgu_w': _jnp.float32, 'sgu_b': _jnp.float32, 'w_up': _jnp.float32, 'w_out': _jnp.float32, 'ffn2_norm': _jnp.float32, 'ffn2_w13': _jnp.float32, 'ffn2_w2': _jnp.float32, 'final_norm': _jnp.float32}
MOMENT_SCALE = {'ffn1_norm': 8.880528e-02, 'ffn1_w13': 3.737522e-02, 'ffn1_w2': 6.092959e-02, 'mix_norm': 1.745788e-01, 'w_in': 6.160885e-02, 'pool_w': 9.143391e-02, 'pool_scale': 9.279957e-02, 'sconv_w': 1.056336e-01, 'cconv_w': 6.468957e-02, 'cconv_ln_g': 7.689565e-02, 'cconv_ln_b': 7.754088e-02, 'sgu_ln_g': 5.751695e-02, 'sgu_ln_b': 5.864540e-02, 'sgu_w': 5.828989e-02, 'sgu_b': 6.931589e-02, 'w_up': 6.275628e-02, 'w_out': 1.262360e-01, 'ffn2_norm': 5.473154e-02, 'ffn2_w13': 2.364483e-02, 'ffn2_w2': 3.861921e-02, 'final_norm': 3.208870e+01}


def _to_microbatches(a, axis):
    t = _jnp.moveaxis(a, axis, 0)
    t = t.reshape((N_MICROBATCH, t.shape[0] // N_MICROBATCH) + t.shape[1:])
    return _jnp.moveaxis(t, 1, axis + 1)


def setup_inputs(seed: int = 0) -> dict:
    inp = _fwd_setup_inputs(seed)
    key = _jax.random.fold_in(_jax.random.key(seed), 7919)
    shape, _ = _output_shape()
    out = dict(inp)
    out["loss_target"] = _jax.random.normal(_jax.random.fold_in(key, 0), shape, _jnp.float32)
    for i, name in enumerate(TWIN_WEIGHTS):
        w = inp[name].astype(_jnp.float32)
        if MOMENT_SCALE is None:
            s = _jnp.sqrt(_jnp.mean(_jnp.square(w)) + 1e-30)
        else:
            s = MOMENT_SCALE[name]
        km, kv = _jax.random.split(_jax.random.fold_in(key, i + 1))
        out[name] = w
        out["m_" + name] = s * _jax.random.normal(km, w.shape, _jnp.float32)
        out["v_" + name] = (s * s) * _jax.random.uniform(kv, w.shape, _jnp.float32, 0.5, 1.5)
    if N_MICROBATCH > 1:
        for name, axis in PER_EXAMPLE_BATCH_AXIS.items():
            out[name] = _to_microbatches(out[name], axis)
    return {'x': out['x'], 'ffn1_norm': out['ffn1_norm'], 'ffn1_w13': out['ffn1_w13'], 'ffn1_w2': out['ffn1_w2'], 'mix_norm': out['mix_norm'], 'w_in': out['w_in'], 'pool_w': out['pool_w'], 'pool_scale': out['pool_scale'], 'sconv_w': out['sconv_w'], 'cconv_w': out['cconv_w'], 'cconv_ln_g': out['cconv_ln_g'], 'cconv_ln_b': out['cconv_ln_b'], 'sgu_ln_g': out['sgu_ln_g'], 'sgu_ln_b': out['sgu_ln_b'], 'sgu_w': out['sgu_w'], 'sgu_b': out['sgu_b'], 'w_up': out['w_up'], 'w_out': out['w_out'], 'ffn2_norm': out['ffn2_norm'], 'ffn2_w13': out['ffn2_w13'], 'ffn2_w2': out['ffn2_w2'], 'final_norm': out['final_norm'], 'loss_target': out['loss_target'], 'm_ffn1_norm': out['m_ffn1_norm'], 'm_ffn1_w13': out['m_ffn1_w13'], 'm_ffn1_w2': out['m_ffn1_w2'], 'm_mix_norm': out['m_mix_norm'], 'm_w_in': out['m_w_in'], 'm_pool_w': out['m_pool_w'], 'm_pool_scale': out['m_pool_scale'], 'm_sconv_w': out['m_sconv_w'], 'm_cconv_w': out['m_cconv_w'], 'm_cconv_ln_g': out['m_cconv_ln_g'], 'm_cconv_ln_b': out['m_cconv_ln_b'], 'm_sgu_ln_g': out['m_sgu_ln_g'], 'm_sgu_ln_b': out['m_sgu_ln_b'], 'm_sgu_w': out['m_sgu_w'], 'm_sgu_b': out['m_sgu_b'], 'm_w_up': out['m_w_up'], 'm_w_out': out['m_w_out'], 'm_ffn2_norm': out['m_ffn2_norm'], 'm_ffn2_w13': out['m_ffn2_w13'], 'm_ffn2_w2': out['m_ffn2_w2'], 'm_final_norm': out['m_final_norm'], 'v_ffn1_norm': out['v_ffn1_norm'], 'v_ffn1_w13': out['v_ffn1_w13'], 'v_ffn1_w2': out['v_ffn1_w2'], 'v_mix_norm': out['v_mix_norm'], 'v_w_in': out['v_w_in'], 'v_pool_w': out['v_pool_w'], 'v_pool_scale': out['v_pool_scale'], 'v_sconv_w': out['v_sconv_w'], 'v_cconv_w': out['v_cconv_w'], 'v_cconv_ln_g': out['v_cconv_ln_g'], 'v_cconv_ln_b': out['v_cconv_ln_b'], 'v_sgu_ln_g': out['v_sgu_ln_g'], 'v_sgu_ln_b': out['v_sgu_ln_b'], 'v_sgu_w': out['v_sgu_w'], 'v_sgu_b': out['v_sgu_b'], 'v_w_up': out['v_w_up'], 'v_w_out': out['v_w_out'], 'v_ffn2_norm': out['v_ffn2_norm'], 'v_ffn2_w13': out['v_ffn2_w13'], 'v_ffn2_w2': out['v_ffn2_w2'], 'v_final_norm': out['v_final_norm']}


def _loss(weights, diff, rest, loss_target):
    with _jax.named_scope("forward"):
        args = {**rest, TWIN_DIFF_INPUT: diff, **{k: w.astype(_WEIGHT_DTYPES[k]) for k, w in weights.items()}}
        y = _forward(args)
    with _jax.named_scope("loss_head"):
        err = _jnp.square(y.astype(_jnp.float32) - loss_target)
        return 0.5 * _jnp.sum(_jnp.mean(err, axis=-1)) if err.ndim else 0.5 * err


def _adamw(w, g, m, v):
    m = ADAM_B1 * m + (1.0 - ADAM_B1) * g
    v = ADAM_B2 * v + (1.0 - ADAM_B2) * _jnp.square(g)
    m_hat = m / (1.0 - ADAM_B1 ** ADAM_STEP)
    v_hat = v / (1.0 - ADAM_B2 ** ADAM_STEP)
    delta = -ADAM_LR * (m_hat / (_jnp.sqrt(v_hat) + ADAM_EPS) + ADAM_WD * w)
    return delta, m, v


def reference(x, ffn1_norm, ffn1_w13, ffn1_w2, mix_norm, w_in, pool_w, pool_scale, sconv_w, cconv_w, cconv_ln_g, cconv_ln_b, sgu_ln_g, sgu_ln_b, sgu_w, sgu_b, w_up, w_out, ffn2_norm, ffn2_w13, ffn2_w2, final_norm, loss_target, m_ffn1_norm, m_ffn1_w13, m_ffn1_w2, m_mix_norm, m_w_in, m_pool_w, m_pool_scale, m_sconv_w, m_cconv_w, m_cconv_ln_g, m_cconv_ln_b, m_sgu_ln_g, m_sgu_ln_b, m_sgu_w, m_sgu_b, m_w_up, m_w_out, m_ffn2_norm, m_ffn2_w13, m_ffn2_w2, m_final_norm, v_ffn1_norm, v_ffn1_w13, v_ffn1_w2, v_mix_norm, v_w_in, v_pool_w, v_pool_scale, v_sconv_w, v_cconv_w, v_cconv_ln_g, v_cconv_ln_b, v_sgu_ln_g, v_sgu_ln_b, v_sgu_w, v_sgu_b, v_w_up, v_w_out, v_ffn2_norm, v_ffn2_w13, v_ffn2_w2, v_final_norm):
    given = dict(x=x, ffn1_norm=ffn1_norm, ffn1_w13=ffn1_w13, ffn1_w2=ffn1_w2, mix_norm=mix_norm, w_in=w_in, pool_w=pool_w, pool_scale=pool_scale, sconv_w=sconv_w, cconv_w=cconv_w, cconv_ln_g=cconv_ln_g, cconv_ln_b=cconv_ln_b, sgu_ln_g=sgu_ln_g, sgu_ln_b=sgu_ln_b, sgu_w=sgu_w, sgu_b=sgu_b, w_up=w_up, w_out=w_out, ffn2_norm=ffn2_norm, ffn2_w13=ffn2_w13, ffn2_w2=ffn2_w2, final_norm=final_norm, loss_target=loss_target, m_ffn1_norm=m_ffn1_norm, m_ffn1_w13=m_ffn1_w13, m_ffn1_w2=m_ffn1_w2, m_mix_norm=m_mix_norm, m_w_in=m_w_in, m_pool_w=m_pool_w, m_pool_scale=m_pool_scale, m_sconv_w=m_sconv_w, m_cconv_w=m_cconv_w, m_cconv_ln_g=m_cconv_ln_g, m_cconv_ln_b=m_cconv_ln_b, m_sgu_ln_g=m_sgu_ln_g, m_sgu_ln_b=m_sgu_ln_b, m_sgu_w=m_sgu_w, m_sgu_b=m_sgu_b, m_w_up=m_w_up, m_w_out=m_w_out, m_ffn2_norm=m_ffn2_norm, m_ffn2_w13=m_ffn2_w13, m_ffn2_w2=m_ffn2_w2, m_final_norm=m_final_norm, v_ffn1_norm=v_ffn1_norm, v_ffn1_w13=v_ffn1_w13, v_ffn1_w2=v_ffn1_w2, v_mix_norm=v_mix_norm, v_w_in=v_w_in, v_pool_w=v_pool_w, v_pool_scale=v_pool_scale, v_sconv_w=v_sconv_w, v_cconv_w=v_cconv_w, v_cconv_ln_g=v_cconv_ln_g, v_cconv_ln_b=v_cconv_ln_b, v_sgu_ln_g=v_sgu_ln_g, v_sgu_ln_b=v_sgu_ln_b, v_sgu_w=v_sgu_w, v_sgu_b=v_sgu_b, v_w_up=v_w_up, v_w_out=v_w_out, v_ffn2_norm=v_ffn2_norm, v_ffn2_w13=v_ffn2_w13, v_ffn2_w2=v_ffn2_w2, v_final_norm=v_final_norm)
    weights = {n: given[n] for n in TWIN_WEIGHTS}
    shared = {n: given[n] for n in SHARED_INPUTS}
    per_example = {n: given[n] for n in ['x']}
    grad_fn = _jax.value_and_grad(_loss, argnums=(0, 1))

    def one_microbatch(ex, loss_target):
        ex = dict(ex)
        diff = ex.pop(TWIN_DIFF_INPUT)
        return grad_fn(weights, diff, {**shared, **ex}, loss_target)

    if N_MICROBATCH == 1:
        loss, (grad_w, grad_x) = one_microbatch(per_example, given["loss_target"])
    else:
        def body(carry, xs):
            loss_sum, grad_sum = carry
            l_k, (gw_k, gx_k) = one_microbatch(xs[0], xs[1])
            with _jax.named_scope("update"):
                return (loss_sum + l_k, _jax.tree.map(_jnp.add, grad_sum, gw_k)), gx_k

        init = (_jnp.zeros((), _jnp.float32), _jax.tree.map(_jnp.zeros_like, weights))
        (loss, grad_w), grad_x = _jax.lax.scan(body, init, (per_example, given["loss_target"]))
    with _jax.named_scope("update"):
        delta_w, new_m, new_v = {}, {}, {}
        for n in TWIN_WEIGHTS:
            delta_w[n], new_m[n], new_v[n] = _adamw(weights[n], grad_w[n], given["m_" + n], given["v_" + n])
    return (loss, grad_x, *[grad_w[n] for n in TWIN_WEIGHTS], *[delta_w[n] for n in TWIN_WEIGHTS],
            *[new_m[n] for n in TWIN_WEIGHTS], *[new_v[n] for n in TWIN_WEIGHTS])
```

```python
import jax
import jax.numpy as jnp
from jax import lax
from jax.experimental import pallas as pl
from jax.experimental.pallas import tpu as pltpu

D = 1024
FF = 2816
BW = 512
NBR = 4
MIXC = 4096
INC = 8192
GW = 128
CHUNK = 64
SK = 3
CK = 31
SKP = 8
CKP = 32
HALO = 32
TOKEN_TILE = 256
BRANCH_TILE = 512
EPS = 1e-6
NCHIP = 4
NDEV = 8

ADAM_LR = 0.001
ADAM_B1 = 0.9
ADAM_B2 = 0.999
ADAM_EPS = 1e-08
ADAM_WD = 0.01
ADAM_STEP = 10

VMEM_LIMIT = 56 * 1024 * 1024

BF = jnp.bfloat16
F32 = jnp.float32
MESH = pl.DeviceIdType.MESH
NT = (((1,), (1,)), ((), ()))
TN = (((0,), (0,)), ((), ()))


def _params(n_axes):
    return pltpu.CompilerParams(dimension_semantics=("arbitrary",) * n_axes, vmem_limit_bytes=VMEM_LIMIT)


def _full(shape):
    nd = len(shape)
    return pl.BlockSpec(shape, lambda *_: (0,) * nd, pipeline_mode=pl.Buffered(1))


def _rows(ts, width, col=0):
    return pl.BlockSpec((ts, width), lambda i: (i, col))


def _sig(v):
    return jax.nn.sigmoid(v)


def _rms_stats(x):
    r = lax.rsqrt(jnp.mean(x * x, axis=-1, keepdims=True) + EPS)
    return r, x * r


def _rms_bwd(x, g, dh):
    r, xh = _rms_stats(x)
    dg = jnp.sum(dh * xh, axis=0, keepdims=True)
    dxh = dh * g
    dx = r * (dxh - xh * jnp.mean(dxh * xh, axis=-1, keepdims=True))
    return dx, dg


def _accumulate(ref, val, first):
    @pl.when(first)
    def _():
        ref[...] = val

    @pl.when(jnp.logical_not(first))
    def _():
        ref[...] += val


def _ffn_fwd(x, g, w13, w2):
    s_len = x.shape[0]
    ts = min(TOKEN_TILE, s_len)

    def body(x_ref, g_ref, w13_ref, w2_ref, xo_ref, h_ref, ab_ref):
        xv = x_ref[...]
        r, xh = _rms_stats(xv)
        h = (xh * g_ref[...]).astype(BF)
        h_ref[...] = h
        ab = jnp.dot(h, w13_ref[...], preferred_element_type=F32)
        ab_ref[...] = ab.astype(BF)
        a = ab[:, :FF]
        b = ab[:, FF:]
        s = (a * _sig(a) * b).astype(BF)
        xo_ref[...] = xv + 0.5 * jnp.dot(s, w2_ref[...], preferred_element_type=F32)

    return pl.pallas_call(
        body, name="ffn_fwd", grid=(s_len // ts,),
        in_specs=[_rows(ts, D), _full((1, D)), _full((D, 2 * FF)), _full((FF, D))],
        out_specs=[_rows(ts, D), _rows(ts, D), _rows(ts, 2 * FF)],
        out_shape=[jax.ShapeDtypeStruct((s_len, D), F32), jax.ShapeDtypeStruct((s_len, D), BF),
                   jax.ShapeDtypeStruct((s_len, 2 * FF), BF)],
        compiler_params=_params(1),
    )(x, g, w13, w2)


def _ffn_bwd(dxo, x, g, ab, w13, w2):
    s_len = x.shape[0]
    ts = min(TOKEN_TILE, s_len)

    def body(dxo_ref, x_ref, g_ref, ab_ref, w13_ref, w2_ref, dxi_ref, dab_ref, s_ref, dy_ref, dg_ref):
        i = pl.program_id(0)
        dxo_v = dxo_ref[...]
        dy = (0.5 * dxo_v).astype(BF)
        dy_ref[...] = dy
        ds = lax.dot_general(dy, w2_ref[...], NT, preferred_element_type=F32)
        a = ab_ref[:, :FF].astype(F32)
        b = ab_ref[:, FF:].astype(F32)
        sg = _sig(a)
        sil = a * sg
        s_ref[...] = (sil * b).astype(BF)
        dab_ref[:, :FF] = (ds * b * (sg * (1.0 + a * (1.0 - sg)))).astype(BF)
        dab_ref[:, FF:] = (ds * sil).astype(BF)
        dh = lax.dot_general(dab_ref[...], w13_ref[...], NT, preferred_element_type=F32)
        dx, dg = _rms_bwd(x_ref[...], g_ref[...], dh)
        dxi_ref[...] = dxo_v + dx
        _accumulate(dg_ref, dg, i == 0)

    return pl.pallas_call(
        body, name="ffn_bwd", grid=(s_len // ts,),
        in_specs=[_rows(ts, D), _rows(ts, D), _full((1, D)), _rows(ts, 2 * FF), _full((D, 2 * FF)), _full((FF, D))],
        out_specs=[_rows(ts, D), _rows(ts, 2 * FF), _rows(ts, FF), _rows(ts, D), pl.BlockSpec((1, D), lambda i: (0, 0))],
        out_shape=[jax.ShapeDtypeStruct((s_len, D), F32), jax.ShapeDtypeStruct((s_len, 2 * FF), BF),
                   jax.ShapeDtypeStruct((s_len, FF), BF), jax.ShapeDtypeStruct((s_len, D), BF),
                   jax.ShapeDtypeStruct((1, D), F32)],
        compiler_params=_params(1),
    )(dxo, x, g, ab, w13, w2)


def _wgrad(a, b, tk, tn, name):
    s_len, k = a.shape
    n = b.shape[1]

    def body(a_ref, b_ref, o_ref):
        o_ref[...] = lax.dot_general(a_ref[...], b_ref[...], TN, preferred_element_type=F32).astype(BF)

    return pl.pallas_call(
        body, name=name, grid=(k // tk, n // tn),
        in_specs=[pl.BlockSpec((s_len, tk), lambda i, j: (0, i)), pl.BlockSpec((s_len, tn), lambda i, j: (0, j))],
        out_specs=pl.BlockSpec((tk, tn), lambda i, j: (i, j)),
        out_shape=jax.ShapeDtypeStruct((k, n), BF),
        compiler_params=_params(2),
    )(a, b)


def _wgrad_groups(a, b, ka, nb, name):
    s_len = a.shape[0]
    groups = a.shape[1] // ka

    def body(a_ref, b_ref, o_ref):
        o_ref[...] = lax.dot_general(a_ref[...], b_ref[...], TN, preferred_element_type=F32).astype(BF)

    return pl.pallas_call(
        body, name=name, grid=(groups,),
        in_specs=[pl.BlockSpec((s_len, ka), lambda gi: (0, gi)), pl.BlockSpec((s_len, nb), lambda gi: (0, gi))],
        out_specs=pl.BlockSpec((ka, nb), lambda gi: (gi, 0)),
        out_shape=jax.ShapeDtypeStruct((groups * ka, nb), BF),
        compiler_params=_params(1),
    )(a, b)


def _mix_in(x, g, w_in):
    s_len = x.shape[0]
    ts = min(TOKEN_TILE, s_len)

    def body(x_ref, g_ref, w_ref, h_ref, p_ref):
        _, xh = _rms_stats(x_ref[...])
        h = (xh * g_ref[...]).astype(BF)
        h_ref[...] = h
        p_ref[...] = jnp.dot(h, w_ref[...], preferred_element_type=F32).astype(BF)

    return pl.pallas_call(
        body, name="mix_in", grid=(s_len // ts,),
        in_specs=[_rows(ts, D), _full((1, D)), _full((D, INC))],
        out_specs=[_rows(ts, D), _rows(ts, INC)],
        out_shape=[jax.ShapeDtypeStruct((s_len, D), BF), jax.ShapeDtypeStruct((s_len, INC), BF)],
        compiler_params=_params(1),
    )(x, g, w_in)


def _shift(e, j):
    n = e.shape[0]
    j = j % n
    return e if j == 0 else pltpu.roll(e, j, 0)


def _ln_stats(z):
    mu = jnp.mean(z, axis=-1, keepdims=True)
    zc = z - mu
    rs = lax.rsqrt(jnp.mean(zc * zc, axis=-1, keepdims=True) + EPS)
    return rs, zc * rs


def _ln_bwd(rs, zn, dzn):
    return rs * (dzn - jnp.mean(dzn, axis=-1, keepdims=True) - zn * jnp.mean(dzn * zn, axis=-1, keepdims=True))


_GELU_C0 = 0.7978845608028654
_GELU_C1 = 0.044715


def _gelu(p):
    th = jnp.tanh(_GELU_C0 * (p + _GELU_C1 * p * p * p))
    return 0.5 * p * (1.0 + th), th


def _gelu_grad(p, th):
    return 0.5 * (1.0 + th) + 0.5 * p * (1.0 - th * th) * (_GELU_C0 * (1.0 + 3.0 * _GELU_C1 * p * p))


def _pool_diff(a, t, sign):
    outs = []
    for gi in range(NBR):
        win = 2 ** (gi + 1)
        ag = a[:, gi * GW:(gi + 1) * GW]
        cnt = jnp.clip(t + 1, 1, win).astype(F32)
        ws = ag if sign > 0 else ag / cnt
        for s in range(gi + 1):
            ws = ws + _shift(ws, sign * (2 ** s))
        outs.append((ws / cnt if sign > 0 else ws) - ag)
    return outs


def _sgu_mask():
    row = lax.broadcasted_iota(jnp.int32, (GW, GW), 0)
    col = lax.broadcasted_iota(jnp.int32, (GW, GW), 1)
    return (col // CHUNK) <= (row // CHUNK)


def _assemble(pe_ref, prev_ref, cur_ref, next_ref, i, last, ts):
    pe_ref[0:HALO, :] = jnp.where(i > 0, prev_ref[...], jnp.zeros_like(prev_ref))
    pe_ref[HALO:HALO + ts, :] = cur_ref[...]
    if next_ref is not None:
        pe_ref[HALO + ts:, :] = jnp.where(i < last, next_ref[...], jnp.zeros_like(next_ref))


def _halo_specs(ts, width, s_len, with_next):
    per = ts // HALO
    specs = [pl.BlockSpec((HALO, width), lambda i: (jnp.maximum(i * per - 1, 0), 0)),
             pl.BlockSpec((ts, width), lambda i: (i, 0))]
    if with_next:
        specs.append(pl.BlockSpec((HALO, width), lambda i: (jnp.minimum((i + 1) * per, s_len // HALO - 1), 0)))
    return specs


def _branch_weights_specs():
    return [_full((NBR, GW, GW)), _full((1, BW)), _full((SKP, BW)), _full((CKP, BW)), _full((1, BW)), _full((1, BW)),
            _full((1, BW)), _full((1, BW)), _full((NBR, GW, GW)), _full((GW, NBR))]


def _mix_branches_fwd(p, pool_w, pool_scale, sconv_w, cconv_w, cln_g, cln_b, sln_g, sln_b, sgu_w, sgu_bt):
    s_len = p.shape[0]
    ts = min(BRANCH_TILE, s_len)
    ext = HALO + ts

    def body(pp_ref, pc_ref, pw_ref, ps_ref, sw_ref, cw_ref, clg_ref, clb_ref, slg_ref, slb_ref, gw_ref, gb_ref,
             y_ref, pe_ref):
        i = pl.program_id(0)
        _assemble(pe_ref, pp_ref, pc_ref, None, i, 0, ts)
        t = i * ts - HALO + lax.broadcasted_iota(jnp.int32, (ext, 1), 0)

        dgs = _pool_diff(pe_ref[:, 0:BW].astype(F32), t, 1)
        for gi in range(NBR):
            e = jnp.dot(dgs[gi][HALO:].astype(BF), pw_ref[gi].astype(BF), preferred_element_type=F32)
            y_ref[:, gi * GW:(gi + 1) * GW] = (e * ps_ref[:, gi * GW:(gi + 1) * GW]).astype(BF)

        xin = pe_ref[:, BW:2 * BW].astype(F32)
        cg = pe_ref[:, 3 * BW:4 * BW].astype(F32)
        q = cg * xin
        cv = sw_ref[2:3, :] * q + sw_ref[1:2, :] * _shift(q, 1) + sw_ref[0:1, :] * _shift(q, 2)
        y_ref[:, BW:2 * BW] = (pe_ref[HALO:, 2 * BW:3 * BW].astype(F32) * cv[HALO:]).astype(BF)

        yg = pe_ref[:, 4 * BW:5 * BW].astype(F32) * _sig(pe_ref[:, 5 * BW:6 * BW].astype(F32))
        z = cw_ref[CK - 1:CK, :] * yg
        for j in range(1, CK):
            z = z + cw_ref[CK - 1 - j:CK - j, :] * _shift(yg, j)
        _, zn = _ln_stats(z[HALO:])
        nn = zn * clg_ref[...] + clb_ref[...]
        y_ref[:, 2 * BW:3 * BW] = (nn * _sig(nn)).astype(BF)

        u, _ = _gelu(pc_ref[:, 6 * BW:7 * BW].astype(F32))
        v, _ = _gelu(pc_ref[:, 7 * BW:8 * BW].astype(F32))
        _, vn = _ln_stats(v)
        vn = (vn * slg_ref[...] + slb_ref[...]).astype(BF)
        mask = _sgu_mask()
        for hd in range(NBR):
            wm = jnp.where(mask, gw_ref[hd], 0.0).astype(BF)
            for blk in range(ts // GW):
                rows = slice(blk * GW, (blk + 1) * GW)
                cols = slice(hd * GW, (hd + 1) * GW)
                zz = jnp.dot(wm, vn[rows, cols], preferred_element_type=F32) + gb_ref[:, hd:hd + 1]
                y_ref[rows, 3 * BW + hd * GW:3 * BW + (hd + 1) * GW] = (u[rows, cols] * zz).astype(BF)

    return pl.pallas_call(
        body, name="mix_branches_fwd", grid=(s_len // ts,),
        in_specs=_halo_specs(ts, MIXC, s_len, False) + _branch_weights_specs(),
        out_specs=_rows(ts, NBR * BW),
        out_shape=jax.ShapeDtypeStruct((s_len, NBR * BW), BF),
        scratch_shapes=[pltpu.VMEM((ext, MIXC), BF)],
        compiler_params=_params(1),
    )(p, p, pool_w, pool_scale, sconv_w, cconv_w, cln_g, cln_b, sln_g, sln_b, sgu_w, sgu_bt)


def _mix_branches_bwd(p, dy, dp, pool_w, pool_scale, sconv_w, cconv_w, cln_g, cln_b, sln_g, sln_b, sgu_w, sgu_bt):
    s_len = p.shape[0]
    ts = min(BRANCH_TILE, s_len)
    ext = ts + 2 * HALO
    last = s_len // ts - 1
    tile = slice(HALO, HALO + ts)
    small_shapes = [(NBR, GW, GW), (1, BW), (SKP, BW), (CKP, BW), (1, BW), (1, BW), (1, BW), (1, BW), (NBR, GW, GW),
                    (NBR, GW, GW)]

    def body(pp_ref, pc_ref, pn_ref, dyp_ref, dyc_ref, dyn_ref, dpin_ref,
             pw_ref, ps_ref, sw_ref, cw_ref, clg_ref, clb_ref, slg_ref, slb_ref, gw_ref, gb_ref,
             dp_ref, dpw_ref, dps_ref, dsw_ref, dcw_ref, dclg_ref, dclb_ref, dslg_ref, dslb_ref, dgw_ref, dgb_ref,
             pe_ref, de_ref):
        del dyp_ref, dpin_ref
        i = pl.program_id(0)
        first = i == 0
        _assemble(pe_ref, pp_ref, pc_ref, pn_ref, i, last, ts)
        de_ref[0:HALO, :] = jnp.zeros((HALO, NBR * BW), BF)
        de_ref[HALO:HALO + ts, :] = dyc_ref[...]
        de_ref[HALO + ts:, :] = jnp.where(i < last, dyn_ref[...], jnp.zeros_like(dyn_ref))
        t = i * ts - HALO + lax.broadcasted_iota(jnp.int32, (ext, 1), 0)

        @pl.when(first)
        def _():
            dsw_ref[...] = jnp.zeros((SKP, BW), F32)
            dcw_ref[...] = jnp.zeros((CKP, BW), F32)

        dgs = _pool_diff(pe_ref[:, 0:BW].astype(F32), t, 1)
        dya = de_ref[:, 0:BW].astype(F32)
        dds = []
        for gi in range(NBR):
            cols = slice(gi * GW, (gi + 1) * GW)
            pw = pw_ref[gi].astype(BF)
            d_t = dgs[gi][tile].astype(BF)
            e = jnp.dot(d_t, pw, preferred_element_type=F32)
            _accumulate(dps_ref.at[:, cols], jnp.sum(dya[tile, cols] * e, axis=0, keepdims=True), first)
            de_g = (dya[:, cols] * ps_ref[:, cols]).astype(BF)
            _accumulate(dpw_ref.at[gi], lax.dot_general(d_t, de_g[tile], TN, preferred_element_type=F32), first)
            dds.append(lax.dot_general(de_g, pw, NT, preferred_element_type=F32))
        das = _pool_diff(jnp.concatenate(dds, axis=1), t, -1)
        for gi in range(NBR):
            dp_ref[:, gi * GW:(gi + 1) * GW] = das[gi][tile].astype(BF)

        xin = pe_ref[:, BW:2 * BW].astype(F32)
        bg = pe_ref[:, 2 * BW:3 * BW].astype(F32)
        cg = pe_ref[:, 3 * BW:4 * BW].astype(F32)
        q = cg * xin
        qs = [q, _shift(q, 1), _shift(q, 2)]
        cv = sw_ref[2:3, :] * qs[0] + sw_ref[1:2, :] * qs[1] + sw_ref[0:1, :] * qs[2]
        dyb = de_ref[:, BW:2 * BW].astype(F32)
        dcv = dyb * bg
        for j in range(SK):
            dsw_ref[SK - 1 - j:SK - j, :] += jnp.sum(dcv[tile] * qs[j][tile], axis=0, keepdims=True)
        dq = sw_ref[2:3, :] * dcv + sw_ref[1:2, :] * _shift(dcv, -1) + sw_ref[0:1, :] * _shift(dcv, -2)
        dp_ref[:, BW:2 * BW] = (dq * cg)[tile].astype(BF)
        dp_ref[:, 2 * BW:3 * BW] = (dyb * cv)[tile].astype(BF)
        dp_ref[:, 3 * BW:4 * BW] = (dq * xin)[tile].astype(BF)

        ca = pe_ref[:, 4 * BW:5 * BW].astype(F32)
        sb = _sig(pe_ref[:, 5 * BW:6 * BW].astype(F32))
        yg = ca * sb
        z = cw_ref[CK - 1:CK, :] * yg
        for j in range(1, CK):
            z = z + cw_ref[CK - 1 - j:CK - j, :] * _shift(yg, j)
        rs, zn = _ln_stats(z)
        nn = zn * clg_ref[...] + clb_ref[...]
        sn = _sig(nn)
        dn = de_ref[:, 2 * BW:3 * BW].astype(F32) * (sn * (1.0 + nn * (1.0 - sn)))
        _accumulate(dclg_ref, jnp.sum((dn * zn)[tile], axis=0, keepdims=True), first)
        _accumulate(dclb_ref, jnp.sum(dn[tile], axis=0, keepdims=True), first)
        dz = _ln_bwd(rs, zn, dn * clg_ref[...])
        dz = jnp.where(t >= i * ts, dz, 0.0)
        dz_t = dz[tile]
        dyg = cw_ref[CK - 1:CK, :] * dz
        dcw_ref[CK - 1:CK, :] += jnp.sum(dz_t * yg[tile], axis=0, keepdims=True)
        for j in range(1, CK):
            dyg = dyg + cw_ref[CK - 1 - j:CK - j, :] * _shift(dz, -j)
            dcw_ref[CK - 1 - j:CK - j, :] += jnp.sum(dz_t * _shift(yg, j)[tile], axis=0, keepdims=True)
        dp_ref[:, 4 * BW:5 * BW] = (dyg * sb)[tile].astype(BF)
        dp_ref[:, 5 * BW:6 * BW] = (dyg * ca * sb * (1.0 - sb))[tile].astype(BF)

        pu = pc_ref[:, 6 * BW:7 * BW].astype(F32)
        pv = pc_ref[:, 7 * BW:8 * BW].astype(F32)
        u, thu = _gelu(pu)
        v, thv = _gelu(pv)
        vrs, vn0 = _ln_stats(v)
        vn = (vn0 * slg_ref[...] + slb_ref[...]).astype(BF)
        dyd = dyc_ref[:, 3 * BW:4 * BW].astype(F32)
        dzz = dyd * u
        dzb = dzz.astype(BF)
        mask = _sgu_mask()
        dvn_cols = []
        for hd in range(NBR):
            cols = slice(hd * GW, (hd + 1) * GW)
            wm = jnp.where(mask, gw_ref[hd], 0.0).astype(BF)
            dwm = jnp.zeros((GW, GW), F32)
            dbs = jnp.zeros((GW, GW), F32)
            dvn_rows = []
            for blk in range(ts // GW):
                rows = slice(blk * GW, (blk + 1) * GW)
                zz = jnp.dot(wm, vn[rows, cols], preferred_element_type=F32) + gb_ref[:, hd:hd + 1]
                dp_ref[rows, 6 * BW + hd * GW:6 * BW + (hd + 1) * GW] = (
                    dyd[rows, cols] * zz * _gelu_grad(pu[rows, cols], thu[rows, cols])).astype(BF)
                dwm = dwm + lax.dot_general(dzb[rows, cols], vn[rows, cols], NT, preferred_element_type=F32)
                dbs = dbs + dzz[rows, cols]
                dvn_rows.append(lax.dot_general(wm, dzb[rows, cols], TN, preferred_element_type=F32))
            _accumulate(dgw_ref.at[hd], jnp.where(mask, dwm, 0.0), first)
            _accumulate(dgb_ref.at[hd], dbs, first)
            dvn_cols.append(jnp.concatenate(dvn_rows, axis=0))
        dvn = jnp.concatenate(dvn_cols, axis=1)
        _accumulate(dslg_ref, jnp.sum(dvn * vn0, axis=0, keepdims=True), first)
        _accumulate(dslb_ref, jnp.sum(dvn, axis=0, keepdims=True), first)
        dv = _ln_bwd(vrs, vn0, dvn * slg_ref[...])
        dp_ref[:, 7 * BW:8 * BW] = (dv * _gelu_grad(pv, thv)).astype(BF)

    const = lambda shp: pl.BlockSpec(shp, lambda i: (0,) * len(shp))
    outs = pl.pallas_call(
        body, name="mix_branches_bwd", grid=(s_len // ts,),
        in_specs=(_halo_specs(ts, MIXC, s_len, True) + _halo_specs(ts, NBR * BW, s_len, True)
                  + [pl.BlockSpec(memory_space=pl.ANY)] + _branch_weights_specs()),
        out_specs=[pl.BlockSpec((ts, MIXC), lambda i: (i, 0))] + [const(s) for s in small_shapes],
        out_shape=[jax.ShapeDtypeStruct((s_len, INC), BF)] + [jax.ShapeDtypeStruct(s, F32) for s in small_shapes],
        scratch_shapes=[pltpu.VMEM((ext, MIXC), BF), pltpu.VMEM((ext, NBR * BW), BF)],
        input_output_aliases={6: 0},
        compiler_params=_params(1),
    )(p, p, p, dy, dy, dy, dp, pool_w, pool_scale, sconv_w, cconv_w, cln_g, cln_b, sln_g, sln_b, sgu_w, sgu_bt)
    return outs


def _mix_out_fwd(x, y, p, w_up, w_out):
    s_len = x.shape[0]
    ts = min(TOKEN_TILE, s_len)

    def body(x_ref, y_ref, pg_ref, wu_ref, wo_ref, xo_ref, m_ref, up_ref):
        m = jnp.zeros((ts, D), F32)
        for gi in range(NBR):
            up = jnp.dot(y_ref[:, gi * BW:(gi + 1) * BW], wu_ref[gi], preferred_element_type=F32)
            up_ref[:, gi * D:(gi + 1) * D] = up.astype(BF)
            m = m + _sig(pg_ref[:, gi * D:(gi + 1) * D].astype(F32)) * up
        mb = m.astype(BF)
        m_ref[...] = mb
        xo_ref[...] = x_ref[...] + jnp.dot(mb, wo_ref[...], preferred_element_type=F32)

    return pl.pallas_call(
        body, name="mix_out_fwd", grid=(s_len // ts,),
        in_specs=[_rows(ts, D), _rows(ts, NBR * BW), _rows(ts, NBR * D, 1), _full((NBR, BW, D)), _full((D, D))],
        out_specs=[_rows(ts, D), _rows(ts, D), _rows(ts, NBR * D)],
        out_shape=[jax.ShapeDtypeStruct((s_len, D), F32), jax.ShapeDtypeStruct((s_len, D), BF),
                   jax.ShapeDtypeStruct((s_len, NBR * D), BF)],
        compiler_params=_params(1),
    )(x, y, p, w_up, w_out)


def _mix_out_bwd(dxo, up, p, w_up, w_out):
    s_len = dxo.shape[0]
    ts = min(TOKEN_TILE, s_len)

    def body(dxo_ref, up_ref, pg_ref, wu_ref, wo_ref, dy_ref, dp_ref, dup_ref, dxb_ref):
        dxb = dxo_ref[...].astype(BF)
        dxb_ref[...] = dxb
        dm = lax.dot_general(dxb, wo_ref[...], NT, preferred_element_type=F32)
        for gi in range(NBR):
            cols = slice(gi * D, (gi + 1) * D)
            gate = _sig(pg_ref[:, cols].astype(F32))
            dp_ref[:, cols] = (dm * up_ref[:, cols].astype(F32) * gate * (1.0 - gate)).astype(BF)
            dup = (dm * gate).astype(BF)
            dup_ref[:, cols] = dup
            dy_ref[:, gi * BW:(gi + 1) * BW] = lax.dot_general(
                dup, wu_ref[gi], NT, preferred_element_type=F32).astype(BF)

    return pl.pallas_call(
        body, name="mix_out_bwd", grid=(s_len // ts,),
        in_specs=[_rows(ts, D), _rows(ts, NBR * D), _rows(ts, NBR * D, 1), _full((NBR, BW, D)), _full((D, D))],
        out_specs=[_rows(ts, NBR * BW), _rows(ts, NBR * D, 1), _rows(ts, NBR * D), _rows(ts, D)],
        out_shape=[jax.ShapeDtypeStruct((s_len, NBR * BW), BF), jax.ShapeDtypeStruct((s_len, INC), BF),
                   jax.ShapeDtypeStruct((s_len, NBR * D), BF), jax.ShapeDtypeStruct((s_len, D), BF)],
        compiler_params=_params(1),
    )(dxo, up, p, w_up, w_out)


def _mix_in_bwd(dp, w_in, x, g, dxo):
    s_len = x.shape[0]
    ts = min(TOKEN_TILE, s_len)

    def body(dp_ref, w_ref, x_ref, g_ref, dxo_ref, dxi_ref, dg_ref):
        i = pl.program_id(0)
        dh = lax.dot_general(dp_ref[...], w_ref[...], NT, preferred_element_type=F32)
        dx, dg = _rms_bwd(x_ref[...], g_ref[...], dh)
        dxi_ref[...] = dxo_ref[...] + dx
        _accumulate(dg_ref, dg, i == 0)

    return pl.pallas_call(
        body, name="mix_in_bwd", grid=(s_len // ts,),
        in_specs=[_rows(ts, INC), _full((D, INC)), _rows(ts, D), _full((1, D)), _rows(ts, D)],
        out_specs=[_rows(ts, D), pl.BlockSpec((1, D), lambda i: (0, 0))],
        out_shape=[jax.ShapeDtypeStruct((s_len, D), F32), jax.ShapeDtypeStruct((1, D), F32)],
        compiler_params=_params(1),
    )(dp, w_in, x, g, dxo)


def _loss_head(x, g, target):
    s_len = x.shape[0]
    ts = min(512, s_len)

    def body(x_ref, g_ref, t_ref, dx_ref, dg_ref, loss_ref):
        i = pl.program_id(0)
        xv = x_ref[...]
        _, xh = _rms_stats(xv)
        err = xh * g_ref[...] - t_ref[...]
        part = 0.5 * jnp.sum(jnp.mean(err * err, axis=-1, keepdims=True), axis=0, keepdims=True)
        dx, dg = _rms_bwd(xv, g_ref[...], err * (1.0 / D))
        dx_ref[...] = dx
        _accumulate(dg_ref, dg, i == 0)
        _accumulate(loss_ref, jnp.broadcast_to(part, (1, GW)), i == 0)

    return pl.pallas_call(
        body, name="loss_head", grid=(s_len // ts,),
        in_specs=[_rows(ts, D), _full((1, D)), _rows(ts, D)],
        out_specs=[_rows(ts, D), pl.BlockSpec((1, D), lambda i: (0, 0)), pl.BlockSpec((1, GW), lambda i: (0, 0))],
        out_shape=[jax.ShapeDtypeStruct((s_len, D), F32), jax.ShapeDtypeStruct((1, D), F32),
                   jax.ShapeDtypeStruct((1, GW), F32)],
        compiler_params=_params(1),
    )(x, g, target)


def _row_tile(rows, cols, budget=1 << 18):
    tr = rows
    while tr * cols > budget and tr % 16 == 0:
        tr //= 2
    return tr


def _elementwise(fn, name, ins, out_dtypes):
    rows, cols = ins[0].shape
    tr = _row_tile(rows, cols)
    n_in = len(ins)

    def body(*refs):
        res = fn(*[r[...] for r in refs[:n_in]])
        for o_ref, val in zip(refs[n_in:], res):
            o_ref[...] = val.astype(o_ref.dtype)

    outs = pl.pallas_call(
        body, name=name, grid=(rows // tr,),
        in_specs=[_rows(tr, cols)] * n_in, out_specs=[_rows(tr, cols)] * len(out_dtypes),
        out_shape=[jax.ShapeDtypeStruct((rows, cols), dt) for dt in out_dtypes],
        compiler_params=_params(1),
    )(*ins)
    return outs


def _to_bf16(w):
    shape = w.shape
    return _elementwise(lambda v: (v,), "to_bf16", [w.reshape(-1, shape[-1])], [BF])[0].reshape(shape)


def _adamw_math(w, g, m, v):
    m = ADAM_B1 * m + (1.0 - ADAM_B1) * g
    v = ADAM_B2 * v + (1.0 - ADAM_B2) * (g * g)
    m_hat = m / (1.0 - ADAM_B1 ** ADAM_STEP)
    v_hat = v / (1.0 - ADAM_B2 ** ADAM_STEP)
    delta = -ADAM_LR * (m_hat / (jnp.sqrt(v_hat) + ADAM_EPS) + ADAM_WD * w)
    return delta, m, v


def _adamw(w, g, m, v):
    shape = w.shape
    two_d = lambda a: a.reshape(-1, shape[-1])
    outs = _elementwise(_adamw_math, "adamw", [two_d(w), two_d(g), two_d(m), two_d(v)], [F32, F32, F32])
    return [o.reshape(shape) for o in outs]


ANY = pl.BlockSpec(memory_space=pl.ANY)


def _place():
    x, y, c = lax.axis_index("x"), lax.axis_index("y"), lax.axis_index("c")
    chips = [(1 - x, y), (x, 1 - y), (1 - x, 1 - y)]
    return x, y, c, chips


def _cols(ref, start, size):
    idx = (slice(None),) * (len(ref.shape) - 1) + (pl.ds(pl.multiple_of(start, GW), size),)
    return ref.at[idx]


def _rows_of(ref, start, size):
    nd = len(ref.shape)
    idx = (slice(None),) * (nd - 2) + (pl.ds(pl.multiple_of(start, 16), size), slice(None))
    return ref.at[idx]


def _gather_weights(shards, col_sharded):
    n = len(shards)

    def full_shape(s, col):
        return s.shape[:-1] + (s.shape[-1] * NCHIP,) if col else s.shape[:-2] + (s.shape[-2] * NCHIP, s.shape[-1])

    def body(*refs):
        ins, outs = refs[:n], refs[n:2 * n]
        send_sems, recv_sems, local_sems = refs[2 * n:]
        x, y, c, chips = _place()
        sibling = (x, y, 1 - c)

        def place(k, layer, chip):
            width = ins[k].shape[-1] if col_sharded[k] else ins[k].shape[-2]
            start = (2 * chip[0] + chip[1]) * width
            whole = outs[k] if layer is None else outs[k].at[layer]
            return _cols(whole, start, width) if col_sharded[k] else _rows_of(whole, start, width)

        def copy(sem, k, layer, chip, to, src=None):
            dst = place(k, layer, chip)
            return pltpu.make_async_remote_copy(
                src_ref=dst if src is None else src, dst_ref=dst, send_sem=send_sems.at[sem], recv_sem=recv_sems.at[sem],
                device_id=to, device_id_type=MESH)

        mine = [pltpu.make_async_copy(ins[k], place(k, None, (x, y)), local_sems.at[k]) for k in range(n)]
        for cp in mine:
            cp.start()
        first = [copy(3 * k + j, k, c, (x, y), (*chip, c), src=ins[k].at[c]) for k in range(n) for j, chip in enumerate(chips)]
        for cp in first:
            cp.start()
        passed = []
        for j, chip in enumerate(chips):
            for k in range(n):
                copy(3 * k + j, k, c, chip, (x, y, c)).wait_recv()
                fwd = copy(3 * n + 3 * k + j, k, c, chip, sibling)
                fwd.start()
                passed.append(fwd)
        for j, chip in enumerate(chips):
            for k in range(n):
                copy(3 * n + 3 * k + j, k, 1 - c, chip, (x, y, c)).wait_recv()
        for cp in first + passed:
            cp.wait_send()
        for cp in mine:
            cp.wait()

    return pl.pallas_call(
        body, name="gather_weights",
        in_specs=[ANY] * n, out_specs=[ANY] * n,
        out_shape=[jax.ShapeDtypeStruct(full_shape(s, col), s.dtype) for s, col in zip(shards, col_sharded)],
        scratch_shapes=[pltpu.SemaphoreType.DMA((6 * n,)), pltpu.SemaphoreType.DMA((6 * n,)),
                        pltpu.SemaphoreType.DMA((n,))],
    )(*shards)


def _half(ref, col_sharded, c):
    k, n = ref.shape[-2:]
    return _rows_of(ref, c * (k // 2), k // 2) if col_sharded else _cols(ref, c * (n // 2), n // 2)


def _quarter(ref, col_sharded, j):
    k, n = ref.shape[-2:]
    return _cols(ref, j * (n // NCHIP), n // NCHIP) if col_sharded else _rows_of(ref, j * (k // NCHIP), k // NCHIP)


def _pair_exchange(grads, col_sharded):
    n = len(grads)

    def half_shape(g, col):
        return (g.shape[0] // 2, g.shape[1]) if col else (g.shape[0], g.shape[1] // 2)

    def body(*refs):
        ins, own, got = refs[:n], refs[n:2 * n], refs[2 * n:3 * n]
        send_sems, recv_sems, local_sems = refs[3 * n:]
        x, y, c, _ = _place()
        local = [pltpu.make_async_copy(_half(ins[k], col_sharded[k], c), own[k], local_sems.at[k]) for k in range(n)]
        sends = [pltpu.make_async_remote_copy(
            src_ref=_half(ins[k], col_sharded[k], 1 - c), dst_ref=got[k], send_sem=send_sems.at[k],
            recv_sem=recv_sems.at[k], device_id=(x, y, 1 - c), device_id_type=MESH) for k in range(n)]
        for cp in local + sends:
            cp.start()
        for cp in sends:
            cp.wait()
        for cp in local:
            cp.wait()

    halves = [jax.ShapeDtypeStruct(half_shape(g, col), g.dtype) for g, col in zip(grads, col_sharded)]
    outs = pl.pallas_call(
        body, name="grad_pair_exchange", in_specs=[ANY] * n, out_specs=[ANY] * (2 * n), out_shape=halves + halves,
        scratch_shapes=[pltpu.SemaphoreType.DMA((n,)), pltpu.SemaphoreType.DMA((n,)), pltpu.SemaphoreType.DMA((n,))],
    )(*grads)
    return outs[:n], outs[n:]


def _chip_exchange(halves, col_sharded):
    n = len(halves)

    def quarter_shape(h, col):
        return (h.shape[0], h.shape[1] // NCHIP) if col else (h.shape[0] // NCHIP, h.shape[1])

    def body(*refs):
        ins, own, got = refs[:n], refs[n:2 * n], refs[2 * n:3 * n]
        send_sems, recv_sems, local_sems = refs[3 * n:]
        x, y, c, chips = _place()
        local = [pltpu.make_async_copy(_quarter(ins[k], col_sharded[k], 2 * x + y), own[k], local_sems.at[k])
                 for k in range(n)]
        sends = [pltpu.make_async_remote_copy(
            src_ref=_quarter(ins[k], col_sharded[k], 2 * chip[0] + chip[1]), dst_ref=got[k].at[j],
            send_sem=send_sems.at[3 * k + j], recv_sem=recv_sems.at[3 * k + j], device_id=(*chip, c), device_id_type=MESH)
            for k in range(n) for j, chip in enumerate(chips)]
        for cp in local + sends:
            cp.start()
        for cp in sends:
            cp.wait()
        for cp in local:
            cp.wait()

    quarters = [jax.ShapeDtypeStruct(quarter_shape(h, col), h.dtype) for h, col in zip(halves, col_sharded)]
    threes = [jax.ShapeDtypeStruct((3,) + q.shape, q.dtype) for q in quarters]
    outs = pl.pallas_call(
        body, name="grad_chip_exchange", in_specs=[ANY] * n, out_specs=[ANY] * (2 * n), out_shape=quarters + threes,
        scratch_shapes=[pltpu.SemaphoreType.DMA((3 * n,)), pltpu.SemaphoreType.DMA((3 * n,)),
                        pltpu.SemaphoreType.DMA((n,))],
    )(*halves)
    return outs[:n], outs[n:]


def _sibling_exchange(parts, col_sharded):
    n = len(parts)

    def shard_shape(q, col):
        return (2, q.shape[0] * 2, q.shape[1]) if col else (2, q.shape[0], q.shape[1] * 2)

    def body(*refs):
        ins, outs = refs[:n], refs[n:n + n // 2]
        send_sems, recv_sems, local_sems = refs[n + n // 2:]
        x, y, c, _ = _place()
        local, sends = [], []
        for i in range(n):
            dst = _half(outs[i // 2].at[i % 2], col_sharded[i // 2], c)
            local.append(pltpu.make_async_copy(ins[i], dst, local_sems.at[i]))
            sends.append(pltpu.make_async_remote_copy(
                src_ref=ins[i], dst_ref=dst, send_sem=send_sems.at[i], recv_sem=recv_sems.at[i],
                device_id=(x, y, 1 - c), device_id_type=MESH))
        for cp in local + sends:
            cp.start()
        for i, cp in enumerate(sends):
            cp.wait_send()
            pltpu.make_async_remote_copy(
                src_ref=ins[i], dst_ref=_half(outs[i // 2].at[i % 2], col_sharded[i // 2], 1 - c),
                send_sem=send_sems.at[i], recv_sem=recv_sems.at[i], device_id=(x, y, 1 - c), device_id_type=MESH).wait_recv()
        for cp in local:
            cp.wait()

    return pl.pallas_call(
        body, name="grad_sibling_exchange", in_specs=[ANY] * n, out_specs=[ANY] * (n // 2),
        out_shape=[jax.ShapeDtypeStruct(shard_shape(parts[2 * k], col_sharded[k]), F32) for k in range(n // 2)],
        scratch_shapes=[pltpu.SemaphoreType.DMA((n,)), pltpu.SemaphoreType.DMA((n,)), pltpu.SemaphoreType.DMA((n,))],
    )(*parts)


def _all_reduce_small(buf):
    rows = buf.shape[0]
    per = rows // NDEV
    flips = [(fx, fy, fc) for fx in (0, 1) for fy in (0, 1) for fc in (0, 1)][1:]

    def body(in_ref, out_ref, got_ref, send_sems, recv_sems):
        x, y, c, _ = _place()
        me = 4 * x + 2 * y + c

        def peer(f):
            return tuple(1 - pos if flip else pos for pos, flip in zip((x, y, c), f))

        def block(ref, dev):
            return ref.at[pl.ds(pl.multiple_of(dev * per, 8), per), :]

        scatter = []
        for k, f in enumerate(flips):
            px, py, pc = peer(f)
            scatter.append(pltpu.make_async_remote_copy(
                src_ref=block(in_ref, 4 * px + 2 * py + pc), dst_ref=got_ref.at[k], send_sem=send_sems.at[k],
                recv_sem=recv_sems.at[k], device_id=(px, py, pc), device_id_type=MESH))
        for cp in scatter:
            cp.start()
        for cp in scatter:
            cp.wait()
        total = block(in_ref, me)[...]
        for k in range(len(flips)):
            total = total + got_ref[k]
        block(out_ref, me)[...] = total
        share = []
        for k, f in enumerate(flips):
            share.append(pltpu.make_async_remote_copy(
                src_ref=block(out_ref, me), dst_ref=block(out_ref, me), send_sem=send_sems.at[7 + k],
                recv_sem=recv_sems.at[7 + k], device_id=peer(f), device_id_type=MESH))
        for cp in share:
            cp.start()
        for k, f in enumerate(flips):
            share[k].wait_send()
            px, py, pc = peer(f)
            theirs = block(out_ref, 4 * px + 2 * py + pc)
            pltpu.make_async_remote_copy(
                src_ref=theirs, dst_ref=theirs, send_sem=send_sems.at[7 + k], recv_sem=recv_sems.at[7 + k],
                device_id=(px, py, pc), device_id_type=MESH).wait_recv()

    vmem = pl.BlockSpec(memory_space=pltpu.VMEM)
    return pl.pallas_call(
        body, name="all_reduce_small", in_specs=[vmem], out_specs=vmem,
        out_shape=jax.ShapeDtypeStruct((rows, GW), F32),
        scratch_shapes=[pltpu.VMEM((NDEV - 1, per, GW), F32), pltpu.SemaphoreType.DMA((14,)),
                        pltpu.SemaphoreType.DMA((14,))],
    )(buf)


BIG = ("ffn1_w13", "ffn1_w2", "w_in", "w_up", "w_out", "ffn2_w13", "ffn2_w2")
BIG_COL_SHARDED = (True, False, True, True, False, True, False)
SMALL = ("ffn1_norm", "mix_norm", "pool_w", "pool_scale", "sconv_w", "cconv_w", "cconv_ln_g", "cconv_ln_b",
         "sgu_ln_g", "sgu_ln_b", "sgu_w", "sgu_b", "ffn2_norm", "final_norm")
WEIGHTS = ("ffn1_norm", "ffn1_w13", "ffn1_w2", "mix_norm", "w_in", "pool_w", "pool_scale", "sconv_w", "cconv_w",
           "cconv_ln_g", "cconv_ln_b", "sgu_ln_g", "sgu_ln_b", "sgu_w", "sgu_b", "w_up", "w_out", "ffn2_norm",
           "ffn2_w13", "ffn2_w2", "final_norm")


def _pad_rows(a, rows):
    return jnp.pad(a, ((0, 0), (0, rows - a.shape[1]), (0, 0)))


def kernel(x, ffn1_norm, ffn1_w13, ffn1_w2, mix_norm, w_in, pool_w, pool_scale, sconv_w, cconv_w, cconv_ln_g, cconv_ln_b, sgu_ln_g, sgu_ln_b, sgu_w, sgu_b, w_up, w_out, ffn2_norm, ffn2_w13, ffn2_w2, final_norm, loss_target, m_ffn1_norm, m_ffn1_w13, m_ffn1_w2, m_mix_norm, m_w_in, m_pool_w, m_pool_scale, m_sconv_w, m_cconv_w, m_cconv_ln_g, m_cconv_ln_b, m_sgu_ln_g, m_sgu_ln_b, m_sgu_w, m_sgu_b, m_w_up, m_w_out, m_ffn2_norm, m_ffn2_w13, m_ffn2_w2, m_final_norm, v_ffn1_norm, v_ffn1_w13, v_ffn1_w2, v_mix_norm, v_w_in, v_pool_w, v_pool_scale, v_sconv_w, v_cconv_w, v_cconv_ln_g, v_cconv_ln_b, v_sgu_ln_g, v_sgu_ln_b, v_sgu_w, v_sgu_b, v_w_up, v_w_out, v_ffn2_norm, v_ffn2_w13, v_ffn2_w2, v_final_norm):
    args = dict(locals())
    w = {nm: args[nm] for nm in WEIGHTS}
    m = {nm: args["m_" + nm] for nm in WEIGHTS}
    v = {nm: args["v_" + nm] for nm in WEIGHTS}
    depth = ffn1_w13.shape[0]
    chip = 2 * lax.axis_index("x") + lax.axis_index("y")

    w_up_2d = w_up.reshape(depth, NBR * BW, w_up.shape[-1])
    shards = [_to_bf16(ffn1_w13), _to_bf16(ffn1_w2), _to_bf16(w_in), _to_bf16(w_up_2d), _to_bf16(w_out),
              _to_bf16(ffn2_w13), _to_bf16(ffn2_w2), _pad_rows(sconv_w, SKP), _pad_rows(cconv_w, CKP)]
    full = _gather_weights(shards, list(BIG_COL_SHARDED) + [True, True])
    fw = dict(zip(BIG, full[:7]))
    sconv_full, cconv_full = full[7], full[8]

    xs = x[0]
    row = lambda a: a.reshape(1, -1)
    saved = []
    for l in range(depth):
        lw = dict(
            g1=row(ffn1_norm[l]), gm=row(mix_norm[l]), g2=row(ffn2_norm[l]),
            w13a=fw["ffn1_w13"][l], w2a=fw["ffn1_w2"][l], w13b=fw["ffn2_w13"][l], w2b=fw["ffn2_w2"][l],
            w_in=fw["w_in"][l], w_up=fw["w_up"][l].reshape(NBR, BW, D), w_out=fw["w_out"][l],
            branch=(pool_w[l], row(pool_scale[l]), sconv_full[l], cconv_full[l], row(cconv_ln_g[l]), row(cconv_ln_b[l]),
                    row(sgu_ln_g[l]), row(sgu_ln_b[l]), sgu_w[l], sgu_b[l].T))
        x1, h1, ab1 = _ffn_fwd(xs, lw["g1"], lw["w13a"], lw["w2a"])
        hm, p = _mix_in(x1, lw["gm"], lw["w_in"])
        y = _mix_branches_fwd(p, *lw["branch"])
        x2, merged, up = _mix_out_fwd(x1, y, p, lw["w_up"], lw["w_out"])
        x3, h2, ab2 = _ffn_fwd(x2, lw["g2"], lw["w13b"], lw["w2b"])
        saved.append(dict(lw=lw, x0=xs, x1=x1, x2=x2, h1=h1, ab1=ab1, hm=hm, p=p, y=y, merged=merged, up=up, h2=h2,
                          ab2=ab2))
        xs = x3

    dx, d_final, loss_part = _loss_head(xs, row(final_norm), loss_target[0])
    loss = lax.psum(loss_part[0, 0], ("x", "y", "c"))

    big_parts = {nm: [None] * depth for nm in BIG}
    small_parts = {nm: [None] * depth for nm in SMALL if nm != "final_norm"}
    for l in reversed(range(depth)):
        sv = saved[l]
        lw = sv["lw"]
        dx, dab, s_act, dyh, dg2 = _ffn_bwd(dx, sv["x2"], lw["g2"], sv["ab2"], lw["w13b"], lw["w2b"])
        big_parts["ffn2_w13"][l] = _wgrad(sv["h2"], dab, D, 512, "wgrad_w13")
        big_parts["ffn2_w2"][l] = _wgrad(s_act, dyh, 256, D, "wgrad_w2")
        small_parts["ffn2_norm"][l] = dg2

        dy, dp, dup, dxb = _mix_out_bwd(dx, sv["up"], sv["p"], lw["w_up"], lw["w_out"])
        big_parts["w_out"][l] = _wgrad(sv["merged"], dxb, D, 512, "wgrad_w_out")
        big_parts["w_up"][l] = _wgrad_groups(sv["y"], dup, BW, D, "wgrad_w_up")
        (dp, d_pool_w, d_pool_scale, d_sconv, d_cconv, d_clg, d_clb, d_slg, d_slb, d_sgu_w, d_sgu_b) = _mix_branches_bwd(
            sv["p"], dy, dp, *lw["branch"])
        big_parts["w_in"][l] = _wgrad(sv["hm"], dp, D, 512, "wgrad_w_in")
        dx, dgm = _mix_in_bwd(dp, lw["w_in"], sv["x1"], lw["gm"], dx)
        small_parts["mix_norm"][l] = dgm
        small_parts["pool_w"][l] = d_pool_w
        small_parts["pool_scale"][l] = d_pool_scale
        small_parts["sconv_w"][l] = d_sconv[:SK]
        small_parts["cconv_w"][l] = d_cconv[:CK]
        small_parts["cconv_ln_g"][l] = d_clg
        small_parts["cconv_ln_b"][l] = d_clb
        small_parts["sgu_ln_g"][l] = d_slg
        small_parts["sgu_ln_b"][l] = d_slb
        small_parts["sgu_w"][l] = d_sgu_w
        small_parts["sgu_b"][l] = jnp.sum(d_sgu_b, axis=-1)

        dx, dab, s_act, dyh, dg1 = _ffn_bwd(dx, sv["x0"], lw["g1"], sv["ab1"], lw["w13a"], lw["w2a"])
        big_parts["ffn1_w13"][l] = _wgrad(sv["h1"], dab, D, 512, "wgrad_w13")
        big_parts["ffn1_w2"][l] = _wgrad(s_act, dyh, 256, D, "wgrad_w2")
        small_parts["ffn1_norm"][l] = dg1
    grad_x = dx[None]

    flat = [big_parts[nm][l] for nm in BIG for l in range(depth)]
    cols = [col for col in BIG_COL_SHARDED for _ in range(depth)]
    own_half, sib_half = _pair_exchange(flat, cols)
    pair_sum = [_elementwise(lambda a, b: (a.astype(F32) + b.astype(F32),), "pair_sum", [a, b], [BF])[0]
                for a, b in zip(own_half, sib_half)]
    own_q, got_q = _chip_exchange(pair_sum, cols)
    reduced = [_elementwise(lambda a, b, c_, d_: (a.astype(F32) + b.astype(F32) + c_.astype(F32) + d_.astype(F32),),
                            "chip_sum", [a, b3[0], b3[1], b3[2]], [F32])[0] for a, b3 in zip(own_q, got_q)]
    big_grads = dict(zip(BIG, _sibling_exchange(reduced, list(BIG_COL_SHARDED))))
    big_grads["w_up"] = big_grads["w_up"].reshape(w_up.shape)

    small_local = {nm: jnp.stack(parts).reshape(depth, *w[nm].shape[1:-1], -1) if nm not in ("sconv_w", "cconv_w")
                   else jnp.stack(parts) for nm, parts in small_parts.items()}
    small_local["final_norm"] = d_final.reshape(-1)
    sizes = [small_local[nm].size for nm in SMALL]
    total = sum(sizes)
    pad_to = NDEV * 8 * GW
    padded = -(-total // pad_to) * pad_to
    packed = jnp.concatenate([small_local[nm].reshape(-1) for nm in SMALL] + [jnp.zeros((padded - total,), F32)])
    summed = _all_reduce_small(packed.reshape(-1, GW)).reshape(-1)
    small_grads, off = {}, 0
    for nm, size in zip(SMALL, sizes):
        small_grads[nm] = summed[off:off + size].reshape(small_local[nm].shape)
        off += size
    for nm in ("sconv_w", "cconv_w"):
        small_grads[nm] = lax.dynamic_slice_in_dim(small_grads[nm], chip * GW, GW, axis=2)

    grads = {**big_grads, **small_grads}

    delta, new_m, new_v = {}, {}, {}
    for nm in BIG:
        delta[nm], new_m[nm], new_v[nm] = _adamw(w[nm], grads[nm], m[nm], v[nm])
    s_sizes = [w[nm].size for nm in SMALL]
    s_total = sum(s_sizes)
    s_padded = -(-s_total // (8 * GW)) * (8 * GW)

    def pack(tree):
        return jnp.concatenate([tree[nm].reshape(-1) for nm in SMALL] + [jnp.ones((s_padded - s_total,), F32)]).reshape(-1, GW)

    packed_out = _adamw(pack(w), pack(grads), pack(m), pack(v))
    off = 0
    for nm, size in zip(SMALL, s_sizes):
        for tree, arr in zip((delta, new_m, new_v), packed_out):
            tree[nm] = arr.reshape(-1)[off:off + size].reshape(w[nm].shape)
        off += size

    return (loss, grad_x, *[grads[nm] for nm in WEIGHTS], *[delta[nm] for nm in WEIGHTS],
            *[new_m[nm] for nm in WEIGHTS], *[new_v[nm] for nm in WEIGHTS])
```

```python
import jax
import jax.numpy as jnp
from jax import lax
from jax.experimental import pallas as pl
from jax.experimental.pallas import tpu as pltpu

D = 1024
FF = 2816
BW = 512
NBR = 4
MIXC = 4096
INC = 8192
GW = 128
CHUNK = 64
SK = 3
CK = 31
SKP = 8
CKP = 32
HALO = 32
TOKEN_TILE = 256
BRANCH_TILE = 512
EPS = 1e-6
NCHIP = 4
NDEV = 8

ADAM_LR = 0.001
ADAM_B1 = 0.9
ADAM_B2 = 0.999
ADAM_EPS = 1e-08
ADAM_WD = 0.01
ADAM_STEP = 10

VMEM_LIMIT = 56 * 1024 * 1024

BF = jnp.bfloat16
F32 = jnp.float32
MESH = pl.DeviceIdType.MESH
NT = (((1,), (1,)), ((), ()))
TN = (((0,), (0,)), ((), ()))


def _params(n_axes):
    return pltpu.CompilerParams(dimension_semantics=("arbitrary",) * n_axes, vmem_limit_bytes=VMEM_LIMIT)


def _full(shape):
    nd = len(shape)
    return pl.BlockSpec(shape, lambda *_: (0,) * nd, pipeline_mode=pl.Buffered(1))


def _rows(ts, width, col=0):
    return pl.BlockSpec((ts, width), lambda i: (i, col))


def _sig(v):
    return jax.nn.sigmoid(v)


def _rms_stats(x):
    r = lax.rsqrt(jnp.mean(x * x, axis=-1, keepdims=True) + EPS)
    return r, x * r


def _rms_bwd(x, g, dh):
    r, xh = _rms_stats(x)
    dg = jnp.sum(dh * xh, axis=0, keepdims=True)
    dxh = dh * g
    dx = r * (dxh - xh * jnp.mean(dxh * xh, axis=-1, keepdims=True))
    return dx, dg


def _accumulate(ref, val, first):
    @pl.when(first)
    def _():
        ref[...] = val

    @pl.when(jnp.logical_not(first))
    def _():
        ref[...] += val


def _ffn_fwd(x, g, w13, w2):
    s_len = x.shape[0]
    ts = min(TOKEN_TILE, s_len)

    def body(x_ref, g_ref, w13_ref, w2_ref, xo_ref, h_ref, ab_ref):
        xv = x_ref[...]
        r, xh = _rms_stats(xv)
        h = (xh * g_ref[...]).astype(BF)
        h_ref[...] = h
        ab = jnp.dot(h, w13_ref[...], preferred_element_type=F32)
        ab_ref[...] = ab.astype(BF)
        a = ab[:, :FF]
        b = ab[:, FF:]
        s = (a * _sig(a) * b).astype(BF)
        xo_ref[...] = xv + 0.5 * jnp.dot(s, w2_ref[...], preferred_element_type=F32)

    return pl.pallas_call(
        body, name="ffn_fwd", grid=(s_len // ts,),
        in_specs=[_rows(ts, D), _full((1, D)), _full((D, 2 * FF)), _full((FF, D))],
        out_specs=[_rows(ts, D), _rows(ts, D), _rows(ts, 2 * FF)],
        out_shape=[jax.ShapeDtypeStruct((s_len, D), F32), jax.ShapeDtypeStruct((s_len, D), BF),
                   jax.ShapeDtypeStruct((s_len, 2 * FF), BF)],
        compiler_params=_params(1),
    )(x, g, w13, w2)


def _ffn_bwd(dxo, x, g, ab, w13, w2):
    s_len = x.shape[0]
    ts = min(TOKEN_TILE, s_len)

    def body(dxo_ref, x_ref, g_ref, ab_ref, w13_ref, w2_ref, dxi_ref, dab_ref, s_ref, dy_ref, dg_ref):
        i = pl.program_id(0)
        dxo_v = dxo_ref[...]
        dy = (0.5 * dxo_v).astype(BF)
        dy_ref[...] = dy
        ds = lax.dot_general(dy, w2_ref[...], NT, preferred_element_type=F32)
        a = ab_ref[:, :FF].astype(F32)
        b = ab_ref[:, FF:].astype(F32)
        sg = _sig(a)
        sil = a * sg
        s_ref[...] = (sil * b).astype(BF)
        dab_ref[:, :FF] = (ds * b * (sg * (1.0 + a * (1.0 - sg)))).astype(BF)
        dab_ref[:, FF:] = (ds * sil).astype(BF)
        dh = lax.dot_general(dab_ref[...], w13_ref[...], NT, preferred_element_type=F32)
        dx, dg = _rms_bwd(x_ref[...], g_ref[...], dh)
        dxi_ref[...] = dxo_v + dx
        _accumulate(dg_ref, dg, i == 0)

    return pl.pallas_call(
        body, name="ffn_bwd", grid=(s_len // ts,),
        in_specs=[_rows(ts, D), _rows(ts, D), _full((1, D)), _rows(ts, 2 * FF), _full((D, 2 * FF)), _full((FF, D))],
        out_specs=[_rows(ts, D), _rows(ts, 2 * FF), _rows(ts, FF), _rows(ts, D), pl.BlockSpec((1, D), lambda i: (0, 0))],
        out_shape=[jax.ShapeDtypeStruct((s_len, D), F32), jax.ShapeDtypeStruct((s_len, 2 * FF), BF),
                   jax.ShapeDtypeStruct((s_len, FF), BF), jax.ShapeDtypeStruct((s_len, D), BF),
                   jax.ShapeDtypeStruct((1, D), F32)],
        compiler_params=_params(1),
    )(dxo, x, g, ab, w13, w2)


def _wgrad(a, b, tk, tn, name):
    s_len, k = a.shape
    n = b.shape[1]

    def body(a_ref, b_ref, o_ref):
        o_ref[...] = lax.dot_general(a_ref[...], b_ref[...], TN, preferred_element_type=F32).astype(BF)

    return pl.pallas_call(
        body, name=name, grid=(k // tk, n // tn),
        in_specs=[pl.BlockSpec((s_len, tk), lambda i, j: (0, i)), pl.BlockSpec((s_len, tn), lambda i, j: (0, j))],
        out_specs=pl.BlockSpec((tk, tn), lambda i, j: (i, j)),
        out_shape=jax.ShapeDtypeStruct((k, n), BF),
        compiler_params=_params(2),
    )(a, b)


def _wgrad_groups(a, b, ka, nb, name):
    s_len = a.shape[0]
    groups = a.shape[1] // ka

    def body(a_ref, b_ref, o_ref):
        o_ref[...] = lax.dot_general(a_ref[...], b_ref[...], TN, preferred_element_type=F32).astype(BF)

    return pl.pallas_call(
        body, name=name, grid=(groups,),
        in_specs=[pl.BlockSpec((s_len, ka), lambda gi: (0, gi)), pl.BlockSpec((s_len, nb), lambda gi: (0, gi))],
        out_specs=pl.BlockSpec((ka, nb), lambda gi: (gi, 0)),
        out_shape=jax.ShapeDtypeStruct((groups * ka, nb), BF),
        compiler_params=_params(1),
    )(a, b)


def _mix_in(x, g, w_in):
    s_len = x.shape[0]
    ts = min(TOKEN_TILE, s_len)

    def body(x_ref, g_ref, w_ref, h_ref, p_ref):
        _, xh = _rms_stats(x_ref[...])
        h = (xh * g_ref[...]).astype(BF)
        h_ref[...] = h
        p_ref[...] = jnp.dot(h, w_ref[...], preferred_element_type=F32).astype(BF)

    return pl.pallas_call(
        body, name="mix_in", grid=(s_len // ts,),
        in_specs=[_rows(ts, D), _full((1, D)), _full((D, INC))],
        out_specs=[_rows(ts, D), _rows(ts, INC)],
        out_shape=[jax.ShapeDtypeStruct((s_len, D), BF), jax.ShapeDtypeStruct((s_len, INC), BF)],
        compiler_params=_params(1),
    )(x, g, w_in)


def _shift(e, j):
    n = e.shape[0]
    j = j % n
    return e if j == 0 else pltpu.roll(e, j, 0)


def _ln_stats(z):
    mu = jnp.mean(z, axis=-1, keepdims=True)
    zc = z - mu
    rs = lax.rsqrt(jnp.mean(zc * zc, axis=-1, keepdims=True) + EPS)
    return rs, zc * rs


def _ln_bwd(rs, zn, dzn):
    return rs * (dzn - jnp.mean(dzn, axis=-1, keepdims=True) - zn * jnp.mean(dzn * zn, axis=-1, keepdims=True))


_GELU_C0 = 0.7978845608028654
_GELU_C1 = 0.044715


def _gelu(p):
    th = jnp.tanh(_GELU_C0 * (p + _GELU_C1 * p * p * p))
    return 0.5 * p * (1.0 + th), th


def _gelu_grad(p, th):
    return 0.5 * (1.0 + th) + 0.5 * p * (1.0 - th * th) * (_GELU_C0 * (1.0 + 3.0 * _GELU_C1 * p * p))


def _pool_diff(a, t, sign):
    outs = []
    for gi in range(NBR):
        win = 2 ** (gi + 1)
        ag = a[:, gi * GW:(gi + 1) * GW]
        cnt = jnp.clip(t + 1, 1, win).astype(F32)
        ws = ag if sign > 0 else ag / cnt
        for s in range(gi + 1):
            ws = ws + _shift(ws, sign * (2 ** s))
        outs.append((ws / cnt if sign > 0 else ws) - ag)
    return outs


def _sgu_mask():
    row = lax.broadcasted_iota(jnp.int32, (GW, GW), 0)
    col = lax.broadcasted_iota(jnp.int32, (GW, GW), 1)
    return (col // CHUNK) <= (row // CHUNK)


def _assemble(pe_ref, prev_ref, cur_ref, next_ref, i, last, ts):
    pe_ref[0:HALO, :] = jnp.where(i > 0, prev_ref[...], jnp.zeros_like(prev_ref))
    pe_ref[HALO:HALO + ts, :] = cur_ref[...]
    if next_ref is not None:
        pe_ref[HALO + ts:, :] = jnp.where(i < last, next_ref[...], jnp.zeros_like(next_ref))


def _halo_specs(ts, width, s_len, with_next):
    per = ts // HALO
    specs = [pl.BlockSpec((HALO, width), lambda i: (jnp.maximum(i * per - 1, 0), 0)),
             pl.BlockSpec((ts, width), lambda i: (i, 0))]
    if with_next:
        specs.append(pl.BlockSpec((HALO, width), lambda i: (jnp.minimum((i + 1) * per, s_len // HALO - 1), 0)))
    return specs


def _branch_weights_specs():
    return [_full((NBR, GW, GW)), _full((1, BW)), _full((SKP, BW)), _full((CKP, BW)), _full((1, BW)), _full((1, BW)),
            _full((1, BW)), _full((1, BW)), _full((NBR, GW, GW)), _full((GW, NBR))]


def _mix_branches_fwd(p, pool_w, pool_scale, sconv_w, cconv_w, cln_g, cln_b, sln_g, sln_b, sgu_w, sgu_bt):
    s_len = p.shape[0]
    ts = min(BRANCH_TILE, s_len)
    ext = HALO + ts

    def body(pp_ref, pc_ref, pw_ref, ps_ref, sw_ref, cw_ref, clg_ref, clb_ref, slg_ref, slb_ref, gw_ref, gb_ref,
             y_ref, pe_ref):
        i = pl.program_id(0)
        _assemble(pe_ref, pp_ref, pc_ref, None, i, 0, ts)
        t = i * ts - HALO + lax.broadcasted_iota(jnp.int32, (ext, 1), 0)

        dgs = _pool_diff(pe_ref[:, 0:BW].astype(F32), t, 1)
        for gi in range(NBR):
            e = jnp.dot(dgs[gi][HALO:].astype(BF), pw_ref[gi].astype(BF), preferred_element_type=F32)
            y_ref[:, gi * GW:(gi + 1) * GW] = (e * ps_ref[:, gi * GW:(gi + 1) * GW]).astype(BF)

        xin = pe_ref[:, BW:2 * BW].astype(F32)
        cg = pe_ref[:, 3 * BW:4 * BW].astype(F32)
        q = cg * xin
        cv = sw_ref[2:3, :] * q + sw_ref[1:2, :] * _shift(q, 1) + sw_ref[0:1, :] * _shift(q, 2)
        y_ref[:, BW:2 * BW] = (pe_ref[HALO:, 2 * BW:3 * BW].astype(F32) * cv[HALO:]).astype(BF)

        yg = pe_ref[:, 4 * BW:5 * BW].astype(F32) * _sig(pe_ref[:, 5 * BW:6 * BW].astype(F32))
        z = cw_ref[CK - 1:CK, :] * yg
        for j in range(1, CK):
            z = z + cw_ref[CK - 1 - j:CK - j, :] * _shift(yg, j)
        _, zn = _ln_stats(z[HALO:])
        nn = zn * clg_ref[...] + clb_ref[...]
        y_ref[:, 2 * BW:3 * BW] = (nn * _sig(nn)).astype(BF)

        u, _ = _gelu(pc_ref[:, 6 * BW:7 * BW].astype(F32))
        v, _ = _gelu(pc_ref[:, 7 * BW:8 * BW].astype(F32))
        _, vn = _ln_stats(v)
        vn = (vn * slg_ref[...] + slb_ref[...]).astype(BF)
        mask = _sgu_mask()
        for hd in range(NBR):
            wm = jnp.where(mask, gw_ref[hd], 0.0).astype(BF)
            for blk in range(ts // GW):
                rows = slice(blk * GW, (blk + 1) * GW)
                cols = slice(hd * GW, (hd + 1) * GW)
                zz = jnp.dot(wm, vn[rows, cols], preferred_element_type=F32) + gb_ref[:, hd:hd + 1]
                y_ref[rows, 3 * BW + hd * GW:3 * BW + (hd + 1) * GW] = (u[rows, cols] * zz).astype(BF)

    return pl.pallas_call(
        body, name="mix_branches_fwd", grid=(s_len // ts,),
        in_specs=_halo_specs(ts, MIXC, s_len, False) + _branch_weights_specs(),
        out_specs=_rows(ts, NBR * BW),
        out_shape=jax.ShapeDtypeStruct((s_len, NBR * BW), BF),
        scratch_shapes=[pltpu.VMEM((ext, MIXC), BF)],
        compiler_params=_params(1),
    )(p, p, pool_w, pool_scale, sconv_w, cconv_w, cln_g, cln_b, sln_g, sln_b, sgu_w, sgu_bt)


def _mix_branches_bwd(p, dy, dp, pool_w, pool_scale, sconv_w, cconv_w, cln_g, cln_b, sln_g, sln_b, sgu_w, sgu_bt):
    s_len = p.shape[0]
    ts = min(BRANCH_TILE, s_len)
    ext = ts + 2 * HALO
    last = s_len // ts - 1
    tile = slice(HALO, HALO + ts)
    small_shapes = [(NBR, GW, GW), (1, BW), (SKP, BW), (CKP, BW), (1, BW), (1, BW), (1, BW), (1, BW), (NBR, GW, GW),
                    (NBR, GW, GW)]

    def body(pp_ref, pc_ref, pn_ref, dyp_ref, dyc_ref, dyn_ref, dpin_ref,
             pw_ref, ps_ref, sw_ref, cw_ref, clg_ref, clb_ref, slg_ref, slb_ref, gw_ref, gb_ref,
             dp_ref, dpw_ref, dps_ref, dsw_ref, dcw_ref, dclg_ref, dclb_ref, dslg_ref, dslb_ref, dgw_ref, dgb_ref,
             pe_ref, de_ref):
        del dyp_ref, dpin_ref
        i = pl.program_id(0)
        first = i == 0
        _assemble(pe_ref, pp_ref, pc_ref, pn_ref, i, last, ts)
        de_ref[0:HALO, :] = jnp.zeros((HALO, NBR * BW), BF)
        de_ref[HALO:HALO + ts, :] = dyc_ref[...]
        de_ref[HALO + ts:, :] = jnp.where(i < last, dyn_ref[...], jnp.zeros_like(dyn_ref))
        t = i * ts - HALO + lax.broadcasted_iota(jnp.int32, (ext, 1), 0)

        @pl.when(first)
        def _():
            dsw_ref[...] = jnp.zeros((SKP, BW), F32)
            dcw_ref[...] = jnp.zeros((CKP, BW), F32)

        dgs = _pool_diff(pe_ref[:, 0:BW].astype(F32), t, 1)
        dya = de_ref[:, 0:BW].astype(F32)
        dds = []
        for gi in range(NBR):
            cols = slice(gi * GW, (gi + 1) * GW)
            pw = pw_ref[gi].astype(BF)
            d_t = dgs[gi][tile].astype(BF)
            e = jnp.dot(d_t, pw, preferred_element_type=F32)
            _accumulate(dps_ref.at[:, cols], jnp.sum(dya[tile, cols] * e, axis=0, keepdims=True), first)
            de_g = (dya[:, cols] * ps_ref[:, cols]).astype(BF)
            _accumulate(dpw_ref.at[gi], lax.dot_general(d_t, de_g[tile], TN, preferred_element_type=F32), first)
            dds.append(lax.dot_general(de_g, pw, NT, preferred_element_type=F32))
        das = _pool_diff(jnp.concatenate(dds, axis=1), t, -1)
        for gi in range(NBR):
            dp_ref[:, gi * GW:(gi + 1) * GW] = das[gi][tile].astype(BF)

        xin = pe_ref[:, BW:2 * BW].astype(F32)
        bg = pe_ref[:, 2 * BW:3 * BW].astype(F32)
        cg = pe_ref[:, 3 * BW:4 * BW].astype(F32)
        q = cg * xin
        qs = [q, _shift(q, 1), _shift(q, 2)]
        cv = sw_ref[2:3, :] * qs[0] + sw_ref[1:2, :] * qs[1] + sw_ref[0:1, :] * qs[2]
        dyb = de_ref[:, BW:2 * BW].astype(F32)
        dcv = dyb * bg
        for j in range(SK):
            dsw_ref[SK - 1 - j:SK - j, :] += jnp.sum(dcv[tile] * qs[j][tile], axis=0, keepdims=True)
        dq = sw_ref[2:3, :] * dcv + sw_ref[1:2, :] * _shift(dcv, -1) + sw_ref[0:1, :] * _shift(dcv, -2)
        dp_ref[:, BW:2 * BW] = (dq * cg)[tile].astype(BF)
        dp_ref[:, 2 * BW:3 * BW] = (dyb * cv)[tile].astype(BF)
        dp_ref[:, 3 * BW:4 * BW] = (dq * xin)[tile].astype(BF)

        ca = pe_ref[:, 4 * BW:5 * BW].astype(F32)
        sb = _sig(pe_ref[:, 5 * BW:6 * BW].astype(F32))
        yg = ca * sb
        z = cw_ref[CK - 1:CK, :] * yg
        for j in range(1, CK):
            z = z + cw_ref[CK - 1 - j:CK - j, :] * _shift(yg, j)
        rs, zn = _ln_stats(z)
        nn = zn * clg_ref[...] + clb_ref[...]
        sn = _sig(nn)
        dn = de_ref[:, 2 * BW:3 * BW].astype(F32) * (sn * (1.0 + nn * (1.0 - sn)))
        _accumulate(dclg_ref, jnp.sum((dn * zn)[tile], axis=0, keepdims=True), first)
        _accumulate(dclb_ref, jnp.sum(dn[tile], axis=0, keepdims=True), first)
        dz = _ln_bwd(rs, zn, dn * clg_ref[...])
        dz = jnp.where(t >= i * ts, dz, 0.0)
        dz_t = dz[tile]
        dyg = cw_ref[CK - 1:CK, :] * dz
        dcw_ref[CK - 1:CK, :] += jnp.sum(dz_t * yg[tile], axis=0, keepdims=True)
        for j in range(1, CK):
            dyg = dyg + cw_ref[CK - 1 - j:CK - j, :] * _shift(dz, -j)
            dcw_ref[CK - 1 - j:CK - j, :] += jnp.sum(dz_t * _shift(yg, j)[tile], axis=0, keepdims=True)
        dp_ref[:, 4 * BW:5 * BW] = (dyg * sb)[tile].astype(BF)
        dp_ref[:, 5 * BW:6 * BW] = (dyg * ca * sb * (1.0 - sb))[tile].astype(BF)

        pu = pc_ref[:, 6 * BW:7 * BW].astype(F32)
        pv = pc_ref[:, 7 * BW:8 * BW].astype(F32)
        u, thu = _gelu(pu)
        v, thv = _gelu(pv)
        vrs, vn0 = _ln_stats(v)
        vn = (vn0 * slg_ref[...] + slb_ref[...]).astype(BF)
        dyd = dyc_ref[:, 3 * BW:4 * BW].astype(F32)
        dzz = dyd * u
        dzb = dzz.astype(BF)
        mask = _sgu_mask()
        dvn_cols = []
        for hd in range(NBR):
            cols = slice(hd * GW, (hd + 1) * GW)
            wm = jnp.where(mask, gw_ref[hd], 0.0).astype(BF)
            dwm = jnp.zeros((GW, GW), F32)
            dbs = jnp.zeros((GW, GW), F32)
            dvn_rows = []
            for blk in range(ts // GW):
                rows = slice(blk * GW, (blk + 1) * GW)
                zz = jnp.dot(wm, vn[rows, cols], preferred_element_type=F32) + gb_ref[:, hd:hd + 1]
                dp_ref[rows, 6 * BW + hd * GW:6 * BW + (hd + 1) * GW] = (
                    dyd[rows, cols] * zz * _gelu_grad(pu[rows, cols], thu[rows, cols])).astype(BF)
                dwm = dwm + lax.dot_general(dzb[rows, cols], vn[rows, cols], NT, preferred_element_type=F32)
                dbs = dbs + dzz[rows, cols]
                dvn_rows.append(lax.dot_general(wm, dzb[rows, cols], TN, preferred_element_type=F32))
            _accumulate(dgw_ref.at[hd], jnp.where(mask, dwm, 0.0), first)
            _accumulate(dgb_ref.at[hd], dbs, first)
            dvn_cols.append(jnp.concatenate(dvn_rows, axis=0))
        dvn = jnp.concatenate(dvn_cols, axis=1)
        _accumulate(dslg_ref, jnp.sum(dvn * vn0, axis=0, keepdims=True), first)
        _accumulate(dslb_ref, jnp.sum(dvn, axis=0, keepdims=True), first)
        dv = _ln_bwd(vrs, vn0, dvn * slg_ref[...])
        dp_ref[:, 7 * BW:8 * BW] = (dv * _gelu_grad(pv, thv)).astype(BF)

    const = lambda shp: pl.BlockSpec(shp, lambda i: (0,) * len(shp))
    outs = pl.pallas_call(
        body, name="mix_branches_bwd", grid=(s_len // ts,),
        in_specs=(_halo_specs(ts, MIXC, s_len, True) + _halo_specs(ts, NBR * BW, s_len, True)
                  + [pl.BlockSpec(memory_space=pl.ANY)] + _branch_weights_specs()),
        out_specs=[pl.BlockSpec((ts, MIXC), lambda i: (i, 0))] + [const(s) for s in small_shapes],
        out_shape=[jax.ShapeDtypeStruct((s_len, INC), BF)] + [jax.ShapeDtypeStruct(s, F32) for s in small_shapes],
        scratch_shapes=[pltpu.VMEM((ext, MIXC), BF), pltpu.VMEM((ext, NBR * BW), BF)],
        input_output_aliases={6: 0},
        compiler_params=_params(1),
    )(p, p, p, dy, dy, dy, dp, pool_w, pool_scale, sconv_w, cconv_w, cln_g, cln_b, sln_g, sln_b, sgu_w, sgu_bt)
    return outs


def _mix_out_fwd(x, y, p, w_up, w_out):
    s_len = x.shape[0]
    ts = min(TOKEN_TILE, s_len)

    def body(x_ref, y_ref, pg_ref, wu_ref, wo_ref, xo_ref, m_ref, up_ref):
        m = jnp.zeros((ts, D), F32)
        for gi in range(NBR):
            up = jnp.dot(y_ref[:, gi * BW:(gi + 1) * BW], wu_ref[gi], preferred_element_type=F32)
            up_ref[:, gi * D:(gi + 1) * D] = up.astype(BF)
            m = m + _sig(pg_ref[:, gi * D:(gi + 1) * D].astype(F32)) * up
        mb = m.astype(BF)
        m_ref[...] = mb
        xo_ref[...] = x_ref[...] + jnp.dot(mb, wo_ref[...], preferred_element_type=F32)

    return pl.pallas_call(
        body, name="mix_out_fwd", grid=(s_len // ts,),
        in_specs=[_rows(ts, D), _rows(ts, NBR * BW), _rows(ts, NBR * D, 1), _full((NBR, BW, D)), _full((D, D))],
        out_specs=[_rows(ts, D), _rows(ts, D), _rows(ts, NBR * D)],
        out_shape=[jax.ShapeDtypeStruct((s_len, D), F32), jax.ShapeDtypeStruct((s_len, D), BF),
                   jax.ShapeDtypeStruct((s_len, NBR * D), BF)],
        compiler_params=_params(1),
    )(x, y, p, w_up, w_out)


def _mix_out_bwd(dxo, up, p, w_up, w_out):
    s_len = dxo.shape[0]
    ts = min(TOKEN_TILE, s_len)

    def body(dxo_ref, up_ref, pg_ref, wu_ref, wo_ref, dy_ref, dp_ref, dup_ref, dxb_ref):
        dxb = dxo_ref[...].astype(BF)
        dxb_ref[...] = dxb
        dm = lax.dot_general(dxb, wo_ref[...], NT, preferred_element_type=F32)
        for gi in range(NBR):
            cols = slice(gi * D, (gi + 1) * D)
            gate = _sig(pg_ref[:, cols].astype(F32))
            dp_ref[:, cols] = (dm * up_ref[:, cols].astype(F32) * gate * (1.0 - gate)).astype(BF)
            dup = (dm * gate).astype(BF)
            dup_ref[:, cols] = dup
            dy_ref[:, gi * BW:(gi + 1) * BW] = lax.dot_general(
                dup, wu_ref[gi], NT, preferred_element_type=F32).astype(BF)

    return pl.pallas_call(
        body, name="mix_out_bwd", grid=(s_len // ts,),
        in_specs=[_rows(ts, D), _rows(ts, NBR * D), _rows(ts, NBR * D, 1), _full((NBR, BW, D)), _full((D, D))],
        out_specs=[_rows(ts, NBR * BW), _rows(ts, NBR * D, 1), _rows(ts, NBR * D), _rows(ts, D)],
        out_shape=[jax.ShapeDtypeStruct((s_len, NBR * BW), BF), jax.ShapeDtypeStruct((s_len, INC), BF),
                   jax.ShapeDtypeStruct((s_len, NBR * D), BF), jax.ShapeDtypeStruct((s_len, D), BF)],
        compiler_params=_params(1),
    )(dxo, up, p, w_up, w_out)


def _mix_in_bwd(dp, w_in, x, g, dxo):
    s_len = x.shape[0]
    ts = min(TOKEN_TILE, s_len)

    def body(dp_ref, w_ref, x_ref, g_ref, dxo_ref, dxi_ref, dg_ref):
        i = pl.program_id(0)
        dh = lax.dot_general(dp_ref[...], w_ref[...], NT, preferred_element_type=F32)
        dx, dg = _rms_bwd(x_ref[...], g_ref[...], dh)
        dxi_ref[...] = dxo_ref[...] + dx
        _accumulate(dg_ref, dg, i == 0)

    return pl.pallas_call(
        body, name="mix_in_bwd", grid=(s_len // ts,),
        in_specs=[_rows(ts, INC), _full((D, INC)), _rows(ts, D), _full((1, D)), _rows(ts, D)],
        out_specs=[_rows(ts, D), pl.BlockSpec((1, D), lambda i: (0, 0))],
        out_shape=[jax.ShapeDtypeStruct((s_len, D), F32), jax.ShapeDtypeStruct((1, D), F32)],
        compiler_params=_params(1),
    )(dp, w_in, x, g, dxo)


def _loss_head(x, g, target):
    s_len = x.shape[0]
    ts = min(512, s_len)

    def body(x_ref, g_ref, t_ref, dx_ref, dg_ref, loss_ref):
        i = pl.program_id(0)
        xv = x_ref[...]
        _, xh = _rms_stats(xv)
        err = xh * g_ref[...] - t_ref[...]
        part = 0.5 * jnp.sum(jnp.mean(err * err, axis=-1, keepdims=True), axis=0, keepdims=True)
        dx, dg = _rms_bwd(xv, g_ref[...], err * (1.0 / D))
        dx_ref[...] = dx
        _accumulate(dg_ref, dg, i == 0)
        _accumulate(loss_ref, jnp.broadcast_to(part, (1, GW)), i == 0)

    return pl.pallas_call(
        body, name="loss_head", grid=(s_len // ts,),
        in_specs=[_rows(ts, D), _full((1, D)), _rows(ts, D)],
        out_specs=[_rows(ts, D), pl.BlockSpec((1, D), lambda i: (0, 0)), pl.BlockSpec((1, GW), lambda i: (0, 0))],
        out_shape=[jax.ShapeDtypeStruct((s_len, D), F32), jax.ShapeDtypeStruct((1, D), F32),
                   jax.ShapeDtypeStruct((1, GW), F32)],
        compiler_params=_params(1),
    )(x, g, target)


def _row_tile(rows, cols, budget=1 << 18):
    tr = rows
    while tr * cols > budget and tr % 16 == 0:
        tr //= 2
    return tr


def _elementwise(fn, name, ins, out_dtypes):
    rows, cols = ins[0].shape
    tr = _row_tile(rows, cols)
    n_in = len(ins)

    def body(*refs):
        res = fn(*[r[...] for r in refs[:n_in]])
        for o_ref, val in zip(refs[n_in:], res):
            o_ref[...] = val.astype(o_ref.dtype)

    outs = pl.pallas_call(
        body, name=name, grid=(rows // tr,),
        in_specs=[_rows(tr, cols)] * n_in, out_specs=[_rows(tr, cols)] * len(out_dtypes),
        out_shape=[jax.ShapeDtypeStruct((rows, cols), dt) for dt in out_dtypes],
        compiler_params=_params(1),
    )(*ins)
    return outs


def _tiled(fn, name, grid, in_specs, out_specs, out_shape, args, scalars=None, carried=0):
    n_in = len(in_specs) - carried
    n_pre = 0 if scalars is None else 1

    def body(*refs):
        refs = refs[n_pre:]
        res = fn(*[r[...] for r in refs[:n_in]])
        for o_ref, val in zip(refs[n_in + carried:], res):
            o_ref[...] = val.astype(o_ref.dtype)

    aliases = {n_pre + n_in + k: k for k in range(carried)}
    if scalars is None:
        return pl.pallas_call(body, name=name, grid=grid, in_specs=in_specs, out_specs=out_specs, out_shape=out_shape,
                              input_output_aliases=aliases, compiler_params=_params(len(grid)))(*args)
    spec = pltpu.PrefetchScalarGridSpec(num_scalar_prefetch=1, grid=grid, in_specs=in_specs, out_specs=out_specs)
    return pl.pallas_call(body, name=name, grid_spec=spec, out_shape=out_shape, input_output_aliases=aliases,
                          compiler_params=_params(len(grid)))(scalars, *args)


def _cast_into(shard, layer, col, dtype, sc):
    ks, ns = shard.shape[1:]
    tr = _row_tile(ks, ns)
    full = (ks, ns * NCHIP) if col else (ks * NCHIP, ns)
    out_idx = (lambda i, s: (i, s[1])) if col else (lambda i, s: (s[1] * (ks // tr) + i, 0))
    return _tiled(lambda v: (v,), "cast_into", (ks // tr,), [pl.BlockSpec((None, tr, ns), lambda i, s: (layer, i, 0))],
                  [pl.BlockSpec((tr, ns), out_idx)], [jax.ShapeDtypeStruct(full, dtype)], [shard], sc)[0]


def _pair_sum(g, got, col, sc):
    hk, hn = got.shape
    tr = _row_tile(hk, hn)
    g_idx = (lambda i, s: (s[0] * (hk // tr) + i, 0)) if col else (lambda i, s: (i, s[0]))
    plain = pl.BlockSpec((tr, hn), lambda i, s: (i, 0))
    return _tiled(lambda a, b: (a.astype(F32) + b.astype(F32),), "pair_sum", (hk // tr,),
                  [pl.BlockSpec((tr, hn), g_idx), plain], [plain], [jax.ShapeDtypeStruct((hk, hn), BF)], [g, got], sc)[0]


def _chip_sum(ph, got, col, layer, depth, sc, carry):
    qk, qn = got.shape[1:]
    tr = _row_tile(qk, qn)
    ph_idx = (lambda i, s: (i, s[1])) if col else (lambda i, s: (s[1] * (qk // tr) + i, 0))
    out_shape = (depth, 2 * qk, qn) if col else (depth, qk, 2 * qn)
    out_idx = (lambda i, s: (layer, s[0] * (qk // tr) + i, 0)) if col else (lambda i, s: (layer, i, s[0]))
    in_specs = [pl.BlockSpec((tr, qn), ph_idx)] + [pl.BlockSpec((None, tr, qn), lambda i, s, j=j: (j, i, 0)) for j in range(3)]
    args = [ph, got, got, got]
    if carry is not None:
        in_specs.append(ANY)
        args.append(carry)
    return _tiled(lambda a, b, c_, d_: (a.astype(F32) + b.astype(F32) + c_.astype(F32) + d_.astype(F32),), "chip_sum",
                  (qk // tr,), in_specs, [pl.BlockSpec((None, tr, qn), out_idx)],
                  [jax.ShapeDtypeStruct(out_shape, F32)], args, sc, carried=0 if carry is None else 1)[0]


def _adamw_math(w, g, m, v):
    m = ADAM_B1 * m + (1.0 - ADAM_B1) * g
    v = ADAM_B2 * v + (1.0 - ADAM_B2) * (g * g)
    m_hat = m / (1.0 - ADAM_B1 ** ADAM_STEP)
    v_hat = v / (1.0 - ADAM_B2 ** ADAM_STEP)
    delta = -ADAM_LR * (m_hat / (jnp.sqrt(v_hat) + ADAM_EPS) + ADAM_WD * w)
    return delta, m, v


def _adamw_layer(w, g, m, v, layer, carry):
    k, n = w.shape[1:]
    tr = _row_tile(k, n)
    blk = pl.BlockSpec((None, tr, n), lambda i: (layer, i, 0))
    carry = list(carry or [])
    return _tiled(_adamw_math, "adamw_layer", (k // tr,), [blk] * 4 + [ANY] * len(carry), [blk] * 3,
                  [jax.ShapeDtypeStruct(w.shape, F32)] * 3, [w, g, m, v] + carry, None, carried=len(carry))


def _adamw(w, g, m, v):
    shape = w.shape
    two_d = lambda a: a.reshape(-1, shape[-1])
    outs = _elementwise(_adamw_math, "adamw", [two_d(w), two_d(g), two_d(m), two_d(v)], [F32, F32, F32])
    return [o.reshape(shape) for o in outs]


ANY = pl.BlockSpec(memory_space=pl.ANY)


def _place():
    x, y, c = lax.axis_index("x"), lax.axis_index("y"), lax.axis_index("c")
    chips = [(1 - x, y), (x, 1 - y), (1 - x, 1 - y)]
    return x, y, c, chips


def _cols(ref, start, size):
    idx = (slice(None),) * (len(ref.shape) - 1) + (pl.ds(pl.multiple_of(start, GW), size),)
    return ref.at[idx]


def _rows_of(ref, start, size):
    nd = len(ref.shape)
    idx = (slice(None),) * (nd - 2) + (pl.ds(pl.multiple_of(start, 16), size), slice(None))
    return ref.at[idx]


def _region(ref, col_sharded, chip, half):
    k, n = ref.shape
    align = 16 if ref.dtype == BF else 8
    if col_sharded:
        return ref.at[pl.ds(pl.multiple_of(half * (k // 2), align), k // 2),
                      pl.ds(pl.multiple_of(chip * (n // NCHIP), GW), n // NCHIP)]
    rows = k // (2 * NCHIP)
    return ref.at[pl.ds(pl.multiple_of((2 * chip + half) * rows, align), rows), :]


def _gather_stage(bufs, col_sharded, stage):
    n = len(bufs)

    def body(*refs):
        outs = refs[n:2 * n]
        send_sems, recv_sems = refs[2 * n:]
        x, y, c, chips = _place()
        sends, lands = [], []
        for k in range(n):
            for j, chip in enumerate(chips):
                theirs = 2 * chip[0] + chip[1]
                if stage == 0:
                    src, to = _region(outs[k], col_sharded[k], 2 * x + y, c), (*chip, c)
                    land = _region(outs[k], col_sharded[k], theirs, c)
                else:
                    src, to = _region(outs[k], col_sharded[k], theirs, c), (x, y, 1 - c)
                    land = _region(outs[k], col_sharded[k], theirs, 1 - c)
                sems = dict(send_sem=send_sems.at[3 * k + j], recv_sem=recv_sems.at[3 * k + j], device_id=to,
                            device_id_type=MESH)
                sends.append(pltpu.make_async_remote_copy(src_ref=src, dst_ref=src, **sems))
                lands.append(pltpu.make_async_remote_copy(src_ref=land, dst_ref=land, **sems))
        for cp in sends:
            cp.start()
        for cp in lands:
            cp.wait_recv()
        for cp in sends:
            cp.wait_send()

    return pl.pallas_call(
        body, name="gather_ici" if stage == 0 else "gather_d2d",
        in_specs=[ANY] * n, out_specs=[ANY] * n, input_output_aliases={k: k for k in range(n)},
        out_shape=[jax.ShapeDtypeStruct(b.shape, b.dtype) for b in bufs],
        scratch_shapes=[pltpu.SemaphoreType.DMA((3 * n,)), pltpu.SemaphoreType.DMA((3 * n,))],
    )(*bufs)


def _half(ref, col_sharded, c):
    k, n = ref.shape[-2:]
    return _rows_of(ref, c * (k // 2), k // 2) if col_sharded else _cols(ref, c * (n // 2), n // 2)


def _quarter(ref, col_sharded, j):
    k, n = ref.shape[-2:]
    return _cols(ref, j * (n // NCHIP), n // NCHIP) if col_sharded else _rows_of(ref, j * (k // NCHIP), k // NCHIP)


def _pair_exchange(grads, col_sharded):
    n = len(grads)

    def half_shape(g, col):
        return (g.shape[0] // 2, g.shape[1]) if col else (g.shape[0], g.shape[1] // 2)

    def body(*refs):
        ins, got = refs[:n], refs[n:2 * n]
        send_sems, recv_sems = refs[2 * n:]
        x, y, c, _ = _place()
        sends = [pltpu.make_async_remote_copy(
            src_ref=_half(ins[k], col_sharded[k], 1 - c), dst_ref=got[k], send_sem=send_sems.at[k],
            recv_sem=recv_sems.at[k], device_id=(x, y, 1 - c), device_id_type=MESH) for k in range(n)]
        for cp in sends:
            cp.start()
        for cp in sends:
            cp.wait()

    return pl.pallas_call(
        body, name="grad_pair_exchange", in_specs=[ANY] * n, out_specs=[ANY] * n,
        out_shape=[jax.ShapeDtypeStruct(half_shape(g, col), g.dtype) for g, col in zip(grads, col_sharded)],
        scratch_shapes=[pltpu.SemaphoreType.DMA((n,)), pltpu.SemaphoreType.DMA((n,))],
    )(*grads)


def _chip_exchange(halves, col_sharded):
    n = len(halves)

    def quarter_shape(h, col):
        return (3, h.shape[0], h.shape[1] // NCHIP) if col else (3, h.shape[0] // NCHIP, h.shape[1])

    def body(*refs):
        ins, got = refs[:n], refs[n:2 * n]
        send_sems, recv_sems = refs[2 * n:]
        x, y, c, chips = _place()
        sends = [pltpu.make_async_remote_copy(
            src_ref=_quarter(ins[k], col_sharded[k], 2 * chip[0] + chip[1]), dst_ref=got[k].at[j],
            send_sem=send_sems.at[3 * k + j], recv_sem=recv_sems.at[3 * k + j], device_id=(*chip, c), device_id_type=MESH)
            for k in range(n) for j, chip in enumerate(chips)]
        for cp in sends:
            cp.start()
        for cp in sends:
            cp.wait()

    return pl.pallas_call(
        body, name="grad_chip_exchange", in_specs=[ANY] * n, out_specs=[ANY] * n,
        out_shape=[jax.ShapeDtypeStruct(quarter_shape(h, col), h.dtype) for h, col in zip(halves, col_sharded)],
        scratch_shapes=[pltpu.SemaphoreType.DMA((3 * n,)), pltpu.SemaphoreType.DMA((3 * n,))],
    )(*halves)


def _sibling_exchange(shards, col_sharded, layer):
    n = len(shards)

    def body(*refs):
        outs = refs[n:2 * n]
        send_sems, recv_sems = refs[2 * n:]
        x, y, c, _ = _place()
        sends, lands = [], []
        for k in range(n):
            sems = dict(send_sem=send_sems.at[k], recv_sem=recv_sems.at[k], device_id=(x, y, 1 - c), device_id_type=MESH)
            mine = _half(outs[k].at[layer], col_sharded[k], c)
            theirs = _half(outs[k].at[layer], col_sharded[k], 1 - c)
            sends.append(pltpu.make_async_remote_copy(src_ref=mine, dst_ref=mine, **sems))
            lands.append(pltpu.make_async_remote_copy(src_ref=theirs, dst_ref=theirs, **sems))
        for cp in sends:
            cp.start()
        for cp in lands:
            cp.wait_recv()
        for cp in sends:
            cp.wait_send()

    return pl.pallas_call(
        body, name="grad_sibling_exchange", in_specs=[ANY] * n, out_specs=[ANY] * n,
        input_output_aliases={k: k for k in range(n)},
        out_shape=[jax.ShapeDtypeStruct(s.shape, s.dtype) for s in shards],
        scratch_shapes=[pltpu.SemaphoreType.DMA((n,)), pltpu.SemaphoreType.DMA((n,))],
    )(*shards)


def _all_reduce_small(buf):
    rows = buf.shape[0]
    per = rows // NDEV
    flips = [(fx, fy, fc) for fx in (0, 1) for fy in (0, 1) for fc in (0, 1)][1:]

    def body(in_ref, out_ref, got_ref, send_sems, recv_sems):
        x, y, c, _ = _place()
        me = 4 * x + 2 * y + c

        def peer(f):
            return tuple(1 - pos if flip else pos for pos, flip in zip((x, y, c), f))

        def block(ref, dev):
            return ref.at[pl.ds(pl.multiple_of(dev * per, 8), per), :]

        scatter = []
        for k, f in enumerate(flips):
            px, py, pc = peer(f)
            scatter.append(pltpu.make_async_remote_copy(
                src_ref=block(in_ref, 4 * px + 2 * py + pc), dst_ref=got_ref.at[k], send_sem=send_sems.at[k],
                recv_sem=recv_sems.at[k], device_id=(px, py, pc), device_id_type=MESH))
        for cp in scatter:
            cp.start()
        for cp in scatter:
            cp.wait()
        total = block(in_ref, me)[...]
        for k in range(len(flips)):
            total = total + got_ref[k]
        block(out_ref, me)[...] = total
        share = []
        for k, f in enumerate(flips):
            share.append(pltpu.make_async_remote_copy(
                src_ref=block(out_ref, me), dst_ref=block(out_ref, me), send_sem=send_sems.at[7 + k],
                recv_sem=recv_sems.at[7 + k], device_id=peer(f), device_id_type=MESH))
        for cp in share:
            cp.start()
        for k, f in enumerate(flips):
            share[k].wait_send()
            px, py, pc = peer(f)
            theirs = block(out_ref, 4 * px + 2 * py + pc)
            pltpu.make_async_remote_copy(
                src_ref=theirs, dst_ref=theirs, send_sem=send_sems.at[7 + k], recv_sem=recv_sems.at[7 + k],
                device_id=(px, py, pc), device_id_type=MESH).wait_recv()

    vmem = pl.BlockSpec(memory_space=pltpu.VMEM)
    return pl.pallas_call(
        body, name="all_reduce_small", in_specs=[vmem], out_specs=vmem,
        out_shape=jax.ShapeDtypeStruct((rows, GW), F32),
        scratch_shapes=[pltpu.VMEM((NDEV - 1, per, GW), F32), pltpu.SemaphoreType.DMA((14,)),
                        pltpu.SemaphoreType.DMA((14,))],
    )(buf)


BIG = ("ffn1_w13", "ffn1_w2", "w_in", "w_up", "w_out", "ffn2_w13", "ffn2_w2")
BIG_COL_SHARDED = (True, False, True, True, False, True, False)
SMALL = ("ffn1_norm", "mix_norm", "pool_w", "pool_scale", "sconv_w", "cconv_w", "cconv_ln_g", "cconv_ln_b",
         "sgu_ln_g", "sgu_ln_b", "sgu_w", "sgu_b", "ffn2_norm", "final_norm")
WEIGHTS = ("ffn1_norm", "ffn1_w13", "ffn1_w2", "mix_norm", "w_in", "pool_w", "pool_scale", "sconv_w", "cconv_w",
           "cconv_ln_g", "cconv_ln_b", "sgu_ln_g", "sgu_ln_b", "sgu_w", "sgu_b", "w_up", "w_out", "ffn2_norm",
           "ffn2_w13", "ffn2_w2", "final_norm")


def _pad_rows(a, rows):
    return jnp.pad(a, ((0, 0), (0, rows - a.shape[1]), (0, 0)))


def kernel(x, ffn1_norm, ffn1_w13, ffn1_w2, mix_norm, w_in, pool_w, pool_scale, sconv_w, cconv_w, cconv_ln_g, cconv_ln_b, sgu_ln_g, sgu_ln_b, sgu_w, sgu_b, w_up, w_out, ffn2_norm, ffn2_w13, ffn2_w2, final_norm, loss_target, m_ffn1_norm, m_ffn1_w13, m_ffn1_w2, m_mix_norm, m_w_in, m_pool_w, m_pool_scale, m_sconv_w, m_cconv_w, m_cconv_ln_g, m_cconv_ln_b, m_sgu_ln_g, m_sgu_ln_b, m_sgu_w, m_sgu_b, m_w_up, m_w_out, m_ffn2_norm, m_ffn2_w13, m_ffn2_w2, m_final_norm, v_ffn1_norm, v_ffn1_w13, v_ffn1_w2, v_mix_norm, v_w_in, v_pool_w, v_pool_scale, v_sconv_w, v_cconv_w, v_cconv_ln_g, v_cconv_ln_b, v_sgu_ln_g, v_sgu_ln_b, v_sgu_w, v_sgu_b, v_w_up, v_w_out, v_ffn2_norm, v_ffn2_w13, v_ffn2_w2, v_final_norm):
    args = dict(locals())
    w = {nm: args[nm] for nm in WEIGHTS}
    m = {nm: args["m_" + nm] for nm in WEIGHTS}
    v = {nm: args["v_" + nm] for nm in WEIGHTS}
    depth = ffn1_w13.shape[0]
    chip = 2 * lax.axis_index("x") + lax.axis_index("y")

    sc = jnp.stack([lax.axis_index("c"), chip]).astype(jnp.int32)
    w_up_2d = w_up.reshape(depth, NBR * BW, w_up.shape[-1])
    to_gather = [(w[nm] if nm != "w_up" else w_up_2d, col, BF) for nm, col in zip(BIG, BIG_COL_SHARDED)]
    to_gather += [(_pad_rows(sconv_w, 2 * SKP), True, F32), (_pad_rows(cconv_w, 2 * CKP), True, F32)]
    gathered = []
    for l in range(depth):
        bufs = [_cast_into(shard, l, col, dt, sc) for shard, col, dt in to_gather]
        cols = [col for _, col, _ in to_gather]
        gathered.append(_gather_stage(_gather_stage(bufs, cols, 0), cols, 1))

    xs = x[0]
    row = lambda a: a.reshape(1, -1)
    saved = []
    for l in range(depth):
        fw = dict(zip(BIG, gathered[l][:7]))
        sconv_full, cconv_full = gathered[l][7][:SKP], gathered[l][8][:CKP]
        lw = dict(
            g1=row(ffn1_norm[l]), gm=row(mix_norm[l]), g2=row(ffn2_norm[l]),
            w13a=fw["ffn1_w13"], w2a=fw["ffn1_w2"], w13b=fw["ffn2_w13"], w2b=fw["ffn2_w2"],
            w_in=fw["w_in"], w_up=fw["w_up"].reshape(NBR, BW, D), w_out=fw["w_out"],
            branch=(pool_w[l], row(pool_scale[l]), sconv_full, cconv_full, row(cconv_ln_g[l]), row(cconv_ln_b[l]),
                    row(sgu_ln_g[l]), row(sgu_ln_b[l]), sgu_w[l], sgu_b[l].T))
        x1, h1, ab1 = _ffn_fwd(xs, lw["g1"], lw["w13a"], lw["w2a"])
        hm, p = _mix_in(x1, lw["gm"], lw["w_in"])
        y = _mix_branches_fwd(p, *lw["branch"])
        x2, merged, up = _mix_out_fwd(x1, y, p, lw["w_up"], lw["w_out"])
        x3, h2, ab2 = _ffn_fwd(x2, lw["g2"], lw["w13b"], lw["w2b"])
        saved.append(dict(lw=lw, x0=xs, x1=x1, x2=x2, h1=h1, ab1=ab1, hm=hm, p=p, y=y, merged=merged, up=up, h2=h2,
                          ab2=ab2))
        xs = x3

    dx, d_final, loss_part = _loss_head(xs, row(final_norm), loss_target[0])
    loss = lax.psum(loss_part[0, 0], ("x", "y", "c"))

    big_parts = {nm: [None] * depth for nm in BIG}
    small_parts = {nm: [None] * depth for nm in SMALL if nm != "final_norm"}
    reduced = [None] * len(BIG)
    big_updates = {}
    for l in reversed(range(depth)):
        sv = saved[l]
        lw = sv["lw"]
        dx, dab, s_act, dyh, dg2 = _ffn_bwd(dx, sv["x2"], lw["g2"], sv["ab2"], lw["w13b"], lw["w2b"])
        big_parts["ffn2_w13"][l] = _wgrad(sv["h2"], dab, D, 512, "wgrad_w13")
        big_parts["ffn2_w2"][l] = _wgrad(s_act, dyh, 256, D, "wgrad_w2")
        small_parts["ffn2_norm"][l] = dg2

        dy, dp, dup, dxb = _mix_out_bwd(dx, sv["up"], sv["p"], lw["w_up"], lw["w_out"])
        big_parts["w_out"][l] = _wgrad(sv["merged"], dxb, D, 512, "wgrad_w_out")
        big_parts["w_up"][l] = _wgrad_groups(sv["y"], dup, BW, D, "wgrad_w_up")
        (dp, d_pool_w, d_pool_scale, d_sconv, d_cconv, d_clg, d_clb, d_slg, d_slb, d_sgu_w, d_sgu_b) = _mix_branches_bwd(
            sv["p"], dy, dp, *lw["branch"])
        big_parts["w_in"][l] = _wgrad(sv["hm"], dp, D, 512, "wgrad_w_in")
        dx, dgm = _mix_in_bwd(dp, lw["w_in"], sv["x1"], lw["gm"], dx)
        small_parts["mix_norm"][l] = dgm
        small_parts["pool_w"][l] = d_pool_w
        small_parts["pool_scale"][l] = d_pool_scale
        small_parts["sconv_w"][l] = d_sconv[:SK]
        small_parts["cconv_w"][l] = d_cconv[:CK]
        small_parts["cconv_ln_g"][l] = d_clg
        small_parts["cconv_ln_b"][l] = d_clb
        small_parts["sgu_ln_g"][l] = d_slg
        small_parts["sgu_ln_b"][l] = d_slb
        small_parts["sgu_w"][l] = d_sgu_w
        small_parts["sgu_b"][l] = jnp.sum(d_sgu_b, axis=-1)

        dx, dab, s_act, dyh, dg1 = _ffn_bwd(dx, sv["x0"], lw["g1"], sv["ab1"], lw["w13a"], lw["w2a"])
        big_parts["ffn1_w13"][l] = _wgrad(sv["h1"], dab, D, 512, "wgrad_w13")
        big_parts["ffn1_w2"][l] = _wgrad(s_act, dyh, 256, D, "wgrad_w2")
        small_parts["ffn1_norm"][l] = dg1

        cols = list(BIG_COL_SHARDED)
        parts = [big_parts[nm][l] for nm in BIG]
        sib_half = _pair_exchange(parts, cols)
        pair_sums = [_pair_sum(g, got, col, sc) for g, got, col in zip(parts, sib_half, cols)]
        got_q = _chip_exchange(pair_sums, cols)
        shards = [_chip_sum(ph, got, col, l, depth, sc, carry)
                  for ph, got, col, carry in zip(pair_sums, got_q, cols, reduced)]
        reduced = _sibling_exchange(shards, cols, l)
        for k, nm in enumerate(BIG):
            shape3 = reduced[k].shape
            outs = _adamw_layer(w[nm].reshape(shape3), reduced[k], m[nm].reshape(shape3), v[nm].reshape(shape3), l,
                                big_updates.get(nm))
            big_updates[nm] = outs
    grad_x = dx[None]
    big_grads = {nm: g.reshape(w[nm].shape) for nm, g in zip(BIG, reduced)}

    small_local = {nm: jnp.stack(parts).reshape(depth, *w[nm].shape[1:-1], -1) if nm not in ("sconv_w", "cconv_w")
                   else jnp.stack(parts) for nm, parts in small_parts.items()}
    small_local["final_norm"] = d_final.reshape(-1)
    sizes = [small_local[nm].size for nm in SMALL]
    total = sum(sizes)
    pad_to = NDEV * 8 * GW
    padded = -(-total // pad_to) * pad_to
    packed = jnp.concatenate([small_local[nm].reshape(-1) for nm in SMALL] + [jnp.zeros((padded - total,), F32)])
    summed = _all_reduce_small(packed.reshape(-1, GW)).reshape(-1)
    small_grads, off = {}, 0
    for nm, size in zip(SMALL, sizes):
        small_grads[nm] = summed[off:off + size].reshape(small_local[nm].shape)
        off += size
    for nm in ("sconv_w", "cconv_w"):
        small_grads[nm] = lax.dynamic_slice_in_dim(small_grads[nm], chip * GW, GW, axis=2)

    grads = {**big_grads, **small_grads}

    delta, new_m, new_v = {}, {}, {}
    for nm in BIG:
        delta[nm], new_m[nm], new_v[nm] = [a.reshape(w[nm].shape) for a in big_updates[nm]]
    s_sizes = [w[nm].size for nm in SMALL]
    s_total = sum(s_sizes)
    s_padded = -(-s_total // (8 * GW)) * (8 * GW)

    def pack(tree):
        return jnp.concatenate([tree[nm].reshape(-1) for nm in SMALL] + [jnp.ones((s_padded - s_total,), F32)]).reshape(-1, GW)

    packed_out = _adamw(pack(w), pack(grads), pack(m), pack(v))
    off = 0
    for nm, size in zip(SMALL, s_sizes):
        for tree, arr in zip((delta, new_m, new_v), packed_out):
            tree[nm] = arr.reshape(-1)[off:off + size].reshape(w[nm].shape)
        off += size

    return (loss, grad_x, *[grads[nm] for nm in WEIGHTS], *[delta[nm] for nm in WEIGHTS],
            *[new_m[nm] for nm in WEIGHTS], *[new_v[nm] for nm in WEIGHTS])
```

```python
import jax
import jax.numpy as jnp
from jax import lax
from jax.experimental import pallas as pl
from jax.experimental.pallas import tpu as pltpu

D = 1024
FF = 2816
BW = 512
NBR = 4
MIXC = 4096
INC = 8192
GW = 128
CHUNK = 64
SK = 3
CK = 31
SKP = 8
CKP = 32
HALO = 32
TOKEN_TILE = 256
BRANCH_TILE = 512
EPS = 1e-6
NCHIP = 4
NDEV = 8

ADAM_LR = 0.001
ADAM_B1 = 0.9
ADAM_B2 = 0.999
ADAM_EPS = 1e-08
ADAM_WD = 0.01
ADAM_STEP = 10

VMEM_LIMIT = 56 * 1024 * 1024

BF = jnp.bfloat16
F32 = jnp.float32
MESH = pl.DeviceIdType.MESH
NT = (((1,), (1,)), ((), ()))
TN = (((0,), (0,)), ((), ()))


def _params(n_axes):
    return pltpu.CompilerParams(dimension_semantics=("arbitrary",) * n_axes, vmem_limit_bytes=VMEM_LIMIT)


class _Job:
    def __init__(self, args, n_inplace, fresh, nsem, phases):
        self.args, self.n_inplace, self.fresh, self.nsem, self.phases = list(args), n_inplace, list(fresh), nsem, phases


def _pc(jobs, body, name, in_specs, out_specs, out_shape, grid=(), scratch_shapes=(), input_output_aliases=None,
        compiler_params=None, grid_spec_scalars=None):
    single = not isinstance(out_shape, (list, tuple))
    core_out_specs = [out_specs] if single else list(out_specs)
    core_out_shape = [out_shape] if single else list(out_shape)
    n_in, n_out, n_scr = len(in_specs), len(core_out_specs), len(scratch_shapes)
    n_pre = 0 if grid_spec_scalars is None else 1
    all_in, all_out, all_shape = list(in_specs), list(core_out_specs), list(core_out_shape)
    all_scr, aliases, extra_args, layout = list(scratch_shapes), dict(input_output_aliases or {}), [], []
    for job in jobs:
        n_job_out = job.n_inplace + len(job.fresh)
        layout.append((len(all_in), len(job.args), len(all_out), n_job_out, len(all_scr)))
        for a in range(job.n_inplace):
            aliases[n_pre + len(all_in) + a] = len(all_out) + a
        all_in += [ANY] * len(job.args)
        extra_args += job.args
        all_out += [ANY] * n_job_out
        all_shape += [jax.ShapeDtypeStruct(a.shape, a.dtype) for a in job.args[:job.n_inplace]] + job.fresh
        all_scr += [pltpu.SemaphoreType.DMA((job.nsem,)), pltpu.SemaphoreType.DMA((job.nsem,))]
    steps = 1
    for extent in grid:
        steps *= extent
    events = []
    for (i0, na, o0, no, s0), job in zip(layout, jobs):
        for frac, fn in job.phases:
            events.append((min(int(frac * steps), steps - 1), frac >= 1.0, len(events), fn, (i0, na, o0, no, s0)))
    events.sort(key=lambda e: e[:3])

    def wrapped(*refs):
        pre, refs = refs[:n_pre], refs[n_pre:]
        ins, outs, scr = refs[:len(all_in)], refs[len(all_in):len(all_in) + len(all_out)], refs[len(all_in) + len(all_out):]
        step = 0
        for axis, extent in enumerate(grid):
            step = step * extent + pl.program_id(axis)

        def emit(event):
            at, _, _, fn, (i0, na, o0, no, s0) = event
            run = lambda: fn(ins[i0:i0 + na], outs[o0:o0 + no], scr[s0], scr[s0 + 1])
            if steps == 1:
                run()
            else:
                pl.when(step == at)(run)

        for event in events:
            if not event[1]:
                emit(event)
        body(*pre, *ins[:n_in], *outs[:n_out], *scr[:n_scr])
        for event in events:
            if event[1]:
                emit(event)

    kwargs = dict(name=name, out_shape=all_shape, input_output_aliases=aliases)
    if compiler_params is not None:
        kwargs["compiler_params"] = compiler_params
    if grid_spec_scalars is not None:
        kwargs["grid_spec"] = pltpu.PrefetchScalarGridSpec(
            num_scalar_prefetch=1, grid=grid, in_specs=all_in, out_specs=all_out, scratch_shapes=all_scr)
    else:
        kwargs.update(in_specs=all_in, out_specs=all_out, scratch_shapes=all_scr)
        if grid:
            kwargs["grid"] = grid
    call = pl.pallas_call(wrapped, **kwargs)

    def run_call(*args):
        pre_args = [] if grid_spec_scalars is None else [grid_spec_scalars]
        res = list(call(*pre_args, *args, *extra_args))
        core = res[0] if single else res[:n_out]
        job_outs = [res[o0:o0 + no] for (_, _, o0, no, _) in layout]
        return core, job_outs

    return run_call


def _full(shape):
    nd = len(shape)
    return pl.BlockSpec(shape, lambda *_: (0,) * nd, pipeline_mode=pl.Buffered(1))


def _rows(ts, width, col=0):
    return pl.BlockSpec((ts, width), lambda i: (i, col))


def _sig(v):
    return jax.nn.sigmoid(v)


def _rms_stats(x):
    r = lax.rsqrt(jnp.mean(x * x, axis=-1, keepdims=True) + EPS)
    return r, x * r


def _rms_bwd(x, g, dh):
    r, xh = _rms_stats(x)
    dg = jnp.sum(dh * xh, axis=0, keepdims=True)
    dxh = dh * g
    dx = r * (dxh - xh * jnp.mean(dxh * xh, axis=-1, keepdims=True))
    return dx, dg


def _accumulate(ref, val, first):
    @pl.when(first)
    def _():
        ref[...] = val

    @pl.when(jnp.logical_not(first))
    def _():
        ref[...] += val


def _ffn_fwd(x, g, w13, w2, jobs=()):
    s_len = x.shape[0]
    ts = min(TOKEN_TILE, s_len)

    def body(x_ref, g_ref, w13_ref, w2_ref, xo_ref, h_ref, ab_ref):
        xv = x_ref[...]
        r, xh = _rms_stats(xv)
        h = (xh * g_ref[...]).astype(BF)
        h_ref[...] = h
        ab = jnp.dot(h, w13_ref[...], preferred_element_type=F32)
        ab_ref[...] = ab.astype(BF)
        a = ab[:, :FF]
        b = ab[:, FF:]
        s = (a * _sig(a) * b).astype(BF)
        xo_ref[...] = xv + 0.5 * jnp.dot(s, w2_ref[...], preferred_element_type=F32)

    return _pc(
        jobs, body, name="ffn_fwd", grid=(s_len // ts,),
        in_specs=[_rows(ts, D), _full((1, D)), _full((D, 2 * FF)), _full((FF, D))],
        out_specs=[_rows(ts, D), _rows(ts, D), _rows(ts, 2 * FF)],
        out_shape=[jax.ShapeDtypeStruct((s_len, D), F32), jax.ShapeDtypeStruct((s_len, D), BF),
                   jax.ShapeDtypeStruct((s_len, 2 * FF), BF)],
        compiler_params=_params(1),
    )(x, g, w13, w2)


def _ffn_bwd(dxo, x, g, ab, w13, w2, jobs=()):
    s_len = x.shape[0]
    ts = min(TOKEN_TILE, s_len)

    def body(dxo_ref, x_ref, g_ref, ab_ref, w13_ref, w2_ref, dxi_ref, dab_ref, s_ref, dy_ref, dg_ref):
        i = pl.program_id(0)
        dxo_v = dxo_ref[...]
        dy = (0.5 * dxo_v).astype(BF)
        dy_ref[...] = dy
        ds = lax.dot_general(dy, w2_ref[...], NT, preferred_element_type=F32)
        a = ab_ref[:, :FF].astype(F32)
        b = ab_ref[:, FF:].astype(F32)
        sg = _sig(a)
        sil = a * sg
        s_ref[...] = (sil * b).astype(BF)
        dab_ref[:, :FF] = (ds * b * (sg * (1.0 + a * (1.0 - sg)))).astype(BF)
        dab_ref[:, FF:] = (ds * sil).astype(BF)
        dh = lax.dot_general(dab_ref[...], w13_ref[...], NT, preferred_element_type=F32)
        dx, dg = _rms_bwd(x_ref[...], g_ref[...], dh)
        dxi_ref[...] = dxo_v + dx
        _accumulate(dg_ref, dg, i == 0)

    return _pc(
        jobs, body, name="ffn_bwd", grid=(s_len // ts,),
        in_specs=[_rows(ts, D), _rows(ts, D), _full((1, D)), _rows(ts, 2 * FF), _full((D, 2 * FF)), _full((FF, D))],
        out_specs=[_rows(ts, D), _rows(ts, 2 * FF), _rows(ts, FF), _rows(ts, D), pl.BlockSpec((1, D), lambda i: (0, 0))],
        out_shape=[jax.ShapeDtypeStruct((s_len, D), F32), jax.ShapeDtypeStruct((s_len, 2 * FF), BF),
                   jax.ShapeDtypeStruct((s_len, FF), BF), jax.ShapeDtypeStruct((s_len, D), BF),
                   jax.ShapeDtypeStruct((1, D), F32)],
        compiler_params=_params(1),
    )(dxo, x, g, ab, w13, w2)


def _wgrad(a, b, tk, tn, name, jobs=()):
    s_len, k = a.shape
    n = b.shape[1]

    def body(a_ref, b_ref, o_ref):
        o_ref[...] = lax.dot_general(a_ref[...], b_ref[...], TN, preferred_element_type=F32).astype(BF)

    return _pc(
        jobs, body, name=name, grid=(k // tk, n // tn),
        in_specs=[pl.BlockSpec((s_len, tk), lambda i, j: (0, i)), pl.BlockSpec((s_len, tn), lambda i, j: (0, j))],
        out_specs=pl.BlockSpec((tk, tn), lambda i, j: (i, j)),
        out_shape=jax.ShapeDtypeStruct((k, n), BF),
        compiler_params=_params(2),
    )(a, b)


def _wgrad_groups(a, b, ka, nb, name, jobs=()):
    s_len = a.shape[0]
    groups = a.shape[1] // ka

    def body(a_ref, b_ref, o_ref):
        o_ref[...] = lax.dot_general(a_ref[...], b_ref[...], TN, preferred_element_type=F32).astype(BF)

    return _pc(
        jobs, body, name=name, grid=(groups,),
        in_specs=[pl.BlockSpec((s_len, ka), lambda gi: (0, gi)), pl.BlockSpec((s_len, nb), lambda gi: (0, gi))],
        out_specs=pl.BlockSpec((ka, nb), lambda gi: (gi, 0)),
        out_shape=jax.ShapeDtypeStruct((groups * ka, nb), BF),
        compiler_params=_params(1),
    )(a, b)


def _mix_in(x, g, w_in, jobs=()):
    s_len = x.shape[0]
    ts = min(TOKEN_TILE, s_len)

    def body(x_ref, g_ref, w_ref, h_ref, p_ref):
        _, xh = _rms_stats(x_ref[...])
        h = (xh * g_ref[...]).astype(BF)
        h_ref[...] = h
        p_ref[...] = jnp.dot(h, w_ref[...], preferred_element_type=F32).astype(BF)

    return _pc(
        jobs, body, name="mix_in", grid=(s_len // ts,),
        in_specs=[_rows(ts, D), _full((1, D)), _full((D, INC))],
        out_specs=[_rows(ts, D), _rows(ts, INC)],
        out_shape=[jax.ShapeDtypeStruct((s_len, D), BF), jax.ShapeDtypeStruct((s_len, INC), BF)],
        compiler_params=_params(1),
    )(x, g, w_in)


def _shift(e, j):
    n = e.shape[0]
    j = j % n
    return e if j == 0 else pltpu.roll(e, j, 0)


def _ln_stats(z):
    mu = jnp.mean(z, axis=-1, keepdims=True)
    zc = z - mu
    rs = lax.rsqrt(jnp.mean(zc * zc, axis=-1, keepdims=True) + EPS)
    return rs, zc * rs


def _ln_bwd(rs, zn, dzn):
    return rs * (dzn - jnp.mean(dzn, axis=-1, keepdims=True) - zn * jnp.mean(dzn * zn, axis=-1, keepdims=True))


_GELU_C0 = 0.7978845608028654
_GELU_C1 = 0.044715


def _gelu(p):
    th = jnp.tanh(_GELU_C0 * (p + _GELU_C1 * p * p * p))
    return 0.5 * p * (1.0 + th), th


def _gelu_grad(p, th):
    return 0.5 * (1.0 + th) + 0.5 * p * (1.0 - th * th) * (_GELU_C0 * (1.0 + 3.0 * _GELU_C1 * p * p))


def _pool_diff(a, t, sign):
    outs = []
    for gi in range(NBR):
        win = 2 ** (gi + 1)
        ag = a[:, gi * GW:(gi + 1) * GW]
        cnt = jnp.clip(t + 1, 1, win).astype(F32)
        ws = ag if sign > 0 else ag / cnt
        for s in range(gi + 1):
            ws = ws + _shift(ws, sign * (2 ** s))
        outs.append((ws / cnt if sign > 0 else ws) - ag)
    return outs


def _sgu_mask():
    row = lax.broadcasted_iota(jnp.int32, (GW, GW), 0)
    col = lax.broadcasted_iota(jnp.int32, (GW, GW), 1)
    return (col // CHUNK) <= (row // CHUNK)


def _assemble(pe_ref, prev_ref, cur_ref, next_ref, i, last, ts):
    pe_ref[0:HALO, :] = jnp.where(i > 0, prev_ref[...], jnp.zeros_like(prev_ref))
    pe_ref[HALO:HALO + ts, :] = cur_ref[...]
    if next_ref is not None:
        pe_ref[HALO + ts:, :] = jnp.where(i < last, next_ref[...], jnp.zeros_like(next_ref))


def _halo_specs(ts, width, s_len, with_next):
    per = ts // HALO
    specs = [pl.BlockSpec((HALO, width), lambda i: (jnp.maximum(i * per - 1, 0), 0)),
             pl.BlockSpec((ts, width), lambda i: (i, 0))]
    if with_next:
        specs.append(pl.BlockSpec((HALO, width), lambda i: (jnp.minimum((i + 1) * per, s_len // HALO - 1), 0)))
    return specs


def _branch_weights_specs():
    return [_full((NBR, GW, GW)), _full((1, BW)), _full((SKP, BW)), _full((CKP, BW)), _full((1, BW)), _full((1, BW)),
            _full((1, BW)), _full((1, BW)), _full((NBR, GW, GW)), _full((GW, NBR))]


def _mix_branches_fwd(p, pool_w, pool_scale, sconv_w, cconv_w, cln_g, cln_b, sln_g, sln_b, sgu_w, sgu_bt, jobs=()):
    s_len = p.shape[0]
    ts = min(BRANCH_TILE, s_len)
    ext = HALO + ts

    def body(pp_ref, pc_ref, pw_ref, ps_ref, sw_ref, cw_ref, clg_ref, clb_ref, slg_ref, slb_ref, gw_ref, gb_ref,
             y_ref, pe_ref):
        i = pl.program_id(0)
        _assemble(pe_ref, pp_ref, pc_ref, None, i, 0, ts)
        t = i * ts - HALO + lax.broadcasted_iota(jnp.int32, (ext, 1), 0)

        dgs = _pool_diff(pe_ref[:, 0:BW].astype(F32), t, 1)
        for gi in range(NBR):
            e = jnp.dot(dgs[gi][HALO:].astype(BF), pw_ref[gi].astype(BF), preferred_element_type=F32)
            y_ref[:, gi * GW:(gi + 1) * GW] = (e * ps_ref[:, gi * GW:(gi + 1) * GW]).astype(BF)

        xin = pe_ref[:, BW:2 * BW].astype(F32)
        cg = pe_ref[:, 3 * BW:4 * BW].astype(F32)
        q = cg * xin
        cv = sw_ref[2:3, :] * q + sw_ref[1:2, :] * _shift(q, 1) + sw_ref[0:1, :] * _shift(q, 2)
        y_ref[:, BW:2 * BW] = (pe_ref[HALO:, 2 * BW:3 * BW].astype(F32) * cv[HALO:]).astype(BF)

        yg = pe_ref[:, 4 * BW:5 * BW].astype(F32) * _sig(pe_ref[:, 5 * BW:6 * BW].astype(F32))
        z = cw_ref[CK - 1:CK, :] * yg
        for j in range(1, CK):
            z = z + cw_ref[CK - 1 - j:CK - j, :] * _shift(yg, j)
        _, zn = _ln_stats(z[HALO:])
        nn = zn * clg_ref[...] + clb_ref[...]
        y_ref[:, 2 * BW:3 * BW] = (nn * _sig(nn)).astype(BF)

        u, _ = _gelu(pc_ref[:, 6 * BW:7 * BW].astype(F32))
        v, _ = _gelu(pc_ref[:, 7 * BW:8 * BW].astype(F32))
        _, vn = _ln_stats(v)
        vn = (vn * slg_ref[...] + slb_ref[...]).astype(BF)
        mask = _sgu_mask()
        for hd in range(NBR):
            wm = jnp.where(mask, gw_ref[hd], 0.0).astype(BF)
            for blk in range(ts // GW):
                rows = slice(blk * GW, (blk + 1) * GW)
                cols = slice(hd * GW, (hd + 1) * GW)
                zz = jnp.dot(wm, vn[rows, cols], preferred_element_type=F32) + gb_ref[:, hd:hd + 1]
                y_ref[rows, 3 * BW + hd * GW:3 * BW + (hd + 1) * GW] = (u[rows, cols] * zz).astype(BF)

    return _pc(
        jobs, body, name="mix_branches_fwd", grid=(s_len // ts,),
        in_specs=_halo_specs(ts, MIXC, s_len, False) + _branch_weights_specs(),
        out_specs=_rows(ts, NBR * BW),
        out_shape=jax.ShapeDtypeStruct((s_len, NBR * BW), BF),
        scratch_shapes=[pltpu.VMEM((ext, MIXC), BF)],
        compiler_params=_params(1),
    )(p, p, pool_w, pool_scale, sconv_w, cconv_w, cln_g, cln_b, sln_g, sln_b, sgu_w, sgu_bt)


def _mix_branches_bwd(p, dy, dp, pool_w, pool_scale, sconv_w, cconv_w, cln_g, cln_b, sln_g, sln_b, sgu_w, sgu_bt, jobs=()):
    s_len = p.shape[0]
    ts = min(BRANCH_TILE, s_len)
    ext = ts + 2 * HALO
    last = s_len // ts - 1
    tile = slice(HALO, HALO + ts)
    small_shapes = [(NBR, GW, GW), (1, BW), (SKP, BW), (CKP, BW), (1, BW), (1, BW), (1, BW), (1, BW), (NBR, GW, GW),
                    (NBR, GW, GW)]

    def body(pp_ref, pc_ref, pn_ref, dyp_ref, dyc_ref, dyn_ref, dpin_ref,
             pw_ref, ps_ref, sw_ref, cw_ref, clg_ref, clb_ref, slg_ref, slb_ref, gw_ref, gb_ref,
             dp_ref, dpw_ref, dps_ref, dsw_ref, dcw_ref, dclg_ref, dclb_ref, dslg_ref, dslb_ref, dgw_ref, dgb_ref,
             pe_ref, de_ref):
        del dyp_ref, dpin_ref
        i = pl.program_id(0)
        first = i == 0
        _assemble(pe_ref, pp_ref, pc_ref, pn_ref, i, last, ts)
        de_ref[0:HALO, :] = jnp.zeros((HALO, NBR * BW), BF)
        de_ref[HALO:HALO + ts, :] = dyc_ref[...]
        de_ref[HALO + ts:, :] = jnp.where(i < last, dyn_ref[...], jnp.zeros_like(dyn_ref))
        t = i * ts - HALO + lax.broadcasted_iota(jnp.int32, (ext, 1), 0)

        @pl.when(first)
        def _():
            dsw_ref[...] = jnp.zeros((SKP, BW), F32)
            dcw_ref[...] = jnp.zeros((CKP, BW), F32)

        dgs = _pool_diff(pe_ref[:, 0:BW].astype(F32), t, 1)
        dya = de_ref[:, 0:BW].astype(F32)
        dds = []
        for gi in range(NBR):
            cols = slice(gi * GW, (gi + 1) * GW)
            pw = pw_ref[gi].astype(BF)
            d_t = dgs[gi][tile].astype(BF)
            e = jnp.dot(d_t, pw, preferred_element_type=F32)
            _accumulate(dps_ref.at[:, cols], jnp.sum(dya[tile, cols] * e, axis=0, keepdims=True), first)
            de_g = (dya[:, cols] * ps_ref[:, cols]).astype(BF)
            _accumulate(dpw_ref.at[gi], lax.dot_general(d_t, de_g[tile], TN, preferred_element_type=F32), first)
            dds.append(lax.dot_general(de_g, pw, NT, preferred_element_type=F32))
        das = _pool_diff(jnp.concatenate(dds, axis=1), t, -1)
        for gi in range(NBR):
            dp_ref[:, gi * GW:(gi + 1) * GW] = das[gi][tile].astype(BF)

        xin = pe_ref[:, BW:2 * BW].astype(F32)
        bg = pe_ref[:, 2 * BW:3 * BW].astype(F32)
        cg = pe_ref[:, 3 * BW:4 * BW].astype(F32)
        q = cg * xin
        qs = [q, _shift(q, 1), _shift(q, 2)]
        cv = sw_ref[2:3, :] * qs[0] + sw_ref[1:2, :] * qs[1] + sw_ref[0:1, :] * qs[2]
        dyb = de_ref[:, BW:2 * BW].astype(F32)
        dcv = dyb * bg
        for j in range(SK):
            dsw_ref[SK - 1 - j:SK - j, :] += jnp.sum(dcv[tile] * qs[j][tile], axis=0, keepdims=True)
        dq = sw_ref[2:3, :] * dcv + sw_ref[1:2, :] * _shift(dcv, -1) + sw_ref[0:1, :] * _shift(dcv, -2)
        dp_ref[:, BW:2 * BW] = (dq * cg)[tile].astype(BF)
        dp_ref[:, 2 * BW:3 * BW] = (dyb * cv)[tile].astype(BF)
        dp_ref[:, 3 * BW:4 * BW] = (dq * xin)[tile].astype(BF)

        ca = pe_ref[:, 4 * BW:5 * BW].astype(F32)
        sb = _sig(pe_ref[:, 5 * BW:6 * BW].astype(F32))
        yg = ca * sb
        z = cw_ref[CK - 1:CK, :] * yg
        for j in range(1, CK):
            z = z + cw_ref[CK - 1 - j:CK - j, :] * _shift(yg, j)
        rs, zn = _ln_stats(z)
        nn = zn * clg_ref[...] + clb_ref[...]
        sn = _sig(nn)
        dn = de_ref[:, 2 * BW:3 * BW].astype(F32) * (sn * (1.0 + nn * (1.0 - sn)))
        _accumulate(dclg_ref, jnp.sum((dn * zn)[tile], axis=0, keepdims=True), first)
        _accumulate(dclb_ref, jnp.sum(dn[tile], axis=0, keepdims=True), first)
        dz = _ln_bwd(rs, zn, dn * clg_ref[...])
        dz = jnp.where(t >= i * ts, dz, 0.0)
        yg_t = yg[tile]
        dyg = cw_ref[CK - 1:CK, :] * dz
        dcw_ref[CK - 1:CK, :] += jnp.sum(dz[tile] * yg_t, axis=0, keepdims=True)
        for j in range(1, CK):
            dz_ahead = _shift(dz, -j)
            dyg = dyg + cw_ref[CK - 1 - j:CK - j, :] * dz_ahead
            dcw_ref[CK - 1 - j:CK - j, :] += jnp.sum(dz_ahead[tile] * yg_t, axis=0, keepdims=True)
        dp_ref[:, 4 * BW:5 * BW] = (dyg * sb)[tile].astype(BF)
        dp_ref[:, 5 * BW:6 * BW] = (dyg * ca * sb * (1.0 - sb))[tile].astype(BF)

        pu = pc_ref[:, 6 * BW:7 * BW].astype(F32)
        pv = pc_ref[:, 7 * BW:8 * BW].astype(F32)
        u, thu = _gelu(pu)
        v, thv = _gelu(pv)
        vrs, vn0 = _ln_stats(v)
        vn = (vn0 * slg_ref[...] + slb_ref[...]).astype(BF)
        dyd = dyc_ref[:, 3 * BW:4 * BW].astype(F32)
        dzz = dyd * u
        dzb = dzz.astype(BF)
        mask = _sgu_mask()
        dvn_cols = []
        for hd in range(NBR):
            cols = slice(hd * GW, (hd + 1) * GW)
            wm = jnp.where(mask, gw_ref[hd], 0.0).astype(BF)
            dwm = jnp.zeros((GW, GW), F32)
            dbs = jnp.zeros((GW, GW), F32)
            dvn_rows = []
            for blk in range(ts // GW):
                rows = slice(blk * GW, (blk + 1) * GW)
                zz = jnp.dot(wm, vn[rows, cols], preferred_element_type=F32) + gb_ref[:, hd:hd + 1]
                dp_ref[rows, 6 * BW + hd * GW:6 * BW + (hd + 1) * GW] = (
                    dyd[rows, cols] * zz * _gelu_grad(pu[rows, cols], thu[rows, cols])).astype(BF)
                dwm = dwm + lax.dot_general(dzb[rows, cols], vn[rows, cols], NT, preferred_element_type=F32)
                dbs = dbs + dzz[rows, cols]
                dvn_rows.append(lax.dot_general(wm, dzb[rows, cols], TN, preferred_element_type=F32))
            _accumulate(dgw_ref.at[hd], jnp.where(mask, dwm, 0.0), first)
            _accumulate(dgb_ref.at[hd], dbs, first)
            dvn_cols.append(jnp.concatenate(dvn_rows, axis=0))
        dvn = jnp.concatenate(dvn_cols, axis=1)
        _accumulate(dslg_ref, jnp.sum(dvn * vn0, axis=0, keepdims=True), first)
        _accumulate(dslb_ref, jnp.sum(dvn, axis=0, keepdims=True), first)
        dv = _ln_bwd(vrs, vn0, dvn * slg_ref[...])
        dp_ref[:, 7 * BW:8 * BW] = (dv * _gelu_grad(pv, thv)).astype(BF)

    const = lambda shp: pl.BlockSpec(shp, lambda i: (0,) * len(shp))
    return _pc(
        jobs, body, name="mix_branches_bwd", grid=(s_len // ts,),
        in_specs=(_halo_specs(ts, MIXC, s_len, True) + _halo_specs(ts, NBR * BW, s_len, True)
                  + [pl.BlockSpec(memory_space=pl.ANY)] + _branch_weights_specs()),
        out_specs=[pl.BlockSpec((ts, MIXC), lambda i: (i, 0))] + [const(s) for s in small_shapes],
        out_shape=[jax.ShapeDtypeStruct((s_len, INC), BF)] + [jax.ShapeDtypeStruct(s, F32) for s in small_shapes],
        scratch_shapes=[pltpu.VMEM((ext, MIXC), BF), pltpu.VMEM((ext, NBR * BW), BF)],
        input_output_aliases={6: 0},
        compiler_params=_params(1),
    )(p, p, p, dy, dy, dy, dp, pool_w, pool_scale, sconv_w, cconv_w, cln_g, cln_b, sln_g, sln_b, sgu_w, sgu_bt)


def _mix_out_fwd(x, y, p, w_up, w_out, jobs=()):
    s_len = x.shape[0]
    ts = min(TOKEN_TILE, s_len)

    def body(x_ref, y_ref, pg_ref, wu_ref, wo_ref, xo_ref, m_ref, up_ref):
        m = jnp.zeros((ts, D), F32)
        for gi in range(NBR):
            up = jnp.dot(y_ref[:, gi * BW:(gi + 1) * BW], wu_ref[gi], preferred_element_type=F32)
            up_ref[:, gi * D:(gi + 1) * D] = up.astype(BF)
            m = m + _sig(pg_ref[:, gi * D:(gi + 1) * D].astype(F32)) * up
        mb = m.astype(BF)
        m_ref[...] = mb
        xo_ref[...] = x_ref[...] + jnp.dot(mb, wo_ref[...], preferred_element_type=F32)

    return _pc(
        jobs, body, name="mix_out_fwd", grid=(s_len // ts,),
        in_specs=[_rows(ts, D), _rows(ts, NBR * BW), _rows(ts, NBR * D, 1), _full((NBR, BW, D)), _full((D, D))],
        out_specs=[_rows(ts, D), _rows(ts, D), _rows(ts, NBR * D)],
        out_shape=[jax.ShapeDtypeStruct((s_len, D), F32), jax.ShapeDtypeStruct((s_len, D), BF),
                   jax.ShapeDtypeStruct((s_len, NBR * D), BF)],
        compiler_params=_params(1),
    )(x, y, p, w_up, w_out)


def _mix_out_bwd(dxo, up, p, w_up, w_out, jobs=()):
    s_len = dxo.shape[0]
    ts = min(TOKEN_TILE, s_len)

    def body(dxo_ref, up_ref, pg_ref, wu_ref, wo_ref, dy_ref, dp_ref, dup_ref, dxb_ref):
        dxb = dxo_ref[...].astype(BF)
        dxb_ref[...] = dxb
        dm = lax.dot_general(dxb, wo_ref[...], NT, preferred_element_type=F32)
        for gi in range(NBR):
            cols = slice(gi * D, (gi + 1) * D)
            gate = _sig(pg_ref[:, cols].astype(F32))
            dp_ref[:, cols] = (dm * up_ref[:, cols].astype(F32) * gate * (1.0 - gate)).astype(BF)
            dup = (dm * gate).astype(BF)
            dup_ref[:, cols] = dup
            dy_ref[:, gi * BW:(gi + 1) * BW] = lax.dot_general(
                dup, wu_ref[gi], NT, preferred_element_type=F32).astype(BF)

    return _pc(
        jobs, body, name="mix_out_bwd", grid=(s_len // ts,),
        in_specs=[_rows(ts, D), _rows(ts, NBR * D), _rows(ts, NBR * D, 1), _full((NBR, BW, D)), _full((D, D))],
        out_specs=[_rows(ts, NBR * BW), _rows(ts, NBR * D, 1), _rows(ts, NBR * D), _rows(ts, D)],
        out_shape=[jax.ShapeDtypeStruct((s_len, NBR * BW), BF), jax.ShapeDtypeStruct((s_len, INC), BF),
                   jax.ShapeDtypeStruct((s_len, NBR * D), BF), jax.ShapeDtypeStruct((s_len, D), BF)],
        compiler_params=_params(1),
    )(dxo, up, p, w_up, w_out)


def _mix_in_bwd(dp, w_in, x, g, dxo, jobs=()):
    s_len = x.shape[0]
    ts = min(TOKEN_TILE, s_len)

    def body(dp_ref, w_ref, x_ref, g_ref, dxo_ref, dxi_ref, dg_ref):
        i = pl.program_id(0)
        dh = lax.dot_general(dp_ref[...], w_ref[...], NT, preferred_element_type=F32)
        dx, dg = _rms_bwd(x_ref[...], g_ref[...], dh)
        dxi_ref[...] = dxo_ref[...] + dx
        _accumulate(dg_ref, dg, i == 0)

    return _pc(
        jobs, body, name="mix_in_bwd", grid=(s_len // ts,),
        in_specs=[_rows(ts, INC), _full((D, INC)), _rows(ts, D), _full((1, D)), _rows(ts, D)],
        out_specs=[_rows(ts, D), pl.BlockSpec((1, D), lambda i: (0, 0))],
        out_shape=[jax.ShapeDtypeStruct((s_len, D), F32), jax.ShapeDtypeStruct((1, D), F32)],
        compiler_params=_params(1),
    )(dp, w_in, x, g, dxo)


def _loss_head(x, g, target, jobs=()):
    s_len = x.shape[0]
    ts = min(512, s_len)

    def body(x_ref, g_ref, t_ref, dx_ref, dg_ref, loss_ref):
        i = pl.program_id(0)
        xv = x_ref[...]
        _, xh = _rms_stats(xv)
        err = xh * g_ref[...] - t_ref[...]
        part = 0.5 * jnp.sum(jnp.mean(err * err, axis=-1, keepdims=True), axis=0, keepdims=True)
        dx, dg = _rms_bwd(xv, g_ref[...], err * (1.0 / D))
        dx_ref[...] = dx
        _accumulate(dg_ref, dg, i == 0)
        _accumulate(loss_ref, jnp.broadcast_to(part, (1, GW)), i == 0)

    return _pc(
        jobs, body, name="loss_head", grid=(s_len // ts,),
        in_specs=[_rows(ts, D), _full((1, D)), _rows(ts, D)],
        out_specs=[_rows(ts, D), pl.BlockSpec((1, D), lambda i: (0, 0)), pl.BlockSpec((1, GW), lambda i: (0, 0))],
        out_shape=[jax.ShapeDtypeStruct((s_len, D), F32), jax.ShapeDtypeStruct((1, D), F32),
                   jax.ShapeDtypeStruct((1, GW), F32)],
        compiler_params=_params(1),
    )(x, g, target)


def _row_tile(rows, cols, budget=1 << 18):
    tr = rows
    while tr * cols > budget and tr % 16 == 0:
        tr //= 2
    return tr


def _elementwise(fn, name, ins, out_dtypes):
    rows, cols = ins[0].shape
    tr = _row_tile(rows, cols)
    n_in = len(ins)

    def body(*refs):
        res = fn(*[r[...] for r in refs[:n_in]])
        for o_ref, val in zip(refs[n_in:], res):
            o_ref[...] = val.astype(o_ref.dtype)

    outs = pl.pallas_call(
        body, name=name, grid=(rows // tr,),
        in_specs=[_rows(tr, cols)] * n_in, out_specs=[_rows(tr, cols)] * len(out_dtypes),
        out_shape=[jax.ShapeDtypeStruct((rows, cols), dt) for dt in out_dtypes],
        compiler_params=_params(1),
    )(*ins)
    return outs


def _tiled(fn, name, grid, in_specs, out_specs, out_shape, args, scalars=None, carried=0):
    n_in = len(in_specs) - carried
    n_pre = 0 if scalars is None else 1

    def body(*refs):
        refs = refs[n_pre:]
        res = fn(*[r[...] for r in refs[:n_in]])
        for o_ref, val in zip(refs[n_in + carried:], res):
            o_ref[...] = val.astype(o_ref.dtype)

    aliases = {n_pre + n_in + k: k for k in range(carried)}
    if scalars is None:
        return pl.pallas_call(body, name=name, grid=grid, in_specs=in_specs, out_specs=out_specs, out_shape=out_shape,
                              input_output_aliases=aliases, compiler_params=_params(len(grid)))(*args)
    spec = pltpu.PrefetchScalarGridSpec(num_scalar_prefetch=1, grid=grid, in_specs=in_specs, out_specs=out_specs)
    return pl.pallas_call(body, name=name, grid_spec=spec, out_shape=out_shape, input_output_aliases=aliases,
                          compiler_params=_params(len(grid)))(scalars, *args)


def _cast_into(shard, layer, col, dtype, sc):
    ks, ns = shard.shape[1:]
    tr = _row_tile(ks, ns)
    full = (ks, ns * NCHIP) if col else (ks * NCHIP, ns)
    out_idx = (lambda i, s: (i, s[1])) if col else (lambda i, s: (s[1] * (ks // tr) + i, 0))
    return _tiled(lambda v: (v,), "cast_into", (ks // tr,), [pl.BlockSpec((None, tr, ns), lambda i, s: (layer, i, 0))],
                  [pl.BlockSpec((tr, ns), out_idx)], [jax.ShapeDtypeStruct(full, dtype)], [shard], sc)[0]


def _pair_sum(g, got, col, sc):
    hk, hn = got.shape
    tr = _row_tile(hk, hn)
    g_idx = (lambda i, s: (s[0] * (hk // tr) + i, 0)) if col else (lambda i, s: (i, s[0]))
    plain = pl.BlockSpec((tr, hn), lambda i, s: (i, 0))
    return _tiled(lambda a, b: (a.astype(F32) + b.astype(F32),), "pair_sum", (hk // tr,),
                  [pl.BlockSpec((tr, hn), g_idx), plain], [plain], [jax.ShapeDtypeStruct((hk, hn), BF)], [g, got], sc)[0]


def _chip_sum(ph, got, col, layer, depth, sc, carry):
    qk, qn = got.shape[1:]
    tr = _row_tile(qk, qn)
    ph_idx = (lambda i, s: (i, s[1])) if col else (lambda i, s: (s[1] * (qk // tr) + i, 0))
    out_shape = (depth, 2 * qk, qn) if col else (depth, qk, 2 * qn)
    out_idx = (lambda i, s: (layer, s[0] * (qk // tr) + i, 0)) if col else (lambda i, s: (layer, i, s[0]))
    in_specs = [pl.BlockSpec((tr, qn), ph_idx)] + [pl.BlockSpec((None, tr, qn), lambda i, s, j=j: (j, i, 0)) for j in range(3)]
    args = [ph, got, got, got]
    if carry is not None:
        in_specs.append(ANY)
        args.append(carry)
    return _tiled(lambda a, b, c_, d_: (a.astype(F32) + b.astype(F32) + c_.astype(F32) + d_.astype(F32),), "chip_sum",
                  (qk // tr,), in_specs, [pl.BlockSpec((None, tr, qn), out_idx)],
                  [jax.ShapeDtypeStruct(out_shape, F32)], args, sc, carried=0 if carry is None else 1)[0]


def _adamw_math(w, g, m, v):
    m = ADAM_B1 * m + (1.0 - ADAM_B1) * g
    v = ADAM_B2 * v + (1.0 - ADAM_B2) * (g * g)
    m_hat = m / (1.0 - ADAM_B1 ** ADAM_STEP)
    v_hat = v / (1.0 - ADAM_B2 ** ADAM_STEP)
    delta = -ADAM_LR * (m_hat / (jnp.sqrt(v_hat) + ADAM_EPS) + ADAM_WD * w)
    return delta, m, v


def _adamw_layer(w, g, m, v, layer, carry):
    k, n = w.shape[1:]
    tr = _row_tile(k, n)
    blk = pl.BlockSpec((None, tr, n), lambda i: (layer, i, 0))
    carry = list(carry or [])
    return _tiled(_adamw_math, "adamw_layer", (k // tr,), [blk] * 4 + [ANY] * len(carry), [blk] * 3,
                  [jax.ShapeDtypeStruct(w.shape, F32)] * 3, [w, g, m, v] + carry, None, carried=len(carry))


def _adamw(w, g, m, v):
    shape = w.shape
    two_d = lambda a: a.reshape(-1, shape[-1])
    outs = _elementwise(_adamw_math, "adamw", [two_d(w), two_d(g), two_d(m), two_d(v)], [F32, F32, F32])
    return [o.reshape(shape) for o in outs]


ANY = pl.BlockSpec(memory_space=pl.ANY)


def _place():
    x, y, c = lax.axis_index("x"), lax.axis_index("y"), lax.axis_index("c")
    chips = [(1 - x, y), (x, 1 - y), (1 - x, 1 - y)]
    return x, y, c, chips


def _cols(ref, start, size):
    idx = (slice(None),) * (len(ref.shape) - 1) + (pl.ds(pl.multiple_of(start, GW), size),)
    return ref.at[idx]


def _rows_of(ref, start, size):
    nd = len(ref.shape)
    idx = (slice(None),) * (nd - 2) + (pl.ds(pl.multiple_of(start, 16), size), slice(None))
    return ref.at[idx]


def _region(ref, col_sharded, chip, half):
    k, n = ref.shape
    align = 16 if ref.dtype == BF else 8
    if col_sharded:
        return ref.at[pl.ds(pl.multiple_of(half * (k // 2), align), k // 2),
                      pl.ds(pl.multiple_of(chip * (n // NCHIP), GW), n // NCHIP)]
    rows = k // (2 * NCHIP)
    return ref.at[pl.ds(pl.multiple_of((2 * chip + half) * rows, align), rows), :]


def _job_gather(bufs, col_sharded, handoff):
    n = len(bufs)

    def copies(outs, send_sems, recv_sems, stage):
        x, y, c, chips = _place()
        sends, lands = [], []
        for k in range(n):
            for j, chip in enumerate(chips):
                theirs = 2 * chip[0] + chip[1]
                if stage == 0:
                    src, to = _region(outs[k], col_sharded[k], 2 * x + y, c), (*chip, c)
                    land = _region(outs[k], col_sharded[k], theirs, c)
                else:
                    src, to = _region(outs[k], col_sharded[k], theirs, c), (x, y, 1 - c)
                    land = _region(outs[k], col_sharded[k], theirs, 1 - c)
                sem = 3 * n * stage + 3 * k + j
                sems = dict(send_sem=send_sems.at[sem], recv_sem=recv_sems.at[sem], device_id=to, device_id_type=MESH)
                sends.append(pltpu.make_async_remote_copy(src_ref=src, dst_ref=src, **sems))
                lands.append(pltpu.make_async_remote_copy(src_ref=land, dst_ref=land, **sems))
        return sends, lands

    def start(ins, outs, send_sems, recv_sems):
        for cp in copies(outs, send_sems, recv_sems, 0)[0]:
            cp.start()

    def hand_on(ins, outs, send_sems, recv_sems):
        for cp in copies(outs, send_sems, recv_sems, 0)[1]:
            cp.wait_recv()
        for cp in copies(outs, send_sems, recv_sems, 1)[0]:
            cp.start()

    def finish(ins, outs, send_sems, recv_sems):
        sends, lands = copies(outs, send_sems, recv_sems, 1)
        for cp in lands:
            cp.wait_recv()
        for cp in copies(outs, send_sems, recv_sems, 0)[0] + sends:
            cp.wait_send()

    return _Job(bufs, n, [], 6 * n, [(0.0, start), (handoff, hand_on), (1.0, finish)])


def _half(ref, col_sharded, c):
    k, n = ref.shape[-2:]
    return _rows_of(ref, c * (k // 2), k // 2) if col_sharded else _cols(ref, c * (n // 2), n // 2)


def _quarter(ref, col_sharded, j):
    k, n = ref.shape[-2:]
    return _cols(ref, j * (n // NCHIP), n // NCHIP) if col_sharded else _rows_of(ref, j * (k // NCHIP), k // NCHIP)


def _job_pair(grads, col_sharded):
    n = len(grads)

    def half_shape(g, col):
        return (g.shape[0] // 2, g.shape[1]) if col else (g.shape[0], g.shape[1] // 2)

    def copies(ins, got, send_sems, recv_sems):
        x, y, c, _ = _place()
        return [pltpu.make_async_remote_copy(
            src_ref=_half(ins[k], col_sharded[k], 1 - c), dst_ref=got[k], send_sem=send_sems.at[k],
            recv_sem=recv_sems.at[k], device_id=(x, y, 1 - c), device_id_type=MESH) for k in range(n)]

    def start(*refs):
        for cp in copies(*refs):
            cp.start()

    def finish(*refs):
        for cp in copies(*refs):
            cp.wait()

    fresh = [jax.ShapeDtypeStruct(half_shape(g, col), g.dtype) for g, col in zip(grads, col_sharded)]
    return _Job(grads, 0, fresh, n, [(0.0, start), (1.0, finish)])


def _job_chip(halves, col_sharded):
    n = len(halves)

    def quarter_shape(h, col):
        return (3, h.shape[0], h.shape[1] // NCHIP) if col else (3, h.shape[0] // NCHIP, h.shape[1])

    def copies(ins, got, send_sems, recv_sems):
        x, y, c, chips = _place()
        return [pltpu.make_async_remote_copy(
            src_ref=_quarter(ins[k], col_sharded[k], 2 * chip[0] + chip[1]), dst_ref=got[k].at[j],
            send_sem=send_sems.at[3 * k + j], recv_sem=recv_sems.at[3 * k + j], device_id=(*chip, c), device_id_type=MESH)
            for k in range(n) for j, chip in enumerate(chips)]

    def start(*refs):
        for cp in copies(*refs):
            cp.start()

    def finish(*refs):
        for cp in copies(*refs):
            cp.wait()

    fresh = [jax.ShapeDtypeStruct(quarter_shape(h, col), h.dtype) for h, col in zip(halves, col_sharded)]
    return _Job(halves, 0, fresh, 3 * n, [(0.0, start), (1.0, finish)])


def _job_sibling(shards, col_sharded, layers):
    n = len(shards)

    def copies(outs, send_sems, recv_sems):
        x, y, c, _ = _place()
        sends, lands = [], []
        for k in range(n):
            sems = dict(send_sem=send_sems.at[k], recv_sem=recv_sems.at[k], device_id=(x, y, 1 - c), device_id_type=MESH)
            mine = _half(outs[k].at[layers[k]], col_sharded[k], c)
            theirs = _half(outs[k].at[layers[k]], col_sharded[k], 1 - c)
            sends.append(pltpu.make_async_remote_copy(src_ref=mine, dst_ref=mine, **sems))
            lands.append(pltpu.make_async_remote_copy(src_ref=theirs, dst_ref=theirs, **sems))
        return sends, lands

    def start(ins, outs, send_sems, recv_sems):
        for cp in copies(outs, send_sems, recv_sems)[0]:
            cp.start()

    def finish(ins, outs, send_sems, recv_sems):
        sends, lands = copies(outs, send_sems, recv_sems)
        for cp in lands:
            cp.wait_recv()
        for cp in sends:
            cp.wait_send()

    return _Job(shards, n, [], n, [(0.0, start), (1.0, finish)])


def _standalone(jobs, name):
    return _pc(jobs, lambda: None, name=name, in_specs=[], out_specs=[], out_shape=[])()[1]


def _all_reduce_small(buf):
    rows = buf.shape[0]
    per = rows // NDEV
    flips = [(fx, fy, fc) for fx in (0, 1) for fy in (0, 1) for fc in (0, 1)][1:]

    def body(in_ref, out_ref, got_ref, send_sems, recv_sems):
        x, y, c, _ = _place()
        me = 4 * x + 2 * y + c

        def peer(f):
            return tuple(1 - pos if flip else pos for pos, flip in zip((x, y, c), f))

        def block(ref, dev):
            return ref.at[pl.ds(pl.multiple_of(dev * per, 8), per), :]

        scatter = []
        for k, f in enumerate(flips):
            px, py, pc = peer(f)
            scatter.append(pltpu.make_async_remote_copy(
                src_ref=block(in_ref, 4 * px + 2 * py + pc), dst_ref=got_ref.at[k], send_sem=send_sems.at[k],
                recv_sem=recv_sems.at[k], device_id=(px, py, pc), device_id_type=MESH))
        for cp in scatter:
            cp.start()
        for cp in scatter:
            cp.wait()
        total = block(in_ref, me)[...]
        for k in range(len(flips)):
            total = total + got_ref[k]
        block(out_ref, me)[...] = total
        share = []
        for k, f in enumerate(flips):
            share.append(pltpu.make_async_remote_copy(
                src_ref=block(out_ref, me), dst_ref=block(out_ref, me), send_sem=send_sems.at[7 + k],
                recv_sem=recv_sems.at[7 + k], device_id=peer(f), device_id_type=MESH))
        for cp in share:
            cp.start()
        for k, f in enumerate(flips):
            share[k].wait_send()
            px, py, pc = peer(f)
            theirs = block(out_ref, 4 * px + 2 * py + pc)
            pltpu.make_async_remote_copy(
                src_ref=theirs, dst_ref=theirs, send_sem=send_sems.at[7 + k], recv_sem=recv_sems.at[7 + k],
                device_id=(px, py, pc), device_id_type=MESH).wait_recv()

    vmem = pl.BlockSpec(memory_space=pltpu.VMEM)
    return pl.pallas_call(
        body, name="all_reduce_small", in_specs=[vmem], out_specs=vmem,
        out_shape=jax.ShapeDtypeStruct((rows, GW), F32),
        scratch_shapes=[pltpu.VMEM((NDEV - 1, per, GW), F32), pltpu.SemaphoreType.DMA((14,)),
                        pltpu.SemaphoreType.DMA((14,))],
    )(buf)


BIG = ("ffn1_w13", "ffn1_w2", "w_in", "w_up", "w_out", "ffn2_w13", "ffn2_w2")
BIG_COL_SHARDED = (True, False, True, True, False, True, False)
SMALL = ("ffn1_norm", "mix_norm", "pool_w", "pool_scale", "sconv_w", "cconv_w", "cconv_ln_g", "cconv_ln_b",
         "sgu_ln_g", "sgu_ln_b", "sgu_w", "sgu_b", "ffn2_norm", "final_norm")
WEIGHTS = ("ffn1_norm", "ffn1_w13", "ffn1_w2", "mix_norm", "w_in", "pool_w", "pool_scale", "sconv_w", "cconv_w",
           "cconv_ln_g", "cconv_ln_b", "sgu_ln_g", "sgu_ln_b", "sgu_w", "sgu_b", "w_up", "w_out", "ffn2_norm",
           "ffn2_w13", "ffn2_w2", "final_norm")


def _pad_rows(a, rows):
    return jnp.pad(a, ((0, 0), (0, rows - a.shape[1]), (0, 0)))


def kernel(x, ffn1_norm, ffn1_w13, ffn1_w2, mix_norm, w_in, pool_w, pool_scale, sconv_w, cconv_w, cconv_ln_g, cconv_ln_b, sgu_ln_g, sgu_ln_b, sgu_w, sgu_b, w_up, w_out, ffn2_norm, ffn2_w13, ffn2_w2, final_norm, loss_target, m_ffn1_norm, m_ffn1_w13, m_ffn1_w2, m_mix_norm, m_w_in, m_pool_w, m_pool_scale, m_sconv_w, m_cconv_w, m_cconv_ln_g, m_cconv_ln_b, m_sgu_ln_g, m_sgu_ln_b, m_sgu_w, m_sgu_b, m_w_up, m_w_out, m_ffn2_norm, m_ffn2_w13, m_ffn2_w2, m_final_norm, v_ffn1_norm, v_ffn1_w13, v_ffn1_w2, v_mix_norm, v_w_in, v_pool_w, v_pool_scale, v_sconv_w, v_cconv_w, v_cconv_ln_g, v_cconv_ln_b, v_sgu_ln_g, v_sgu_ln_b, v_sgu_w, v_sgu_b, v_w_up, v_w_out, v_ffn2_norm, v_ffn2_w13, v_ffn2_w2, v_final_norm):
    args = dict(locals())
    w = {nm: args[nm] for nm in WEIGHTS}
    m = {nm: args["m_" + nm] for nm in WEIGHTS}
    v = {nm: args["v_" + nm] for nm in WEIGHTS}
    depth = ffn1_w13.shape[0]
    chip = 2 * lax.axis_index("x") + lax.axis_index("y")

    sc = jnp.stack([lax.axis_index("c"), chip]).astype(jnp.int32)
    col_of = dict(zip(BIG, BIG_COL_SHARDED), sconv_w=True, cconv_w=True)
    sources = {nm: (w[nm], BF) for nm in BIG}
    sources["w_up"] = (w_up.reshape(depth, NBR * BW, w_up.shape[-1]), BF)
    sources["sconv_w"] = (_pad_rows(sconv_w, 2 * SKP), F32)
    sources["cconv_w"] = (_pad_rows(cconv_w, 2 * CKP), F32)
    full = [{nm: _cast_into(src, l, col_of[nm], dt, sc) for nm, (src, dt) in sources.items()} for l in range(depth)]

    def gather_job(l, names, handoff):
        return _job_gather([full[l][nm] for nm in names], [col_of[nm] for nm in names], handoff)

    def gather_with(l, names, handoff, call):
        res, job_outs = call([gather_job(l, names, handoff)] if l < depth else [])
        if l < depth:
            full[l].update(zip(names, job_outs[0]))
        return res

    first_group = ("ffn1_w13", "ffn1_w2")
    full[0].update(zip(first_group, _standalone([gather_job(0, first_group, 0.0)], "gather_first")[0]))

    xs = x[0]
    row = lambda a: a.reshape(1, -1)
    saved = []
    for l in range(depth):
        g1, gm, g2 = row(ffn1_norm[l]), row(mix_norm[l]), row(ffn2_norm[l])
        x1, h1, ab1 = gather_with(l, ("w_in",), 0.85, lambda jobs: _ffn_fwd(
            xs, g1, full[l]["ffn1_w13"], full[l]["ffn1_w2"], jobs))
        hm, p = gather_with(l, ("w_up", "w_out", "sconv_w", "cconv_w", "ffn2_w2"), 0.75, lambda jobs: _mix_in(
            x1, gm, full[l]["w_in"], jobs))
        branch = (pool_w[l], row(pool_scale[l]), full[l]["sconv_w"][:SKP], full[l]["cconv_w"][:CKP], row(cconv_ln_g[l]),
                  row(cconv_ln_b[l]), row(sgu_ln_g[l]), row(sgu_ln_b[l]), sgu_w[l], sgu_b[l].T)
        y = gather_with(l, ("ffn2_w13",), 0.85, lambda jobs: _mix_branches_fwd(p, *branch, jobs=jobs))
        w_up_l = full[l]["w_up"].reshape(NBR, BW, D)
        x2, merged, up = gather_with(l + 1, ("ffn1_w2",), 0.7, lambda jobs: _mix_out_fwd(
            x1, y, p, w_up_l, full[l]["w_out"], jobs))
        x3, h2, ab2 = gather_with(l + 1, ("ffn1_w13",), 0.75, lambda jobs: _ffn_fwd(
            x2, g2, full[l]["ffn2_w13"], full[l]["ffn2_w2"], jobs))
        lw = dict(g1=g1, gm=gm, g2=g2, w13a=full[l]["ffn1_w13"], w2a=full[l]["ffn1_w2"], w13b=full[l]["ffn2_w13"],
                  w2b=full[l]["ffn2_w2"], w_in=full[l]["w_in"], w_up=w_up_l, w_out=full[l]["w_out"], branch=branch)
        saved.append(dict(lw=lw, x0=xs, x1=x1, x2=x2, h1=h1, ab1=ab1, hm=hm, p=p, y=y, merged=merged, up=up, h2=h2,
                          ab2=ab2))
        xs = x3

    (dx, d_final, loss_part), _ = _loss_head(xs, row(final_norm), loss_target[0])
    loss = lax.psum(loss_part[0, 0], ("x", "y", "c"))

    ici_us = dict(ffn1_w13=64, ffn1_w2=32, w_in=93, w_up=23, w_out=12, ffn2_w13=64, ffn2_w2=32)
    parts, pair_sums, reduced, big_updates, pending = {}, {}, {}, {}, []

    def take_jobs(budget_us):
        chosen = []
        for task in list(pending):
            kind, (nm, _) = task
            if kind == "chip":
                if ici_us[nm] > budget_us:
                    continue
                budget_us -= ici_us[nm]
            if kind == "sib" and any(k == "sib" and key[0] == nm for k, key in chosen):
                continue
            chosen.append(task)
            pending.remove(task)
        groups, jobs = [], []
        for kind in ("pair", "sib", "chip"):
            keys = [key for k, key in chosen if k == kind]
            if not keys:
                continue
            cols = [col_of[nm] for nm, _ in keys]
            groups.append((kind, keys))
            if kind == "pair":
                jobs.append(_job_pair([parts[key] for key in keys], cols))
            elif kind == "chip":
                jobs.append(_job_chip([pair_sums[key] for key in keys], cols))
            else:
                jobs.append(_job_sibling([reduced[nm] for nm, _ in keys], cols, [layer for _, layer in keys]))
        return groups, jobs

    def settle(groups, job_outs):
        for (kind, keys), outs in zip(groups, job_outs):
            for key, out in zip(keys, outs):
                nm, layer = key
                if kind == "pair":
                    pair_sums[key] = _pair_sum(parts[key], out, col_of[nm], sc)
                    pending.append(("chip", key))
                elif kind == "chip":
                    assert not any(k == "sib" and other[0] == nm for k, other in pending)
                    reduced[nm] = _chip_sum(pair_sums[key], out, col_of[nm], layer, depth, sc, reduced.get(nm))
                    pending.append(("sib", key))
                else:
                    reduced[nm] = out
                    as3 = lambda a: a.reshape(out.shape)
                    big_updates[nm] = _adamw_layer(as3(w[nm]), out, as3(m[nm]), as3(v[nm]), layer, big_updates.get(nm))

    def run(budget_us, call):
        groups, jobs = take_jobs(budget_us)
        res, job_outs = call(jobs)
        settle(groups, job_outs)
        return res

    def wgrad_done(key, partial):
        parts[key] = partial
        pending.append(("pair", key))

    small_parts = {nm: [None] * depth for nm in SMALL if nm != "final_norm"}
    for l in reversed(range(depth)):
        sv = saved[l]
        lw = sv["lw"]
        dx, dab, s_act, dyh, dg2 = run(108, lambda jobs: _ffn_bwd(
            dx, sv["x2"], lw["g2"], sv["ab2"], lw["w13b"], lw["w2b"], jobs))
        wgrad_done(("ffn2_w13", l), run(58, lambda jobs: _wgrad(sv["h2"], dab, D, 512, "wgrad_w13", jobs)))
        wgrad_done(("ffn2_w2", l), run(33, lambda jobs: _wgrad(s_act, dyh, 256, D, "wgrad_w2", jobs)))
        small_parts["ffn2_norm"][l] = dg2

        dy, dp, dup, dxb = run(70, lambda jobs: _mix_out_bwd(dx, sv["up"], sv["p"], lw["w_up"], lw["w_out"], jobs))
        wgrad_done(("w_out", l), run(16, lambda jobs: _wgrad(sv["merged"], dxb, D, 512, "wgrad_w_out", jobs)))
        wgrad_done(("w_up", l), run(25, lambda jobs: _wgrad_groups(sv["y"], dup, BW, D, "wgrad_w_up", jobs)))
        (dp, d_pool_w, d_pool_scale, d_sconv, d_cconv, d_clg, d_clb, d_slg, d_slb, d_sgu_w, d_sgu_b) = run(
            214, lambda jobs: _mix_branches_bwd(sv["p"], dy, dp, *lw["branch"], jobs=jobs))
        wgrad_done(("w_in", l), run(80, lambda jobs: _wgrad(sv["hm"], dp, D, 512, "wgrad_w_in", jobs)))
        dx, dgm = run(91, lambda jobs: _mix_in_bwd(dp, lw["w_in"], sv["x1"], lw["gm"], dx, jobs))
        small_parts["mix_norm"][l] = dgm
        small_parts["pool_w"][l] = d_pool_w
        small_parts["pool_scale"][l] = d_pool_scale
        small_parts["sconv_w"][l] = d_sconv[:SK]
        small_parts["cconv_w"][l] = d_cconv[:CK]
        small_parts["cconv_ln_g"][l] = d_clg
        small_parts["cconv_ln_b"][l] = d_clb
        small_parts["sgu_ln_g"][l] = d_slg
        small_parts["sgu_ln_b"][l] = d_slb
        small_parts["sgu_w"][l] = d_sgu_w
        small_parts["sgu_b"][l] = jnp.sum(d_sgu_b, axis=-1)

        dx, dab, s_act, dyh, dg1 = run(108, lambda jobs: _ffn_bwd(
            dx, sv["x0"], lw["g1"], sv["ab1"], lw["w13a"], lw["w2a"], jobs))
        wgrad_done(("ffn1_w13", l), run(58, lambda jobs: _wgrad(sv["h1"], dab, D, 512, "wgrad_w13", jobs)))
        wgrad_done(("ffn1_w2", l), run(33, lambda jobs: _wgrad(s_act, dyh, 256, D, "wgrad_w2", jobs)))
        small_parts["ffn1_norm"][l] = dg1
    grad_x = dx[None]
    flushes = 0
    while pending:
        groups, jobs = take_jobs(float("inf"))
        settle(groups, _standalone(jobs, "grad_flush_%d" % flushes))
        flushes += 1
    big_grads = {nm: reduced[nm].reshape(w[nm].shape) for nm in BIG}

    small_local = {nm: jnp.stack(parts).reshape(depth, *w[nm].shape[1:-1], -1) if nm not in ("sconv_w", "cconv_w")
                   else jnp.stack(parts) for nm, parts in small_parts.items()}
    small_local["final_norm"] = d_final.reshape(-1)
    sizes = [small_local[nm].size for nm in SMALL]
    total = sum(sizes)
    pad_to = NDEV * 8 * GW
    padded = -(-total // pad_to) * pad_to
    packed = jnp.concatenate([small_local[nm].reshape(-1) for nm in SMALL] + [jnp.zeros((padded - total,), F32)])
    summed = _all_reduce_small(packed.reshape(-1, GW)).reshape(-1)
    small_grads, off = {}, 0
    for nm, size in zip(SMALL, sizes):
        small_grads[nm] = summed[off:off + size].reshape(small_local[nm].shape)
        off += size
    for nm in ("sconv_w", "cconv_w"):
        small_grads[nm] = lax.dynamic_slice_in_dim(small_grads[nm], chip * GW, GW, axis=2)

    grads = {**big_grads, **small_grads}

    delta, new_m, new_v = {}, {}, {}
    for nm in BIG:
        delta[nm], new_m[nm], new_v[nm] = [a.reshape(w[nm].shape) for a in big_updates[nm]]
    s_sizes = [w[nm].size for nm in SMALL]
    s_total = sum(s_sizes)
    s_padded = -(-s_total // (8 * GW)) * (8 * GW)

    def pack(tree):
        return jnp.concatenate([tree[nm].reshape(-1) for nm in SMALL] + [jnp.ones((s_padded - s_total,), F32)]).reshape(-1, GW)

    packed_out = _adamw(pack(w), pack(grads), pack(m), pack(v))
    off = 0
    for nm, size in zip(SMALL, s_sizes):
        for tree, arr in zip((delta, new_m, new_v), packed_out):
            tree[nm] = arr.reshape(-1)[off:off + size].reshape(w[nm].shape)
        off += size

    return (loss, grad_x, *[grads[nm] for nm in WEIGHTS], *[delta[nm] for nm in WEIGHTS],
            *[new_m[nm] for nm in WEIGHTS], *[new_v[nm] for nm in WEIGHTS])
```

```python
import jax
import jax.numpy as jnp
from jax import lax
from jax.experimental import pallas as pl
from jax.experimental.pallas import tpu as pltpu

D = 1024
FF = 2816
BW = 512
NBR = 4
MIXC = 4096
INC = 8192
GW = 128
CHUNK = 64
SK = 3
CK = 31
SKP = 8
CKP = 32
HALO = 32
TOKEN_TILE = 256
BRANCH_TILE = 512
EPS = 1e-6
NCHIP = 4
NDEV = 8

ADAM_LR = 0.001
ADAM_B1 = 0.9
ADAM_B2 = 0.999
ADAM_EPS = 1e-08
ADAM_WD = 0.01
ADAM_STEP = 10

VMEM_LIMIT = 56 * 1024 * 1024

BF = jnp.bfloat16
F32 = jnp.float32
MESH = pl.DeviceIdType.MESH
NT = (((1,), (1,)), ((), ()))
TN = (((0,), (0,)), ((), ()))


def _params(n_axes):
    return pltpu.CompilerParams(dimension_semantics=("arbitrary",) * n_axes, vmem_limit_bytes=VMEM_LIMIT)


class _Job:
    def __init__(self, args, n_inplace, fresh, nsem, phases):
        self.args, self.n_inplace, self.fresh, self.nsem, self.phases = list(args), n_inplace, list(fresh), nsem, phases


def _pc(jobs, body, name, in_specs, out_specs, out_shape, grid=(), scratch_shapes=(), input_output_aliases=None,
        compiler_params=None, grid_spec_scalars=None):
    single = not isinstance(out_shape, (list, tuple))
    core_out_specs = [out_specs] if single else list(out_specs)
    core_out_shape = [out_shape] if single else list(out_shape)
    n_in, n_out, n_scr = len(in_specs), len(core_out_specs), len(scratch_shapes)
    n_pre = 0 if grid_spec_scalars is None else 1
    all_in, all_out, all_shape = list(in_specs), list(core_out_specs), list(core_out_shape)
    all_scr, aliases, extra_args, layout = list(scratch_shapes), dict(input_output_aliases or {}), [], []
    for job in jobs:
        n_job_out = job.n_inplace + len(job.fresh)
        layout.append((len(all_in), len(job.args), len(all_out), n_job_out, len(all_scr)))
        for a in range(job.n_inplace):
            aliases[n_pre + len(all_in) + a] = len(all_out) + a
        all_in += [ANY] * len(job.args)
        extra_args += job.args
        all_out += [ANY] * n_job_out
        all_shape += [jax.ShapeDtypeStruct(a.shape, a.dtype) for a in job.args[:job.n_inplace]] + job.fresh
        all_scr += [pltpu.SemaphoreType.DMA((job.nsem,)), pltpu.SemaphoreType.DMA((job.nsem,))]
    steps = 1
    for extent in grid:
        steps *= extent
    events = []
    for (i0, na, o0, no, s0), job in zip(layout, jobs):
        for frac, fn in job.phases:
            events.append((min(int(frac * steps), steps - 1), frac >= 1.0, len(events), fn, (i0, na, o0, no, s0)))
    events.sort(key=lambda e: e[:3])

    def wrapped(*refs):
        pre, refs = refs[:n_pre], refs[n_pre:]
        ins, outs, scr = refs[:len(all_in)], refs[len(all_in):len(all_in) + len(all_out)], refs[len(all_in) + len(all_out):]
        step = 0
        for axis, extent in enumerate(grid):
            step = step * extent + pl.program_id(axis)

        def emit(event):
            at, _, _, fn, (i0, na, o0, no, s0) = event
            run = lambda: fn(ins[i0:i0 + na], outs[o0:o0 + no], scr[s0], scr[s0 + 1])
            if steps == 1:
                run()
            else:
                pl.when(step == at)(run)

        for event in events:
            if not event[1]:
                emit(event)
        body(*pre, *ins[:n_in], *outs[:n_out], *scr[:n_scr])
        for event in events:
            if event[1]:
                emit(event)

    kwargs = dict(name=name, out_shape=all_shape, input_output_aliases=aliases)
    if compiler_params is not None:
        kwargs["compiler_params"] = compiler_params
    if grid_spec_scalars is not None:
        kwargs["grid_spec"] = pltpu.PrefetchScalarGridSpec(
            num_scalar_prefetch=1, grid=grid, in_specs=all_in, out_specs=all_out, scratch_shapes=all_scr)
    else:
        kwargs.update(in_specs=all_in, out_specs=all_out, scratch_shapes=all_scr)
        if grid:
            kwargs["grid"] = grid
    call = pl.pallas_call(wrapped, **kwargs)

    def run_call(*args):
        pre_args = [] if grid_spec_scalars is None else [grid_spec_scalars]
        res = list(call(*pre_args, *args, *extra_args))
        core = res[0] if single else res[:n_out]
        job_outs = [res[o0:o0 + no] for (_, _, o0, no, _) in layout]
        return core, job_outs

    return run_call


def _full(shape):
    nd = len(shape)
    return pl.BlockSpec(shape, lambda *_: (0,) * nd, pipeline_mode=pl.Buffered(1))


def _rows(ts, width, col=0):
    return pl.BlockSpec((ts, width), lambda i: (i, col))


def _sig(v):
    return jax.nn.sigmoid(v)


def _rms_stats(x):
    r = lax.rsqrt(jnp.mean(x * x, axis=-1, keepdims=True) + EPS)
    return r, x * r


def _rms_bwd(x, g, dh):
    r, xh = _rms_stats(x)
    dg = jnp.sum(dh * xh, axis=0, keepdims=True)
    dxh = dh * g
    dx = r * (dxh - xh * jnp.mean(dxh * xh, axis=-1, keepdims=True))
    return dx, dg


def _accumulate(ref, val, first):
    @pl.when(first)
    def _():
        ref[...] = val

    @pl.when(jnp.logical_not(first))
    def _():
        ref[...] += val


def _ffn_fwd(x, g, w13, w2, jobs=()):
    s_len = x.shape[0]
    ts = min(TOKEN_TILE, s_len)

    def body(x_ref, g_ref, w13_ref, w2_ref, xo_ref, h_ref, ab_ref):
        xv = x_ref[...]
        r, xh = _rms_stats(xv)
        h = (xh * g_ref[...]).astype(BF)
        h_ref[...] = h
        ab = jnp.dot(h, w13_ref[...], preferred_element_type=F32)
        ab_ref[...] = ab.astype(BF)
        a = ab[:, :FF]
        b = ab[:, FF:]
        s = (a * _sig(a) * b).astype(BF)
        xo_ref[...] = xv + 0.5 * jnp.dot(s, w2_ref[...], preferred_element_type=F32)

    return _pc(
        jobs, body, name="ffn_fwd", grid=(s_len // ts,),
        in_specs=[_rows(ts, D), _full((1, D)), _full((D, 2 * FF)), _full((FF, D))],
        out_specs=[_rows(ts, D), _rows(ts, D), _rows(ts, 2 * FF)],
        out_shape=[jax.ShapeDtypeStruct((s_len, D), F32), jax.ShapeDtypeStruct((s_len, D), BF),
                   jax.ShapeDtypeStruct((s_len, 2 * FF), BF)],
        compiler_params=_params(1),
    )(x, g, w13, w2)


def _ffn_bwd(dxo, x, g, ab, w13, w2, jobs=()):
    s_len = x.shape[0]
    ts = min(TOKEN_TILE, s_len)

    def body(dxo_ref, x_ref, g_ref, ab_ref, w13_ref, w2_ref, dxi_ref, dab_ref, s_ref, dy_ref, dg_ref):
        i = pl.program_id(0)
        dxo_v = dxo_ref[...]
        dy = (0.5 * dxo_v).astype(BF)
        dy_ref[...] = dy
        ds = lax.dot_general(dy, w2_ref[...], NT, preferred_element_type=F32)
        a = ab_ref[:, :FF].astype(F32)
        b = ab_ref[:, FF:].astype(F32)
        sg = _sig(a)
        sil = a * sg
        s_ref[...] = (sil * b).astype(BF)
        dab_ref[:, :FF] = (ds * b * (sg * (1.0 + a * (1.0 - sg)))).astype(BF)
        dab_ref[:, FF:] = (ds * sil).astype(BF)
        dh = lax.dot_general(dab_ref[...], w13_ref[...], NT, preferred_element_type=F32)
        dx, dg = _rms_bwd(x_ref[...], g_ref[...], dh)
        dxi_ref[...] = dxo_v + dx
        _accumulate(dg_ref, dg, i == 0)

    return _pc(
        jobs, body, name="ffn_bwd", grid=(s_len // ts,),
        in_specs=[_rows(ts, D), _rows(ts, D), _full((1, D)), _rows(ts, 2 * FF), _full((D, 2 * FF)), _full((FF, D))],
        out_specs=[_rows(ts, D), _rows(ts, 2 * FF), _rows(ts, FF), _rows(ts, D), pl.BlockSpec((1, D), lambda i: (0, 0))],
        out_shape=[jax.ShapeDtypeStruct((s_len, D), F32), jax.ShapeDtypeStruct((s_len, 2 * FF), BF),
                   jax.ShapeDtypeStruct((s_len, FF), BF), jax.ShapeDtypeStruct((s_len, D), BF),
                   jax.ShapeDtypeStruct((1, D), F32)],
        compiler_params=_params(1),
    )(dxo, x, g, ab, w13, w2)


def _wgrad(a, b, tk, tn, name, jobs=()):
    s_len, k = a.shape
    n = b.shape[1]

    def body(a_ref, b_ref, o_ref):
        o_ref[...] = lax.dot_general(a_ref[...], b_ref[...], TN, preferred_element_type=F32).astype(BF)

    return _pc(
        jobs, body, name=name, grid=(k // tk, n // tn),
        in_specs=[pl.BlockSpec((s_len, tk), lambda i, j: (0, i)), pl.BlockSpec((s_len, tn), lambda i, j: (0, j))],
        out_specs=pl.BlockSpec((tk, tn), lambda i, j: (i, j)),
        out_shape=jax.ShapeDtypeStruct((k, n), BF),
        compiler_params=_params(2),
    )(a, b)


def _wgrad_groups(a, b, ka, nb, name, jobs=()):
    s_len = a.shape[0]
    groups = a.shape[1] // ka

    def body(a_ref, b_ref, o_ref):
        o_ref[...] = lax.dot_general(a_ref[...], b_ref[...], TN, preferred_element_type=F32).astype(BF)

    return _pc(
        jobs, body, name=name, grid=(groups,),
        in_specs=[pl.BlockSpec((s_len, ka), lambda gi: (0, gi)), pl.BlockSpec((s_len, nb), lambda gi: (0, gi))],
        out_specs=pl.BlockSpec((ka, nb), lambda gi: (gi, 0)),
        out_shape=jax.ShapeDtypeStruct((groups * ka, nb), BF),
        compiler_params=_params(1),
    )(a, b)


def _mix_in(x, g, w_in, jobs=()):
    s_len = x.shape[0]
    ts = min(TOKEN_TILE, s_len)

    def body(x_ref, g_ref, w_ref, h_ref, p_ref):
        _, xh = _rms_stats(x_ref[...])
        h = (xh * g_ref[...]).astype(BF)
        h_ref[...] = h
        p_ref[...] = jnp.dot(h, w_ref[...], preferred_element_type=F32).astype(BF)

    return _pc(
        jobs, body, name="mix_in", grid=(s_len // ts,),
        in_specs=[_rows(ts, D), _full((1, D)), _full((D, INC))],
        out_specs=[_rows(ts, D), _rows(ts, INC)],
        out_shape=[jax.ShapeDtypeStruct((s_len, D), BF), jax.ShapeDtypeStruct((s_len, INC), BF)],
        compiler_params=_params(1),
    )(x, g, w_in)


def _shift(e, j):
    n = e.shape[0]
    j = j % n
    return e if j == 0 else pltpu.roll(e, j, 0)


def _tap_sum(e, w_ref, lead, rows, sign):
    acc = None
    for res in range(8):
        e_res = _shift(e, sign * res)
        for j in range(res, CK, 8):
            start = lead - sign * (j - res)
            term = w_ref[CK - 1 - j:CK - j, :] * e_res[start:start + rows]
            acc = term if acc is None else acc + term
    return acc


def _ln_stats(z):
    mu = jnp.mean(z, axis=-1, keepdims=True)
    zc = z - mu
    rs = lax.rsqrt(jnp.mean(zc * zc, axis=-1, keepdims=True) + EPS)
    return rs, zc * rs


def _ln_bwd(rs, zn, dzn):
    return rs * (dzn - jnp.mean(dzn, axis=-1, keepdims=True) - zn * jnp.mean(dzn * zn, axis=-1, keepdims=True))


_GELU_C0 = 0.7978845608028654
_GELU_C1 = 0.044715


def _gelu(p):
    th = jnp.tanh(_GELU_C0 * (p + _GELU_C1 * p * p * p))
    return 0.5 * p * (1.0 + th), th


def _gelu_grad(p, th):
    return 0.5 * (1.0 + th) + 0.5 * p * (1.0 - th * th) * (_GELU_C0 * (1.0 + 3.0 * _GELU_C1 * p * p))


def _pool_diff(a, t, sign):
    outs = []
    for gi in range(NBR):
        win = 2 ** (gi + 1)
        ag = a[:, gi * GW:(gi + 1) * GW]
        cnt = jnp.clip(t + 1, 1, win).astype(F32)
        ws = ag if sign > 0 else ag / cnt
        for s in range(gi + 1):
            ws = ws + _shift(ws, sign * (2 ** s))
        outs.append((ws / cnt if sign > 0 else ws) - ag)
    return outs


def _sgu_mask():
    row = lax.broadcasted_iota(jnp.int32, (GW, GW), 0)
    col = lax.broadcasted_iota(jnp.int32, (GW, GW), 1)
    return (col // CHUNK) <= (row // CHUNK)


def _assemble(pe_ref, prev_ref, cur_ref, next_ref, i, last, ts):
    pe_ref[0:HALO, :] = jnp.where(i > 0, prev_ref[...], jnp.zeros_like(prev_ref))
    pe_ref[HALO:HALO + ts, :] = cur_ref[...]
    if next_ref is not None:
        pe_ref[HALO + ts:, :] = jnp.where(i < last, next_ref[...], jnp.zeros_like(next_ref))


def _halo_specs(ts, width, s_len, with_next):
    per = ts // HALO
    specs = [pl.BlockSpec((HALO, width), lambda i: (jnp.maximum(i * per - 1, 0), 0)),
             pl.BlockSpec((ts, width), lambda i: (i, 0))]
    if with_next:
        specs.append(pl.BlockSpec((HALO, width), lambda i: (jnp.minimum((i + 1) * per, s_len // HALO - 1), 0)))
    return specs


def _branch_weights_specs():
    return [_full((NBR, GW, GW)), _full((1, BW)), _full((SKP, BW)), _full((CKP, BW)), _full((1, BW)), _full((1, BW)),
            _full((1, BW)), _full((1, BW)), _full((NBR, GW, GW)), _full((GW, NBR))]


def _mix_branches_fwd(p, pool_w, pool_scale, sconv_w, cconv_w, cln_g, cln_b, sln_g, sln_b, sgu_w, sgu_bt, jobs=()):
    s_len = p.shape[0]
    ts = min(BRANCH_TILE, s_len)
    ext = HALO + ts

    def body(pp_ref, pc_ref, pw_ref, ps_ref, sw_ref, cw_ref, clg_ref, clb_ref, slg_ref, slb_ref, gw_ref, gb_ref,
             y_ref, pe_ref):
        i = pl.program_id(0)
        _assemble(pe_ref, pp_ref, pc_ref, None, i, 0, ts)
        t = i * ts - HALO + lax.broadcasted_iota(jnp.int32, (ext, 1), 0)

        dgs = _pool_diff(pe_ref[:, 0:BW].astype(F32), t, 1)
        for gi in range(NBR):
            e = jnp.dot(dgs[gi][HALO:].astype(BF), pw_ref[gi].astype(BF), preferred_element_type=F32)
            y_ref[:, gi * GW:(gi + 1) * GW] = (e * ps_ref[:, gi * GW:(gi + 1) * GW]).astype(BF)

        xin = pe_ref[:, BW:2 * BW].astype(F32)
        cg = pe_ref[:, 3 * BW:4 * BW].astype(F32)
        q = cg * xin
        cv = sw_ref[2:3, :] * q + sw_ref[1:2, :] * _shift(q, 1) + sw_ref[0:1, :] * _shift(q, 2)
        y_ref[:, BW:2 * BW] = (pe_ref[HALO:, 2 * BW:3 * BW].astype(F32) * cv[HALO:]).astype(BF)

        yg = pe_ref[:, 4 * BW:5 * BW].astype(F32) * _sig(pe_ref[:, 5 * BW:6 * BW].astype(F32))
        z = _tap_sum(yg, cw_ref, HALO, ts, 1)
        _, zn = _ln_stats(z)
        nn = zn * clg_ref[...] + clb_ref[...]
        y_ref[:, 2 * BW:3 * BW] = (nn * _sig(nn)).astype(BF)

        u, _ = _gelu(pc_ref[:, 6 * BW:7 * BW].astype(F32))
        v, _ = _gelu(pc_ref[:, 7 * BW:8 * BW].astype(F32))
        _, vn = _ln_stats(v)
        vn = (vn * slg_ref[...] + slb_ref[...]).astype(BF)
        mask = _sgu_mask()
        for hd in range(NBR):
            wm = jnp.where(mask, gw_ref[hd], 0.0).astype(BF)
            for blk in range(ts // GW):
                rows = slice(blk * GW, (blk + 1) * GW)
                cols = slice(hd * GW, (hd + 1) * GW)
                zz = jnp.dot(wm, vn[rows, cols], preferred_element_type=F32) + gb_ref[:, hd:hd + 1]
                y_ref[rows, 3 * BW + hd * GW:3 * BW + (hd + 1) * GW] = (u[rows, cols] * zz).astype(BF)

    return _pc(
        jobs, body, name="mix_branches_fwd", grid=(s_len // ts,),
        in_specs=_halo_specs(ts, MIXC, s_len, False) + _branch_weights_specs(),
        out_specs=_rows(ts, NBR * BW),
        out_shape=jax.ShapeDtypeStruct((s_len, NBR * BW), BF),
        scratch_shapes=[pltpu.VMEM((ext, MIXC), BF)],
        compiler_params=_params(1),
    )(p, p, pool_w, pool_scale, sconv_w, cconv_w, cln_g, cln_b, sln_g, sln_b, sgu_w, sgu_bt)


def _mix_branches_bwd(p, dy, dp, pool_w, pool_scale, sconv_w, cconv_w, cln_g, cln_b, sln_g, sln_b, sgu_w, sgu_bt, jobs=()):
    s_len = p.shape[0]
    ts = min(BRANCH_TILE, s_len)
    ext = ts + 2 * HALO
    last = s_len // ts - 1
    tile = slice(HALO, HALO + ts)
    small_shapes = [(NBR, GW, GW), (1, BW), (SKP, BW), (CKP, BW), (1, BW), (1, BW), (1, BW), (1, BW), (NBR, GW, GW),
                    (NBR, GW, GW)]

    def body(pp_ref, pc_ref, pn_ref, dyp_ref, dyc_ref, dyn_ref, dpin_ref,
             pw_ref, ps_ref, sw_ref, cw_ref, clg_ref, clb_ref, slg_ref, slb_ref, gw_ref, gb_ref,
             dp_ref, dpw_ref, dps_ref, dsw_ref, dcw_ref, dclg_ref, dclb_ref, dslg_ref, dslb_ref, dgw_ref, dgb_ref,
             pe_ref, de_ref):
        del dyp_ref, dpin_ref
        i = pl.program_id(0)
        first = i == 0
        _assemble(pe_ref, pp_ref, pc_ref, pn_ref, i, last, ts)
        de_ref[0:HALO, :] = jnp.zeros((HALO, NBR * BW), BF)
        de_ref[HALO:HALO + ts, :] = dyc_ref[...]
        de_ref[HALO + ts:, :] = jnp.where(i < last, dyn_ref[...], jnp.zeros_like(dyn_ref))
        t = i * ts - HALO + lax.broadcasted_iota(jnp.int32, (ext, 1), 0)

        @pl.when(first)
        def _():
            dsw_ref[...] = jnp.zeros((SKP, BW), F32)
            dcw_ref[...] = jnp.zeros((CKP, BW), F32)

        dgs = _pool_diff(pe_ref[:, 0:BW].astype(F32), t, 1)
        dya = de_ref[:, 0:BW].astype(F32)
        dds = []
        for gi in range(NBR):
            cols = slice(gi * GW, (gi + 1) * GW)
            pw = pw_ref[gi].astype(BF)
            d_t = dgs[gi][tile].astype(BF)
            e = jnp.dot(d_t, pw, preferred_element_type=F32)
            _accumulate(dps_ref.at[:, cols], jnp.sum(dya[tile, cols] * e, axis=0, keepdims=True), first)
            de_g = (dya[:, cols] * ps_ref[:, cols]).astype(BF)
            _accumulate(dpw_ref.at[gi], lax.dot_general(d_t, de_g[tile], TN, preferred_element_type=F32), first)
            dds.append(lax.dot_general(de_g, pw, NT, preferred_element_type=F32))
        das = _pool_diff(jnp.concatenate(dds, axis=1), t, -1)
        for gi in range(NBR):
            dp_ref[:, gi * GW:(gi + 1) * GW] = das[gi][tile].astype(BF)

        xin = pe_ref[:, BW:2 * BW].astype(F32)
        bg = pe_ref[:, 2 * BW:3 * BW].astype(F32)
        cg = pe_ref[:, 3 * BW:4 * BW].astype(F32)
        q = cg * xin
        qs = [q, _shift(q, 1), _shift(q, 2)]
        cv = sw_ref[2:3, :] * qs[0] + sw_ref[1:2, :] * qs[1] + sw_ref[0:1, :] * qs[2]
        dyb = de_ref[:, BW:2 * BW].astype(F32)
        dcv = dyb * bg
        for j in range(SK):
            dsw_ref[SK - 1 - j:SK - j, :] += jnp.sum(dcv[tile] * qs[j][tile], axis=0, keepdims=True)
        dq = sw_ref[2:3, :] * dcv + sw_ref[1:2, :] * _shift(dcv, -1) + sw_ref[0:1, :] * _shift(dcv, -2)
        dp_ref[:, BW:2 * BW] = (dq * cg)[tile].astype(BF)
        dp_ref[:, 2 * BW:3 * BW] = (dyb * cv)[tile].astype(BF)
        dp_ref[:, 3 * BW:4 * BW] = (dq * xin)[tile].astype(BF)

        ca = pe_ref[:, 4 * BW:5 * BW].astype(F32)
        sb = _sig(pe_ref[:, 5 * BW:6 * BW].astype(F32))
        yg = ca * sb
        z = _tap_sum(yg, cw_ref, HALO, ts + HALO, 1)
        rs, zn = _ln_stats(z)
        nn = zn * clg_ref[...] + clb_ref[...]
        sn = _sig(nn)
        dn = de_ref[HALO:, 2 * BW:3 * BW].astype(F32) * (sn * (1.0 + nn * (1.0 - sn)))
        _accumulate(dclg_ref, jnp.sum((dn * zn)[:ts], axis=0, keepdims=True), first)
        _accumulate(dclb_ref, jnp.sum(dn[:ts], axis=0, keepdims=True), first)
        dz = _ln_bwd(rs, zn, dn * clg_ref[...])
        yg_t, sb_t = yg[tile], sb[tile]
        dyg = None
        for res in range(8):
            dz_res = _shift(dz, -res)
            for j in range(res, CK, 8):
                dz_ahead = dz_res[j - res:j - res + ts]
                term = cw_ref[CK - 1 - j:CK - j, :] * dz_ahead
                dyg = term if dyg is None else dyg + term
                dcw_ref[CK - 1 - j:CK - j, :] += jnp.sum(dz_ahead * yg_t, axis=0, keepdims=True)
        dp_ref[:, 4 * BW:5 * BW] = (dyg * sb_t).astype(BF)
        dp_ref[:, 5 * BW:6 * BW] = (dyg * ca[tile] * sb_t * (1.0 - sb_t)).astype(BF)

        pu = pc_ref[:, 6 * BW:7 * BW].astype(F32)
        pv = pc_ref[:, 7 * BW:8 * BW].astype(F32)
        u, thu = _gelu(pu)
        v, thv = _gelu(pv)
        vrs, vn0 = _ln_stats(v)
        vn = (vn0 * slg_ref[...] + slb_ref[...]).astype(BF)
        dyd = dyc_ref[:, 3 * BW:4 * BW].astype(F32)
        dzz = dyd * u
        dzb = dzz.astype(BF)
        mask = _sgu_mask()
        dvn_cols = []
        for hd in range(NBR):
            cols = slice(hd * GW, (hd + 1) * GW)
            wm = jnp.where(mask, gw_ref[hd], 0.0).astype(BF)
            dwm = jnp.zeros((GW, GW), F32)
            dbs = jnp.zeros((GW, GW), F32)
            dvn_rows = []
            for blk in range(ts // GW):
                rows = slice(blk * GW, (blk + 1) * GW)
                zz = jnp.dot(wm, vn[rows, cols], preferred_element_type=F32) + gb_ref[:, hd:hd + 1]
                dp_ref[rows, 6 * BW + hd * GW:6 * BW + (hd + 1) * GW] = (
                    dyd[rows, cols] * zz * _gelu_grad(pu[rows, cols], thu[rows, cols])).astype(BF)
                dwm = dwm + lax.dot_general(dzb[rows, cols], vn[rows, cols], NT, preferred_element_type=F32)
                dbs = dbs + dzz[rows, cols]
                dvn_rows.append(lax.dot_general(wm, dzb[rows, cols], TN, preferred_element_type=F32))
            _accumulate(dgw_ref.at[hd], jnp.where(mask, dwm, 0.0), first)
            _accumulate(dgb_ref.at[hd], dbs, first)
            dvn_cols.append(jnp.concatenate(dvn_rows, axis=0))
        dvn = jnp.concatenate(dvn_cols, axis=1)
        _accumulate(dslg_ref, jnp.sum(dvn * vn0, axis=0, keepdims=True), first)
        _accumulate(dslb_ref, jnp.sum(dvn, axis=0, keepdims=True), first)
        dv = _ln_bwd(vrs, vn0, dvn * slg_ref[...])
        dp_ref[:, 7 * BW:8 * BW] = (dv * _gelu_grad(pv, thv)).astype(BF)

    const = lambda shp: pl.BlockSpec(shp, lambda i: (0,) * len(shp))
    return _pc(
        jobs, body, name="mix_branches_bwd", grid=(s_len // ts,),
        in_specs=(_halo_specs(ts, MIXC, s_len, True) + _halo_specs(ts, NBR * BW, s_len, True)
                  + [pl.BlockSpec(memory_space=pl.ANY)] + _branch_weights_specs()),
        out_specs=[pl.BlockSpec((ts, MIXC), lambda i: (i, 0))] + [const(s) for s in small_shapes],
        out_shape=[jax.ShapeDtypeStruct((s_len, INC), BF)] + [jax.ShapeDtypeStruct(s, F32) for s in small_shapes],
        scratch_shapes=[pltpu.VMEM((ext, MIXC), BF), pltpu.VMEM((ext, NBR * BW), BF)],
        input_output_aliases={6: 0},
        compiler_params=_params(1),
    )(p, p, p, dy, dy, dy, dp, pool_w, pool_scale, sconv_w, cconv_w, cln_g, cln_b, sln_g, sln_b, sgu_w, sgu_bt)


def _mix_out_fwd(x, y, p, w_up, w_out, jobs=()):
    s_len = x.shape[0]
    ts = min(TOKEN_TILE, s_len)

    def body(x_ref, y_ref, pg_ref, wu_ref, wo_ref, xo_ref, m_ref, up_ref):
        m = jnp.zeros((ts, D), F32)
        for gi in range(NBR):
            up = jnp.dot(y_ref[:, gi * BW:(gi + 1) * BW], wu_ref[gi], preferred_element_type=F32)
            up_ref[:, gi * D:(gi + 1) * D] = up.astype(BF)
            m = m + _sig(pg_ref[:, gi * D:(gi + 1) * D].astype(F32)) * up
        mb = m.astype(BF)
        m_ref[...] = mb
        xo_ref[...] = x_ref[...] + jnp.dot(mb, wo_ref[...], preferred_element_type=F32)

    return _pc(
        jobs, body, name="mix_out_fwd", grid=(s_len // ts,),
        in_specs=[_rows(ts, D), _rows(ts, NBR * BW), _rows(ts, NBR * D, 1), _full((NBR, BW, D)), _full((D, D))],
        out_specs=[_rows(ts, D), _rows(ts, D), _rows(ts, NBR * D)],
        out_shape=[jax.ShapeDtypeStruct((s_len, D), F32), jax.ShapeDtypeStruct((s_len, D), BF),
                   jax.ShapeDtypeStruct((s_len, NBR * D), BF)],
        compiler_params=_params(1),
    )(x, y, p, w_up, w_out)


def _mix_out_bwd(dxo, up, p, w_up, w_out, jobs=()):
    s_len = dxo.shape[0]
    ts = min(TOKEN_TILE, s_len)

    def body(dxo_ref, up_ref, pg_ref, wu_ref, wo_ref, dy_ref, dp_ref, dup_ref, dxb_ref):
        dxb = dxo_ref[...].astype(BF)
        dxb_ref[...] = dxb
        dm = lax.dot_general(dxb, wo_ref[...], NT, preferred_element_type=F32)
        for gi in range(NBR):
            cols = slice(gi * D, (gi + 1) * D)
            gate = _sig(pg_ref[:, cols].astype(F32))
            dp_ref[:, cols] = (dm * up_ref[:, cols].astype(F32) * gate * (1.0 - gate)).astype(BF)
            dup = (dm * gate).astype(BF)
            dup_ref[:, cols] = dup
            dy_ref[:, gi * BW:(gi + 1) * BW] = lax.dot_general(
                dup, wu_ref[gi], NT, preferred_element_type=F32).astype(BF)

    return _pc(
        jobs, body, name="mix_out_bwd", grid=(s_len // ts,),
        in_specs=[_rows(ts, D), _rows(ts, NBR * D), _rows(ts, NBR * D, 1), _full((NBR, BW, D)), _full((D, D))],
        out_specs=[_rows(ts, NBR * BW), _rows(ts, NBR * D, 1), _rows(ts, NBR * D), _rows(ts, D)],
        out_shape=[jax.ShapeDtypeStruct((s_len, NBR * BW), BF), jax.ShapeDtypeStruct((s_len, INC), BF),
                   jax.ShapeDtypeStruct((s_len, NBR * D), BF), jax.ShapeDtypeStruct((s_len, D), BF)],
        compiler_params=_params(1),
    )(dxo, up, p, w_up, w_out)


def _mix_in_bwd(dp, w_in, x, g, dxo, jobs=()):
    s_len = x.shape[0]
    ts = min(TOKEN_TILE, s_len)

    def body(dp_ref, w_ref, x_ref, g_ref, dxo_ref, dxi_ref, dg_ref):
        i = pl.program_id(0)
        dh = lax.dot_general(dp_ref[...], w_ref[...], NT, preferred_element_type=F32)
        dx, dg = _rms_bwd(x_ref[...], g_ref[...], dh)
        dxi_ref[...] = dxo_ref[...] + dx
        _accumulate(dg_ref, dg, i == 0)

    return _pc(
        jobs, body, name="mix_in_bwd", grid=(s_len // ts,),
        in_specs=[_rows(ts, INC), _full((D, INC)), _rows(ts, D), _full((1, D)), _rows(ts, D)],
        out_specs=[_rows(ts, D), pl.BlockSpec((1, D), lambda i: (0, 0))],
        out_shape=[jax.ShapeDtypeStruct((s_len, D), F32), jax.ShapeDtypeStruct((1, D), F32)],
        compiler_params=_params(1),
    )(dp, w_in, x, g, dxo)


def _loss_head(x, g, target, jobs=()):
    s_len = x.shape[0]
    ts = min(512, s_len)

    def body(x_ref, g_ref, t_ref, dx_ref, dg_ref, loss_ref):
        i = pl.program_id(0)
        xv = x_ref[...]
        _, xh = _rms_stats(xv)
        err = xh * g_ref[...] - t_ref[...]
        part = 0.5 * jnp.sum(jnp.mean(err * err, axis=-1, keepdims=True), axis=0, keepdims=True)
        dx, dg = _rms_bwd(xv, g_ref[...], err * (1.0 / D))
        dx_ref[...] = dx
        _accumulate(dg_ref, dg, i == 0)
        _accumulate(loss_ref, jnp.broadcast_to(part, (1, GW)), i == 0)

    return _pc(
        jobs, body, name="loss_head", grid=(s_len // ts,),
        in_specs=[_rows(ts, D), _full((1, D)), _rows(ts, D)],
        out_specs=[_rows(ts, D), pl.BlockSpec((1, D), lambda i: (0, 0)), pl.BlockSpec((1, GW), lambda i: (0, 0))],
        out_shape=[jax.ShapeDtypeStruct((s_len, D), F32), jax.ShapeDtypeStruct((1, D), F32),
                   jax.ShapeDtypeStruct((1, GW), F32)],
        compiler_params=_params(1),
    )(x, g, target)


SUM_TILE = 1 << 20
ADAM_TILE = 1 << 19


def _row_tile(rows, cols, budget=1 << 18):
    tr = rows
    while tr * cols > budget and tr % 16 == 0:
        tr //= 2
    return tr


def _elementwise(fn, name, ins, out_dtypes):
    rows, cols = ins[0].shape
    tr = _row_tile(rows, cols)
    n_in = len(ins)

    def body(*refs):
        res = fn(*[r[...] for r in refs[:n_in]])
        for o_ref, val in zip(refs[n_in:], res):
            o_ref[...] = val.astype(o_ref.dtype)

    outs = pl.pallas_call(
        body, name=name, grid=(rows // tr,),
        in_specs=[_rows(tr, cols)] * n_in, out_specs=[_rows(tr, cols)] * len(out_dtypes),
        out_shape=[jax.ShapeDtypeStruct((rows, cols), dt) for dt in out_dtypes],
        compiler_params=_params(1),
    )(*ins)
    return outs


def _tiled(fn, name, grid, in_specs, out_specs, out_shape, args, scalars=None, alias=None):
    alias = alias or {}
    n_in = len(in_specs) - len(alias)
    n_pre = 0 if scalars is None else 1

    def body(*refs):
        refs = refs[n_pre:]
        res = fn(*[r[...] for r in refs[:n_in]])
        for o_ref, val in zip(refs[len(in_specs):], res):
            o_ref[...] = val.astype(o_ref.dtype)

    aliases = {n_pre + pos: out for pos, out in alias.items()}
    if scalars is None:
        return pl.pallas_call(body, name=name, grid=grid, in_specs=in_specs, out_specs=out_specs, out_shape=out_shape,
                              input_output_aliases=aliases, compiler_params=_params(len(grid)))(*args)
    spec = pltpu.PrefetchScalarGridSpec(num_scalar_prefetch=1, grid=grid, in_specs=in_specs, out_specs=out_specs)
    return pl.pallas_call(body, name=name, grid_spec=spec, out_shape=out_shape, input_output_aliases=aliases,
                          compiler_params=_params(len(grid)))(scalars, *args)


def _cast_into(shard, layer, col, dtype, sc):
    ks, ns = shard.shape[1:]
    tr = _row_tile(ks, ns, SUM_TILE)
    full = (ks, ns * NCHIP) if col else (ks * NCHIP, ns)
    out_idx = (lambda i, s: (i, s[1])) if col else (lambda i, s: (s[1] * (ks // tr) + i, 0))
    return _tiled(lambda v: (v,), "cast_into", (ks // tr,), [pl.BlockSpec((None, tr, ns), lambda i, s: (layer, i, 0))],
                  [pl.BlockSpec((tr, ns), out_idx)], [jax.ShapeDtypeStruct(full, dtype)], [shard], sc)[0]


def _pair_sum(g, got, col, sc):
    hk, hn = got.shape
    tr = _row_tile(hk, hn, SUM_TILE)
    g_idx = (lambda i, s: (s[0] * (hk // tr) + i, 0)) if col else (lambda i, s: (i, s[0]))
    plain = pl.BlockSpec((tr, hn), lambda i, s: (i, 0))
    return _tiled(lambda a, b: (a.astype(F32) + b.astype(F32),), "pair_sum", (hk // tr,),
                  [pl.BlockSpec((tr, hn), g_idx), plain], [plain], [jax.ShapeDtypeStruct((hk, hn), BF)], [g, got], sc)[0]


def _chip_sum(ph, got, col, layer, depth, sc, carry):
    qk, qn = got.shape[1:]
    tr = _row_tile(qk, qn, SUM_TILE)
    ph_idx = (lambda i, s: (i, s[1])) if col else (lambda i, s: (s[1] * (qk // tr) + i, 0))
    out_shape = (depth, 2 * qk, qn) if col else (depth, qk, 2 * qn)
    out_idx = (lambda i, s: (layer, s[0] * (qk // tr) + i, 0)) if col else (lambda i, s: (layer, i, s[0]))
    in_specs = [pl.BlockSpec((tr, qn), ph_idx)] + [pl.BlockSpec((None, tr, qn), lambda i, s, j=j: (j, i, 0)) for j in range(3)]
    args = [ph, got, got, got]
    if carry is not None:
        in_specs.append(ANY)
        args.append(carry)
    return _tiled(lambda a, b, c_, d_: (a.astype(F32) + b.astype(F32) + c_.astype(F32) + d_.astype(F32),), "chip_sum",
                  (qk // tr,), in_specs, [pl.BlockSpec((None, tr, qn), out_idx)],
                  [jax.ShapeDtypeStruct(out_shape, F32)], args, sc, alias=None if carry is None else {4: 0})[0]


def _adamw_math(w, g, m, v):
    m = ADAM_B1 * m + (1.0 - ADAM_B1) * g
    v = ADAM_B2 * v + (1.0 - ADAM_B2) * (g * g)
    m_hat = m / (1.0 - ADAM_B1 ** ADAM_STEP)
    v_hat = v / (1.0 - ADAM_B2 ** ADAM_STEP)
    delta = -ADAM_LR * (m_hat / (jnp.sqrt(v_hat) + ADAM_EPS) + ADAM_WD * w)
    return delta, m, v


def _adamw_layer(w, g, m, v, layer, carry):
    k, n = w.shape[1:]
    tr = _row_tile(k, n, ADAM_TILE)
    blk = pl.BlockSpec((None, tr, n), lambda i: (layer, i, 0))
    carry = list(carry or [])
    alias = {4 + pos: pos for pos in range(len(carry))}
    alias[4 + len(carry)] = 3
    struct = jax.ShapeDtypeStruct(w.shape, F32)
    return _tiled(_adamw_math, "adamw_layer", (k // tr,), [blk] * 4 + [ANY] * (len(carry) + 1), [blk] * 3 + [ANY],
                  [struct] * 4, [w, g, m, v] + carry + [g], None, alias=alias)


def _adamw(w, g, m, v):
    shape = w.shape
    two_d = lambda a: a.reshape(-1, shape[-1])
    outs = _elementwise(_adamw_math, "adamw", [two_d(w), two_d(g), two_d(m), two_d(v)], [F32, F32, F32])
    return [o.reshape(shape) for o in outs]


ANY = pl.BlockSpec(memory_space=pl.ANY)


def _place():
    x, y, c = lax.axis_index("x"), lax.axis_index("y"), lax.axis_index("c")
    chips = [(1 - x, y), (x, 1 - y), (1 - x, 1 - y)]
    return x, y, c, chips


def _cols(ref, start, size):
    idx = (slice(None),) * (len(ref.shape) - 1) + (pl.ds(pl.multiple_of(start, GW), size),)
    return ref.at[idx]


def _rows_of(ref, start, size):
    nd = len(ref.shape)
    idx = (slice(None),) * (nd - 2) + (pl.ds(pl.multiple_of(start, 16), size), slice(None))
    return ref.at[idx]


def _region(ref, col_sharded, chip, half):
    k, n = ref.shape
    align = 16 if ref.dtype == BF else 8
    if col_sharded:
        return ref.at[pl.ds(pl.multiple_of(half * (k // 2), align), k // 2),
                      pl.ds(pl.multiple_of(chip * (n // NCHIP), GW), n // NCHIP)]
    rows = k // (2 * NCHIP)
    return ref.at[pl.ds(pl.multiple_of((2 * chip + half) * rows, align), rows), :]


def _job_gather(bufs, col_sharded, handoff):
    n = len(bufs)

    def copies(outs, send_sems, recv_sems, stage):
        x, y, c, chips = _place()
        sends, lands = [], []
        for k in range(n):
            for j, chip in enumerate(chips):
                theirs = 2 * chip[0] + chip[1]
                if stage == 0:
                    src, to = _region(outs[k], col_sharded[k], 2 * x + y, c), (*chip, c)
                    land = _region(outs[k], col_sharded[k], theirs, c)
                else:
                    src, to = _region(outs[k], col_sharded[k], theirs, c), (x, y, 1 - c)
                    land = _region(outs[k], col_sharded[k], theirs, 1 - c)
                sem = 3 * n * stage + 3 * k + j
                sems = dict(send_sem=send_sems.at[sem], recv_sem=recv_sems.at[sem], device_id=to, device_id_type=MESH)
                sends.append(pltpu.make_async_remote_copy(src_ref=src, dst_ref=src, **sems))
                lands.append(pltpu.make_async_remote_copy(src_ref=land, dst_ref=land, **sems))
        return sends, lands

    def start(ins, outs, send_sems, recv_sems):
        for cp in copies(outs, send_sems, recv_sems, 0)[0]:
            cp.start()

    def hand_on(ins, outs, send_sems, recv_sems):
        for cp in copies(outs, send_sems, recv_sems, 0)[1]:
            cp.wait_recv()
        for cp in copies(outs, send_sems, recv_sems, 1)[0]:
            cp.start()

    def finish(ins, outs, send_sems, recv_sems):
        sends, lands = copies(outs, send_sems, recv_sems, 1)
        for cp in lands:
            cp.wait_recv()
        for cp in copies(outs, send_sems, recv_sems, 0)[0] + sends:
            cp.wait_send()

    return _Job(bufs, n, [], 6 * n, [(0.0, start), (handoff, hand_on), (1.0, finish)])


def _half(ref, col_sharded, c):
    k, n = ref.shape[-2:]
    return _rows_of(ref, c * (k // 2), k // 2) if col_sharded else _cols(ref, c * (n // 2), n // 2)


def _quarter(ref, col_sharded, j):
    k, n = ref.shape[-2:]
    return _cols(ref, j * (n // NCHIP), n // NCHIP) if col_sharded else _rows_of(ref, j * (k // NCHIP), k // NCHIP)


def _job_pair(grads, col_sharded):
    n = len(grads)

    def half_shape(g, col):
        return (g.shape[0] // 2, g.shape[1]) if col else (g.shape[0], g.shape[1] // 2)

    def copies(ins, got, send_sems, recv_sems):
        x, y, c, _ = _place()
        return [pltpu.make_async_remote_copy(
            src_ref=_half(ins[k], col_sharded[k], 1 - c), dst_ref=got[k], send_sem=send_sems.at[k],
            recv_sem=recv_sems.at[k], device_id=(x, y, 1 - c), device_id_type=MESH) for k in range(n)]

    def start(*refs):
        for cp in copies(*refs):
            cp.start()

    def finish(*refs):
        for cp in copies(*refs):
            cp.wait()

    fresh = [jax.ShapeDtypeStruct(half_shape(g, col), g.dtype) for g, col in zip(grads, col_sharded)]
    return _Job(grads, 0, fresh, n, [(0.0, start), (1.0, finish)])


def _job_chip(halves, col_sharded):
    n = len(halves)

    def quarter_shape(h, col):
        return (3, h.shape[0], h.shape[1] // NCHIP) if col else (3, h.shape[0] // NCHIP, h.shape[1])

    def copies(ins, got, send_sems, recv_sems):
        x, y, c, chips = _place()
        return [pltpu.make_async_remote_copy(
            src_ref=_quarter(ins[k], col_sharded[k], 2 * chip[0] + chip[1]), dst_ref=got[k].at[j],
            send_sem=send_sems.at[3 * k + j], recv_sem=recv_sems.at[3 * k + j], device_id=(*chip, c), device_id_type=MESH)
            for k in range(n) for j, chip in enumerate(chips)]

    def start(*refs):
        for cp in copies(*refs):
            cp.start()

    def finish(*refs):
        for cp in copies(*refs):
            cp.wait()

    fresh = [jax.ShapeDtypeStruct(quarter_shape(h, col), h.dtype) for h, col in zip(halves, col_sharded)]
    return _Job(halves, 0, fresh, 3 * n, [(0.0, start), (1.0, finish)])


def _job_sibling(shards, col_sharded, layers):
    n = len(shards)

    def copies(outs, send_sems, recv_sems):
        x, y, c, _ = _place()
        sends, lands = [], []
        for k in range(n):
            sems = dict(send_sem=send_sems.at[k], recv_sem=recv_sems.at[k], device_id=(x, y, 1 - c), device_id_type=MESH)
            mine = _half(outs[k].at[layers[k]], col_sharded[k], c)
            theirs = _half(outs[k].at[layers[k]], col_sharded[k], 1 - c)
            sends.append(pltpu.make_async_remote_copy(src_ref=mine, dst_ref=mine, **sems))
            lands.append(pltpu.make_async_remote_copy(src_ref=theirs, dst_ref=theirs, **sems))
        return sends, lands

    def start(ins, outs, send_sems, recv_sems):
        for cp in copies(outs, send_sems, recv_sems)[0]:
            cp.start()

    def finish(ins, outs, send_sems, recv_sems):
        sends, lands = copies(outs, send_sems, recv_sems)
        for cp in lands:
            cp.wait_recv()
        for cp in sends:
            cp.wait_send()

    return _Job(shards, n, [], n, [(0.0, start), (1.0, finish)])


def _standalone(jobs, name):
    return _pc(jobs, lambda: None, name=name, in_specs=[], out_specs=[], out_shape=[])()[1]


def _all_reduce_small(buf, jobs=()):
    rows = buf.shape[0]
    per = rows // NDEV
    flips = [(fx, fy, fc) for fx in (0, 1) for fy in (0, 1) for fc in (0, 1)][1:]

    def body(in_ref, out_ref, got_ref, send_sems, recv_sems):
        x, y, c, _ = _place()
        me = 4 * x + 2 * y + c

        def peer(f):
            return tuple(1 - pos if flip else pos for pos, flip in zip((x, y, c), f))

        def block(ref, dev):
            return ref.at[pl.ds(pl.multiple_of(dev * per, 8), per), :]

        scatter = []
        for k, f in enumerate(flips):
            px, py, pc = peer(f)
            scatter.append(pltpu.make_async_remote_copy(
                src_ref=block(in_ref, 4 * px + 2 * py + pc), dst_ref=got_ref.at[k], send_sem=send_sems.at[k],
                recv_sem=recv_sems.at[k], device_id=(px, py, pc), device_id_type=MESH))
        for cp in scatter:
            cp.start()
        for cp in scatter:
            cp.wait()
        total = block(in_ref, me)[...]
        for k in range(len(flips)):
            total = total + got_ref[k]
        block(out_ref, me)[...] = total
        share = []
        for k, f in enumerate(flips):
            share.append(pltpu.make_async_remote_copy(
                src_ref=block(out_ref, me), dst_ref=block(out_ref, me), send_sem=send_sems.at[7 + k],
                recv_sem=recv_sems.at[7 + k], device_id=peer(f), device_id_type=MESH))
        for cp in share:
            cp.start()
        for k, f in enumerate(flips):
            share[k].wait_send()
            px, py, pc = peer(f)
            theirs = block(out_ref, 4 * px + 2 * py + pc)
            pltpu.make_async_remote_copy(
                src_ref=theirs, dst_ref=theirs, send_sem=send_sems.at[7 + k], recv_sem=recv_sems.at[7 + k],
                device_id=(px, py, pc), device_id_type=MESH).wait_recv()

    vmem = pl.BlockSpec(memory_space=pltpu.VMEM)
    return _pc(
        jobs, body, name="all_reduce_small", in_specs=[vmem], out_specs=vmem,
        out_shape=jax.ShapeDtypeStruct((rows, GW), F32),
        scratch_shapes=[pltpu.VMEM((NDEV - 1, per, GW), F32), pltpu.SemaphoreType.DMA((14,)),
                        pltpu.SemaphoreType.DMA((14,))],
    )(buf)


BIG = ("ffn1_w13", "ffn1_w2", "w_in", "w_up", "w_out", "ffn2_w13", "ffn2_w2")
BIG_COL_SHARDED = (True, False, True, True, False, True, False)
SMALL = ("ffn1_norm", "mix_norm", "pool_w", "pool_scale", "sconv_w", "cconv_w", "cconv_ln_g", "cconv_ln_b",
         "sgu_ln_g", "sgu_ln_b", "sgu_w", "sgu_b", "ffn2_norm", "final_norm")
WEIGHTS = ("ffn1_norm", "ffn1_w13", "ffn1_w2", "mix_norm", "w_in", "pool_w", "pool_scale", "sconv_w", "cconv_w",
           "cconv_ln_g", "cconv_ln_b", "sgu_ln_g", "sgu_ln_b", "sgu_w", "sgu_b", "w_up", "w_out", "ffn2_norm",
           "ffn2_w13", "ffn2_w2", "final_norm")


def _pad_rows(a, rows):
    return jnp.pad(a, ((0, 0), (0, rows - a.shape[1]), (0, 0)))


def kernel(x, ffn1_norm, ffn1_w13, ffn1_w2, mix_norm, w_in, pool_w, pool_scale, sconv_w, cconv_w, cconv_ln_g, cconv_ln_b, sgu_ln_g, sgu_ln_b, sgu_w, sgu_b, w_up, w_out, ffn2_norm, ffn2_w13, ffn2_w2, final_norm, loss_target, m_ffn1_norm, m_ffn1_w13, m_ffn1_w2, m_mix_norm, m_w_in, m_pool_w, m_pool_scale, m_sconv_w, m_cconv_w, m_cconv_ln_g, m_cconv_ln_b, m_sgu_ln_g, m_sgu_ln_b, m_sgu_w, m_sgu_b, m_w_up, m_w_out, m_ffn2_norm, m_ffn2_w13, m_ffn2_w2, m_final_norm, v_ffn1_norm, v_ffn1_w13, v_ffn1_w2, v_mix_norm, v_w_in, v_pool_w, v_pool_scale, v_sconv_w, v_cconv_w, v_cconv_ln_g, v_cconv_ln_b, v_sgu_ln_g, v_sgu_ln_b, v_sgu_w, v_sgu_b, v_w_up, v_w_out, v_ffn2_norm, v_ffn2_w13, v_ffn2_w2, v_final_norm):
    args = dict(locals())
    w = {nm: args[nm] for nm in WEIGHTS}
    m = {nm: args["m_" + nm] for nm in WEIGHTS}
    v = {nm: args["v_" + nm] for nm in WEIGHTS}
    depth = ffn1_w13.shape[0]
    chip = 2 * lax.axis_index("x") + lax.axis_index("y")

    sc = jnp.stack([lax.axis_index("c"), chip]).astype(jnp.int32)
    col_of = dict(zip(BIG, BIG_COL_SHARDED), sconv_w=True, cconv_w=True)
    sources = {nm: (w[nm], BF) for nm in BIG}
    sources["w_up"] = (w_up.reshape(depth, NBR * BW, w_up.shape[-1]), BF)
    sources["sconv_w"] = (_pad_rows(sconv_w, 2 * SKP), F32)
    sources["cconv_w"] = (_pad_rows(cconv_w, 2 * CKP), F32)
    full = [{nm: _cast_into(src, l, col_of[nm], dt, sc) for nm, (src, dt) in sources.items()} for l in range(depth)]

    def gather_job(l, names, handoff):
        return _job_gather([full[l][nm] for nm in names], [col_of[nm] for nm in names], handoff)

    def gather_with(l, names, handoff, call):
        res, job_outs = call([gather_job(l, names, handoff)] if l < depth else [])
        if l < depth:
            full[l].update(zip(names, job_outs[0]))
        return res

    first_group = ("ffn1_w13", "ffn1_w2")
    full[0].update(zip(first_group, _standalone([gather_job(0, first_group, 0.0)], "gather_first")[0]))

    xs = x[0]
    row = lambda a: a.reshape(1, -1)
    saved = []
    for l in range(depth):
        g1, gm, g2 = row(ffn1_norm[l]), row(mix_norm[l]), row(ffn2_norm[l])
        x1, h1, ab1 = gather_with(l, ("w_in",), 0.85, lambda jobs: _ffn_fwd(
            xs, g1, full[l]["ffn1_w13"], full[l]["ffn1_w2"], jobs))
        hm, p = gather_with(l, ("w_up", "w_out", "sconv_w", "cconv_w", "ffn2_w2"), 0.75, lambda jobs: _mix_in(
            x1, gm, full[l]["w_in"], jobs))
        branch = (pool_w[l], row(pool_scale[l]), full[l]["sconv_w"][:SKP], full[l]["cconv_w"][:CKP], row(cconv_ln_g[l]),
                  row(cconv_ln_b[l]), row(sgu_ln_g[l]), row(sgu_ln_b[l]), sgu_w[l], sgu_b[l].T)
        y = gather_with(l, ("ffn2_w13",), 0.85, lambda jobs: _mix_branches_fwd(p, *branch, jobs=jobs))
        w_up_l = full[l]["w_up"].reshape(NBR, BW, D)
        x2, merged, up = gather_with(l + 1, ("ffn1_w2",), 0.7, lambda jobs: _mix_out_fwd(
            x1, y, p, w_up_l, full[l]["w_out"], jobs))
        x3, h2, ab2 = gather_with(l + 1, ("ffn1_w13",), 0.75, lambda jobs: _ffn_fwd(
            x2, g2, full[l]["ffn2_w13"], full[l]["ffn2_w2"], jobs))
        lw = dict(g1=g1, gm=gm, g2=g2, w13a=full[l]["ffn1_w13"], w2a=full[l]["ffn1_w2"], w13b=full[l]["ffn2_w13"],
                  w2b=full[l]["ffn2_w2"], w_in=full[l]["w_in"], w_up=w_up_l, w_out=full[l]["w_out"], branch=branch)
        saved.append(dict(lw=lw, x0=xs, x1=x1, x2=x2, h1=h1, ab1=ab1, hm=hm, p=p, y=y, merged=merged, up=up, h2=h2,
                          ab2=ab2))
        xs = x3

    (dx, d_final, loss_part), _ = _loss_head(xs, row(final_norm), loss_target[0])
    loss = lax.psum(loss_part[0, 0], ("x", "y", "c"))

    ici_us = dict(ffn1_w13=64, ffn1_w2=32, w_in=93, w_up=23, w_out=12, ffn2_w13=64, ffn2_w2=32)
    parts, pair_sums, reduced, big_updates, pending = {}, {}, {}, {}, []

    def take_jobs(budget_us):
        chosen = []
        for task in list(pending):
            kind, (nm, _) = task
            if kind == "chip":
                if ici_us[nm] > budget_us:
                    continue
                budget_us -= ici_us[nm]
            if kind == "sib" and any(k == "sib" and key[0] == nm for k, key in chosen):
                continue
            chosen.append(task)
            pending.remove(task)
        groups, jobs = [], []
        for kind in ("pair", "sib", "chip"):
            keys = [key for k, key in chosen if k == kind]
            if not keys:
                continue
            cols = [col_of[nm] for nm, _ in keys]
            groups.append((kind, keys))
            if kind == "pair":
                jobs.append(_job_pair([parts[key] for key in keys], cols))
            elif kind == "chip":
                jobs.append(_job_chip([pair_sums[key] for key in keys], cols))
            else:
                jobs.append(_job_sibling([reduced[nm] for nm, _ in keys], cols, [layer for _, layer in keys]))
        return groups, jobs

    def settle(groups, job_outs):
        for (kind, keys), outs in zip(groups, job_outs):
            for key, out in zip(keys, outs):
                nm, layer = key
                if kind == "pair":
                    pair_sums[key] = _pair_sum(parts[key], out, col_of[nm], sc)
                    pending.append(("chip", key))
                elif kind == "chip":
                    assert not any(k == "sib" and other[0] == nm for k, other in pending)
                    reduced[nm] = _chip_sum(pair_sums[key], out, col_of[nm], layer, depth, sc, reduced.get(nm))
                    pending.append(("sib", key))
                else:
                    as3 = lambda a: a.reshape(out.shape)
                    *big_updates[nm], reduced[nm] = _adamw_layer(
                        as3(w[nm]), out, as3(m[nm]), as3(v[nm]), layer, big_updates.get(nm))

    def run(budget_us, call, carrier=True):
        groups, jobs = take_jobs(budget_us) if carrier else ([], [])
        res, job_outs = call(jobs)
        settle(groups, job_outs)
        return res

    def wgrad_done(key, partial):
        parts[key] = partial
        pending.append(("pair", key))

    small_parts = {nm: [None] * depth for nm in SMALL if nm != "final_norm"}
    for l in reversed(range(depth)):
        sv = saved[l]
        lw = sv["lw"]
        dx, dab, s_act, dyh, dg2 = run(108, lambda jobs: _ffn_bwd(
            dx, sv["x2"], lw["g2"], sv["ab2"], lw["w13b"], lw["w2b"], jobs))
        wgrad_done(("ffn2_w13", l), run(58, lambda jobs: _wgrad(sv["h2"], dab, D, 512, "wgrad_w13", jobs)))
        wgrad_done(("ffn2_w2", l), run(33, lambda jobs: _wgrad(s_act, dyh, 256, D, "wgrad_w2", jobs)))
        small_parts["ffn2_norm"][l] = dg2

        dy, dp, dup, dxb = run(70, lambda jobs: _mix_out_bwd(dx, sv["up"], sv["p"], lw["w_up"], lw["w_out"], jobs))
        wgrad_done(("w_out", l), run(16, lambda jobs: _wgrad(sv["merged"], dxb, D, 512, "wgrad_w_out", jobs), False))
        wgrad_done(("w_up", l), run(25, lambda jobs: _wgrad_groups(sv["y"], dup, BW, D, "wgrad_w_up", jobs), False))
        (dp, d_pool_w, d_pool_scale, d_sconv, d_cconv, d_clg, d_clb, d_slg, d_slb, d_sgu_w, d_sgu_b) = run(
            214, lambda jobs: _mix_branches_bwd(sv["p"], dy, dp, *lw["branch"], jobs=jobs))
        wgrad_done(("w_in", l), run(80, lambda jobs: _wgrad(sv["hm"], dp, D, 512, "wgrad_w_in", jobs)))
        dx, dgm = run(91, lambda jobs: _mix_in_bwd(dp, lw["w_in"], sv["x1"], lw["gm"], dx, jobs))
        small_parts["mix_norm"][l] = dgm
        small_parts["pool_w"][l] = d_pool_w
        small_parts["pool_scale"][l] = d_pool_scale
        small_parts["sconv_w"][l] = d_sconv[:SK]
        small_parts["cconv_w"][l] = d_cconv[:CK]
        small_parts["cconv_ln_g"][l] = d_clg
        small_parts["cconv_ln_b"][l] = d_clb
        small_parts["sgu_ln_g"][l] = d_slg
        small_parts["sgu_ln_b"][l] = d_slb
        small_parts["sgu_w"][l] = d_sgu_w
        small_parts["sgu_b"][l] = jnp.sum(d_sgu_b, axis=-1)

        dx, dab, s_act, dyh, dg1 = run(108, lambda jobs: _ffn_bwd(
            dx, sv["x0"], lw["g1"], sv["ab1"], lw["w13a"], lw["w2a"], jobs))
        wgrad_done(("ffn1_w13", l), run(58, lambda jobs: _wgrad(sv["h1"], dab, D, 512, "wgrad_w13", jobs)))
        wgrad_done(("ffn1_w2", l), run(33, lambda jobs: _wgrad(s_act, dyh, 256, D, "wgrad_w2", jobs)))
        small_parts["ffn1_norm"][l] = dg1
    grad_x = dx[None]

    small_local = {nm: jnp.stack(parts).reshape(depth, *w[nm].shape[1:-1], -1) if nm not in ("sconv_w", "cconv_w")
                   else jnp.stack(parts) for nm, parts in small_parts.items()}
    small_local["final_norm"] = d_final.reshape(-1)
    sizes = [small_local[nm].size for nm in SMALL]
    total = sum(sizes)
    pad_to = NDEV * 8 * GW
    padded = -(-total // pad_to) * pad_to
    packed = jnp.concatenate([small_local[nm].reshape(-1) for nm in SMALL] + [jnp.zeros((padded - total,), F32)])
    groups, jobs = take_jobs(float("inf"))
    summed, job_outs = _all_reduce_small(packed.reshape(-1, GW), jobs)
    settle(groups, job_outs)
    summed = summed.reshape(-1)
    flushes = 0
    while pending:
        groups, jobs = take_jobs(float("inf"))
        settle(groups, _standalone(jobs, "grad_flush_%d" % flushes))
        flushes += 1
    big_grads = {nm: reduced[nm].reshape(w[nm].shape) for nm in BIG}
    small_grads, off = {}, 0
    for nm, size in zip(SMALL, sizes):
        small_grads[nm] = summed[off:off + size].reshape(small_local[nm].shape)
        off += size
    for nm in ("sconv_w", "cconv_w"):
        small_grads[nm] = lax.dynamic_slice_in_dim(small_grads[nm], chip * GW, GW, axis=2)

    grads = {**big_grads, **small_grads}

    delta, new_m, new_v = {}, {}, {}
    for nm in BIG:
        delta[nm], new_m[nm], new_v[nm] = [a.reshape(w[nm].shape) for a in big_updates[nm]]
    s_sizes = [w[nm].size for nm in SMALL]
    s_total = sum(s_sizes)
    s_padded = -(-s_total // (8 * GW)) * (8 * GW)

    def pack(tree):
        return jnp.concatenate([tree[nm].reshape(-1) for nm in SMALL] + [jnp.ones((s_padded - s_total,), F32)]).reshape(-1, GW)

    packed_out = _adamw(pack(w), pack(grads), pack(m), pack(v))
    off = 0
    for nm, size in zip(SMALL, s_sizes):
        for tree, arr in zip((delta, new_m, new_v), packed_out):
            tree[nm] = arr.reshape(-1)[off:off + size].reshape(w[nm].shape)
        off += size

    return (loss, grad_x, *[grads[nm] for nm in WEIGHTS], *[delta[nm] for nm in WEIGHTS],
            *[new_m[nm] for nm in WEIGHTS], *[new_v[nm] for nm in WEIGHTS])
```

```python
import jax
import jax.numpy as jnp
from jax import lax
from jax.experimental import pallas as pl
from jax.experimental.pallas import tpu as pltpu

D = 1024
FF = 2816
BW = 512
NBR = 4
MIXC = 4096
INC = 8192
GW = 128
CHUNK = 64
SK = 3
CK = 31
SKP = 8
CKP = 32
HALO = 32
TOKEN_TILE = 512
FFN_BWD_TILE = 256
BRANCH_TILE = 512
EPS = 1e-6
NCHIP = 4
NDEV = 8

ADAM_LR = 0.001
ADAM_B1 = 0.9
ADAM_B2 = 0.999
ADAM_EPS = 1e-08
ADAM_WD = 0.01
ADAM_STEP = 10

VMEM_LIMIT = 56 * 1024 * 1024

BF = jnp.bfloat16
F32 = jnp.float32
MESH = pl.DeviceIdType.MESH
NT = (((1,), (1,)), ((), ()))
TN = (((0,), (0,)), ((), ()))


def _params(n_axes):
    return pltpu.CompilerParams(dimension_semantics=("arbitrary",) * n_axes, vmem_limit_bytes=VMEM_LIMIT)


class _Job:
    def __init__(self, args, n_inplace, fresh, nsem, phases):
        self.args, self.n_inplace, self.fresh, self.nsem, self.phases = list(args), n_inplace, list(fresh), nsem, phases


def _pc(jobs, body, name, in_specs, out_specs, out_shape, grid=(), scratch_shapes=(), input_output_aliases=None,
        compiler_params=None, grid_spec_scalars=None):
    single = not isinstance(out_shape, (list, tuple))
    core_out_specs = [out_specs] if single else list(out_specs)
    core_out_shape = [out_shape] if single else list(out_shape)
    n_in, n_out, n_scr = len(in_specs), len(core_out_specs), len(scratch_shapes)
    n_pre = 0 if grid_spec_scalars is None else 1
    all_in, all_out, all_shape = list(in_specs), list(core_out_specs), list(core_out_shape)
    all_scr, aliases, extra_args, layout = list(scratch_shapes), dict(input_output_aliases or {}), [], []
    for job in jobs:
        n_job_out = job.n_inplace + len(job.fresh)
        layout.append((len(all_in), len(job.args), len(all_out), n_job_out, len(all_scr)))
        for a in range(job.n_inplace):
            aliases[n_pre + len(all_in) + a] = len(all_out) + a
        all_in += [ANY] * len(job.args)
        extra_args += job.args
        all_out += [ANY] * n_job_out
        all_shape += [jax.ShapeDtypeStruct(a.shape, a.dtype) for a in job.args[:job.n_inplace]] + job.fresh
        all_scr += [pltpu.SemaphoreType.DMA((job.nsem,)), pltpu.SemaphoreType.DMA((job.nsem,))]
    steps = 1
    for extent in grid:
        steps *= extent
    events = []
    for (i0, na, o0, no, s0), job in zip(layout, jobs):
        for frac, fn in job.phases:
            events.append((min(int(frac * steps), steps - 1), frac >= 1.0, len(events), fn, (i0, na, o0, no, s0)))
    events.sort(key=lambda e: e[:3])

    def wrapped(*refs):
        pre, refs = refs[:n_pre], refs[n_pre:]
        ins, outs, scr = refs[:len(all_in)], refs[len(all_in):len(all_in) + len(all_out)], refs[len(all_in) + len(all_out):]
        step = 0
        for axis, extent in enumerate(grid):
            step = step * extent + pl.program_id(axis)

        def emit(event):
            at, _, _, fn, (i0, na, o0, no, s0) = event
            run = lambda: fn(ins[i0:i0 + na], outs[o0:o0 + no], scr[s0], scr[s0 + 1])
            if steps == 1:
                run()
            else:
                pl.when(step == at)(run)

        for event in events:
            if not event[1]:
                emit(event)
        body(*pre, *ins[:n_in], *outs[:n_out], *scr[:n_scr])
        for event in events:
            if event[1]:
                emit(event)

    kwargs = dict(name=name, out_shape=all_shape, input_output_aliases=aliases)
    if compiler_params is not None:
        kwargs["compiler_params"] = compiler_params
    if grid_spec_scalars is not None:
        kwargs["grid_spec"] = pltpu.PrefetchScalarGridSpec(
            num_scalar_prefetch=1, grid=grid, in_specs=all_in, out_specs=all_out, scratch_shapes=all_scr)
    else:
        kwargs.update(in_specs=all_in, out_specs=all_out, scratch_shapes=all_scr)
        if grid:
            kwargs["grid"] = grid
    call = pl.pallas_call(wrapped, **kwargs)

    def run_call(*args):
        pre_args = [] if grid_spec_scalars is None else [grid_spec_scalars]
        res = list(call(*pre_args, *args, *extra_args))
        core = res[0] if single else res[:n_out]
        job_outs = [res[o0:o0 + no] for (_, _, o0, no, _) in layout]
        return core, job_outs

    return run_call


def _full(shape):
    nd = len(shape)
    return pl.BlockSpec(shape, lambda *_: (0,) * nd, pipeline_mode=pl.Buffered(1))


def _rows(ts, width, col=0):
    return pl.BlockSpec((ts, width), lambda i: (i, col))


def _sig(v):
    return jax.nn.sigmoid(v)


def _rms_stats(x):
    r = lax.rsqrt(jnp.mean(x * x, axis=-1, keepdims=True) + EPS)
    return r, x * r


def _rms_bwd(x, g, dh):
    r, xh = _rms_stats(x)
    dg = jnp.sum(dh * xh, axis=0, keepdims=True)
    dxh = dh * g
    dx = r * (dxh - xh * jnp.mean(dxh * xh, axis=-1, keepdims=True))
    return dx, dg


def _accumulate(ref, val, first):
    @pl.when(first)
    def _():
        ref[...] = val

    @pl.when(jnp.logical_not(first))
    def _():
        ref[...] += val


def _ffn_fwd(x, g, w13, w2, jobs=()):
    s_len = x.shape[0]
    ts = min(TOKEN_TILE, s_len)

    def body(x_ref, g_ref, w13_ref, w2_ref, xo_ref, h_ref, ab_ref):
        xv = x_ref[...]
        r, xh = _rms_stats(xv)
        h = (xh * g_ref[...]).astype(BF)
        h_ref[...] = h
        ab = jnp.dot(h, w13_ref[...], preferred_element_type=F32)
        ab_ref[...] = ab.astype(BF)
        a = ab[:, :FF]
        b = ab[:, FF:]
        s = (a * _sig(a) * b).astype(BF)
        xo_ref[...] = xv + 0.5 * jnp.dot(s, w2_ref[...], preferred_element_type=F32)

    return _pc(
        jobs, body, name="ffn_fwd", grid=(s_len // ts,),
        in_specs=[_rows(ts, D), _full((1, D)), _full((D, 2 * FF)), _full((FF, D))],
        out_specs=[_rows(ts, D), _rows(ts, D), _rows(ts, 2 * FF)],
        out_shape=[jax.ShapeDtypeStruct((s_len, D), F32), jax.ShapeDtypeStruct((s_len, D), BF),
                   jax.ShapeDtypeStruct((s_len, 2 * FF), BF)],
        compiler_params=_params(1),
    )(x, g, w13, w2)


def _ffn_bwd(dxo, x, g, ab, w13, w2, jobs=()):
    s_len = x.shape[0]
    ts = min(FFN_BWD_TILE, s_len)

    def body(dxo_ref, x_ref, g_ref, ab_ref, w13_ref, w2_ref, dxi_ref, dab_ref, s_ref, dy_ref, dg_ref):
        i = pl.program_id(0)
        dxo_v = dxo_ref[...]
        dy = (0.5 * dxo_v).astype(BF)
        dy_ref[...] = dy
        ds = lax.dot_general(dy, w2_ref[...], NT, preferred_element_type=F32)
        a = ab_ref[:, :FF].astype(F32)
        b = ab_ref[:, FF:].astype(F32)
        sg = _sig(a)
        sil = a * sg
        s_ref[...] = (sil * b).astype(BF)
        dab_ref[:, :FF] = (ds * b * (sg * (1.0 + a * (1.0 - sg)))).astype(BF)
        dab_ref[:, FF:] = (ds * sil).astype(BF)
        dh = lax.dot_general(dab_ref[...], w13_ref[...], NT, preferred_element_type=F32)
        dx, dg = _rms_bwd(x_ref[...], g_ref[...], dh)
        dxi_ref[...] = dxo_v + dx
        _accumulate(dg_ref, dg, i == 0)

    return _pc(
        jobs, body, name="ffn_bwd", grid=(s_len // ts,),
        in_specs=[_rows(ts, D), _rows(ts, D), _full((1, D)), _rows(ts, 2 * FF), _full((D, 2 * FF)), _full((FF, D))],
        out_specs=[_rows(ts, D), _rows(ts, 2 * FF), _rows(ts, FF), _rows(ts, D), pl.BlockSpec((1, D), lambda i: (0, 0))],
        out_shape=[jax.ShapeDtypeStruct((s_len, D), F32), jax.ShapeDtypeStruct((s_len, 2 * FF), BF),
                   jax.ShapeDtypeStruct((s_len, FF), BF), jax.ShapeDtypeStruct((s_len, D), BF),
                   jax.ShapeDtypeStruct((1, D), F32)],
        compiler_params=_params(1),
    )(dxo, x, g, ab, w13, w2)


def _wgrad(a, b, tk, tn, name, jobs=()):
    s_len, k = a.shape
    n = b.shape[1]

    def body(a_ref, b_ref, o_ref):
        o_ref[...] = lax.dot_general(a_ref[...], b_ref[...], TN, preferred_element_type=F32).astype(BF)

    return _pc(
        jobs, body, name=name, grid=(k // tk, n // tn),
        in_specs=[pl.BlockSpec((s_len, tk), lambda i, j: (0, i)), pl.BlockSpec((s_len, tn), lambda i, j: (0, j))],
        out_specs=pl.BlockSpec((tk, tn), lambda i, j: (i, j)),
        out_shape=jax.ShapeDtypeStruct((k, n), BF),
        compiler_params=_params(2),
    )(a, b)


def _wgrad_groups(a, b, ka, nb, name, jobs=()):
    s_len = a.shape[0]
    groups = a.shape[1] // ka

    def body(a_ref, b_ref, o_ref):
        o_ref[...] = lax.dot_general(a_ref[...], b_ref[...], TN, preferred_element_type=F32).astype(BF)

    return _pc(
        jobs, body, name=name, grid=(groups,),
        in_specs=[pl.BlockSpec((s_len, ka), lambda gi: (0, gi)), pl.BlockSpec((s_len, nb), lambda gi: (0, gi))],
        out_specs=pl.BlockSpec((ka, nb), lambda gi: (gi, 0)),
        out_shape=jax.ShapeDtypeStruct((groups * ka, nb), BF),
        compiler_params=_params(1),
    )(a, b)


def _mix_in(x, g, w_in, jobs=()):
    s_len = x.shape[0]
    ts = min(TOKEN_TILE, s_len)

    def body(x_ref, g_ref, w_ref, h_ref, p_ref):
        _, xh = _rms_stats(x_ref[...])
        h = (xh * g_ref[...]).astype(BF)
        h_ref[...] = h
        p_ref[...] = jnp.dot(h, w_ref[...], preferred_element_type=F32).astype(BF)

    return _pc(
        jobs, body, name="mix_in", grid=(s_len // ts,),
        in_specs=[_rows(ts, D), _full((1, D)), _full((D, INC))],
        out_specs=[_rows(ts, D), _rows(ts, INC)],
        out_shape=[jax.ShapeDtypeStruct((s_len, D), BF), jax.ShapeDtypeStruct((s_len, INC), BF)],
        compiler_params=_params(1),
    )(x, g, w_in)


def _shift(e, j):
    n = e.shape[0]
    j = j % n
    return e if j == 0 else pltpu.roll(e, j, 0)


def _ln_stats(z):
    mu = jnp.mean(z, axis=-1, keepdims=True)
    zc = z - mu
    rs = lax.rsqrt(jnp.mean(zc * zc, axis=-1, keepdims=True) + EPS)
    return rs, zc * rs


def _ln_bwd(rs, zn, dzn):
    return rs * (dzn - jnp.mean(dzn, axis=-1, keepdims=True) - zn * jnp.mean(dzn * zn, axis=-1, keepdims=True))


_GELU_C0 = 0.7978845608028654
_GELU_C1 = 0.044715


def _gelu(p):
    th = jnp.tanh(_GELU_C0 * (p + _GELU_C1 * p * p * p))
    return 0.5 * p * (1.0 + th), th


def _gelu_grad(p, th):
    return 0.5 * (1.0 + th) + 0.5 * p * (1.0 - th * th) * (_GELU_C0 * (1.0 + 3.0 * _GELU_C1 * p * p))


def _pool_diff(a, t, sign):
    outs = []
    for gi in range(NBR):
        win = 2 ** (gi + 1)
        ag = a[:, gi * GW:(gi + 1) * GW]
        cnt = jnp.clip(t + 1, 1, win).astype(F32)
        ws = ag if sign > 0 else ag / cnt
        for s in range(gi + 1):
            ws = ws + _shift(ws, sign * (2 ** s))
        outs.append((ws / cnt if sign > 0 else ws) - ag)
    return outs


def _sgu_mask():
    row = lax.broadcasted_iota(jnp.int32, (GW, GW), 0)
    col = lax.broadcasted_iota(jnp.int32, (GW, GW), 1)
    return (col // CHUNK) <= (row // CHUNK)


def _assemble(pe_ref, prev_ref, cur_ref, next_ref, i, last, ts):
    pe_ref[0:HALO, :] = jnp.where(i > 0, prev_ref[...], jnp.zeros_like(prev_ref))
    pe_ref[HALO:HALO + ts, :] = cur_ref[...]
    if next_ref is not None:
        pe_ref[HALO + ts:, :] = jnp.where(i < last, next_ref[...], jnp.zeros_like(next_ref))


def _halo_specs(ts, width, s_len, with_next):
    per = ts // HALO
    specs = [pl.BlockSpec((HALO, width), lambda i: (jnp.maximum(i * per - 1, 0), 0)),
             pl.BlockSpec((ts, width), lambda i: (i, 0))]
    if with_next:
        specs.append(pl.BlockSpec((HALO, width), lambda i: (jnp.minimum((i + 1) * per, s_len // HALO - 1), 0)))
    return specs


def _branch_weights_specs():
    return [_full((NBR, GW, GW)), _full((1, BW)), _full((SKP, BW)), _full((CKP, BW)), _full((1, BW)), _full((1, BW)),
            _full((1, BW)), _full((1, BW)), _full((NBR, GW, GW)), _full((GW, NBR))]


def _mix_branches_fwd(p, pool_w, pool_scale, sconv_w, cconv_w, cln_g, cln_b, sln_g, sln_b, sgu_w, sgu_bt, jobs=()):
    s_len = p.shape[0]
    ts = min(BRANCH_TILE, s_len)
    ext = HALO + ts

    def body(pp_ref, pc_ref, pw_ref, ps_ref, sw_ref, cw_ref, clg_ref, clb_ref, slg_ref, slb_ref, gw_ref, gb_ref,
             y_ref, pe_ref):
        i = pl.program_id(0)
        _assemble(pe_ref, pp_ref, pc_ref, None, i, 0, ts)
        t = i * ts - HALO + lax.broadcasted_iota(jnp.int32, (ext, 1), 0)

        dgs = _pool_diff(pe_ref[:, 0:BW].astype(F32), t, 1)
        for gi in range(NBR):
            e = jnp.dot(dgs[gi][HALO:].astype(BF), pw_ref[gi].astype(BF), preferred_element_type=F32)
            y_ref[:, gi * GW:(gi + 1) * GW] = (e * ps_ref[:, gi * GW:(gi + 1) * GW]).astype(BF)

        xin = pe_ref[:, BW:2 * BW].astype(F32)
        cg = pe_ref[:, 3 * BW:4 * BW].astype(F32)
        q = cg * xin
        cv = sw_ref[2:3, :] * q + sw_ref[1:2, :] * _shift(q, 1) + sw_ref[0:1, :] * _shift(q, 2)
        y_ref[:, BW:2 * BW] = (pe_ref[HALO:, 2 * BW:3 * BW].astype(F32) * cv[HALO:]).astype(BF)

        yg = pe_ref[:, 4 * BW:5 * BW].astype(F32) * _sig(pe_ref[:, 5 * BW:6 * BW].astype(F32))
        z = cw_ref[CK - 1:CK, :] * yg
        for j in range(1, CK):
            z = z + cw_ref[CK - 1 - j:CK - j, :] * _shift(yg, j)
        _, zn = _ln_stats(z[HALO:])
        nn = zn * clg_ref[...] + clb_ref[...]
        y_ref[:, 2 * BW:3 * BW] = (nn * _sig(nn)).astype(BF)

        u, _ = _gelu(pc_ref[:, 6 * BW:7 * BW].astype(F32))
        v, _ = _gelu(pc_ref[:, 7 * BW:8 * BW].astype(F32))
        _, vn = _ln_stats(v)
        vn = (vn * slg_ref[...] + slb_ref[...]).astype(BF)
        mask = _sgu_mask()
        for hd in range(NBR):
            wm = jnp.where(mask, gw_ref[hd], 0.0).astype(BF)
            for blk in range(ts // GW):
                rows = slice(blk * GW, (blk + 1) * GW)
                cols = slice(hd * GW, (hd + 1) * GW)
                zz = jnp.dot(wm, vn[rows, cols], preferred_element_type=F32) + gb_ref[:, hd:hd + 1]
                y_ref[rows, 3 * BW + hd * GW:3 * BW + (hd + 1) * GW] = (u[rows, cols] * zz).astype(BF)

    return _pc(
        jobs, body, name="mix_branches_fwd", grid=(s_len // ts,),
        in_specs=_halo_specs(ts, MIXC, s_len, False) + _branch_weights_specs(),
        out_specs=_rows(ts, NBR * BW),
        out_shape=jax.ShapeDtypeStruct((s_len, NBR * BW), BF),
        scratch_shapes=[pltpu.VMEM((ext, MIXC), BF)],
        compiler_params=_params(1),
    )(p, p, pool_w, pool_scale, sconv_w, cconv_w, cln_g, cln_b, sln_g, sln_b, sgu_w, sgu_bt)


def _mix_branches_bwd(p, dy, dp, pool_w, pool_scale, sconv_w, cconv_w, cln_g, cln_b, sln_g, sln_b, sgu_w, sgu_bt, jobs=()):
    s_len = p.shape[0]
    ts = min(BRANCH_TILE, s_len)
    ext = ts + 2 * HALO
    last = s_len // ts - 1
    tile = slice(HALO, HALO + ts)
    small_shapes = [(NBR, GW, GW), (1, BW), (SKP, BW), (CKP, BW), (1, BW), (1, BW), (1, BW), (1, BW), (NBR, GW, GW),
                    (NBR, GW, GW)]

    def body(pp_ref, pc_ref, pn_ref, dyp_ref, dyc_ref, dyn_ref, dpin_ref,
             pw_ref, ps_ref, sw_ref, cw_ref, clg_ref, clb_ref, slg_ref, slb_ref, gw_ref, gb_ref,
             dp_ref, dpw_ref, dps_ref, dsw_ref, dcw_ref, dclg_ref, dclb_ref, dslg_ref, dslb_ref, dgw_ref, dgb_ref,
             pe_ref, de_ref):
        del dyp_ref, dpin_ref
        i = pl.program_id(0)
        first = i == 0
        _assemble(pe_ref, pp_ref, pc_ref, pn_ref, i, last, ts)
        de_ref[0:HALO, :] = jnp.zeros((HALO, NBR * BW), BF)
        de_ref[HALO:HALO + ts, :] = dyc_ref[...]
        de_ref[HALO + ts:, :] = jnp.where(i < last, dyn_ref[...], jnp.zeros_like(dyn_ref))
        t = i * ts - HALO + lax.broadcasted_iota(jnp.int32, (ext, 1), 0)

        @pl.when(first)
        def _():
            dsw_ref[...] = jnp.zeros((SKP, BW), F32)
            dcw_ref[...] = jnp.zeros((CKP, BW), F32)

        dgs = _pool_diff(pe_ref[:, 0:BW].astype(F32), t, 1)
        dya = de_ref[:, 0:BW].astype(F32)
        dds = []
        for gi in range(NBR):
            cols = slice(gi * GW, (gi + 1) * GW)
            pw = pw_ref[gi].astype(BF)
            d_t = dgs[gi][tile].astype(BF)
            e = jnp.dot(d_t, pw, preferred_element_type=F32)
            _accumulate(dps_ref.at[:, cols], jnp.sum(dya[tile, cols] * e, axis=0, keepdims=True), first)
            de_g = (dya[:, cols] * ps_ref[:, cols]).astype(BF)
            _accumulate(dpw_ref.at[gi], lax.dot_general(d_t, de_g[tile], TN, preferred_element_type=F32), first)
            dds.append(lax.dot_general(de_g, pw, NT, preferred_element_type=F32))
        das = _pool_diff(jnp.concatenate(dds, axis=1), t, -1)
        for gi in range(NBR):
            dp_ref[:, gi * GW:(gi + 1) * GW] = das[gi][tile].astype(BF)

        xin = pe_ref[:, BW:2 * BW].astype(F32)
        bg = pe_ref[:, 2 * BW:3 * BW].astype(F32)
        cg = pe_ref[:, 3 * BW:4 * BW].astype(F32)
        q = cg * xin
        qs = [q, _shift(q, 1), _shift(q, 2)]
        cv = sw_ref[2:3, :] * qs[0] + sw_ref[1:2, :] * qs[1] + sw_ref[0:1, :] * qs[2]
        dyb = de_ref[:, BW:2 * BW].astype(F32)
        dcv = dyb * bg
        for j in range(SK):
            dsw_ref[SK - 1 - j:SK - j, :] += jnp.sum(dcv[tile] * qs[j][tile], axis=0, keepdims=True)
        dq = sw_ref[2:3, :] * dcv + sw_ref[1:2, :] * _shift(dcv, -1) + sw_ref[0:1, :] * _shift(dcv, -2)
        dp_ref[:, BW:2 * BW] = (dq * cg)[tile].astype(BF)
        dp_ref[:, 2 * BW:3 * BW] = (dyb * cv)[tile].astype(BF)
        dp_ref[:, 3 * BW:4 * BW] = (dq * xin)[tile].astype(BF)

        ca = pe_ref[:, 4 * BW:5 * BW].astype(F32)
        sb = _sig(pe_ref[:, 5 * BW:6 * BW].astype(F32))
        yg = ca * sb
        z = cw_ref[CK - 1:CK, :] * yg
        for j in range(1, CK):
            z = z + cw_ref[CK - 1 - j:CK - j, :] * _shift(yg, j)
        rs, zn = _ln_stats(z)
        nn = zn * clg_ref[...] + clb_ref[...]
        sn = _sig(nn)
        dn = de_ref[:, 2 * BW:3 * BW].astype(F32) * (sn * (1.0 + nn * (1.0 - sn)))
        _accumulate(dclg_ref, jnp.sum((dn * zn)[tile], axis=0, keepdims=True), first)
        _accumulate(dclb_ref, jnp.sum(dn[tile], axis=0, keepdims=True), first)
        dz = _ln_bwd(rs, zn, dn * clg_ref[...])
        dz = jnp.where(t >= i * ts, dz, 0.0)
        yg_t = yg[tile]
        dyg = cw_ref[CK - 1:CK, :] * dz
        dcw_ref[CK - 1:CK, :] += jnp.sum(dz[tile] * yg_t, axis=0, keepdims=True)
        for j in range(1, CK):
            dz_ahead = _shift(dz, -j)
            dyg = dyg + cw_ref[CK - 1 - j:CK - j, :] * dz_ahead
            dcw_ref[CK - 1 - j:CK - j, :] += jnp.sum(dz_ahead[tile] * yg_t, axis=0, keepdims=True)
        dp_ref[:, 4 * BW:5 * BW] = (dyg * sb)[tile].astype(BF)
        dp_ref[:, 5 * BW:6 * BW] = (dyg * ca * sb * (1.0 - sb))[tile].astype(BF)

        pu = pc_ref[:, 6 * BW:7 * BW].astype(F32)
        pv = pc_ref[:, 7 * BW:8 * BW].astype(F32)
        u, thu = _gelu(pu)
        v, thv = _gelu(pv)
        vrs, vn0 = _ln_stats(v)
        vn = (vn0 * slg_ref[...] + slb_ref[...]).astype(BF)
        dyd = dyc_ref[:, 3 * BW:4 * BW].astype(F32)
        dzz = dyd * u
        dzb = dzz.astype(BF)
        mask = _sgu_mask()
        dvn_cols = []
        for hd in range(NBR):
            cols = slice(hd * GW, (hd + 1) * GW)
            wm = jnp.where(mask, gw_ref[hd], 0.0).astype(BF)
            dwm = jnp.zeros((GW, GW), F32)
            dbs = jnp.zeros((GW, GW), F32)
            dvn_rows = []
            for blk in range(ts // GW):
                rows = slice(blk * GW, (blk + 1) * GW)
                zz = jnp.dot(wm, vn[rows, cols], preferred_element_type=F32) + gb_ref[:, hd:hd + 1]
                dp_ref[rows, 6 * BW + hd * GW:6 * BW + (hd + 1) * GW] = (
                    dyd[rows, cols] * zz * _gelu_grad(pu[rows, cols], thu[rows, cols])).astype(BF)
                dwm = dwm + lax.dot_general(dzb[rows, cols], vn[rows, cols], NT, preferred_element_type=F32)
                dbs = dbs + dzz[rows, cols]
                dvn_rows.append(lax.dot_general(wm, dzb[rows, cols], TN, preferred_element_type=F32))
            _accumulate(dgw_ref.at[hd], jnp.where(mask, dwm, 0.0), first)
            _accumulate(dgb_ref.at[hd], dbs, first)
            dvn_cols.append(jnp.concatenate(dvn_rows, axis=0))
        dvn = jnp.concatenate(dvn_cols, axis=1)
        _accumulate(dslg_ref, jnp.sum(dvn * vn0, axis=0, keepdims=True), first)
        _accumulate(dslb_ref, jnp.sum(dvn, axis=0, keepdims=True), first)
        dv = _ln_bwd(vrs, vn0, dvn * slg_ref[...])
        dp_ref[:, 7 * BW:8 * BW] = (dv * _gelu_grad(pv, thv)).astype(BF)

    const = lambda shp: pl.BlockSpec(shp, lambda i: (0,) * len(shp))
    return _pc(
        jobs, body, name="mix_branches_bwd", grid=(s_len // ts,),
        in_specs=(_halo_specs(ts, MIXC, s_len, True) + _halo_specs(ts, NBR * BW, s_len, True)
                  + [pl.BlockSpec(memory_space=pl.ANY)] + _branch_weights_specs()),
        out_specs=[pl.BlockSpec((ts, MIXC), lambda i: (i, 0))] + [const(s) for s in small_shapes],
        out_shape=[jax.ShapeDtypeStruct((s_len, INC), BF)] + [jax.ShapeDtypeStruct(s, F32) for s in small_shapes],
        scratch_shapes=[pltpu.VMEM((ext, MIXC), BF), pltpu.VMEM((ext, NBR * BW), BF)],
        input_output_aliases={6: 0},
        compiler_params=_params(1),
    )(p, p, p, dy, dy, dy, dp, pool_w, pool_scale, sconv_w, cconv_w, cln_g, cln_b, sln_g, sln_b, sgu_w, sgu_bt)


def _mix_out_fwd(x, y, p, w_up, w_out, jobs=()):
    s_len = x.shape[0]
    ts = min(TOKEN_TILE, s_len)

    def body(x_ref, y_ref, pg_ref, wu_ref, wo_ref, xo_ref, m_ref, up_ref):
        m = jnp.zeros((ts, D), F32)
        for gi in range(NBR):
            up = jnp.dot(y_ref[:, gi * BW:(gi + 1) * BW], wu_ref[gi], preferred_element_type=F32)
            up_ref[:, gi * D:(gi + 1) * D] = up.astype(BF)
            m = m + _sig(pg_ref[:, gi * D:(gi + 1) * D].astype(F32)) * up
        mb = m.astype(BF)
        m_ref[...] = mb
        xo_ref[...] = x_ref[...] + jnp.dot(mb, wo_ref[...], preferred_element_type=F32)

    return _pc(
        jobs, body, name="mix_out_fwd", grid=(s_len // ts,),
        in_specs=[_rows(ts, D), _rows(ts, NBR * BW), _rows(ts, NBR * D, 1), _full((NBR, BW, D)), _full((D, D))],
        out_specs=[_rows(ts, D), _rows(ts, D), _rows(ts, NBR * D)],
        out_shape=[jax.ShapeDtypeStruct((s_len, D), F32), jax.ShapeDtypeStruct((s_len, D), BF),
                   jax.ShapeDtypeStruct((s_len, NBR * D), BF)],
        compiler_params=_params(1),
    )(x, y, p, w_up, w_out)


def _mix_out_bwd(dxo, up, p, w_up, w_out, jobs=()):
    s_len = dxo.shape[0]
    ts = min(TOKEN_TILE, s_len)

    def body(dxo_ref, up_ref, pg_ref, wu_ref, wo_ref, dy_ref, dp_ref, dup_ref, dxb_ref):
        dxb = dxo_ref[...].astype(BF)
        dxb_ref[...] = dxb
        dm = lax.dot_general(dxb, wo_ref[...], NT, preferred_element_type=F32)
        for gi in range(NBR):
            cols = slice(gi * D, (gi + 1) * D)
            gate = _sig(pg_ref[:, cols].astype(F32))
            dp_ref[:, cols] = (dm * up_ref[:, cols].astype(F32) * gate * (1.0 - gate)).astype(BF)
            dup = (dm * gate).astype(BF)
            dup_ref[:, cols] = dup
            dy_ref[:, gi * BW:(gi + 1) * BW] = lax.dot_general(
                dup, wu_ref[gi], NT, preferred_element_type=F32).astype(BF)

    return _pc(
        jobs, body, name="mix_out_bwd", grid=(s_len // ts,),
        in_specs=[_rows(ts, D), _rows(ts, NBR * D), _rows(ts, NBR * D, 1), _full((NBR, BW, D)), _full((D, D))],
        out_specs=[_rows(ts, NBR * BW), _rows(ts, NBR * D, 1), _rows(ts, NBR * D), _rows(ts, D)],
        out_shape=[jax.ShapeDtypeStruct((s_len, NBR * BW), BF), jax.ShapeDtypeStruct((s_len, INC), BF),
                   jax.ShapeDtypeStruct((s_len, NBR * D), BF), jax.ShapeDtypeStruct((s_len, D), BF)],
        compiler_params=_params(1),
    )(dxo, up, p, w_up, w_out)


def _mix_in_bwd(dp, w_in, x, g, dxo, jobs=()):
    s_len = x.shape[0]
    ts = min(TOKEN_TILE, s_len)

    def body(dp_ref, w_ref, x_ref, g_ref, dxo_ref, dxi_ref, dg_ref):
        i = pl.program_id(0)
        dh = lax.dot_general(dp_ref[...], w_ref[...], NT, preferred_element_type=F32)
        dx, dg = _rms_bwd(x_ref[...], g_ref[...], dh)
        dxi_ref[...] = dxo_ref[...] + dx
        _accumulate(dg_ref, dg, i == 0)

    return _pc(
        jobs, body, name="mix_in_bwd", grid=(s_len // ts,),
        in_specs=[_rows(ts, INC), _full((D, INC)), _rows(ts, D), _full((1, D)), _rows(ts, D)],
        out_specs=[_rows(ts, D), pl.BlockSpec((1, D), lambda i: (0, 0))],
        out_shape=[jax.ShapeDtypeStruct((s_len, D), F32), jax.ShapeDtypeStruct((1, D), F32)],
        compiler_params=_params(1),
    )(dp, w_in, x, g, dxo)


def _loss_head(x, g, target, jobs=()):
    s_len = x.shape[0]
    ts = min(512, s_len)

    def body(x_ref, g_ref, t_ref, dx_ref, dg_ref, loss_ref):
        i = pl.program_id(0)
        xv = x_ref[...]
        _, xh = _rms_stats(xv)
        err = xh * g_ref[...] - t_ref[...]
        part = 0.5 * jnp.sum(jnp.mean(err * err, axis=-1, keepdims=True), axis=0, keepdims=True)
        dx, dg = _rms_bwd(xv, g_ref[...], err * (1.0 / D))
        dx_ref[...] = dx
        _accumulate(dg_ref, dg, i == 0)
        _accumulate(loss_ref, jnp.broadcast_to(part, (1, GW)), i == 0)

    return _pc(
        jobs, body, name="loss_head", grid=(s_len // ts,),
        in_specs=[_rows(ts, D), _full((1, D)), _rows(ts, D)],
        out_specs=[_rows(ts, D), pl.BlockSpec((1, D), lambda i: (0, 0)), pl.BlockSpec((1, GW), lambda i: (0, 0))],
        out_shape=[jax.ShapeDtypeStruct((s_len, D), F32), jax.ShapeDtypeStruct((1, D), F32),
                   jax.ShapeDtypeStruct((1, GW), F32)],
        compiler_params=_params(1),
    )(x, g, target)


SUM_TILE = 1 << 20
ADAM_TILE = 1 << 19


def _row_tile(rows, cols, budget=1 << 18):
    tr = rows
    while tr * cols > budget and tr % 16 == 0:
        tr //= 2
    return tr


def _elementwise(fn, name, ins, out_dtypes):
    rows, cols = ins[0].shape
    tr = _row_tile(rows, cols)
    n_in = len(ins)

    def body(*refs):
        res = fn(*[r[...] for r in refs[:n_in]])
        for o_ref, val in zip(refs[n_in:], res):
            o_ref[...] = val.astype(o_ref.dtype)

    outs = pl.pallas_call(
        body, name=name, grid=(rows // tr,),
        in_specs=[_rows(tr, cols)] * n_in, out_specs=[_rows(tr, cols)] * len(out_dtypes),
        out_shape=[jax.ShapeDtypeStruct((rows, cols), dt) for dt in out_dtypes],
        compiler_params=_params(1),
    )(*ins)
    return outs


def _tiled(fn, name, grid, in_specs, out_specs, out_shape, args, scalars=None, alias=None):
    alias = alias or {}
    n_in = len(in_specs) - len(alias)
    n_pre = 0 if scalars is None else 1

    def body(*refs):
        refs = refs[n_pre:]
        res = fn(*[r[...] for r in refs[:n_in]])
        for o_ref, val in zip(refs[len(in_specs):], res):
            o_ref[...] = val.astype(o_ref.dtype)

    aliases = {n_pre + pos: out for pos, out in alias.items()}
    if scalars is None:
        return pl.pallas_call(body, name=name, grid=grid, in_specs=in_specs, out_specs=out_specs, out_shape=out_shape,
                              input_output_aliases=aliases, compiler_params=_params(len(grid)))(*args)
    spec = pltpu.PrefetchScalarGridSpec(num_scalar_prefetch=1, grid=grid, in_specs=in_specs, out_specs=out_specs)
    return pl.pallas_call(body, name=name, grid_spec=spec, out_shape=out_shape, input_output_aliases=aliases,
                          compiler_params=_params(len(grid)))(scalars, *args)


def _cast_into(shard, layer, col, dtype, sc):
    ks, ns = shard.shape[1:]
    tr = _row_tile(ks, ns, SUM_TILE)
    full = (ks, ns * NCHIP) if col else (ks * NCHIP, ns)
    out_idx = (lambda i, s: (i, s[1])) if col else (lambda i, s: (s[1] * (ks // tr) + i, 0))
    return _tiled(lambda v: (v,), "cast_into", (ks // tr,), [pl.BlockSpec((None, tr, ns), lambda i, s: (layer, i, 0))],
                  [pl.BlockSpec((tr, ns), out_idx)], [jax.ShapeDtypeStruct(full, dtype)], [shard], sc)[0]


def _pair_sum(g, got, col, sc):
    hk, hn = got.shape
    tr = _row_tile(hk, hn, SUM_TILE)
    g_idx = (lambda i, s: (s[0] * (hk // tr) + i, 0)) if col else (lambda i, s: (i, s[0]))
    plain = pl.BlockSpec((tr, hn), lambda i, s: (i, 0))
    return _tiled(lambda a, b: (a.astype(F32) + b.astype(F32),), "pair_sum", (hk // tr,),
                  [pl.BlockSpec((tr, hn), g_idx), plain], [plain], [jax.ShapeDtypeStruct((hk, hn), BF)], [g, got], sc)[0]


def _chip_sum(ph, got, col, layer, depth, sc, carry):
    qk, qn = got.shape[1:]
    tr = _row_tile(qk, qn, SUM_TILE)
    ph_idx = (lambda i, s: (i, s[1])) if col else (lambda i, s: (s[1] * (qk // tr) + i, 0))
    out_shape = (depth, 2 * qk, qn) if col else (depth, qk, 2 * qn)
    out_idx = (lambda i, s: (layer, s[0] * (qk // tr) + i, 0)) if col else (lambda i, s: (layer, i, s[0]))
    in_specs = [pl.BlockSpec((tr, qn), ph_idx)] + [pl.BlockSpec((None, tr, qn), lambda i, s, j=j: (j, i, 0)) for j in range(3)]
    args = [ph, got, got, got]
    if carry is not None:
        in_specs.append(ANY)
        args.append(carry)
    return _tiled(lambda a, b, c_, d_: (a.astype(F32) + b.astype(F32) + c_.astype(F32) + d_.astype(F32),), "chip_sum",
                  (qk // tr,), in_specs, [pl.BlockSpec((None, tr, qn), out_idx)],
                  [jax.ShapeDtypeStruct(out_shape, F32)], args, sc, alias=None if carry is None else {4: 0})[0]


def _adamw_math(w, g, m, v):
    m = ADAM_B1 * m + (1.0 - ADAM_B1) * g
    v = ADAM_B2 * v + (1.0 - ADAM_B2) * (g * g)
    m_hat = m / (1.0 - ADAM_B1 ** ADAM_STEP)
    v_hat = v / (1.0 - ADAM_B2 ** ADAM_STEP)
    delta = -ADAM_LR * (m_hat / (jnp.sqrt(v_hat) + ADAM_EPS) + ADAM_WD * w)
    return delta, m, v


def _adamw_layer(w, g, m, v, layer, carry):
    k, n = w.shape[1:]
    tr = _row_tile(k, n, ADAM_TILE)
    blk = pl.BlockSpec((None, tr, n), lambda i: (layer, i, 0))
    carry = list(carry or [])
    return _tiled(_adamw_math, "adamw_layer", (k // tr,), [blk] * 4 + [ANY] * len(carry), [blk] * 3,
                  [jax.ShapeDtypeStruct(w.shape, F32)] * 3, [w, g, m, v] + carry, None,
                  alias={4 + pos: pos for pos in range(len(carry))})


def _hand_through(arrays):
    n = len(arrays)
    return pl.pallas_call(
        lambda *refs: None, name="hand_through", in_specs=[ANY] * n, out_specs=[ANY] * n,
        out_shape=[jax.ShapeDtypeStruct(a.shape, a.dtype) for a in arrays],
        input_output_aliases={k: k for k in range(n)})(*arrays)


def _adamw(w, g, m, v):
    shape = w.shape
    two_d = lambda a: a.reshape(-1, shape[-1])
    outs = _elementwise(_adamw_math, "adamw", [two_d(w), two_d(g), two_d(m), two_d(v)], [F32, F32, F32])
    return [o.reshape(shape) for o in outs]


ANY = pl.BlockSpec(memory_space=pl.ANY)


def _place():
    x, y, c = lax.axis_index("x"), lax.axis_index("y"), lax.axis_index("c")
    chips = [(1 - x, y), (x, 1 - y), (1 - x, 1 - y)]
    return x, y, c, chips


def _cols(ref, start, size):
    idx = (slice(None),) * (len(ref.shape) - 1) + (pl.ds(pl.multiple_of(start, GW), size),)
    return ref.at[idx]


def _rows_of(ref, start, size):
    nd = len(ref.shape)
    idx = (slice(None),) * (nd - 2) + (pl.ds(pl.multiple_of(start, 16), size), slice(None))
    return ref.at[idx]


def _region(ref, col_sharded, chip, half):
    k, n = ref.shape
    align = 16 if ref.dtype == BF else 8
    if col_sharded:
        return ref.at[pl.ds(pl.multiple_of(half * (k // 2), align), k // 2),
                      pl.ds(pl.multiple_of(chip * (n // NCHIP), GW), n // NCHIP)]
    rows = k // (2 * NCHIP)
    return ref.at[pl.ds(pl.multiple_of((2 * chip + half) * rows, align), rows), :]


def _job_gather(bufs, col_sharded, handoff):
    n = len(bufs)

    def copies(outs, send_sems, recv_sems, stage):
        x, y, c, chips = _place()
        sends, lands = [], []
        for k in range(n):
            for j, chip in enumerate(chips):
                theirs = 2 * chip[0] + chip[1]
                if stage == 0:
                    src, to = _region(outs[k], col_sharded[k], 2 * x + y, c), (*chip, c)
                    land = _region(outs[k], col_sharded[k], theirs, c)
                else:
                    src, to = _region(outs[k], col_sharded[k], theirs, c), (x, y, 1 - c)
                    land = _region(outs[k], col_sharded[k], theirs, 1 - c)
                sem = 3 * n * stage + 3 * k + j
                sems = dict(send_sem=send_sems.at[sem], recv_sem=recv_sems.at[sem], device_id=to, device_id_type=MESH)
                sends.append(pltpu.make_async_remote_copy(src_ref=src, dst_ref=src, **sems))
                lands.append(pltpu.make_async_remote_copy(src_ref=land, dst_ref=land, **sems))
        return sends, lands

    def start(ins, outs, send_sems, recv_sems):
        for cp in copies(outs, send_sems, recv_sems, 0)[0]:
            cp.start()

    def hand_on(ins, outs, send_sems, recv_sems):
        for cp in copies(outs, send_sems, recv_sems, 0)[1]:
            cp.wait_recv()
        for cp in copies(outs, send_sems, recv_sems, 1)[0]:
            cp.start()

    def finish(ins, outs, send_sems, recv_sems):
        sends, lands = copies(outs, send_sems, recv_sems, 1)
        for cp in lands:
            cp.wait_recv()
        for cp in copies(outs, send_sems, recv_sems, 0)[0] + sends:
            cp.wait_send()

    return _Job(bufs, n, [], 6 * n, [(0.0, start), (handoff, hand_on), (1.0, finish)])


def _half(ref, col_sharded, c):
    k, n = ref.shape[-2:]
    return _rows_of(ref, c * (k // 2), k // 2) if col_sharded else _cols(ref, c * (n // 2), n // 2)


def _quarter(ref, col_sharded, j):
    k, n = ref.shape[-2:]
    return _cols(ref, j * (n // NCHIP), n // NCHIP) if col_sharded else _rows_of(ref, j * (k // NCHIP), k // NCHIP)


def _job_pair(grads, col_sharded):
    n = len(grads)

    def half_shape(g, col):
        return (g.shape[0] // 2, g.shape[1]) if col else (g.shape[0], g.shape[1] // 2)

    def copies(ins, got, send_sems, recv_sems):
        x, y, c, _ = _place()
        return [pltpu.make_async_remote_copy(
            src_ref=_half(ins[k], col_sharded[k], 1 - c), dst_ref=got[k], send_sem=send_sems.at[k],
            recv_sem=recv_sems.at[k], device_id=(x, y, 1 - c), device_id_type=MESH) for k in range(n)]

    def start(*refs):
        for cp in copies(*refs):
            cp.start()

    def finish(*refs):
        for cp in copies(*refs):
            cp.wait()

    fresh = [jax.ShapeDtypeStruct(half_shape(g, col), g.dtype) for g, col in zip(grads, col_sharded)]
    return _Job(grads, 0, fresh, n, [(0.0, start), (1.0, finish)])


def _job_chip(halves, col_sharded):
    n = len(halves)

    def quarter_shape(h, col):
        return (3, h.shape[0], h.shape[1] // NCHIP) if col else (3, h.shape[0] // NCHIP, h.shape[1])

    def copies(ins, got, send_sems, recv_sems):
        x, y, c, chips = _place()
        return [pltpu.make_async_remote_copy(
            src_ref=_quarter(ins[k], col_sharded[k], 2 * chip[0] + chip[1]), dst_ref=got[k].at[j],
            send_sem=send_sems.at[3 * k + j], recv_sem=recv_sems.at[3 * k + j], device_id=(*chip, c), device_id_type=MESH)
            for k in range(n) for j, chip in enumerate(chips)]

    def start(*refs):
        for cp in copies(*refs):
            cp.start()

    def finish(*refs):
        for cp in copies(*refs):
            cp.wait()

    fresh = [jax.ShapeDtypeStruct(quarter_shape(h, col), h.dtype) for h, col in zip(halves, col_sharded)]
    return _Job(halves, 0, fresh, 3 * n, [(0.0, start), (1.0, finish)])


def _job_sibling(shards, col_sharded, layers):
    n = len(shards)

    def copies(outs, send_sems, recv_sems):
        x, y, c, _ = _place()
        sends, lands = [], []
        for k in range(n):
            sems = dict(send_sem=send_sems.at[k], recv_sem=recv_sems.at[k], device_id=(x, y, 1 - c), device_id_type=MESH)
            mine = _half(outs[k].at[layers[k]], col_sharded[k], c)
            theirs = _half(outs[k].at[layers[k]], col_sharded[k], 1 - c)
            sends.append(pltpu.make_async_remote_copy(src_ref=mine, dst_ref=mine, **sems))
            lands.append(pltpu.make_async_remote_copy(src_ref=theirs, dst_ref=theirs, **sems))
        return sends, lands

    def start(ins, outs, send_sems, recv_sems):
        for cp in copies(outs, send_sems, recv_sems)[0]:
            cp.start()

    def finish(ins, outs, send_sems, recv_sems):
        sends, lands = copies(outs, send_sems, recv_sems)
        for cp in lands:
            cp.wait_recv()
        for cp in sends:
            cp.wait_send()

    return _Job(shards, n, [], n, [(0.0, start), (1.0, finish)])


def _standalone(jobs, name):
    return _pc(jobs, lambda: None, name=name, in_specs=[], out_specs=[], out_shape=[])()[1]


def _all_reduce_small(buf, jobs=()):
    rows = buf.shape[0]
    per = rows // NDEV
    flips = [(fx, fy, fc) for fx in (0, 1) for fy in (0, 1) for fc in (0, 1)][1:]

    def body(in_ref, out_ref, got_ref, send_sems, recv_sems):
        x, y, c, _ = _place()
        me = 4 * x + 2 * y + c

        def peer(f):
            return tuple(1 - pos if flip else pos for pos, flip in zip((x, y, c), f))

        def block(ref, dev):
            return ref.at[pl.ds(pl.multiple_of(dev * per, 8), per), :]

        scatter = []
        for k, f in enumerate(flips):
            px, py, pc = peer(f)
            scatter.append(pltpu.make_async_remote_copy(
                src_ref=block(in_ref, 4 * px + 2 * py + pc), dst_ref=got_ref.at[k], send_sem=send_sems.at[k],
                recv_sem=recv_sems.at[k], device_id=(px, py, pc), device_id_type=MESH))
        for cp in scatter:
            cp.start()
        for cp in scatter:
            cp.wait()
        total = block(in_ref, me)[...]
        for k in range(len(flips)):
            total = total + got_ref[k]
        block(out_ref, me)[...] = total
        share = []
        for k, f in enumerate(flips):
            share.append(pltpu.make_async_remote_copy(
                src_ref=block(out_ref, me), dst_ref=block(out_ref, me), send_sem=send_sems.at[7 + k],
                recv_sem=recv_sems.at[7 + k], device_id=peer(f), device_id_type=MESH))
        for cp in share:
            cp.start()
        for k, f in enumerate(flips):
            share[k].wait_send()
            px, py, pc = peer(f)
            theirs = block(out_ref, 4 * px + 2 * py + pc)
            pltpu.make_async_remote_copy(
                src_ref=theirs, dst_ref=theirs, send_sem=send_sems.at[7 + k], recv_sem=recv_sems.at[7 + k],
                device_id=(px, py, pc), device_id_type=MESH).wait_recv()

    vmem = pl.BlockSpec(memory_space=pltpu.VMEM)
    return _pc(
        jobs, body, name="all_reduce_small", in_specs=[vmem], out_specs=vmem,
        out_shape=jax.ShapeDtypeStruct((rows, GW), F32),
        scratch_shapes=[pltpu.VMEM((NDEV - 1, per, GW), F32), pltpu.SemaphoreType.DMA((14,)),
                        pltpu.SemaphoreType.DMA((14,))],
    )(buf)


BIG = ("ffn1_w13", "ffn1_w2", "w_in", "w_up", "w_out", "ffn2_w13", "ffn2_w2")
BIG_COL_SHARDED = (True, False, True, True, False, True, False)
SMALL = ("ffn1_norm", "mix_norm", "pool_w", "pool_scale", "sconv_w", "cconv_w", "cconv_ln_g", "cconv_ln_b",
         "sgu_ln_g", "sgu_ln_b", "sgu_w", "sgu_b", "ffn2_norm", "final_norm")
WEIGHTS = ("ffn1_norm", "ffn1_w13", "ffn1_w2", "mix_norm", "w_in", "pool_w", "pool_scale", "sconv_w", "cconv_w",
           "cconv_ln_g", "cconv_ln_b", "sgu_ln_g", "sgu_ln_b", "sgu_w", "sgu_b", "w_up", "w_out", "ffn2_norm",
           "ffn2_w13", "ffn2_w2", "final_norm")


def _pad_rows(a, rows):
    return jnp.pad(a, ((0, 0), (0, rows - a.shape[1]), (0, 0)))


def kernel(x, ffn1_norm, ffn1_w13, ffn1_w2, mix_norm, w_in, pool_w, pool_scale, sconv_w, cconv_w, cconv_ln_g, cconv_ln_b, sgu_ln_g, sgu_ln_b, sgu_w, sgu_b, w_up, w_out, ffn2_norm, ffn2_w13, ffn2_w2, final_norm, loss_target, m_ffn1_norm, m_ffn1_w13, m_ffn1_w2, m_mix_norm, m_w_in, m_pool_w, m_pool_scale, m_sconv_w, m_cconv_w, m_cconv_ln_g, m_cconv_ln_b, m_sgu_ln_g, m_sgu_ln_b, m_sgu_w, m_sgu_b, m_w_up, m_w_out, m_ffn2_norm, m_ffn2_w13, m_ffn2_w2, m_final_norm, v_ffn1_norm, v_ffn1_w13, v_ffn1_w2, v_mix_norm, v_w_in, v_pool_w, v_pool_scale, v_sconv_w, v_cconv_w, v_cconv_ln_g, v_cconv_ln_b, v_sgu_ln_g, v_sgu_ln_b, v_sgu_w, v_sgu_b, v_w_up, v_w_out, v_ffn2_norm, v_ffn2_w13, v_ffn2_w2, v_final_norm):
    args = dict(locals())
    w = {nm: args[nm] for nm in WEIGHTS}
    m = {nm: args["m_" + nm] for nm in WEIGHTS}
    v = {nm: args["v_" + nm] for nm in WEIGHTS}
    depth = ffn1_w13.shape[0]
    chip = 2 * lax.axis_index("x") + lax.axis_index("y")

    sc = jnp.stack([lax.axis_index("c"), chip]).astype(jnp.int32)
    col_of = dict(zip(BIG, BIG_COL_SHARDED), sconv_w=True, cconv_w=True)
    sources = {nm: (w[nm], BF) for nm in BIG}
    sources["w_up"] = (w_up.reshape(depth, NBR * BW, w_up.shape[-1]), BF)
    sources["sconv_w"] = (_pad_rows(sconv_w, 2 * SKP), F32)
    sources["cconv_w"] = (_pad_rows(cconv_w, 2 * CKP), F32)
    full = [{nm: _cast_into(src, l, col_of[nm], dt, sc) for nm, (src, dt) in sources.items()} for l in range(depth)]

    def gather_job(l, names, handoff):
        return _job_gather([full[l][nm] for nm in names], [col_of[nm] for nm in names], handoff)

    def gather_with(l, names, handoff, call):
        res, job_outs = call([gather_job(l, names, handoff)] if l < depth else [])
        if l < depth:
            full[l].update(zip(names, job_outs[0]))
        return res

    first_group = ("ffn1_w13", "ffn1_w2")
    full[0].update(zip(first_group, _standalone([gather_job(0, first_group, 0.0)], "gather_first")[0]))

    xs = x[0]
    row = lambda a: a.reshape(1, -1)
    saved = []
    for l in range(depth):
        g1, gm, g2 = row(ffn1_norm[l]), row(mix_norm[l]), row(ffn2_norm[l])
        x1, h1, ab1 = gather_with(l, ("w_in",), 0.85, lambda jobs: _ffn_fwd(
            xs, g1, full[l]["ffn1_w13"], full[l]["ffn1_w2"], jobs))
        hm, p = gather_with(l, ("w_up", "w_out", "sconv_w", "cconv_w", "ffn2_w2"), 0.75, lambda jobs: _mix_in(
            x1, gm, full[l]["w_in"], jobs))
        branch = (pool_w[l], row(pool_scale[l]), full[l]["sconv_w"][:SKP], full[l]["cconv_w"][:CKP], row(cconv_ln_g[l]),
                  row(cconv_ln_b[l]), row(sgu_ln_g[l]), row(sgu_ln_b[l]), sgu_w[l], sgu_b[l].T)
        y = gather_with(l, ("ffn2_w13",), 0.85, lambda jobs: _mix_branches_fwd(p, *branch, jobs=jobs))
        w_up_l = full[l]["w_up"].reshape(NBR, BW, D)
        x2, merged, up = gather_with(l + 1, ("ffn1_w2",), 0.7, lambda jobs: _mix_out_fwd(
            x1, y, p, w_up_l, full[l]["w_out"], jobs))
        x3, h2, ab2 = gather_with(l + 1, ("ffn1_w13",), 0.75, lambda jobs: _ffn_fwd(
            x2, g2, full[l]["ffn2_w13"], full[l]["ffn2_w2"], jobs))
        lw = dict(g1=g1, gm=gm, g2=g2, w13a=full[l]["ffn1_w13"], w2a=full[l]["ffn1_w2"], w13b=full[l]["ffn2_w13"],
                  w2b=full[l]["ffn2_w2"], w_in=full[l]["w_in"], w_up=w_up_l, w_out=full[l]["w_out"], branch=branch)
        saved.append(dict(lw=lw, x0=xs, x1=x1, x2=x2, h1=h1, ab1=ab1, hm=hm, p=p, y=y, merged=merged, up=up, h2=h2,
                          ab2=ab2))
        xs = x3

    (dx, d_final, loss_part), _ = _loss_head(xs, row(final_norm), loss_target[0])
    loss = lax.psum(loss_part[0, 0], ("x", "y", "c"))

    ici_us = dict(ffn1_w13=64, ffn1_w2=32, w_in=93, w_up=23, w_out=12, ffn2_w13=64, ffn2_w2=32)
    parts, pair_sums, reduced, big_updates, pending = {}, {}, {}, {}, []

    def take_jobs(budget_us):
        chosen = []
        for task in list(pending):
            kind, (nm, _) = task
            if kind == "chip":
                if ici_us[nm] > budget_us:
                    continue
                budget_us -= ici_us[nm]
            if kind == "sib" and any(k == "sib" and key[0] == nm for k, key in chosen):
                continue
            chosen.append(task)
            pending.remove(task)
        groups, jobs = [], []
        for kind in ("pair", "sib", "chip"):
            keys = [key for k, key in chosen if k == kind]
            if not keys:
                continue
            cols = [col_of[nm] for nm, _ in keys]
            groups.append((kind, keys))
            if kind == "pair":
                jobs.append(_job_pair([parts[key] for key in keys], cols))
            elif kind == "chip":
                jobs.append(_job_chip([pair_sums[key] for key in keys], cols))
            else:
                jobs.append(_job_sibling([reduced[nm] for nm, _ in keys], cols, [layer for _, layer in keys]))
        return groups, jobs

    def settle(groups, job_outs):
        for (kind, keys), outs in zip(groups, job_outs):
            for key, out in zip(keys, outs):
                nm, layer = key
                if kind == "pair":
                    pair_sums[key] = _pair_sum(parts[key], out, col_of[nm], sc)
                    pending.append(("chip", key))
                elif kind == "chip":
                    assert not any(k == "sib" and other[0] == nm for k, other in pending)
                    reduced[nm] = _chip_sum(pair_sums[key], out, col_of[nm], layer, depth, sc, reduced.get(nm))
                    pending.append(("sib", key))
                else:
                    reduced[nm] = out
                    as3 = lambda a: a.reshape(out.shape)
                    big_updates[nm] = _adamw_layer(as3(w[nm]), out, as3(m[nm]), as3(v[nm]), layer, big_updates.get(nm))

    def run(budget_us, call, carrier=True):
        groups, jobs = take_jobs(budget_us) if carrier else ([], [])
        res, job_outs = call(jobs)
        settle(groups, job_outs)
        return res

    def wgrad_done(key, partial):
        parts[key] = partial
        pending.append(("pair", key))

    small_parts = {nm: [None] * depth for nm in SMALL if nm != "final_norm"}
    for l in reversed(range(depth)):
        sv = saved[l]
        lw = sv["lw"]
        dx, dab, s_act, dyh, dg2 = run(108, lambda jobs: _ffn_bwd(
            dx, sv["x2"], lw["g2"], sv["ab2"], lw["w13b"], lw["w2b"], jobs))
        wgrad_done(("ffn2_w13", l), run(58, lambda jobs: _wgrad(sv["h2"], dab, D, 512, "wgrad_w13", jobs)))
        wgrad_done(("ffn2_w2", l), run(33, lambda jobs: _wgrad(s_act, dyh, 256, D, "wgrad_w2", jobs)))
        small_parts["ffn2_norm"][l] = dg2

        dy, dp, dup, dxb = run(70, lambda jobs: _mix_out_bwd(dx, sv["up"], sv["p"], lw["w_up"], lw["w_out"], jobs))
        wgrad_done(("w_out", l), run(16, lambda jobs: _wgrad(sv["merged"], dxb, D, 512, "wgrad_w_out", jobs), False))
        wgrad_done(("w_up", l), run(25, lambda jobs: _wgrad_groups(sv["y"], dup, BW, D, "wgrad_w_up", jobs), False))
        (dp, d_pool_w, d_pool_scale, d_sconv, d_cconv, d_clg, d_clb, d_slg, d_slb, d_sgu_w, d_sgu_b) = run(
            214, lambda jobs: _mix_branches_bwd(sv["p"], dy, dp, *lw["branch"], jobs=jobs))
        wgrad_done(("w_in", l), run(80, lambda jobs: _wgrad(sv["hm"], dp, D, 512, "wgrad_w_in", jobs)))
        dx, dgm = run(91, lambda jobs: _mix_in_bwd(dp, lw["w_in"], sv["x1"], lw["gm"], dx, jobs))
        small_parts["mix_norm"][l] = dgm
        small_parts["pool_w"][l] = d_pool_w
        small_parts["pool_scale"][l] = d_pool_scale
        small_parts["sconv_w"][l] = d_sconv[:SK]
        small_parts["cconv_w"][l] = d_cconv[:CK]
        small_parts["cconv_ln_g"][l] = d_clg
        small_parts["cconv_ln_b"][l] = d_clb
        small_parts["sgu_ln_g"][l] = d_slg
        small_parts["sgu_ln_b"][l] = d_slb
        small_parts["sgu_w"][l] = d_sgu_w
        small_parts["sgu_b"][l] = jnp.sum(d_sgu_b, axis=-1)

        dx, dab, s_act, dyh, dg1 = run(108, lambda jobs: _ffn_bwd(
            dx, sv["x0"], lw["g1"], sv["ab1"], lw["w13a"], lw["w2a"], jobs))
        wgrad_done(("ffn1_w13", l), run(58, lambda jobs: _wgrad(sv["h1"], dab, D, 512, "wgrad_w13", jobs)))
        wgrad_done(("ffn1_w2", l), run(33, lambda jobs: _wgrad(s_act, dyh, 256, D, "wgrad_w2", jobs)))
        small_parts["ffn1_norm"][l] = dg1
    grad_x = dx[None]

    small_local = {nm: jnp.stack(parts).reshape(depth, *w[nm].shape[1:-1], -1) if nm not in ("sconv_w", "cconv_w")
                   else jnp.stack(parts) for nm, parts in small_parts.items()}
    small_local["final_norm"] = d_final.reshape(-1)
    sizes = [small_local[nm].size for nm in SMALL]
    total = sum(sizes)
    pad_to = NDEV * 8 * GW
    padded = -(-total // pad_to) * pad_to
    packed = jnp.concatenate([small_local[nm].reshape(-1) for nm in SMALL] + [jnp.zeros((padded - total,), F32)])
    groups, jobs = take_jobs(float("inf"))
    summed, job_outs = _all_reduce_small(packed.reshape(-1, GW), jobs)
    settle(groups, job_outs)
    summed = summed.reshape(-1)
    flushes = 0
    while pending:
        groups, jobs = take_jobs(float("inf"))
        settle(groups, _standalone(jobs, "grad_flush_%d" % flushes))
        flushes += 1
    big_grads = {nm: g.reshape(w[nm].shape) for nm, g in zip(BIG, _hand_through([reduced[nm] for nm in BIG]))}
    small_grads, off = {}, 0
    for nm, size in zip(SMALL, sizes):
        small_grads[nm] = summed[off:off + size].reshape(small_local[nm].shape)
        off += size
    for nm in ("sconv_w", "cconv_w"):
        small_grads[nm] = lax.dynamic_slice_in_dim(small_grads[nm], chip * GW, GW, axis=2)

    grads = {**big_grads, **small_grads}

    delta, new_m, new_v = {}, {}, {}
    for nm in BIG:
        delta[nm], new_m[nm], new_v[nm] = [a.reshape(w[nm].shape) for a in big_updates[nm]]
    s_sizes = [w[nm].size for nm in SMALL]
    s_total = sum(s_sizes)
    s_padded = -(-s_total // (8 * GW)) * (8 * GW)

    def pack(tree):
        return jnp.concatenate([tree[nm].reshape(-1) for nm in SMALL] + [jnp.ones((s_padded - s_total,), F32)]).reshape(-1, GW)

    packed_out = _adamw(pack(w), pack(grads), pack(m), pack(v))
    off = 0
    for nm, size in zip(SMALL, s_sizes):
        for tree, arr in zip((delta, new_m, new_v), packed_out):
            tree[nm] = arr.reshape(-1)[off:off + size].reshape(w[nm].shape)
        off += size

    return (loss, grad_x, *[grads[nm] for nm in WEIGHTS], *[delta[nm] for nm in WEIGHTS],
            *[new_m[nm] for nm in WEIGHTS], *[new_v[nm] for nm in WEIGHTS])
```

```python
import jax
import jax.numpy as jnp
from jax import lax
from jax.experimental import pallas as pl
from jax.experimental.pallas import tpu as pltpu

D = 1024
FF = 2816
BW = 512
NBR = 4
MIXC = 4096
INC = 8192
GW = 128
CHUNK = 64
SK = 3
CK = 31
SKP = 8
CKP = 32
HALO = 32
TOKEN_TILE = 256
BRANCH_TILE = 512
EPS = 1e-6
NCHIP = 4
NDEV = 8

ADAM_LR = 0.001
ADAM_B1 = 0.9
ADAM_B2 = 0.999
ADAM_EPS = 1e-08
ADAM_WD = 0.01
ADAM_STEP = 10

VMEM_LIMIT = 56 * 1024 * 1024

BF = jnp.bfloat16
F32 = jnp.float32
MESH = pl.DeviceIdType.MESH
NT = (((1,), (1,)), ((), ()))
TN = (((0,), (0,)), ((), ()))


def _params(n_axes):
    return pltpu.CompilerParams(dimension_semantics=("arbitrary",) * n_axes, vmem_limit_bytes=VMEM_LIMIT)


class _Job:
    def __init__(self, args, n_inplace, fresh, nsem, phases):
        self.args, self.n_inplace, self.fresh, self.nsem, self.phases = list(args), n_inplace, list(fresh), nsem, phases


def _pc(jobs, body, name, in_specs, out_specs, out_shape, grid=(), scratch_shapes=(), input_output_aliases=None,
        compiler_params=None, grid_spec_scalars=None):
    single = not isinstance(out_shape, (list, tuple))
    core_out_specs = [out_specs] if single else list(out_specs)
    core_out_shape = [out_shape] if single else list(out_shape)
    n_in, n_out, n_scr = len(in_specs), len(core_out_specs), len(scratch_shapes)
    n_pre = 0 if grid_spec_scalars is None else 1
    all_in, all_out, all_shape = list(in_specs), list(core_out_specs), list(core_out_shape)
    all_scr, aliases, extra_args, layout = list(scratch_shapes), dict(input_output_aliases or {}), [], []
    for job in jobs:
        n_job_out = job.n_inplace + len(job.fresh)
        layout.append((len(all_in), len(job.args), len(all_out), n_job_out, len(all_scr)))
        for a in range(job.n_inplace):
            aliases[n_pre + len(all_in) + a] = len(all_out) + a
        all_in += [ANY] * len(job.args)
        extra_args += job.args
        all_out += [ANY] * n_job_out
        all_shape += [jax.ShapeDtypeStruct(a.shape, a.dtype) for a in job.args[:job.n_inplace]] + job.fresh
        all_scr += [pltpu.SemaphoreType.DMA((job.nsem,)), pltpu.SemaphoreType.DMA((job.nsem,))]
    steps = 1
    for extent in grid:
        steps *= extent
    events = []
    for (i0, na, o0, no, s0), job in zip(layout, jobs):
        for frac, fn in job.phases:
            events.append((min(int(frac * steps), steps - 1), frac >= 1.0, len(events), fn, (i0, na, o0, no, s0)))
    events.sort(key=lambda e: e[:3])

    def wrapped(*refs):
        pre, refs = refs[:n_pre], refs[n_pre:]
        ins, outs, scr = refs[:len(all_in)], refs[len(all_in):len(all_in) + len(all_out)], refs[len(all_in) + len(all_out):]
        step = 0
        for axis, extent in enumerate(grid):
            step = step * extent + pl.program_id(axis)

        def emit(event):
            at, _, _, fn, (i0, na, o0, no, s0) = event
            run = lambda: fn(ins[i0:i0 + na], outs[o0:o0 + no], scr[s0], scr[s0 + 1])
            if steps == 1:
                run()
            else:
                pl.when(step == at)(run)

        for event in events:
            if not event[1]:
                emit(event)
        body(*pre, *ins[:n_in], *outs[:n_out], *scr[:n_scr])
        for event in events:
            if event[1]:
                emit(event)

    kwargs = dict(name=name, out_shape=all_shape, input_output_aliases=aliases)
    if compiler_params is not None:
        kwargs["compiler_params"] = compiler_params
    if grid_spec_scalars is not None:
        kwargs["grid_spec"] = pltpu.PrefetchScalarGridSpec(
            num_scalar_prefetch=1, grid=grid, in_specs=all_in, out_specs=all_out, scratch_shapes=all_scr)
    else:
        kwargs.update(in_specs=all_in, out_specs=all_out, scratch_shapes=all_scr)
        if grid:
            kwargs["grid"] = grid
    call = pl.pallas_call(wrapped, **kwargs)

    def run_call(*args):
        pre_args = [] if grid_spec_scalars is None else [grid_spec_scalars]
        res = list(call(*pre_args, *args, *extra_args))
        core = res[0] if single else res[:n_out]
        job_outs = [res[o0:o0 + no] for (_, _, o0, no, _) in layout]
        return core, job_outs

    return run_call


def _full(shape):
    nd = len(shape)
    return pl.BlockSpec(shape, lambda *_: (0,) * nd, pipeline_mode=pl.Buffered(1))


def _rows(ts, width, col=0):
    return pl.BlockSpec((ts, width), lambda i: (i, col))


def _sig(v):
    return jax.nn.sigmoid(v)


def _rms_stats(x):
    r = lax.rsqrt(jnp.mean(x * x, axis=-1, keepdims=True) + EPS)
    return r, x * r


def _rms_bwd(x, g, dh):
    r, xh = _rms_stats(x)
    dg = jnp.sum(dh * xh, axis=0, keepdims=True)
    dxh = dh * g
    dx = r * (dxh - xh * jnp.mean(dxh * xh, axis=-1, keepdims=True))
    return dx, dg


def _accumulate(ref, val, first):
    @pl.when(first)
    def _():
        ref[...] = val

    @pl.when(jnp.logical_not(first))
    def _():
        ref[...] += val


def _ffn_fwd(x, g, w13, w2, jobs=()):
    s_len = x.shape[0]
    ts = min(TOKEN_TILE, s_len)

    def body(x_ref, g_ref, w13_ref, w2_ref, xo_ref, h_ref, ab_ref):
        xv = x_ref[...]
        r, xh = _rms_stats(xv)
        h = (xh * g_ref[...]).astype(BF)
        h_ref[...] = h
        ab = jnp.dot(h, w13_ref[...], preferred_element_type=F32)
        ab_ref[...] = ab.astype(BF)
        a = ab[:, :FF]
        b = ab[:, FF:]
        s = (a * _sig(a) * b).astype(BF)
        xo_ref[...] = xv + 0.5 * jnp.dot(s, w2_ref[...], preferred_element_type=F32)

    return _pc(
        jobs, body, name="ffn_fwd", grid=(s_len // ts,),
        in_specs=[_rows(ts, D), _full((1, D)), _full((D, 2 * FF)), _full((FF, D))],
        out_specs=[_rows(ts, D), _rows(ts, D), _rows(ts, 2 * FF)],
        out_shape=[jax.ShapeDtypeStruct((s_len, D), F32), jax.ShapeDtypeStruct((s_len, D), BF),
                   jax.ShapeDtypeStruct((s_len, 2 * FF), BF)],
        compiler_params=_params(1),
    )(x, g, w13, w2)


def _ffn_bwd(dxo, x, g, ab, w13, w2, jobs=()):
    s_len = x.shape[0]
    ts = min(TOKEN_TILE, s_len)

    def body(dxo_ref, x_ref, g_ref, ab_ref, w13_ref, w2_ref, dxi_ref, dab_ref, s_ref, dy_ref, dg_ref):
        i = pl.program_id(0)
        dxo_v = dxo_ref[...]
        dy = (0.5 * dxo_v).astype(BF)
        dy_ref[...] = dy
        ds = lax.dot_general(dy, w2_ref[...], NT, preferred_element_type=F32)
        a = ab_ref[:, :FF].astype(F32)
        b = ab_ref[:, FF:].astype(F32)
        sg = _sig(a)
        sil = a * sg
        s_ref[...] = (sil * b).astype(BF)
        dab_ref[:, :FF] = (ds * b * (sg * (1.0 + a * (1.0 - sg)))).astype(BF)
        dab_ref[:, FF:] = (ds * sil).astype(BF)
        dh = lax.dot_general(dab_ref[...], w13_ref[...], NT, preferred_element_type=F32)
        dx, dg = _rms_bwd(x_ref[...], g_ref[...], dh)
        dxi_ref[...] = dxo_v + dx
        _accumulate(dg_ref, dg, i == 0)

    return _pc(
        jobs, body, name="ffn_bwd", grid=(s_len // ts,),
        in_specs=[_rows(ts, D), _rows(ts, D), _full((1, D)), _rows(ts, 2 * FF), _full((D, 2 * FF)), _full((FF, D))],
        out_specs=[_rows(ts, D), _rows(ts, 2 * FF), _rows(ts, FF), _rows(ts, D), pl.BlockSpec((1, D), lambda i: (0, 0))],
        out_shape=[jax.ShapeDtypeStruct((s_len, D), F32), jax.ShapeDtypeStruct((s_len, 2 * FF), BF),
                   jax.ShapeDtypeStruct((s_len, FF), BF), jax.ShapeDtypeStruct((s_len, D), BF),
                   jax.ShapeDtypeStruct((1, D), F32)],
        compiler_params=_params(1),
    )(dxo, x, g, ab, w13, w2)


def _wgrad(a, b, tk, tn, name, jobs=()):
    s_len, k = a.shape
    n = b.shape[1]

    def body(a_ref, b_ref, o_ref):
        o_ref[...] = lax.dot_general(a_ref[...], b_ref[...], TN, preferred_element_type=F32).astype(BF)

    return _pc(
        jobs, body, name=name, grid=(k // tk, n // tn),
        in_specs=[pl.BlockSpec((s_len, tk), lambda i, j: (0, i)), pl.BlockSpec((s_len, tn), lambda i, j: (0, j))],
        out_specs=pl.BlockSpec((tk, tn), lambda i, j: (i, j)),
        out_shape=jax.ShapeDtypeStruct((k, n), BF),
        compiler_params=_params(2),
    )(a, b)


def _wgrad_groups(a, b, ka, nb, name, jobs=()):
    s_len = a.shape[0]
    groups = a.shape[1] // ka

    def body(a_ref, b_ref, o_ref):
        o_ref[...] = lax.dot_general(a_ref[...], b_ref[...], TN, preferred_element_type=F32).astype(BF)

    return _pc(
        jobs, body, name=name, grid=(groups,),
        in_specs=[pl.BlockSpec((s_len, ka), lambda gi: (0, gi)), pl.BlockSpec((s_len, nb), lambda gi: (0, gi))],
        out_specs=pl.BlockSpec((ka, nb), lambda gi: (gi, 0)),
        out_shape=jax.ShapeDtypeStruct((groups * ka, nb), BF),
        compiler_params=_params(1),
    )(a, b)


def _mix_in(x, g, w_in, jobs=()):
    s_len = x.shape[0]
    ts = min(TOKEN_TILE, s_len)

    def body(x_ref, g_ref, w_ref, h_ref, p_ref):
        _, xh = _rms_stats(x_ref[...])
        h = (xh * g_ref[...]).astype(BF)
        h_ref[...] = h
        p_ref[...] = jnp.dot(h, w_ref[...], preferred_element_type=F32).astype(BF)

    return _pc(
        jobs, body, name="mix_in", grid=(s_len // ts,),
        in_specs=[_rows(ts, D), _full((1, D)), _full((D, INC))],
        out_specs=[_rows(ts, D), _rows(ts, INC)],
        out_shape=[jax.ShapeDtypeStruct((s_len, D), BF), jax.ShapeDtypeStruct((s_len, INC), BF)],
        compiler_params=_params(1),
    )(x, g, w_in)


def _shift(e, j):
    n = e.shape[0]
    j = j % n
    return e if j == 0 else pltpu.roll(e, j, 0)


def _ln_stats(z):
    mu = jnp.mean(z, axis=-1, keepdims=True)
    zc = z - mu
    rs = lax.rsqrt(jnp.mean(zc * zc, axis=-1, keepdims=True) + EPS)
    return rs, zc * rs


def _ln_bwd(rs, zn, dzn):
    return rs * (dzn - jnp.mean(dzn, axis=-1, keepdims=True) - zn * jnp.mean(dzn * zn, axis=-1, keepdims=True))


_GELU_C0 = 0.7978845608028654
_GELU_C1 = 0.044715


def _gelu(p):
    th = jnp.tanh(_GELU_C0 * (p + _GELU_C1 * p * p * p))
    return 0.5 * p * (1.0 + th), th


def _gelu_grad(p, th):
    return 0.5 * (1.0 + th) + 0.5 * p * (1.0 - th * th) * (_GELU_C0 * (1.0 + 3.0 * _GELU_C1 * p * p))


def _pool_diff(a, t, sign):
    outs = []
    for gi in range(NBR):
        win = 2 ** (gi + 1)
        ag = a[:, gi * GW:(gi + 1) * GW]
        cnt = jnp.clip(t + 1, 1, win).astype(F32)
        ws = ag if sign > 0 else ag / cnt
        for s in range(gi + 1):
            ws = ws + _shift(ws, sign * (2 ** s))
        outs.append((ws / cnt if sign > 0 else ws) - ag)
    return outs


def _sgu_mask():
    row = lax.broadcasted_iota(jnp.int32, (GW, GW), 0)
    col = lax.broadcasted_iota(jnp.int32, (GW, GW), 1)
    return (col // CHUNK) <= (row // CHUNK)


def _assemble(pe_ref, prev_ref, cur_ref, next_ref, i, last, ts):
    pe_ref[0:HALO, :] = jnp.where(i > 0, prev_ref[...], jnp.zeros_like(prev_ref))
    pe_ref[HALO:HALO + ts, :] = cur_ref[...]
    if next_ref is not None:
        pe_ref[HALO + ts:, :] = jnp.where(i < last, next_ref[...], jnp.zeros_like(next_ref))


def _halo_specs(ts, width, s_len, with_next):
    per = ts // HALO
    specs = [pl.BlockSpec((HALO, width), lambda i: (jnp.maximum(i * per - 1, 0), 0)),
             pl.BlockSpec((ts, width), lambda i: (i, 0))]
    if with_next:
        specs.append(pl.BlockSpec((HALO, width), lambda i: (jnp.minimum((i + 1) * per, s_len // HALO - 1), 0)))
    return specs


def _branch_weights_specs():
    return [_full((NBR, GW, GW)), _full((1, BW)), _full((SKP, BW)), _full((CKP, BW)), _full((1, BW)), _full((1, BW)),
            _full((1, BW)), _full((1, BW)), _full((NBR, GW, GW)), _full((GW, NBR))]


def _mix_branches_fwd(p, pool_w, pool_scale, sconv_w, cconv_w, cln_g, cln_b, sln_g, sln_b, sgu_w, sgu_bt, jobs=()):
    s_len = p.shape[0]
    ts = min(BRANCH_TILE, s_len)
    ext = HALO + ts

    def body(pp_ref, pc_ref, pw_ref, ps_ref, sw_ref, cw_ref, clg_ref, clb_ref, slg_ref, slb_ref, gw_ref, gb_ref,
             y_ref, z_ref, pe_ref):
        i = pl.program_id(0)
        _assemble(pe_ref, pp_ref, pc_ref, None, i, 0, ts)
        t = i * ts - HALO + lax.broadcasted_iota(jnp.int32, (ext, 1), 0)

        dgs = _pool_diff(pe_ref[:, 0:BW].astype(F32), t, 1)
        for gi in range(NBR):
            e = jnp.dot(dgs[gi][HALO:].astype(BF), pw_ref[gi].astype(BF), preferred_element_type=F32)
            y_ref[:, gi * GW:(gi + 1) * GW] = (e * ps_ref[:, gi * GW:(gi + 1) * GW]).astype(BF)

        xin = pe_ref[:, BW:2 * BW].astype(F32)
        cg = pe_ref[:, 3 * BW:4 * BW].astype(F32)
        q = cg * xin
        cv = sw_ref[2:3, :] * q + sw_ref[1:2, :] * _shift(q, 1) + sw_ref[0:1, :] * _shift(q, 2)
        y_ref[:, BW:2 * BW] = (pe_ref[HALO:, 2 * BW:3 * BW].astype(F32) * cv[HALO:]).astype(BF)

        yg = pe_ref[:, 4 * BW:5 * BW].astype(F32) * _sig(pe_ref[:, 5 * BW:6 * BW].astype(F32))
        z = cw_ref[CK - 1:CK, :] * yg
        for j in range(1, CK):
            z = z + cw_ref[CK - 1 - j:CK - j, :] * _shift(yg, j)
        z_ref[...] = z[HALO:].astype(BF)
        _, zn = _ln_stats(z[HALO:])
        nn = zn * clg_ref[...] + clb_ref[...]
        y_ref[:, 2 * BW:3 * BW] = (nn * _sig(nn)).astype(BF)

        u, _ = _gelu(pc_ref[:, 6 * BW:7 * BW].astype(F32))
        v, _ = _gelu(pc_ref[:, 7 * BW:8 * BW].astype(F32))
        _, vn = _ln_stats(v)
        vn = (vn * slg_ref[...] + slb_ref[...]).astype(BF)
        mask = _sgu_mask()
        for hd in range(NBR):
            wm = jnp.where(mask, gw_ref[hd], 0.0).astype(BF)
            for blk in range(ts // GW):
                rows = slice(blk * GW, (blk + 1) * GW)
                cols = slice(hd * GW, (hd + 1) * GW)
                zz = jnp.dot(wm, vn[rows, cols], preferred_element_type=F32) + gb_ref[:, hd:hd + 1]
                y_ref[rows, 3 * BW + hd * GW:3 * BW + (hd + 1) * GW] = (u[rows, cols] * zz).astype(BF)

    return _pc(
        jobs, body, name="mix_branches_fwd", grid=(s_len // ts,),
        in_specs=_halo_specs(ts, MIXC, s_len, False) + _branch_weights_specs(),
        out_specs=[_rows(ts, NBR * BW), _rows(ts, BW)],
        out_shape=[jax.ShapeDtypeStruct((s_len, NBR * BW), BF), jax.ShapeDtypeStruct((s_len, BW), BF)],
        scratch_shapes=[pltpu.VMEM((ext, MIXC), BF)],
        compiler_params=_params(1),
    )(p, p, pool_w, pool_scale, sconv_w, cconv_w, cln_g, cln_b, sln_g, sln_b, sgu_w, sgu_bt)


def _mix_branches_bwd(p, dy, z, dp, pool_w, pool_scale, sconv_w, cconv_w, cln_g, cln_b, sln_g, sln_b, sgu_w, sgu_bt,
                      jobs=()):
    s_len = p.shape[0]
    ts = min(BRANCH_TILE, s_len)
    ext = ts + 2 * HALO
    last = s_len // ts - 1
    tile = slice(HALO, HALO + ts)
    small_shapes = [(NBR, GW, GW), (1, BW), (SKP, BW), (CKP, BW), (1, BW), (1, BW), (1, BW), (1, BW), (NBR, GW, GW),
                    (NBR, GW, GW)]

    def body(pp_ref, pc_ref, pn_ref, dyp_ref, dyc_ref, dyn_ref, zc_ref, zn_ref, dpin_ref,
             pw_ref, ps_ref, sw_ref, cw_ref, clg_ref, clb_ref, slg_ref, slb_ref, gw_ref, gb_ref,
             dp_ref, dpw_ref, dps_ref, dsw_ref, dcw_ref, dclg_ref, dclb_ref, dslg_ref, dslb_ref, dgw_ref, dgb_ref,
             pe_ref, de_ref):
        del dyp_ref, dpin_ref
        i = pl.program_id(0)
        first = i == 0
        _assemble(pe_ref, pp_ref, pc_ref, pn_ref, i, last, ts)
        de_ref[0:HALO, :] = jnp.zeros((HALO, NBR * BW), BF)
        de_ref[HALO:HALO + ts, :] = dyc_ref[...]
        de_ref[HALO + ts:, :] = jnp.where(i < last, dyn_ref[...], jnp.zeros_like(dyn_ref))
        t = i * ts - HALO + lax.broadcasted_iota(jnp.int32, (ext, 1), 0)

        @pl.when(first)
        def _():
            dsw_ref[...] = jnp.zeros((SKP, BW), F32)
            dcw_ref[...] = jnp.zeros((CKP, BW), F32)

        dgs = _pool_diff(pe_ref[:, 0:BW].astype(F32), t, 1)
        dya = de_ref[:, 0:BW].astype(F32)
        dds = []
        for gi in range(NBR):
            cols = slice(gi * GW, (gi + 1) * GW)
            pw = pw_ref[gi].astype(BF)
            d_t = dgs[gi][tile].astype(BF)
            e = jnp.dot(d_t, pw, preferred_element_type=F32)
            _accumulate(dps_ref.at[:, cols], jnp.sum(dya[tile, cols] * e, axis=0, keepdims=True), first)
            de_g = (dya[:, cols] * ps_ref[:, cols]).astype(BF)
            _accumulate(dpw_ref.at[gi], lax.dot_general(d_t, de_g[tile], TN, preferred_element_type=F32), first)
            dds.append(lax.dot_general(de_g, pw, NT, preferred_element_type=F32))
        das = _pool_diff(jnp.concatenate(dds, axis=1), t, -1)
        for gi in range(NBR):
            dp_ref[:, gi * GW:(gi + 1) * GW] = das[gi][tile].astype(BF)

        xin = pe_ref[:, BW:2 * BW].astype(F32)
        bg = pe_ref[:, 2 * BW:3 * BW].astype(F32)
        cg = pe_ref[:, 3 * BW:4 * BW].astype(F32)
        q = cg * xin
        qs = [q, _shift(q, 1), _shift(q, 2)]
        cv = sw_ref[2:3, :] * qs[0] + sw_ref[1:2, :] * qs[1] + sw_ref[0:1, :] * qs[2]
        dyb = de_ref[:, BW:2 * BW].astype(F32)
        dcv = dyb * bg
        for j in range(SK):
            dsw_ref[SK - 1 - j:SK - j, :] += jnp.sum(dcv[tile] * qs[j][tile], axis=0, keepdims=True)
        dq = sw_ref[2:3, :] * dcv + sw_ref[1:2, :] * _shift(dcv, -1) + sw_ref[0:1, :] * _shift(dcv, -2)
        dp_ref[:, BW:2 * BW] = (dq * cg)[tile].astype(BF)
        dp_ref[:, 2 * BW:3 * BW] = (dyb * cv)[tile].astype(BF)
        dp_ref[:, 3 * BW:4 * BW] = (dq * xin)[tile].astype(BF)

        ca = pc_ref[:, 4 * BW:5 * BW].astype(F32)
        sb = _sig(pc_ref[:, 5 * BW:6 * BW].astype(F32))
        yg_t = ca * sb
        z = jnp.concatenate([zc_ref[...].astype(F32), zn_ref[...].astype(F32)], axis=0)
        rs, zn = _ln_stats(z)
        nn = zn * clg_ref[...] + clb_ref[...]
        sn = _sig(nn)
        dn = de_ref[HALO:, 2 * BW:3 * BW].astype(F32) * (sn * (1.0 + nn * (1.0 - sn)))
        _accumulate(dclg_ref, jnp.sum((dn * zn)[:ts], axis=0, keepdims=True), first)
        _accumulate(dclb_ref, jnp.sum(dn[:ts], axis=0, keepdims=True), first)
        dz = _ln_bwd(rs, zn, dn * clg_ref[...])
        dyg = cw_ref[CK - 1:CK, :] * dz[:ts]
        dcw_ref[CK - 1:CK, :] += jnp.sum(dz[:ts] * yg_t, axis=0, keepdims=True)
        for j in range(1, CK):
            dz_ahead = _shift(dz, -j)[:ts]
            dyg = dyg + cw_ref[CK - 1 - j:CK - j, :] * dz_ahead
            dcw_ref[CK - 1 - j:CK - j, :] += jnp.sum(dz_ahead * yg_t, axis=0, keepdims=True)
        dp_ref[:, 4 * BW:5 * BW] = (dyg * sb).astype(BF)
        dp_ref[:, 5 * BW:6 * BW] = (dyg * ca * sb * (1.0 - sb)).astype(BF)

        pu = pc_ref[:, 6 * BW:7 * BW].astype(F32)
        pv = pc_ref[:, 7 * BW:8 * BW].astype(F32)
        u, thu = _gelu(pu)
        v, thv = _gelu(pv)
        vrs, vn0 = _ln_stats(v)
        vn = (vn0 * slg_ref[...] + slb_ref[...]).astype(BF)
        dyd = dyc_ref[:, 3 * BW:4 * BW].astype(F32)
        dzz = dyd * u
        dzb = dzz.astype(BF)
        mask = _sgu_mask()
        dvn_cols = []
        for hd in range(NBR):
            cols = slice(hd * GW, (hd + 1) * GW)
            wm = jnp.where(mask, gw_ref[hd], 0.0).astype(BF)
            dwm = jnp.zeros((GW, GW), F32)
            dbs = jnp.zeros((GW, GW), F32)
            dvn_rows = []
            for blk in range(ts // GW):
                rows = slice(blk * GW, (blk + 1) * GW)
                zz = jnp.dot(wm, vn[rows, cols], preferred_element_type=F32) + gb_ref[:, hd:hd + 1]
                dp_ref[rows, 6 * BW + hd * GW:6 * BW + (hd + 1) * GW] = (
                    dyd[rows, cols] * zz * _gelu_grad(pu[rows, cols], thu[rows, cols])).astype(BF)
                dwm = dwm + lax.dot_general(dzb[rows, cols], vn[rows, cols], NT, preferred_element_type=F32)
                dbs = dbs + dzz[rows, cols]
                dvn_rows.append(lax.dot_general(wm, dzb[rows, cols], TN, preferred_element_type=F32))
            _accumulate(dgw_ref.at[hd], jnp.where(mask, dwm, 0.0), first)
            _accumulate(dgb_ref.at[hd], dbs, first)
            dvn_cols.append(jnp.concatenate(dvn_rows, axis=0))
        dvn = jnp.concatenate(dvn_cols, axis=1)
        _accumulate(dslg_ref, jnp.sum(dvn * vn0, axis=0, keepdims=True), first)
        _accumulate(dslb_ref, jnp.sum(dvn, axis=0, keepdims=True), first)
        dv = _ln_bwd(vrs, vn0, dvn * slg_ref[...])
        dp_ref[:, 7 * BW:8 * BW] = (dv * _gelu_grad(pv, thv)).astype(BF)

    const = lambda shp: pl.BlockSpec(shp, lambda i: (0,) * len(shp))
    return _pc(
        jobs, body, name="mix_branches_bwd", grid=(s_len // ts,),
        in_specs=(_halo_specs(ts, MIXC, s_len, True) + _halo_specs(ts, NBR * BW, s_len, True)
                  + _halo_specs(ts, BW, s_len, True)[1:] + [pl.BlockSpec(memory_space=pl.ANY)] + _branch_weights_specs()),
        out_specs=[pl.BlockSpec((ts, MIXC), lambda i: (i, 0))] + [const(s) for s in small_shapes],
        out_shape=[jax.ShapeDtypeStruct((s_len, INC), BF)] + [jax.ShapeDtypeStruct(s, F32) for s in small_shapes],
        scratch_shapes=[pltpu.VMEM((ext, MIXC), BF), pltpu.VMEM((ext, NBR * BW), BF)],
        input_output_aliases={8: 0},
        compiler_params=_params(1),
    )(p, p, p, dy, dy, dy, z, z, dp, pool_w, pool_scale, sconv_w, cconv_w, cln_g, cln_b, sln_g, sln_b, sgu_w, sgu_bt)


def _mix_out_fwd(x, y, p, w_up, w_out, jobs=()):
    s_len = x.shape[0]
    ts = min(TOKEN_TILE, s_len)

    def body(x_ref, y_ref, pg_ref, wu_ref, wo_ref, xo_ref, m_ref, up_ref):
        m = jnp.zeros((ts, D), F32)
        for gi in range(NBR):
            up = jnp.dot(y_ref[:, gi * BW:(gi + 1) * BW], wu_ref[gi], preferred_element_type=F32)
            up_ref[:, gi * D:(gi + 1) * D] = up.astype(BF)
            m = m + _sig(pg_ref[:, gi * D:(gi + 1) * D].astype(F32)) * up
        mb = m.astype(BF)
        m_ref[...] = mb
        xo_ref[...] = x_ref[...] + jnp.dot(mb, wo_ref[...], preferred_element_type=F32)

    return _pc(
        jobs, body, name="mix_out_fwd", grid=(s_len // ts,),
        in_specs=[_rows(ts, D), _rows(ts, NBR * BW), _rows(ts, NBR * D, 1), _full((NBR, BW, D)), _full((D, D))],
        out_specs=[_rows(ts, D), _rows(ts, D), _rows(ts, NBR * D)],
        out_shape=[jax.ShapeDtypeStruct((s_len, D), F32), jax.ShapeDtypeStruct((s_len, D), BF),
                   jax.ShapeDtypeStruct((s_len, NBR * D), BF)],
        compiler_params=_params(1),
    )(x, y, p, w_up, w_out)


def _mix_out_bwd(dxo, up, p, w_up, w_out, jobs=()):
    s_len = dxo.shape[0]
    ts = min(TOKEN_TILE, s_len)

    def body(dxo_ref, up_ref, pg_ref, wu_ref, wo_ref, dy_ref, dp_ref, dup_ref, dxb_ref):
        dxb = dxo_ref[...].astype(BF)
        dxb_ref[...] = dxb
        dm = lax.dot_general(dxb, wo_ref[...], NT, preferred_element_type=F32)
        for gi in range(NBR):
            cols = slice(gi * D, (gi + 1) * D)
            gate = _sig(pg_ref[:, cols].astype(F32))
            dp_ref[:, cols] = (dm * up_ref[:, cols].astype(F32) * gate * (1.0 - gate)).astype(BF)
            dup = (dm * gate).astype(BF)
            dup_ref[:, cols] = dup
            dy_ref[:, gi * BW:(gi + 1) * BW] = lax.dot_general(
                dup, wu_ref[gi], NT, preferred_element_type=F32).astype(BF)

    return _pc(
        jobs, body, name="mix_out_bwd", grid=(s_len // ts,),
        in_specs=[_rows(ts, D), _rows(ts, NBR * D), _rows(ts, NBR * D, 1), _full((NBR, BW, D)), _full((D, D))],
        out_specs=[_rows(ts, NBR * BW), _rows(ts, NBR * D, 1), _rows(ts, NBR * D), _rows(ts, D)],
        out_shape=[jax.ShapeDtypeStruct((s_len, NBR * BW), BF), jax.ShapeDtypeStruct((s_len, INC), BF),
                   jax.ShapeDtypeStruct((s_len, NBR * D), BF), jax.ShapeDtypeStruct((s_len, D), BF)],
        compiler_params=_params(1),
    )(dxo, up, p, w_up, w_out)


def _mix_in_bwd(dp, w_in, x, g, dxo, jobs=()):
    s_len = x.shape[0]
    ts = min(TOKEN_TILE, s_len)

    def body(dp_ref, w_ref, x_ref, g_ref, dxo_ref, dxi_ref, dg_ref):
        i = pl.program_id(0)
        dh = lax.dot_general(dp_ref[...], w_ref[...], NT, preferred_element_type=F32)
        dx, dg = _rms_bwd(x_ref[...], g_ref[...], dh)
        dxi_ref[...] = dxo_ref[...] + dx
        _accumulate(dg_ref, dg, i == 0)

    return _pc(
        jobs, body, name="mix_in_bwd", grid=(s_len // ts,),
        in_specs=[_rows(ts, INC), _full((D, INC)), _rows(ts, D), _full((1, D)), _rows(ts, D)],
        out_specs=[_rows(ts, D), pl.BlockSpec((1, D), lambda i: (0, 0))],
        out_shape=[jax.ShapeDtypeStruct((s_len, D), F32), jax.ShapeDtypeStruct((1, D), F32)],
        compiler_params=_params(1),
    )(dp, w_in, x, g, dxo)


def _loss_head(x, g, target, jobs=()):
    s_len = x.shape[0]
    ts = min(512, s_len)

    def body(x_ref, g_ref, t_ref, dx_ref, dg_ref, loss_ref):
        i = pl.program_id(0)
        xv = x_ref[...]
        _, xh = _rms_stats(xv)
        err = xh * g_ref[...] - t_ref[...]
        part = 0.5 * jnp.sum(jnp.mean(err * err, axis=-1, keepdims=True), axis=0, keepdims=True)
        dx, dg = _rms_bwd(xv, g_ref[...], err * (1.0 / D))
        dx_ref[...] = dx
        _accumulate(dg_ref, dg, i == 0)
        _accumulate(loss_ref, jnp.broadcast_to(part, (1, GW)), i == 0)

    return _pc(
        jobs, body, name="loss_head", grid=(s_len // ts,),
        in_specs=[_rows(ts, D), _full((1, D)), _rows(ts, D)],
        out_specs=[_rows(ts, D), pl.BlockSpec((1, D), lambda i: (0, 0)), pl.BlockSpec((1, GW), lambda i: (0, 0))],
        out_shape=[jax.ShapeDtypeStruct((s_len, D), F32), jax.ShapeDtypeStruct((1, D), F32),
                   jax.ShapeDtypeStruct((1, GW), F32)],
        compiler_params=_params(1),
    )(x, g, target)


SUM_TILE = 1 << 20
ADAM_TILE = 1 << 19


def _row_tile(rows, cols, budget=1 << 18):
    tr = rows
    while tr * cols > budget and tr % 16 == 0:
        tr //= 2
    return tr


def _elementwise(fn, name, ins, out_dtypes):
    rows, cols = ins[0].shape
    tr = _row_tile(rows, cols)
    n_in = len(ins)

    def body(*refs):
        res = fn(*[r[...] for r in refs[:n_in]])
        for o_ref, val in zip(refs[n_in:], res):
            o_ref[...] = val.astype(o_ref.dtype)

    outs = pl.pallas_call(
        body, name=name, grid=(rows // tr,),
        in_specs=[_rows(tr, cols)] * n_in, out_specs=[_rows(tr, cols)] * len(out_dtypes),
        out_shape=[jax.ShapeDtypeStruct((rows, cols), dt) for dt in out_dtypes],
        compiler_params=_params(1),
    )(*ins)
    return outs


def _tiled(fn, name, grid, in_specs, out_specs, out_shape, args, scalars=None, alias=None):
    alias = alias or {}
    n_in = len(in_specs) - len(alias)
    n_pre = 0 if scalars is None else 1

    def body(*refs):
        refs = refs[n_pre:]
        res = fn(*[r[...] for r in refs[:n_in]])
        for o_ref, val in zip(refs[len(in_specs):], res):
            o_ref[...] = val.astype(o_ref.dtype)

    aliases = {n_pre + pos: out for pos, out in alias.items()}
    if scalars is None:
        return pl.pallas_call(body, name=name, grid=grid, in_specs=in_specs, out_specs=out_specs, out_shape=out_shape,
                              input_output_aliases=aliases, compiler_params=_params(len(grid)))(*args)
    spec = pltpu.PrefetchScalarGridSpec(num_scalar_prefetch=1, grid=grid, in_specs=in_specs, out_specs=out_specs)
    return pl.pallas_call(body, name=name, grid_spec=spec, out_shape=out_shape, input_output_aliases=aliases,
                          compiler_params=_params(len(grid)))(scalars, *args)


def _cast_into(shard, layer, col, dtype, sc):
    ks, ns = shard.shape[1:]
    tr = _row_tile(ks, ns, SUM_TILE)
    full = (ks, ns * NCHIP) if col else (ks * NCHIP, ns)
    out_idx = (lambda i, s: (i, s[1])) if col else (lambda i, s: (s[1] * (ks // tr) + i, 0))
    return _tiled(lambda v: (v,), "cast_into", (ks // tr,), [pl.BlockSpec((None, tr, ns), lambda i, s: (layer, i, 0))],
                  [pl.BlockSpec((tr, ns), out_idx)], [jax.ShapeDtypeStruct(full, dtype)], [shard], sc)[0]


def _pair_sum(g, got, col, sc):
    hk, hn = got.shape
    tr = _row_tile(hk, hn, SUM_TILE)
    g_idx = (lambda i, s: (s[0] * (hk // tr) + i, 0)) if col else (lambda i, s: (i, s[0]))
    plain = pl.BlockSpec((tr, hn), lambda i, s: (i, 0))
    return _tiled(lambda a, b: (a.astype(F32) + b.astype(F32),), "pair_sum", (hk // tr,),
                  [pl.BlockSpec((tr, hn), g_idx), plain], [plain], [jax.ShapeDtypeStruct((hk, hn), BF)], [g, got], sc)[0]


def _chip_sum(ph, got, col, layer, depth, sc, carry):
    qk, qn = got.shape[1:]
    tr = _row_tile(qk, qn, SUM_TILE)
    ph_idx = (lambda i, s: (i, s[1])) if col else (lambda i, s: (s[1] * (qk // tr) + i, 0))
    out_shape = (depth, 2 * qk, qn) if col else (depth, qk, 2 * qn)
    out_idx = (lambda i, s: (layer, s[0] * (qk // tr) + i, 0)) if col else (lambda i, s: (layer, i, s[0]))
    in_specs = [pl.BlockSpec((tr, qn), ph_idx)] + [pl.BlockSpec((None, tr, qn), lambda i, s, j=j: (j, i, 0)) for j in range(3)]
    args = [ph, got, got, got]
    if carry is not None:
        in_specs.append(ANY)
        args.append(carry)
    return _tiled(lambda a, b, c_, d_: (a.astype(F32) + b.astype(F32) + c_.astype(F32) + d_.astype(F32),), "chip_sum",
                  (qk // tr,), in_specs, [pl.BlockSpec((None, tr, qn), out_idx)],
                  [jax.ShapeDtypeStruct(out_shape, F32)], args, sc, alias=None if carry is None else {4: 0})[0]


def _adamw_math(w, g, m, v):
    m = ADAM_B1 * m + (1.0 - ADAM_B1) * g
    v = ADAM_B2 * v + (1.0 - ADAM_B2) * (g * g)
    m_hat = m / (1.0 - ADAM_B1 ** ADAM_STEP)
    v_hat = v / (1.0 - ADAM_B2 ** ADAM_STEP)
    delta = -ADAM_LR * (m_hat / (jnp.sqrt(v_hat) + ADAM_EPS) + ADAM_WD * w)
    return delta, m, v


def _adamw_layer(w, g, m, v, layer, carry):
    k, n = w.shape[1:]
    tr = _row_tile(k, n, ADAM_TILE)
    blk = pl.BlockSpec((None, tr, n), lambda i: (layer, i, 0))
    carry = list(carry or [])
    return _tiled(lambda *a: (*_adamw_math(*a), a[1]), "adamw_layer", (k // tr,), [blk] * 4 + [ANY] * len(carry),
                  [blk] * 4, [jax.ShapeDtypeStruct(w.shape, F32)] * 4, [w, g, m, v] + carry, None,
                  alias={4 + pos: pos for pos in range(len(carry))})


def _adamw(w, g, m, v):
    shape = w.shape
    two_d = lambda a: a.reshape(-1, shape[-1])
    outs = _elementwise(_adamw_math, "adamw", [two_d(w), two_d(g), two_d(m), two_d(v)], [F32, F32, F32])
    return [o.reshape(shape) for o in outs]


ANY = pl.BlockSpec(memory_space=pl.ANY)


def _place():
    x, y, c = lax.axis_index("x"), lax.axis_index("y"), lax.axis_index("c")
    chips = [(1 - x, y), (x, 1 - y), (1 - x, 1 - y)]
    return x, y, c, chips


def _cols(ref, start, size):
    idx = (slice(None),) * (len(ref.shape) - 1) + (pl.ds(pl.multiple_of(start, GW), size),)
    return ref.at[idx]


def _rows_of(ref, start, size):
    nd = len(ref.shape)
    idx = (slice(None),) * (nd - 2) + (pl.ds(pl.multiple_of(start, 16), size), slice(None))
    return ref.at[idx]


def _region(ref, col_sharded, chip, half):
    k, n = ref.shape
    align = 16 if ref.dtype == BF else 8
    if col_sharded:
        return ref.at[pl.ds(pl.multiple_of(half * (k // 2), align), k // 2),
                      pl.ds(pl.multiple_of(chip * (n // NCHIP), GW), n // NCHIP)]
    rows = k // (2 * NCHIP)
    return ref.at[pl.ds(pl.multiple_of((2 * chip + half) * rows, align), rows), :]


def _job_gather(bufs, col_sharded, handoff):
    n = len(bufs)

    def copies(outs, send_sems, recv_sems, stage):
        x, y, c, chips = _place()
        sends, lands = [], []
        for k in range(n):
            for j, chip in enumerate(chips):
                theirs = 2 * chip[0] + chip[1]
                if stage == 0:
                    src, to = _region(outs[k], col_sharded[k], 2 * x + y, c), (*chip, c)
                    land = _region(outs[k], col_sharded[k], theirs, c)
                else:
                    src, to = _region(outs[k], col_sharded[k], theirs, c), (x, y, 1 - c)
                    land = _region(outs[k], col_sharded[k], theirs, 1 - c)
                sem = 3 * n * stage + 3 * k + j
                sems = dict(send_sem=send_sems.at[sem], recv_sem=recv_sems.at[sem], device_id=to, device_id_type=MESH)
                sends.append(pltpu.make_async_remote_copy(src_ref=src, dst_ref=src, **sems))
                lands.append(pltpu.make_async_remote_copy(src_ref=land, dst_ref=land, **sems))
        return sends, lands

    def start(ins, outs, send_sems, recv_sems):
        for cp in copies(outs, send_sems, recv_sems, 0)[0]:
            cp.start()

    def hand_on(ins, outs, send_sems, recv_sems):
        for cp in copies(outs, send_sems, recv_sems, 0)[1]:
            cp.wait_recv()
        for cp in copies(outs, send_sems, recv_sems, 1)[0]:
            cp.start()

    def finish(ins, outs, send_sems, recv_sems):
        sends, lands = copies(outs, send_sems, recv_sems, 1)
        for cp in lands:
            cp.wait_recv()
        for cp in copies(outs, send_sems, recv_sems, 0)[0] + sends:
            cp.wait_send()

    return _Job(bufs, n, [], 6 * n, [(0.0, start), (handoff, hand_on), (1.0, finish)])


def _half(ref, col_sharded, c):
    k, n = ref.shape[-2:]
    return _rows_of(ref, c * (k // 2), k // 2) if col_sharded else _cols(ref, c * (n // 2), n // 2)


def _quarter(ref, col_sharded, j):
    k, n = ref.shape[-2:]
    return _cols(ref, j * (n // NCHIP), n // NCHIP) if col_sharded else _rows_of(ref, j * (k // NCHIP), k // NCHIP)


def _job_pair(grads, col_sharded):
    n = len(grads)

    def half_shape(g, col):
        return (g.shape[0] // 2, g.shape[1]) if col else (g.shape[0], g.shape[1] // 2)

    def copies(ins, got, send_sems, recv_sems):
        x, y, c, _ = _place()
        return [pltpu.make_async_remote_copy(
            src_ref=_half(ins[k], col_sharded[k], 1 - c), dst_ref=got[k], send_sem=send_sems.at[k],
            recv_sem=recv_sems.at[k], device_id=(x, y, 1 - c), device_id_type=MESH) for k in range(n)]

    def start(*refs):
        for cp in copies(*refs):
            cp.start()

    def finish(*refs):
        for cp in copies(*refs):
            cp.wait()

    fresh = [jax.ShapeDtypeStruct(half_shape(g, col), g.dtype) for g, col in zip(grads, col_sharded)]
    return _Job(grads, 0, fresh, n, [(0.0, start), (1.0, finish)])


def _job_chip(halves, col_sharded):
    n = len(halves)

    def quarter_shape(h, col):
        return (3, h.shape[0], h.shape[1] // NCHIP) if col else (3, h.shape[0] // NCHIP, h.shape[1])

    def copies(ins, got, send_sems, recv_sems):
        x, y, c, chips = _place()
        return [pltpu.make_async_remote_copy(
            src_ref=_quarter(ins[k], col_sharded[k], 2 * chip[0] + chip[1]), dst_ref=got[k].at[j],
            send_sem=send_sems.at[3 * k + j], recv_sem=recv_sems.at[3 * k + j], device_id=(*chip, c), device_id_type=MESH)
            for k in range(n) for j, chip in enumerate(chips)]

    def start(*refs):
        for cp in copies(*refs):
            cp.start()

    def finish(*refs):
        for cp in copies(*refs):
            cp.wait()

    fresh = [jax.ShapeDtypeStruct(quarter_shape(h, col), h.dtype) for h, col in zip(halves, col_sharded)]
    return _Job(halves, 0, fresh, 3 * n, [(0.0, start), (1.0, finish)])


def _job_sibling(shards, col_sharded, layers):
    n = len(shards)

    def copies(outs, send_sems, recv_sems):
        x, y, c, _ = _place()
        sends, lands = [], []
        for k in range(n):
            sems = dict(send_sem=send_sems.at[k], recv_sem=recv_sems.at[k], device_id=(x, y, 1 - c), device_id_type=MESH)
            mine = _half(outs[k].at[layers[k]], col_sharded[k], c)
            theirs = _half(outs[k].at[layers[k]], col_sharded[k], 1 - c)
            sends.append(pltpu.make_async_remote_copy(src_ref=mine, dst_ref=mine, **sems))
            lands.append(pltpu.make_async_remote_copy(src_ref=theirs, dst_ref=theirs, **sems))
        return sends, lands

    def start(ins, outs, send_sems, recv_sems):
        for cp in copies(outs, send_sems, recv_sems)[0]:
            cp.start()

    def finish(ins, outs, send_sems, recv_sems):
        sends, lands = copies(outs, send_sems, recv_sems)
        for cp in lands:
            cp.wait_recv()
        for cp in sends:
            cp.wait_send()

    return _Job(shards, n, [], n, [(0.0, start), (1.0, finish)])


def _standalone(jobs, name):
    return _pc(jobs, lambda: None, name=name, in_specs=[], out_specs=[], out_shape=[])()[1]


def _all_reduce_small(buf, jobs=()):
    rows = buf.shape[0]
    per = rows // NDEV
    flips = [(fx, fy, fc) for fx in (0, 1) for fy in (0, 1) for fc in (0, 1)][1:]

    def body(in_ref, out_ref, got_ref, send_sems, recv_sems):
        x, y, c, _ = _place()
        me = 4 * x + 2 * y + c

        def peer(f):
            return tuple(1 - pos if flip else pos for pos, flip in zip((x, y, c), f))

        def block(ref, dev):
            return ref.at[pl.ds(pl.multiple_of(dev * per, 8), per), :]

        scatter = []
        for k, f in enumerate(flips):
            px, py, pc = peer(f)
            scatter.append(pltpu.make_async_remote_copy(
                src_ref=block(in_ref, 4 * px + 2 * py + pc), dst_ref=got_ref.at[k], send_sem=send_sems.at[k],
                recv_sem=recv_sems.at[k], device_id=(px, py, pc), device_id_type=MESH))
        for cp in scatter:
            cp.start()
        for cp in scatter:
            cp.wait()
        total = block(in_ref, me)[...]
        for k in range(len(flips)):
            total = total + got_ref[k]
        block(out_ref, me)[...] = total
        share = []
        for k, f in enumerate(flips):
            share.append(pltpu.make_async_remote_copy(
                src_ref=block(out_ref, me), dst_ref=block(out_ref, me), send_sem=send_sems.at[7 + k],
                recv_sem=recv_sems.at[7 + k], device_id=peer(f), device_id_type=MESH))
        for cp in share:
            cp.start()
        for k, f in enumerate(flips):
            share[k].wait_send()
            px, py, pc = peer(f)
            theirs = block(out_ref, 4 * px + 2 * py + pc)
            pltpu.make_async_remote_copy(
                src_ref=theirs, dst_ref=theirs, send_sem=send_sems.at[7 + k], recv_sem=recv_sems.at[7 + k],
                device_id=(px, py, pc), device_id_type=MESH).wait_recv()

    vmem = pl.BlockSpec(memory_space=pltpu.VMEM)
    return _pc(
        jobs, body, name="all_reduce_small", in_specs=[vmem], out_specs=vmem,
        out_shape=jax.ShapeDtypeStruct((rows, GW), F32),
        scratch_shapes=[pltpu.VMEM((NDEV - 1, per, GW), F32), pltpu.SemaphoreType.DMA((14,)),
                        pltpu.SemaphoreType.DMA((14,))],
    )(buf)


BIG = ("ffn1_w13", "ffn1_w2", "w_in", "w_up", "w_out", "ffn2_w13", "ffn2_w2")
BIG_COL_SHARDED = (True, False, True, True, False, True, False)
SMALL = ("ffn1_norm", "mix_norm", "pool_w", "pool_scale", "sconv_w", "cconv_w", "cconv_ln_g", "cconv_ln_b",
         "sgu_ln_g", "sgu_ln_b", "sgu_w", "sgu_b", "ffn2_norm", "final_norm")
WEIGHTS = ("ffn1_norm", "ffn1_w13", "ffn1_w2", "mix_norm", "w_in", "pool_w", "pool_scale", "sconv_w", "cconv_w",
           "cconv_ln_g", "cconv_ln_b", "sgu_ln_g", "sgu_ln_b", "sgu_w", "sgu_b", "w_up", "w_out", "ffn2_norm",
           "ffn2_w13", "ffn2_w2", "final_norm")


def _pad_rows(a, rows):
    return jnp.pad(a, ((0, 0), (0, rows - a.shape[1]), (0, 0)))


def kernel(x, ffn1_norm, ffn1_w13, ffn1_w2, mix_norm, w_in, pool_w, pool_scale, sconv_w, cconv_w, cconv_ln_g, cconv_ln_b, sgu_ln_g, sgu_ln_b, sgu_w, sgu_b, w_up, w_out, ffn2_norm, ffn2_w13, ffn2_w2, final_norm, loss_target, m_ffn1_norm, m_ffn1_w13, m_ffn1_w2, m_mix_norm, m_w_in, m_pool_w, m_pool_scale, m_sconv_w, m_cconv_w, m_cconv_ln_g, m_cconv_ln_b, m_sgu_ln_g, m_sgu_ln_b, m_sgu_w, m_sgu_b, m_w_up, m_w_out, m_ffn2_norm, m_ffn2_w13, m_ffn2_w2, m_final_norm, v_ffn1_norm, v_ffn1_w13, v_ffn1_w2, v_mix_norm, v_w_in, v_pool_w, v_pool_scale, v_sconv_w, v_cconv_w, v_cconv_ln_g, v_cconv_ln_b, v_sgu_ln_g, v_sgu_ln_b, v_sgu_w, v_sgu_b, v_w_up, v_w_out, v_ffn2_norm, v_ffn2_w13, v_ffn2_w2, v_final_norm):
    args = dict(locals())
    w = {nm: args[nm] for nm in WEIGHTS}
    m = {nm: args["m_" + nm] for nm in WEIGHTS}
    v = {nm: args["v_" + nm] for nm in WEIGHTS}
    depth = ffn1_w13.shape[0]
    chip = 2 * lax.axis_index("x") + lax.axis_index("y")

    sc = jnp.stack([lax.axis_index("c"), chip]).astype(jnp.int32)
    col_of = dict(zip(BIG, BIG_COL_SHARDED), sconv_w=True, cconv_w=True)
    sources = {nm: (w[nm], BF) for nm in BIG}
    sources["w_up"] = (w_up.reshape(depth, NBR * BW, w_up.shape[-1]), BF)
    sources["sconv_w"] = (_pad_rows(sconv_w, 2 * SKP), F32)
    sources["cconv_w"] = (_pad_rows(cconv_w, 2 * CKP), F32)
    full = [{nm: _cast_into(src, l, col_of[nm], dt, sc) for nm, (src, dt) in sources.items()} for l in range(depth)]

    def gather_job(l, names, handoff):
        return _job_gather([full[l][nm] for nm in names], [col_of[nm] for nm in names], handoff)

    def gather_with(l, names, handoff, call):
        res, job_outs = call([gather_job(l, names, handoff)] if l < depth else [])
        if l < depth:
            full[l].update(zip(names, job_outs[0]))
        return res

    first_group = ("ffn1_w13", "ffn1_w2")
    full[0].update(zip(first_group, _standalone([gather_job(0, first_group, 0.0)], "gather_first")[0]))

    xs = x[0]
    row = lambda a: a.reshape(1, -1)
    saved = []
    for l in range(depth):
        g1, gm, g2 = row(ffn1_norm[l]), row(mix_norm[l]), row(ffn2_norm[l])
        x1, h1, ab1 = gather_with(l, ("w_in",), 0.85, lambda jobs: _ffn_fwd(
            xs, g1, full[l]["ffn1_w13"], full[l]["ffn1_w2"], jobs))
        hm, p = gather_with(l, ("w_up", "w_out", "sconv_w", "cconv_w", "ffn2_w2"), 0.75, lambda jobs: _mix_in(
            x1, gm, full[l]["w_in"], jobs))
        branch = (pool_w[l], row(pool_scale[l]), full[l]["sconv_w"][:SKP], full[l]["cconv_w"][:CKP], row(cconv_ln_g[l]),
                  row(cconv_ln_b[l]), row(sgu_ln_g[l]), row(sgu_ln_b[l]), sgu_w[l], sgu_b[l].T)
        y, conv_z = gather_with(l, ("ffn2_w13",), 0.85, lambda jobs: _mix_branches_fwd(p, *branch, jobs=jobs))
        w_up_l = full[l]["w_up"].reshape(NBR, BW, D)
        x2, merged, up = gather_with(l + 1, ("ffn1_w2",), 0.7, lambda jobs: _mix_out_fwd(
            x1, y, p, w_up_l, full[l]["w_out"], jobs))
        x3, h2, ab2 = gather_with(l + 1, ("ffn1_w13",), 0.75, lambda jobs: _ffn_fwd(
            x2, g2, full[l]["ffn2_w13"], full[l]["ffn2_w2"], jobs))
        lw = dict(g1=g1, gm=gm, g2=g2, w13a=full[l]["ffn1_w13"], w2a=full[l]["ffn1_w2"], w13b=full[l]["ffn2_w13"],
                  w2b=full[l]["ffn2_w2"], w_in=full[l]["w_in"], w_up=w_up_l, w_out=full[l]["w_out"], branch=branch)
        saved.append(dict(lw=lw, x0=xs, x1=x1, x2=x2, h1=h1, ab1=ab1, hm=hm, p=p, y=y, z=conv_z, merged=merged, up=up, h2=h2,
                          ab2=ab2))
        xs = x3

    (dx, d_final, loss_part), _ = _loss_head(xs, row(final_norm), loss_target[0])
    loss = lax.psum(loss_part[0, 0], ("x", "y", "c"))

    ici_us = dict(ffn1_w13=64, ffn1_w2=32, w_in=93, w_up=23, w_out=12, ffn2_w13=64, ffn2_w2=32)
    parts, pair_sums, reduced, big_updates, pending = {}, {}, {}, {}, []

    def take_jobs(budget_us):
        chosen = []
        for task in list(pending):
            kind, (nm, _) = task
            if kind == "chip":
                if ici_us[nm] > budget_us:
                    continue
                budget_us -= ici_us[nm]
            if kind == "sib" and any(k == "sib" and key[0] == nm for k, key in chosen):
                continue
            chosen.append(task)
            pending.remove(task)
        groups, jobs = [], []
        for kind in ("pair", "sib", "chip"):
            keys = [key for k, key in chosen if k == kind]
            if not keys:
                continue
            cols = [col_of[nm] for nm, _ in keys]
            groups.append((kind, keys))
            if kind == "pair":
                jobs.append(_job_pair([parts[key] for key in keys], cols))
            elif kind == "chip":
                jobs.append(_job_chip([pair_sums[key] for key in keys], cols))
            else:
                jobs.append(_job_sibling([reduced[nm] for nm, _ in keys], cols, [layer for _, layer in keys]))
        return groups, jobs

    def settle(groups, job_outs):
        for (kind, keys), outs in zip(groups, job_outs):
            for key, out in zip(keys, outs):
                nm, layer = key
                if kind == "pair":
                    pair_sums[key] = _pair_sum(parts[key], out, col_of[nm], sc)
                    pending.append(("chip", key))
                elif kind == "chip":
                    assert not any(k == "sib" and other[0] == nm for k, other in pending)
                    reduced[nm] = _chip_sum(pair_sums[key], out, col_of[nm], layer, depth, sc, reduced.get(nm))
                    pending.append(("sib", key))
                else:
                    reduced[nm] = out
                    as3 = lambda a: a.reshape(out.shape)
                    big_updates[nm] = _adamw_layer(as3(w[nm]), out, as3(m[nm]), as3(v[nm]), layer, big_updates.get(nm))

    def run(budget_us, call, carrier=True):
        groups, jobs = take_jobs(budget_us) if carrier else ([], [])
        res, job_outs = call(jobs)
        settle(groups, job_outs)
        return res

    def wgrad_done(key, partial):
        parts[key] = partial
        pending.append(("pair", key))

    small_parts = {nm: [None] * depth for nm in SMALL if nm != "final_norm"}
    for l in reversed(range(depth)):
        sv = saved[l]
        lw = sv["lw"]
        dx, dab, s_act, dyh, dg2 = run(108, lambda jobs: _ffn_bwd(
            dx, sv["x2"], lw["g2"], sv["ab2"], lw["w13b"], lw["w2b"], jobs))
        wgrad_done(("ffn2_w13", l), run(58, lambda jobs: _wgrad(sv["h2"], dab, D, 512, "wgrad_w13", jobs)))
        wgrad_done(("ffn2_w2", l), run(33, lambda jobs: _wgrad(s_act, dyh, 256, D, "wgrad_w2", jobs)))
        small_parts["ffn2_norm"][l] = dg2

        dy, dp, dup, dxb = run(70, lambda jobs: _mix_out_bwd(dx, sv["up"], sv["p"], lw["w_up"], lw["w_out"], jobs))
        wgrad_done(("w_out", l), run(16, lambda jobs: _wgrad(sv["merged"], dxb, D, 512, "wgrad_w_out", jobs), False))
        wgrad_done(("w_up", l), run(25, lambda jobs: _wgrad_groups(sv["y"], dup, BW, D, "wgrad_w_up", jobs), False))
        (dp, d_pool_w, d_pool_scale, d_sconv, d_cconv, d_clg, d_clb, d_slg, d_slb, d_sgu_w, d_sgu_b) = run(
            130, lambda jobs: _mix_branches_bwd(sv["p"], dy, sv["z"], dp, *lw["branch"], jobs=jobs))
        wgrad_done(("w_in", l), run(80, lambda jobs: _wgrad(sv["hm"], dp, D, 512, "wgrad_w_in", jobs)))
        dx, dgm = run(91, lambda jobs: _mix_in_bwd(dp, lw["w_in"], sv["x1"], lw["gm"], dx, jobs))
        small_parts["mix_norm"][l] = dgm
        small_parts["pool_w"][l] = d_pool_w
        small_parts["pool_scale"][l] = d_pool_scale
        small_parts["sconv_w"][l] = d_sconv[:SK]
        small_parts["cconv_w"][l] = d_cconv[:CK]
        small_parts["cconv_ln_g"][l] = d_clg
        small_parts["cconv_ln_b"][l] = d_clb
        small_parts["sgu_ln_g"][l] = d_slg
        small_parts["sgu_ln_b"][l] = d_slb
        small_parts["sgu_w"][l] = d_sgu_w
        small_parts["sgu_b"][l] = jnp.sum(d_sgu_b, axis=-1)

        dx, dab, s_act, dyh, dg1 = run(108, lambda jobs: _ffn_bwd(
            dx, sv["x0"], lw["g1"], sv["ab1"], lw["w13a"], lw["w2a"], jobs))
        wgrad_done(("ffn1_w13", l), run(58, lambda jobs: _wgrad(sv["h1"], dab, D, 512, "wgrad_w13", jobs)))
        wgrad_done(("ffn1_w2", l), run(33, lambda jobs: _wgrad(s_act, dyh, 256, D, "wgrad_w2", jobs)))
        small_parts["ffn1_norm"][l] = dg1
    grad_x = dx[None]

    small_local = {nm: jnp.stack(parts).reshape(depth, *w[nm].shape[1:-1], -1) if nm not in ("sconv_w", "cconv_w")
                   else jnp.stack(parts) for nm, parts in small_parts.items()}
    small_local["final_norm"] = d_final.reshape(-1)
    sizes = [small_local[nm].size for nm in SMALL]
    total = sum(sizes)
    pad_to = NDEV * 8 * GW
    padded = -(-total // pad_to) * pad_to
    packed = jnp.concatenate([small_local[nm].reshape(-1) for nm in SMALL] + [jnp.zeros((padded - total,), F32)])
    groups, jobs = take_jobs(float("inf"))
    summed, job_outs = _all_reduce_small(packed.reshape(-1, GW), jobs)
    settle(groups, job_outs)
    summed = summed.reshape(-1)
    flushes = 0
    while pending:
        groups, jobs = take_jobs(float("inf"))
        settle(groups, _standalone(jobs, "grad_flush_%d" % flushes))
        flushes += 1
    big_grads = {nm: big_updates[nm][3].reshape(w[nm].shape) for nm in BIG}
    small_grads, off = {}, 0
    for nm, size in zip(SMALL, sizes):
        small_grads[nm] = summed[off:off + size].reshape(small_local[nm].shape)
        off += size
    for nm in ("sconv_w", "cconv_w"):
        small_grads[nm] = lax.dynamic_slice_in_dim(small_grads[nm], chip * GW, GW, axis=2)

    grads = {**big_grads, **small_grads}

    delta, new_m, new_v = {}, {}, {}
    for nm in BIG:
        delta[nm], new_m[nm], new_v[nm] = [a.reshape(w[nm].shape) for a in big_updates[nm][:3]]
    s_sizes = [w[nm].size for nm in SMALL]
    s_total = sum(s_sizes)
    s_padded = -(-s_total // (8 * GW)) * (8 * GW)

    def pack(tree):
        return jnp.concatenate([tree[nm].reshape(-1) for nm in SMALL] + [jnp.ones((s_padded - s_total,), F32)]).reshape(-1, GW)

    packed_out = _adamw(pack(w), pack(grads), pack(m), pack(v))
    off = 0
    for nm, size in zip(SMALL, s_sizes):
        for tree, arr in zip((delta, new_m, new_v), packed_out):
            tree[nm] = arr.reshape(-1)[off:off + size].reshape(w[nm].shape)
        off += size

    return (loss, grad_x, *[grads[nm] for nm in WEIGHTS], *[delta[nm] for nm in WEIGHTS],
            *[new_m[nm] for nm in WEIGHTS], *[new_v[nm] for nm in WEIGHTS])
```

```python
import jax
import jax.numpy as jnp
from jax import lax
from jax.experimental import pallas as pl
from jax.experimental.pallas import tpu as pltpu

D = 1024
FF = 2816
BW = 512
NBR = 4
MIXC = 4096
INC = 8192
GW = 128
CHUNK = 64
SK = 3
CK = 31
SKP = 8
CKP = 32
HALO = 32
TOKEN_TILE = 256
BRANCH_TILE = 512
EPS = 1e-6
NCHIP = 4
NDEV = 8

ADAM_LR = 0.001
ADAM_B1 = 0.9
ADAM_B2 = 0.999
ADAM_EPS = 1e-08
ADAM_WD = 0.01
ADAM_STEP = 10

VMEM_LIMIT = 56 * 1024 * 1024

BF = jnp.bfloat16
F32 = jnp.float32
MESH = pl.DeviceIdType.MESH
NT = (((1,), (1,)), ((), ()))
TN = (((0,), (0,)), ((), ()))


def _params(n_axes):
    return pltpu.CompilerParams(dimension_semantics=("arbitrary",) * n_axes, vmem_limit_bytes=VMEM_LIMIT)


class _Job:
    def __init__(self, args, n_inplace, fresh, nsem, phases):
        self.args, self.n_inplace, self.fresh, self.nsem, self.phases = list(args), n_inplace, list(fresh), nsem, phases


def _pc(jobs, body, name, in_specs, out_specs, out_shape, grid=(), scratch_shapes=(), input_output_aliases=None,
        compiler_params=None, grid_spec_scalars=None):
    single = not isinstance(out_shape, (list, tuple))
    core_out_specs = [out_specs] if single else list(out_specs)
    core_out_shape = [out_shape] if single else list(out_shape)
    n_in, n_out, n_scr = len(in_specs), len(core_out_specs), len(scratch_shapes)
    n_pre = 0 if grid_spec_scalars is None else 1
    all_in, all_out, all_shape = list(in_specs), list(core_out_specs), list(core_out_shape)
    all_scr, aliases, extra_args, layout = list(scratch_shapes), dict(input_output_aliases or {}), [], []
    for job in jobs:
        n_job_out = job.n_inplace + len(job.fresh)
        layout.append((len(all_in), len(job.args), len(all_out), n_job_out, len(all_scr)))
        for a in range(job.n_inplace):
            aliases[n_pre + len(all_in) + a] = len(all_out) + a
        all_in += [ANY] * len(job.args)
        extra_args += job.args
        all_out += [ANY] * n_job_out
        all_shape += [jax.ShapeDtypeStruct(a.shape, a.dtype) for a in job.args[:job.n_inplace]] + job.fresh
        all_scr += [pltpu.SemaphoreType.DMA((job.nsem,)), pltpu.SemaphoreType.DMA((job.nsem,))]
    steps = 1
    for extent in grid:
        steps *= extent
    events = []
    for (i0, na, o0, no, s0), job in zip(layout, jobs):
        for frac, fn in job.phases:
            events.append((min(int(frac * steps), steps - 1), frac >= 1.0, len(events), fn, (i0, na, o0, no, s0)))
    events.sort(key=lambda e: e[:3])

    def wrapped(*refs):
        pre, refs = refs[:n_pre], refs[n_pre:]
        ins, outs, scr = refs[:len(all_in)], refs[len(all_in):len(all_in) + len(all_out)], refs[len(all_in) + len(all_out):]
        step = 0
        for axis, extent in enumerate(grid):
            step = step * extent + pl.program_id(axis)

        def emit(event):
            at, _, _, fn, (i0, na, o0, no, s0) = event
            run = lambda: fn(ins[i0:i0 + na], outs[o0:o0 + no], scr[s0], scr[s0 + 1])
            if steps == 1:
                run()
            else:
                pl.when(step == at)(run)

        for event in events:
            if not event[1]:
                emit(event)
        body(*pre, *ins[:n_in], *outs[:n_out], *scr[:n_scr])
        for event in events:
            if event[1]:
                emit(event)

    kwargs = dict(name=name, out_shape=all_shape, input_output_aliases=aliases)
    if compiler_params is not None:
        kwargs["compiler_params"] = compiler_params
    if grid_spec_scalars is not None:
        kwargs["grid_spec"] = pltpu.PrefetchScalarGridSpec(
            num_scalar_prefetch=1, grid=grid, in_specs=all_in, out_specs=all_out, scratch_shapes=all_scr)
    else:
        kwargs.update(in_specs=all_in, out_specs=all_out, scratch_shapes=all_scr)
        if grid:
            kwargs["grid"] = grid
    call = pl.pallas_call(wrapped, **kwargs)

    def run_call(*args):
        pre_args = [] if grid_spec_scalars is None else [grid_spec_scalars]
        res = list(call(*pre_args, *args, *extra_args))
        core = res[0] if single else res[:n_out]
        job_outs = [res[o0:o0 + no] for (_, _, o0, no, _) in layout]
        return core, job_outs

    return run_call


def _full(shape):
    nd = len(shape)
    return pl.BlockSpec(shape, lambda *_: (0,) * nd, pipeline_mode=pl.Buffered(1))


def _rows(ts, width, col=0):
    return pl.BlockSpec((ts, width), lambda i: (i, col))


def _sig(v):
    return jax.nn.sigmoid(v)


def _rms_stats(x):
    r = lax.rsqrt(jnp.mean(x * x, axis=-1, keepdims=True) + EPS)
    return r, x * r


def _rms_bwd(x, g, dh):
    r, xh = _rms_stats(x)
    dg = jnp.sum(dh * xh, axis=0, keepdims=True)
    dxh = dh * g
    dx = r * (dxh - xh * jnp.mean(dxh * xh, axis=-1, keepdims=True))
    return dx, dg


def _accumulate(ref, val, first):
    @pl.when(first)
    def _():
        ref[...] = val

    @pl.when(jnp.logical_not(first))
    def _():
        ref[...] += val


def _ffn_fwd(x, g, w13, w2, jobs=()):
    s_len = x.shape[0]
    ts = min(TOKEN_TILE, s_len)

    def body(x_ref, g_ref, w13_ref, w2_ref, xo_ref, h_ref, ab_ref):
        xv = x_ref[...]
        r, xh = _rms_stats(xv)
        h = (xh * g_ref[...]).astype(BF)
        h_ref[...] = h
        ab = jnp.dot(h, w13_ref[...], preferred_element_type=F32)
        ab_ref[...] = ab.astype(BF)
        a = ab[:, :FF]
        b = ab[:, FF:]
        s = (a * _sig(a) * b).astype(BF)
        xo_ref[...] = xv + 0.5 * jnp.dot(s, w2_ref[...], preferred_element_type=F32)

    return _pc(
        jobs, body, name="ffn_fwd", grid=(s_len // ts,),
        in_specs=[_rows(ts, D), _full((1, D)), _full((D, 2 * FF)), _full((FF, D))],
        out_specs=[_rows(ts, D), _rows(ts, D), _rows(ts, 2 * FF)],
        out_shape=[jax.ShapeDtypeStruct((s_len, D), F32), jax.ShapeDtypeStruct((s_len, D), BF),
                   jax.ShapeDtypeStruct((s_len, 2 * FF), BF)],
        compiler_params=_params(1),
    )(x, g, w13, w2)


def _ffn_bwd(dxo, x, g, ab, w13, w2, jobs=()):
    s_len = x.shape[0]
    ts = min(TOKEN_TILE, s_len)

    def body(dxo_ref, x_ref, g_ref, ab_ref, w13_ref, w2_ref, dxi_ref, dab_ref, s_ref, dy_ref, dg_ref):
        i = pl.program_id(0)
        dxo_v = dxo_ref[...]
        dy = (0.5 * dxo_v).astype(BF)
        dy_ref[...] = dy
        ds = lax.dot_general(dy, w2_ref[...], NT, preferred_element_type=F32)
        a = ab_ref[:, :FF].astype(F32)
        b = ab_ref[:, FF:].astype(F32)
        sg = _sig(a)
        sil = a * sg
        s_ref[...] = (sil * b).astype(BF)
        dab_ref[:, :FF] = (ds * b * (sg * (1.0 + a * (1.0 - sg)))).astype(BF)
        dab_ref[:, FF:] = (ds * sil).astype(BF)
        dh = lax.dot_general(dab_ref[...], w13_ref[...], NT, preferred_element_type=F32)
        dx, dg = _rms_bwd(x_ref[...], g_ref[...], dh)
        dxi_ref[...] = dxo_v + dx
        _accumulate(dg_ref, dg, i == 0)

    return _pc(
        jobs, body, name="ffn_bwd", grid=(s_len // ts,),
        in_specs=[_rows(ts, D), _rows(ts, D), _full((1, D)), _rows(ts, 2 * FF), _full((D, 2 * FF)), _full((FF, D))],
        out_specs=[_rows(ts, D), _rows(ts, 2 * FF), _rows(ts, FF), _rows(ts, D), pl.BlockSpec((1, D), lambda i: (0, 0))],
        out_shape=[jax.ShapeDtypeStruct((s_len, D), F32), jax.ShapeDtypeStruct((s_len, 2 * FF), BF),
                   jax.ShapeDtypeStruct((s_len, FF), BF), jax.ShapeDtypeStruct((s_len, D), BF),
                   jax.ShapeDtypeStruct((1, D), F32)],
        compiler_params=_params(1),
    )(dxo, x, g, ab, w13, w2)


def _wgrad(a, b, tk, tn, name, jobs=()):
    s_len, k = a.shape
    n = b.shape[1]

    def body(a_ref, b_ref, o_ref):
        o_ref[...] = lax.dot_general(a_ref[...], b_ref[...], TN, preferred_element_type=F32).astype(BF)

    return _pc(
        jobs, body, name=name, grid=(k // tk, n // tn),
        in_specs=[pl.BlockSpec((s_len, tk), lambda i, j: (0, i)), pl.BlockSpec((s_len, tn), lambda i, j: (0, j))],
        out_specs=pl.BlockSpec((tk, tn), lambda i, j: (i, j)),
        out_shape=jax.ShapeDtypeStruct((k, n), BF),
        compiler_params=_params(2),
    )(a, b)


def _wgrad_groups(a, b, ka, nb, name, jobs=()):
    s_len = a.shape[0]
    groups = a.shape[1] // ka

    def body(a_ref, b_ref, o_ref):
        o_ref[...] = lax.dot_general(a_ref[...], b_ref[...], TN, preferred_element_type=F32).astype(BF)

    return _pc(
        jobs, body, name=name, grid=(groups,),
        in_specs=[pl.BlockSpec((s_len, ka), lambda gi: (0, gi)), pl.BlockSpec((s_len, nb), lambda gi: (0, gi))],
        out_specs=pl.BlockSpec((ka, nb), lambda gi: (gi, 0)),
        out_shape=jax.ShapeDtypeStruct((groups * ka, nb), BF),
        compiler_params=_params(1),
    )(a, b)


def _mix_in(x, g, w_in, jobs=()):
    s_len = x.shape[0]
    ts = min(TOKEN_TILE, s_len)

    def body(x_ref, g_ref, w_ref, h_ref, p_ref):
        _, xh = _rms_stats(x_ref[...])
        h = (xh * g_ref[...]).astype(BF)
        h_ref[...] = h
        p_ref[...] = jnp.dot(h, w_ref[...], preferred_element_type=F32).astype(BF)

    return _pc(
        jobs, body, name="mix_in", grid=(s_len // ts,),
        in_specs=[_rows(ts, D), _full((1, D)), _full((D, INC))],
        out_specs=[_rows(ts, D), _rows(ts, INC)],
        out_shape=[jax.ShapeDtypeStruct((s_len, D), BF), jax.ShapeDtypeStruct((s_len, INC), BF)],
        compiler_params=_params(1),
    )(x, g, w_in)


def _shift(e, j):
    n = e.shape[0]
    j = j % n
    return e if j == 0 else pltpu.roll(e, j, 0)


def _ln_stats(z):
    mu = jnp.mean(z, axis=-1, keepdims=True)
    zc = z - mu
    rs = lax.rsqrt(jnp.mean(zc * zc, axis=-1, keepdims=True) + EPS)
    return rs, zc * rs


def _ln_bwd(rs, zn, dzn):
    return rs * (dzn - jnp.mean(dzn, axis=-1, keepdims=True) - zn * jnp.mean(dzn * zn, axis=-1, keepdims=True))


_GELU_C0 = 0.7978845608028654
_GELU_C1 = 0.044715


def _gelu(p):
    th = jnp.tanh(_GELU_C0 * (p + _GELU_C1 * p * p * p))
    return 0.5 * p * (1.0 + th), th


def _gelu_grad(p, th):
    return 0.5 * (1.0 + th) + 0.5 * p * (1.0 - th * th) * (_GELU_C0 * (1.0 + 3.0 * _GELU_C1 * p * p))


def _pool_diff(a, t, sign):
    outs = []
    for gi in range(NBR):
        win = 2 ** (gi + 1)
        ag = a[:, gi * GW:(gi + 1) * GW]
        cnt = jnp.clip(t + 1, 1, win).astype(F32)
        ws = ag if sign > 0 else ag / cnt
        for s in range(gi + 1):
            ws = ws + _shift(ws, sign * (2 ** s))
        outs.append((ws / cnt if sign > 0 else ws) - ag)
    return outs


def _sgu_mask():
    row = lax.broadcasted_iota(jnp.int32, (GW, GW), 0)
    col = lax.broadcasted_iota(jnp.int32, (GW, GW), 1)
    return (col // CHUNK) <= (row // CHUNK)


def _assemble(pe_ref, prev_ref, cur_ref, next_ref, i, last, ts):
    pe_ref[0:HALO, :] = jnp.where(i > 0, prev_ref[...], jnp.zeros_like(prev_ref))
    pe_ref[HALO:HALO + ts, :] = cur_ref[...]
    if next_ref is not None:
        pe_ref[HALO + ts:, :] = jnp.where(i < last, next_ref[...], jnp.zeros_like(next_ref))


def _halo_specs(ts, width, s_len, with_next):
    per = ts // HALO
    specs = [pl.BlockSpec((HALO, width), lambda i: (jnp.maximum(i * per - 1, 0), 0)),
             pl.BlockSpec((ts, width), lambda i: (i, 0))]
    if with_next:
        specs.append(pl.BlockSpec((HALO, width), lambda i: (jnp.minimum((i + 1) * per, s_len // HALO - 1), 0)))
    return specs


def _branch_weights_specs():
    return [_full((NBR, GW, GW)), _full((1, BW)), _full((SKP, BW)), _full((CKP, BW)), _full((1, BW)), _full((1, BW)),
            _full((1, BW)), _full((1, BW)), _full((NBR, GW, GW)), _full((GW, NBR))]


def _mix_branches_fwd(p, pool_w, pool_scale, sconv_w, cconv_w, cln_g, cln_b, sln_g, sln_b, sgu_w, sgu_bt, jobs=()):
    s_len = p.shape[0]
    ts = min(BRANCH_TILE, s_len)
    ext = HALO + ts

    def body(pp_ref, pc_ref, pw_ref, ps_ref, sw_ref, cw_ref, clg_ref, clb_ref, slg_ref, slb_ref, gw_ref, gb_ref,
             y_ref, z_ref, pe_ref):
        i = pl.program_id(0)
        _assemble(pe_ref, pp_ref, pc_ref, None, i, 0, ts)
        t = i * ts - HALO + lax.broadcasted_iota(jnp.int32, (ext, 1), 0)

        dgs = _pool_diff(pe_ref[:, 0:BW].astype(F32), t, 1)
        for gi in range(NBR):
            e = jnp.dot(dgs[gi][HALO:].astype(BF), pw_ref[gi].astype(BF), preferred_element_type=F32)
            y_ref[:, gi * GW:(gi + 1) * GW] = (e * ps_ref[:, gi * GW:(gi + 1) * GW]).astype(BF)

        xin = pe_ref[:, BW:2 * BW].astype(F32)
        cg = pe_ref[:, 3 * BW:4 * BW].astype(F32)
        q = cg * xin
        cv = sw_ref[2:3, :] * q + sw_ref[1:2, :] * _shift(q, 1) + sw_ref[0:1, :] * _shift(q, 2)
        y_ref[:, BW:2 * BW] = (pe_ref[HALO:, 2 * BW:3 * BW].astype(F32) * cv[HALO:]).astype(BF)

        yg = pe_ref[:, 4 * BW:5 * BW].astype(F32) * _sig(pe_ref[:, 5 * BW:6 * BW].astype(F32))
        z = cw_ref[CK - 1:CK, :] * yg
        for j in range(1, CK):
            z = z + cw_ref[CK - 1 - j:CK - j, :] * _shift(yg, j)
        z_ref[...] = z[HALO:].astype(BF)
        _, zn = _ln_stats(z[HALO:])
        nn = zn * clg_ref[...] + clb_ref[...]
        y_ref[:, 2 * BW:3 * BW] = (nn * _sig(nn)).astype(BF)

        u, _ = _gelu(pc_ref[:, 6 * BW:7 * BW].astype(F32))
        v, _ = _gelu(pc_ref[:, 7 * BW:8 * BW].astype(F32))
        _, vn = _ln_stats(v)
        vn = (vn * slg_ref[...] + slb_ref[...]).astype(BF)
        mask = _sgu_mask()
        for hd in range(NBR):
            wm = jnp.where(mask, gw_ref[hd], 0.0).astype(BF)
            for blk in range(ts // GW):
                rows = slice(blk * GW, (blk + 1) * GW)
                cols = slice(hd * GW, (hd + 1) * GW)
                zz = jnp.dot(wm, vn[rows, cols], preferred_element_type=F32) + gb_ref[:, hd:hd + 1]
                y_ref[rows, 3 * BW + hd * GW:3 * BW + (hd + 1) * GW] = (u[rows, cols] * zz).astype(BF)

    return _pc(
        jobs, body, name="mix_branches_fwd", grid=(s_len // ts,),
        in_specs=_halo_specs(ts, MIXC, s_len, False) + _branch_weights_specs(),
        out_specs=[_rows(ts, NBR * BW), _rows(ts, BW)],
        out_shape=[jax.ShapeDtypeStruct((s_len, NBR * BW), BF), jax.ShapeDtypeStruct((s_len, BW), BF)],
        scratch_shapes=[pltpu.VMEM((ext, MIXC), BF)],
        compiler_params=_params(1),
    )(p, p, pool_w, pool_scale, sconv_w, cconv_w, cln_g, cln_b, sln_g, sln_b, sgu_w, sgu_bt)


def _mix_branches_bwd(p, dy, z, dp, pool_w, pool_scale, sconv_w, cconv_w, cln_g, cln_b, sln_g, sln_b, sgu_w, sgu_bt,
                      jobs=()):
    s_len = p.shape[0]
    ts = min(BRANCH_TILE, s_len)
    ext = ts + 2 * HALO
    last = s_len // ts - 1
    tile = slice(HALO, HALO + ts)
    small_shapes = [(NBR, GW, GW), (1, BW), (SKP, BW), (CKP, BW), (1, BW), (1, BW), (1, BW), (1, BW), (NBR, GW, GW),
                    (NBR, GW, GW)]

    def body(pp_ref, pc_ref, pn_ref, dyp_ref, dyc_ref, dyn_ref, zc_ref, zn_ref, dpin_ref,
             pw_ref, ps_ref, sw_ref, cw_ref, clg_ref, clb_ref, slg_ref, slb_ref, gw_ref, gb_ref,
             dp_ref, dpw_ref, dps_ref, dsw_ref, dcw_ref, dclg_ref, dclb_ref, dslg_ref, dslb_ref, dgw_ref, dgb_ref,
             pe_ref, de_ref):
        del dyp_ref, dpin_ref
        i = pl.program_id(0)
        first = i == 0
        _assemble(pe_ref, pp_ref, pc_ref, pn_ref, i, last, ts)
        de_ref[0:HALO, :] = jnp.zeros((HALO, NBR * BW), BF)
        de_ref[HALO:HALO + ts, :] = dyc_ref[...]
        de_ref[HALO + ts:, :] = jnp.where(i < last, dyn_ref[...], jnp.zeros_like(dyn_ref))
        t = i * ts - HALO + lax.broadcasted_iota(jnp.int32, (ext, 1), 0)

        @pl.when(first)
        def _():
            dsw_ref[...] = jnp.zeros((SKP, BW), F32)
            dcw_ref[...] = jnp.zeros((CKP, BW), F32)

        dgs = _pool_diff(pe_ref[:, 0:BW].astype(F32), t, 1)
        dya = de_ref[:, 0:BW].astype(F32)
        dds = []
        for gi in range(NBR):
            cols = slice(gi * GW, (gi + 1) * GW)
            pw = pw_ref[gi].astype(BF)
            d_t = dgs[gi][tile].astype(BF)
            e = jnp.dot(d_t, pw, preferred_element_type=F32)
            _accumulate(dps_ref.at[:, cols], jnp.sum(dya[tile, cols] * e, axis=0, keepdims=True), first)
            de_g = (dya[:, cols] * ps_ref[:, cols]).astype(BF)
            _accumulate(dpw_ref.at[gi], lax.dot_general(d_t, de_g[tile], TN, preferred_element_type=F32), first)
            dds.append(lax.dot_general(de_g, pw, NT, preferred_element_type=F32))
        das = _pool_diff(jnp.concatenate(dds, axis=1), t, -1)
        for gi in range(NBR):
            dp_ref[:, gi * GW:(gi + 1) * GW] = das[gi][tile].astype(BF)

        xin = pe_ref[:, BW:2 * BW].astype(F32)
        bg = pe_ref[:, 2 * BW:3 * BW].astype(F32)
        cg = pe_ref[:, 3 * BW:4 * BW].astype(F32)
        q = cg * xin
        qs = [q, _shift(q, 1), _shift(q, 2)]
        cv = sw_ref[2:3, :] * qs[0] + sw_ref[1:2, :] * qs[1] + sw_ref[0:1, :] * qs[2]
        dyb = de_ref[:, BW:2 * BW].astype(F32)
        dcv = dyb * bg
        for j in range(SK):
            dsw_ref[SK - 1 - j:SK - j, :] += jnp.sum(dcv[tile] * qs[j][tile], axis=0, keepdims=True)
        dq = sw_ref[2:3, :] * dcv + sw_ref[1:2, :] * _shift(dcv, -1) + sw_ref[0:1, :] * _shift(dcv, -2)
        dp_ref[:, BW:2 * BW] = (dq * cg)[tile].astype(BF)
        dp_ref[:, 2 * BW:3 * BW] = (dyb * cv)[tile].astype(BF)
        dp_ref[:, 3 * BW:4 * BW] = (dq * xin)[tile].astype(BF)

        ca = pc_ref[:, 4 * BW:5 * BW].astype(F32)
        sb = _sig(pc_ref[:, 5 * BW:6 * BW].astype(F32))
        yg_t = ca * sb
        z = jnp.concatenate([zc_ref[...].astype(F32), zn_ref[...].astype(F32)], axis=0)
        rs, zn = _ln_stats(z)
        nn = zn * clg_ref[...] + clb_ref[...]
        sn = _sig(nn)
        dn = de_ref[HALO:, 2 * BW:3 * BW].astype(F32) * (sn * (1.0 + nn * (1.0 - sn)))
        _accumulate(dclg_ref, jnp.sum((dn * zn)[:ts], axis=0, keepdims=True), first)
        _accumulate(dclb_ref, jnp.sum(dn[:ts], axis=0, keepdims=True), first)
        dz = _ln_bwd(rs, zn, dn * clg_ref[...])
        dyg = cw_ref[CK - 1:CK, :] * dz[:ts]
        dcw_ref[CK - 1:CK, :] += jnp.sum(dz[:ts] * yg_t, axis=0, keepdims=True)
        for j in range(1, CK):
            dz_ahead = _shift(dz, -j)[:ts]
            dyg = dyg + cw_ref[CK - 1 - j:CK - j, :] * dz_ahead
            dcw_ref[CK - 1 - j:CK - j, :] += jnp.sum(dz_ahead * yg_t, axis=0, keepdims=True)
        dp_ref[:, 4 * BW:5 * BW] = (dyg * sb).astype(BF)
        dp_ref[:, 5 * BW:6 * BW] = (dyg * ca * sb * (1.0 - sb)).astype(BF)

        pu = pc_ref[:, 6 * BW:7 * BW].astype(F32)
        pv = pc_ref[:, 7 * BW:8 * BW].astype(F32)
        u, thu = _gelu(pu)
        v, thv = _gelu(pv)
        vrs, vn0 = _ln_stats(v)
        vn = (vn0 * slg_ref[...] + slb_ref[...]).astype(BF)
        dyd = dyc_ref[:, 3 * BW:4 * BW].astype(F32)
        dzz = dyd * u
        dzb = dzz.astype(BF)
        mask = _sgu_mask()
        dvn_cols = []
        for hd in range(NBR):
            cols = slice(hd * GW, (hd + 1) * GW)
            wm = jnp.where(mask, gw_ref[hd], 0.0).astype(BF)
            dwm = jnp.zeros((GW, GW), F32)
            dbs = jnp.zeros((GW, GW), F32)
            dvn_rows = []
            for blk in range(ts // GW):
                rows = slice(blk * GW, (blk + 1) * GW)
                zz = jnp.dot(wm, vn[rows, cols], preferred_element_type=F32) + gb_ref[:, hd:hd + 1]
                dp_ref[rows, 6 * BW + hd * GW:6 * BW + (hd + 1) * GW] = (
                    dyd[rows, cols] * zz * _gelu_grad(pu[rows, cols], thu[rows, cols])).astype(BF)
                dwm = dwm + lax.dot_general(dzb[rows, cols], vn[rows, cols], NT, preferred_element_type=F32)
                dbs = dbs + dzz[rows, cols]
                dvn_rows.append(lax.dot_general(wm, dzb[rows, cols], TN, preferred_element_type=F32))
            _accumulate(dgw_ref.at[hd], jnp.where(mask, dwm, 0.0), first)
            _accumulate(dgb_ref.at[hd], dbs, first)
            dvn_cols.append(jnp.concatenate(dvn_rows, axis=0))
        dvn = jnp.concatenate(dvn_cols, axis=1)
        _accumulate(dslg_ref, jnp.sum(dvn * vn0, axis=0, keepdims=True), first)
        _accumulate(dslb_ref, jnp.sum(dvn, axis=0, keepdims=True), first)
        dv = _ln_bwd(vrs, vn0, dvn * slg_ref[...])
        dp_ref[:, 7 * BW:8 * BW] = (dv * _gelu_grad(pv, thv)).astype(BF)

    const = lambda shp: pl.BlockSpec(shp, lambda i: (0,) * len(shp))
    return _pc(
        jobs, body, name="mix_branches_bwd", grid=(s_len // ts,),
        in_specs=(_halo_specs(ts, MIXC, s_len, True) + _halo_specs(ts, NBR * BW, s_len, True)
                  + _halo_specs(ts, BW, s_len, True)[1:] + [pl.BlockSpec(memory_space=pl.ANY)] + _branch_weights_specs()),
        out_specs=[pl.BlockSpec((ts, MIXC), lambda i: (i, 0))] + [const(s) for s in small_shapes],
        out_shape=[jax.ShapeDtypeStruct((s_len, INC), BF)] + [jax.ShapeDtypeStruct(s, F32) for s in small_shapes],
        scratch_shapes=[pltpu.VMEM((ext, MIXC), BF), pltpu.VMEM((ext, NBR * BW), BF)],
        input_output_aliases={8: 0},
        compiler_params=_params(1),
    )(p, p, p, dy, dy, dy, z, z, dp, pool_w, pool_scale, sconv_w, cconv_w, cln_g, cln_b, sln_g, sln_b, sgu_w, sgu_bt)


def _mix_out_fwd(x, y, p, w_up, w_out, jobs=()):
    s_len = x.shape[0]
    ts = min(TOKEN_TILE, s_len)

    def body(x_ref, y_ref, pg_ref, wu_ref, wo_ref, xo_ref, m_ref, up_ref):
        m = jnp.zeros((ts, D), F32)
        for gi in range(NBR):
            up = jnp.dot(y_ref[:, gi * BW:(gi + 1) * BW], wu_ref[gi], preferred_element_type=F32)
            up_ref[:, gi * D:(gi + 1) * D] = up.astype(BF)
            m = m + _sig(pg_ref[:, gi * D:(gi + 1) * D].astype(F32)) * up
        mb = m.astype(BF)
        m_ref[...] = mb
        xo_ref[...] = x_ref[...] + jnp.dot(mb, wo_ref[...], preferred_element_type=F32)

    return _pc(
        jobs, body, name="mix_out_fwd", grid=(s_len // ts,),
        in_specs=[_rows(ts, D), _rows(ts, NBR * BW), _rows(ts, NBR * D, 1), _full((NBR, BW, D)), _full((D, D))],
        out_specs=[_rows(ts, D), _rows(ts, D), _rows(ts, NBR * D)],
        out_shape=[jax.ShapeDtypeStruct((s_len, D), F32), jax.ShapeDtypeStruct((s_len, D), BF),
                   jax.ShapeDtypeStruct((s_len, NBR * D), BF)],
        compiler_params=_params(1),
    )(x, y, p, w_up, w_out)


def _mix_out_bwd(dxo, up, p, w_up, w_out, jobs=()):
    s_len = dxo.shape[0]
    ts = min(TOKEN_TILE, s_len)

    def body(dxo_ref, up_ref, pg_ref, wu_ref, wo_ref, dy_ref, dp_ref, dup_ref, dxb_ref):
        dxb = dxo_ref[...].astype(BF)
        dxb_ref[...] = dxb
        dm = lax.dot_general(dxb, wo_ref[...], NT, preferred_element_type=F32)
        for gi in range(NBR):
            cols = slice(gi * D, (gi + 1) * D)
            gate = _sig(pg_ref[:, cols].astype(F32))
            dp_ref[:, cols] = (dm * up_ref[:, cols].astype(F32) * gate * (1.0 - gate)).astype(BF)
            dup = (dm * gate).astype(BF)
            dup_ref[:, cols] = dup
            dy_ref[:, gi * BW:(gi + 1) * BW] = lax.dot_general(
                dup, wu_ref[gi], NT, preferred_element_type=F32).astype(BF)

    return _pc(
        jobs, body, name="mix_out_bwd", grid=(s_len // ts,),
        in_specs=[_rows(ts, D), _rows(ts, NBR * D), _rows(ts, NBR * D, 1), _full((NBR, BW, D)), _full((D, D))],
        out_specs=[_rows(ts, NBR * BW), _rows(ts, NBR * D, 1), _rows(ts, NBR * D), _rows(ts, D)],
        out_shape=[jax.ShapeDtypeStruct((s_len, NBR * BW), BF), jax.ShapeDtypeStruct((s_len, INC), BF),
                   jax.ShapeDtypeStruct((s_len, NBR * D), BF), jax.ShapeDtypeStruct((s_len, D), BF)],
        compiler_params=_params(1),
    )(dxo, up, p, w_up, w_out)


def _mix_in_bwd(dp, w_in, x, g, dxo, jobs=()):
    s_len = x.shape[0]
    ts = min(TOKEN_TILE, s_len)

    def body(dp_ref, w_ref, x_ref, g_ref, dxo_ref, dxi_ref, dg_ref):
        i = pl.program_id(0)
        dh = lax.dot_general(dp_ref[...], w_ref[...], NT, preferred_element_type=F32)
        dx, dg = _rms_bwd(x_ref[...], g_ref[...], dh)
        dxi_ref[...] = dxo_ref[...] + dx
        _accumulate(dg_ref, dg, i == 0)

    return _pc(
        jobs, body, name="mix_in_bwd", grid=(s_len // ts,),
        in_specs=[_rows(ts, INC), _full((D, INC)), _rows(ts, D), _full((1, D)), _rows(ts, D)],
        out_specs=[_rows(ts, D), pl.BlockSpec((1, D), lambda i: (0, 0))],
        out_shape=[jax.ShapeDtypeStruct((s_len, D), F32), jax.ShapeDtypeStruct((1, D), F32)],
        compiler_params=_params(1),
    )(dp, w_in, x, g, dxo)


def _loss_head(x, g, target, jobs=()):
    s_len = x.shape[0]
    ts = min(512, s_len)

    def body(x_ref, g_ref, t_ref, dx_ref, dg_ref, loss_ref):
        i = pl.program_id(0)
        xv = x_ref[...]
        _, xh = _rms_stats(xv)
        err = xh * g_ref[...] - t_ref[...]
        part = 0.5 * jnp.sum(jnp.mean(err * err, axis=-1, keepdims=True), axis=0, keepdims=True)
        dx, dg = _rms_bwd(xv, g_ref[...], err * (1.0 / D))
        dx_ref[...] = dx
        _accumulate(dg_ref, dg, i == 0)
        _accumulate(loss_ref, jnp.broadcast_to(part, (1, GW)), i == 0)

    return _pc(
        jobs, body, name="loss_head", grid=(s_len // ts,),
        in_specs=[_rows(ts, D), _full((1, D)), _rows(ts, D)],
        out_specs=[_rows(ts, D), pl.BlockSpec((1, D), lambda i: (0, 0)), pl.BlockSpec((1, GW), lambda i: (0, 0))],
        out_shape=[jax.ShapeDtypeStruct((s_len, D), F32), jax.ShapeDtypeStruct((1, D), F32),
                   jax.ShapeDtypeStruct((1, GW), F32)],
        compiler_params=_params(1),
    )(x, g, target)


SUM_TILE = 1 << 20
ADAM_TILE = 1 << 19
CAST_TILE = 1 << 19


def _row_tile(rows, cols, budget=1 << 18):
    tr = rows
    while tr * cols > budget and tr % 16 == 0:
        tr //= 2
    return tr


def _elementwise(fn, name, ins, out_dtypes):
    rows, cols = ins[0].shape
    tr = _row_tile(rows, cols)
    n_in = len(ins)

    def body(*refs):
        res = fn(*[r[...] for r in refs[:n_in]])
        for o_ref, val in zip(refs[n_in:], res):
            o_ref[...] = val.astype(o_ref.dtype)

    outs = pl.pallas_call(
        body, name=name, grid=(rows // tr,),
        in_specs=[_rows(tr, cols)] * n_in, out_specs=[_rows(tr, cols)] * len(out_dtypes),
        out_shape=[jax.ShapeDtypeStruct((rows, cols), dt) for dt in out_dtypes],
        compiler_params=_params(1),
    )(*ins)
    return outs


def _tiled(fn, name, grid, in_specs, out_specs, out_shape, args, scalars=None, alias=None):
    alias = alias or {}
    n_in = len(in_specs) - len(alias)
    n_pre = 0 if scalars is None else 1

    def body(*refs):
        refs = refs[n_pre:]
        res = fn(*[r[...] for r in refs[:n_in]])
        for o_ref, val in zip(refs[len(in_specs):], res):
            o_ref[...] = val.astype(o_ref.dtype)

    aliases = {n_pre + pos: out for pos, out in alias.items()}
    if scalars is None:
        return pl.pallas_call(body, name=name, grid=grid, in_specs=in_specs, out_specs=out_specs, out_shape=out_shape,
                              input_output_aliases=aliases, compiler_params=_params(len(grid)))(*args)
    spec = pltpu.PrefetchScalarGridSpec(num_scalar_prefetch=1, grid=grid, in_specs=in_specs, out_specs=out_specs)
    return pl.pallas_call(body, name=name, grid_spec=spec, out_shape=out_shape, input_output_aliases=aliases,
                          compiler_params=_params(len(grid)))(scalars, *args)


def _cast_into(shard, layer, col, dtype, sc):
    ks, ns = shard.shape[1:]
    tr = _row_tile(ks, ns, SUM_TILE)
    full = (ks, ns * NCHIP) if col else (ks * NCHIP, ns)
    out_idx = (lambda i, s: (i, s[1])) if col else (lambda i, s: (s[1] * (ks // tr) + i, 0))
    return _tiled(lambda v: (v,), "cast_into", (ks // tr,), [pl.BlockSpec((None, tr, ns), lambda i, s: (layer, i, 0))],
                  [pl.BlockSpec((tr, ns), out_idx)], [jax.ShapeDtypeStruct(full, dtype)], [shard], sc)[0]


def _cast_rest(items, jobs=()):
    n = len(items)
    tiles, slot_sets, counts = [], {}, {}
    for k, (shard, _, _, dt) in enumerate(items):
        ks, ns = shard.shape[1:]
        tr = _row_tile(ks, ns, CAST_TILE)
        which = slot_sets.setdefault((tr, ns, shard.dtype, dt), len(slot_sets))
        for t in range(ks // tr):
            tiles.append((k, t, tr, which, counts.get(which, 0) % 2))
            counts[which] = counts.get(which, 0) + 1
    n_sets = len(slot_sets)

    def body(*refs):
        ins, outs = refs[:n], refs[n:2 * n]
        bufs = refs[2 * n:2 * n + 2 * n_sets]
        in_sems, out_sems = refs[2 * n + 2 * n_sets:]
        chip = 2 * lax.axis_index("x") + lax.axis_index("y")
        fetch, store = [], []
        for k, t, tr, which, slot in tiles:
            shard, layer, col, _ = items[k]
            ks, ns = shard.shape[1:]
            if col:
                place = outs[k].at[pl.ds(t * tr, tr), pl.ds(pl.multiple_of(chip * ns, GW), ns)]
            else:
                place = outs[k].at[pl.ds(pl.multiple_of(chip * ks + t * tr, 16), tr), :]
            fetch.append(pltpu.make_async_copy(
                ins[k].at[layer, pl.ds(t * tr, tr), :], bufs[2 * which].at[slot], in_sems.at[which, slot]))
            store.append(pltpu.make_async_copy(bufs[2 * which + 1].at[slot], place, out_sems.at[which, slot]))
        busy = {}
        fetch[0].start()
        for i, (k, t, tr, which, slot) in enumerate(tiles):
            if i + 1 < len(tiles):
                fetch[i + 1].start()
            fetch[i].wait()
            if (which, slot) in busy:
                store[busy[which, slot]].wait()
            bufs[2 * which + 1][slot] = bufs[2 * which][slot].astype(bufs[2 * which + 1].dtype)
            store[i].start()
            busy[which, slot] = i
        for i in busy.values():
            store[i].wait()

    def full_shape(shard, col):
        ks, ns = shard.shape[1:]
        return (ks, ns * NCHIP) if col else (ks * NCHIP, ns)

    scratch = []
    for (tr, ns, dt_in, dt_out) in slot_sets:
        scratch += [pltpu.VMEM((2, tr, ns), dt_in), pltpu.VMEM((2, tr, ns), dt_out)]
    scratch += [pltpu.SemaphoreType.DMA((n_sets, 2)), pltpu.SemaphoreType.DMA((n_sets, 2))]
    return _pc(
        jobs, body, name="cast_rest", in_specs=[ANY] * n, out_specs=[ANY] * n,
        out_shape=[jax.ShapeDtypeStruct(full_shape(shard, col), dt) for shard, _, col, dt in items],
        scratch_shapes=scratch, compiler_params=pltpu.CompilerParams(vmem_limit_bytes=VMEM_LIMIT),
    )(*[shard for shard, _, _, _ in items])


def _pair_sum(g, got, col, sc):
    hk, hn = got.shape
    tr = _row_tile(hk, hn, SUM_TILE)
    g_idx = (lambda i, s: (s[0] * (hk // tr) + i, 0)) if col else (lambda i, s: (i, s[0]))
    plain = pl.BlockSpec((tr, hn), lambda i, s: (i, 0))
    return _tiled(lambda a, b: (a.astype(F32) + b.astype(F32),), "pair_sum", (hk // tr,),
                  [pl.BlockSpec((tr, hn), g_idx), plain], [plain], [jax.ShapeDtypeStruct((hk, hn), BF)], [g, got], sc)[0]


def _chip_sum(ph, got, col, layer, depth, sc, carry):
    qk, qn = got.shape[1:]
    tr = _row_tile(qk, qn, SUM_TILE)
    ph_idx = (lambda i, s: (i, s[1])) if col else (lambda i, s: (s[1] * (qk // tr) + i, 0))
    out_shape = (depth, 2 * qk, qn) if col else (depth, qk, 2 * qn)
    out_idx = (lambda i, s: (layer, s[0] * (qk // tr) + i, 0)) if col else (lambda i, s: (layer, i, s[0]))
    in_specs = [pl.BlockSpec((tr, qn), ph_idx)] + [pl.BlockSpec((None, tr, qn), lambda i, s, j=j: (j, i, 0)) for j in range(3)]
    args = [ph, got, got, got]
    if carry is not None:
        in_specs.append(ANY)
        args.append(carry)
    return _tiled(lambda a, b, c_, d_: (a.astype(F32) + b.astype(F32) + c_.astype(F32) + d_.astype(F32),), "chip_sum",
                  (qk // tr,), in_specs, [pl.BlockSpec((None, tr, qn), out_idx)],
                  [jax.ShapeDtypeStruct(out_shape, F32)], args, sc, alias=None if carry is None else {4: 0})[0]


def _adamw_math(w, g, m, v):
    m = ADAM_B1 * m + (1.0 - ADAM_B1) * g
    v = ADAM_B2 * v + (1.0 - ADAM_B2) * (g * g)
    m_hat = m / (1.0 - ADAM_B1 ** ADAM_STEP)
    v_hat = v / (1.0 - ADAM_B2 ** ADAM_STEP)
    delta = -ADAM_LR * (m_hat / (jnp.sqrt(v_hat) + ADAM_EPS) + ADAM_WD * w)
    return delta, m, v


def _adamw_layer(w, g, m, v, layer, carry):
    k, n = w.shape[1:]
    tr = _row_tile(k, n, ADAM_TILE)
    blk = pl.BlockSpec((None, tr, n), lambda i: (layer, i, 0))
    carry = list(carry or [])
    return _tiled(lambda *a: (*_adamw_math(*a), a[1]), "adamw_layer", (k // tr,), [blk] * 4 + [ANY] * len(carry),
                  [blk] * 4, [jax.ShapeDtypeStruct(w.shape, F32)] * 4, [w, g, m, v] + carry, None,
                  alias={4 + pos: pos for pos in range(len(carry))})


def _adamw(w, g, m, v):
    shape = w.shape
    two_d = lambda a: a.reshape(-1, shape[-1])
    outs = _elementwise(_adamw_math, "adamw", [two_d(w), two_d(g), two_d(m), two_d(v)], [F32, F32, F32])
    return [o.reshape(shape) for o in outs]


ANY = pl.BlockSpec(memory_space=pl.ANY)


def _place():
    x, y, c = lax.axis_index("x"), lax.axis_index("y"), lax.axis_index("c")
    chips = [(1 - x, y), (x, 1 - y), (1 - x, 1 - y)]
    return x, y, c, chips


def _cols(ref, start, size):
    idx = (slice(None),) * (len(ref.shape) - 1) + (pl.ds(pl.multiple_of(start, GW), size),)
    return ref.at[idx]


def _rows_of(ref, start, size):
    nd = len(ref.shape)
    idx = (slice(None),) * (nd - 2) + (pl.ds(pl.multiple_of(start, 16), size), slice(None))
    return ref.at[idx]


def _region(ref, col_sharded, chip, half):
    k, n = ref.shape
    align = 16 if ref.dtype == BF else 8
    if col_sharded:
        return ref.at[pl.ds(pl.multiple_of(half * (k // 2), align), k // 2),
                      pl.ds(pl.multiple_of(chip * (n // NCHIP), GW), n // NCHIP)]
    rows = k // (2 * NCHIP)
    return ref.at[pl.ds(pl.multiple_of((2 * chip + half) * rows, align), rows), :]


def _job_gather(bufs, col_sharded, handoff):
    n = len(bufs)

    def copies(outs, send_sems, recv_sems, stage):
        x, y, c, chips = _place()
        sends, lands = [], []
        for k in range(n):
            for j, chip in enumerate(chips):
                theirs = 2 * chip[0] + chip[1]
                if stage == 0:
                    src, to = _region(outs[k], col_sharded[k], 2 * x + y, c), (*chip, c)
                    land = _region(outs[k], col_sharded[k], theirs, c)
                else:
                    src, to = _region(outs[k], col_sharded[k], theirs, c), (x, y, 1 - c)
                    land = _region(outs[k], col_sharded[k], theirs, 1 - c)
                sem = 3 * n * stage + 3 * k + j
                sems = dict(send_sem=send_sems.at[sem], recv_sem=recv_sems.at[sem], device_id=to, device_id_type=MESH)
                sends.append(pltpu.make_async_remote_copy(src_ref=src, dst_ref=src, **sems))
                lands.append(pltpu.make_async_remote_copy(src_ref=land, dst_ref=land, **sems))
        return sends, lands

    def start(ins, outs, send_sems, recv_sems):
        for cp in copies(outs, send_sems, recv_sems, 0)[0]:
            cp.start()

    def hand_on(ins, outs, send_sems, recv_sems):
        for cp in copies(outs, send_sems, recv_sems, 0)[1]:
            cp.wait_recv()
        for cp in copies(outs, send_sems, recv_sems, 1)[0]:
            cp.start()

    def finish(ins, outs, send_sems, recv_sems):
        sends, lands = copies(outs, send_sems, recv_sems, 1)
        for cp in lands:
            cp.wait_recv()
        for cp in copies(outs, send_sems, recv_sems, 0)[0] + sends:
            cp.wait_send()

    return _Job(bufs, n, [], 6 * n, [(0.0, start), (handoff, hand_on), (1.0, finish)])


def _half(ref, col_sharded, c):
    k, n = ref.shape[-2:]
    return _rows_of(ref, c * (k // 2), k // 2) if col_sharded else _cols(ref, c * (n // 2), n // 2)


def _quarter(ref, col_sharded, j):
    k, n = ref.shape[-2:]
    return _cols(ref, j * (n // NCHIP), n // NCHIP) if col_sharded else _rows_of(ref, j * (k // NCHIP), k // NCHIP)


def _job_pair(grads, col_sharded):
    n = len(grads)

    def half_shape(g, col):
        return (g.shape[0] // 2, g.shape[1]) if col else (g.shape[0], g.shape[1] // 2)

    def copies(ins, got, send_sems, recv_sems):
        x, y, c, _ = _place()
        return [pltpu.make_async_remote_copy(
            src_ref=_half(ins[k], col_sharded[k], 1 - c), dst_ref=got[k], send_sem=send_sems.at[k],
            recv_sem=recv_sems.at[k], device_id=(x, y, 1 - c), device_id_type=MESH) for k in range(n)]

    def start(*refs):
        for cp in copies(*refs):
            cp.start()

    def finish(*refs):
        for cp in copies(*refs):
            cp.wait()

    fresh = [jax.ShapeDtypeStruct(half_shape(g, col), g.dtype) for g, col in zip(grads, col_sharded)]
    return _Job(grads, 0, fresh, n, [(0.0, start), (1.0, finish)])


def _job_chip(halves, col_sharded):
    n = len(halves)

    def quarter_shape(h, col):
        return (3, h.shape[0], h.shape[1] // NCHIP) if col else (3, h.shape[0] // NCHIP, h.shape[1])

    def copies(ins, got, send_sems, recv_sems):
        x, y, c, chips = _place()
        return [pltpu.make_async_remote_copy(
            src_ref=_quarter(ins[k], col_sharded[k], 2 * chip[0] + chip[1]), dst_ref=got[k].at[j],
            send_sem=send_sems.at[3 * k + j], recv_sem=recv_sems.at[3 * k + j], device_id=(*chip, c), device_id_type=MESH)
            for k in range(n) for j, chip in enumerate(chips)]

    def start(*refs):
        for cp in copies(*refs):
            cp.start()

    def finish(*refs):
        for cp in copies(*refs):
            cp.wait()

    fresh = [jax.ShapeDtypeStruct(quarter_shape(h, col), h.dtype) for h, col in zip(halves, col_sharded)]
    return _Job(halves, 0, fresh, 3 * n, [(0.0, start), (1.0, finish)])


def _job_sibling(shards, col_sharded, layers):
    n = len(shards)

    def copies(outs, send_sems, recv_sems):
        x, y, c, _ = _place()
        sends, lands = [], []
        for k in range(n):
            sems = dict(send_sem=send_sems.at[k], recv_sem=recv_sems.at[k], device_id=(x, y, 1 - c), device_id_type=MESH)
            mine = _half(outs[k].at[layers[k]], col_sharded[k], c)
            theirs = _half(outs[k].at[layers[k]], col_sharded[k], 1 - c)
            sends.append(pltpu.make_async_remote_copy(src_ref=mine, dst_ref=mine, **sems))
            lands.append(pltpu.make_async_remote_copy(src_ref=theirs, dst_ref=theirs, **sems))
        return sends, lands

    def start(ins, outs, send_sems, recv_sems):
        for cp in copies(outs, send_sems, recv_sems)[0]:
            cp.start()

    def finish(ins, outs, send_sems, recv_sems):
        sends, lands = copies(outs, send_sems, recv_sems)
        for cp in lands:
            cp.wait_recv()
        for cp in sends:
            cp.wait_send()

    return _Job(shards, n, [], n, [(0.0, start), (1.0, finish)])


def _standalone(jobs, name):
    return _pc(jobs, lambda: None, name=name, in_specs=[], out_specs=[], out_shape=[])()[1]


def _all_reduce_small(buf, jobs=()):
    rows = buf.shape[0]
    per = rows // NDEV
    flips = [(fx, fy, fc) for fx in (0, 1) for fy in (0, 1) for fc in (0, 1)][1:]

    def body(in_ref, out_ref, got_ref, send_sems, recv_sems):
        x, y, c, _ = _place()
        me = 4 * x + 2 * y + c

        def peer(f):
            return tuple(1 - pos if flip else pos for pos, flip in zip((x, y, c), f))

        def block(ref, dev):
            return ref.at[pl.ds(pl.multiple_of(dev * per, 8), per), :]

        scatter = []
        for k, f in enumerate(flips):
            px, py, pc = peer(f)
            scatter.append(pltpu.make_async_remote_copy(
                src_ref=block(in_ref, 4 * px + 2 * py + pc), dst_ref=got_ref.at[k], send_sem=send_sems.at[k],
                recv_sem=recv_sems.at[k], device_id=(px, py, pc), device_id_type=MESH))
        for cp in scatter:
            cp.start()
        for cp in scatter:
            cp.wait()
        total = block(in_ref, me)[...]
        for k in range(len(flips)):
            total = total + got_ref[k]
        block(out_ref, me)[...] = total
        share = []
        for k, f in enumerate(flips):
            share.append(pltpu.make_async_remote_copy(
                src_ref=block(out_ref, me), dst_ref=block(out_ref, me), send_sem=send_sems.at[7 + k],
                recv_sem=recv_sems.at[7 + k], device_id=peer(f), device_id_type=MESH))
        for cp in share:
            cp.start()
        for k, f in enumerate(flips):
            share[k].wait_send()
            px, py, pc = peer(f)
            theirs = block(out_ref, 4 * px + 2 * py + pc)
            pltpu.make_async_remote_copy(
                src_ref=theirs, dst_ref=theirs, send_sem=send_sems.at[7 + k], recv_sem=recv_sems.at[7 + k],
                device_id=(px, py, pc), device_id_type=MESH).wait_recv()

    vmem = pl.BlockSpec(memory_space=pltpu.VMEM)
    return _pc(
        jobs, body, name="all_reduce_small", in_specs=[vmem], out_specs=vmem,
        out_shape=jax.ShapeDtypeStruct((rows, GW), F32),
        scratch_shapes=[pltpu.VMEM((NDEV - 1, per, GW), F32), pltpu.SemaphoreType.DMA((14,)),
                        pltpu.SemaphoreType.DMA((14,))],
    )(buf)


BIG = ("ffn1_w13", "ffn1_w2", "w_in", "w_up", "w_out", "ffn2_w13", "ffn2_w2")
BIG_COL_SHARDED = (True, False, True, True, False, True, False)
SMALL = ("ffn1_norm", "mix_norm", "pool_w", "pool_scale", "sconv_w", "cconv_w", "cconv_ln_g", "cconv_ln_b",
         "sgu_ln_g", "sgu_ln_b", "sgu_w", "sgu_b", "ffn2_norm", "final_norm")
WEIGHTS = ("ffn1_norm", "ffn1_w13", "ffn1_w2", "mix_norm", "w_in", "pool_w", "pool_scale", "sconv_w", "cconv_w",
           "cconv_ln_g", "cconv_ln_b", "sgu_ln_g", "sgu_ln_b", "sgu_w", "sgu_b", "w_up", "w_out", "ffn2_norm",
           "ffn2_w13", "ffn2_w2", "final_norm")


def _pad_rows(a, rows):
    return jnp.pad(a, ((0, 0), (0, rows - a.shape[1]), (0, 0)))


def kernel(x, ffn1_norm, ffn1_w13, ffn1_w2, mix_norm, w_in, pool_w, pool_scale, sconv_w, cconv_w, cconv_ln_g, cconv_ln_b, sgu_ln_g, sgu_ln_b, sgu_w, sgu_b, w_up, w_out, ffn2_norm, ffn2_w13, ffn2_w2, final_norm, loss_target, m_ffn1_norm, m_ffn1_w13, m_ffn1_w2, m_mix_norm, m_w_in, m_pool_w, m_pool_scale, m_sconv_w, m_cconv_w, m_cconv_ln_g, m_cconv_ln_b, m_sgu_ln_g, m_sgu_ln_b, m_sgu_w, m_sgu_b, m_w_up, m_w_out, m_ffn2_norm, m_ffn2_w13, m_ffn2_w2, m_final_norm, v_ffn1_norm, v_ffn1_w13, v_ffn1_w2, v_mix_norm, v_w_in, v_pool_w, v_pool_scale, v_sconv_w, v_cconv_w, v_cconv_ln_g, v_cconv_ln_b, v_sgu_ln_g, v_sgu_ln_b, v_sgu_w, v_sgu_b, v_w_up, v_w_out, v_ffn2_norm, v_ffn2_w13, v_ffn2_w2, v_final_norm):
    args = dict(locals())
    w = {nm: args[nm] for nm in WEIGHTS}
    m = {nm: args["m_" + nm] for nm in WEIGHTS}
    v = {nm: args["v_" + nm] for nm in WEIGHTS}
    depth = ffn1_w13.shape[0]
    chip = 2 * lax.axis_index("x") + lax.axis_index("y")

    sc = jnp.stack([lax.axis_index("c"), chip]).astype(jnp.int32)
    col_of = dict(zip(BIG, BIG_COL_SHARDED), sconv_w=True, cconv_w=True)
    sources = {nm: (w[nm], BF) for nm in BIG}
    sources["w_up"] = (w_up.reshape(depth, NBR * BW, w_up.shape[-1]), BF)
    sources["sconv_w"] = (_pad_rows(sconv_w, 2 * SKP), F32)
    sources["cconv_w"] = (_pad_rows(cconv_w, 2 * CKP), F32)
    first_group = ("ffn1_w13", "ffn1_w2")
    full = [dict() for _ in range(depth)]
    for nm in first_group:
        full[0][nm] = _cast_into(sources[nm][0], 0, col_of[nm], sources[nm][1], sc)

    def gather_job(l, names, handoff):
        return _job_gather([full[l][nm] for nm in names], [col_of[nm] for nm in names], handoff)

    def gather_with(l, names, handoff, call):
        res, job_outs = call([gather_job(l, names, handoff)] if l < depth else [])
        if l < depth:
            full[l].update(zip(names, job_outs[0]))
        return res

    rest = [(nm, l) for l in range(depth) for nm in sources if not (l == 0 and nm in first_group)]
    casted = gather_with(0, first_group, 1.0, lambda jobs: _cast_rest(
        [(sources[nm][0], l, col_of[nm], sources[nm][1]) for nm, l in rest], jobs))
    for (nm, l), arr in zip(rest, casted):
        full[l][nm] = arr

    xs = x[0]
    row = lambda a: a.reshape(1, -1)
    saved = []
    for l in range(depth):
        g1, gm, g2 = row(ffn1_norm[l]), row(mix_norm[l]), row(ffn2_norm[l])
        x1, h1, ab1 = gather_with(l, ("w_in",), 0.85, lambda jobs: _ffn_fwd(
            xs, g1, full[l]["ffn1_w13"], full[l]["ffn1_w2"], jobs))
        hm, p = gather_with(l, ("w_up", "w_out", "sconv_w", "cconv_w", "ffn2_w2"), 0.75, lambda jobs: _mix_in(
            x1, gm, full[l]["w_in"], jobs))
        branch = (pool_w[l], row(pool_scale[l]), full[l]["sconv_w"][:SKP], full[l]["cconv_w"][:CKP], row(cconv_ln_g[l]),
                  row(cconv_ln_b[l]), row(sgu_ln_g[l]), row(sgu_ln_b[l]), sgu_w[l], sgu_b[l].T)
        y, conv_z = gather_with(l, ("ffn2_w13",), 0.85, lambda jobs: _mix_branches_fwd(p, *branch, jobs=jobs))
        w_up_l = full[l]["w_up"].reshape(NBR, BW, D)
        x2, merged, up = gather_with(l + 1, ("ffn1_w2",), 0.7, lambda jobs: _mix_out_fwd(
            x1, y, p, w_up_l, full[l]["w_out"], jobs))
        x3, h2, ab2 = gather_with(l + 1, ("ffn1_w13",), 0.75, lambda jobs: _ffn_fwd(
            x2, g2, full[l]["ffn2_w13"], full[l]["ffn2_w2"], jobs))
        lw = dict(g1=g1, gm=gm, g2=g2, w13a=full[l]["ffn1_w13"], w2a=full[l]["ffn1_w2"], w13b=full[l]["ffn2_w13"],
                  w2b=full[l]["ffn2_w2"], w_in=full[l]["w_in"], w_up=w_up_l, w_out=full[l]["w_out"], branch=branch)
        saved.append(dict(lw=lw, x0=xs, x1=x1, x2=x2, h1=h1, ab1=ab1, hm=hm, p=p, y=y, z=conv_z, merged=merged, up=up, h2=h2,
                          ab2=ab2))
        xs = x3

    (dx, d_final, loss_part), _ = _loss_head(xs, row(final_norm), loss_target[0])
    loss = lax.psum(loss_part[0, 0], ("x", "y", "c"))

    ici_us = dict(ffn1_w13=64, ffn1_w2=32, w_in=93, w_up=23, w_out=12, ffn2_w13=64, ffn2_w2=32)
    parts, pair_sums, reduced, big_updates, pending = {}, {}, {}, {}, []

    def take_jobs(budget_us):
        chosen = []
        for task in list(pending):
            kind, (nm, _) = task
            if kind == "chip":
                if ici_us[nm] > budget_us:
                    continue
                budget_us -= ici_us[nm]
            if kind == "sib" and any(k == "sib" and key[0] == nm for k, key in chosen):
                continue
            chosen.append(task)
            pending.remove(task)
        groups, jobs = [], []
        for kind in ("pair", "sib", "chip"):
            keys = [key for k, key in chosen if k == kind]
            if not keys:
                continue
            cols = [col_of[nm] for nm, _ in keys]
            groups.append((kind, keys))
            if kind == "pair":
                jobs.append(_job_pair([parts[key] for key in keys], cols))
            elif kind == "chip":
                jobs.append(_job_chip([pair_sums[key] for key in keys], cols))
            else:
                jobs.append(_job_sibling([reduced[nm] for nm, _ in keys], cols, [layer for _, layer in keys]))
        return groups, jobs

    def settle(groups, job_outs):
        for (kind, keys), outs in zip(groups, job_outs):
            for key, out in zip(keys, outs):
                nm, layer = key
                if kind == "pair":
                    pair_sums[key] = _pair_sum(parts[key], out, col_of[nm], sc)
                    pending.append(("chip", key))
                elif kind == "chip":
                    assert not any(k == "sib" and other[0] == nm for k, other in pending)
                    reduced[nm] = _chip_sum(pair_sums[key], out, col_of[nm], layer, depth, sc, reduced.get(nm))
                    pending.append(("sib", key))
                else:
                    reduced[nm] = out
                    as3 = lambda a: a.reshape(out.shape)
                    big_updates[nm] = _adamw_layer(as3(w[nm]), out, as3(m[nm]), as3(v[nm]), layer, big_updates.get(nm))

    def run(budget_us, call, carrier=True):
        groups, jobs = take_jobs(budget_us) if carrier else ([], [])
        res, job_outs = call(jobs)
        settle(groups, job_outs)
        return res

    def wgrad_done(key, partial):
        parts[key] = partial
        pending.append(("pair", key))

    small_parts = {nm: [None] * depth for nm in SMALL if nm != "final_norm"}
    for l in reversed(range(depth)):
        sv = saved[l]
        lw = sv["lw"]
        dx, dab, s_act, dyh, dg2 = run(108, lambda jobs: _ffn_bwd(
            dx, sv["x2"], lw["g2"], sv["ab2"], lw["w13b"], lw["w2b"], jobs))
        wgrad_done(("ffn2_w13", l), run(58, lambda jobs: _wgrad(sv["h2"], dab, D, 512, "wgrad_w13", jobs)))
        wgrad_done(("ffn2_w2", l), run(33, lambda jobs: _wgrad(s_act, dyh, 256, D, "wgrad_w2", jobs)))
        small_parts["ffn2_norm"][l] = dg2

        dy, dp, dup, dxb = run(70, lambda jobs: _mix_out_bwd(dx, sv["up"], sv["p"], lw["w_up"], lw["w_out"], jobs))
        wgrad_done(("w_out", l), run(16, lambda jobs: _wgrad(sv["merged"], dxb, D, 512, "wgrad_w_out", jobs), False))
        wgrad_done(("w_up", l), run(25, lambda jobs: _wgrad_groups(sv["y"], dup, BW, D, "wgrad_w_up", jobs), False))
        (dp, d_pool_w, d_pool_scale, d_sconv, d_cconv, d_clg, d_clb, d_slg, d_slb, d_sgu_w, d_sgu_b) = run(
            130, lambda jobs: _mix_branches_bwd(sv["p"], dy, sv["z"], dp, *lw["branch"], jobs=jobs))
        wgrad_done(("w_in", l), run(80, lambda jobs: _wgrad(sv["hm"], dp, D, 512, "wgrad_w_in", jobs)))
        dx, dgm = run(91, lambda jobs: _mix_in_bwd(dp, lw["w_in"], sv["x1"], lw["gm"], dx, jobs))
        small_parts["mix_norm"][l] = dgm
        small_parts["pool_w"][l] = d_pool_w
        small_parts["pool_scale"][l] = d_pool_scale
        small_parts["sconv_w"][l] = d_sconv[:SK]
        small_parts["cconv_w"][l] = d_cconv[:CK]
        small_parts["cconv_ln_g"][l] = d_clg
        small_parts["cconv_ln_b"][l] = d_clb
        small_parts["sgu_ln_g"][l] = d_slg
        small_parts["sgu_ln_b"][l] = d_slb
        small_parts["sgu_w"][l] = d_sgu_w
        small_parts["sgu_b"][l] = jnp.sum(d_sgu_b, axis=-1)

        dx, dab, s_act, dyh, dg1 = run(108, lambda jobs: _ffn_bwd(
            dx, sv["x0"], lw["g1"], sv["ab1"], lw["w13a"], lw["w2a"], jobs))
        wgrad_done(("ffn1_w2", l), run(33, lambda jobs: _wgrad(s_act, dyh, 256, D, "wgrad_w2", jobs)))
        wgrad_done(("ffn1_w13", l), run(58, lambda jobs: _wgrad(sv["h1"], dab, D, 512, "wgrad_w13", jobs)))
        small_parts["ffn1_norm"][l] = dg1
    grad_x = dx[None]

    small_local = {nm: jnp.stack(parts).reshape(depth, *w[nm].shape[1:-1], -1) if nm not in ("sconv_w", "cconv_w")
                   else jnp.stack(parts) for nm, parts in small_parts.items()}
    small_local["final_norm"] = d_final.reshape(-1)
    sizes = [small_local[nm].size for nm in SMALL]
    total = sum(sizes)
    pad_to = NDEV * 8 * GW
    padded = -(-total // pad_to) * pad_to
    packed = jnp.concatenate([small_local[nm].reshape(-1) for nm in SMALL] + [jnp.zeros((padded - total,), F32)])
    groups, jobs = take_jobs(float("inf"))
    summed, job_outs = _all_reduce_small(packed.reshape(-1, GW), jobs)
    settle(groups, job_outs)
    summed = summed.reshape(-1)
    flushes = 0
    while pending:
        groups, jobs = take_jobs(float("inf"))
        settle(groups, _standalone(jobs, "grad_flush_%d" % flushes))
        flushes += 1
    big_grads = {nm: big_updates[nm][3].reshape(w[nm].shape) for nm in BIG}
    small_grads, off = {}, 0
    for nm, size in zip(SMALL, sizes):
        small_grads[nm] = summed[off:off + size].reshape(small_local[nm].shape)
        off += size
    for nm in ("sconv_w", "cconv_w"):
        small_grads[nm] = lax.dynamic_slice_in_dim(small_grads[nm], chip * GW, GW, axis=2)

    grads = {**big_grads, **small_grads}

    delta, new_m, new_v = {}, {}, {}
    for nm in BIG:
        delta[nm], new_m[nm], new_v[nm] = [a.reshape(w[nm].shape) for a in big_updates[nm][:3]]
    s_sizes = [w[nm].size for nm in SMALL]
    s_total = sum(s_sizes)
    s_padded = -(-s_total // (8 * GW)) * (8 * GW)

    def pack(tree):
        return jnp.concatenate([tree[nm].reshape(-1) for nm in SMALL] + [jnp.ones((s_padded - s_total,), F32)]).reshape(-1, GW)

    packed_out = _adamw(pack(w), pack(grads), pack(m), pack(v))
    off = 0
    for nm, size in zip(SMALL, s_sizes):
        for tree, arr in zip((delta, new_m, new_v), packed_out):
            tree[nm] = arr.reshape(-1)[off:off + size].reshape(w[nm].shape)
        off += size

    return (loss, grad_x, *[grads[nm] for nm in WEIGHTS], *[delta[nm] for nm in WEIGHTS],
            *[new_m[nm] for nm in WEIGHTS], *[new_v[nm] for nm in WEIGHTS])
```

```python
import jax
import jax.numpy as jnp
from jax import lax
from jax.experimental import pallas as pl
from jax.experimental.pallas import tpu as pltpu

D = 1024
FF = 2816
BW = 512
NBR = 4
MIXC = 4096
INC = 8192
GW = 128
CHUNK = 64
SK = 3
CK = 31
SKP = 8
CKP = 32
HALO = 32
TOKEN_TILE = 256
BRANCH_TILE = 512
EPS = 1e-6
NCHIP = 4
NDEV = 8

ADAM_LR = 0.001
ADAM_B1 = 0.9
ADAM_B2 = 0.999
ADAM_EPS = 1e-08
ADAM_WD = 0.01
ADAM_STEP = 10

VMEM_LIMIT = 56 * 1024 * 1024

BF = jnp.bfloat16
F32 = jnp.float32
MESH = pl.DeviceIdType.MESH
NT = (((1,), (1,)), ((), ()))
TN = (((0,), (0,)), ((), ()))


def _params(n_axes):
    return pltpu.CompilerParams(dimension_semantics=("arbitrary",) * n_axes, vmem_limit_bytes=VMEM_LIMIT)


class _Job:
    def __init__(self, args, n_inplace, fresh, nsem, phases):
        self.args, self.n_inplace, self.fresh, self.nsem, self.phases = list(args), n_inplace, list(fresh), nsem, phases


def _pc(jobs, body, name, in_specs, out_specs, out_shape, grid=(), scratch_shapes=(), input_output_aliases=None,
        compiler_params=None, grid_spec_scalars=None):
    single = not isinstance(out_shape, (list, tuple))
    core_out_specs = [out_specs] if single else list(out_specs)
    core_out_shape = [out_shape] if single else list(out_shape)
    n_in, n_out, n_scr = len(in_specs), len(core_out_specs), len(scratch_shapes)
    n_pre = 0 if grid_spec_scalars is None else 1
    all_in, all_out, all_shape = list(in_specs), list(core_out_specs), list(core_out_shape)
    all_scr, aliases, extra_args, layout = list(scratch_shapes), dict(input_output_aliases or {}), [], []
    for job in jobs:
        n_job_out = job.n_inplace + len(job.fresh)
        layout.append((len(all_in), len(job.args), len(all_out), n_job_out, len(all_scr)))
        for a in range(job.n_inplace):
            aliases[n_pre + len(all_in) + a] = len(all_out) + a
        all_in += [ANY] * len(job.args)
        extra_args += job.args
        all_out += [ANY] * n_job_out
        all_shape += [jax.ShapeDtypeStruct(a.shape, a.dtype) for a in job.args[:job.n_inplace]] + job.fresh
        all_scr += [pltpu.SemaphoreType.DMA((job.nsem,)), pltpu.SemaphoreType.DMA((job.nsem,))]
    steps = 1
    for extent in grid:
        steps *= extent
    events = []
    for (i0, na, o0, no, s0), job in zip(layout, jobs):
        for frac, fn in job.phases:
            events.append((min(int(frac * steps), steps - 1), frac >= 1.0, len(events), fn, (i0, na, o0, no, s0)))
    events.sort(key=lambda e: e[:3])

    def wrapped(*refs):
        pre, refs = refs[:n_pre], refs[n_pre:]
        ins, outs, scr = refs[:len(all_in)], refs[len(all_in):len(all_in) + len(all_out)], refs[len(all_in) + len(all_out):]
        step = 0
        for axis, extent in enumerate(grid):
            step = step * extent + pl.program_id(axis)

        def emit(event):
            at, _, _, fn, (i0, na, o0, no, s0) = event
            run = lambda: fn(ins[i0:i0 + na], outs[o0:o0 + no], scr[s0], scr[s0 + 1])
            if steps == 1:
                run()
            else:
                pl.when(step == at)(run)

        for event in events:
            if not event[1]:
                emit(event)
        body(*pre, *ins[:n_in], *outs[:n_out], *scr[:n_scr])
        for event in events:
            if event[1]:
                emit(event)

    kwargs = dict(name=name, out_shape=all_shape, input_output_aliases=aliases)
    if compiler_params is not None:
        kwargs["compiler_params"] = compiler_params
    if grid_spec_scalars is not None:
        kwargs["grid_spec"] = pltpu.PrefetchScalarGridSpec(
            num_scalar_prefetch=1, grid=grid, in_specs=all_in, out_specs=all_out, scratch_shapes=all_scr)
    else:
        kwargs.update(in_specs=all_in, out_specs=all_out, scratch_shapes=all_scr)
        if grid:
            kwargs["grid"] = grid
    call = pl.pallas_call(wrapped, **kwargs)

    def run_call(*args):
        pre_args = [] if grid_spec_scalars is None else [grid_spec_scalars]
        res = list(call(*pre_args, *args, *extra_args))
        core = res[0] if single else res[:n_out]
        job_outs = [res[o0:o0 + no] for (_, _, o0, no, _) in layout]
        return core, job_outs

    return run_call


def _full(shape):
    nd = len(shape)
    return pl.BlockSpec(shape, lambda *_: (0,) * nd, pipeline_mode=pl.Buffered(1))


def _rows(ts, width, col=0):
    return pl.BlockSpec((ts, width), lambda i: (i, col))


def _sig(v):
    return jax.nn.sigmoid(v)


def _rms_stats(x):
    r = lax.rsqrt(jnp.mean(x * x, axis=-1, keepdims=True) + EPS)
    return r, x * r


def _rms_bwd(x, g, dh):
    r, xh = _rms_stats(x)
    dg = jnp.sum(dh * xh, axis=0, keepdims=True)
    dxh = dh * g
    dx = r * (dxh - xh * jnp.mean(dxh * xh, axis=-1, keepdims=True))
    return dx, dg


def _accumulate(ref, val, first):
    @pl.when(first)
    def _():
        ref[...] = val

    @pl.when(jnp.logical_not(first))
    def _():
        ref[...] += val


def _ffn_fwd(x, g, w13, w2, jobs=()):
    s_len = x.shape[0]
    ts = min(TOKEN_TILE, s_len)

    def body(x_ref, g_ref, w13_ref, w2_ref, xo_ref, h_ref, s_ref, dsab_ref):
        xv = x_ref[...]
        r, xh = _rms_stats(xv)
        h = (xh * g_ref[...]).astype(BF)
        h_ref[...] = h
        ab = jnp.dot(h, w13_ref[...], preferred_element_type=F32)
        a = ab[:, :FF]
        b = ab[:, FF:]
        sg = _sig(a)
        sil = a * sg
        dsab_ref[:, :FF] = (b * (sg * (1.0 + a * (1.0 - sg)))).astype(BF)
        dsab_ref[:, FF:] = sil.astype(BF)
        s = (sil * b).astype(BF)
        s_ref[...] = s
        xo_ref[...] = xv + 0.5 * jnp.dot(s, w2_ref[...], preferred_element_type=F32)

    return _pc(
        jobs, body, name="ffn_fwd", grid=(s_len // ts,),
        in_specs=[_rows(ts, D), _full((1, D)), _full((D, 2 * FF)), _full((FF, D))],
        out_specs=[_rows(ts, D), _rows(ts, D), _rows(ts, FF), _rows(ts, 2 * FF)],
        out_shape=[jax.ShapeDtypeStruct((s_len, D), F32), jax.ShapeDtypeStruct((s_len, D), BF),
                   jax.ShapeDtypeStruct((s_len, FF), BF), jax.ShapeDtypeStruct((s_len, 2 * FF), BF)],
        compiler_params=_params(1),
    )(x, g, w13, w2)


def _ffn_bwd(dxo, x, g, dsab, w13, w2, jobs=()):
    s_len = x.shape[0]
    ts = min(TOKEN_TILE, s_len)

    def body(dxo_ref, x_ref, g_ref, dsab_ref, w13_ref, w2_ref, dxi_ref, dab_ref, dy_ref, dg_ref):
        i = pl.program_id(0)
        dxo_v = dxo_ref[...]
        dy = (0.5 * dxo_v).astype(BF)
        dy_ref[...] = dy
        ds = lax.dot_general(dy, w2_ref[...], NT, preferred_element_type=F32)
        dab_ref[:, :FF] = (ds * dsab_ref[:, :FF].astype(F32)).astype(BF)
        dab_ref[:, FF:] = (ds * dsab_ref[:, FF:].astype(F32)).astype(BF)
        dh = lax.dot_general(dab_ref[...], w13_ref[...], NT, preferred_element_type=F32)
        dx, dg = _rms_bwd(x_ref[...], g_ref[...], dh)
        dxi_ref[...] = dxo_v + dx
        _accumulate(dg_ref, dg, i == 0)

    return _pc(
        jobs, body, name="ffn_bwd", grid=(s_len // ts,),
        in_specs=[_rows(ts, D), _rows(ts, D), _full((1, D)), _rows(ts, 2 * FF), _full((D, 2 * FF)), _full((FF, D))],
        out_specs=[_rows(ts, D), _rows(ts, 2 * FF), _rows(ts, D), pl.BlockSpec((1, D), lambda i: (0, 0))],
        out_shape=[jax.ShapeDtypeStruct((s_len, D), F32), jax.ShapeDtypeStruct((s_len, 2 * FF), BF),
                   jax.ShapeDtypeStruct((s_len, D), BF), jax.ShapeDtypeStruct((1, D), F32)],
        compiler_params=_params(1),
    )(dxo, x, g, dsab, w13, w2)


def _wgrad(a, b, tk, tn, name, jobs=()):
    s_len, k = a.shape
    n = b.shape[1]

    def body(a_ref, b_ref, o_ref):
        o_ref[...] = lax.dot_general(a_ref[...], b_ref[...], TN, preferred_element_type=F32).astype(BF)

    return _pc(
        jobs, body, name=name, grid=(k // tk, n // tn),
        in_specs=[pl.BlockSpec((s_len, tk), lambda i, j: (0, i)), pl.BlockSpec((s_len, tn), lambda i, j: (0, j))],
        out_specs=pl.BlockSpec((tk, tn), lambda i, j: (i, j)),
        out_shape=jax.ShapeDtypeStruct((k, n), BF),
        compiler_params=_params(2),
    )(a, b)


def _wgrad_groups(a, b, ka, nb, name, jobs=()):
    s_len = a.shape[0]
    groups = a.shape[1] // ka

    def body(a_ref, b_ref, o_ref):
        o_ref[...] = lax.dot_general(a_ref[...], b_ref[...], TN, preferred_element_type=F32).astype(BF)

    return _pc(
        jobs, body, name=name, grid=(groups,),
        in_specs=[pl.BlockSpec((s_len, ka), lambda gi: (0, gi)), pl.BlockSpec((s_len, nb), lambda gi: (0, gi))],
        out_specs=pl.BlockSpec((ka, nb), lambda gi: (gi, 0)),
        out_shape=jax.ShapeDtypeStruct((groups * ka, nb), BF),
        compiler_params=_params(1),
    )(a, b)


def _mix_in(x, g, w_in, jobs=()):
    s_len = x.shape[0]
    ts = min(TOKEN_TILE, s_len)

    def body(x_ref, g_ref, w_ref, h_ref, p_ref):
        _, xh = _rms_stats(x_ref[...])
        h = (xh * g_ref[...]).astype(BF)
        h_ref[...] = h
        p_ref[...] = jnp.dot(h, w_ref[...], preferred_element_type=F32).astype(BF)

    return _pc(
        jobs, body, name="mix_in", grid=(s_len // ts,),
        in_specs=[_rows(ts, D), _full((1, D)), _full((D, INC))],
        out_specs=[_rows(ts, D), _rows(ts, INC)],
        out_shape=[jax.ShapeDtypeStruct((s_len, D), BF), jax.ShapeDtypeStruct((s_len, INC), BF)],
        compiler_params=_params(1),
    )(x, g, w_in)


def _shift(e, j):
    n = e.shape[0]
    j = j % n
    return e if j == 0 else pltpu.roll(e, j, 0)


def _ln_stats(z):
    mu = jnp.mean(z, axis=-1, keepdims=True)
    zc = z - mu
    rs = lax.rsqrt(jnp.mean(zc * zc, axis=-1, keepdims=True) + EPS)
    return rs, zc * rs


def _ln_bwd(rs, zn, dzn):
    return rs * (dzn - jnp.mean(dzn, axis=-1, keepdims=True) - zn * jnp.mean(dzn * zn, axis=-1, keepdims=True))


_GELU_C0 = 0.7978845608028654
_GELU_C1 = 0.044715


def _gelu(p):
    th = jnp.tanh(_GELU_C0 * (p + _GELU_C1 * p * p * p))
    return 0.5 * p * (1.0 + th), th


def _gelu_grad(p, th):
    return 0.5 * (1.0 + th) + 0.5 * p * (1.0 - th * th) * (_GELU_C0 * (1.0 + 3.0 * _GELU_C1 * p * p))


def _pool_diff(a, t, sign):
    outs = []
    for gi in range(NBR):
        win = 2 ** (gi + 1)
        ag = a[:, gi * GW:(gi + 1) * GW]
        cnt = jnp.clip(t + 1, 1, win).astype(F32)
        ws = ag if sign > 0 else ag / cnt
        for s in range(gi + 1):
            ws = ws + _shift(ws, sign * (2 ** s))
        outs.append((ws / cnt if sign > 0 else ws) - ag)
    return outs


def _sgu_mask():
    row = lax.broadcasted_iota(jnp.int32, (GW, GW), 0)
    col = lax.broadcasted_iota(jnp.int32, (GW, GW), 1)
    return (col // CHUNK) <= (row // CHUNK)


def _assemble(pe_ref, prev_ref, cur_ref, next_ref, i, last, ts):
    pe_ref[0:HALO, :] = jnp.where(i > 0, prev_ref[...], jnp.zeros_like(prev_ref))
    pe_ref[HALO:HALO + ts, :] = cur_ref[...]
    if next_ref is not None:
        pe_ref[HALO + ts:, :] = jnp.where(i < last, next_ref[...], jnp.zeros_like(next_ref))


def _halo_specs(ts, width, s_len, with_next):
    per = ts // HALO
    specs = [pl.BlockSpec((HALO, width), lambda i: (jnp.maximum(i * per - 1, 0), 0)),
             pl.BlockSpec((ts, width), lambda i: (i, 0))]
    if with_next:
        specs.append(pl.BlockSpec((HALO, width), lambda i: (jnp.minimum((i + 1) * per, s_len // HALO - 1), 0)))
    return specs


def _branch_weights_specs():
    return [_full((NBR, GW, GW)), _full((1, BW)), _full((SKP, BW)), _full((CKP, BW)), _full((1, BW)), _full((1, BW)),
            _full((1, BW)), _full((1, BW)), _full((NBR, GW, GW)), _full((GW, NBR))]


def _mix_branches_fwd(p, pool_w, pool_scale, sconv_w, cconv_w, cln_g, cln_b, sln_g, sln_b, sgu_w, sgu_bt, jobs=()):
    s_len = p.shape[0]
    ts = min(BRANCH_TILE, s_len)
    ext = HALO + ts

    def body(pp_ref, pc_ref, pw_ref, ps_ref, sw_ref, cw_ref, clg_ref, clb_ref, slg_ref, slb_ref, gw_ref, gb_ref,
             y_ref, z_ref, pe_ref):
        i = pl.program_id(0)
        _assemble(pe_ref, pp_ref, pc_ref, None, i, 0, ts)
        t = i * ts - HALO + lax.broadcasted_iota(jnp.int32, (ext, 1), 0)

        dgs = _pool_diff(pe_ref[:, 0:BW].astype(F32), t, 1)
        for gi in range(NBR):
            e = jnp.dot(dgs[gi][HALO:].astype(BF), pw_ref[gi].astype(BF), preferred_element_type=F32)
            y_ref[:, gi * GW:(gi + 1) * GW] = (e * ps_ref[:, gi * GW:(gi + 1) * GW]).astype(BF)

        xin = pe_ref[:, BW:2 * BW].astype(F32)
        cg = pe_ref[:, 3 * BW:4 * BW].astype(F32)
        q = cg * xin
        cv = sw_ref[2:3, :] * q + sw_ref[1:2, :] * _shift(q, 1) + sw_ref[0:1, :] * _shift(q, 2)
        y_ref[:, BW:2 * BW] = (pe_ref[HALO:, 2 * BW:3 * BW].astype(F32) * cv[HALO:]).astype(BF)

        yg = pe_ref[:, 4 * BW:5 * BW].astype(F32) * _sig(pe_ref[:, 5 * BW:6 * BW].astype(F32))
        z = cw_ref[CK - 1:CK, :] * yg
        for j in range(1, CK):
            z = z + cw_ref[CK - 1 - j:CK - j, :] * _shift(yg, j)
        z_ref[...] = z[HALO:].astype(BF)
        _, zn = _ln_stats(z[HALO:])
        nn = zn * clg_ref[...] + clb_ref[...]
        y_ref[:, 2 * BW:3 * BW] = (nn * _sig(nn)).astype(BF)

        u, _ = _gelu(pc_ref[:, 6 * BW:7 * BW].astype(F32))
        v, _ = _gelu(pc_ref[:, 7 * BW:8 * BW].astype(F32))
        _, vn = _ln_stats(v)
        vn = (vn * slg_ref[...] + slb_ref[...]).astype(BF)
        mask = _sgu_mask()
        for hd in range(NBR):
            wm = jnp.where(mask, gw_ref[hd], 0.0).astype(BF)
            for blk in range(ts // GW):
                rows = slice(blk * GW, (blk + 1) * GW)
                cols = slice(hd * GW, (hd + 1) * GW)
                zz = jnp.dot(wm, vn[rows, cols], preferred_element_type=F32) + gb_ref[:, hd:hd + 1]
                y_ref[rows, 3 * BW + hd * GW:3 * BW + (hd + 1) * GW] = (u[rows, cols] * zz).astype(BF)

    return _pc(
        jobs, body, name="mix_branches_fwd", grid=(s_len // ts,),
        in_specs=_halo_specs(ts, MIXC, s_len, False) + _branch_weights_specs(),
        out_specs=[_rows(ts, NBR * BW), _rows(ts, BW)],
        out_shape=[jax.ShapeDtypeStruct((s_len, NBR * BW), BF), jax.ShapeDtypeStruct((s_len, BW), BF)],
        scratch_shapes=[pltpu.VMEM((ext, MIXC), BF)],
        compiler_params=_params(1),
    )(p, p, pool_w, pool_scale, sconv_w, cconv_w, cln_g, cln_b, sln_g, sln_b, sgu_w, sgu_bt)


def _mix_branches_bwd(p, dy, z, dp, pool_w, pool_scale, sconv_w, cconv_w, cln_g, cln_b, sln_g, sln_b, sgu_w, sgu_bt,
                      jobs=()):
    s_len = p.shape[0]
    ts = min(BRANCH_TILE, s_len)
    ext = ts + 2 * HALO
    last = s_len // ts - 1
    tile = slice(HALO, HALO + ts)
    small_shapes = [(NBR, GW, GW), (1, BW), (SKP, BW), (CKP, BW), (1, BW), (1, BW), (1, BW), (1, BW), (NBR, GW, GW),
                    (NBR, GW, GW)]

    def body(pp_ref, pc_ref, pn_ref, dyp_ref, dyc_ref, dyn_ref, zc_ref, zn_ref, dpin_ref,
             pw_ref, ps_ref, sw_ref, cw_ref, clg_ref, clb_ref, slg_ref, slb_ref, gw_ref, gb_ref,
             dp_ref, dpw_ref, dps_ref, dsw_ref, dcw_ref, dclg_ref, dclb_ref, dslg_ref, dslb_ref, dgw_ref, dgb_ref,
             pe_ref, de_ref):
        del dyp_ref, dpin_ref
        i = pl.program_id(0)
        first = i == 0
        _assemble(pe_ref, pp_ref, pc_ref, pn_ref, i, last, ts)
        de_ref[0:HALO, :] = jnp.zeros((HALO, NBR * BW), BF)
        de_ref[HALO:HALO + ts, :] = dyc_ref[...]
        de_ref[HALO + ts:, :] = jnp.where(i < last, dyn_ref[...], jnp.zeros_like(dyn_ref))
        t = i * ts - HALO + lax.broadcasted_iota(jnp.int32, (ext, 1), 0)

        @pl.when(first)
        def _():
            dsw_ref[...] = jnp.zeros((SKP, BW), F32)
            dcw_ref[...] = jnp.zeros((CKP, BW), F32)

        dgs = _pool_diff(pe_ref[:, 0:BW].astype(F32), t, 1)
        dya = de_ref[:, 0:BW].astype(F32)
        dds = []
        for gi in range(NBR):
            cols = slice(gi * GW, (gi + 1) * GW)
            pw = pw_ref[gi].astype(BF)
            d_t = dgs[gi][tile].astype(BF)
            e = jnp.dot(d_t, pw, preferred_element_type=F32)
            _accumulate(dps_ref.at[:, cols], jnp.sum(dya[tile, cols] * e, axis=0, keepdims=True), first)
            de_g = (dya[:, cols] * ps_ref[:, cols]).astype(BF)
            _accumulate(dpw_ref.at[gi], lax.dot_general(d_t, de_g[tile], TN, preferred_element_type=F32), first)
            dds.append(lax.dot_general(de_g, pw, NT, preferred_element_type=F32))
        das = _pool_diff(jnp.concatenate(dds, axis=1), t, -1)
        for gi in range(NBR):
            dp_ref[:, gi * GW:(gi + 1) * GW] = das[gi][tile].astype(BF)

        xin = pe_ref[:, BW:2 * BW].astype(F32)
        bg = pe_ref[:, 2 * BW:3 * BW].astype(F32)
        cg = pe_ref[:, 3 * BW:4 * BW].astype(F32)
        q = cg * xin
        qs = [q, _shift(q, 1), _shift(q, 2)]
        cv = sw_ref[2:3, :] * qs[0] + sw_ref[1:2, :] * qs[1] + sw_ref[0:1, :] * qs[2]
        dyb = de_ref[:, BW:2 * BW].astype(F32)
        dcv = dyb * bg
        for j in range(SK):
            dsw_ref[SK - 1 - j:SK - j, :] += jnp.sum(dcv[tile] * qs[j][tile], axis=0, keepdims=True)
        dq = sw_ref[2:3, :] * dcv + sw_ref[1:2, :] * _shift(dcv, -1) + sw_ref[0:1, :] * _shift(dcv, -2)
        dp_ref[:, BW:2 * BW] = (dq * cg)[tile].astype(BF)
        dp_ref[:, 2 * BW:3 * BW] = (dyb * cv)[tile].astype(BF)
        dp_ref[:, 3 * BW:4 * BW] = (dq * xin)[tile].astype(BF)

        ca = pc_ref[:, 4 * BW:5 * BW].astype(F32)
        sb = _sig(pc_ref[:, 5 * BW:6 * BW].astype(F32))
        yg_t = ca * sb
        z = jnp.concatenate([zc_ref[...].astype(F32), zn_ref[...].astype(F32)], axis=0)
        rs, zn = _ln_stats(z)
        nn = zn * clg_ref[...] + clb_ref[...]
        sn = _sig(nn)
        dn = de_ref[HALO:, 2 * BW:3 * BW].astype(F32) * (sn * (1.0 + nn * (1.0 - sn)))
        _accumulate(dclg_ref, jnp.sum((dn * zn)[:ts], axis=0, keepdims=True), first)
        _accumulate(dclb_ref, jnp.sum(dn[:ts], axis=0, keepdims=True), first)
        dz = _ln_bwd(rs, zn, dn * clg_ref[...])
        dyg = cw_ref[CK - 1:CK, :] * dz[:ts]
        dcw_ref[CK - 1:CK, :] += jnp.sum(dz[:ts] * yg_t, axis=0, keepdims=True)
        for j in range(1, CK):
            dz_ahead = _shift(dz, -j)[:ts]
            dyg = dyg + cw_ref[CK - 1 - j:CK - j, :] * dz_ahead
            dcw_ref[CK - 1 - j:CK - j, :] += jnp.sum(dz_ahead * yg_t, axis=0, keepdims=True)
        dp_ref[:, 4 * BW:5 * BW] = (dyg * sb).astype(BF)
        dp_ref[:, 5 * BW:6 * BW] = (dyg * ca * sb * (1.0 - sb)).astype(BF)

        pu = pc_ref[:, 6 * BW:7 * BW].astype(F32)
        pv = pc_ref[:, 7 * BW:8 * BW].astype(F32)
        u, thu = _gelu(pu)
        v, thv = _gelu(pv)
        vrs, vn0 = _ln_stats(v)
        vn = (vn0 * slg_ref[...] + slb_ref[...]).astype(BF)
        dyd = dyc_ref[:, 3 * BW:4 * BW].astype(F32)
        dzz = dyd * u
        dzb = dzz.astype(BF)
        mask = _sgu_mask()
        dvn_cols = []
        for hd in range(NBR):
            cols = slice(hd * GW, (hd + 1) * GW)
            wm = jnp.where(mask, gw_ref[hd], 0.0).astype(BF)
            dwm = jnp.zeros((GW, GW), F32)
            dbs = jnp.zeros((GW, GW), F32)
            dvn_rows = []
            for blk in range(ts // GW):
                rows = slice(blk * GW, (blk + 1) * GW)
                zz = jnp.dot(wm, vn[rows, cols], preferred_element_type=F32) + gb_ref[:, hd:hd + 1]
                dp_ref[rows, 6 * BW + hd * GW:6 * BW + (hd + 1) * GW] = (
                    dyd[rows, cols] * zz * _gelu_grad(pu[rows, cols], thu[rows, cols])).astype(BF)
                dwm = dwm + lax.dot_general(dzb[rows, cols], vn[rows, cols], NT, preferred_element_type=F32)
                dbs = dbs + dzz[rows, cols]
                dvn_rows.append(lax.dot_general(wm, dzb[rows, cols], TN, preferred_element_type=F32))
            _accumulate(dgw_ref.at[hd], jnp.where(mask, dwm, 0.0), first)
            _accumulate(dgb_ref.at[hd], dbs, first)
            dvn_cols.append(jnp.concatenate(dvn_rows, axis=0))
        dvn = jnp.concatenate(dvn_cols, axis=1)
        _accumulate(dslg_ref, jnp.sum(dvn * vn0, axis=0, keepdims=True), first)
        _accumulate(dslb_ref, jnp.sum(dvn, axis=0, keepdims=True), first)
        dv = _ln_bwd(vrs, vn0, dvn * slg_ref[...])
        dp_ref[:, 7 * BW:8 * BW] = (dv * _gelu_grad(pv, thv)).astype(BF)

    const = lambda shp: pl.BlockSpec(shp, lambda i: (0,) * len(shp))
    return _pc(
        jobs, body, name="mix_branches_bwd", grid=(s_len // ts,),
        in_specs=(_halo_specs(ts, MIXC, s_len, True) + _halo_specs(ts, NBR * BW, s_len, True)
                  + _halo_specs(ts, BW, s_len, True)[1:] + [pl.BlockSpec(memory_space=pl.ANY)] + _branch_weights_specs()),
        out_specs=[pl.BlockSpec((ts, MIXC), lambda i: (i, 0))] + [const(s) for s in small_shapes],
        out_shape=[jax.ShapeDtypeStruct((s_len, INC), BF)] + [jax.ShapeDtypeStruct(s, F32) for s in small_shapes],
        scratch_shapes=[pltpu.VMEM((ext, MIXC), BF), pltpu.VMEM((ext, NBR * BW), BF)],
        input_output_aliases={8: 0},
        compiler_params=_params(1),
    )(p, p, p, dy, dy, dy, z, z, dp, pool_w, pool_scale, sconv_w, cconv_w, cln_g, cln_b, sln_g, sln_b, sgu_w, sgu_bt)


def _mix_out_fwd(x, y, p, w_up, w_out, jobs=()):
    s_len = x.shape[0]
    ts = min(TOKEN_TILE, s_len)

    def body(x_ref, y_ref, pg_ref, wu_ref, wo_ref, xo_ref, m_ref, up_ref):
        m = jnp.zeros((ts, D), F32)
        for gi in range(NBR):
            up = jnp.dot(y_ref[:, gi * BW:(gi + 1) * BW], wu_ref[gi], preferred_element_type=F32)
            up_ref[:, gi * D:(gi + 1) * D] = up.astype(BF)
            m = m + _sig(pg_ref[:, gi * D:(gi + 1) * D].astype(F32)) * up
        mb = m.astype(BF)
        m_ref[...] = mb
        xo_ref[...] = x_ref[...] + jnp.dot(mb, wo_ref[...], preferred_element_type=F32)

    return _pc(
        jobs, body, name="mix_out_fwd", grid=(s_len // ts,),
        in_specs=[_rows(ts, D), _rows(ts, NBR * BW), _rows(ts, NBR * D, 1), _full((NBR, BW, D)), _full((D, D))],
        out_specs=[_rows(ts, D), _rows(ts, D), _rows(ts, NBR * D)],
        out_shape=[jax.ShapeDtypeStruct((s_len, D), F32), jax.ShapeDtypeStruct((s_len, D), BF),
                   jax.ShapeDtypeStruct((s_len, NBR * D), BF)],
        compiler_params=_params(1),
    )(x, y, p, w_up, w_out)


def _mix_out_bwd(dxo, up, p, w_up, w_out, jobs=()):
    s_len = dxo.shape[0]
    ts = min(TOKEN_TILE, s_len)

    def body(dxo_ref, up_ref, pg_ref, wu_ref, wo_ref, dy_ref, dp_ref, dup_ref, dxb_ref):
        dxb = dxo_ref[...].astype(BF)
        dxb_ref[...] = dxb
        dm = lax.dot_general(dxb, wo_ref[...], NT, preferred_element_type=F32)
        for gi in range(NBR):
            cols = slice(gi * D, (gi + 1) * D)
            gate = _sig(pg_ref[:, cols].astype(F32))
            dp_ref[:, cols] = (dm * up_ref[:, cols].astype(F32) * gate * (1.0 - gate)).astype(BF)
            dup = (dm * gate).astype(BF)
            dup_ref[:, cols] = dup
            dy_ref[:, gi * BW:(gi + 1) * BW] = lax.dot_general(
                dup, wu_ref[gi], NT, preferred_element_type=F32).astype(BF)

    return _pc(
        jobs, body, name="mix_out_bwd", grid=(s_len // ts,),
        in_specs=[_rows(ts, D), _rows(ts, NBR * D), _rows(ts, NBR * D, 1), _full((NBR, BW, D)), _full((D, D))],
        out_specs=[_rows(ts, NBR * BW), _rows(ts, NBR * D, 1), _rows(ts, NBR * D), _rows(ts, D)],
        out_shape=[jax.ShapeDtypeStruct((s_len, NBR * BW), BF), jax.ShapeDtypeStruct((s_len, INC), BF),
                   jax.ShapeDtypeStruct((s_len, NBR * D), BF), jax.ShapeDtypeStruct((s_len, D), BF)],
        compiler_params=_params(1),
    )(dxo, up, p, w_up, w_out)


def _mix_in_bwd(dp, w_in, x, g, dxo, jobs=()):
    s_len = x.shape[0]
    ts = min(TOKEN_TILE, s_len)

    def body(dp_ref, w_ref, x_ref, g_ref, dxo_ref, dxi_ref, dg_ref):
        i = pl.program_id(0)
        dh = lax.dot_general(dp_ref[...], w_ref[...], NT, preferred_element_type=F32)
        dx, dg = _rms_bwd(x_ref[...], g_ref[...], dh)
        dxi_ref[...] = dxo_ref[...] + dx
        _accumulate(dg_ref, dg, i == 0)

    return _pc(
        jobs, body, name="mix_in_bwd", grid=(s_len // ts,),
        in_specs=[_rows(ts, INC), _full((D, INC)), _rows(ts, D), _full((1, D)), _rows(ts, D)],
        out_specs=[_rows(ts, D), pl.BlockSpec((1, D), lambda i: (0, 0))],
        out_shape=[jax.ShapeDtypeStruct((s_len, D), F32), jax.ShapeDtypeStruct((1, D), F32)],
        compiler_params=_params(1),
    )(dp, w_in, x, g, dxo)


def _loss_head(x, g, target, jobs=()):
    s_len = x.shape[0]
    ts = min(512, s_len)

    def body(x_ref, g_ref, t_ref, dx_ref, dg_ref, loss_ref):
        i = pl.program_id(0)
        xv = x_ref[...]
        _, xh = _rms_stats(xv)
        err = xh * g_ref[...] - t_ref[...]
        part = 0.5 * jnp.sum(jnp.mean(err * err, axis=-1, keepdims=True), axis=0, keepdims=True)
        dx, dg = _rms_bwd(xv, g_ref[...], err * (1.0 / D))
        dx_ref[...] = dx
        _accumulate(dg_ref, dg, i == 0)
        _accumulate(loss_ref, jnp.broadcast_to(part, (1, GW)), i == 0)

    return _pc(
        jobs, body, name="loss_head", grid=(s_len // ts,),
        in_specs=[_rows(ts, D), _full((1, D)), _rows(ts, D)],
        out_specs=[_rows(ts, D), pl.BlockSpec((1, D), lambda i: (0, 0)), pl.BlockSpec((1, GW), lambda i: (0, 0))],
        out_shape=[jax.ShapeDtypeStruct((s_len, D), F32), jax.ShapeDtypeStruct((1, D), F32),
                   jax.ShapeDtypeStruct((1, GW), F32)],
        compiler_params=_params(1),
    )(x, g, target)


SUM_TILE = 1 << 20
ADAM_TILE = 1 << 19
CAST_TILE = 1 << 19
ADAM_STREAM_TILE = 1 << 17


def _row_tile(rows, cols, budget=1 << 18):
    tr = rows
    while tr * cols > budget and tr % 16 == 0:
        tr //= 2
    return tr


def _elementwise(fn, name, ins, out_dtypes):
    rows, cols = ins[0].shape
    tr = _row_tile(rows, cols)
    n_in = len(ins)

    def body(*refs):
        res = fn(*[r[...] for r in refs[:n_in]])
        for o_ref, val in zip(refs[n_in:], res):
            o_ref[...] = val.astype(o_ref.dtype)

    outs = pl.pallas_call(
        body, name=name, grid=(rows // tr,),
        in_specs=[_rows(tr, cols)] * n_in, out_specs=[_rows(tr, cols)] * len(out_dtypes),
        out_shape=[jax.ShapeDtypeStruct((rows, cols), dt) for dt in out_dtypes],
        compiler_params=_params(1),
    )(*ins)
    return outs


def _tiled(fn, name, grid, in_specs, out_specs, out_shape, args, scalars=None, alias=None):
    alias = alias or {}
    n_in = len(in_specs) - len(alias)
    n_pre = 0 if scalars is None else 1

    def body(*refs):
        refs = refs[n_pre:]
        res = fn(*[r[...] for r in refs[:n_in]])
        for o_ref, val in zip(refs[len(in_specs):], res):
            o_ref[...] = val.astype(o_ref.dtype)

    aliases = {n_pre + pos: out for pos, out in alias.items()}
    if scalars is None:
        return pl.pallas_call(body, name=name, grid=grid, in_specs=in_specs, out_specs=out_specs, out_shape=out_shape,
                              input_output_aliases=aliases, compiler_params=_params(len(grid)))(*args)
    spec = pltpu.PrefetchScalarGridSpec(num_scalar_prefetch=1, grid=grid, in_specs=in_specs, out_specs=out_specs)
    return pl.pallas_call(body, name=name, grid_spec=spec, out_shape=out_shape, input_output_aliases=aliases,
                          compiler_params=_params(len(grid)))(scalars, *args)


def _cast_into(shard, layer, col, dtype, sc):
    ks, ns = shard.shape[1:]
    tr = _row_tile(ks, ns, SUM_TILE)
    full = (ks, ns * NCHIP) if col else (ks * NCHIP, ns)
    out_idx = (lambda i, s: (i, s[1])) if col else (lambda i, s: (s[1] * (ks // tr) + i, 0))
    return _tiled(lambda v: (v,), "cast_into", (ks // tr,), [pl.BlockSpec((None, tr, ns), lambda i, s: (layer, i, 0))],
                  [pl.BlockSpec((tr, ns), out_idx)], [jax.ShapeDtypeStruct(full, dtype)], [shard], sc)[0]


def _cast_rest(items, jobs=()):
    n = len(items)
    tiles, slot_sets, counts = [], {}, {}
    for k, (shard, _, _, dt) in enumerate(items):
        ks, ns = shard.shape[1:]
        tr = _row_tile(ks, ns, CAST_TILE)
        which = slot_sets.setdefault((tr, ns, shard.dtype, dt), len(slot_sets))
        for t in range(ks // tr):
            tiles.append((k, t, tr, which, counts.get(which, 0) % 2))
            counts[which] = counts.get(which, 0) + 1
    n_sets = len(slot_sets)

    def body(*refs):
        ins, outs = refs[:n], refs[n:2 * n]
        bufs = refs[2 * n:2 * n + 2 * n_sets]
        in_sems, out_sems = refs[2 * n + 2 * n_sets:]
        chip = 2 * lax.axis_index("x") + lax.axis_index("y")
        fetch, store = [], []
        for k, t, tr, which, slot in tiles:
            shard, layer, col, _ = items[k]
            ks, ns = shard.shape[1:]
            if col:
                place = outs[k].at[pl.ds(t * tr, tr), pl.ds(pl.multiple_of(chip * ns, GW), ns)]
            else:
                place = outs[k].at[pl.ds(pl.multiple_of(chip * ks + t * tr, 16), tr), :]
            fetch.append(pltpu.make_async_copy(
                ins[k].at[layer, pl.ds(t * tr, tr), :], bufs[2 * which].at[slot], in_sems.at[which, slot]))
            store.append(pltpu.make_async_copy(bufs[2 * which + 1].at[slot], place, out_sems.at[which, slot]))
        busy = {}
        fetch[0].start()
        for i, (k, t, tr, which, slot) in enumerate(tiles):
            if i + 1 < len(tiles):
                fetch[i + 1].start()
            fetch[i].wait()
            if (which, slot) in busy:
                store[busy[which, slot]].wait()
            bufs[2 * which + 1][slot] = bufs[2 * which][slot].astype(bufs[2 * which + 1].dtype)
            store[i].start()
            busy[which, slot] = i
        for i in busy.values():
            store[i].wait()

    def full_shape(shard, col):
        ks, ns = shard.shape[1:]
        return (ks, ns * NCHIP) if col else (ks * NCHIP, ns)

    scratch = []
    for (tr, ns, dt_in, dt_out) in slot_sets:
        scratch += [pltpu.VMEM((2, tr, ns), dt_in), pltpu.VMEM((2, tr, ns), dt_out)]
    scratch += [pltpu.SemaphoreType.DMA((n_sets, 2)), pltpu.SemaphoreType.DMA((n_sets, 2))]
    return _pc(
        jobs, body, name="cast_rest", in_specs=[ANY] * n, out_specs=[ANY] * n,
        out_shape=[jax.ShapeDtypeStruct(full_shape(shard, col), dt) for shard, _, col, dt in items],
        scratch_shapes=scratch, compiler_params=pltpu.CompilerParams(vmem_limit_bytes=VMEM_LIMIT),
    )(*[shard for shard, _, _, _ in items])


def _pair_sum(g, got, col, sc):
    hk, hn = got.shape
    tr = _row_tile(hk, hn, SUM_TILE)
    g_idx = (lambda i, s: (s[0] * (hk // tr) + i, 0)) if col else (lambda i, s: (i, s[0]))
    plain = pl.BlockSpec((tr, hn), lambda i, s: (i, 0))
    return _tiled(lambda a, b: (a.astype(F32) + b.astype(F32),), "pair_sum", (hk // tr,),
                  [pl.BlockSpec((tr, hn), g_idx), plain], [plain], [jax.ShapeDtypeStruct((hk, hn), BF)], [g, got], sc)[0]


def _chip_sum(ph, got, col, layer, depth, sc, carry):
    qk, qn = got.shape[1:]
    tr = _row_tile(qk, qn, SUM_TILE)
    ph_idx = (lambda i, s: (i, s[1])) if col else (lambda i, s: (s[1] * (qk // tr) + i, 0))
    out_shape = (depth, 2 * qk, qn) if col else (depth, qk, 2 * qn)
    out_idx = (lambda i, s: (layer, s[0] * (qk // tr) + i, 0)) if col else (lambda i, s: (layer, i, s[0]))
    in_specs = [pl.BlockSpec((tr, qn), ph_idx)] + [pl.BlockSpec((None, tr, qn), lambda i, s, j=j: (j, i, 0)) for j in range(3)]
    args = [ph, got, got, got]
    if carry is not None:
        in_specs.append(ANY)
        args.append(carry)
    return _tiled(lambda a, b, c_, d_: (a.astype(F32) + b.astype(F32) + c_.astype(F32) + d_.astype(F32),), "chip_sum",
                  (qk // tr,), in_specs, [pl.BlockSpec((None, tr, qn), out_idx)],
                  [jax.ShapeDtypeStruct(out_shape, F32)], args, sc, alias=None if carry is None else {4: 0})[0]


def _adamw_math(w, g, m, v):
    m = ADAM_B1 * m + (1.0 - ADAM_B1) * g
    v = ADAM_B2 * v + (1.0 - ADAM_B2) * (g * g)
    m_hat = m / (1.0 - ADAM_B1 ** ADAM_STEP)
    v_hat = v / (1.0 - ADAM_B2 ** ADAM_STEP)
    delta = -ADAM_LR * (m_hat / (jnp.sqrt(v_hat) + ADAM_EPS) + ADAM_WD * w)
    return delta, m, v


def _adamw_layer(w, g, m, v, layer, carry):
    k, n = w.shape[1:]
    tr = _row_tile(k, n, ADAM_TILE)
    blk = pl.BlockSpec((None, tr, n), lambda i: (layer, i, 0))
    carry = list(carry or [])
    return _tiled(lambda *a: (*_adamw_math(*a), a[1]), "adamw_layer", (k // tr,), [blk] * 4 + [ANY] * len(carry),
                  [blk] * 4, [jax.ShapeDtypeStruct(w.shape, F32)] * 4, [w, g, m, v] + carry, None,
                  alias={4 + pos: pos for pos in range(len(carry))})


def _adamw_stream(items, jobs=()):
    n = len(items)
    tiles, slot_sets, counts = [], {}, {}
    for k, item in enumerate(items):
        ks, ns = item[0].shape[1:]
        tr = _row_tile(ks, ns, ADAM_STREAM_TILE)
        which = slot_sets.setdefault((tr, ns), len(slot_sets))
        for t in range(ks // tr):
            tiles.append((k, t, tr, which, counts.get(which, 0) % 2))
            counts[which] = counts.get(which, 0) + 1
    n_sets = len(slot_sets)

    def body(*refs):
        ins, outs = refs[:8 * n], refs[8 * n:12 * n]
        bufs = refs[12 * n:12 * n + 2 * n_sets]
        in_sems, out_sems = refs[12 * n + 2 * n_sets:]
        fetch, store = [], []
        for k, t, tr, which, slot in tiles:
            layer, rows = items[k][4], pl.ds(t * tr, tr)
            fetch.append([pltpu.make_async_copy(ins[8 * k + j].at[layer, rows, :], bufs[2 * which].at[slot, j],
                                                in_sems.at[which, slot, j]) for j in range(4)])
            store.append([pltpu.make_async_copy(bufs[2 * which + 1].at[slot, j], outs[4 * k + j].at[layer, rows, :],
                                                out_sems.at[which, slot, j]) for j in range(4)])
        busy = {}
        for cp in fetch[0]:
            cp.start()
        for i, (k, t, tr, which, slot) in enumerate(tiles):
            if i + 1 < len(tiles):
                for cp in fetch[i + 1]:
                    cp.start()
            for cp in fetch[i]:
                cp.wait()
            if (which, slot) in busy:
                for cp in store[busy[which, slot]]:
                    cp.wait()
            src, dst = bufs[2 * which], bufs[2 * which + 1]
            grad = src[slot, 1]
            for j, val in enumerate((*_adamw_math(src[slot, 0], grad, src[slot, 2], src[slot, 3]), grad)):
                dst[slot, j] = val
            for cp in store[i]:
                cp.start()
            busy[which, slot] = i
        for i in busy.values():
            for cp in store[i]:
                cp.wait()

    scratch = []
    for tr, ns in slot_sets:
        scratch += [pltpu.VMEM((2, 4, tr, ns), F32), pltpu.VMEM((2, 4, tr, ns), F32)]
    scratch += [pltpu.SemaphoreType.DMA((n_sets, 2, 4)), pltpu.SemaphoreType.DMA((n_sets, 2, 4))]
    args = [a for w, g, m, v, _, carry in items for a in (w, g, m, v, *carry)]
    outs, job_outs = _pc(
        jobs, body, name="adamw_stream", in_specs=[ANY] * (8 * n), out_specs=[ANY] * (4 * n),
        out_shape=[jax.ShapeDtypeStruct(item[0].shape, F32) for item in items for _ in range(4)],
        input_output_aliases={8 * k + 4 + j: 4 * k + j for k in range(n) for j in range(4)},
        scratch_shapes=scratch, compiler_params=pltpu.CompilerParams(vmem_limit_bytes=VMEM_LIMIT),
    )(*args)
    return [outs[4 * k:4 * k + 4] for k in range(n)], job_outs


def _adamw(w, g, m, v):
    shape = w.shape
    two_d = lambda a: a.reshape(-1, shape[-1])
    outs = _elementwise(_adamw_math, "adamw", [two_d(w), two_d(g), two_d(m), two_d(v)], [F32, F32, F32])
    return [o.reshape(shape) for o in outs]


ANY = pl.BlockSpec(memory_space=pl.ANY)


def _place():
    x, y, c = lax.axis_index("x"), lax.axis_index("y"), lax.axis_index("c")
    chips = [(1 - x, y), (x, 1 - y), (1 - x, 1 - y)]
    return x, y, c, chips


def _cols(ref, start, size):
    idx = (slice(None),) * (len(ref.shape) - 1) + (pl.ds(pl.multiple_of(start, GW), size),)
    return ref.at[idx]


def _rows_of(ref, start, size):
    nd = len(ref.shape)
    idx = (slice(None),) * (nd - 2) + (pl.ds(pl.multiple_of(start, 16), size), slice(None))
    return ref.at[idx]


def _region(ref, col_sharded, chip, half):
    k, n = ref.shape
    align = 16 if ref.dtype == BF else 8
    if col_sharded:
        return ref.at[pl.ds(pl.multiple_of(half * (k // 2), align), k // 2),
                      pl.ds(pl.multiple_of(chip * (n // NCHIP), GW), n // NCHIP)]
    rows = k // (2 * NCHIP)
    return ref.at[pl.ds(pl.multiple_of((2 * chip + half) * rows, align), rows), :]


def _job_gather(bufs, col_sharded, handoff):
    n = len(bufs)

    def copies(outs, send_sems, recv_sems, stage):
        x, y, c, chips = _place()
        sends, lands = [], []
        for k in range(n):
            for j, chip in enumerate(chips):
                theirs = 2 * chip[0] + chip[1]
                if stage == 0:
                    src, to = _region(outs[k], col_sharded[k], 2 * x + y, c), (*chip, c)
                    land = _region(outs[k], col_sharded[k], theirs, c)
                else:
                    src, to = _region(outs[k], col_sharded[k], theirs, c), (x, y, 1 - c)
                    land = _region(outs[k], col_sharded[k], theirs, 1 - c)
                sem = 3 * n * stage + 3 * k + j
                sems = dict(send_sem=send_sems.at[sem], recv_sem=recv_sems.at[sem], device_id=to, device_id_type=MESH)
                sends.append(pltpu.make_async_remote_copy(src_ref=src, dst_ref=src, **sems))
                lands.append(pltpu.make_async_remote_copy(src_ref=land, dst_ref=land, **sems))
        return sends, lands

    def start(ins, outs, send_sems, recv_sems):
        for cp in copies(outs, send_sems, recv_sems, 0)[0]:
            cp.start()

    def hand_on(ins, outs, send_sems, recv_sems):
        for cp in copies(outs, send_sems, recv_sems, 0)[1]:
            cp.wait_recv()
        for cp in copies(outs, send_sems, recv_sems, 1)[0]:
            cp.start()

    def finish(ins, outs, send_sems, recv_sems):
        sends, lands = copies(outs, send_sems, recv_sems, 1)
        for cp in lands:
            cp.wait_recv()
        for cp in copies(outs, send_sems, recv_sems, 0)[0] + sends:
            cp.wait_send()

    return _Job(bufs, n, [], 6 * n, [(0.0, start), (handoff, hand_on), (1.0, finish)])


def _half(ref, col_sharded, c):
    k, n = ref.shape[-2:]
    return _rows_of(ref, c * (k // 2), k // 2) if col_sharded else _cols(ref, c * (n // 2), n // 2)


def _quarter(ref, col_sharded, j):
    k, n = ref.shape[-2:]
    return _cols(ref, j * (n // NCHIP), n // NCHIP) if col_sharded else _rows_of(ref, j * (k // NCHIP), k // NCHIP)


def _job_pair(grads, col_sharded):
    n = len(grads)

    def half_shape(g, col):
        return (g.shape[0] // 2, g.shape[1]) if col else (g.shape[0], g.shape[1] // 2)

    def copies(ins, got, send_sems, recv_sems):
        x, y, c, _ = _place()
        return [pltpu.make_async_remote_copy(
            src_ref=_half(ins[k], col_sharded[k], 1 - c), dst_ref=got[k], send_sem=send_sems.at[k],
            recv_sem=recv_sems.at[k], device_id=(x, y, 1 - c), device_id_type=MESH) for k in range(n)]

    def start(*refs):
        for cp in copies(*refs):
            cp.start()

    def finish(*refs):
        for cp in copies(*refs):
            cp.wait()

    fresh = [jax.ShapeDtypeStruct(half_shape(g, col), g.dtype) for g, col in zip(grads, col_sharded)]
    return _Job(grads, 0, fresh, n, [(0.0, start), (1.0, finish)])


def _job_chip(halves, col_sharded):
    n = len(halves)

    def quarter_shape(h, col):
        return (3, h.shape[0], h.shape[1] // NCHIP) if col else (3, h.shape[0] // NCHIP, h.shape[1])

    def copies(ins, got, send_sems, recv_sems):
        x, y, c, chips = _place()
        return [pltpu.make_async_remote_copy(
            src_ref=_quarter(ins[k], col_sharded[k], 2 * chip[0] + chip[1]), dst_ref=got[k].at[j],
            send_sem=send_sems.at[3 * k + j], recv_sem=recv_sems.at[3 * k + j], device_id=(*chip, c), device_id_type=MESH)
            for k in range(n) for j, chip in enumerate(chips)]

    def start(*refs):
        for cp in copies(*refs):
            cp.start()

    def finish(*refs):
        for cp in copies(*refs):
            cp.wait()

    fresh = [jax.ShapeDtypeStruct(quarter_shape(h, col), h.dtype) for h, col in zip(halves, col_sharded)]
    return _Job(halves, 0, fresh, 3 * n, [(0.0, start), (1.0, finish)])


def _job_sibling(shards, col_sharded, layers):
    n = len(shards)

    def copies(outs, send_sems, recv_sems):
        x, y, c, _ = _place()
        sends, lands = [], []
        for k in range(n):
            sems = dict(send_sem=send_sems.at[k], recv_sem=recv_sems.at[k], device_id=(x, y, 1 - c), device_id_type=MESH)
            mine = _half(outs[k].at[layers[k]], col_sharded[k], c)
            theirs = _half(outs[k].at[layers[k]], col_sharded[k], 1 - c)
            sends.append(pltpu.make_async_remote_copy(src_ref=mine, dst_ref=mine, **sems))
            lands.append(pltpu.make_async_remote_copy(src_ref=theirs, dst_ref=theirs, **sems))
        return sends, lands

    def start(ins, outs, send_sems, recv_sems):
        for cp in copies(outs, send_sems, recv_sems)[0]:
            cp.start()

    def finish(ins, outs, send_sems, recv_sems):
        sends, lands = copies(outs, send_sems, recv_sems)
        for cp in lands:
            cp.wait_recv()
        for cp in sends:
            cp.wait_send()

    return _Job(shards, n, [], n, [(0.0, start), (1.0, finish)])


def _all_reduce_small(buf, jobs=()):
    rows = buf.shape[0]
    per = rows // NDEV
    flips = [(fx, fy, fc) for fx in (0, 1) for fy in (0, 1) for fc in (0, 1)][1:]

    def body(in_ref, out_ref, got_ref, send_sems, recv_sems):
        x, y, c, _ = _place()
        me = 4 * x + 2 * y + c

        def peer(f):
            return tuple(1 - pos if flip else pos for pos, flip in zip((x, y, c), f))

        def block(ref, dev):
            return ref.at[pl.ds(pl.multiple_of(dev * per, 8), per), :]

        scatter = []
        for k, f in enumerate(flips):
            px, py, pc = peer(f)
            scatter.append(pltpu.make_async_remote_copy(
                src_ref=block(in_ref, 4 * px + 2 * py + pc), dst_ref=got_ref.at[k], send_sem=send_sems.at[k],
                recv_sem=recv_sems.at[k], device_id=(px, py, pc), device_id_type=MESH))
        for cp in scatter:
            cp.start()
        for cp in scatter:
            cp.wait()
        total = block(in_ref, me)[...]
        for k in range(len(flips)):
            total = total + got_ref[k]
        block(out_ref, me)[...] = total
        share = []
        for k, f in enumerate(flips):
            share.append(pltpu.make_async_remote_copy(
                src_ref=block(out_ref, me), dst_ref=block(out_ref, me), send_sem=send_sems.at[7 + k],
                recv_sem=recv_sems.at[7 + k], device_id=peer(f), device_id_type=MESH))
        for cp in share:
            cp.start()
        for k, f in enumerate(flips):
            share[k].wait_send()
            px, py, pc = peer(f)
            theirs = block(out_ref, 4 * px + 2 * py + pc)
            pltpu.make_async_remote_copy(
                src_ref=theirs, dst_ref=theirs, send_sem=send_sems.at[7 + k], recv_sem=recv_sems.at[7 + k],
                device_id=(px, py, pc), device_id_type=MESH).wait_recv()

    vmem = pl.BlockSpec(memory_space=pltpu.VMEM)
    return _pc(
        jobs, body, name="all_reduce_small", in_specs=[vmem], out_specs=vmem,
        out_shape=jax.ShapeDtypeStruct((rows, GW), F32),
        scratch_shapes=[pltpu.VMEM((NDEV - 1, per, GW), F32), pltpu.SemaphoreType.DMA((14,)),
                        pltpu.SemaphoreType.DMA((14,))],
    )(buf)


BIG = ("ffn1_w13", "ffn1_w2", "w_in", "w_up", "w_out", "ffn2_w13", "ffn2_w2")
BIG_COL_SHARDED = (True, False, True, True, False, True, False)
SMALL = ("ffn1_norm", "mix_norm", "pool_w", "pool_scale", "sconv_w", "cconv_w", "cconv_ln_g", "cconv_ln_b",
         "sgu_ln_g", "sgu_ln_b", "sgu_w", "sgu_b", "ffn2_norm", "final_norm")
WEIGHTS = ("ffn1_norm", "ffn1_w13", "ffn1_w2", "mix_norm", "w_in", "pool_w", "pool_scale", "sconv_w", "cconv_w",
           "cconv_ln_g", "cconv_ln_b", "sgu_ln_g", "sgu_ln_b", "sgu_w", "sgu_b", "w_up", "w_out", "ffn2_norm",
           "ffn2_w13", "ffn2_w2", "final_norm")


def _pad_rows(a, rows):
    return jnp.pad(a, ((0, 0), (0, rows - a.shape[1]), (0, 0)))


def kernel(x, ffn1_norm, ffn1_w13, ffn1_w2, mix_norm, w_in, pool_w, pool_scale, sconv_w, cconv_w, cconv_ln_g, cconv_ln_b, sgu_ln_g, sgu_ln_b, sgu_w, sgu_b, w_up, w_out, ffn2_norm, ffn2_w13, ffn2_w2, final_norm, loss_target, m_ffn1_norm, m_ffn1_w13, m_ffn1_w2, m_mix_norm, m_w_in, m_pool_w, m_pool_scale, m_sconv_w, m_cconv_w, m_cconv_ln_g, m_cconv_ln_b, m_sgu_ln_g, m_sgu_ln_b, m_sgu_w, m_sgu_b, m_w_up, m_w_out, m_ffn2_norm, m_ffn2_w13, m_ffn2_w2, m_final_norm, v_ffn1_norm, v_ffn1_w13, v_ffn1_w2, v_mix_norm, v_w_in, v_pool_w, v_pool_scale, v_sconv_w, v_cconv_w, v_cconv_ln_g, v_cconv_ln_b, v_sgu_ln_g, v_sgu_ln_b, v_sgu_w, v_sgu_b, v_w_up, v_w_out, v_ffn2_norm, v_ffn2_w13, v_ffn2_w2, v_final_norm):
    args = dict(locals())
    w = {nm: args[nm] for nm in WEIGHTS}
    m = {nm: args["m_" + nm] for nm in WEIGHTS}
    v = {nm: args["v_" + nm] for nm in WEIGHTS}
    depth = ffn1_w13.shape[0]
    chip = 2 * lax.axis_index("x") + lax.axis_index("y")

    sc = jnp.stack([lax.axis_index("c"), chip]).astype(jnp.int32)
    col_of = dict(zip(BIG, BIG_COL_SHARDED), sconv_w=True, cconv_w=True)
    sources = {nm: (w[nm], BF) for nm in BIG}
    sources["w_up"] = (w_up.reshape(depth, NBR * BW, w_up.shape[-1]), BF)
    sources["sconv_w"] = (_pad_rows(sconv_w, 2 * SKP), F32)
    sources["cconv_w"] = (_pad_rows(cconv_w, 2 * CKP), F32)
    first_group = ("ffn1_w13", "ffn1_w2")
    full = [dict() for _ in range(depth)]
    for nm in first_group:
        full[0][nm] = _cast_into(sources[nm][0], 0, col_of[nm], sources[nm][1], sc)

    def gather_job(l, names, handoff):
        return _job_gather([full[l][nm] for nm in names], [col_of[nm] for nm in names], handoff)

    def gather_with(l, names, handoff, call):
        res, job_outs = call([gather_job(l, names, handoff)] if l < depth else [])
        if l < depth:
            full[l].update(zip(names, job_outs[0]))
        return res

    rest = [(nm, l) for l in range(depth) for nm in sources if not (l == 0 and nm in first_group)]
    casted = gather_with(0, first_group, 1.0, lambda jobs: _cast_rest(
        [(sources[nm][0], l, col_of[nm], sources[nm][1]) for nm, l in rest], jobs))
    for (nm, l), arr in zip(rest, casted):
        full[l][nm] = arr

    xs = x[0]
    row = lambda a: a.reshape(1, -1)
    saved = []
    for l in range(depth):
        g1, gm, g2 = row(ffn1_norm[l]), row(mix_norm[l]), row(ffn2_norm[l])
        x1, h1, s1, ab1 = gather_with(l, ("w_in",), 0.85, lambda jobs: _ffn_fwd(
            xs, g1, full[l]["ffn1_w13"], full[l]["ffn1_w2"], jobs))
        hm, p = gather_with(l, ("w_up", "w_out", "sconv_w", "cconv_w", "ffn2_w2"), 0.75, lambda jobs: _mix_in(
            x1, gm, full[l]["w_in"], jobs))
        branch = (pool_w[l], row(pool_scale[l]), full[l]["sconv_w"][:SKP], full[l]["cconv_w"][:CKP], row(cconv_ln_g[l]),
                  row(cconv_ln_b[l]), row(sgu_ln_g[l]), row(sgu_ln_b[l]), sgu_w[l], sgu_b[l].T)
        y, conv_z = gather_with(l, ("ffn2_w13",), 0.85, lambda jobs: _mix_branches_fwd(p, *branch, jobs=jobs))
        w_up_l = full[l]["w_up"].reshape(NBR, BW, D)
        x2, merged, up = gather_with(l + 1, ("ffn1_w2",), 0.7, lambda jobs: _mix_out_fwd(
            x1, y, p, w_up_l, full[l]["w_out"], jobs))
        x3, h2, s2, ab2 = gather_with(l + 1, ("ffn1_w13",), 0.75, lambda jobs: _ffn_fwd(
            x2, g2, full[l]["ffn2_w13"], full[l]["ffn2_w2"], jobs))
        lw = dict(g1=g1, gm=gm, g2=g2, w13a=full[l]["ffn1_w13"], w2a=full[l]["ffn1_w2"], w13b=full[l]["ffn2_w13"],
                  w2b=full[l]["ffn2_w2"], w_in=full[l]["w_in"], w_up=w_up_l, w_out=full[l]["w_out"], branch=branch)
        saved.append(dict(lw=lw, x0=xs, x1=x1, x2=x2, h1=h1, s1=s1, ab1=ab1, hm=hm, p=p, y=y, z=conv_z, merged=merged, up=up, h2=h2,
                          s2=s2, ab2=ab2))
        xs = x3

    (dx, d_final, loss_part), _ = _loss_head(xs, row(final_norm), loss_target[0])
    loss = lax.psum(loss_part[0, 0], ("x", "y", "c"))

    ici_us = dict(ffn1_w13=64, ffn1_w2=32, w_in=93, w_up=23, w_out=12, ffn2_w13=64, ffn2_w2=32)
    parts, pair_sums, reduced, big_updates, pending = {}, {}, {}, {}, []

    def take_jobs(budget_us):
        chosen = []
        for task in list(pending):
            kind, (nm, _) = task
            if kind == "chip":
                if ici_us[nm] > budget_us:
                    continue
                budget_us -= ici_us[nm]
            if kind == "sib" and any(k == "sib" and key[0] == nm for k, key in chosen):
                continue
            chosen.append(task)
            pending.remove(task)
        groups, jobs = [], []
        for kind in ("pair", "sib", "chip"):
            keys = [key for k, key in chosen if k == kind]
            if not keys:
                continue
            cols = [col_of[nm] for nm, _ in keys]
            groups.append((kind, keys))
            if kind == "pair":
                jobs.append(_job_pair([parts[key] for key in keys], cols))
            elif kind == "chip":
                jobs.append(_job_chip([pair_sums[key] for key in keys], cols))
            else:
                jobs.append(_job_sibling([reduced[nm] for nm, _ in keys], cols, [layer for _, layer in keys]))
        return groups, jobs

    def settle(groups, job_outs):
        for (kind, keys), outs in zip(groups, job_outs):
            for key, out in zip(keys, outs):
                nm, layer = key
                if kind == "pair":
                    pair_sums[key] = _pair_sum(parts[key], out, col_of[nm], sc)
                    pending.append(("chip", key))
                elif kind == "chip":
                    assert not any(k == "sib" and other[0] == nm for k, other in pending)
                    reduced[nm] = _chip_sum(pair_sums[key], out, col_of[nm], layer, depth, sc, reduced.get(nm))
                    pending.append(("sib", key))
                else:
                    reduced[nm] = out
                    if layer > 0:
                        big_updates[nm] = _adamw_layer(*adam_operands(nm), layer, big_updates.get(nm))

    def adam_operands(nm):
        as3 = lambda a: a.reshape(reduced[nm].shape)
        return as3(w[nm]), reduced[nm], as3(m[nm]), as3(v[nm])

    def run(budget_us, call, carrier=True):
        groups, jobs = take_jobs(budget_us) if carrier else ([], [])
        res, job_outs = call(jobs)
        settle(groups, job_outs)
        return res

    def wgrad_done(key, partial):
        parts[key] = partial
        pending.append(("pair", key))

    small_parts = {nm: [None] * depth for nm in SMALL if nm != "final_norm"}
    for l in reversed(range(depth)):
        sv = saved[l]
        lw = sv["lw"]
        dx, dab, dyh, dg2 = run(100, lambda jobs: _ffn_bwd(
            dx, sv["x2"], lw["g2"], sv["ab2"], lw["w13b"], lw["w2b"], jobs))
        wgrad_done(("ffn2_w13", l), run(58, lambda jobs: _wgrad(sv["h2"], dab, D, 512, "wgrad_w13", jobs)))
        wgrad_done(("ffn2_w2", l), run(33, lambda jobs: _wgrad(sv["s2"], dyh, 256, D, "wgrad_w2", jobs)))
        small_parts["ffn2_norm"][l] = dg2

        dy, dp, dup, dxb = run(70, lambda jobs: _mix_out_bwd(dx, sv["up"], sv["p"], lw["w_up"], lw["w_out"], jobs))
        wgrad_done(("w_out", l), run(16, lambda jobs: _wgrad(sv["merged"], dxb, D, 512, "wgrad_w_out", jobs), False))
        wgrad_done(("w_up", l), run(25, lambda jobs: _wgrad_groups(sv["y"], dup, BW, D, "wgrad_w_up", jobs), False))
        (dp, d_pool_w, d_pool_scale, d_sconv, d_cconv, d_clg, d_clb, d_slg, d_slb, d_sgu_w, d_sgu_b) = run(
            130, lambda jobs: _mix_branches_bwd(sv["p"], dy, sv["z"], dp, *lw["branch"], jobs=jobs))
        wgrad_done(("w_in", l), run(80, lambda jobs: _wgrad(sv["hm"], dp, D, 512, "wgrad_w_in", jobs)))
        dx, dgm = run(91, lambda jobs: _mix_in_bwd(dp, lw["w_in"], sv["x1"], lw["gm"], dx, jobs))
        small_parts["mix_norm"][l] = dgm
        small_parts["pool_w"][l] = d_pool_w
        small_parts["pool_scale"][l] = d_pool_scale
        small_parts["sconv_w"][l] = d_sconv[:SK]
        small_parts["cconv_w"][l] = d_cconv[:CK]
        small_parts["cconv_ln_g"][l] = d_clg
        small_parts["cconv_ln_b"][l] = d_clb
        small_parts["sgu_ln_g"][l] = d_slg
        small_parts["sgu_ln_b"][l] = d_slb
        small_parts["sgu_w"][l] = d_sgu_w
        small_parts["sgu_b"][l] = jnp.sum(d_sgu_b, axis=-1)

        dx, dab, dyh, dg1 = run(100, lambda jobs: _ffn_bwd(
            dx, sv["x0"], lw["g1"], sv["ab1"], lw["w13a"], lw["w2a"], jobs))
        wgrad_done(("ffn1_w2", l), run(33, lambda jobs: _wgrad(sv["s1"], dyh, 256, D, "wgrad_w2", jobs)))
        wgrad_done(("ffn1_w13", l), run(58, lambda jobs: _wgrad(sv["h1"], dab, D, 512, "wgrad_w13", jobs)))
        small_parts["ffn1_norm"][l] = dg1
    grad_x = dx[None]

    small_local = {nm: jnp.stack(parts).reshape(depth, *w[nm].shape[1:-1], -1) if nm not in ("sconv_w", "cconv_w")
                   else jnp.stack(parts) for nm, parts in small_parts.items()}
    small_local["final_norm"] = d_final.reshape(-1)
    sizes = [small_local[nm].size for nm in SMALL]
    total = sum(sizes)
    pad_to = NDEV * 8 * GW
    padded = -(-total // pad_to) * pad_to
    packed = jnp.concatenate([small_local[nm].reshape(-1) for nm in SMALL] + [jnp.zeros((padded - total,), F32)])
    summed = _all_reduce_small(packed.reshape(-1, GW))[0].reshape(-1)

    for names in (("w_in",), ("ffn2_w13", "ffn2_w2", "w_up", "w_out"), ("ffn1_w2",), ("ffn1_w13",)):
        groups, jobs = take_jobs(float("inf"))
        updates, job_outs = _adamw_stream([(*adam_operands(nm), 0, big_updates[nm]) for nm in names], jobs)
        big_updates.update(zip(names, updates))
        settle(groups, job_outs)
    assert not pending
    big_grads = {nm: big_updates[nm][3].reshape(w[nm].shape) for nm in BIG}
    small_grads, off = {}, 0
    for nm, size in zip(SMALL, sizes):
        small_grads[nm] = summed[off:off + size].reshape(small_local[nm].shape)
        off += size
    for nm in ("sconv_w", "cconv_w"):
        small_grads[nm] = lax.dynamic_slice_in_dim(small_grads[nm], chip * GW, GW, axis=2)

    grads = {**big_grads, **small_grads}

    delta, new_m, new_v = {}, {}, {}
    for nm in BIG:
        delta[nm], new_m[nm], new_v[nm] = [a.reshape(w[nm].shape) for a in big_updates[nm][:3]]
    s_sizes = [w[nm].size for nm in SMALL]
    s_total = sum(s_sizes)
    s_padded = -(-s_total // (8 * GW)) * (8 * GW)

    def pack(tree):
        return jnp.concatenate([tree[nm].reshape(-1) for nm in SMALL] + [jnp.ones((s_padded - s_total,), F32)]).reshape(-1, GW)

    packed_out = _adamw(pack(w), pack(grads), pack(m), pack(v))
    off = 0
    for nm, size in zip(SMALL, s_sizes):
        for tree, arr in zip((delta, new_m, new_v), packed_out):
            tree[nm] = arr.reshape(-1)[off:off + size].reshape(w[nm].shape)
        off += size

    return (loss, grad_x, *[grads[nm] for nm in WEIGHTS], *[delta[nm] for nm in WEIGHTS],
            *[new_m[nm] for nm in WEIGHTS], *[new_v[nm] for nm in WEIGHTS])
```

```python
import jax
import jax.numpy as jnp
from jax import lax
from jax.experimental import pallas as pl
from jax.experimental.pallas import tpu as pltpu

D = 1024
FF = 2816
BW = 512
NBR = 4
MIXC = 4096
INC = 8192
GW = 128
CHUNK = 64
SK = 3
CK = 31
SKP = 8
CKP = 32
HALO = 32
TOKEN_TILE = 256
BRANCH_TILE = 512
EPS = 1e-6
NCHIP = 4
NDEV = 8

ADAM_LR = 0.001
ADAM_B1 = 0.9
ADAM_B2 = 0.999
ADAM_EPS = 1e-08
ADAM_WD = 0.01
ADAM_STEP = 10

VMEM_LIMIT = 56 * 1024 * 1024

BF = jnp.bfloat16
F32 = jnp.float32
MESH = pl.DeviceIdType.MESH
NT = (((1,), (1,)), ((), ()))
TN = (((0,), (0,)), ((), ()))


def _params(n_axes):
    return pltpu.CompilerParams(dimension_semantics=("arbitrary",) * n_axes, vmem_limit_bytes=VMEM_LIMIT)


class _Job:
    def __init__(self, args, n_inplace, fresh, nsem, phases):
        self.args, self.n_inplace, self.fresh, self.nsem, self.phases = list(args), n_inplace, list(fresh), nsem, phases


def _pc(jobs, body, name, in_specs, out_specs, out_shape, grid=(), scratch_shapes=(), input_output_aliases=None,
        compiler_params=None, grid_spec_scalars=None):
    single = not isinstance(out_shape, (list, tuple))
    core_out_specs = [out_specs] if single else list(out_specs)
    core_out_shape = [out_shape] if single else list(out_shape)
    n_in, n_out, n_scr = len(in_specs), len(core_out_specs), len(scratch_shapes)
    n_pre = 0 if grid_spec_scalars is None else 1
    all_in, all_out, all_shape = list(in_specs), list(core_out_specs), list(core_out_shape)
    all_scr, aliases, extra_args, layout = list(scratch_shapes), dict(input_output_aliases or {}), [], []
    for job in jobs:
        n_job_out = job.n_inplace + len(job.fresh)
        layout.append((len(all_in), len(job.args), len(all_out), n_job_out, len(all_scr)))
        for a in range(job.n_inplace):
            aliases[n_pre + len(all_in) + a] = len(all_out) + a
        all_in += [ANY] * len(job.args)
        extra_args += job.args
        all_out += [ANY] * n_job_out
        all_shape += [jax.ShapeDtypeStruct(a.shape, a.dtype) for a in job.args[:job.n_inplace]] + job.fresh
        all_scr += [pltpu.SemaphoreType.DMA((job.nsem,)), pltpu.SemaphoreType.DMA((job.nsem,))]
    steps = 1
    for extent in grid:
        steps *= extent
    events = []
    for (i0, na, o0, no, s0), job in zip(layout, jobs):
        for frac, fn in job.phases:
            events.append((min(int(frac * steps), steps - 1), frac >= 1.0, len(events), fn, (i0, na, o0, no, s0)))
    events.sort(key=lambda e: e[:3])

    def wrapped(*refs):
        pre, refs = refs[:n_pre], refs[n_pre:]
        ins, outs, scr = refs[:len(all_in)], refs[len(all_in):len(all_in) + len(all_out)], refs[len(all_in) + len(all_out):]
        step = 0
        for axis, extent in enumerate(grid):
            step = step * extent + pl.program_id(axis)

        def emit(event):
            at, _, _, fn, (i0, na, o0, no, s0) = event
            run = lambda: fn(ins[i0:i0 + na], outs[o0:o0 + no], scr[s0], scr[s0 + 1])
            if steps == 1:
                run()
            else:
                pl.when(step == at)(run)

        for event in events:
            if not event[1]:
                emit(event)
        body(*pre, *ins[:n_in], *outs[:n_out], *scr[:n_scr])
        for event in events:
            if event[1]:
                emit(event)

    kwargs = dict(name=name, out_shape=all_shape, input_output_aliases=aliases)
    if compiler_params is not None:
        kwargs["compiler_params"] = compiler_params
    if grid_spec_scalars is not None:
        kwargs["grid_spec"] = pltpu.PrefetchScalarGridSpec(
            num_scalar_prefetch=1, grid=grid, in_specs=all_in, out_specs=all_out, scratch_shapes=all_scr)
    else:
        kwargs.update(in_specs=all_in, out_specs=all_out, scratch_shapes=all_scr)
        if grid:
            kwargs["grid"] = grid
    call = pl.pallas_call(wrapped, **kwargs)

    def run_call(*args):
        pre_args = [] if grid_spec_scalars is None else [grid_spec_scalars]
        res = list(call(*pre_args, *args, *extra_args))
        core = res[0] if single else res[:n_out]
        job_outs = [res[o0:o0 + no] for (_, _, o0, no, _) in layout]
        return core, job_outs

    return run_call


def _full(shape):
    nd = len(shape)
    return pl.BlockSpec(shape, lambda *_: (0,) * nd, pipeline_mode=pl.Buffered(1))


def _rows(ts, width, col=0):
    return pl.BlockSpec((ts, width), lambda i: (i, col))


def _sig(v):
    return jax.nn.sigmoid(v)


def _rms_stats(x):
    r = lax.rsqrt(jnp.mean(x * x, axis=-1, keepdims=True) + EPS)
    return r, x * r


def _rms_bwd(x, g, dh):
    r, xh = _rms_stats(x)
    dg = jnp.sum(dh * xh, axis=0, keepdims=True)
    dxh = dh * g
    dx = r * (dxh - xh * jnp.mean(dxh * xh, axis=-1, keepdims=True))
    return dx, dg


def _accumulate(ref, val, first):
    @pl.when(first)
    def _():
        ref[...] = val

    @pl.when(jnp.logical_not(first))
    def _():
        ref[...] += val


def _ffn_fwd(x, g, w13, w2, jobs=()):
    s_len = x.shape[0]
    ts = min(TOKEN_TILE, s_len)

    def body(x_ref, g_ref, w13_ref, w2_ref, xo_ref, h_ref, s_ref, dsab_ref):
        xv = x_ref[...]
        r, xh = _rms_stats(xv)
        h = (xh * g_ref[...]).astype(BF)
        h_ref[...] = h
        ab = jnp.dot(h, w13_ref[...], preferred_element_type=F32)
        a = ab[:, :FF]
        b = ab[:, FF:]
        sg = _sig(a)
        sil = a * sg
        dsab_ref[:, :FF] = (b * (sg * (1.0 + a * (1.0 - sg)))).astype(BF)
        dsab_ref[:, FF:] = sil.astype(BF)
        s = (sil * b).astype(BF)
        s_ref[...] = s
        xo_ref[...] = xv + 0.5 * jnp.dot(s, w2_ref[...], preferred_element_type=F32)

    return _pc(
        jobs, body, name="ffn_fwd", grid=(s_len // ts,),
        in_specs=[_rows(ts, D), _full((1, D)), _full((D, 2 * FF)), _full((FF, D))],
        out_specs=[_rows(ts, D), _rows(ts, D), _rows(ts, FF), _rows(ts, 2 * FF)],
        out_shape=[jax.ShapeDtypeStruct((s_len, D), F32), jax.ShapeDtypeStruct((s_len, D), BF),
                   jax.ShapeDtypeStruct((s_len, FF), BF), jax.ShapeDtypeStruct((s_len, 2 * FF), BF)],
        compiler_params=_params(1),
    )(x, g, w13, w2)


def _ffn_bwd(dxo, x, g, dsab, w13, w2, jobs=()):
    s_len = x.shape[0]
    ts = min(TOKEN_TILE, s_len)

    def body(dxo_ref, x_ref, g_ref, dsab_ref, w13_ref, w2_ref, dxi_ref, dab_ref, dy_ref, dg_ref):
        i = pl.program_id(0)
        dxo_v = dxo_ref[...]
        dy = (0.5 * dxo_v).astype(BF)
        dy_ref[...] = dy
        ds = lax.dot_general(dy, w2_ref[...], NT, preferred_element_type=F32)
        dab_ref[:, :FF] = (ds * dsab_ref[:, :FF].astype(F32)).astype(BF)
        dab_ref[:, FF:] = (ds * dsab_ref[:, FF:].astype(F32)).astype(BF)
        dh = lax.dot_general(dab_ref[...], w13_ref[...], NT, preferred_element_type=F32)
        dx, dg = _rms_bwd(x_ref[...], g_ref[...], dh)
        dxi_ref[...] = dxo_v + dx
        _accumulate(dg_ref, dg, i == 0)

    return _pc(
        jobs, body, name="ffn_bwd", grid=(s_len // ts,),
        in_specs=[_rows(ts, D), _rows(ts, D), _full((1, D)), _rows(ts, 2 * FF), _full((D, 2 * FF)), _full((FF, D))],
        out_specs=[_rows(ts, D), _rows(ts, 2 * FF), _rows(ts, D), pl.BlockSpec((1, D), lambda i: (0, 0))],
        out_shape=[jax.ShapeDtypeStruct((s_len, D), F32), jax.ShapeDtypeStruct((s_len, 2 * FF), BF),
                   jax.ShapeDtypeStruct((s_len, D), BF), jax.ShapeDtypeStruct((1, D), F32)],
        compiler_params=_params(1),
    )(dxo, x, g, dsab, w13, w2)


def _wgrad(a, b, tk, tn, name, jobs=()):
    s_len, k = a.shape
    n = b.shape[1]

    def body(a_ref, b_ref, o_ref):
        o_ref[...] = lax.dot_general(a_ref[...], b_ref[...], TN, preferred_element_type=F32).astype(BF)

    return _pc(
        jobs, body, name=name, grid=(k // tk, n // tn),
        in_specs=[pl.BlockSpec((s_len, tk), lambda i, j: (0, i)), pl.BlockSpec((s_len, tn), lambda i, j: (0, j))],
        out_specs=pl.BlockSpec((tk, tn), lambda i, j: (i, j)),
        out_shape=jax.ShapeDtypeStruct((k, n), BF),
        compiler_params=_params(2),
    )(a, b)


def _wgrad_groups(a, b, ka, nb, name, jobs=()):
    s_len = a.shape[0]
    groups = a.shape[1] // ka

    def body(a_ref, b_ref, o_ref):
        o_ref[...] = lax.dot_general(a_ref[...], b_ref[...], TN, preferred_element_type=F32).astype(BF)

    return _pc(
        jobs, body, name=name, grid=(groups,),
        in_specs=[pl.BlockSpec((s_len, ka), lambda gi: (0, gi)), pl.BlockSpec((s_len, nb), lambda gi: (0, gi))],
        out_specs=pl.BlockSpec((ka, nb), lambda gi: (gi, 0)),
        out_shape=jax.ShapeDtypeStruct((groups * ka, nb), BF),
        compiler_params=_params(1),
    )(a, b)


def _mix_in(x, g, w_in, jobs=()):
    s_len = x.shape[0]
    ts = min(TOKEN_TILE, s_len)

    def body(x_ref, g_ref, w_ref, h_ref, p_ref):
        _, xh = _rms_stats(x_ref[...])
        h = (xh * g_ref[...]).astype(BF)
        h_ref[...] = h
        p_ref[...] = jnp.dot(h, w_ref[...], preferred_element_type=F32).astype(BF)

    return _pc(
        jobs, body, name="mix_in", grid=(s_len // ts,),
        in_specs=[_rows(ts, D), _full((1, D)), _full((D, INC))],
        out_specs=[_rows(ts, D), _rows(ts, INC)],
        out_shape=[jax.ShapeDtypeStruct((s_len, D), BF), jax.ShapeDtypeStruct((s_len, INC), BF)],
        compiler_params=_params(1),
    )(x, g, w_in)


def _shift(e, j):
    n = e.shape[0]
    j = j % n
    return e if j == 0 else pltpu.roll(e, j, 0)


def _ln_stats(z):
    mu = jnp.mean(z, axis=-1, keepdims=True)
    zc = z - mu
    rs = lax.rsqrt(jnp.mean(zc * zc, axis=-1, keepdims=True) + EPS)
    return rs, zc * rs


def _ln_bwd(rs, zn, dzn):
    return rs * (dzn - jnp.mean(dzn, axis=-1, keepdims=True) - zn * jnp.mean(dzn * zn, axis=-1, keepdims=True))


_GELU_C0 = 0.7978845608028654
_GELU_C1 = 0.044715


def _gelu(p):
    th = jnp.tanh(_GELU_C0 * (p + _GELU_C1 * p * p * p))
    return 0.5 * p * (1.0 + th), th


def _gelu_grad(p, th):
    return 0.5 * (1.0 + th) + 0.5 * p * (1.0 - th * th) * (_GELU_C0 * (1.0 + 3.0 * _GELU_C1 * p * p))


def _pool_diff(a, t, sign):
    outs = []
    for gi in range(NBR):
        win = 2 ** (gi + 1)
        ag = a[:, gi * GW:(gi + 1) * GW]
        cnt = jnp.clip(t + 1, 1, win).astype(F32)
        ws = ag if sign > 0 else ag / cnt
        for s in range(gi + 1):
            ws = ws + _shift(ws, sign * (2 ** s))
        outs.append((ws / cnt if sign > 0 else ws) - ag)
    return outs


def _sgu_mask():
    row = lax.broadcasted_iota(jnp.int32, (GW, GW), 0)
    col = lax.broadcasted_iota(jnp.int32, (GW, GW), 1)
    return (col // CHUNK) <= (row // CHUNK)


def _assemble(pe_ref, prev_ref, cur_ref, next_ref, i, last, ts):
    pe_ref[0:HALO, :] = jnp.where(i > 0, prev_ref[...], jnp.zeros_like(prev_ref))
    pe_ref[HALO:HALO + ts, :] = cur_ref[...]
    if next_ref is not None:
        pe_ref[HALO + ts:, :] = jnp.where(i < last, next_ref[...], jnp.zeros_like(next_ref))


def _halo_specs(ts, width, s_len, with_next):
    per = ts // HALO
    specs = [pl.BlockSpec((HALO, width), lambda i: (jnp.maximum(i * per - 1, 0), 0)),
             pl.BlockSpec((ts, width), lambda i: (i, 0))]
    if with_next:
        specs.append(pl.BlockSpec((HALO, width), lambda i: (jnp.minimum((i + 1) * per, s_len // HALO - 1), 0)))
    return specs


def _branch_weights_specs():
    return [_full((NBR, GW, GW)), _full((1, BW)), _full((SKP, BW)), _full((CKP, BW)), _full((1, BW)), _full((1, BW)),
            _full((1, BW)), _full((1, BW)), _full((NBR, GW, GW)), _full((GW, NBR))]


def _mix_branches_fwd(p, pool_w, pool_scale, sconv_w, cconv_w, cln_g, cln_b, sln_g, sln_b, sgu_w, sgu_bt, jobs=()):
    s_len = p.shape[0]
    ts = min(BRANCH_TILE, s_len)
    ext = HALO + ts

    def body(pp_ref, pc_ref, pw_ref, ps_ref, sw_ref, cw_ref, clg_ref, clb_ref, slg_ref, slb_ref, gw_ref, gb_ref,
             y_ref, z_ref, pe_ref):
        i = pl.program_id(0)
        _assemble(pe_ref, pp_ref, pc_ref, None, i, 0, ts)
        t = i * ts - HALO + lax.broadcasted_iota(jnp.int32, (ext, 1), 0)

        dgs = _pool_diff(pe_ref[:, 0:BW].astype(F32), t, 1)
        for gi in range(NBR):
            e = jnp.dot(dgs[gi][HALO:].astype(BF), pw_ref[gi].astype(BF), preferred_element_type=F32)
            y_ref[:, gi * GW:(gi + 1) * GW] = (e * ps_ref[:, gi * GW:(gi + 1) * GW]).astype(BF)

        xin = pe_ref[:, BW:2 * BW].astype(F32)
        cg = pe_ref[:, 3 * BW:4 * BW].astype(F32)
        q = cg * xin
        cv = sw_ref[2:3, :] * q + sw_ref[1:2, :] * _shift(q, 1) + sw_ref[0:1, :] * _shift(q, 2)
        y_ref[:, BW:2 * BW] = (pe_ref[HALO:, 2 * BW:3 * BW].astype(F32) * cv[HALO:]).astype(BF)

        yg = pe_ref[:, 4 * BW:5 * BW].astype(F32) * _sig(pe_ref[:, 5 * BW:6 * BW].astype(F32))
        z = cw_ref[CK - 1:CK, :] * yg
        for j in range(1, CK):
            z = z + cw_ref[CK - 1 - j:CK - j, :] * _shift(yg, j)
        z_ref[...] = z[HALO:].astype(BF)
        _, zn = _ln_stats(z[HALO:])
        nn = zn * clg_ref[...] + clb_ref[...]
        y_ref[:, 2 * BW:3 * BW] = (nn * _sig(nn)).astype(BF)

        u, _ = _gelu(pc_ref[:, 6 * BW:7 * BW].astype(F32))
        v, _ = _gelu(pc_ref[:, 7 * BW:8 * BW].astype(F32))
        _, vn = _ln_stats(v)
        vn = (vn * slg_ref[...] + slb_ref[...]).astype(BF)
        mask = _sgu_mask()
        for hd in range(NBR):
            wm = jnp.where(mask, gw_ref[hd], 0.0).astype(BF)
            for blk in range(ts // GW):
                rows = slice(blk * GW, (blk + 1) * GW)
                cols = slice(hd * GW, (hd + 1) * GW)
                zz = jnp.dot(wm, vn[rows, cols], preferred_element_type=F32) + gb_ref[:, hd:hd + 1]
                y_ref[rows, 3 * BW + hd * GW:3 * BW + (hd + 1) * GW] = (u[rows, cols] * zz).astype(BF)

    return _pc(
        jobs, body, name="mix_branches_fwd", grid=(s_len // ts,),
        in_specs=_halo_specs(ts, MIXC, s_len, False) + _branch_weights_specs(),
        out_specs=[_rows(ts, NBR * BW), _rows(ts, BW)],
        out_shape=[jax.ShapeDtypeStruct((s_len, NBR * BW), BF), jax.ShapeDtypeStruct((s_len, BW), BF)],
        scratch_shapes=[pltpu.VMEM((ext, MIXC), BF)],
        compiler_params=_params(1),
    )(p, p, pool_w, pool_scale, sconv_w, cconv_w, cln_g, cln_b, sln_g, sln_b, sgu_w, sgu_bt)


def _mix_branches_bwd(p, dy, z, dp, pool_w, pool_scale, sconv_w, cconv_w, cln_g, cln_b, sln_g, sln_b, sgu_w, sgu_bt,
                      jobs=()):
    s_len = p.shape[0]
    ts = min(BRANCH_TILE, s_len)
    ext = ts + 2 * HALO
    last = s_len // ts - 1
    tile = slice(HALO, HALO + ts)
    small_shapes = [(NBR, GW, GW), (1, BW), (SKP, BW), (CKP, BW), (1, BW), (1, BW), (1, BW), (1, BW), (NBR, GW, GW),
                    (NBR, GW, GW)]

    def body(pp_ref, pc_ref, pn_ref, dyp_ref, dyc_ref, dyn_ref, zc_ref, zn_ref, dpin_ref,
             pw_ref, ps_ref, sw_ref, cw_ref, clg_ref, clb_ref, slg_ref, slb_ref, gw_ref, gb_ref,
             dp_ref, dpw_ref, dps_ref, dsw_ref, dcw_ref, dclg_ref, dclb_ref, dslg_ref, dslb_ref, dgw_ref, dgb_ref,
             pe_ref, de_ref):
        del dyp_ref, dpin_ref
        i = pl.program_id(0)
        first = i == 0
        _assemble(pe_ref, pp_ref, pc_ref, pn_ref, i, last, ts)
        de_ref[0:HALO, :] = jnp.zeros((HALO, NBR * BW), BF)
        de_ref[HALO:HALO + ts, :] = dyc_ref[...]
        de_ref[HALO + ts:, :] = jnp.where(i < last, dyn_ref[...], jnp.zeros_like(dyn_ref))
        t = i * ts - HALO + lax.broadcasted_iota(jnp.int32, (ext, 1), 0)

        @pl.when(first)
        def _():
            dsw_ref[...] = jnp.zeros((SKP, BW), F32)
            dcw_ref[...] = jnp.zeros((CKP, BW), F32)

        dgs = _pool_diff(pe_ref[:, 0:BW].astype(F32), t, 1)
        dya = de_ref[:, 0:BW].astype(F32)
        dds = []
        for gi in range(NBR):
            cols = slice(gi * GW, (gi + 1) * GW)
            pw = pw_ref[gi].astype(BF)
            d_t = dgs[gi][tile].astype(BF)
            e = jnp.dot(d_t, pw, preferred_element_type=F32)
            _accumulate(dps_ref.at[:, cols], jnp.sum(dya[tile, cols] * e, axis=0, keepdims=True), first)
            de_g = (dya[:, cols] * ps_ref[:, cols]).astype(BF)
            _accumulate(dpw_ref.at[gi], lax.dot_general(d_t, de_g[tile], TN, preferred_element_type=F32), first)
            dds.append(lax.dot_general(de_g, pw, NT, preferred_element_type=F32))
        das = _pool_diff(jnp.concatenate(dds, axis=1), t, -1)
        for gi in range(NBR):
            dp_ref[:, gi * GW:(gi + 1) * GW] = das[gi][tile].astype(BF)

        xin = pe_ref[:, BW:2 * BW].astype(F32)
        bg = pe_ref[:, 2 * BW:3 * BW].astype(F32)
        cg = pe_ref[:, 3 * BW:4 * BW].astype(F32)
        q = cg * xin
        qs = [q, _shift(q, 1), _shift(q, 2)]
        cv = sw_ref[2:3, :] * qs[0] + sw_ref[1:2, :] * qs[1] + sw_ref[0:1, :] * qs[2]
        dyb = de_ref[:, BW:2 * BW].astype(F32)
        dcv = dyb * bg
        for j in range(SK):
            dsw_ref[SK - 1 - j:SK - j, :] += jnp.sum(dcv[tile] * qs[j][tile], axis=0, keepdims=True)
        dq = sw_ref[2:3, :] * dcv + sw_ref[1:2, :] * _shift(dcv, -1) + sw_ref[0:1, :] * _shift(dcv, -2)
        dp_ref[:, BW:2 * BW] = (dq * cg)[tile].astype(BF)
        dp_ref[:, 2 * BW:3 * BW] = (dyb * cv)[tile].astype(BF)
        dp_ref[:, 3 * BW:4 * BW] = (dq * xin)[tile].astype(BF)

        ca = pc_ref[:, 4 * BW:5 * BW].astype(F32)
        sb = _sig(pc_ref[:, 5 * BW:6 * BW].astype(F32))
        yg_t = ca * sb
        z = jnp.concatenate([zc_ref[...].astype(F32), zn_ref[...].astype(F32)], axis=0)
        rs, zn = _ln_stats(z)
        nn = zn * clg_ref[...] + clb_ref[...]
        sn = _sig(nn)
        dn = de_ref[HALO:, 2 * BW:3 * BW].astype(F32) * (sn * (1.0 + nn * (1.0 - sn)))
        _accumulate(dclg_ref, jnp.sum((dn * zn)[:ts], axis=0, keepdims=True), first)
        _accumulate(dclb_ref, jnp.sum(dn[:ts], axis=0, keepdims=True), first)
        dz = _ln_bwd(rs, zn, dn * clg_ref[...])
        dyg = cw_ref[CK - 1:CK, :] * dz[:ts]
        dcw_ref[CK - 1:CK, :] += jnp.sum(dz[:ts] * yg_t, axis=0, keepdims=True)
        for j in range(1, CK):
            dz_ahead = _shift(dz, -j)[:ts]
            dyg = dyg + cw_ref[CK - 1 - j:CK - j, :] * dz_ahead
            dcw_ref[CK - 1 - j:CK - j, :] += jnp.sum(dz_ahead * yg_t, axis=0, keepdims=True)
        dp_ref[:, 4 * BW:5 * BW] = (dyg * sb).astype(BF)
        dp_ref[:, 5 * BW:6 * BW] = (dyg * ca * sb * (1.0 - sb)).astype(BF)

        pu = pc_ref[:, 6 * BW:7 * BW].astype(F32)
        pv = pc_ref[:, 7 * BW:8 * BW].astype(F32)
        u, thu = _gelu(pu)
        v, thv = _gelu(pv)
        vrs, vn0 = _ln_stats(v)
        vn = (vn0 * slg_ref[...] + slb_ref[...]).astype(BF)
        dyd = dyc_ref[:, 3 * BW:4 * BW].astype(F32)
        dzz = dyd * u
        dzb = dzz.astype(BF)
        mask = _sgu_mask()
        dvn_cols = []
        for hd in range(NBR):
            cols = slice(hd * GW, (hd + 1) * GW)
            wm = jnp.where(mask, gw_ref[hd], 0.0).astype(BF)
            dwm = jnp.zeros((GW, GW), F32)
            dbs = jnp.zeros((GW, GW), F32)
            dvn_rows = []
            for blk in range(ts // GW):
                rows = slice(blk * GW, (blk + 1) * GW)
                zz = jnp.dot(wm, vn[rows, cols], preferred_element_type=F32) + gb_ref[:, hd:hd + 1]
                dp_ref[rows, 6 * BW + hd * GW:6 * BW + (hd + 1) * GW] = (
                    dyd[rows, cols] * zz * _gelu_grad(pu[rows, cols], thu[rows, cols])).astype(BF)
                dwm = dwm + lax.dot_general(dzb[rows, cols], vn[rows, cols], NT, preferred_element_type=F32)
                dbs = dbs + dzz[rows, cols]
                dvn_rows.append(lax.dot_general(wm, dzb[rows, cols], TN, preferred_element_type=F32))
            _accumulate(dgw_ref.at[hd], jnp.where(mask, dwm, 0.0), first)
            _accumulate(dgb_ref.at[hd], dbs, first)
            dvn_cols.append(jnp.concatenate(dvn_rows, axis=0))
        dvn = jnp.concatenate(dvn_cols, axis=1)
        _accumulate(dslg_ref, jnp.sum(dvn * vn0, axis=0, keepdims=True), first)
        _accumulate(dslb_ref, jnp.sum(dvn, axis=0, keepdims=True), first)
        dv = _ln_bwd(vrs, vn0, dvn * slg_ref[...])
        dp_ref[:, 7 * BW:8 * BW] = (dv * _gelu_grad(pv, thv)).astype(BF)

    const = lambda shp: pl.BlockSpec(shp, lambda i: (0,) * len(shp))
    return _pc(
        jobs, body, name="mix_branches_bwd", grid=(s_len // ts,),
        in_specs=(_halo_specs(ts, MIXC, s_len, True) + _halo_specs(ts, NBR * BW, s_len, True)
                  + _halo_specs(ts, BW, s_len, True)[1:] + [pl.BlockSpec(memory_space=pl.ANY)] + _branch_weights_specs()),
        out_specs=[pl.BlockSpec((ts, MIXC), lambda i: (i, 0))] + [const(s) for s in small_shapes],
        out_shape=[jax.ShapeDtypeStruct((s_len, INC), BF)] + [jax.ShapeDtypeStruct(s, F32) for s in small_shapes],
        scratch_shapes=[pltpu.VMEM((ext, MIXC), BF), pltpu.VMEM((ext, NBR * BW), BF)],
        input_output_aliases={8: 0},
        compiler_params=_params(1),
    )(p, p, p, dy, dy, dy, z, z, dp, pool_w, pool_scale, sconv_w, cconv_w, cln_g, cln_b, sln_g, sln_b, sgu_w, sgu_bt)


def _mix_out_fwd(x, y, p, w_up, w_out, jobs=()):
    s_len = x.shape[0]
    ts = min(TOKEN_TILE, s_len)

    def body(x_ref, y_ref, pg_ref, wu_ref, wo_ref, xo_ref, m_ref, up_ref):
        m = jnp.zeros((ts, D), F32)
        for gi in range(NBR):
            up = jnp.dot(y_ref[:, gi * BW:(gi + 1) * BW], wu_ref[gi], preferred_element_type=F32)
            up_ref[:, gi * D:(gi + 1) * D] = up.astype(BF)
            m = m + _sig(pg_ref[:, gi * D:(gi + 1) * D].astype(F32)) * up
        mb = m.astype(BF)
        m_ref[...] = mb
        xo_ref[...] = x_ref[...] + jnp.dot(mb, wo_ref[...], preferred_element_type=F32)

    return _pc(
        jobs, body, name="mix_out_fwd", grid=(s_len // ts,),
        in_specs=[_rows(ts, D), _rows(ts, NBR * BW), _rows(ts, NBR * D, 1), _full((NBR, BW, D)), _full((D, D))],
        out_specs=[_rows(ts, D), _rows(ts, D), _rows(ts, NBR * D)],
        out_shape=[jax.ShapeDtypeStruct((s_len, D), F32), jax.ShapeDtypeStruct((s_len, D), BF),
                   jax.ShapeDtypeStruct((s_len, NBR * D), BF)],
        compiler_params=_params(1),
    )(x, y, p, w_up, w_out)


def _mix_out_bwd(dxo, up, p, w_up, w_out, jobs=()):
    s_len = dxo.shape[0]
    ts = min(TOKEN_TILE, s_len)

    def body(dxo_ref, up_ref, pg_ref, wu_ref, wo_ref, dy_ref, dp_ref, dup_ref, dxb_ref):
        dxb = dxo_ref[...].astype(BF)
        dxb_ref[...] = dxb
        dm = lax.dot_general(dxb, wo_ref[...], NT, preferred_element_type=F32)
        for gi in range(NBR):
            cols = slice(gi * D, (gi + 1) * D)
            gate = _sig(pg_ref[:, cols].astype(F32))
            dp_ref[:, cols] = (dm * up_ref[:, cols].astype(F32) * gate * (1.0 - gate)).astype(BF)
            dup = (dm * gate).astype(BF)
            dup_ref[:, cols] = dup
            dy_ref[:, gi * BW:(gi + 1) * BW] = lax.dot_general(
                dup, wu_ref[gi], NT, preferred_element_type=F32).astype(BF)

    return _pc(
        jobs, body, name="mix_out_bwd", grid=(s_len // ts,),
        in_specs=[_rows(ts, D), _rows(ts, NBR * D), _rows(ts, NBR * D, 1), _full((NBR, BW, D)), _full((D, D))],
        out_specs=[_rows(ts, NBR * BW), _rows(ts, NBR * D, 1), _rows(ts, NBR * D), _rows(ts, D)],
        out_shape=[jax.ShapeDtypeStruct((s_len, NBR * BW), BF), jax.ShapeDtypeStruct((s_len, INC), BF),
                   jax.ShapeDtypeStruct((s_len, NBR * D), BF), jax.ShapeDtypeStruct((s_len, D), BF)],
        compiler_params=_params(1),
    )(dxo, up, p, w_up, w_out)


def _mix_in_bwd(dp, w_in, x, g, dxo, jobs=()):
    s_len = x.shape[0]
    ts = min(TOKEN_TILE, s_len)

    def body(dp_ref, w_ref, x_ref, g_ref, dxo_ref, dxi_ref, dg_ref):
        i = pl.program_id(0)
        dh = lax.dot_general(dp_ref[...], w_ref[...], NT, preferred_element_type=F32)
        dx, dg = _rms_bwd(x_ref[...], g_ref[...], dh)
        dxi_ref[...] = dxo_ref[...] + dx
        _accumulate(dg_ref, dg, i == 0)

    return _pc(
        jobs, body, name="mix_in_bwd", grid=(s_len // ts,),
        in_specs=[_rows(ts, INC), _full((D, INC)), _rows(ts, D), _full((1, D)), _rows(ts, D)],
        out_specs=[_rows(ts, D), pl.BlockSpec((1, D), lambda i: (0, 0))],
        out_shape=[jax.ShapeDtypeStruct((s_len, D), F32), jax.ShapeDtypeStruct((1, D), F32)],
        compiler_params=_params(1),
    )(dp, w_in, x, g, dxo)


def _loss_head(x, g, target, jobs=()):
    s_len = x.shape[0]
    ts = min(512, s_len)

    def body(x_ref, g_ref, t_ref, dx_ref, dg_ref, loss_ref):
        i = pl.program_id(0)
        xv = x_ref[...]
        _, xh = _rms_stats(xv)
        err = xh * g_ref[...] - t_ref[...]
        part = 0.5 * jnp.sum(jnp.mean(err * err, axis=-1, keepdims=True), axis=0, keepdims=True)
        dx, dg = _rms_bwd(xv, g_ref[...], err * (1.0 / D))
        dx_ref[...] = dx
        _accumulate(dg_ref, dg, i == 0)
        _accumulate(loss_ref, jnp.broadcast_to(part, (1, GW)), i == 0)

    return _pc(
        jobs, body, name="loss_head", grid=(s_len // ts,),
        in_specs=[_rows(ts, D), _full((1, D)), _rows(ts, D)],
        out_specs=[_rows(ts, D), pl.BlockSpec((1, D), lambda i: (0, 0)), pl.BlockSpec((1, GW), lambda i: (0, 0))],
        out_shape=[jax.ShapeDtypeStruct((s_len, D), F32), jax.ShapeDtypeStruct((1, D), F32),
                   jax.ShapeDtypeStruct((1, GW), F32)],
        compiler_params=_params(1),
    )(x, g, target)


SUM_TILE = 1 << 20
ADAM_TILE = 1 << 19
CAST_TILE = 1 << 19


def _row_tile(rows, cols, budget=1 << 18):
    tr = rows
    while tr * cols > budget and tr % 16 == 0:
        tr //= 2
    return tr


def _elementwise(fn, name, ins, out_dtypes):
    rows, cols = ins[0].shape
    tr = _row_tile(rows, cols)
    n_in = len(ins)

    def body(*refs):
        res = fn(*[r[...] for r in refs[:n_in]])
        for o_ref, val in zip(refs[n_in:], res):
            o_ref[...] = val.astype(o_ref.dtype)

    outs = pl.pallas_call(
        body, name=name, grid=(rows // tr,),
        in_specs=[_rows(tr, cols)] * n_in, out_specs=[_rows(tr, cols)] * len(out_dtypes),
        out_shape=[jax.ShapeDtypeStruct((rows, cols), dt) for dt in out_dtypes],
        compiler_params=_params(1),
    )(*ins)
    return outs


def _tiled(fn, name, grid, in_specs, out_specs, out_shape, args, scalars=None, alias=None):
    alias = alias or {}
    n_in = len(in_specs) - len(alias)
    n_pre = 0 if scalars is None else 1

    def body(*refs):
        refs = refs[n_pre:]
        res = fn(*[r[...] for r in refs[:n_in]])
        for o_ref, val in zip(refs[len(in_specs):], res):
            o_ref[...] = val.astype(o_ref.dtype)

    aliases = {n_pre + pos: out for pos, out in alias.items()}
    if scalars is None:
        return pl.pallas_call(body, name=name, grid=grid, in_specs=in_specs, out_specs=out_specs, out_shape=out_shape,
                              input_output_aliases=aliases, compiler_params=_params(len(grid)))(*args)
    spec = pltpu.PrefetchScalarGridSpec(num_scalar_prefetch=1, grid=grid, in_specs=in_specs, out_specs=out_specs)
    return pl.pallas_call(body, name=name, grid_spec=spec, out_shape=out_shape, input_output_aliases=aliases,
                          compiler_params=_params(len(grid)))(scalars, *args)


def _cast_into(shard, layer, col, dtype, sc):
    ks, ns = shard.shape[1:]
    tr = _row_tile(ks, ns, SUM_TILE)
    full = (ks, ns * NCHIP) if col else (ks * NCHIP, ns)
    out_idx = (lambda i, s: (i, s[1])) if col else (lambda i, s: (s[1] * (ks // tr) + i, 0))
    return _tiled(lambda v: (v,), "cast_into", (ks // tr,), [pl.BlockSpec((None, tr, ns), lambda i, s: (layer, i, 0))],
                  [pl.BlockSpec((tr, ns), out_idx)], [jax.ShapeDtypeStruct(full, dtype)], [shard], sc)[0]


def _cast_rest(items, jobs=()):
    n = len(items)
    tiles, slot_sets, counts = [], {}, {}
    for k, (shard, _, _, dt) in enumerate(items):
        ks, ns = shard.shape[1:]
        tr = _row_tile(ks, ns, CAST_TILE)
        which = slot_sets.setdefault((tr, ns, shard.dtype, dt), len(slot_sets))
        for t in range(ks // tr):
            tiles.append((k, t, tr, which, counts.get(which, 0) % 2))
            counts[which] = counts.get(which, 0) + 1
    n_sets = len(slot_sets)

    def body(*refs):
        ins, outs = refs[:n], refs[n:2 * n]
        bufs = refs[2 * n:2 * n + 2 * n_sets]
        in_sems, out_sems = refs[2 * n + 2 * n_sets:]
        chip = 2 * lax.axis_index("x") + lax.axis_index("y")
        fetch, store = [], []
        for k, t, tr, which, slot in tiles:
            shard, layer, col, _ = items[k]
            ks, ns = shard.shape[1:]
            if col:
                place = outs[k].at[pl.ds(t * tr, tr), pl.ds(pl.multiple_of(chip * ns, GW), ns)]
            else:
                place = outs[k].at[pl.ds(pl.multiple_of(chip * ks + t * tr, 16), tr), :]
            fetch.append(pltpu.make_async_copy(
                ins[k].at[layer, pl.ds(t * tr, tr), :], bufs[2 * which].at[slot], in_sems.at[which, slot]))
            store.append(pltpu.make_async_copy(bufs[2 * which + 1].at[slot], place, out_sems.at[which, slot]))
        busy = {}
        fetch[0].start()
        for i, (k, t, tr, which, slot) in enumerate(tiles):
            if i + 1 < len(tiles):
                fetch[i + 1].start()
            fetch[i].wait()
            if (which, slot) in busy:
                store[busy[which, slot]].wait()
            bufs[2 * which + 1][slot] = bufs[2 * which][slot].astype(bufs[2 * which + 1].dtype)
            store[i].start()
            busy[which, slot] = i
        for i in busy.values():
            store[i].wait()

    def full_shape(shard, col):
        ks, ns = shard.shape[1:]
        return (ks, ns * NCHIP) if col else (ks * NCHIP, ns)

    scratch = []
    for (tr, ns, dt_in, dt_out) in slot_sets:
        scratch += [pltpu.VMEM((2, tr, ns), dt_in), pltpu.VMEM((2, tr, ns), dt_out)]
    scratch += [pltpu.SemaphoreType.DMA((n_sets, 2)), pltpu.SemaphoreType.DMA((n_sets, 2))]
    return _pc(
        jobs, body, name="cast_rest", in_specs=[ANY] * n, out_specs=[ANY] * n,
        out_shape=[jax.ShapeDtypeStruct(full_shape(shard, col), dt) for shard, _, col, dt in items],
        scratch_shapes=scratch, compiler_params=pltpu.CompilerParams(vmem_limit_bytes=VMEM_LIMIT),
    )(*[shard for shard, _, _, _ in items])


def _pair_sum(g, got, col, sc):
    hk, hn = got.shape
    tr = _row_tile(hk, hn, SUM_TILE)
    g_idx = (lambda i, s: (s[0] * (hk // tr) + i, 0)) if col else (lambda i, s: (i, s[0]))
    plain = pl.BlockSpec((tr, hn), lambda i, s: (i, 0))
    return _tiled(lambda a, b: (a.astype(F32) + b.astype(F32),), "pair_sum", (hk // tr,),
                  [pl.BlockSpec((tr, hn), g_idx), plain], [plain], [jax.ShapeDtypeStruct((hk, hn), BF)], [g, got], sc)[0]


def _chip_sum(ph, got, col, layer, depth, sc, carry):
    qk, qn = got.shape[1:]
    tr = _row_tile(qk, qn, SUM_TILE)
    ph_idx = (lambda i, s: (i, s[1])) if col else (lambda i, s: (s[1] * (qk // tr) + i, 0))
    out_shape = (depth, 2 * qk, qn) if col else (depth, qk, 2 * qn)
    out_idx = (lambda i, s: (layer, s[0] * (qk // tr) + i, 0)) if col else (lambda i, s: (layer, i, s[0]))
    in_specs = [pl.BlockSpec((tr, qn), ph_idx)] + [pl.BlockSpec((None, tr, qn), lambda i, s, j=j: (j, i, 0)) for j in range(3)]
    args = [ph, got, got, got]
    if carry is not None:
        in_specs.append(ANY)
        args.append(carry)
    return _tiled(lambda a, b, c_, d_: (a.astype(F32) + b.astype(F32) + c_.astype(F32) + d_.astype(F32),), "chip_sum",
                  (qk // tr,), in_specs, [pl.BlockSpec((None, tr, qn), out_idx)],
                  [jax.ShapeDtypeStruct(out_shape, F32)], args, sc, alias=None if carry is None else {4: 0})[0]


def _adamw_math(w, g, m, v):
    m = ADAM_B1 * m + (1.0 - ADAM_B1) * g
    v = ADAM_B2 * v + (1.0 - ADAM_B2) * (g * g)
    m_hat = m / (1.0 - ADAM_B1 ** ADAM_STEP)
    v_hat = v / (1.0 - ADAM_B2 ** ADAM_STEP)
    delta = -ADAM_LR * (m_hat / (jnp.sqrt(v_hat) + ADAM_EPS) + ADAM_WD * w)
    return delta, m, v


def _adamw_layer(w, g, m, v, layer, carry):
    k, n = w.shape[1:]
    tr = _row_tile(k, n, ADAM_TILE)
    blk = pl.BlockSpec((None, tr, n), lambda i: (layer, i, 0))
    carry = list(carry or [])
    return _tiled(lambda *a: (*_adamw_math(*a), a[1]), "adamw_layer", (k // tr,), [blk] * 4 + [ANY] * len(carry),
                  [blk] * 4, [jax.ShapeDtypeStruct(w.shape, F32)] * 4, [w, g, m, v] + carry, None,
                  alias={4 + pos: pos for pos in range(len(carry))})


def _adamw(w, g, m, v):
    shape = w.shape
    two_d = lambda a: a.reshape(-1, shape[-1])
    outs = _elementwise(_adamw_math, "adamw", [two_d(w), two_d(g), two_d(m), two_d(v)], [F32, F32, F32])
    return [o.reshape(shape) for o in outs]


ANY = pl.BlockSpec(memory_space=pl.ANY)


def _place():
    x, y, c = lax.axis_index("x"), lax.axis_index("y"), lax.axis_index("c")
    chips = [(1 - x, y), (x, 1 - y), (1 - x, 1 - y)]
    return x, y, c, chips


def _cols(ref, start, size):
    idx = (slice(None),) * (len(ref.shape) - 1) + (pl.ds(pl.multiple_of(start, GW), size),)
    return ref.at[idx]


def _rows_of(ref, start, size):
    nd = len(ref.shape)
    idx = (slice(None),) * (nd - 2) + (pl.ds(pl.multiple_of(start, 16), size), slice(None))
    return ref.at[idx]


def _region(ref, col_sharded, chip, half):
    k, n = ref.shape
    align = 16 if ref.dtype == BF else 8
    if col_sharded:
        return ref.at[pl.ds(pl.multiple_of(half * (k // 2), align), k // 2),
                      pl.ds(pl.multiple_of(chip * (n // NCHIP), GW), n // NCHIP)]
    rows = k // (2 * NCHIP)
    return ref.at[pl.ds(pl.multiple_of((2 * chip + half) * rows, align), rows), :]


def _job_gather(bufs, col_sharded, handoff):
    n = len(bufs)

    def copies(outs, send_sems, recv_sems, stage):
        x, y, c, chips = _place()
        sends, lands = [], []
        for k in range(n):
            for j, chip in enumerate(chips):
                theirs = 2 * chip[0] + chip[1]
                if stage == 0:
                    src, to = _region(outs[k], col_sharded[k], 2 * x + y, c), (*chip, c)
                    land = _region(outs[k], col_sharded[k], theirs, c)
                else:
                    src, to = _region(outs[k], col_sharded[k], theirs, c), (x, y, 1 - c)
                    land = _region(outs[k], col_sharded[k], theirs, 1 - c)
                sem = 3 * n * stage + 3 * k + j
                sems = dict(send_sem=send_sems.at[sem], recv_sem=recv_sems.at[sem], device_id=to, device_id_type=MESH)
                sends.append(pltpu.make_async_remote_copy(src_ref=src, dst_ref=src, **sems))
                lands.append(pltpu.make_async_remote_copy(src_ref=land, dst_ref=land, **sems))
        return sends, lands

    def start(ins, outs, send_sems, recv_sems):
        for cp in copies(outs, send_sems, recv_sems, 0)[0]:
            cp.start()

    def hand_on(ins, outs, send_sems, recv_sems):
        for cp in copies(outs, send_sems, recv_sems, 0)[1]:
            cp.wait_recv()
        for cp in copies(outs, send_sems, recv_sems, 1)[0]:
            cp.start()

    def finish(ins, outs, send_sems, recv_sems):
        sends, lands = copies(outs, send_sems, recv_sems, 1)
        for cp in lands:
            cp.wait_recv()
        for cp in copies(outs, send_sems, recv_sems, 0)[0] + sends:
            cp.wait_send()

    return _Job(bufs, n, [], 6 * n, [(0.0, start), (handoff, hand_on), (1.0, finish)])


def _half(ref, col_sharded, c):
    k, n = ref.shape[-2:]
    return _rows_of(ref, c * (k // 2), k // 2) if col_sharded else _cols(ref, c * (n // 2), n // 2)


def _quarter(ref, col_sharded, j):
    k, n = ref.shape[-2:]
    return _cols(ref, j * (n // NCHIP), n // NCHIP) if col_sharded else _rows_of(ref, j * (k // NCHIP), k // NCHIP)


def _job_pair(grads, col_sharded):
    n = len(grads)

    def half_shape(g, col):
        return (g.shape[0] // 2, g.shape[1]) if col else (g.shape[0], g.shape[1] // 2)

    def copies(ins, got, send_sems, recv_sems):
        x, y, c, _ = _place()
        return [pltpu.make_async_remote_copy(
            src_ref=_half(ins[k], col_sharded[k], 1 - c), dst_ref=got[k], send_sem=send_sems.at[k],
            recv_sem=recv_sems.at[k], device_id=(x, y, 1 - c), device_id_type=MESH) for k in range(n)]

    def start(*refs):
        for cp in copies(*refs):
            cp.start()

    def finish(*refs):
        for cp in copies(*refs):
            cp.wait()

    fresh = [jax.ShapeDtypeStruct(half_shape(g, col), g.dtype) for g, col in zip(grads, col_sharded)]
    return _Job(grads, 0, fresh, n, [(0.0, start), (1.0, finish)])


def _job_chip(halves, col_sharded):
    n = len(halves)

    def quarter_shape(h, col):
        return (3, h.shape[0], h.shape[1] // NCHIP) if col else (3, h.shape[0] // NCHIP, h.shape[1])

    def copies(ins, got, send_sems, recv_sems):
        x, y, c, chips = _place()
        return [pltpu.make_async_remote_copy(
            src_ref=_quarter(ins[k], col_sharded[k], 2 * chip[0] + chip[1]), dst_ref=got[k].at[j],
            send_sem=send_sems.at[3 * k + j], recv_sem=recv_sems.at[3 * k + j], device_id=(*chip, c), device_id_type=MESH)
            for k in range(n) for j, chip in enumerate(chips)]

    def start(*refs):
        for cp in copies(*refs):
            cp.start()

    def finish(*refs):
        for cp in copies(*refs):
            cp.wait()

    fresh = [jax.ShapeDtypeStruct(quarter_shape(h, col), h.dtype) for h, col in zip(halves, col_sharded)]
    return _Job(halves, 0, fresh, 3 * n, [(0.0, start), (1.0, finish)])


def _job_sibling(shards, col_sharded, layers):
    n = len(shards)

    def copies(outs, send_sems, recv_sems):
        x, y, c, _ = _place()
        sends, lands = [], []
        for k in range(n):
            sems = dict(send_sem=send_sems.at[k], recv_sem=recv_sems.at[k], device_id=(x, y, 1 - c), device_id_type=MESH)
            mine = _half(outs[k].at[layers[k]], col_sharded[k], c)
            theirs = _half(outs[k].at[layers[k]], col_sharded[k], 1 - c)
            sends.append(pltpu.make_async_remote_copy(src_ref=mine, dst_ref=mine, **sems))
            lands.append(pltpu.make_async_remote_copy(src_ref=theirs, dst_ref=theirs, **sems))
        return sends, lands

    def start(ins, outs, send_sems, recv_sems):
        for cp in copies(outs, send_sems, recv_sems)[0]:
            cp.start()

    def finish(ins, outs, send_sems, recv_sems):
        sends, lands = copies(outs, send_sems, recv_sems)
        for cp in lands:
            cp.wait_recv()
        for cp in sends:
            cp.wait_send()

    return _Job(shards, n, [], n, [(0.0, start), (1.0, finish)])


def _standalone(jobs, name):
    return _pc(jobs, lambda: None, name=name, in_specs=[], out_specs=[], out_shape=[])()[1]


def _all_reduce_small(buf, jobs=()):
    rows = buf.shape[0]
    per = rows // NDEV
    flips = [(fx, fy, fc) for fx in (0, 1) for fy in (0, 1) for fc in (0, 1)][1:]

    def body(in_ref, out_ref, got_ref, send_sems, recv_sems):
        x, y, c, _ = _place()
        me = 4 * x + 2 * y + c

        def peer(f):
            return tuple(1 - pos if flip else pos for pos, flip in zip((x, y, c), f))

        def block(ref, dev):
            return ref.at[pl.ds(pl.multiple_of(dev * per, 8), per), :]

        scatter = []
        for k, f in enumerate(flips):
            px, py, pc = peer(f)
            scatter.append(pltpu.make_async_remote_copy(
                src_ref=block(in_ref, 4 * px + 2 * py + pc), dst_ref=got_ref.at[k], send_sem=send_sems.at[k],
                recv_sem=recv_sems.at[k], device_id=(px, py, pc), device_id_type=MESH))
        for cp in scatter:
            cp.start()
        for cp in scatter:
            cp.wait()
        total = block(in_ref, me)[...]
        for k in range(len(flips)):
            total = total + got_ref[k]
        block(out_ref, me)[...] = total
        share = []
        for k, f in enumerate(flips):
            share.append(pltpu.make_async_remote_copy(
                src_ref=block(out_ref, me), dst_ref=block(out_ref, me), send_sem=send_sems.at[7 + k],
                recv_sem=recv_sems.at[7 + k], device_id=peer(f), device_id_type=MESH))
        for cp in share:
            cp.start()
        for k, f in enumerate(flips):
            share[k].wait_send()
            px, py, pc = peer(f)
            theirs = block(out_ref, 4 * px + 2 * py + pc)
            pltpu.make_async_remote_copy(
                src_ref=theirs, dst_ref=theirs, send_sem=send_sems.at[7 + k], recv_sem=recv_sems.at[7 + k],
                device_id=(px, py, pc), device_id_type=MESH).wait_recv()

    vmem = pl.BlockSpec(memory_space=pltpu.VMEM)
    return _pc(
        jobs, body, name="all_reduce_small", in_specs=[vmem], out_specs=vmem,
        out_shape=jax.ShapeDtypeStruct((rows, GW), F32),
        scratch_shapes=[pltpu.VMEM((NDEV - 1, per, GW), F32), pltpu.SemaphoreType.DMA((14,)),
                        pltpu.SemaphoreType.DMA((14,))],
    )(buf)


BIG = ("ffn1_w13", "ffn1_w2", "w_in", "w_up", "w_out", "ffn2_w13", "ffn2_w2")
BIG_COL_SHARDED = (True, False, True, True, False, True, False)
SMALL = ("ffn1_norm", "mix_norm", "pool_w", "pool_scale", "sconv_w", "cconv_w", "cconv_ln_g", "cconv_ln_b",
         "sgu_ln_g", "sgu_ln_b", "sgu_w", "sgu_b", "ffn2_norm", "final_norm")
WEIGHTS = ("ffn1_norm", "ffn1_w13", "ffn1_w2", "mix_norm", "w_in", "pool_w", "pool_scale", "sconv_w", "cconv_w",
           "cconv_ln_g", "cconv_ln_b", "sgu_ln_g", "sgu_ln_b", "sgu_w", "sgu_b", "w_up", "w_out", "ffn2_norm",
           "ffn2_w13", "ffn2_w2", "final_norm")


def _pad_rows(a, rows):
    return jnp.pad(a, ((0, 0), (0, rows - a.shape[1]), (0, 0)))


def kernel(x, ffn1_norm, ffn1_w13, ffn1_w2, mix_norm, w_in, pool_w, pool_scale, sconv_w, cconv_w, cconv_ln_g, cconv_ln_b, sgu_ln_g, sgu_ln_b, sgu_w, sgu_b, w_up, w_out, ffn2_norm, ffn2_w13, ffn2_w2, final_norm, loss_target, m_ffn1_norm, m_ffn1_w13, m_ffn1_w2, m_mix_norm, m_w_in, m_pool_w, m_pool_scale, m_sconv_w, m_cconv_w, m_cconv_ln_g, m_cconv_ln_b, m_sgu_ln_g, m_sgu_ln_b, m_sgu_w, m_sgu_b, m_w_up, m_w_out, m_ffn2_norm, m_ffn2_w13, m_ffn2_w2, m_final_norm, v_ffn1_norm, v_ffn1_w13, v_ffn1_w2, v_mix_norm, v_w_in, v_pool_w, v_pool_scale, v_sconv_w, v_cconv_w, v_cconv_ln_g, v_cconv_ln_b, v_sgu_ln_g, v_sgu_ln_b, v_sgu_w, v_sgu_b, v_w_up, v_w_out, v_ffn2_norm, v_ffn2_w13, v_ffn2_w2, v_final_norm):
    args = dict(locals())
    w = {nm: args[nm] for nm in WEIGHTS}
    m = {nm: args["m_" + nm] for nm in WEIGHTS}
    v = {nm: args["v_" + nm] for nm in WEIGHTS}
    depth = ffn1_w13.shape[0]
    chip = 2 * lax.axis_index("x") + lax.axis_index("y")

    sc = jnp.stack([lax.axis_index("c"), chip]).astype(jnp.int32)
    col_of = dict(zip(BIG, BIG_COL_SHARDED), sconv_w=True, cconv_w=True)
    sources = {nm: (w[nm], BF) for nm in BIG}
    sources["w_up"] = (w_up.reshape(depth, NBR * BW, w_up.shape[-1]), BF)
    sources["sconv_w"] = (_pad_rows(sconv_w, 2 * SKP), F32)
    sources["cconv_w"] = (_pad_rows(cconv_w, 2 * CKP), F32)
    first_group = ("ffn1_w13", "ffn1_w2")
    full = [dict() for _ in range(depth)]
    for nm in first_group:
        full[0][nm] = _cast_into(sources[nm][0], 0, col_of[nm], sources[nm][1], sc)

    def gather_job(l, names, handoff):
        return _job_gather([full[l][nm] for nm in names], [col_of[nm] for nm in names], handoff)

    def gather_with(l, names, handoff, call):
        res, job_outs = call([gather_job(l, names, handoff)] if l < depth else [])
        if l < depth:
            full[l].update(zip(names, job_outs[0]))
        return res

    rest = [(nm, l) for l in range(depth) for nm in sources if not (l == 0 and nm in first_group)]
    casted = gather_with(0, first_group, 1.0, lambda jobs: _cast_rest(
        [(sources[nm][0], l, col_of[nm], sources[nm][1]) for nm, l in rest], jobs))
    for (nm, l), arr in zip(rest, casted):
        full[l][nm] = arr

    xs = x[0]
    row = lambda a: a.reshape(1, -1)
    saved = []
    for l in range(depth):
        g1, gm, g2 = row(ffn1_norm[l]), row(mix_norm[l]), row(ffn2_norm[l])
        x1, h1, s1, ab1 = gather_with(l, ("w_in",), 0.85, lambda jobs: _ffn_fwd(
            xs, g1, full[l]["ffn1_w13"], full[l]["ffn1_w2"], jobs))
        hm, p = gather_with(l, ("w_up", "w_out", "sconv_w", "cconv_w", "ffn2_w2"), 0.75, lambda jobs: _mix_in(
            x1, gm, full[l]["w_in"], jobs))
        branch = (pool_w[l], row(pool_scale[l]), full[l]["sconv_w"][:SKP], full[l]["cconv_w"][:CKP], row(cconv_ln_g[l]),
                  row(cconv_ln_b[l]), row(sgu_ln_g[l]), row(sgu_ln_b[l]), sgu_w[l], sgu_b[l].T)
        y, conv_z = gather_with(l, ("ffn2_w13",), 0.85, lambda jobs: _mix_branches_fwd(p, *branch, jobs=jobs))
        w_up_l = full[l]["w_up"].reshape(NBR, BW, D)
        x2, merged, up = gather_with(l + 1, ("ffn1_w2",), 0.7, lambda jobs: _mix_out_fwd(
            x1, y, p, w_up_l, full[l]["w_out"], jobs))
        x3, h2, s2, ab2 = gather_with(l + 1, ("ffn1_w13",), 0.75, lambda jobs: _ffn_fwd(
            x2, g2, full[l]["ffn2_w13"], full[l]["ffn2_w2"], jobs))
        lw = dict(g1=g1, gm=gm, g2=g2, w13a=full[l]["ffn1_w13"], w2a=full[l]["ffn1_w2"], w13b=full[l]["ffn2_w13"],
                  w2b=full[l]["ffn2_w2"], w_in=full[l]["w_in"], w_up=w_up_l, w_out=full[l]["w_out"], branch=branch)
        saved.append(dict(lw=lw, x0=xs, x1=x1, x2=x2, h1=h1, s1=s1, ab1=ab1, hm=hm, p=p, y=y, z=conv_z, merged=merged, up=up, h2=h2,
                          s2=s2, ab2=ab2))
        xs = x3

    (dx, d_final, loss_part), _ = _loss_head(xs, row(final_norm), loss_target[0])
    loss = lax.psum(loss_part[0, 0], ("x", "y", "c"))

    ici_us = dict(ffn1_w13=64, ffn1_w2=32, w_in=93, w_up=23, w_out=12, ffn2_w13=64, ffn2_w2=32)
    parts, pair_sums, reduced, big_updates, pending = {}, {}, {}, {}, []

    def take_jobs(budget_us):
        chosen = []
        for task in list(pending):
            kind, (nm, _) = task
            if kind == "chip":
                if ici_us[nm] > budget_us:
                    continue
                budget_us -= ici_us[nm]
            if kind == "sib" and any(k == "sib" and key[0] == nm for k, key in chosen):
                continue
            chosen.append(task)
            pending.remove(task)
        groups, jobs = [], []
        for kind in ("pair", "sib", "chip"):
            keys = [key for k, key in chosen if k == kind]
            if not keys:
                continue
            cols = [col_of[nm] for nm, _ in keys]
            groups.append((kind, keys))
            if kind == "pair":
                jobs.append(_job_pair([parts[key] for key in keys], cols))
            elif kind == "chip":
                jobs.append(_job_chip([pair_sums[key] for key in keys], cols))
            else:
                jobs.append(_job_sibling([reduced[nm] for nm, _ in keys], cols, [layer for _, layer in keys]))
        return groups, jobs

    def settle(groups, job_outs):
        for (kind, keys), outs in zip(groups, job_outs):
            for key, out in zip(keys, outs):
                nm, layer = key
                if kind == "pair":
                    pair_sums[key] = _pair_sum(parts[key], out, col_of[nm], sc)
                    pending.append(("chip", key))
                elif kind == "chip":
                    assert not any(k == "sib" and other[0] == nm for k, other in pending)
                    reduced[nm] = _chip_sum(pair_sums[key], out, col_of[nm], layer, depth, sc, reduced.get(nm))
                    pending.append(("sib", key))
                else:
                    reduced[nm] = out
                    as3 = lambda a: a.reshape(out.shape)
                    big_updates[nm] = _adamw_layer(as3(w[nm]), out, as3(m[nm]), as3(v[nm]), layer, big_updates.get(nm))

    def run(budget_us, call, carrier=True):
        groups, jobs = take_jobs(budget_us) if carrier else ([], [])
        res, job_outs = call(jobs)
        settle(groups, job_outs)
        return res

    def wgrad_done(key, partial):
        parts[key] = partial
        pending.append(("pair", key))

    small_parts = {nm: [None] * depth for nm in SMALL if nm != "final_norm"}
    for l in reversed(range(depth)):
        sv = saved[l]
        lw = sv["lw"]
        dx, dab, dyh, dg2 = run(100, lambda jobs: _ffn_bwd(
            dx, sv["x2"], lw["g2"], sv["ab2"], lw["w13b"], lw["w2b"], jobs))
        wgrad_done(("ffn2_w13", l), run(58, lambda jobs: _wgrad(sv["h2"], dab, D, 512, "wgrad_w13", jobs)))
        wgrad_done(("ffn2_w2", l), run(33, lambda jobs: _wgrad(sv["s2"], dyh, 256, D, "wgrad_w2", jobs)))
        small_parts["ffn2_norm"][l] = dg2

        dy, dp, dup, dxb = run(70, lambda jobs: _mix_out_bwd(dx, sv["up"], sv["p"], lw["w_up"], lw["w_out"], jobs))
        wgrad_done(("w_out", l), run(16, lambda jobs: _wgrad(sv["merged"], dxb, D, 512, "wgrad_w_out", jobs), False))
        wgrad_done(("w_up", l), run(25, lambda jobs: _wgrad_groups(sv["y"], dup, BW, D, "wgrad_w_up", jobs), False))
        (dp, d_pool_w, d_pool_scale, d_sconv, d_cconv, d_clg, d_clb, d_slg, d_slb, d_sgu_w, d_sgu_b) = run(
            130, lambda jobs: _mix_branches_bwd(sv["p"], dy, sv["z"], dp, *lw["branch"], jobs=jobs))
        wgrad_done(("w_in", l), run(80, lambda jobs: _wgrad(sv["hm"], dp, D, 512, "wgrad_w_in", jobs)))
        dx, dgm = run(91, lambda jobs: _mix_in_bwd(dp, lw["w_in"], sv["x1"], lw["gm"], dx, jobs))
        small_parts["mix_norm"][l] = dgm
        small_parts["pool_w"][l] = d_pool_w
        small_parts["pool_scale"][l] = d_pool_scale
        small_parts["sconv_w"][l] = d_sconv[:SK]
        small_parts["cconv_w"][l] = d_cconv[:CK]
        small_parts["cconv_ln_g"][l] = d_clg
        small_parts["cconv_ln_b"][l] = d_clb
        small_parts["sgu_ln_g"][l] = d_slg
        small_parts["sgu_ln_b"][l] = d_slb
        small_parts["sgu_w"][l] = d_sgu_w
        small_parts["sgu_b"][l] = jnp.sum(d_sgu_b, axis=-1)

        dx, dab, dyh, dg1 = run(100, lambda jobs: _ffn_bwd(
            dx, sv["x0"], lw["g1"], sv["ab1"], lw["w13a"], lw["w2a"], jobs))
        wgrad_done(("ffn1_w2", l), run(33, lambda jobs: _wgrad(sv["s1"], dyh, 256, D, "wgrad_w2", jobs)))
        wgrad_done(("ffn1_w13", l), run(58, lambda jobs: _wgrad(sv["h1"], dab, D, 512, "wgrad_w13", jobs)))
        small_parts["ffn1_norm"][l] = dg1
    grad_x = dx[None]

    small_local = {nm: jnp.stack(parts).reshape(depth, *w[nm].shape[1:-1], -1) if nm not in ("sconv_w", "cconv_w")
                   else jnp.stack(parts) for nm, parts in small_parts.items()}
    small_local["final_norm"] = d_final.reshape(-1)
    sizes = [small_local[nm].size for nm in SMALL]
    total = sum(sizes)
    pad_to = NDEV * 8 * GW
    padded = -(-total // pad_to) * pad_to
    packed = jnp.concatenate([small_local[nm].reshape(-1) for nm in SMALL] + [jnp.zeros((padded - total,), F32)])
    groups, jobs = take_jobs(float("inf"))
    summed, job_outs = _all_reduce_small(packed.reshape(-1, GW), jobs)
    settle(groups, job_outs)
    summed = summed.reshape(-1)
    flushes = 0
    while pending:
        groups, jobs = take_jobs(float("inf"))
        settle(groups, _standalone(jobs, "grad_flush_%d" % flushes))
        flushes += 1
    big_grads = {nm: big_updates[nm][3].reshape(w[nm].shape) for nm in BIG}
    small_grads, off = {}, 0
    for nm, size in zip(SMALL, sizes):
        small_grads[nm] = summed[off:off + size].reshape(small_local[nm].shape)
        off += size
    for nm in ("sconv_w", "cconv_w"):
        small_grads[nm] = lax.dynamic_slice_in_dim(small_grads[nm], chip * GW, GW, axis=2)

    grads = {**big_grads, **small_grads}

    delta, new_m, new_v = {}, {}, {}
    for nm in BIG:
        delta[nm], new_m[nm], new_v[nm] = [a.reshape(w[nm].shape) for a in big_updates[nm][:3]]
    s_sizes = [w[nm].size for nm in SMALL]
    s_total = sum(s_sizes)
    s_padded = -(-s_total // (8 * GW)) * (8 * GW)

    def pack(tree):
        return jnp.concatenate([tree[nm].reshape(-1) for nm in SMALL] + [jnp.ones((s_padded - s_total,), F32)]).reshape(-1, GW)

    packed_out = _adamw(pack(w), pack(grads), pack(m), pack(v))
    off = 0
    for nm, size in zip(SMALL, s_sizes):
        for tree, arr in zip((delta, new_m, new_v), packed_out):
            tree[nm] = arr.reshape(-1)[off:off + size].reshape(w[nm].shape)
        off += size

    return (loss, grad_x, *[grads[nm] for nm in WEIGHTS], *[delta[nm] for nm in WEIGHTS],
            *[new_m[nm] for nm in WEIGHTS], *[new_v[nm] for nm in WEIGHTS])
```

```python
import jax
import jax.numpy as jnp
from jax import lax
from jax.experimental import pallas as pl
from jax.experimental.pallas import tpu as pltpu

D = 1024
FF = 2816
BW = 512
NBR = 4
MIXC = 4096
INC = 8192
GW = 128
CHUNK = 64
SK = 3
CK = 31
SKP = 8
CKP = 32
HALO = 32
TOKEN_TILE = 256
BRANCH_TILE = 512
EPS = 1e-6
NCHIP = 4
NDEV = 8

ADAM_LR = 0.001
ADAM_B1 = 0.9
ADAM_B2 = 0.999
ADAM_EPS = 1e-08
ADAM_WD = 0.01
ADAM_STEP = 10

VMEM_LIMIT = 56 * 1024 * 1024

BF = jnp.bfloat16
F32 = jnp.float32
MESH = pl.DeviceIdType.MESH
NT = (((1,), (1,)), ((), ()))
TN = (((0,), (0,)), ((), ()))


def _params(n_axes):
    return pltpu.CompilerParams(dimension_semantics=("arbitrary",) * n_axes, vmem_limit_bytes=VMEM_LIMIT)


class _Job:
    def __init__(self, args, n_inplace, fresh, nsem, phases):
        self.args, self.n_inplace, self.fresh, self.nsem, self.phases = list(args), n_inplace, list(fresh), nsem, phases


def _pc(jobs, body, name, in_specs, out_specs, out_shape, grid=(), scratch_shapes=(), input_output_aliases=None,
        compiler_params=None, grid_spec_scalars=None):
    single = not isinstance(out_shape, (list, tuple))
    core_out_specs = [out_specs] if single else list(out_specs)
    core_out_shape = [out_shape] if single else list(out_shape)
    n_in, n_out, n_scr = len(in_specs), len(core_out_specs), len(scratch_shapes)
    n_pre = 0 if grid_spec_scalars is None else 1
    all_in, all_out, all_shape = list(in_specs), list(core_out_specs), list(core_out_shape)
    all_scr, aliases, extra_args, layout = list(scratch_shapes), dict(input_output_aliases or {}), [], []
    for job in jobs:
        n_job_out = job.n_inplace + len(job.fresh)
        layout.append((len(all_in), len(job.args), len(all_out), n_job_out, len(all_scr)))
        for a in range(job.n_inplace):
            aliases[n_pre + len(all_in) + a] = len(all_out) + a
        all_in += [ANY] * len(job.args)
        extra_args += job.args
        all_out += [ANY] * n_job_out
        all_shape += [jax.ShapeDtypeStruct(a.shape, a.dtype) for a in job.args[:job.n_inplace]] + job.fresh
        all_scr += [pltpu.SemaphoreType.DMA((job.nsem,)), pltpu.SemaphoreType.DMA((job.nsem,))]
    steps = 1
    for extent in grid:
        steps *= extent
    events = []
    for (i0, na, o0, no, s0), job in zip(layout, jobs):
        for frac, fn in job.phases:
            events.append((min(int(frac * steps), steps - 1), frac >= 1.0, len(events), fn, (i0, na, o0, no, s0)))
    events.sort(key=lambda e: e[:3])

    def wrapped(*refs):
        pre, refs = refs[:n_pre], refs[n_pre:]
        ins, outs, scr = refs[:len(all_in)], refs[len(all_in):len(all_in) + len(all_out)], refs[len(all_in) + len(all_out):]
        step = 0
        for axis, extent in enumerate(grid):
            step = step * extent + pl.program_id(axis)

        def emit(event):
            at, _, _, fn, (i0, na, o0, no, s0) = event
            run = lambda: fn(ins[i0:i0 + na], outs[o0:o0 + no], scr[s0], scr[s0 + 1])
            if steps == 1:
                run()
            else:
                pl.when(step == at)(run)

        for event in events:
            if not event[1]:
                emit(event)
        body(*pre, *ins[:n_in], *outs[:n_out], *scr[:n_scr])
        for event in events:
            if event[1]:
                emit(event)

    kwargs = dict(name=name, out_shape=all_shape, input_output_aliases=aliases)
    if compiler_params is not None:
        kwargs["compiler_params"] = compiler_params
    if grid_spec_scalars is not None:
        kwargs["grid_spec"] = pltpu.PrefetchScalarGridSpec(
            num_scalar_prefetch=1, grid=grid, in_specs=all_in, out_specs=all_out, scratch_shapes=all_scr)
    else:
        kwargs.update(in_specs=all_in, out_specs=all_out, scratch_shapes=all_scr)
        if grid:
            kwargs["grid"] = grid
    call = pl.pallas_call(wrapped, **kwargs)

    def run_call(*args):
        pre_args = [] if grid_spec_scalars is None else [grid_spec_scalars]
        res = list(call(*pre_args, *args, *extra_args))
        core = res[0] if single else res[:n_out]
        job_outs = [res[o0:o0 + no] for (_, _, o0, no, _) in layout]
        return core, job_outs

    return run_call


def _full(shape):
    nd = len(shape)
    return pl.BlockSpec(shape, lambda *_: (0,) * nd, pipeline_mode=pl.Buffered(1))


def _rows(ts, width, col=0):
    return pl.BlockSpec((ts, width), lambda i: (i, col))


def _sig(v):
    return jax.nn.sigmoid(v)


def _rms_stats(x):
    r = lax.rsqrt(jnp.mean(x * x, axis=-1, keepdims=True) + EPS)
    return r, x * r


def _rms_bwd(x, g, dh):
    r, xh = _rms_stats(x)
    dg = jnp.sum(dh * xh, axis=0, keepdims=True)
    dxh = dh * g
    dx = r * (dxh - xh * jnp.mean(dxh * xh, axis=-1, keepdims=True))
    return dx, dg


def _accumulate(ref, val, first):
    @pl.when(first)
    def _():
        ref[...] = val

    @pl.when(jnp.logical_not(first))
    def _():
        ref[...] += val


def _ffn_fwd(x, g, w13, w2, jobs=()):
    s_len = x.shape[0]
    ts = min(TOKEN_TILE, s_len)

    def body(x_ref, g_ref, w13_ref, w2_ref, xo_ref, h_ref, s_ref, dsab_ref):
        xv = x_ref[...]
        r, xh = _rms_stats(xv)
        h = (xh * g_ref[...]).astype(BF)
        h_ref[...] = h
        ab = jnp.dot(h, w13_ref[...], preferred_element_type=F32)
        a = ab[:, :FF]
        b = ab[:, FF:]
        sg = _sig(a)
        sil = a * sg
        dsab_ref[:, :FF] = (b * (sg * (1.0 + a * (1.0 - sg)))).astype(BF)
        dsab_ref[:, FF:] = sil.astype(BF)
        s = (sil * b).astype(BF)
        s_ref[...] = s
        xo_ref[...] = xv + 0.5 * jnp.dot(s, w2_ref[...], preferred_element_type=F32)

    return _pc(
        jobs, body, name="ffn_fwd", grid=(s_len // ts,),
        in_specs=[_rows(ts, D), _full((1, D)), _full((D, 2 * FF)), _full((FF, D))],
        out_specs=[_rows(ts, D), _rows(ts, D), _rows(ts, FF), _rows(ts, 2 * FF)],
        out_shape=[jax.ShapeDtypeStruct((s_len, D), F32), jax.ShapeDtypeStruct((s_len, D), BF),
                   jax.ShapeDtypeStruct((s_len, FF), BF), jax.ShapeDtypeStruct((s_len, 2 * FF), BF)],
        compiler_params=_params(1),
    )(x, g, w13, w2)


def _ffn_bwd(dxo, x, g, dsab, w13, w2, jobs=()):
    s_len = x.shape[0]
    ts = min(TOKEN_TILE, s_len)

    def body(dxo_ref, x_ref, g_ref, dsab_ref, w13_ref, w2_ref, dxi_ref, dab_ref, dy_ref, dg_ref):
        i = pl.program_id(0)
        dxo_v = dxo_ref[...]
        dy = (0.5 * dxo_v).astype(BF)
        dy_ref[...] = dy
        ds = lax.dot_general(dy, w2_ref[...], NT, preferred_element_type=F32)
        dab_ref[:, :FF] = (ds * dsab_ref[:, :FF].astype(F32)).astype(BF)
        dab_ref[:, FF:] = (ds * dsab_ref[:, FF:].astype(F32)).astype(BF)
        dh = lax.dot_general(dab_ref[...], w13_ref[...], NT, preferred_element_type=F32)
        dx, dg = _rms_bwd(x_ref[...], g_ref[...], dh)
        dxi_ref[...] = dxo_v + dx
        _accumulate(dg_ref, dg, i == 0)

    return _pc(
        jobs, body, name="ffn_bwd", grid=(s_len // ts,),
        in_specs=[_rows(ts, D), _rows(ts, D), _full((1, D)), _rows(ts, 2 * FF), _full((D, 2 * FF)), _full((FF, D))],
        out_specs=[_rows(ts, D), _rows(ts, 2 * FF), _rows(ts, D), pl.BlockSpec((1, D), lambda i: (0, 0))],
        out_shape=[jax.ShapeDtypeStruct((s_len, D), F32), jax.ShapeDtypeStruct((s_len, 2 * FF), BF),
                   jax.ShapeDtypeStruct((s_len, D), BF), jax.ShapeDtypeStruct((1, D), F32)],
        compiler_params=_params(1),
    )(dxo, x, g, dsab, w13, w2)


def _wgrad(a, b, tk, tn, name, jobs=()):
    s_len, k = a.shape
    n = b.shape[1]

    def body(a_ref, b_ref, o_ref):
        o_ref[...] = lax.dot_general(a_ref[...], b_ref[...], TN, preferred_element_type=F32).astype(BF)

    return _pc(
        jobs, body, name=name, grid=(k // tk, n // tn),
        in_specs=[pl.BlockSpec((s_len, tk), lambda i, j: (0, i)), pl.BlockSpec((s_len, tn), lambda i, j: (0, j))],
        out_specs=pl.BlockSpec((tk, tn), lambda i, j: (i, j)),
        out_shape=jax.ShapeDtypeStruct((k, n), BF),
        compiler_params=_params(2),
    )(a, b)


def _wgrad_groups(a, b, ka, nb, name, jobs=()):
    s_len = a.shape[0]
    groups = a.shape[1] // ka

    def body(a_ref, b_ref, o_ref):
        o_ref[...] = lax.dot_general(a_ref[...], b_ref[...], TN, preferred_element_type=F32).astype(BF)

    return _pc(
        jobs, body, name=name, grid=(groups,),
        in_specs=[pl.BlockSpec((s_len, ka), lambda gi: (0, gi)), pl.BlockSpec((s_len, nb), lambda gi: (0, gi))],
        out_specs=pl.BlockSpec((ka, nb), lambda gi: (gi, 0)),
        out_shape=jax.ShapeDtypeStruct((groups * ka, nb), BF),
        compiler_params=_params(1),
    )(a, b)


def _mix_in(x, g, w_in, jobs=()):
    s_len = x.shape[0]
    ts = min(TOKEN_TILE, s_len)

    def body(x_ref, g_ref, w_ref, h_ref, p_ref):
        _, xh = _rms_stats(x_ref[...])
        h = (xh * g_ref[...]).astype(BF)
        h_ref[...] = h
        p_ref[...] = jnp.dot(h, w_ref[...], preferred_element_type=F32).astype(BF)

    return _pc(
        jobs, body, name="mix_in", grid=(s_len // ts,),
        in_specs=[_rows(ts, D), _full((1, D)), _full((D, INC))],
        out_specs=[_rows(ts, D), _rows(ts, INC)],
        out_shape=[jax.ShapeDtypeStruct((s_len, D), BF), jax.ShapeDtypeStruct((s_len, INC), BF)],
        compiler_params=_params(1),
    )(x, g, w_in)


def _shift(e, j):
    n = e.shape[0]
    j = j % n
    return e if j == 0 else pltpu.roll(e, j, 0)


def _ln_stats(z):
    mu = jnp.mean(z, axis=-1, keepdims=True)
    zc = z - mu
    rs = lax.rsqrt(jnp.mean(zc * zc, axis=-1, keepdims=True) + EPS)
    return rs, zc * rs


def _ln_bwd(rs, zn, dzn):
    return rs * (dzn - jnp.mean(dzn, axis=-1, keepdims=True) - zn * jnp.mean(dzn * zn, axis=-1, keepdims=True))


_GELU_C0 = 0.7978845608028654
_GELU_C1 = 0.044715


def _gelu(p):
    th = jnp.tanh(_GELU_C0 * (p + _GELU_C1 * p * p * p))
    return 0.5 * p * (1.0 + th), th


def _gelu_grad(p, th):
    return 0.5 * (1.0 + th) + 0.5 * p * (1.0 - th * th) * (_GELU_C0 * (1.0 + 3.0 * _GELU_C1 * p * p))


def _pool_diff(a, t, sign):
    outs = []
    for gi in range(NBR):
        win = 2 ** (gi + 1)
        ag = a[:, gi * GW:(gi + 1) * GW]
        cnt = jnp.clip(t + 1, 1, win).astype(F32)
        ws = ag if sign > 0 else ag / cnt
        for s in range(gi + 1):
            ws = ws + _shift(ws, sign * (2 ** s))
        outs.append((ws / cnt if sign > 0 else ws) - ag)
    return outs


def _sgu_mask():
    row = lax.broadcasted_iota(jnp.int32, (GW, GW), 0)
    col = lax.broadcasted_iota(jnp.int32, (GW, GW), 1)
    return (col // CHUNK) <= (row // CHUNK)


def _assemble(pe_ref, prev_ref, cur_ref, next_ref, i, last, ts):
    pe_ref[0:HALO, :] = jnp.where(i > 0, prev_ref[...], jnp.zeros_like(prev_ref))
    pe_ref[HALO:HALO + ts, :] = cur_ref[...]
    if next_ref is not None:
        pe_ref[HALO + ts:, :] = jnp.where(i < last, next_ref[...], jnp.zeros_like(next_ref))


def _halo_specs(ts, width, s_len, with_next):
    per = ts // HALO
    specs = [pl.BlockSpec((HALO, width), lambda i: (jnp.maximum(i * per - 1, 0), 0)),
             pl.BlockSpec((ts, width), lambda i: (i, 0))]
    if with_next:
        specs.append(pl.BlockSpec((HALO, width), lambda i: (jnp.minimum((i + 1) * per, s_len // HALO - 1), 0)))
    return specs


def _branch_weights_specs():
    return [_full((NBR, GW, GW)), _full((1, BW)), _full((SKP, BW)), _full((CKP, BW)), _full((1, BW)), _full((1, BW)),
            _full((1, BW)), _full((1, BW)), _full((NBR, GW, GW)), _full((GW, NBR))]


def _mix_branches_fwd(p, pool_w, pool_scale, sconv_w, cconv_w, cln_g, cln_b, sln_g, sln_b, sgu_w, sgu_bt, jobs=()):
    s_len = p.shape[0]
    ts = min(BRANCH_TILE, s_len)
    ext = HALO + ts

    def body(pp_ref, pc_ref, pw_ref, ps_ref, sw_ref, cw_ref, clg_ref, clb_ref, slg_ref, slb_ref, gw_ref, gb_ref,
             y_ref, z_ref, pe_ref):
        i = pl.program_id(0)
        _assemble(pe_ref, pp_ref, pc_ref, None, i, 0, ts)
        t = i * ts - HALO + lax.broadcasted_iota(jnp.int32, (ext, 1), 0)

        dgs = _pool_diff(pe_ref[:, 0:BW].astype(F32), t, 1)
        for gi in range(NBR):
            e = jnp.dot(dgs[gi][HALO:].astype(BF), pw_ref[gi].astype(BF), preferred_element_type=F32)
            y_ref[:, gi * GW:(gi + 1) * GW] = (e * ps_ref[:, gi * GW:(gi + 1) * GW]).astype(BF)

        xin = pe_ref[:, BW:2 * BW].astype(F32)
        cg = pe_ref[:, 3 * BW:4 * BW].astype(F32)
        q = cg * xin
        cv = sw_ref[2:3, :] * q + sw_ref[1:2, :] * _shift(q, 1) + sw_ref[0:1, :] * _shift(q, 2)
        y_ref[:, BW:2 * BW] = (pe_ref[HALO:, 2 * BW:3 * BW].astype(F32) * cv[HALO:]).astype(BF)

        yg = pe_ref[:, 4 * BW:5 * BW].astype(F32) * _sig(pe_ref[:, 5 * BW:6 * BW].astype(F32))
        z = cw_ref[CK - 1:CK, :] * yg
        for j in range(1, CK):
            z = z + cw_ref[CK - 1 - j:CK - j, :] * _shift(yg, j)
        z_ref[...] = z[HALO:].astype(BF)
        _, zn = _ln_stats(z[HALO:])
        nn = zn * clg_ref[...] + clb_ref[...]
        y_ref[:, 2 * BW:3 * BW] = (nn * _sig(nn)).astype(BF)

        u, _ = _gelu(pc_ref[:, 6 * BW:7 * BW].astype(F32))
        v, _ = _gelu(pc_ref[:, 7 * BW:8 * BW].astype(F32))
        _, vn = _ln_stats(v)
        vn = (vn * slg_ref[...] + slb_ref[...]).astype(BF)
        mask = _sgu_mask()
        for hd in range(NBR):
            wm = jnp.where(mask, gw_ref[hd], 0.0).astype(BF)
            for blk in range(ts // GW):
                rows = slice(blk * GW, (blk + 1) * GW)
                cols = slice(hd * GW, (hd + 1) * GW)
                zz = jnp.dot(wm, vn[rows, cols], preferred_element_type=F32) + gb_ref[:, hd:hd + 1]
                y_ref[rows, 3 * BW + hd * GW:3 * BW + (hd + 1) * GW] = (u[rows, cols] * zz).astype(BF)

    return _pc(
        jobs, body, name="mix_branches_fwd", grid=(s_len // ts,),
        in_specs=_halo_specs(ts, MIXC, s_len, False) + _branch_weights_specs(),
        out_specs=[_rows(ts, NBR * BW), _rows(ts, BW)],
        out_shape=[jax.ShapeDtypeStruct((s_len, NBR * BW), BF), jax.ShapeDtypeStruct((s_len, BW), BF)],
        scratch_shapes=[pltpu.VMEM((ext, MIXC), BF)],
        compiler_params=_params(1),
    )(p, p, pool_w, pool_scale, sconv_w, cconv_w, cln_g, cln_b, sln_g, sln_b, sgu_w, sgu_bt)


def _mix_branches_bwd(p, dy, z, dp, pool_w, pool_scale, sconv_w, cconv_w, cln_g, cln_b, sln_g, sln_b, sgu_w, sgu_bt,
                      jobs=()):
    s_len = p.shape[0]
    ts = min(BRANCH_TILE, s_len)
    ext = ts + 2 * HALO
    last = s_len // ts - 1
    tile = slice(HALO, HALO + ts)
    small_shapes = [(NBR, GW, GW), (1, BW), (SKP, BW), (CKP, BW), (1, BW), (1, BW), (1, BW), (1, BW), (NBR, GW, GW),
                    (NBR, GW, GW)]

    def body(pp_ref, pc_ref, pn_ref, dyp_ref, dyc_ref, dyn_ref, zc_ref, zn_ref, dpin_ref,
             pw_ref, ps_ref, sw_ref, cw_ref, clg_ref, clb_ref, slg_ref, slb_ref, gw_ref, gb_ref,
             dp_ref, dpw_ref, dps_ref, dsw_ref, dcw_ref, dclg_ref, dclb_ref, dslg_ref, dslb_ref, dgw_ref, dgb_ref,
             pe_ref, de_ref):
        del dyp_ref, dpin_ref
        i = pl.program_id(0)
        first = i == 0
        _assemble(pe_ref, pp_ref, pc_ref, pn_ref, i, last, ts)
        de_ref[0:HALO, :] = jnp.zeros((HALO, NBR * BW), BF)
        de_ref[HALO:HALO + ts, :] = dyc_ref[...]
        de_ref[HALO + ts:, :] = jnp.where(i < last, dyn_ref[...], jnp.zeros_like(dyn_ref))
        t = i * ts - HALO + lax.broadcasted_iota(jnp.int32, (ext, 1), 0)

        @pl.when(first)
        def _():
            dsw_ref[...] = jnp.zeros((SKP, BW), F32)
            dcw_ref[...] = jnp.zeros((CKP, BW), F32)

        dgs = _pool_diff(pe_ref[:, 0:BW].astype(F32), t, 1)
        dya = de_ref[:, 0:BW].astype(F32)
        dds = []
        for gi in range(NBR):
            cols = slice(gi * GW, (gi + 1) * GW)
            pw = pw_ref[gi].astype(BF)
            d_t = dgs[gi][tile].astype(BF)
            e = jnp.dot(d_t, pw, preferred_element_type=F32)
            _accumulate(dps_ref.at[:, cols], jnp.sum(dya[tile, cols] * e, axis=0, keepdims=True), first)
            de_g = (dya[:, cols] * ps_ref[:, cols]).astype(BF)
            _accumulate(dpw_ref.at[gi], lax.dot_general(d_t, de_g[tile], TN, preferred_element_type=F32), first)
            dds.append(lax.dot_general(de_g, pw, NT, preferred_element_type=F32))
        das = _pool_diff(jnp.concatenate(dds, axis=1), t, -1)
        for gi in range(NBR):
            dp_ref[:, gi * GW:(gi + 1) * GW] = das[gi][tile].astype(BF)

        xin = pe_ref[:, BW:2 * BW].astype(F32)
        bg = pe_ref[:, 2 * BW:3 * BW].astype(F32)
        cg = pe_ref[:, 3 * BW:4 * BW].astype(F32)
        q = cg * xin
        qs = [q, _shift(q, 1), _shift(q, 2)]
        cv = sw_ref[2:3, :] * qs[0] + sw_ref[1:2, :] * qs[1] + sw_ref[0:1, :] * qs[2]
        dyb = de_ref[:, BW:2 * BW].astype(F32)
        dcv = dyb * bg
        for j in range(SK):
            dsw_ref[SK - 1 - j:SK - j, :] += jnp.sum(dcv[tile] * qs[j][tile], axis=0, keepdims=True)
        dq = sw_ref[2:3, :] * dcv + sw_ref[1:2, :] * _shift(dcv, -1) + sw_ref[0:1, :] * _shift(dcv, -2)
        dp_ref[:, BW:2 * BW] = (dq * cg)[tile].astype(BF)
        dp_ref[:, 2 * BW:3 * BW] = (dyb * cv)[tile].astype(BF)
        dp_ref[:, 3 * BW:4 * BW] = (dq * xin)[tile].astype(BF)

        ca = pc_ref[:, 4 * BW:5 * BW].astype(F32)
        sb = _sig(pc_ref[:, 5 * BW:6 * BW].astype(F32))
        yg_t = ca * sb
        z = jnp.concatenate([zc_ref[...].astype(F32), zn_ref[...].astype(F32)], axis=0)
        rs, zn = _ln_stats(z)
        nn = zn * clg_ref[...] + clb_ref[...]
        sn = _sig(nn)
        dn = de_ref[HALO:, 2 * BW:3 * BW].astype(F32) * (sn * (1.0 + nn * (1.0 - sn)))
        _accumulate(dclg_ref, jnp.sum((dn * zn)[:ts], axis=0, keepdims=True), first)
        _accumulate(dclb_ref, jnp.sum(dn[:ts], axis=0, keepdims=True), first)
        dz = _ln_bwd(rs, zn, dn * clg_ref[...])
        dyg = cw_ref[CK - 1:CK, :] * dz[:ts]
        dcw_ref[CK - 1:CK, :] += jnp.sum(dz[:ts] * yg_t, axis=0, keepdims=True)
        for j in range(1, CK):
            dz_ahead = _shift(dz, -j)[:ts]
            dyg = dyg + cw_ref[CK - 1 - j:CK - j, :] * dz_ahead
            dcw_ref[CK - 1 - j:CK - j, :] += jnp.sum(dz_ahead * yg_t, axis=0, keepdims=True)
        dp_ref[:, 4 * BW:5 * BW] = (dyg * sb).astype(BF)
        dp_ref[:, 5 * BW:6 * BW] = (dyg * ca * sb * (1.0 - sb)).astype(BF)

        pu = pc_ref[:, 6 * BW:7 * BW].astype(F32)
        pv = pc_ref[:, 7 * BW:8 * BW].astype(F32)
        u, thu = _gelu(pu)
        v, thv = _gelu(pv)
        vrs, vn0 = _ln_stats(v)
        vn = (vn0 * slg_ref[...] + slb_ref[...]).astype(BF)
        dyd = dyc_ref[:, 3 * BW:4 * BW].astype(F32)
        dzz = dyd * u
        dzb = dzz.astype(BF)
        mask = _sgu_mask()
        dvn_cols = []
        for hd in range(NBR):
            cols = slice(hd * GW, (hd + 1) * GW)
            wm = jnp.where(mask, gw_ref[hd], 0.0).astype(BF)
            dwm = jnp.zeros((GW, GW), F32)
            dbs = jnp.zeros((GW, GW), F32)
            dvn_rows = []
            for blk in range(ts // GW):
                rows = slice(blk * GW, (blk + 1) * GW)
                zz = jnp.dot(wm, vn[rows, cols], preferred_element_type=F32) + gb_ref[:, hd:hd + 1]
                dp_ref[rows, 6 * BW + hd * GW:6 * BW + (hd + 1) * GW] = (
                    dyd[rows, cols] * zz * _gelu_grad(pu[rows, cols], thu[rows, cols])).astype(BF)
                dwm = dwm + lax.dot_general(dzb[rows, cols], vn[rows, cols], NT, preferred_element_type=F32)
                dbs = dbs + dzz[rows, cols]
                dvn_rows.append(lax.dot_general(wm, dzb[rows, cols], TN, preferred_element_type=F32))
            _accumulate(dgw_ref.at[hd], jnp.where(mask, dwm, 0.0), first)
            _accumulate(dgb_ref.at[hd], dbs, first)
            dvn_cols.append(jnp.concatenate(dvn_rows, axis=0))
        dvn = jnp.concatenate(dvn_cols, axis=1)
        _accumulate(dslg_ref, jnp.sum(dvn * vn0, axis=0, keepdims=True), first)
        _accumulate(dslb_ref, jnp.sum(dvn, axis=0, keepdims=True), first)
        dv = _ln_bwd(vrs, vn0, dvn * slg_ref[...])
        dp_ref[:, 7 * BW:8 * BW] = (dv * _gelu_grad(pv, thv)).astype(BF)

    const = lambda shp: pl.BlockSpec(shp, lambda i: (0,) * len(shp))
    return _pc(
        jobs, body, name="mix_branches_bwd", grid=(s_len // ts,),
        in_specs=(_halo_specs(ts, MIXC, s_len, True) + _halo_specs(ts, NBR * BW, s_len, True)
                  + _halo_specs(ts, BW, s_len, True)[1:] + [pl.BlockSpec(memory_space=pl.ANY)] + _branch_weights_specs()),
        out_specs=[pl.BlockSpec((ts, MIXC), lambda i: (i, 0))] + [const(s) for s in small_shapes],
        out_shape=[jax.ShapeDtypeStruct((s_len, INC), BF)] + [jax.ShapeDtypeStruct(s, F32) for s in small_shapes],
        scratch_shapes=[pltpu.VMEM((ext, MIXC), BF), pltpu.VMEM((ext, NBR * BW), BF)],
        input_output_aliases={8: 0},
        compiler_params=_params(1),
    )(p, p, p, dy, dy, dy, z, z, dp, pool_w, pool_scale, sconv_w, cconv_w, cln_g, cln_b, sln_g, sln_b, sgu_w, sgu_bt)


def _mix_out_fwd(x, y, p, w_up, w_out, jobs=()):
    s_len = x.shape[0]
    ts = min(TOKEN_TILE, s_len)

    def body(x_ref, y_ref, pg_ref, wu_ref, wo_ref, xo_ref, m_ref, up_ref):
        m = jnp.zeros((ts, D), F32)
        for gi in range(NBR):
            up = jnp.dot(y_ref[:, gi * BW:(gi + 1) * BW], wu_ref[gi], preferred_element_type=F32)
            up_ref[:, gi * D:(gi + 1) * D] = up.astype(BF)
            m = m + _sig(pg_ref[:, gi * D:(gi + 1) * D].astype(F32)) * up
        mb = m.astype(BF)
        m_ref[...] = mb
        xo_ref[...] = x_ref[...] + jnp.dot(mb, wo_ref[...], preferred_element_type=F32)

    return _pc(
        jobs, body, name="mix_out_fwd", grid=(s_len // ts,),
        in_specs=[_rows(ts, D), _rows(ts, NBR * BW), _rows(ts, NBR * D, 1), _full((NBR, BW, D)), _full((D, D))],
        out_specs=[_rows(ts, D), _rows(ts, D), _rows(ts, NBR * D)],
        out_shape=[jax.ShapeDtypeStruct((s_len, D), F32), jax.ShapeDtypeStruct((s_len, D), BF),
                   jax.ShapeDtypeStruct((s_len, NBR * D), BF)],
        compiler_params=_params(1),
    )(x, y, p, w_up, w_out)


def _mix_out_bwd(dxo, up, p, w_up, w_out, jobs=()):
    s_len = dxo.shape[0]
    ts = min(TOKEN_TILE, s_len)

    def body(dxo_ref, up_ref, pg_ref, wu_ref, wo_ref, dy_ref, dp_ref, dup_ref, dxb_ref):
        dxb = dxo_ref[...].astype(BF)
        dxb_ref[...] = dxb
        dm = lax.dot_general(dxb, wo_ref[...], NT, preferred_element_type=F32)
        for gi in range(NBR):
            cols = slice(gi * D, (gi + 1) * D)
            gate = _sig(pg_ref[:, cols].astype(F32))
            dp_ref[:, cols] = (dm * up_ref[:, cols].astype(F32) * gate * (1.0 - gate)).astype(BF)
            dup = (dm * gate).astype(BF)
            dup_ref[:, cols] = dup
            dy_ref[:, gi * BW:(gi + 1) * BW] = lax.dot_general(
                dup, wu_ref[gi], NT, preferred_element_type=F32).astype(BF)

    return _pc(
        jobs, body, name="mix_out_bwd", grid=(s_len // ts,),
        in_specs=[_rows(ts, D), _rows(ts, NBR * D), _rows(ts, NBR * D, 1), _full((NBR, BW, D)), _full((D, D))],
        out_specs=[_rows(ts, NBR * BW), _rows(ts, NBR * D, 1), _rows(ts, NBR * D), _rows(ts, D)],
        out_shape=[jax.ShapeDtypeStruct((s_len, NBR * BW), BF), jax.ShapeDtypeStruct((s_len, INC), BF),
                   jax.ShapeDtypeStruct((s_len, NBR * D), BF), jax.ShapeDtypeStruct((s_len, D), BF)],
        compiler_params=_params(1),
    )(dxo, up, p, w_up, w_out)


def _mix_in_bwd(dp, w_in, x, g, dxo, jobs=()):
    s_len = x.shape[0]
    ts = min(TOKEN_TILE, s_len)

    def body(dp_ref, w_ref, x_ref, g_ref, dxo_ref, dxi_ref, dg_ref):
        i = pl.program_id(0)
        dh = lax.dot_general(dp_ref[...], w_ref[...], NT, preferred_element_type=F32)
        dx, dg = _rms_bwd(x_ref[...], g_ref[...], dh)
        dxi_ref[...] = dxo_ref[...] + dx
        _accumulate(dg_ref, dg, i == 0)

    return _pc(
        jobs, body, name="mix_in_bwd", grid=(s_len // ts,),
        in_specs=[_rows(ts, INC), _full((D, INC)), _rows(ts, D), _full((1, D)), _rows(ts, D)],
        out_specs=[_rows(ts, D), pl.BlockSpec((1, D), lambda i: (0, 0))],
        out_shape=[jax.ShapeDtypeStruct((s_len, D), F32), jax.ShapeDtypeStruct((1, D), F32)],
        compiler_params=_params(1),
    )(dp, w_in, x, g, dxo)


def _loss_head(x, g, target, jobs=()):
    s_len = x.shape[0]
    ts = min(512, s_len)

    def body(x_ref, g_ref, t_ref, dx_ref, dg_ref, loss_ref):
        i = pl.program_id(0)
        xv = x_ref[...]
        _, xh = _rms_stats(xv)
        err = xh * g_ref[...] - t_ref[...]
        part = 0.5 * jnp.sum(jnp.mean(err * err, axis=-1, keepdims=True), axis=0, keepdims=True)
        dx, dg = _rms_bwd(xv, g_ref[...], err * (1.0 / D))
        dx_ref[...] = dx
        _accumulate(dg_ref, dg, i == 0)
        _accumulate(loss_ref, jnp.broadcast_to(part, (1, GW)), i == 0)

    return _pc(
        jobs, body, name="loss_head", grid=(s_len // ts,),
        in_specs=[_rows(ts, D), _full((1, D)), _rows(ts, D)],
        out_specs=[_rows(ts, D), pl.BlockSpec((1, D), lambda i: (0, 0)), pl.BlockSpec((1, GW), lambda i: (0, 0))],
        out_shape=[jax.ShapeDtypeStruct((s_len, D), F32), jax.ShapeDtypeStruct((1, D), F32),
                   jax.ShapeDtypeStruct((1, GW), F32)],
        compiler_params=_params(1),
    )(x, g, target)


SUM_TILE = 1 << 20
ADAM_TILE = 1 << 19
CAST_TILE = 1 << 19


def _row_tile(rows, cols, budget=1 << 18):
    tr = rows
    while tr * cols > budget and tr % 16 == 0:
        tr //= 2
    return tr


def _elementwise(fn, name, ins, out_dtypes):
    rows, cols = ins[0].shape
    tr = _row_tile(rows, cols)
    n_in = len(ins)

    def body(*refs):
        res = fn(*[r[...] for r in refs[:n_in]])
        for o_ref, val in zip(refs[n_in:], res):
            o_ref[...] = val.astype(o_ref.dtype)

    outs = pl.pallas_call(
        body, name=name, grid=(rows // tr,),
        in_specs=[_rows(tr, cols)] * n_in, out_specs=[_rows(tr, cols)] * len(out_dtypes),
        out_shape=[jax.ShapeDtypeStruct((rows, cols), dt) for dt in out_dtypes],
        compiler_params=_params(1),
    )(*ins)
    return outs


def _tiled(fn, name, grid, in_specs, out_specs, out_shape, args, scalars=None, alias=None):
    alias = alias or {}
    n_in = len(in_specs) - len(alias)
    n_pre = 0 if scalars is None else 1

    def body(*refs):
        refs = refs[n_pre:]
        res = fn(*[r[...] for r in refs[:n_in]])
        for o_ref, val in zip(refs[len(in_specs):], res):
            o_ref[...] = val.astype(o_ref.dtype)

    aliases = {n_pre + pos: out for pos, out in alias.items()}
    if scalars is None:
        return pl.pallas_call(body, name=name, grid=grid, in_specs=in_specs, out_specs=out_specs, out_shape=out_shape,
                              input_output_aliases=aliases, compiler_params=_params(len(grid)))(*args)
    spec = pltpu.PrefetchScalarGridSpec(num_scalar_prefetch=1, grid=grid, in_specs=in_specs, out_specs=out_specs)
    return pl.pallas_call(body, name=name, grid_spec=spec, out_shape=out_shape, input_output_aliases=aliases,
                          compiler_params=_params(len(grid)))(scalars, *args)


def _cast_into(shard, layer, col, dtype, sc):
    ks, ns = shard.shape[1:]
    tr = _row_tile(ks, ns, SUM_TILE)
    full = (ks, ns * NCHIP) if col else (ks * NCHIP, ns)
    out_idx = (lambda i, s: (i, s[1])) if col else (lambda i, s: (s[1] * (ks // tr) + i, 0))
    return _tiled(lambda v: (v,), "cast_into", (ks // tr,), [pl.BlockSpec((None, tr, ns), lambda i, s: (layer, i, 0))],
                  [pl.BlockSpec((tr, ns), out_idx)], [jax.ShapeDtypeStruct(full, dtype)], [shard], sc)[0]


def _cast_rest(items, jobs=()):
    n = len(items)
    tiles, slot_sets, counts = [], {}, {}
    for k, (shard, _, _, dt) in enumerate(items):
        ks, ns = shard.shape[1:]
        tr = _row_tile(ks, ns, CAST_TILE)
        which = slot_sets.setdefault((tr, ns, shard.dtype, dt), len(slot_sets))
        for t in range(ks // tr):
            tiles.append((k, t, tr, which, counts.get(which, 0) % 2))
            counts[which] = counts.get(which, 0) + 1
    n_sets = len(slot_sets)

    def body(*refs):
        ins, outs = refs[:n], refs[n:2 * n]
        bufs = refs[2 * n:2 * n + 2 * n_sets]
        in_sems, out_sems = refs[2 * n + 2 * n_sets:]
        chip = 2 * lax.axis_index("x") + lax.axis_index("y")
        fetch, store = [], []
        for k, t, tr, which, slot in tiles:
            shard, layer, col, _ = items[k]
            ks, ns = shard.shape[1:]
            if col:
                place = outs[k].at[pl.ds(t * tr, tr), pl.ds(pl.multiple_of(chip * ns, GW), ns)]
            else:
                place = outs[k].at[pl.ds(pl.multiple_of(chip * ks + t * tr, 16), tr), :]
            fetch.append(pltpu.make_async_copy(
                ins[k].at[layer, pl.ds(t * tr, tr), :], bufs[2 * which].at[slot], in_sems.at[which, slot]))
            store.append(pltpu.make_async_copy(bufs[2 * which + 1].at[slot], place, out_sems.at[which, slot]))
        busy = {}
        fetch[0].start()
        for i, (k, t, tr, which, slot) in enumerate(tiles):
            if i + 1 < len(tiles):
                fetch[i + 1].start()
            fetch[i].wait()
            if (which, slot) in busy:
                store[busy[which, slot]].wait()
            bufs[2 * which + 1][slot] = bufs[2 * which][slot].astype(bufs[2 * which + 1].dtype)
            store[i].start()
            busy[which, slot] = i
        for i in busy.values():
            store[i].wait()

    def full_shape(shard, col):
        ks, ns = shard.shape[1:]
        return (ks, ns * NCHIP) if col else (ks * NCHIP, ns)

    scratch = []
    for (tr, ns, dt_in, dt_out) in slot_sets:
        scratch += [pltpu.VMEM((2, tr, ns), dt_in), pltpu.VMEM((2, tr, ns), dt_out)]
    scratch += [pltpu.SemaphoreType.DMA((n_sets, 2)), pltpu.SemaphoreType.DMA((n_sets, 2))]
    return _pc(
        jobs, body, name="cast_rest", in_specs=[ANY] * n, out_specs=[ANY] * n,
        out_shape=[jax.ShapeDtypeStruct(full_shape(shard, col), dt) for shard, _, col, dt in items],
        scratch_shapes=scratch, compiler_params=pltpu.CompilerParams(vmem_limit_bytes=VMEM_LIMIT),
    )(*[shard for shard, _, _, _ in items])


def _pair_sum(g, got, col, sc):
    hk, hn = got.shape
    tr = _row_tile(hk, hn, SUM_TILE)
    g_idx = (lambda i, s: (s[0] * (hk // tr) + i, 0)) if col else (lambda i, s: (i, s[0]))
    plain = pl.BlockSpec((tr, hn), lambda i, s: (i, 0))
    return _tiled(lambda a, b: (a.astype(F32) + b.astype(F32),), "pair_sum", (hk // tr,),
                  [pl.BlockSpec((tr, hn), g_idx), plain], [plain], [jax.ShapeDtypeStruct((hk, hn), BF)], [g, got], sc)[0]


def _chip_sum(ph, got, col, layer, depth, sc, carry):
    qk, qn = got.shape[1:]
    tr = _row_tile(qk, qn, SUM_TILE)
    ph_idx = (lambda i, s: (i, s[1])) if col else (lambda i, s: (s[1] * (qk // tr) + i, 0))
    out_shape = (depth, 2 * qk, qn) if col else (depth, qk, 2 * qn)
    out_idx = (lambda i, s: (layer, s[0] * (qk // tr) + i, 0)) if col else (lambda i, s: (layer, i, s[0]))
    in_specs = [pl.BlockSpec((tr, qn), ph_idx)] + [pl.BlockSpec((None, tr, qn), lambda i, s, j=j: (j, i, 0)) for j in range(3)]
    args = [ph, got, got, got]
    if carry is not None:
        in_specs.append(ANY)
        args.append(carry)
    return _tiled(lambda a, b, c_, d_: (a.astype(F32) + b.astype(F32) + c_.astype(F32) + d_.astype(F32),), "chip_sum",
                  (qk // tr,), in_specs, [pl.BlockSpec((None, tr, qn), out_idx)],
                  [jax.ShapeDtypeStruct(out_shape, F32)], args, sc, alias=None if carry is None else {4: 0})[0]


def _adamw_math(w, g, m, v):
    m = ADAM_B1 * m + (1.0 - ADAM_B1) * g
    v = ADAM_B2 * v + (1.0 - ADAM_B2) * (g * g)
    m_hat = m / (1.0 - ADAM_B1 ** ADAM_STEP)
    v_hat = v / (1.0 - ADAM_B2 ** ADAM_STEP)
    delta = -ADAM_LR * (m_hat / (jnp.sqrt(v_hat) + ADAM_EPS) + ADAM_WD * w)
    return delta, m, v


def _adamw_layer(w, g, m, v, layer, carry):
    k, n = w.shape[1:]
    tr = _row_tile(k, n, ADAM_TILE)
    blk = pl.BlockSpec((None, tr, n), lambda i: (layer, i, 0))
    carry = list(carry or [])
    return _tiled(lambda *a: (*_adamw_math(*a), a[1]), "adamw_layer", (k // tr,), [blk] * 4 + [ANY] * len(carry),
                  [blk] * 4, [jax.ShapeDtypeStruct(w.shape, F32)] * 4, [w, g, m, v] + carry, None,
                  alias={4 + pos: pos for pos in range(len(carry))})


def _adamw(w, g, m, v):
    shape = w.shape
    two_d = lambda a: a.reshape(-1, shape[-1])
    outs = _elementwise(_adamw_math, "adamw", [two_d(w), two_d(g), two_d(m), two_d(v)], [F32, F32, F32])
    return [o.reshape(shape) for o in outs]


ANY = pl.BlockSpec(memory_space=pl.ANY)


def _place():
    x, y, c = lax.axis_index("x"), lax.axis_index("y"), lax.axis_index("c")
    chips = [(1 - x, y), (x, 1 - y), (1 - x, 1 - y)]
    return x, y, c, chips


def _cols(ref, start, size):
    idx = (slice(None),) * (len(ref.shape) - 1) + (pl.ds(pl.multiple_of(start, GW), size),)
    return ref.at[idx]


def _rows_of(ref, start, size):
    nd = len(ref.shape)
    idx = (slice(None),) * (nd - 2) + (pl.ds(pl.multiple_of(start, 16), size), slice(None))
    return ref.at[idx]


def _region(ref, col_sharded, chip, half):
    k, n = ref.shape
    align = 16 if ref.dtype == BF else 8
    if col_sharded:
        return ref.at[pl.ds(pl.multiple_of(half * (k // 2), align), k // 2),
                      pl.ds(pl.multiple_of(chip * (n // NCHIP), GW), n // NCHIP)]
    rows = k // (2 * NCHIP)
    return ref.at[pl.ds(pl.multiple_of((2 * chip + half) * rows, align), rows), :]


def _job_gather(bufs, col_sharded, handoff):
    n = len(bufs)

    def copies(outs, send_sems, recv_sems, stage):
        x, y, c, chips = _place()
        sends, lands = [], []
        for k in range(n):
            for j, chip in enumerate(chips):
                theirs = 2 * chip[0] + chip[1]
                if stage == 0:
                    src, to = _region(outs[k], col_sharded[k], 2 * x + y, c), (*chip, c)
                    land = _region(outs[k], col_sharded[k], theirs, c)
                else:
                    src, to = _region(outs[k], col_sharded[k], theirs, c), (x, y, 1 - c)
                    land = _region(outs[k], col_sharded[k], theirs, 1 - c)
                sem = 3 * n * stage + 3 * k + j
                sems = dict(send_sem=send_sems.at[sem], recv_sem=recv_sems.at[sem], device_id=to, device_id_type=MESH)
                sends.append(pltpu.make_async_remote_copy(src_ref=src, dst_ref=src, **sems))
                lands.append(pltpu.make_async_remote_copy(src_ref=land, dst_ref=land, **sems))
        return sends, lands

    def start(ins, outs, send_sems, recv_sems):
        for cp in copies(outs, send_sems, recv_sems, 0)[0]:
            cp.start()

    def hand_on(ins, outs, send_sems, recv_sems):
        for cp in copies(outs, send_sems, recv_sems, 0)[1]:
            cp.wait_recv()
        for cp in copies(outs, send_sems, recv_sems, 1)[0]:
            cp.start()

    def finish(ins, outs, send_sems, recv_sems):
        sends, lands = copies(outs, send_sems, recv_sems, 1)
        for cp in lands:
            cp.wait_recv()
        for cp in copies(outs, send_sems, recv_sems, 0)[0] + sends:
            cp.wait_send()

    return _Job(bufs, n, [], 6 * n, [(0.0, start), (handoff, hand_on), (1.0, finish)])


def _half(ref, col_sharded, c):
    k, n = ref.shape[-2:]
    return _rows_of(ref, c * (k // 2), k // 2) if col_sharded else _cols(ref, c * (n // 2), n // 2)


def _quarter(ref, col_sharded, j):
    k, n = ref.shape[-2:]
    return _cols(ref, j * (n // NCHIP), n // NCHIP) if col_sharded else _rows_of(ref, j * (k // NCHIP), k // NCHIP)


def _job_pair(grads, col_sharded):
    n = len(grads)

    def half_shape(g, col):
        return (g.shape[0] // 2, g.shape[1]) if col else (g.shape[0], g.shape[1] // 2)

    def copies(ins, got, send_sems, recv_sems):
        x, y, c, _ = _place()
        return [pltpu.make_async_remote_copy(
            src_ref=_half(ins[k], col_sharded[k], 1 - c), dst_ref=got[k], send_sem=send_sems.at[k],
            recv_sem=recv_sems.at[k], device_id=(x, y, 1 - c), device_id_type=MESH) for k in range(n)]

    def start(*refs):
        for cp in copies(*refs):
            cp.start()

    def finish(*refs):
        for cp in copies(*refs):
            cp.wait()

    fresh = [jax.ShapeDtypeStruct(half_shape(g, col), g.dtype) for g, col in zip(grads, col_sharded)]
    return _Job(grads, 0, fresh, n, [(0.0, start), (1.0, finish)])


def _job_chip(halves, col_sharded):
    n = len(halves)

    def quarter_shape(h, col):
        return (3, h.shape[0], h.shape[1] // NCHIP) if col else (3, h.shape[0] // NCHIP, h.shape[1])

    def copies(ins, got, send_sems, recv_sems):
        x, y, c, chips = _place()
        return [pltpu.make_async_remote_copy(
            src_ref=_quarter(ins[k], col_sharded[k], 2 * chip[0] + chip[1]), dst_ref=got[k].at[j],
            send_sem=send_sems.at[3 * k + j], recv_sem=recv_sems.at[3 * k + j], device_id=(*chip, c), device_id_type=MESH)
            for k in range(n) for j, chip in enumerate(chips)]

    def start(*refs):
        for cp in copies(*refs):
            cp.start()

    def finish(*refs):
        for cp in copies(*refs):
            cp.wait()

    fresh = [jax.ShapeDtypeStruct(quarter_shape(h, col), h.dtype) for h, col in zip(halves, col_sharded)]
    return _Job(halves, 0, fresh, 3 * n, [(0.0, start), (1.0, finish)])


def _job_sibling(shards, col_sharded, layers):
    n = len(shards)

    def copies(outs, send_sems, recv_sems):
        x, y, c, _ = _place()
        sends, lands = [], []
        for k in range(n):
            sems = dict(send_sem=send_sems.at[k], recv_sem=recv_sems.at[k], device_id=(x, y, 1 - c), device_id_type=MESH)
            mine = _half(outs[k].at[layers[k]], col_sharded[k], c)
            theirs = _half(outs[k].at[layers[k]], col_sharded[k], 1 - c)
            sends.append(pltpu.make_async_remote_copy(src_ref=mine, dst_ref=mine, **sems))
            lands.append(pltpu.make_async_remote_copy(src_ref=theirs, dst_ref=theirs, **sems))
        return sends, lands

    def start(ins, outs, send_sems, recv_sems):
        for cp in copies(outs, send_sems, recv_sems)[0]:
            cp.start()

    def finish(ins, outs, send_sems, recv_sems):
        sends, lands = copies(outs, send_sems, recv_sems)
        for cp in lands:
            cp.wait_recv()
        for cp in sends:
            cp.wait_send()

    return _Job(shards, n, [], n, [(0.0, start), (1.0, finish)])


def _standalone(jobs, name):
    return _pc(jobs, lambda: None, name=name, in_specs=[], out_specs=[], out_shape=[])()[1]


def _all_reduce_small(buf, jobs=()):
    rows = buf.shape[0]
    per = rows // NDEV
    flips = [(fx, fy, fc) for fx in (0, 1) for fy in (0, 1) for fc in (0, 1)][1:]

    def body(in_ref, out_ref, got_ref, send_sems, recv_sems):
        x, y, c, _ = _place()
        me = 4 * x + 2 * y + c

        def peer(f):
            return tuple(1 - pos if flip else pos for pos, flip in zip((x, y, c), f))

        def block(ref, dev):
            return ref.at[pl.ds(pl.multiple_of(dev * per, 8), per), :]

        scatter = []
        for k, f in enumerate(flips):
            px, py, pc = peer(f)
            scatter.append(pltpu.make_async_remote_copy(
                src_ref=block(in_ref, 4 * px + 2 * py + pc), dst_ref=got_ref.at[k], send_sem=send_sems.at[k],
                recv_sem=recv_sems.at[k], device_id=(px, py, pc), device_id_type=MESH))
        for cp in scatter:
            cp.start()
        for cp in scatter:
            cp.wait()
        total = block(in_ref, me)[...]
        for k in range(len(flips)):
            total = total + got_ref[k]
        block(out_ref, me)[...] = total
        share = []
        for k, f in enumerate(flips):
            share.append(pltpu.make_async_remote_copy(
                src_ref=block(out_ref, me), dst_ref=block(out_ref, me), send_sem=send_sems.at[7 + k],
                recv_sem=recv_sems.at[7 + k], device_id=peer(f), device_id_type=MESH))
        for cp in share:
            cp.start()
        for k, f in enumerate(flips):
            share[k].wait_send()
            px, py, pc = peer(f)
            theirs = block(out_ref, 4 * px + 2 * py + pc)
            pltpu.make_async_remote_copy(
                src_ref=theirs, dst_ref=theirs, send_sem=send_sems.at[7 + k], recv_sem=recv_sems.at[7 + k],
                device_id=(px, py, pc), device_id_type=MESH).wait_recv()

    vmem = pl.BlockSpec(memory_space=pltpu.VMEM)
    return _pc(
        jobs, body, name="all_reduce_small", in_specs=[vmem], out_specs=vmem,
        out_shape=jax.ShapeDtypeStruct((rows, GW), F32),
        scratch_shapes=[pltpu.VMEM((NDEV - 1, per, GW), F32), pltpu.SemaphoreType.DMA((14,)),
                        pltpu.SemaphoreType.DMA((14,))],
    )(buf)


BIG = ("ffn1_w13", "ffn1_w2", "w_in", "w_up", "w_out", "ffn2_w13", "ffn2_w2")
BIG_COL_SHARDED = (True, False, True, True, False, True, False)
SMALL = ("ffn1_norm", "mix_norm", "pool_w", "pool_scale", "sconv_w", "cconv_w", "cconv_ln_g", "cconv_ln_b",
         "sgu_ln_g", "sgu_ln_b", "sgu_w", "sgu_b", "ffn2_norm", "final_norm")
WEIGHTS = ("ffn1_norm", "ffn1_w13", "ffn1_w2", "mix_norm", "w_in", "pool_w", "pool_scale", "sconv_w", "cconv_w",
           "cconv_ln_g", "cconv_ln_b", "sgu_ln_g", "sgu_ln_b", "sgu_w", "sgu_b", "w_up", "w_out", "ffn2_norm",
           "ffn2_w13", "ffn2_w2", "final_norm")


def _pad_rows(a, rows):
    return jnp.pad(a, ((0, 0), (0, rows - a.shape[1]), (0, 0)))


def kernel(x, ffn1_norm, ffn1_w13, ffn1_w2, mix_norm, w_in, pool_w, pool_scale, sconv_w, cconv_w, cconv_ln_g, cconv_ln_b, sgu_ln_g, sgu_ln_b, sgu_w, sgu_b, w_up, w_out, ffn2_norm, ffn2_w13, ffn2_w2, final_norm, loss_target, m_ffn1_norm, m_ffn1_w13, m_ffn1_w2, m_mix_norm, m_w_in, m_pool_w, m_pool_scale, m_sconv_w, m_cconv_w, m_cconv_ln_g, m_cconv_ln_b, m_sgu_ln_g, m_sgu_ln_b, m_sgu_w, m_sgu_b, m_w_up, m_w_out, m_ffn2_norm, m_ffn2_w13, m_ffn2_w2, m_final_norm, v_ffn1_norm, v_ffn1_w13, v_ffn1_w2, v_mix_norm, v_w_in, v_pool_w, v_pool_scale, v_sconv_w, v_cconv_w, v_cconv_ln_g, v_cconv_ln_b, v_sgu_ln_g, v_sgu_ln_b, v_sgu_w, v_sgu_b, v_w_up, v_w_out, v_ffn2_norm, v_ffn2_w13, v_ffn2_w2, v_final_norm):
    args = dict(locals())
    w = {nm: args[nm] for nm in WEIGHTS}
    m = {nm: args["m_" + nm] for nm in WEIGHTS}
    v = {nm: args["v_" + nm] for nm in WEIGHTS}
    depth = ffn1_w13.shape[0]
    chip = 2 * lax.axis_index("x") + lax.axis_index("y")

    sc = jnp.stack([lax.axis_index("c"), chip]).astype(jnp.int32)
    col_of = dict(zip(BIG, BIG_COL_SHARDED), sconv_w=True, cconv_w=True)
    sources = {nm: (w[nm], BF) for nm in BIG}
    sources["w_up"] = (w_up.reshape(depth, NBR * BW, w_up.shape[-1]), BF)
    sources["sconv_w"] = (_pad_rows(sconv_w, 2 * SKP), F32)
    sources["cconv_w"] = (_pad_rows(cconv_w, 2 * CKP), F32)
    first_group = ("ffn1_w13", "ffn1_w2")
    full = [dict() for _ in range(depth)]
    for nm in first_group:
        full[0][nm] = _cast_into(sources[nm][0], 0, col_of[nm], sources[nm][1], sc)

    def gather_job(l, names, handoff):
        return _job_gather([full[l][nm] for nm in names], [col_of[nm] for nm in names], handoff)

    def gather_with(l, names, handoff, call):
        res, job_outs = call([gather_job(l, names, handoff)] if l < depth else [])
        if l < depth:
            full[l].update(zip(names, job_outs[0]))
        return res

    rest = [(nm, l) for l in range(depth) for nm in sources if not (l == 0 and nm in first_group)]
    casted = gather_with(0, first_group, 1.0, lambda jobs: _cast_rest(
        [(sources[nm][0], l, col_of[nm], sources[nm][1]) for nm, l in rest], jobs))
    for (nm, l), arr in zip(rest, casted):
        full[l][nm] = arr

    xs = x[0]
    row = lambda a: a.reshape(1, -1)
    saved = []
    for l in range(depth):
        g1, gm, g2 = row(ffn1_norm[l]), row(mix_norm[l]), row(ffn2_norm[l])
        x1, h1, s1, ab1 = gather_with(l, ("w_in",), 1.0, lambda jobs: _ffn_fwd(
            xs, g1, full[l]["ffn1_w13"], full[l]["ffn1_w2"], jobs))
        hm, p = gather_with(l, ("w_up", "w_out", "sconv_w", "cconv_w", "ffn2_w2"), 0.85, lambda jobs: _mix_in(
            x1, gm, full[l]["w_in"], jobs))
        branch = (pool_w[l], row(pool_scale[l]), full[l]["sconv_w"][:SKP], full[l]["cconv_w"][:CKP], row(cconv_ln_g[l]),
                  row(cconv_ln_b[l]), row(sgu_ln_g[l]), row(sgu_ln_b[l]), sgu_w[l], sgu_b[l].T)
        y, conv_z = gather_with(l, ("ffn2_w13",), 1.0, lambda jobs: _mix_branches_fwd(p, *branch, jobs=jobs))
        w_up_l = full[l]["w_up"].reshape(NBR, BW, D)
        x2, merged, up = gather_with(l + 1, ("ffn1_w2",), 0.8, lambda jobs: _mix_out_fwd(
            x1, y, p, w_up_l, full[l]["w_out"], jobs))
        x3, h2, s2, ab2 = gather_with(l + 1, ("ffn1_w13",), 0.85, lambda jobs: _ffn_fwd(
            x2, g2, full[l]["ffn2_w13"], full[l]["ffn2_w2"], jobs))
        lw = dict(g1=g1, gm=gm, g2=g2, w13a=full[l]["ffn1_w13"], w2a=full[l]["ffn1_w2"], w13b=full[l]["ffn2_w13"],
                  w2b=full[l]["ffn2_w2"], w_in=full[l]["w_in"], w_up=w_up_l, w_out=full[l]["w_out"], branch=branch)
        saved.append(dict(lw=lw, x0=xs, x1=x1, x2=x2, h1=h1, s1=s1, ab1=ab1, hm=hm, p=p, y=y, z=conv_z, merged=merged, up=up, h2=h2,
                          s2=s2, ab2=ab2))
        xs = x3

    (dx, d_final, loss_part), _ = _loss_head(xs, row(final_norm), loss_target[0])
    loss = lax.psum(loss_part[0, 0], ("x", "y", "c"))

    ici_us = dict(ffn1_w13=64, ffn1_w2=32, w_in=93, w_up=23, w_out=12, ffn2_w13=64, ffn2_w2=32)
    parts, pair_sums, reduced, big_updates, pending = {}, {}, {}, {}, []

    def take_jobs(budget_us):
        chosen = []
        for task in list(pending):
            kind, (nm, _) = task
            if kind == "chip":
                if ici_us[nm] > budget_us:
                    continue
                budget_us -= ici_us[nm]
            if kind == "sib" and any(k == "sib" and key[0] == nm for k, key in chosen):
                continue
            chosen.append(task)
            pending.remove(task)
        groups, jobs = [], []
        for kind in ("pair", "sib", "chip"):
            keys = [key for k, key in chosen if k == kind]
            if not keys:
                continue
            cols = [col_of[nm] for nm, _ in keys]
            groups.append((kind, keys))
            if kind == "pair":
                jobs.append(_job_pair([parts[key] for key in keys], cols))
            elif kind == "chip":
                jobs.append(_job_chip([pair_sums[key] for key in keys], cols))
            else:
                jobs.append(_job_sibling([reduced[nm] for nm, _ in keys], cols, [layer for _, layer in keys]))
        return groups, jobs

    def settle(groups, job_outs):
        for (kind, keys), outs in zip(groups, job_outs):
            for key, out in zip(keys, outs):
                nm, layer = key
                if kind == "pair":
                    pair_sums[key] = _pair_sum(parts[key], out, col_of[nm], sc)
                    pending.append(("chip", key))
                elif kind == "chip":
                    assert not any(k == "sib" and other[0] == nm for k, other in pending)
                    reduced[nm] = _chip_sum(pair_sums[key], out, col_of[nm], layer, depth, sc, reduced.get(nm))
                    pending.append(("sib", key))
                else:
                    reduced[nm] = out
                    as3 = lambda a: a.reshape(out.shape)
                    big_updates[nm] = _adamw_layer(as3(w[nm]), out, as3(m[nm]), as3(v[nm]), layer, big_updates.get(nm))

    def run(budget_us, call, carrier=True):
        groups, jobs = take_jobs(budget_us) if carrier else ([], [])
        res, job_outs = call(jobs)
        settle(groups, job_outs)
        return res

    def wgrad_done(key, partial):
        parts[key] = partial
        pending.append(("pair", key))

    small_parts = {nm: [None] * depth for nm in SMALL if nm != "final_norm"}
    for l in reversed(range(depth)):
        sv = saved[l]
        lw = sv["lw"]
        dx, dab, dyh, dg2 = run(100, lambda jobs: _ffn_bwd(
            dx, sv["x2"], lw["g2"], sv["ab2"], lw["w13b"], lw["w2b"], jobs))
        wgrad_done(("ffn2_w13", l), run(58, lambda jobs: _wgrad(sv["h2"], dab, D, 512, "wgrad_w13", jobs)))
        wgrad_done(("ffn2_w2", l), run(33, lambda jobs: _wgrad(sv["s2"], dyh, 256, D, "wgrad_w2", jobs)))
        small_parts["ffn2_norm"][l] = dg2

        dy, dp, dup, dxb = run(70, lambda jobs: _mix_out_bwd(dx, sv["up"], sv["p"], lw["w_up"], lw["w_out"], jobs))
        wgrad_done(("w_out", l), run(16, lambda jobs: _wgrad(sv["merged"], dxb, D, 512, "wgrad_w_out", jobs), False))
        wgrad_done(("w_up", l), run(25, lambda jobs: _wgrad_groups(sv["y"], dup, BW, D, "wgrad_w_up", jobs), False))
        (dp, d_pool_w, d_pool_scale, d_sconv, d_cconv, d_clg, d_clb, d_slg, d_slb, d_sgu_w, d_sgu_b) = run(
            130, lambda jobs: _mix_branches_bwd(sv["p"], dy, sv["z"], dp, *lw["branch"], jobs=jobs))
        wgrad_done(("w_in", l), run(80, lambda jobs: _wgrad(sv["hm"], dp, D, 512, "wgrad_w_in", jobs)))
        dx, dgm = run(91, lambda jobs: _mix_in_bwd(dp, lw["w_in"], sv["x1"], lw["gm"], dx, jobs))
        small_parts["mix_norm"][l] = dgm
        small_parts["pool_w"][l] = d_pool_w
        small_parts["pool_scale"][l] = d_pool_scale
        small_parts["sconv_w"][l] = d_sconv[:SK]
        small_parts["cconv_w"][l] = d_cconv[:CK]
        small_parts["cconv_ln_g"][l] = d_clg
        small_parts["cconv_ln_b"][l] = d_clb
        small_parts["sgu_ln_g"][l] = d_slg
        small_parts["sgu_ln_b"][l] = d_slb
        small_parts["sgu_w"][l] = d_sgu_w
        small_parts["sgu_b"][l] = jnp.sum(d_sgu_b, axis=-1)

        dx, dab, dyh, dg1 = run(100, lambda jobs: _ffn_bwd(
            dx, sv["x0"], lw["g1"], sv["ab1"], lw["w13a"], lw["w2a"], jobs))
        wgrad_done(("ffn1_w2", l), run(33, lambda jobs: _wgrad(sv["s1"], dyh, 256, D, "wgrad_w2", jobs)))
        wgrad_done(("ffn1_w13", l), run(58, lambda jobs: _wgrad(sv["h1"], dab, D, 512, "wgrad_w13", jobs)))
        small_parts["ffn1_norm"][l] = dg1
    grad_x = dx[None]

    small_local = {nm: jnp.stack(parts).reshape(depth, *w[nm].shape[1:-1], -1) if nm not in ("sconv_w", "cconv_w")
                   else jnp.stack(parts) for nm, parts in small_parts.items()}
    small_local["final_norm"] = d_final.reshape(-1)
    sizes = [small_local[nm].size for nm in SMALL]
    total = sum(sizes)
    pad_to = NDEV * 8 * GW
    padded = -(-total // pad_to) * pad_to
    packed = jnp.concatenate([small_local[nm].reshape(-1) for nm in SMALL] + [jnp.zeros((padded - total,), F32)])
    groups, jobs = take_jobs(float("inf"))
    summed, job_outs = _all_reduce_small(packed.reshape(-1, GW), jobs)
    settle(groups, job_outs)
    summed = summed.reshape(-1)
    flushes = 0
    while pending:
        groups, jobs = take_jobs(float("inf"))
        settle(groups, _standalone(jobs, "grad_flush_%d" % flushes))
        flushes += 1
    big_grads = {nm: big_updates[nm][3].reshape(w[nm].shape) for nm in BIG}
    small_grads, off = {}, 0
    for nm, size in zip(SMALL, sizes):
        small_grads[nm] = summed[off:off + size].reshape(small_local[nm].shape)
        off += size
    for nm in ("sconv_w", "cconv_w"):
        small_grads[nm] = lax.dynamic_slice_in_dim(small_grads[nm], chip * GW, GW, axis=2)

    grads = {**big_grads, **small_grads}

    delta, new_m, new_v = {}, {}, {}
    for nm in BIG:
        delta[nm], new_m[nm], new_v[nm] = [a.reshape(w[nm].shape) for a in big_updates[nm][:3]]
    s_sizes = [w[nm].size for nm in SMALL]
    s_total = sum(s_sizes)
    s_padded = -(-s_total // (8 * GW)) * (8 * GW)

    def pack(tree):
        return jnp.concatenate([tree[nm].reshape(-1) for nm in SMALL] + [jnp.ones((s_padded - s_total,), F32)]).reshape(-1, GW)

    packed_out = _adamw(pack(w), pack(grads), pack(m), pack(v))
    off = 0
    for nm, size in zip(SMALL, s_sizes):
        for tree, arr in zip((delta, new_m, new_v), packed_out):
            tree[nm] = arr.reshape(-1)[off:off + size].reshape(w[nm].shape)
        off += size

    return (loss, grad_x, *[grads[nm] for nm in WEIGHTS], *[delta[nm] for nm in WEIGHTS],
            *[new_m[nm] for nm in WEIGHTS], *[new_v[nm] for nm in WEIGHTS])
```

```python
import jax
import jax.numpy as jnp
from jax import lax
from jax.experimental import pallas as pl
from jax.experimental.pallas import tpu as pltpu

D = 1024
FF = 2816
BW = 512
NBR = 4
MIXC = 4096
INC = 8192
GW = 128
CHUNK = 64
SK = 3
CK = 31
SKP = 8
CKP = 32
HALO = 32
TOKEN_TILE = 256
BRANCH_TILE = 512
EPS = 1e-6
NCHIP = 4
NDEV = 8

ADAM_LR = 0.001
ADAM_B1 = 0.9
ADAM_B2 = 0.999
ADAM_EPS = 1e-08
ADAM_WD = 0.01
ADAM_STEP = 10

VMEM_LIMIT = 56 * 1024 * 1024

BF = jnp.bfloat16
F32 = jnp.float32
MESH = pl.DeviceIdType.MESH
NT = (((1,), (1,)), ((), ()))
TN = (((0,), (0,)), ((), ()))


def _params(n_axes):
    return pltpu.CompilerParams(dimension_semantics=("arbitrary",) * n_axes, vmem_limit_bytes=VMEM_LIMIT)


class _Job:
    def __init__(self, args, n_inplace, fresh, nsem, phases):
        self.args, self.n_inplace, self.fresh, self.nsem, self.phases = list(args), n_inplace, list(fresh), nsem, phases


def _pc(jobs, body, name, in_specs, out_specs, out_shape, grid=(), scratch_shapes=(), input_output_aliases=None,
        compiler_params=None, grid_spec_scalars=None):
    single = not isinstance(out_shape, (list, tuple))
    core_out_specs = [out_specs] if single else list(out_specs)
    core_out_shape = [out_shape] if single else list(out_shape)
    n_in, n_out, n_scr = len(in_specs), len(core_out_specs), len(scratch_shapes)
    n_pre = 0 if grid_spec_scalars is None else 1
    all_in, all_out, all_shape = list(in_specs), list(core_out_specs), list(core_out_shape)
    all_scr, aliases, extra_args, layout = list(scratch_shapes), dict(input_output_aliases or {}), [], []
    for job in jobs:
        n_job_out = job.n_inplace + len(job.fresh)
        layout.append((len(all_in), len(job.args), len(all_out), n_job_out, len(all_scr)))
        for a in range(job.n_inplace):
            aliases[n_pre + len(all_in) + a] = len(all_out) + a
        all_in += [ANY] * len(job.args)
        extra_args += job.args
        all_out += [ANY] * n_job_out
        all_shape += [jax.ShapeDtypeStruct(a.shape, a.dtype) for a in job.args[:job.n_inplace]] + job.fresh
        all_scr += [pltpu.SemaphoreType.DMA((job.nsem,)), pltpu.SemaphoreType.DMA((job.nsem,))]
    steps = 1
    for extent in grid:
        steps *= extent
    events = []
    for (i0, na, o0, no, s0), job in zip(layout, jobs):
        for frac, fn in job.phases:
            events.append((min(int(frac * steps), steps - 1), frac >= 1.0, len(events), fn, (i0, na, o0, no, s0)))
    events.sort(key=lambda e: e[:3])

    def wrapped(*refs):
        pre, refs = refs[:n_pre], refs[n_pre:]
        ins, outs, scr = refs[:len(all_in)], refs[len(all_in):len(all_in) + len(all_out)], refs[len(all_in) + len(all_out):]
        step = 0
        for axis, extent in enumerate(grid):
            step = step * extent + pl.program_id(axis)

        def emit(event):
            at, _, _, fn, (i0, na, o0, no, s0) = event
            run = lambda: fn(ins[i0:i0 + na], outs[o0:o0 + no], scr[s0], scr[s0 + 1])
            if steps == 1:
                run()
            else:
                pl.when(step == at)(run)

        for event in events:
            if not event[1]:
                emit(event)
        body(*pre, *ins[:n_in], *outs[:n_out], *scr[:n_scr])
        for event in events:
            if event[1]:
                emit(event)

    kwargs = dict(name=name, out_shape=all_shape, input_output_aliases=aliases)
    if compiler_params is not None:
        kwargs["compiler_params"] = compiler_params
    if grid_spec_scalars is not None:
        kwargs["grid_spec"] = pltpu.PrefetchScalarGridSpec(
            num_scalar_prefetch=1, grid=grid, in_specs=all_in, out_specs=all_out, scratch_shapes=all_scr)
    else:
        kwargs.update(in_specs=all_in, out_specs=all_out, scratch_shapes=all_scr)
        if grid:
            kwargs["grid"] = grid
    call = pl.pallas_call(wrapped, **kwargs)

    def run_call(*args):
        pre_args = [] if grid_spec_scalars is None else [grid_spec_scalars]
        res = list(call(*pre_args, *args, *extra_args))
        core = res[0] if single else res[:n_out]
        job_outs = [res[o0:o0 + no] for (_, _, o0, no, _) in layout]
        return core, job_outs

    return run_call


def _full(shape):
    nd = len(shape)
    return pl.BlockSpec(shape, lambda *_: (0,) * nd, pipeline_mode=pl.Buffered(1))


def _rows(ts, width, col=0):
    return pl.BlockSpec((ts, width), lambda i: (i, col))


def _sig(v):
    return jax.nn.sigmoid(v)


def _rms_stats(x):
    r = lax.rsqrt(jnp.mean(x * x, axis=-1, keepdims=True) + EPS)
    return r, x * r


def _rms_bwd(x, g, dh):
    r, xh = _rms_stats(x)
    dg = jnp.sum(dh * xh, axis=0, keepdims=True)
    dxh = dh * g
    dx = r * (dxh - xh * jnp.mean(dxh * xh, axis=-1, keepdims=True))
    return dx, dg


def _accumulate(ref, val, first):
    @pl.when(first)
    def _():
        ref[...] = val

    @pl.when(jnp.logical_not(first))
    def _():
        ref[...] += val


def _ffn_fwd(x, g, w13, w2, jobs=()):
    s_len = x.shape[0]
    ts = min(TOKEN_TILE, s_len)

    def body(x_ref, g_ref, w13_ref, w2_ref, xo_ref, h_ref, s_ref, dsab_ref):
        xv = x_ref[...]
        r, xh = _rms_stats(xv)
        h = (xh * g_ref[...]).astype(BF)
        h_ref[...] = h
        ab = jnp.dot(h, w13_ref[...], preferred_element_type=F32)
        a = ab[:, :FF]
        b = ab[:, FF:]
        sg = _sig(a)
        sil = a * sg
        dsab_ref[:, :FF] = (b * (sg * (1.0 + a * (1.0 - sg)))).astype(BF)
        dsab_ref[:, FF:] = sil.astype(BF)
        s = (sil * b).astype(BF)
        s_ref[...] = s
        xo_ref[...] = xv + 0.5 * jnp.dot(s, w2_ref[...], preferred_element_type=F32)

    return _pc(
        jobs, body, name="ffn_fwd", grid=(s_len // ts,),
        in_specs=[_rows(ts, D), _full((1, D)), _full((D, 2 * FF)), _full((FF, D))],
        out_specs=[_rows(ts, D), _rows(ts, D), _rows(ts, FF), _rows(ts, 2 * FF)],
        out_shape=[jax.ShapeDtypeStruct((s_len, D), F32), jax.ShapeDtypeStruct((s_len, D), BF),
                   jax.ShapeDtypeStruct((s_len, FF), BF), jax.ShapeDtypeStruct((s_len, 2 * FF), BF)],
        compiler_params=_params(1),
    )(x, g, w13, w2)


def _ffn_bwd(dxo, x, g, dsab, w13, w2, jobs=()):
    s_len = x.shape[0]
    ts = min(TOKEN_TILE, s_len)

    def body(dxo_ref, x_ref, g_ref, dsab_ref, w13_ref, w2_ref, dxi_ref, dab_ref, dy_ref, dg_ref):
        i = pl.program_id(0)
        dxo_v = dxo_ref[...]
        dy = (0.5 * dxo_v).astype(BF)
        dy_ref[...] = dy
        ds = lax.dot_general(dy, w2_ref[...], NT, preferred_element_type=F32)
        dab_ref[:, :FF] = (ds * dsab_ref[:, :FF].astype(F32)).astype(BF)
        dab_ref[:, FF:] = (ds * dsab_ref[:, FF:].astype(F32)).astype(BF)
        dh = lax.dot_general(dab_ref[...], w13_ref[...], NT, preferred_element_type=F32)
        dx, dg = _rms_bwd(x_ref[...], g_ref[...], dh)
        dxi_ref[...] = dxo_v + dx
        _accumulate(dg_ref, dg, i == 0)

    return _pc(
        jobs, body, name="ffn_bwd", grid=(s_len // ts,),
        in_specs=[_rows(ts, D), _rows(ts, D), _full((1, D)), _rows(ts, 2 * FF), _full((D, 2 * FF)), _full((FF, D))],
        out_specs=[_rows(ts, D), _rows(ts, 2 * FF), _rows(ts, D), pl.BlockSpec((1, D), lambda i: (0, 0))],
        out_shape=[jax.ShapeDtypeStruct((s_len, D), F32), jax.ShapeDtypeStruct((s_len, 2 * FF), BF),
                   jax.ShapeDtypeStruct((s_len, D), BF), jax.ShapeDtypeStruct((1, D), F32)],
        compiler_params=_params(1),
    )(dxo, x, g, dsab, w13, w2)


def _wgrad(a, b, tk, tn, name, jobs=()):
    s_len, k = a.shape
    n = b.shape[1]

    def body(a_ref, b_ref, o_ref):
        o_ref[...] = lax.dot_general(a_ref[...], b_ref[...], TN, preferred_element_type=F32).astype(BF)

    return _pc(
        jobs, body, name=name, grid=(k // tk, n // tn),
        in_specs=[pl.BlockSpec((s_len, tk), lambda i, j: (0, i)), pl.BlockSpec((s_len, tn), lambda i, j: (0, j))],
        out_specs=pl.BlockSpec((tk, tn), lambda i, j: (i, j)),
        out_shape=jax.ShapeDtypeStruct((k, n), BF),
        compiler_params=_params(2),
    )(a, b)


def _wgrad_groups(a, b, ka, nb, name, jobs=()):
    s_len = a.shape[0]
    groups = a.shape[1] // ka

    def body(a_ref, b_ref, o_ref):
        o_ref[...] = lax.dot_general(a_ref[...], b_ref[...], TN, preferred_element_type=F32).astype(BF)

    return _pc(
        jobs, body, name=name, grid=(groups,),
        in_specs=[pl.BlockSpec((s_len, ka), lambda gi: (0, gi)), pl.BlockSpec((s_len, nb), lambda gi: (0, gi))],
        out_specs=pl.BlockSpec((ka, nb), lambda gi: (gi, 0)),
        out_shape=jax.ShapeDtypeStruct((groups * ka, nb), BF),
        compiler_params=_params(1),
    )(a, b)


def _mix_in(x, g, w_in, jobs=()):
    s_len = x.shape[0]
    ts = min(TOKEN_TILE, s_len)

    def body(x_ref, g_ref, w_ref, h_ref, p_ref):
        _, xh = _rms_stats(x_ref[...])
        h = (xh * g_ref[...]).astype(BF)
        h_ref[...] = h
        p_ref[...] = jnp.dot(h, w_ref[...], preferred_element_type=F32).astype(BF)

    return _pc(
        jobs, body, name="mix_in", grid=(s_len // ts,),
        in_specs=[_rows(ts, D), _full((1, D)), _full((D, INC))],
        out_specs=[_rows(ts, D), _rows(ts, INC)],
        out_shape=[jax.ShapeDtypeStruct((s_len, D), BF), jax.ShapeDtypeStruct((s_len, INC), BF)],
        compiler_params=_params(1),
    )(x, g, w_in)


def _shift(e, j):
    n = e.shape[0]
    j = j % n
    return e if j == 0 else pltpu.roll(e, j, 0)


def _ln_stats(z):
    mu = jnp.mean(z, axis=-1, keepdims=True)
    zc = z - mu
    rs = lax.rsqrt(jnp.mean(zc * zc, axis=-1, keepdims=True) + EPS)
    return rs, zc * rs


def _ln_bwd(rs, zn, dzn):
    return rs * (dzn - jnp.mean(dzn, axis=-1, keepdims=True) - zn * jnp.mean(dzn * zn, axis=-1, keepdims=True))


_GELU_C0 = 0.7978845608028654
_GELU_C1 = 0.044715


def _gelu(p):
    th = jnp.tanh(_GELU_C0 * (p + _GELU_C1 * p * p * p))
    return 0.5 * p * (1.0 + th), th


def _gelu_grad(p, th):
    return 0.5 * (1.0 + th) + 0.5 * p * (1.0 - th * th) * (_GELU_C0 * (1.0 + 3.0 * _GELU_C1 * p * p))


def _pool_diff(a, t, sign):
    outs = []
    for gi in range(NBR):
        win = 2 ** (gi + 1)
        ag = a[:, gi * GW:(gi + 1) * GW]
        cnt = jnp.clip(t + 1, 1, win).astype(F32)
        ws = ag if sign > 0 else ag / cnt
        for s in range(gi + 1):
            ws = ws + _shift(ws, sign * (2 ** s))
        outs.append((ws / cnt if sign > 0 else ws) - ag)
    return outs


def _sgu_mask():
    row = lax.broadcasted_iota(jnp.int32, (GW, GW), 0)
    col = lax.broadcasted_iota(jnp.int32, (GW, GW), 1)
    return (col // CHUNK) <= (row // CHUNK)


def _assemble(pe_ref, prev_ref, cur_ref, next_ref, i, last, ts):
    pe_ref[0:HALO, :] = jnp.where(i > 0, prev_ref[...], jnp.zeros_like(prev_ref))
    pe_ref[HALO:HALO + ts, :] = cur_ref[...]
    if next_ref is not None:
        pe_ref[HALO + ts:, :] = jnp.where(i < last, next_ref[...], jnp.zeros_like(next_ref))


def _halo_specs(ts, width, s_len, with_next):
    per = ts // HALO
    specs = [pl.BlockSpec((HALO, width), lambda i: (jnp.maximum(i * per - 1, 0), 0)),
             pl.BlockSpec((ts, width), lambda i: (i, 0))]
    if with_next:
        specs.append(pl.BlockSpec((HALO, width), lambda i: (jnp.minimum((i + 1) * per, s_len // HALO - 1), 0)))
    return specs


def _branch_weights_specs():
    return [_full((NBR, GW, GW)), _full((1, BW)), _full((SKP, BW)), _full((CKP, BW)), _full((1, BW)), _full((1, BW)),
            _full((1, BW)), _full((1, BW)), _full((NBR, GW, GW)), _full((GW, NBR))]


def _mix_branches_fwd(p, pool_w, pool_scale, sconv_w, cconv_w, cln_g, cln_b, sln_g, sln_b, sgu_w, sgu_bt, jobs=()):
    s_len = p.shape[0]
    ts = min(BRANCH_TILE, s_len)
    ext = HALO + ts

    def body(pp_ref, pc_ref, pw_ref, ps_ref, sw_ref, cw_ref, clg_ref, clb_ref, slg_ref, slb_ref, gw_ref, gb_ref,
             y_ref, z_ref, pe_ref):
        i = pl.program_id(0)
        _assemble(pe_ref, pp_ref, pc_ref, None, i, 0, ts)
        t = i * ts - HALO + lax.broadcasted_iota(jnp.int32, (ext, 1), 0)

        dgs = _pool_diff(pe_ref[:, 0:BW].astype(F32), t, 1)
        for gi in range(NBR):
            e = jnp.dot(dgs[gi][HALO:].astype(BF), pw_ref[gi].astype(BF), preferred_element_type=F32)
            y_ref[:, gi * GW:(gi + 1) * GW] = (e * ps_ref[:, gi * GW:(gi + 1) * GW]).astype(BF)

        xin = pe_ref[:, BW:2 * BW].astype(F32)
        cg = pe_ref[:, 3 * BW:4 * BW].astype(F32)
        q = cg * xin
        cv = sw_ref[2:3, :] * q + sw_ref[1:2, :] * _shift(q, 1) + sw_ref[0:1, :] * _shift(q, 2)
        y_ref[:, BW:2 * BW] = (pe_ref[HALO:, 2 * BW:3 * BW].astype(F32) * cv[HALO:]).astype(BF)

        yg = pe_ref[:, 4 * BW:5 * BW].astype(F32) * _sig(pe_ref[:, 5 * BW:6 * BW].astype(F32))
        z = cw_ref[CK - 1:CK, :] * yg
        for j in range(1, CK):
            z = z + cw_ref[CK - 1 - j:CK - j, :] * _shift(yg, j)
        z_ref[...] = z[HALO:].astype(BF)
        _, zn = _ln_stats(z[HALO:])
        nn = zn * clg_ref[...] + clb_ref[...]
        y_ref[:, 2 * BW:3 * BW] = (nn * _sig(nn)).astype(BF)

        u, _ = _gelu(pc_ref[:, 6 * BW:7 * BW].astype(F32))
        v, _ = _gelu(pc_ref[:, 7 * BW:8 * BW].astype(F32))
        _, vn = _ln_stats(v)
        vn = (vn * slg_ref[...] + slb_ref[...]).astype(BF)
        mask = _sgu_mask()
        for hd in range(NBR):
            wm = jnp.where(mask, gw_ref[hd], 0.0).astype(BF)
            for blk in range(ts // GW):
                rows = slice(blk * GW, (blk + 1) * GW)
                cols = slice(hd * GW, (hd + 1) * GW)
                zz = jnp.dot(wm, vn[rows, cols], preferred_element_type=F32) + gb_ref[:, hd:hd + 1]
                y_ref[rows, 3 * BW + hd * GW:3 * BW + (hd + 1) * GW] = (u[rows, cols] * zz).astype(BF)

    return _pc(
        jobs, body, name="mix_branches_fwd", grid=(s_len // ts,),
        in_specs=_halo_specs(ts, MIXC, s_len, False) + _branch_weights_specs(),
        out_specs=[_rows(ts, NBR * BW), _rows(ts, BW)],
        out_shape=[jax.ShapeDtypeStruct((s_len, NBR * BW), BF), jax.ShapeDtypeStruct((s_len, BW), BF)],
        scratch_shapes=[pltpu.VMEM((ext, MIXC), BF)],
        compiler_params=_params(1),
    )(p, p, pool_w, pool_scale, sconv_w, cconv_w, cln_g, cln_b, sln_g, sln_b, sgu_w, sgu_bt)


def _mix_branches_bwd(p, dy, z, dp, pool_w, pool_scale, sconv_w, cconv_w, cln_g, cln_b, sln_g, sln_b, sgu_w, sgu_bt,
                      jobs=()):
    s_len = p.shape[0]
    ts = min(BRANCH_TILE, s_len)
    ext = ts + 2 * HALO
    last = s_len // ts - 1
    tile = slice(HALO, HALO + ts)
    small_shapes = [(NBR, GW, GW), (1, BW), (SKP, BW), (CKP, BW), (1, BW), (1, BW), (1, BW), (1, BW), (NBR, GW, GW),
                    (NBR, GW, GW)]

    def body(pp_ref, pc_ref, pn_ref, dyp_ref, dyc_ref, dyn_ref, zc_ref, zn_ref, dpin_ref,
             pw_ref, ps_ref, sw_ref, cw_ref, clg_ref, clb_ref, slg_ref, slb_ref, gw_ref, gb_ref,
             dp_ref, dpw_ref, dps_ref, dsw_ref, dcw_ref, dclg_ref, dclb_ref, dslg_ref, dslb_ref, dgw_ref, dgb_ref,
             pe_ref, de_ref):
        del dyp_ref, dpin_ref
        i = pl.program_id(0)
        first = i == 0
        _assemble(pe_ref, pp_ref, pc_ref, pn_ref, i, last, ts)
        de_ref[0:HALO, :] = jnp.zeros((HALO, NBR * BW), BF)
        de_ref[HALO:HALO + ts, :] = dyc_ref[...]
        de_ref[HALO + ts:, :] = jnp.where(i < last, dyn_ref[...], jnp.zeros_like(dyn_ref))
        t = i * ts - HALO + lax.broadcasted_iota(jnp.int32, (ext, 1), 0)

        @pl.when(first)
        def _():
            dsw_ref[...] = jnp.zeros((SKP, BW), F32)
            dcw_ref[...] = jnp.zeros((CKP, BW), F32)

        dgs = _pool_diff(pe_ref[:, 0:BW].astype(F32), t, 1)
        dya = de_ref[:, 0:BW].astype(F32)
        dds = []
        for gi in range(NBR):
            cols = slice(gi * GW, (gi + 1) * GW)
            pw = pw_ref[gi].astype(BF)
            d_t = dgs[gi][tile].astype(BF)
            e = jnp.dot(d_t, pw, preferred_element_type=F32)
            _accumulate(dps_ref.at[:, cols], jnp.sum(dya[tile, cols] * e, axis=0, keepdims=True), first)
            de_g = (dya[:, cols] * ps_ref[:, cols]).astype(BF)
            _accumulate(dpw_ref.at[gi], lax.dot_general(d_t, de_g[tile], TN, preferred_element_type=F32), first)
            dds.append(lax.dot_general(de_g, pw, NT, preferred_element_type=F32))
        das = _pool_diff(jnp.concatenate(dds, axis=1), t, -1)
        for gi in range(NBR):
            dp_ref[:, gi * GW:(gi + 1) * GW] = das[gi][tile].astype(BF)

        xin = pe_ref[:, BW:2 * BW].astype(F32)
        bg = pe_ref[:, 2 * BW:3 * BW].astype(F32)
        cg = pe_ref[:, 3 * BW:4 * BW].astype(F32)
        q = cg * xin
        qs = [q, _shift(q, 1), _shift(q, 2)]
        cv = sw_ref[2:3, :] * qs[0] + sw_ref[1:2, :] * qs[1] + sw_ref[0:1, :] * qs[2]
        dyb = de_ref[:, BW:2 * BW].astype(F32)
        dcv = dyb * bg
        for j in range(SK):
            dsw_ref[SK - 1 - j:SK - j, :] += jnp.sum(dcv[tile] * qs[j][tile], axis=0, keepdims=True)
        dq = sw_ref[2:3, :] * dcv + sw_ref[1:2, :] * _shift(dcv, -1) + sw_ref[0:1, :] * _shift(dcv, -2)
        dp_ref[:, BW:2 * BW] = (dq * cg)[tile].astype(BF)
        dp_ref[:, 2 * BW:3 * BW] = (dyb * cv)[tile].astype(BF)
        dp_ref[:, 3 * BW:4 * BW] = (dq * xin)[tile].astype(BF)

        ca = pc_ref[:, 4 * BW:5 * BW].astype(F32)
        sb = _sig(pc_ref[:, 5 * BW:6 * BW].astype(F32))
        yg_t = ca * sb
        z = jnp.concatenate([zc_ref[...].astype(F32), zn_ref[...].astype(F32)], axis=0)
        rs, zn = _ln_stats(z)
        nn = zn * clg_ref[...] + clb_ref[...]
        sn = _sig(nn)
        dn = de_ref[HALO:, 2 * BW:3 * BW].astype(F32) * (sn * (1.0 + nn * (1.0 - sn)))
        _accumulate(dclg_ref, jnp.sum((dn * zn)[:ts], axis=0, keepdims=True), first)
        _accumulate(dclb_ref, jnp.sum(dn[:ts], axis=0, keepdims=True), first)
        dz = _ln_bwd(rs, zn, dn * clg_ref[...])
        dyg = cw_ref[CK - 1:CK, :] * dz[:ts]
        dcw_ref[CK - 1:CK, :] += jnp.sum(dz[:ts] * yg_t, axis=0, keepdims=True)
        for j in range(1, CK):
            dz_ahead = _shift(dz, -j)[:ts]
            dyg = dyg + cw_ref[CK - 1 - j:CK - j, :] * dz_ahead
            dcw_ref[CK - 1 - j:CK - j, :] += jnp.sum(dz_ahead * yg_t, axis=0, keepdims=True)
        dp_ref[:, 4 * BW:5 * BW] = (dyg * sb).astype(BF)
        dp_ref[:, 5 * BW:6 * BW] = (dyg * ca * sb * (1.0 - sb)).astype(BF)

        pu = pc_ref[:, 6 * BW:7 * BW].astype(F32)
        pv = pc_ref[:, 7 * BW:8 * BW].astype(F32)
        u, thu = _gelu(pu)
        v, thv = _gelu(pv)
        vrs, vn0 = _ln_stats(v)
        vn = (vn0 * slg_ref[...] + slb_ref[...]).astype(BF)
        dyd = dyc_ref[:, 3 * BW:4 * BW].astype(F32)
        dzz = dyd * u
        dzb = dzz.astype(BF)
        mask = _sgu_mask()
        dvn_cols = []
        for hd in range(NBR):
            cols = slice(hd * GW, (hd + 1) * GW)
            wm = jnp.where(mask, gw_ref[hd], 0.0).astype(BF)
            dwm = jnp.zeros((GW, GW), F32)
            dbs = jnp.zeros((GW, GW), F32)
            dvn_rows = []
            for blk in range(ts // GW):
                rows = slice(blk * GW, (blk + 1) * GW)
                zz = jnp.dot(wm, vn[rows, cols], preferred_element_type=F32) + gb_ref[:, hd:hd + 1]
                dp_ref[rows, 6 * BW + hd * GW:6 * BW + (hd + 1) * GW] = (
                    dyd[rows, cols] * zz * _gelu_grad(pu[rows, cols], thu[rows, cols])).astype(BF)
                dwm = dwm + lax.dot_general(dzb[rows, cols], vn[rows, cols], NT, preferred_element_type=F32)
                dbs = dbs + dzz[rows, cols]
                dvn_rows.append(lax.dot_general(wm, dzb[rows, cols], TN, preferred_element_type=F32))
            _accumulate(dgw_ref.at[hd], jnp.where(mask, dwm, 0.0), first)
            _accumulate(dgb_ref.at[hd], dbs, first)
            dvn_cols.append(jnp.concatenate(dvn_rows, axis=0))
        dvn = jnp.concatenate(dvn_cols, axis=1)
        _accumulate(dslg_ref, jnp.sum(dvn * vn0, axis=0, keepdims=True), first)
        _accumulate(dslb_ref, jnp.sum(dvn, axis=0, keepdims=True), first)
        dv = _ln_bwd(vrs, vn0, dvn * slg_ref[...])
        dp_ref[:, 7 * BW:8 * BW] = (dv * _gelu_grad(pv, thv)).astype(BF)

    const = lambda shp: pl.BlockSpec(shp, lambda i: (0,) * len(shp))
    return _pc(
        jobs, body, name="mix_branches_bwd", grid=(s_len // ts,),
        in_specs=(_halo_specs(ts, MIXC, s_len, True) + _halo_specs(ts, NBR * BW, s_len, True)
                  + _halo_specs(ts, BW, s_len, True)[1:] + [pl.BlockSpec(memory_space=pl.ANY)] + _branch_weights_specs()),
        out_specs=[pl.BlockSpec((ts, MIXC), lambda i: (i, 0))] + [const(s) for s in small_shapes],
        out_shape=[jax.ShapeDtypeStruct((s_len, INC), BF)] + [jax.ShapeDtypeStruct(s, F32) for s in small_shapes],
        scratch_shapes=[pltpu.VMEM((ext, MIXC), BF), pltpu.VMEM((ext, NBR * BW), BF)],
        input_output_aliases={8: 0},
        compiler_params=_params(1),
    )(p, p, p, dy, dy, dy, z, z, dp, pool_w, pool_scale, sconv_w, cconv_w, cln_g, cln_b, sln_g, sln_b, sgu_w, sgu_bt)


def _mix_out_fwd(x, y, p, w_up, w_out, jobs=()):
    s_len = x.shape[0]
    ts = min(TOKEN_TILE, s_len)

    def body(x_ref, y_ref, pg_ref, wu_ref, wo_ref, xo_ref, m_ref, up_ref):
        m = jnp.zeros((ts, D), F32)
        for gi in range(NBR):
            up = jnp.dot(y_ref[:, gi * BW:(gi + 1) * BW], wu_ref[gi], preferred_element_type=F32)
            up_ref[:, gi * D:(gi + 1) * D] = up.astype(BF)
            m = m + _sig(pg_ref[:, gi * D:(gi + 1) * D].astype(F32)) * up
        mb = m.astype(BF)
        m_ref[...] = mb
        xo_ref[...] = x_ref[...] + jnp.dot(mb, wo_ref[...], preferred_element_type=F32)

    return _pc(
        jobs, body, name="mix_out_fwd", grid=(s_len // ts,),
        in_specs=[_rows(ts, D), _rows(ts, NBR * BW), _rows(ts, NBR * D, 1), _full((NBR, BW, D)), _full((D, D))],
        out_specs=[_rows(ts, D), _rows(ts, D), _rows(ts, NBR * D)],
        out_shape=[jax.ShapeDtypeStruct((s_len, D), F32), jax.ShapeDtypeStruct((s_len, D), BF),
                   jax.ShapeDtypeStruct((s_len, NBR * D), BF)],
        compiler_params=_params(1),
    )(x, y, p, w_up, w_out)


def _mix_out_bwd(dxo, up, p, w_up, w_out, jobs=()):
    s_len = dxo.shape[0]
    ts = min(TOKEN_TILE, s_len)

    def body(dxo_ref, up_ref, pg_ref, wu_ref, wo_ref, dy_ref, dp_ref, dup_ref, dxb_ref):
        dxb = dxo_ref[...].astype(BF)
        dxb_ref[...] = dxb
        dm = lax.dot_general(dxb, wo_ref[...], NT, preferred_element_type=F32)
        for gi in range(NBR):
            cols = slice(gi * D, (gi + 1) * D)
            gate = _sig(pg_ref[:, cols].astype(F32))
            dp_ref[:, cols] = (dm * up_ref[:, cols].astype(F32) * gate * (1.0 - gate)).astype(BF)
            dup = (dm * gate).astype(BF)
            dup_ref[:, cols] = dup
            dy_ref[:, gi * BW:(gi + 1) * BW] = lax.dot_general(
                dup, wu_ref[gi], NT, preferred_element_type=F32).astype(BF)

    return _pc(
        jobs, body, name="mix_out_bwd", grid=(s_len // ts,),
        in_specs=[_rows(ts, D), _rows(ts, NBR * D), _rows(ts, NBR * D, 1), _full((NBR, BW, D)), _full((D, D))],
        out_specs=[_rows(ts, NBR * BW), _rows(ts, NBR * D, 1), _rows(ts, NBR * D), _rows(ts, D)],
        out_shape=[jax.ShapeDtypeStruct((s_len, NBR * BW), BF), jax.ShapeDtypeStruct((s_len, INC), BF),
                   jax.ShapeDtypeStruct((s_len, NBR * D), BF), jax.ShapeDtypeStruct((s_len, D), BF)],
        compiler_params=_params(1),
    )(dxo, up, p, w_up, w_out)


def _mix_in_bwd(dp, w_in, x, g, dxo, jobs=()):
    s_len = x.shape[0]
    ts = min(TOKEN_TILE, s_len)

    def body(dp_ref, w_ref, x_ref, g_ref, dxo_ref, dxi_ref, dg_ref):
        i = pl.program_id(0)
        dh = lax.dot_general(dp_ref[...], w_ref[...], NT, preferred_element_type=F32)
        dx, dg = _rms_bwd(x_ref[...], g_ref[...], dh)
        dxi_ref[...] = dxo_ref[...] + dx
        _accumulate(dg_ref, dg, i == 0)

    return _pc(
        jobs, body, name="mix_in_bwd", grid=(s_len // ts,),
        in_specs=[_rows(ts, INC), _full((D, INC)), _rows(ts, D), _full((1, D)), _rows(ts, D)],
        out_specs=[_rows(ts, D), pl.BlockSpec((1, D), lambda i: (0, 0))],
        out_shape=[jax.ShapeDtypeStruct((s_len, D), F32), jax.ShapeDtypeStruct((1, D), F32)],
        compiler_params=_params(1),
    )(dp, w_in, x, g, dxo)


def _loss_head(x, g, target, jobs=()):
    s_len = x.shape[0]
    ts = min(512, s_len)

    def body(x_ref, g_ref, t_ref, dx_ref, dg_ref, loss_ref):
        i = pl.program_id(0)
        xv = x_ref[...]
        _, xh = _rms_stats(xv)
        err = xh * g_ref[...] - t_ref[...]
        part = 0.5 * jnp.sum(jnp.mean(err * err, axis=-1, keepdims=True), axis=0, keepdims=True)
        dx, dg = _rms_bwd(xv, g_ref[...], err * (1.0 / D))
        dx_ref[...] = dx
        _accumulate(dg_ref, dg, i == 0)
        _accumulate(loss_ref, jnp.broadcast_to(part, (1, GW)), i == 0)

    return _pc(
        jobs, body, name="loss_head", grid=(s_len // ts,),
        in_specs=[_rows(ts, D), _full((1, D)), _rows(ts, D)],
        out_specs=[_rows(ts, D), pl.BlockSpec((1, D), lambda i: (0, 0)), pl.BlockSpec((1, GW), lambda i: (0, 0))],
        out_shape=[jax.ShapeDtypeStruct((s_len, D), F32), jax.ShapeDtypeStruct((1, D), F32),
                   jax.ShapeDtypeStruct((1, GW), F32)],
        compiler_params=_params(1),
    )(x, g, target)


SUM_TILE = 1 << 20
ADAM_TILE = 1 << 19
CAST_TILE = 1 << 19


def _row_tile(rows, cols, budget=1 << 18):
    tr = rows
    while tr * cols > budget and tr % 16 == 0:
        tr //= 2
    return tr


def _elementwise(fn, name, ins, out_dtypes):
    rows, cols = ins[0].shape
    tr = _row_tile(rows, cols)
    n_in = len(ins)

    def body(*refs):
        res = fn(*[r[...] for r in refs[:n_in]])
        for o_ref, val in zip(refs[n_in:], res):
            o_ref[...] = val.astype(o_ref.dtype)

    outs = pl.pallas_call(
        body, name=name, grid=(rows // tr,),
        in_specs=[_rows(tr, cols)] * n_in, out_specs=[_rows(tr, cols)] * len(out_dtypes),
        out_shape=[jax.ShapeDtypeStruct((rows, cols), dt) for dt in out_dtypes],
        compiler_params=_params(1),
    )(*ins)
    return outs


def _tiled(fn, name, grid, in_specs, out_specs, out_shape, args, scalars=None, alias=None):
    alias = alias or {}
    n_in = len(in_specs) - len(alias)
    n_pre = 0 if scalars is None else 1

    def body(*refs):
        refs = refs[n_pre:]
        res = fn(*[r[...] for r in refs[:n_in]])
        for o_ref, val in zip(refs[len(in_specs):], res):
            o_ref[...] = val.astype(o_ref.dtype)

    aliases = {n_pre + pos: out for pos, out in alias.items()}
    if scalars is None:
        return pl.pallas_call(body, name=name, grid=grid, in_specs=in_specs, out_specs=out_specs, out_shape=out_shape,
                              input_output_aliases=aliases, compiler_params=_params(len(grid)))(*args)
    spec = pltpu.PrefetchScalarGridSpec(num_scalar_prefetch=1, grid=grid, in_specs=in_specs, out_specs=out_specs)
    return pl.pallas_call(body, name=name, grid_spec=spec, out_shape=out_shape, input_output_aliases=aliases,
                          compiler_params=_params(len(grid)))(scalars, *args)


def _cast_into(shard, layer, col, dtype, sc):
    ks, ns = shard.shape[1:]
    tr = _row_tile(ks, ns, SUM_TILE)
    full = (ks, ns * NCHIP) if col else (ks * NCHIP, ns)
    out_idx = (lambda i, s: (i, s[1])) if col else (lambda i, s: (s[1] * (ks // tr) + i, 0))
    return _tiled(lambda v: (v,), "cast_into", (ks // tr,), [pl.BlockSpec((None, tr, ns), lambda i, s: (layer, i, 0))],
                  [pl.BlockSpec((tr, ns), out_idx)], [jax.ShapeDtypeStruct(full, dtype)], [shard], sc)[0]


def _cast_rest(items, jobs=()):
    n = len(items)
    tiles, slot_sets, counts = [], {}, {}
    for k, (shard, _, _, dt) in enumerate(items):
        ks, ns = shard.shape[1:]
        tr = _row_tile(ks, ns, CAST_TILE)
        which = slot_sets.setdefault((tr, ns, shard.dtype, dt), len(slot_sets))
        for t in range(ks // tr):
            tiles.append((k, t, tr, which, counts.get(which, 0) % 2))
            counts[which] = counts.get(which, 0) + 1
    n_sets = len(slot_sets)

    def body(*refs):
        ins, outs = refs[:n], refs[n:2 * n]
        bufs = refs[2 * n:2 * n + 2 * n_sets]
        in_sems, out_sems = refs[2 * n + 2 * n_sets:]
        chip = 2 * lax.axis_index("x") + lax.axis_index("y")
        fetch, store = [], []
        for k, t, tr, which, slot in tiles:
            shard, layer, col, _ = items[k]
            ks, ns = shard.shape[1:]
            if col:
                place = outs[k].at[pl.ds(t * tr, tr), pl.ds(pl.multiple_of(chip * ns, GW), ns)]
            else:
                place = outs[k].at[pl.ds(pl.multiple_of(chip * ks + t * tr, 16), tr), :]
            fetch.append(pltpu.make_async_copy(
                ins[k].at[layer, pl.ds(t * tr, tr), :], bufs[2 * which].at[slot], in_sems.at[which, slot]))
            store.append(pltpu.make_async_copy(bufs[2 * which + 1].at[slot], place, out_sems.at[which, slot]))
        busy = {}
        fetch[0].start()
        for i, (k, t, tr, which, slot) in enumerate(tiles):
            if i + 1 < len(tiles):
                fetch[i + 1].start()
            fetch[i].wait()
            if (which, slot) in busy:
                store[busy[which, slot]].wait()
            bufs[2 * which + 1][slot] = bufs[2 * which][slot].astype(bufs[2 * which + 1].dtype)
            store[i].start()
            busy[which, slot] = i
        for i in busy.values():
            store[i].wait()

    def full_shape(shard, col):
        ks, ns = shard.shape[1:]
        return (ks, ns * NCHIP) if col else (ks * NCHIP, ns)

    scratch = []
    for (tr, ns, dt_in, dt_out) in slot_sets:
        scratch += [pltpu.VMEM((2, tr, ns), dt_in), pltpu.VMEM((2, tr, ns), dt_out)]
    scratch += [pltpu.SemaphoreType.DMA((n_sets, 2)), pltpu.SemaphoreType.DMA((n_sets, 2))]
    return _pc(
        jobs, body, name="cast_rest", in_specs=[ANY] * n, out_specs=[ANY] * n,
        out_shape=[jax.ShapeDtypeStruct(full_shape(shard, col), dt) for shard, _, col, dt in items],
        scratch_shapes=scratch, compiler_params=pltpu.CompilerParams(vmem_limit_bytes=VMEM_LIMIT),
    )(*[shard for shard, _, _, _ in items])


def _pair_sum(g, got, col, sc):
    hk, hn = got.shape
    tr = _row_tile(hk, hn, SUM_TILE)
    g_idx = (lambda i, s: (s[0] * (hk // tr) + i, 0)) if col else (lambda i, s: (i, s[0]))
    plain = pl.BlockSpec((tr, hn), lambda i, s: (i, 0))
    return _tiled(lambda a, b: (a.astype(F32) + b.astype(F32),), "pair_sum", (hk // tr,),
                  [pl.BlockSpec((tr, hn), g_idx), plain], [plain], [jax.ShapeDtypeStruct((hk, hn), BF)], [g, got], sc)[0]


def _chip_sum(ph, got, col, layer, depth, sc, carry):
    qk, qn = got.shape[1:]
    tr = _row_tile(qk, qn, SUM_TILE)
    ph_idx = (lambda i, s: (i, s[1])) if col else (lambda i, s: (s[1] * (qk // tr) + i, 0))
    out_shape = (depth, 2 * qk, qn) if col else (depth, qk, 2 * qn)
    out_idx = (lambda i, s: (layer, s[0] * (qk // tr) + i, 0)) if col else (lambda i, s: (layer, i, s[0]))
    in_specs = [pl.BlockSpec((tr, qn), ph_idx)] + [pl.BlockSpec((None, tr, qn), lambda i, s, j=j: (j, i, 0)) for j in range(3)]
    args = [ph, got, got, got]
    if carry is not None:
        in_specs.append(ANY)
        args.append(carry)
    return _tiled(lambda a, b, c_, d_: (a.astype(F32) + b.astype(F32) + c_.astype(F32) + d_.astype(F32),), "chip_sum",
                  (qk // tr,), in_specs, [pl.BlockSpec((None, tr, qn), out_idx)],
                  [jax.ShapeDtypeStruct(out_shape, F32)], args, sc, alias=None if carry is None else {4: 0})[0]


def _adamw_math(w, g, m, v):
    m = ADAM_B1 * m + (1.0 - ADAM_B1) * g
    v = ADAM_B2 * v + (1.0 - ADAM_B2) * (g * g)
    m_hat = m / (1.0 - ADAM_B1 ** ADAM_STEP)
    v_hat = v / (1.0 - ADAM_B2 ** ADAM_STEP)
    delta = -ADAM_LR * (m_hat / (jnp.sqrt(v_hat) + ADAM_EPS) + ADAM_WD * w)
    return delta, m, v


def _adamw_layer(w, g, m, v, layer, carry):
    k, n = w.shape[1:]
    tr = _row_tile(k, n, ADAM_TILE)
    blk = pl.BlockSpec((None, tr, n), lambda i: (layer, i, 0))
    carry = list(carry or [])
    return _tiled(lambda *a: (*_adamw_math(*a), a[1]), "adamw_layer", (k // tr,), [blk] * 4 + [ANY] * len(carry),
                  [blk] * 4, [jax.ShapeDtypeStruct(w.shape, F32)] * 4, [w, g, m, v] + carry, None,
                  alias={4 + pos: pos for pos in range(len(carry))})


def _adamw(w, g, m, v):
    shape = w.shape
    two_d = lambda a: a.reshape(-1, shape[-1])
    outs = _elementwise(_adamw_math, "adamw", [two_d(w), two_d(g), two_d(m), two_d(v)], [F32, F32, F32])
    return [o.reshape(shape) for o in outs]


ANY = pl.BlockSpec(memory_space=pl.ANY)


def _place():
    x, y, c = lax.axis_index("x"), lax.axis_index("y"), lax.axis_index("c")
    chips = [(1 - x, y), (x, 1 - y), (1 - x, 1 - y)]
    return x, y, c, chips


def _cols(ref, start, size):
    idx = (slice(None),) * (len(ref.shape) - 1) + (pl.ds(pl.multiple_of(start, GW), size),)
    return ref.at[idx]


def _rows_of(ref, start, size):
    nd = len(ref.shape)
    idx = (slice(None),) * (nd - 2) + (pl.ds(pl.multiple_of(start, 16), size), slice(None))
    return ref.at[idx]


def _region(ref, col_sharded, chip, half):
    k, n = ref.shape
    align = 16 if ref.dtype == BF else 8
    if col_sharded:
        return ref.at[pl.ds(pl.multiple_of(half * (k // 2), align), k // 2),
                      pl.ds(pl.multiple_of(chip * (n // NCHIP), GW), n // NCHIP)]
    rows = k // (2 * NCHIP)
    return ref.at[pl.ds(pl.multiple_of((2 * chip + half) * rows, align), rows), :]


def _job_gather(bufs, col_sharded, handoff):
    n = len(bufs)

    def copies(outs, send_sems, recv_sems, stage):
        x, y, c, chips = _place()
        sends, lands = [], []
        for k in range(n):
            for j, chip in enumerate(chips):
                theirs = 2 * chip[0] + chip[1]
                if stage == 0:
                    src, to = _region(outs[k], col_sharded[k], 2 * x + y, c), (*chip, c)
                    land = _region(outs[k], col_sharded[k], theirs, c)
                else:
                    src, to = _region(outs[k], col_sharded[k], theirs, c), (x, y, 1 - c)
                    land = _region(outs[k], col_sharded[k], theirs, 1 - c)
                sem = 3 * n * stage + 3 * k + j
                sems = dict(send_sem=send_sems.at[sem], recv_sem=recv_sems.at[sem], device_id=to, device_id_type=MESH)
                sends.append(pltpu.make_async_remote_copy(src_ref=src, dst_ref=src, **sems))
                lands.append(pltpu.make_async_remote_copy(src_ref=land, dst_ref=land, **sems))
        return sends, lands

    def start(ins, outs, send_sems, recv_sems):
        for cp in copies(outs, send_sems, recv_sems, 0)[0]:
            cp.start()

    def hand_on(ins, outs, send_sems, recv_sems):
        for cp in copies(outs, send_sems, recv_sems, 0)[1]:
            cp.wait_recv()
        for cp in copies(outs, send_sems, recv_sems, 1)[0]:
            cp.start()

    def finish(ins, outs, send_sems, recv_sems):
        sends, lands = copies(outs, send_sems, recv_sems, 1)
        for cp in lands:
            cp.wait_recv()
        for cp in copies(outs, send_sems, recv_sems, 0)[0] + sends:
            cp.wait_send()

    return _Job(bufs, n, [], 6 * n, [(0.0, start), (handoff, hand_on), (1.0, finish)])


def _half(ref, col_sharded, c):
    k, n = ref.shape[-2:]
    return _rows_of(ref, c * (k // 2), k // 2) if col_sharded else _cols(ref, c * (n // 2), n // 2)


def _quarter(ref, col_sharded, j):
    k, n = ref.shape[-2:]
    return _cols(ref, j * (n // NCHIP), n // NCHIP) if col_sharded else _rows_of(ref, j * (k // NCHIP), k // NCHIP)


def _job_pair(grads, col_sharded):
    n = len(grads)

    def half_shape(g, col):
        return (g.shape[0] // 2, g.shape[1]) if col else (g.shape[0], g.shape[1] // 2)

    def copies(ins, got, send_sems, recv_sems):
        x, y, c, _ = _place()
        return [pltpu.make_async_remote_copy(
            src_ref=_half(ins[k], col_sharded[k], 1 - c), dst_ref=got[k], send_sem=send_sems.at[k],
            recv_sem=recv_sems.at[k], device_id=(x, y, 1 - c), device_id_type=MESH) for k in range(n)]

    def start(*refs):
        for cp in copies(*refs):
            cp.start()

    def finish(*refs):
        for cp in copies(*refs):
            cp.wait()

    fresh = [jax.ShapeDtypeStruct(half_shape(g, col), g.dtype) for g, col in zip(grads, col_sharded)]
    return _Job(grads, 0, fresh, n, [(0.0, start), (1.0, finish)])


def _job_chip(halves, col_sharded):
    n = len(halves)

    def quarter_shape(h, col):
        return (3, h.shape[0], h.shape[1] // NCHIP) if col else (3, h.shape[0] // NCHIP, h.shape[1])

    def copies(ins, got, send_sems, recv_sems):
        x, y, c, chips = _place()
        return [pltpu.make_async_remote_copy(
            src_ref=_quarter(ins[k], col_sharded[k], 2 * chip[0] + chip[1]), dst_ref=got[k].at[j],
            send_sem=send_sems.at[3 * k + j], recv_sem=recv_sems.at[3 * k + j], device_id=(*chip, c), device_id_type=MESH)
            for k in range(n) for j, chip in enumerate(chips)]

    def start(*refs):
        for cp in copies(*refs):
            cp.start()

    def finish(*refs):
        for cp in copies(*refs):
            cp.wait()

    fresh = [jax.ShapeDtypeStruct(quarter_shape(h, col), h.dtype) for h, col in zip(halves, col_sharded)]
    return _Job(halves, 0, fresh, 3 * n, [(0.0, start), (1.0, finish)])


def _job_sibling(shards, col_sharded, layers):
    n = len(shards)

    def copies(outs, send_sems, recv_sems):
        x, y, c, _ = _place()
        sends, lands = [], []
        for k in range(n):
            sems = dict(send_sem=send_sems.at[k], recv_sem=recv_sems.at[k], device_id=(x, y, 1 - c), device_id_type=MESH)
            mine = _half(outs[k].at[layers[k]], col_sharded[k], c)
            theirs = _half(outs[k].at[layers[k]], col_sharded[k], 1 - c)
            sends.append(pltpu.make_async_remote_copy(src_ref=mine, dst_ref=mine, **sems))
            lands.append(pltpu.make_async_remote_copy(src_ref=theirs, dst_ref=theirs, **sems))
        return sends, lands

    def start(ins, outs, send_sems, recv_sems):
        for cp in copies(outs, send_sems, recv_sems)[0]:
            cp.start()

    def finish(ins, outs, send_sems, recv_sems):
        sends, lands = copies(outs, send_sems, recv_sems)
        for cp in lands:
            cp.wait_recv()
        for cp in sends:
            cp.wait_send()

    return _Job(shards, n, [], n, [(0.0, start), (1.0, finish)])


def _standalone(jobs, name):
    return _pc(jobs, lambda: None, name=name, in_specs=[], out_specs=[], out_shape=[])()[1]


def _all_reduce_small(buf, jobs=()):
    rows = buf.shape[0]
    per = rows // NDEV
    flips = [(fx, fy, fc) for fx in (0, 1) for fy in (0, 1) for fc in (0, 1)][1:]

    def body(in_ref, out_ref, got_ref, send_sems, recv_sems):
        x, y, c, _ = _place()
        me = 4 * x + 2 * y + c

        def peer(f):
            return tuple(1 - pos if flip else pos for pos, flip in zip((x, y, c), f))

        def block(ref, dev):
            return ref.at[pl.ds(pl.multiple_of(dev * per, 8), per), :]

        scatter = []
        for k, f in enumerate(flips):
            px, py, pc = peer(f)
            scatter.append(pltpu.make_async_remote_copy(
                src_ref=block(in_ref, 4 * px + 2 * py + pc), dst_ref=got_ref.at[k], send_sem=send_sems.at[k],
                recv_sem=recv_sems.at[k], device_id=(px, py, pc), device_id_type=MESH))
        for cp in scatter:
            cp.start()
        for cp in scatter:
            cp.wait()
        total = block(in_ref, me)[...]
        for k in range(len(flips)):
            total = total + got_ref[k]
        block(out_ref, me)[...] = total
        share = []
        for k, f in enumerate(flips):
            share.append(pltpu.make_async_remote_copy(
                src_ref=block(out_ref, me), dst_ref=block(out_ref, me), send_sem=send_sems.at[7 + k],
                recv_sem=recv_sems.at[7 + k], device_id=peer(f), device_id_type=MESH))
        for cp in share:
            cp.start()
        for k, f in enumerate(flips):
            share[k].wait_send()
            px, py, pc = peer(f)
            theirs = block(out_ref, 4 * px + 2 * py + pc)
            pltpu.make_async_remote_copy(
                src_ref=theirs, dst_ref=theirs, send_sem=send_sems.at[7 + k], recv_sem=recv_sems.at[7 + k],
                device_id=(px, py, pc), device_id_type=MESH).wait_recv()

    vmem = pl.BlockSpec(memory_space=pltpu.VMEM)
    return _pc(
        jobs, body, name="all_reduce_small", in_specs=[vmem], out_specs=vmem,
        out_shape=jax.ShapeDtypeStruct((rows, GW), F32),
        scratch_shapes=[pltpu.VMEM((NDEV - 1, per, GW), F32), pltpu.SemaphoreType.DMA((14,)),
                        pltpu.SemaphoreType.DMA((14,))],
    )(buf)


BIG = ("ffn1_w13", "ffn1_w2", "w_in", "w_up", "w_out", "ffn2_w13", "ffn2_w2")
BIG_COL_SHARDED = (True, False, True, True, False, True, False)
SMALL = ("ffn1_norm", "mix_norm", "pool_w", "pool_scale", "sconv_w", "cconv_w", "cconv_ln_g", "cconv_ln_b",
         "sgu_ln_g", "sgu_ln_b", "sgu_w", "sgu_b", "ffn2_norm", "final_norm")
WEIGHTS = ("ffn1_norm", "ffn1_w13", "ffn1_w2", "mix_norm", "w_in", "pool_w", "pool_scale", "sconv_w", "cconv_w",
           "cconv_ln_g", "cconv_ln_b", "sgu_ln_g", "sgu_ln_b", "sgu_w", "sgu_b", "w_up", "w_out", "ffn2_norm",
           "ffn2_w13", "ffn2_w2", "final_norm")


def _pad_rows(a, rows):
    return jnp.pad(a, ((0, 0), (0, rows - a.shape[1]), (0, 0)))


def kernel(x, ffn1_norm, ffn1_w13, ffn1_w2, mix_norm, w_in, pool_w, pool_scale, sconv_w, cconv_w, cconv_ln_g, cconv_ln_b, sgu_ln_g, sgu_ln_b, sgu_w, sgu_b, w_up, w_out, ffn2_norm, ffn2_w13, ffn2_w2, final_norm, loss_target, m_ffn1_norm, m_ffn1_w13, m_ffn1_w2, m_mix_norm, m_w_in, m_pool_w, m_pool_scale, m_sconv_w, m_cconv_w, m_cconv_ln_g, m_cconv_ln_b, m_sgu_ln_g, m_sgu_ln_b, m_sgu_w, m_sgu_b, m_w_up, m_w_out, m_ffn2_norm, m_ffn2_w13, m_ffn2_w2, m_final_norm, v_ffn1_norm, v_ffn1_w13, v_ffn1_w2, v_mix_norm, v_w_in, v_pool_w, v_pool_scale, v_sconv_w, v_cconv_w, v_cconv_ln_g, v_cconv_ln_b, v_sgu_ln_g, v_sgu_ln_b, v_sgu_w, v_sgu_b, v_w_up, v_w_out, v_ffn2_norm, v_ffn2_w13, v_ffn2_w2, v_final_norm):
    args = dict(locals())
    w = {nm: args[nm] for nm in WEIGHTS}
    m = {nm: args["m_" + nm] for nm in WEIGHTS}
    v = {nm: args["v_" + nm] for nm in WEIGHTS}
    depth = ffn1_w13.shape[0]
    chip = 2 * lax.axis_index("x") + lax.axis_index("y")

    sc = jnp.stack([lax.axis_index("c"), chip]).astype(jnp.int32)
    col_of = dict(zip(BIG, BIG_COL_SHARDED), sconv_w=True, cconv_w=True)
    sources = {nm: (w[nm], BF) for nm in BIG}
    sources["w_up"] = (w_up.reshape(depth, NBR * BW, w_up.shape[-1]), BF)
    sources["sconv_w"] = (_pad_rows(sconv_w, 2 * SKP), F32)
    sources["cconv_w"] = (_pad_rows(cconv_w, 2 * CKP), F32)
    first_group = ("ffn1_w13", "ffn1_w2")
    full = [dict() for _ in range(depth)]
    for nm in first_group:
        full[0][nm] = _cast_into(sources[nm][0], 0, col_of[nm], sources[nm][1], sc)

    def gather_job(l, names, handoff):
        return _job_gather([full[l][nm] for nm in names], [col_of[nm] for nm in names], handoff)

    def gather_with(l, names, handoff, call):
        res, job_outs = call([gather_job(l, names, handoff)] if l < depth else [])
        if l < depth:
            full[l].update(zip(names, job_outs[0]))
        return res

    rest = [(nm, l) for l in range(depth) for nm in sources if not (l == 0 and nm in first_group)]
    casted = gather_with(0, first_group, 1.0, lambda jobs: _cast_rest(
        [(sources[nm][0], l, col_of[nm], sources[nm][1]) for nm, l in rest], jobs))
    for (nm, l), arr in zip(rest, casted):
        full[l][nm] = arr

    xs = x[0]
    row = lambda a: a.reshape(1, -1)
    saved = []
    for l in range(depth):
        g1, gm, g2 = row(ffn1_norm[l]), row(mix_norm[l]), row(ffn2_norm[l])
        x1, h1, s1, ab1 = gather_with(l, ("w_in",), 1.0, lambda jobs: _ffn_fwd(
            xs, g1, full[l]["ffn1_w13"], full[l]["ffn1_w2"], jobs))
        hm, p = gather_with(l, ("w_up", "w_out", "sconv_w", "cconv_w", "ffn2_w2"), 0.85, lambda jobs: _mix_in(
            x1, gm, full[l]["w_in"], jobs))
        branch = (pool_w[l], row(pool_scale[l]), full[l]["sconv_w"][:SKP], full[l]["cconv_w"][:CKP], row(cconv_ln_g[l]),
                  row(cconv_ln_b[l]), row(sgu_ln_g[l]), row(sgu_ln_b[l]), sgu_w[l], sgu_b[l].T)
        y, conv_z = gather_with(l, ("ffn2_w13",), 1.0, lambda jobs: _mix_branches_fwd(p, *branch, jobs=jobs))
        w_up_l = full[l]["w_up"].reshape(NBR, BW, D)
        x2, merged, up = gather_with(l + 1, ("ffn1_w2",), 0.8, lambda jobs: _mix_out_fwd(
            x1, y, p, w_up_l, full[l]["w_out"], jobs))
        x3, h2, s2, ab2 = gather_with(l + 1, ("ffn1_w13",), 1.0, lambda jobs: _ffn_fwd(
            x2, g2, full[l]["ffn2_w13"], full[l]["ffn2_w2"], jobs))
        lw = dict(g1=g1, gm=gm, g2=g2, w13a=full[l]["ffn1_w13"], w2a=full[l]["ffn1_w2"], w13b=full[l]["ffn2_w13"],
                  w2b=full[l]["ffn2_w2"], w_in=full[l]["w_in"], w_up=w_up_l, w_out=full[l]["w_out"], branch=branch)
        saved.append(dict(lw=lw, x0=xs, x1=x1, x2=x2, h1=h1, s1=s1, ab1=ab1, hm=hm, p=p, y=y, z=conv_z, merged=merged, up=up, h2=h2,
                          s2=s2, ab2=ab2))
        xs = x3

    (dx, d_final, loss_part), _ = _loss_head(xs, row(final_norm), loss_target[0])
    loss = lax.psum(loss_part[0, 0], ("x", "y", "c"))

    ici_us = dict(ffn1_w13=64, ffn1_w2=32, w_in=93, w_up=23, w_out=12, ffn2_w13=64, ffn2_w2=32)
    parts, pair_sums, reduced, big_updates, pending = {}, {}, {}, {}, []

    def take_jobs(budget_us):
        chosen = []
        for task in list(pending):
            kind, (nm, _) = task
            if kind == "chip":
                if ici_us[nm] > budget_us:
                    continue
                budget_us -= ici_us[nm]
            if kind == "sib" and any(k == "sib" and key[0] == nm for k, key in chosen):
                continue
            chosen.append(task)
            pending.remove(task)
        groups, jobs = [], []
        for kind in ("pair", "sib", "chip"):
            keys = [key for k, key in chosen if k == kind]
            if not keys:
                continue
            cols = [col_of[nm] for nm, _ in keys]
            groups.append((kind, keys))
            if kind == "pair":
                jobs.append(_job_pair([parts[key] for key in keys], cols))
            elif kind == "chip":
                jobs.append(_job_chip([pair_sums[key] for key in keys], cols))
            else:
                jobs.append(_job_sibling([reduced[nm] for nm, _ in keys], cols, [layer for _, layer in keys]))
        return groups, jobs

    def settle(groups, job_outs):
        for (kind, keys), outs in zip(groups, job_outs):
            for key, out in zip(keys, outs):
                nm, layer = key
                if kind == "pair":
                    pair_sums[key] = _pair_sum(parts[key], out, col_of[nm], sc)
                    pending.append(("chip", key))
                elif kind == "chip":
                    assert not any(k == "sib" and other[0] == nm for k, other in pending)
                    reduced[nm] = _chip_sum(pair_sums[key], out, col_of[nm], layer, depth, sc, reduced.get(nm))
                    pending.append(("sib", key))
                else:
                    reduced[nm] = out
                    as3 = lambda a: a.reshape(out.shape)
                    big_updates[nm] = _adamw_layer(as3(w[nm]), out, as3(m[nm]), as3(v[nm]), layer, big_updates.get(nm))

    def run(budget_us, call, carrier=True):
        groups, jobs = take_jobs(budget_us) if carrier else ([], [])
        res, job_outs = call(jobs)
        settle(groups, job_outs)
        return res

    def wgrad_done(key, partial):
        parts[key] = partial
        pending.append(("pair", key))

    small_parts = {nm: [None] * depth for nm in SMALL if nm != "final_norm"}
    for l in reversed(range(depth)):
        sv = saved[l]
        lw = sv["lw"]
        dx, dab, dyh, dg2 = run(100, lambda jobs: _ffn_bwd(
            dx, sv["x2"], lw["g2"], sv["ab2"], lw["w13b"], lw["w2b"], jobs))
        wgrad_done(("ffn2_w13", l), run(58, lambda jobs: _wgrad(sv["h2"], dab, D, 512, "wgrad_w13", jobs), False))
        wgrad_done(("ffn2_w2", l), run(33, lambda jobs: _wgrad(sv["s2"], dyh, 256, D, "wgrad_w2", jobs), False))
        small_parts["ffn2_norm"][l] = dg2

        dy, dp, dup, dxb = run(70, lambda jobs: _mix_out_bwd(dx, sv["up"], sv["p"], lw["w_up"], lw["w_out"], jobs))
        wgrad_done(("w_out", l), run(16, lambda jobs: _wgrad(sv["merged"], dxb, D, 512, "wgrad_w_out", jobs), False))
        wgrad_done(("w_up", l), run(25, lambda jobs: _wgrad_groups(sv["y"], dup, BW, D, "wgrad_w_up", jobs), False))
        (dp, d_pool_w, d_pool_scale, d_sconv, d_cconv, d_clg, d_clb, d_slg, d_slb, d_sgu_w, d_sgu_b) = run(
            130, lambda jobs: _mix_branches_bwd(sv["p"], dy, sv["z"], dp, *lw["branch"], jobs=jobs))
        wgrad_done(("w_in", l), run(80, lambda jobs: _wgrad(sv["hm"], dp, D, 512, "wgrad_w_in", jobs)))
        dx, dgm = run(91, lambda jobs: _mix_in_bwd(dp, lw["w_in"], sv["x1"], lw["gm"], dx, jobs))
        small_parts["mix_norm"][l] = dgm
        small_parts["pool_w"][l] = d_pool_w
        small_parts["pool_scale"][l] = d_pool_scale
        small_parts["sconv_w"][l] = d_sconv[:SK]
        small_parts["cconv_w"][l] = d_cconv[:CK]
        small_parts["cconv_ln_g"][l] = d_clg
        small_parts["cconv_ln_b"][l] = d_clb
        small_parts["sgu_ln_g"][l] = d_slg
        small_parts["sgu_ln_b"][l] = d_slb
        small_parts["sgu_w"][l] = d_sgu_w
        small_parts["sgu_b"][l] = jnp.sum(d_sgu_b, axis=-1)

        dx, dab, dyh, dg1 = run(100, lambda jobs: _ffn_bwd(
            dx, sv["x0"], lw["g1"], sv["ab1"], lw["w13a"], lw["w2a"], jobs))
        wgrad_done(("ffn1_w2", l), run(33, lambda jobs: _wgrad(sv["s1"], dyh, 256, D, "wgrad_w2", jobs), l == 0))
        wgrad_done(("ffn1_w13", l), run(58, lambda jobs: _wgrad(sv["h1"], dab, D, 512, "wgrad_w13", jobs), l == 0))
        small_parts["ffn1_norm"][l] = dg1
    grad_x = dx[None]

    small_local = {nm: jnp.stack(parts).reshape(depth, *w[nm].shape[1:-1], -1) if nm not in ("sconv_w", "cconv_w")
                   else jnp.stack(parts) for nm, parts in small_parts.items()}
    small_local["final_norm"] = d_final.reshape(-1)
    sizes = [small_local[nm].size for nm in SMALL]
    total = sum(sizes)
    pad_to = NDEV * 8 * GW
    padded = -(-total // pad_to) * pad_to
    packed = jnp.concatenate([small_local[nm].reshape(-1) for nm in SMALL] + [jnp.zeros((padded - total,), F32)])
    groups, jobs = take_jobs(float("inf"))
    summed, job_outs = _all_reduce_small(packed.reshape(-1, GW), jobs)
    settle(groups, job_outs)
    summed = summed.reshape(-1)
    flushes = 0
    while pending:
        groups, jobs = take_jobs(float("inf"))
        settle(groups, _standalone(jobs, "grad_flush_%d" % flushes))
        flushes += 1
    big_grads = {nm: big_updates[nm][3].reshape(w[nm].shape) for nm in BIG}
    small_grads, off = {}, 0
    for nm, size in zip(SMALL, sizes):
        small_grads[nm] = summed[off:off + size].reshape(small_local[nm].shape)
        off += size
    for nm in ("sconv_w", "cconv_w"):
        small_grads[nm] = lax.dynamic_slice_in_dim(small_grads[nm], chip * GW, GW, axis=2)

    grads = {**big_grads, **small_grads}

    delta, new_m, new_v = {}, {}, {}
    for nm in BIG:
        delta[nm], new_m[nm], new_v[nm] = [a.reshape(w[nm].shape) for a in big_updates[nm][:3]]
    s_sizes = [w[nm].size for nm in SMALL]
    s_total = sum(s_sizes)
    s_padded = -(-s_total // (8 * GW)) * (8 * GW)

    def pack(tree):
        return jnp.concatenate([tree[nm].reshape(-1) for nm in SMALL] + [jnp.ones((s_padded - s_total,), F32)]).reshape(-1, GW)

    packed_out = _adamw(pack(w), pack(grads), pack(m), pack(v))
    off = 0
    for nm, size in zip(SMALL, s_sizes):
        for tree, arr in zip((delta, new_m, new_v), packed_out):
            tree[nm] = arr.reshape(-1)[off:off + size].reshape(w[nm].shape)
        off += size

    return (loss, grad_x, *[grads[nm] for nm in WEIGHTS], *[delta[nm] for nm in WEIGHTS],
            *[new_m[nm] for nm in WEIGHTS], *[new_v[nm] for nm in WEIGHTS])
```

```python
import jax
import jax.numpy as jnp
from jax import lax
from jax.experimental import pallas as pl
from jax.experimental.pallas import tpu as pltpu

D = 1024
FF = 2816
BW = 512
NBR = 4
MIXC = 4096
INC = 8192
GW = 128
CHUNK = 64
SK = 3
CK = 31
SKP = 8
CKP = 32
HALO = 32
TOKEN_TILE = 256
BRANCH_TILE = 512
EPS = 1e-6
NCHIP = 4
NDEV = 8

ADAM_LR = 0.001
ADAM_B1 = 0.9
ADAM_B2 = 0.999
ADAM_EPS = 1e-08
ADAM_WD = 0.01
ADAM_STEP = 10

VMEM_LIMIT = 56 * 1024 * 1024

BF = jnp.bfloat16
F32 = jnp.float32
MESH = pl.DeviceIdType.MESH
NT = (((1,), (1,)), ((), ()))
TN = (((0,), (0,)), ((), ()))


def _params(n_axes):
    return pltpu.CompilerParams(dimension_semantics=("arbitrary",) * n_axes, vmem_limit_bytes=VMEM_LIMIT)


class _Job:
    def __init__(self, args, n_inplace, fresh, nsem, phases):
        self.args, self.n_inplace, self.fresh, self.nsem, self.phases = list(args), n_inplace, list(fresh), nsem, phases


def _pc(jobs, body, name, in_specs, out_specs, out_shape, grid=(), scratch_shapes=(), input_output_aliases=None,
        compiler_params=None, grid_spec_scalars=None):
    single = not isinstance(out_shape, (list, tuple))
    core_out_specs = [out_specs] if single else list(out_specs)
    core_out_shape = [out_shape] if single else list(out_shape)
    n_in, n_out, n_scr = len(in_specs), len(core_out_specs), len(scratch_shapes)
    n_pre = 0 if grid_spec_scalars is None else 1
    all_in, all_out, all_shape = list(in_specs), list(core_out_specs), list(core_out_shape)
    all_scr, aliases, extra_args, layout = list(scratch_shapes), dict(input_output_aliases or {}), [], []
    for job in jobs:
        n_job_out = job.n_inplace + len(job.fresh)
        layout.append((len(all_in), len(job.args), len(all_out), n_job_out, len(all_scr)))
        for a in range(job.n_inplace):
            aliases[n_pre + len(all_in) + a] = len(all_out) + a
        all_in += [ANY] * len(job.args)
        extra_args += job.args
        all_out += [ANY] * n_job_out
        all_shape += [jax.ShapeDtypeStruct(a.shape, a.dtype) for a in job.args[:job.n_inplace]] + job.fresh
        all_scr += [pltpu.SemaphoreType.DMA((job.nsem,)), pltpu.SemaphoreType.DMA((job.nsem,))]
    steps = 1
    for extent in grid:
        steps *= extent
    events = []
    for (i0, na, o0, no, s0), job in zip(layout, jobs):
        for frac, fn in job.phases:
            events.append((min(int(frac * steps), steps - 1), frac >= 1.0, len(events), fn, (i0, na, o0, no, s0)))
    events.sort(key=lambda e: e[:3])

    def wrapped(*refs):
        pre, refs = refs[:n_pre], refs[n_pre:]
        ins, outs, scr = refs[:len(all_in)], refs[len(all_in):len(all_in) + len(all_out)], refs[len(all_in) + len(all_out):]
        step = 0
        for axis, extent in enumerate(grid):
            step = step * extent + pl.program_id(axis)

        def emit(event):
            at, _, _, fn, (i0, na, o0, no, s0) = event
            run = lambda: fn(ins[i0:i0 + na], outs[o0:o0 + no], scr[s0], scr[s0 + 1])
            if steps == 1:
                run()
            else:
                pl.when(step == at)(run)

        for event in events:
            if not event[1]:
                emit(event)
        body(*pre, *ins[:n_in], *outs[:n_out], *scr[:n_scr])
        for event in events:
            if event[1]:
                emit(event)

    kwargs = dict(name=name, out_shape=all_shape, input_output_aliases=aliases)
    if compiler_params is not None:
        kwargs["compiler_params"] = compiler_params
    if grid_spec_scalars is not None:
        kwargs["grid_spec"] = pltpu.PrefetchScalarGridSpec(
            num_scalar_prefetch=1, grid=grid, in_specs=all_in, out_specs=all_out, scratch_shapes=all_scr)
    else:
        kwargs.update(in_specs=all_in, out_specs=all_out, scratch_shapes=all_scr)
        if grid:
            kwargs["grid"] = grid
    call = pl.pallas_call(wrapped, **kwargs)

    def run_call(*args):
        pre_args = [] if grid_spec_scalars is None else [grid_spec_scalars]
        res = list(call(*pre_args, *args, *extra_args))
        core = res[0] if single else res[:n_out]
        job_outs = [res[o0:o0 + no] for (_, _, o0, no, _) in layout]
        return core, job_outs

    return run_call


def _full(shape):
    nd = len(shape)
    return pl.BlockSpec(shape, lambda *_: (0,) * nd, pipeline_mode=pl.Buffered(1))


def _rows(ts, width, col=0):
    return pl.BlockSpec((ts, width), lambda i: (i, col))


def _sig(v):
    return jax.nn.sigmoid(v)


def _rms_stats(x):
    r = lax.rsqrt(jnp.mean(x * x, axis=-1, keepdims=True) + EPS)
    return r, x * r


def _rms_bwd(x, g, dh):
    r, xh = _rms_stats(x)
    dg = jnp.sum(dh * xh, axis=0, keepdims=True)
    dxh = dh * g
    dx = r * (dxh - xh * jnp.mean(dxh * xh, axis=-1, keepdims=True))
    return dx, dg


def _accumulate(ref, val, first):
    @pl.when(first)
    def _():
        ref[...] = val

    @pl.when(jnp.logical_not(first))
    def _():
        ref[...] += val


def _ffn_fwd(x, g, w13, w2, jobs=()):
    s_len = x.shape[0]
    ts = min(TOKEN_TILE, s_len)

    def body(x_ref, g_ref, w13_ref, w2_ref, xo_ref, h_ref, s_ref, dsab_ref):
        xv = x_ref[...]
        r, xh = _rms_stats(xv)
        h = (xh * g_ref[...]).astype(BF)
        h_ref[...] = h
        ab = jnp.dot(h, w13_ref[...], preferred_element_type=F32)
        a = ab[:, :FF]
        b = ab[:, FF:]
        sg = _sig(a)
        sil = a * sg
        dsab_ref[:, :FF] = (b * (sg * (1.0 + a * (1.0 - sg)))).astype(BF)
        dsab_ref[:, FF:] = sil.astype(BF)
        s = (sil * b).astype(BF)
        s_ref[...] = s
        xo_ref[...] = xv + 0.5 * jnp.dot(s, w2_ref[...], preferred_element_type=F32)

    return _pc(
        jobs, body, name="ffn_fwd", grid=(s_len // ts,),
        in_specs=[_rows(ts, D), _full((1, D)), _full((D, 2 * FF)), _full((FF, D))],
        out_specs=[_rows(ts, D), _rows(ts, D), _rows(ts, FF), _rows(ts, 2 * FF)],
        out_shape=[jax.ShapeDtypeStruct((s_len, D), F32), jax.ShapeDtypeStruct((s_len, D), BF),
                   jax.ShapeDtypeStruct((s_len, FF), BF), jax.ShapeDtypeStruct((s_len, 2 * FF), BF)],
        compiler_params=_params(1),
    )(x, g, w13, w2)


def _ffn_bwd(dxo, x, g, dsab, w13, w2, jobs=()):
    s_len = x.shape[0]
    ts = min(TOKEN_TILE, s_len)

    def body(dxo_ref, x_ref, g_ref, dsab_ref, w13_ref, w2_ref, dxi_ref, dab_ref, dy_ref, dg_ref):
        i = pl.program_id(0)
        dxo_v = dxo_ref[...]
        dy = (0.5 * dxo_v).astype(BF)
        dy_ref[...] = dy
        ds = lax.dot_general(dy, w2_ref[...], NT, preferred_element_type=F32)
        dab_ref[:, :FF] = (ds * dsab_ref[:, :FF].astype(F32)).astype(BF)
        dab_ref[:, FF:] = (ds * dsab_ref[:, FF:].astype(F32)).astype(BF)
        dh = lax.dot_general(dab_ref[...], w13_ref[...], NT, preferred_element_type=F32)
        dx, dg = _rms_bwd(x_ref[...], g_ref[...], dh)
        dxi_ref[...] = dxo_v + dx
        _accumulate(dg_ref, dg, i == 0)

    return _pc(
        jobs, body, name="ffn_bwd", grid=(s_len // ts,),
        in_specs=[_rows(ts, D), _rows(ts, D), _full((1, D)), _rows(ts, 2 * FF), _full((D, 2 * FF)), _full((FF, D))],
        out_specs=[_rows(ts, D), _rows(ts, 2 * FF), _rows(ts, D), pl.BlockSpec((1, D), lambda i: (0, 0))],
        out_shape=[jax.ShapeDtypeStruct((s_len, D), F32), jax.ShapeDtypeStruct((s_len, 2 * FF), BF),
                   jax.ShapeDtypeStruct((s_len, D), BF), jax.ShapeDtypeStruct((1, D), F32)],
        compiler_params=_params(1),
    )(dxo, x, g, dsab, w13, w2)


def _wgrad(a, b, tk, tn, name, jobs=()):
    s_len, k = a.shape
    n = b.shape[1]

    def body(a_ref, b_ref, o_ref):
        o_ref[...] = lax.dot_general(a_ref[...], b_ref[...], TN, preferred_element_type=F32).astype(BF)

    return _pc(
        jobs, body, name=name, grid=(k // tk, n // tn),
        in_specs=[pl.BlockSpec((s_len, tk), lambda i, j: (0, i)), pl.BlockSpec((s_len, tn), lambda i, j: (0, j))],
        out_specs=pl.BlockSpec((tk, tn), lambda i, j: (i, j)),
        out_shape=jax.ShapeDtypeStruct((k, n), BF),
        compiler_params=_params(2),
    )(a, b)


def _wgrad_groups(a, b, ka, nb, name, jobs=()):
    s_len = a.shape[0]
    groups = a.shape[1] // ka

    def body(a_ref, b_ref, o_ref):
        o_ref[...] = lax.dot_general(a_ref[...], b_ref[...], TN, preferred_element_type=F32).astype(BF)

    return _pc(
        jobs, body, name=name, grid=(groups,),
        in_specs=[pl.BlockSpec((s_len, ka), lambda gi: (0, gi)), pl.BlockSpec((s_len, nb), lambda gi: (0, gi))],
        out_specs=pl.BlockSpec((ka, nb), lambda gi: (gi, 0)),
        out_shape=jax.ShapeDtypeStruct((groups * ka, nb), BF),
        compiler_params=_params(1),
    )(a, b)


def _mix_in(x, g, w_in, jobs=()):
    s_len = x.shape[0]
    ts = min(TOKEN_TILE, s_len)

    def body(x_ref, g_ref, w_ref, h_ref, p_ref):
        _, xh = _rms_stats(x_ref[...])
        h = (xh * g_ref[...]).astype(BF)
        h_ref[...] = h
        p_ref[...] = jnp.dot(h, w_ref[...], preferred_element_type=F32).astype(BF)

    return _pc(
        jobs, body, name="mix_in", grid=(s_len // ts,),
        in_specs=[_rows(ts, D), _full((1, D)), _full((D, INC))],
        out_specs=[_rows(ts, D), _rows(ts, INC)],
        out_shape=[jax.ShapeDtypeStruct((s_len, D), BF), jax.ShapeDtypeStruct((s_len, INC), BF)],
        compiler_params=_params(1),
    )(x, g, w_in)


def _shift(e, j):
    n = e.shape[0]
    j = j % n
    return e if j == 0 else pltpu.roll(e, j, 0)


def _ln_stats(z):
    mu = jnp.mean(z, axis=-1, keepdims=True)
    zc = z - mu
    rs = lax.rsqrt(jnp.mean(zc * zc, axis=-1, keepdims=True) + EPS)
    return rs, zc * rs


def _ln_bwd(rs, zn, dzn):
    return rs * (dzn - jnp.mean(dzn, axis=-1, keepdims=True) - zn * jnp.mean(dzn * zn, axis=-1, keepdims=True))


_GELU_C0 = 0.7978845608028654
_GELU_C1 = 0.044715


def _gelu(p):
    th = jnp.tanh(_GELU_C0 * (p + _GELU_C1 * p * p * p))
    return 0.5 * p * (1.0 + th), th


def _gelu_grad(p, th):
    return 0.5 * (1.0 + th) + 0.5 * p * (1.0 - th * th) * (_GELU_C0 * (1.0 + 3.0 * _GELU_C1 * p * p))


def _pool_diff(a, t, sign):
    outs = []
    for gi in range(NBR):
        win = 2 ** (gi + 1)
        ag = a[:, gi * GW:(gi + 1) * GW]
        cnt = jnp.clip(t + 1, 1, win).astype(F32)
        ws = ag if sign > 0 else ag / cnt
        for s in range(gi + 1):
            ws = ws + _shift(ws, sign * (2 ** s))
        outs.append((ws / cnt if sign > 0 else ws) - ag)
    return outs


def _sgu_mask():
    row = lax.broadcasted_iota(jnp.int32, (GW, GW), 0)
    col = lax.broadcasted_iota(jnp.int32, (GW, GW), 1)
    return (col // CHUNK) <= (row // CHUNK)


def _assemble(pe_ref, prev_ref, cur_ref, next_ref, i, last, ts):
    pe_ref[0:HALO, :] = jnp.where(i > 0, prev_ref[...], jnp.zeros_like(prev_ref))
    pe_ref[HALO:HALO + ts, :] = cur_ref[...]
    if next_ref is not None:
        pe_ref[HALO + ts:, :] = jnp.where(i < last, next_ref[...], jnp.zeros_like(next_ref))


def _halo_specs(ts, width, s_len, with_next):
    per = ts // HALO
    specs = [pl.BlockSpec((HALO, width), lambda i: (jnp.maximum(i * per - 1, 0), 0)),
             pl.BlockSpec((ts, width), lambda i: (i, 0))]
    if with_next:
        specs.append(pl.BlockSpec((HALO, width), lambda i: (jnp.minimum((i + 1) * per, s_len // HALO - 1), 0)))
    return specs


def _branch_weights_specs():
    return [_full((NBR, GW, GW)), _full((1, BW)), _full((SKP, BW)), _full((CKP, BW)), _full((1, BW)), _full((1, BW)),
            _full((1, BW)), _full((1, BW)), _full((NBR, GW, GW)), _full((GW, NBR))]


def _mix_branches_fwd(p, pool_w, pool_scale, sconv_w, cconv_w, cln_g, cln_b, sln_g, sln_b, sgu_w, sgu_bt, jobs=()):
    s_len = p.shape[0]
    ts = min(BRANCH_TILE, s_len)
    ext = HALO + ts

    def body(pp_ref, pc_ref, pw_ref, ps_ref, sw_ref, cw_ref, clg_ref, clb_ref, slg_ref, slb_ref, gw_ref, gb_ref,
             y_ref, z_ref, pe_ref):
        i = pl.program_id(0)
        _assemble(pe_ref, pp_ref, pc_ref, None, i, 0, ts)
        t = i * ts - HALO + lax.broadcasted_iota(jnp.int32, (ext, 1), 0)

        dgs = _pool_diff(pe_ref[:, 0:BW].astype(F32), t, 1)
        for gi in range(NBR):
            e = jnp.dot(dgs[gi][HALO:].astype(BF), pw_ref[gi].astype(BF), preferred_element_type=F32)
            y_ref[:, gi * GW:(gi + 1) * GW] = (e * ps_ref[:, gi * GW:(gi + 1) * GW]).astype(BF)

        xin = pe_ref[:, BW:2 * BW].astype(F32)
        cg = pe_ref[:, 3 * BW:4 * BW].astype(F32)
        q = cg * xin
        cv = sw_ref[2:3, :] * q + sw_ref[1:2, :] * _shift(q, 1) + sw_ref[0:1, :] * _shift(q, 2)
        y_ref[:, BW:2 * BW] = (pe_ref[HALO:, 2 * BW:3 * BW].astype(F32) * cv[HALO:]).astype(BF)

        yg = pe_ref[:, 4 * BW:5 * BW].astype(F32) * _sig(pe_ref[:, 5 * BW:6 * BW].astype(F32))
        z = cw_ref[CK - 1:CK, :] * yg
        for j in range(1, CK):
            z = z + cw_ref[CK - 1 - j:CK - j, :] * _shift(yg, j)
        z_ref[...] = z[HALO:].astype(BF)
        _, zn = _ln_stats(z[HALO:])
        nn = zn * clg_ref[...] + clb_ref[...]
        y_ref[:, 2 * BW:3 * BW] = (nn * _sig(nn)).astype(BF)

        u, _ = _gelu(pc_ref[:, 6 * BW:7 * BW].astype(F32))
        v, _ = _gelu(pc_ref[:, 7 * BW:8 * BW].astype(F32))
        _, vn = _ln_stats(v)
        vn = (vn * slg_ref[...] + slb_ref[...]).astype(BF)
        mask = _sgu_mask()
        for hd in range(NBR):
            wm = jnp.where(mask, gw_ref[hd], 0.0).astype(BF)
            for blk in range(ts // GW):
                rows = slice(blk * GW, (blk + 1) * GW)
                cols = slice(hd * GW, (hd + 1) * GW)
                zz = jnp.dot(wm, vn[rows, cols], preferred_element_type=F32) + gb_ref[:, hd:hd + 1]
                y_ref[rows, 3 * BW + hd * GW:3 * BW + (hd + 1) * GW] = (u[rows, cols] * zz).astype(BF)

    return _pc(
        jobs, body, name="mix_branches_fwd", grid=(s_len // ts,),
        in_specs=_halo_specs(ts, MIXC, s_len, False) + _branch_weights_specs(),
        out_specs=[_rows(ts, NBR * BW), _rows(ts, BW)],
        out_shape=[jax.ShapeDtypeStruct((s_len, NBR * BW), BF), jax.ShapeDtypeStruct((s_len, BW), BF)],
        scratch_shapes=[pltpu.VMEM((ext, MIXC), BF)],
        compiler_params=_params(1),
    )(p, p, pool_w, pool_scale, sconv_w, cconv_w, cln_g, cln_b, sln_g, sln_b, sgu_w, sgu_bt)


def _mix_branches_bwd(p, dy, z, dp, pool_w, pool_scale, sconv_w, cconv_w, cln_g, cln_b, sln_g, sln_b, sgu_w, sgu_bt,
                      jobs=()):
    s_len = p.shape[0]
    ts = min(BRANCH_TILE, s_len)
    ext = ts + 2 * HALO
    last = s_len // ts - 1
    tile = slice(HALO, HALO + ts)
    small_shapes = [(NBR, GW, GW), (1, BW), (SKP, BW), (CKP, BW), (1, BW), (1, BW), (1, BW), (1, BW), (NBR, GW, GW),
                    (NBR, GW, GW)]

    def body(pp_ref, pc_ref, pn_ref, dyp_ref, dyc_ref, dyn_ref, zc_ref, zn_ref, dpin_ref,
             pw_ref, ps_ref, sw_ref, cw_ref, clg_ref, clb_ref, slg_ref, slb_ref, gw_ref, gb_ref,
             dp_ref, dpw_ref, dps_ref, dsw_ref, dcw_ref, dclg_ref, dclb_ref, dslg_ref, dslb_ref, dgw_ref, dgb_ref,
             pe_ref, de_ref):
        del dyp_ref, dpin_ref
        i = pl.program_id(0)
        first = i == 0
        _assemble(pe_ref, pp_ref, pc_ref, pn_ref, i, last, ts)
        de_ref[0:HALO, :] = jnp.zeros((HALO, NBR * BW), BF)
        de_ref[HALO:HALO + ts, :] = dyc_ref[...]
        de_ref[HALO + ts:, :] = jnp.where(i < last, dyn_ref[...], jnp.zeros_like(dyn_ref))
        t = i * ts - HALO + lax.broadcasted_iota(jnp.int32, (ext, 1), 0)

        @pl.when(first)
        def _():
            dsw_ref[...] = jnp.zeros((SKP, BW), F32)
            dcw_ref[...] = jnp.zeros((CKP, BW), F32)

        dgs = _pool_diff(pe_ref[:, 0:BW].astype(F32), t, 1)
        dya = de_ref[:, 0:BW].astype(F32)
        dds = []
        for gi in range(NBR):
            cols = slice(gi * GW, (gi + 1) * GW)
            pw = pw_ref[gi].astype(BF)
            d_t = dgs[gi][tile].astype(BF)
            e = jnp.dot(d_t, pw, preferred_element_type=F32)
            _accumulate(dps_ref.at[:, cols], jnp.sum(dya[tile, cols] * e, axis=0, keepdims=True), first)
            de_g = (dya[:, cols] * ps_ref[:, cols]).astype(BF)
            _accumulate(dpw_ref.at[gi], lax.dot_general(d_t, de_g[tile], TN, preferred_element_type=F32), first)
            dds.append(lax.dot_general(de_g, pw, NT, preferred_element_type=F32))
        das = _pool_diff(jnp.concatenate(dds, axis=1), t, -1)
        for gi in range(NBR):
            dp_ref[:, gi * GW:(gi + 1) * GW] = das[gi][tile].astype(BF)

        xin = pe_ref[:, BW:2 * BW].astype(F32)
        bg = pe_ref[:, 2 * BW:3 * BW].astype(F32)
        cg = pe_ref[:, 3 * BW:4 * BW].astype(F32)
        q = cg * xin
        qs = [q, _shift(q, 1), _shift(q, 2)]
        cv = sw_ref[2:3, :] * qs[0] + sw_ref[1:2, :] * qs[1] + sw_ref[0:1, :] * qs[2]
        dyb = de_ref[:, BW:2 * BW].astype(F32)
        dcv = dyb * bg
        for j in range(SK):
            dsw_ref[SK - 1 - j:SK - j, :] += jnp.sum(dcv[tile] * qs[j][tile], axis=0, keepdims=True)
        dq = sw_ref[2:3, :] * dcv + sw_ref[1:2, :] * _shift(dcv, -1) + sw_ref[0:1, :] * _shift(dcv, -2)
        dp_ref[:, BW:2 * BW] = (dq * cg)[tile].astype(BF)
        dp_ref[:, 2 * BW:3 * BW] = (dyb * cv)[tile].astype(BF)
        dp_ref[:, 3 * BW:4 * BW] = (dq * xin)[tile].astype(BF)

        ca = pc_ref[:, 4 * BW:5 * BW].astype(F32)
        sb = _sig(pc_ref[:, 5 * BW:6 * BW].astype(F32))
        yg_t = ca * sb
        z = jnp.concatenate([zc_ref[...].astype(F32), zn_ref[...].astype(F32)], axis=0)
        rs, zn = _ln_stats(z)
        nn = zn * clg_ref[...] + clb_ref[...]
        sn = _sig(nn)
        dn = de_ref[HALO:, 2 * BW:3 * BW].astype(F32) * (sn * (1.0 + nn * (1.0 - sn)))
        _accumulate(dclg_ref, jnp.sum((dn * zn)[:ts], axis=0, keepdims=True), first)
        _accumulate(dclb_ref, jnp.sum(dn[:ts], axis=0, keepdims=True), first)
        dz = _ln_bwd(rs, zn, dn * clg_ref[...])
        dyg = cw_ref[CK - 1:CK, :] * dz[:ts]
        dcw_ref[CK - 1:CK, :] += jnp.sum(dz[:ts] * yg_t, axis=0, keepdims=True)
        for j in range(1, CK):
            dz_ahead = _shift(dz, -j)[:ts]
            dyg = dyg + cw_ref[CK - 1 - j:CK - j, :] * dz_ahead
            dcw_ref[CK - 1 - j:CK - j, :] += jnp.sum(dz_ahead * yg_t, axis=0, keepdims=True)
        dp_ref[:, 4 * BW:5 * BW] = (dyg * sb).astype(BF)
        dp_ref[:, 5 * BW:6 * BW] = (dyg * ca * sb * (1.0 - sb)).astype(BF)

        pu = pc_ref[:, 6 * BW:7 * BW].astype(F32)
        pv = pc_ref[:, 7 * BW:8 * BW].astype(F32)
        u, thu = _gelu(pu)
        v, thv = _gelu(pv)
        vrs, vn0 = _ln_stats(v)
        vn = (vn0 * slg_ref[...] + slb_ref[...]).astype(BF)
        dyd = dyc_ref[:, 3 * BW:4 * BW].astype(F32)
        dzz = dyd * u
        dzb = dzz.astype(BF)
        mask = _sgu_mask()
        dvn_cols = []
        for hd in range(NBR):
            cols = slice(hd * GW, (hd + 1) * GW)
            wm = jnp.where(mask, gw_ref[hd], 0.0).astype(BF)
            dwm = jnp.zeros((GW, GW), F32)
            dbs = jnp.zeros((GW, GW), F32)
            dvn_rows = []
            for blk in range(ts // GW):
                rows = slice(blk * GW, (blk + 1) * GW)
                zz = jnp.dot(wm, vn[rows, cols], preferred_element_type=F32) + gb_ref[:, hd:hd + 1]
                dp_ref[rows, 6 * BW + hd * GW:6 * BW + (hd + 1) * GW] = (
                    dyd[rows, cols] * zz * _gelu_grad(pu[rows, cols], thu[rows, cols])).astype(BF)
                dwm = dwm + lax.dot_general(dzb[rows, cols], vn[rows, cols], NT, preferred_element_type=F32)
                dbs = dbs + dzz[rows, cols]
                dvn_rows.append(lax.dot_general(wm, dzb[rows, cols], TN, preferred_element_type=F32))
            _accumulate(dgw_ref.at[hd], jnp.where(mask, dwm, 0.0), first)
            _accumulate(dgb_ref.at[hd], dbs, first)
            dvn_cols.append(jnp.concatenate(dvn_rows, axis=0))
        dvn = jnp.concatenate(dvn_cols, axis=1)
        _accumulate(dslg_ref, jnp.sum(dvn * vn0, axis=0, keepdims=True), first)
        _accumulate(dslb_ref, jnp.sum(dvn, axis=0, keepdims=True), first)
        dv = _ln_bwd(vrs, vn0, dvn * slg_ref[...])
        dp_ref[:, 7 * BW:8 * BW] = (dv * _gelu_grad(pv, thv)).astype(BF)

    const = lambda shp: pl.BlockSpec(shp, lambda i: (0,) * len(shp))
    return _pc(
        jobs, body, name="mix_branches_bwd", grid=(s_len // ts,),
        in_specs=(_halo_specs(ts, MIXC, s_len, True) + _halo_specs(ts, NBR * BW, s_len, True)
                  + _halo_specs(ts, BW, s_len, True)[1:] + [pl.BlockSpec(memory_space=pl.ANY)] + _branch_weights_specs()),
        out_specs=[pl.BlockSpec((ts, MIXC), lambda i: (i, 0))] + [const(s) for s in small_shapes],
        out_shape=[jax.ShapeDtypeStruct((s_len, INC), BF)] + [jax.ShapeDtypeStruct(s, F32) for s in small_shapes],
        scratch_shapes=[pltpu.VMEM((ext, MIXC), BF), pltpu.VMEM((ext, NBR * BW), BF)],
        input_output_aliases={8: 0},
        compiler_params=_params(1),
    )(p, p, p, dy, dy, dy, z, z, dp, pool_w, pool_scale, sconv_w, cconv_w, cln_g, cln_b, sln_g, sln_b, sgu_w, sgu_bt)


def _mix_out_fwd(x, y, p, w_up, w_out, jobs=()):
    s_len = x.shape[0]
    ts = min(TOKEN_TILE if jobs else 2 * TOKEN_TILE, s_len)

    def body(x_ref, y_ref, pg_ref, wu_ref, wo_ref, xo_ref, m_ref, up_ref):
        m = jnp.zeros((ts, D), F32)
        for gi in range(NBR):
            up = jnp.dot(y_ref[:, gi * BW:(gi + 1) * BW], wu_ref[gi], preferred_element_type=F32)
            up_ref[:, gi * D:(gi + 1) * D] = up.astype(BF)
            m = m + _sig(pg_ref[:, gi * D:(gi + 1) * D].astype(F32)) * up
        mb = m.astype(BF)
        m_ref[...] = mb
        xo_ref[...] = x_ref[...] + jnp.dot(mb, wo_ref[...], preferred_element_type=F32)

    return _pc(
        jobs, body, name="mix_out_fwd", grid=(s_len // ts,),
        in_specs=[_rows(ts, D), _rows(ts, NBR * BW), _rows(ts, NBR * D, 1), _full((NBR, BW, D)), _full((D, D))],
        out_specs=[_rows(ts, D), _rows(ts, D), _rows(ts, NBR * D)],
        out_shape=[jax.ShapeDtypeStruct((s_len, D), F32), jax.ShapeDtypeStruct((s_len, D), BF),
                   jax.ShapeDtypeStruct((s_len, NBR * D), BF)],
        compiler_params=_params(1),
    )(x, y, p, w_up, w_out)


def _mix_out_bwd(dxo, up, p, w_up, w_out, jobs=()):
    s_len = dxo.shape[0]
    ts = min(TOKEN_TILE, s_len)

    def body(dxo_ref, up_ref, pg_ref, wu_ref, wo_ref, dy_ref, dp_ref, dup_ref, dxb_ref):
        dxb = dxo_ref[...].astype(BF)
        dxb_ref[...] = dxb
        dm = lax.dot_general(dxb, wo_ref[...], NT, preferred_element_type=F32)
        for gi in range(NBR):
            cols = slice(gi * D, (gi + 1) * D)
            gate = _sig(pg_ref[:, cols].astype(F32))
            dp_ref[:, cols] = (dm * up_ref[:, cols].astype(F32) * gate * (1.0 - gate)).astype(BF)
            dup = (dm * gate).astype(BF)
            dup_ref[:, cols] = dup
            dy_ref[:, gi * BW:(gi + 1) * BW] = lax.dot_general(
                dup, wu_ref[gi], NT, preferred_element_type=F32).astype(BF)

    return _pc(
        jobs, body, name="mix_out_bwd", grid=(s_len // ts,),
        in_specs=[_rows(ts, D), _rows(ts, NBR * D), _rows(ts, NBR * D, 1), _full((NBR, BW, D)), _full((D, D))],
        out_specs=[_rows(ts, NBR * BW), _rows(ts, NBR * D, 1), _rows(ts, NBR * D), _rows(ts, D)],
        out_shape=[jax.ShapeDtypeStruct((s_len, NBR * BW), BF), jax.ShapeDtypeStruct((s_len, INC), BF),
                   jax.ShapeDtypeStruct((s_len, NBR * D), BF), jax.ShapeDtypeStruct((s_len, D), BF)],
        compiler_params=_params(1),
    )(dxo, up, p, w_up, w_out)


def _mix_in_bwd(dp, w_in, x, g, dxo, jobs=()):
    s_len = x.shape[0]
    ts = min(TOKEN_TILE, s_len)

    def body(dp_ref, w_ref, x_ref, g_ref, dxo_ref, dxi_ref, dg_ref):
        i = pl.program_id(0)
        dh = lax.dot_general(dp_ref[...], w_ref[...], NT, preferred_element_type=F32)
        dx, dg = _rms_bwd(x_ref[...], g_ref[...], dh)
        dxi_ref[...] = dxo_ref[...] + dx
        _accumulate(dg_ref, dg, i == 0)

    return _pc(
        jobs, body, name="mix_in_bwd", grid=(s_len // ts,),
        in_specs=[_rows(ts, INC), _full((D, INC)), _rows(ts, D), _full((1, D)), _rows(ts, D)],
        out_specs=[_rows(ts, D), pl.BlockSpec((1, D), lambda i: (0, 0))],
        out_shape=[jax.ShapeDtypeStruct((s_len, D), F32), jax.ShapeDtypeStruct((1, D), F32)],
        compiler_params=_params(1),
    )(dp, w_in, x, g, dxo)


def _loss_head(x, g, target, jobs=()):
    s_len = x.shape[0]
    ts = min(512, s_len)

    def body(x_ref, g_ref, t_ref, dx_ref, dg_ref, loss_ref):
        i = pl.program_id(0)
        xv = x_ref[...]
        _, xh = _rms_stats(xv)
        err = xh * g_ref[...] - t_ref[...]
        part = 0.5 * jnp.sum(jnp.mean(err * err, axis=-1, keepdims=True), axis=0, keepdims=True)
        dx, dg = _rms_bwd(xv, g_ref[...], err * (1.0 / D))
        dx_ref[...] = dx
        _accumulate(dg_ref, dg, i == 0)
        _accumulate(loss_ref, jnp.broadcast_to(part, (1, GW)), i == 0)

    return _pc(
        jobs, body, name="loss_head", grid=(s_len // ts,),
        in_specs=[_rows(ts, D), _full((1, D)), _rows(ts, D)],
        out_specs=[_rows(ts, D), pl.BlockSpec((1, D), lambda i: (0, 0)), pl.BlockSpec((1, GW), lambda i: (0, 0))],
        out_shape=[jax.ShapeDtypeStruct((s_len, D), F32), jax.ShapeDtypeStruct((1, D), F32),
                   jax.ShapeDtypeStruct((1, GW), F32)],
        compiler_params=_params(1),
    )(x, g, target)


SUM_TILE = 1 << 20
ADAM_TILE = 1 << 19
CAST_TILE = 1 << 19


def _row_tile(rows, cols, budget=1 << 18):
    tr = rows
    while tr * cols > budget and tr % 16 == 0:
        tr //= 2
    return tr


def _elementwise(fn, name, ins, out_dtypes):
    rows, cols = ins[0].shape
    tr = _row_tile(rows, cols)
    n_in = len(ins)

    def body(*refs):
        res = fn(*[r[...] for r in refs[:n_in]])
        for o_ref, val in zip(refs[n_in:], res):
            o_ref[...] = val.astype(o_ref.dtype)

    outs = pl.pallas_call(
        body, name=name, grid=(rows // tr,),
        in_specs=[_rows(tr, cols)] * n_in, out_specs=[_rows(tr, cols)] * len(out_dtypes),
        out_shape=[jax.ShapeDtypeStruct((rows, cols), dt) for dt in out_dtypes],
        compiler_params=_params(1),
    )(*ins)
    return outs


def _tiled(fn, name, grid, in_specs, out_specs, out_shape, args, scalars=None, alias=None):
    alias = alias or {}
    n_in = len(in_specs) - len(alias)
    n_pre = 0 if scalars is None else 1

    def body(*refs):
        refs = refs[n_pre:]
        res = fn(*[r[...] for r in refs[:n_in]])
        for o_ref, val in zip(refs[len(in_specs):], res):
            o_ref[...] = val.astype(o_ref.dtype)

    aliases = {n_pre + pos: out for pos, out in alias.items()}
    if scalars is None:
        return pl.pallas_call(body, name=name, grid=grid, in_specs=in_specs, out_specs=out_specs, out_shape=out_shape,
                              input_output_aliases=aliases, compiler_params=_params(len(grid)))(*args)
    spec = pltpu.PrefetchScalarGridSpec(num_scalar_prefetch=1, grid=grid, in_specs=in_specs, out_specs=out_specs)
    return pl.pallas_call(body, name=name, grid_spec=spec, out_shape=out_shape, input_output_aliases=aliases,
                          compiler_params=_params(len(grid)))(scalars, *args)


def _cast_into(shard, layer, col, dtype, sc):
    ks, ns = shard.shape[1:]
    tr = _row_tile(ks, ns, SUM_TILE)
    full = (ks, ns * NCHIP) if col else (ks * NCHIP, ns)
    out_idx = (lambda i, s: (i, s[1])) if col else (lambda i, s: (s[1] * (ks // tr) + i, 0))
    return _tiled(lambda v: (v,), "cast_into", (ks // tr,), [pl.BlockSpec((None, tr, ns), lambda i, s: (layer, i, 0))],
                  [pl.BlockSpec((tr, ns), out_idx)], [jax.ShapeDtypeStruct(full, dtype)], [shard], sc)[0]


def _cast_rest(items, jobs=()):
    n = len(items)
    tiles, slot_sets, counts = [], {}, {}
    for k, (shard, _, _, dt) in enumerate(items):
        ks, ns = shard.shape[1:]
        tr = _row_tile(ks, ns, CAST_TILE)
        which = slot_sets.setdefault((tr, ns, shard.dtype, dt), len(slot_sets))
        for t in range(ks // tr):
            tiles.append((k, t, tr, which, counts.get(which, 0) % 2))
            counts[which] = counts.get(which, 0) + 1
    n_sets = len(slot_sets)

    def body(*refs):
        ins, outs = refs[:n], refs[n:2 * n]
        bufs = refs[2 * n:2 * n + 2 * n_sets]
        in_sems, out_sems = refs[2 * n + 2 * n_sets:]
        chip = 2 * lax.axis_index("x") + lax.axis_index("y")
        fetch, store = [], []
        for k, t, tr, which, slot in tiles:
            shard, layer, col, _ = items[k]
            ks, ns = shard.shape[1:]
            if col:
                place = outs[k].at[pl.ds(t * tr, tr), pl.ds(pl.multiple_of(chip * ns, GW), ns)]
            else:
                place = outs[k].at[pl.ds(pl.multiple_of(chip * ks + t * tr, 16), tr), :]
            fetch.append(pltpu.make_async_copy(
                ins[k].at[layer, pl.ds(t * tr, tr), :], bufs[2 * which].at[slot], in_sems.at[which, slot]))
            store.append(pltpu.make_async_copy(bufs[2 * which + 1].at[slot], place, out_sems.at[which, slot]))
        busy = {}
        fetch[0].start()
        for i, (k, t, tr, which, slot) in enumerate(tiles):
            if i + 1 < len(tiles):
                fetch[i + 1].start()
            fetch[i].wait()
            if (which, slot) in busy:
                store[busy[which, slot]].wait()
            bufs[2 * which + 1][slot] = bufs[2 * which][slot].astype(bufs[2 * which + 1].dtype)
            store[i].start()
            busy[which, slot] = i
        for i in busy.values():
            store[i].wait()

    def full_shape(shard, col):
        ks, ns = shard.shape[1:]
        return (ks, ns * NCHIP) if col else (ks * NCHIP, ns)

    scratch = []
    for (tr, ns, dt_in, dt_out) in slot_sets:
        scratch += [pltpu.VMEM((2, tr, ns), dt_in), pltpu.VMEM((2, tr, ns), dt_out)]
    scratch += [pltpu.SemaphoreType.DMA((n_sets, 2)), pltpu.SemaphoreType.DMA((n_sets, 2))]
    return _pc(
        jobs, body, name="cast_rest", in_specs=[ANY] * n, out_specs=[ANY] * n,
        out_shape=[jax.ShapeDtypeStruct(full_shape(shard, col), dt) for shard, _, col, dt in items],
        scratch_shapes=scratch, compiler_params=pltpu.CompilerParams(vmem_limit_bytes=VMEM_LIMIT),
    )(*[shard for shard, _, _, _ in items])


def _pair_sum(g, got, col, sc):
    hk, hn = got.shape
    tr = _row_tile(hk, hn, SUM_TILE)
    g_idx = (lambda i, s: (s[0] * (hk // tr) + i, 0)) if col else (lambda i, s: (i, s[0]))
    plain = pl.BlockSpec((tr, hn), lambda i, s: (i, 0))
    return _tiled(lambda a, b: (a.astype(F32) + b.astype(F32),), "pair_sum", (hk // tr,),
                  [pl.BlockSpec((tr, hn), g_idx), plain], [plain], [jax.ShapeDtypeStruct((hk, hn), BF)], [g, got], sc)[0]


def _chip_sum(ph, got, col, layer, depth, sc, carry):
    qk, qn = got.shape[1:]
    tr = _row_tile(qk, qn, SUM_TILE)
    ph_idx = (lambda i, s: (i, s[1])) if col else (lambda i, s: (s[1] * (qk // tr) + i, 0))
    out_shape = (depth, 2 * qk, qn) if col else (depth, qk, 2 * qn)
    out_idx = (lambda i, s: (layer, s[0] * (qk // tr) + i, 0)) if col else (lambda i, s: (layer, i, s[0]))
    in_specs = [pl.BlockSpec((tr, qn), ph_idx)] + [pl.BlockSpec((None, tr, qn), lambda i, s, j=j: (j, i, 0)) for j in range(3)]
    args = [ph, got, got, got]
    if carry is not None:
        in_specs.append(ANY)
        args.append(carry)
    return _tiled(lambda a, b, c_, d_: (a.astype(F32) + b.astype(F32) + c_.astype(F32) + d_.astype(F32),), "chip_sum",
                  (qk // tr,), in_specs, [pl.BlockSpec((None, tr, qn), out_idx)],
                  [jax.ShapeDtypeStruct(out_shape, F32)], args, sc, alias=None if carry is None else {4: 0})[0]


def _adamw_math(w, g, m, v):
    m = ADAM_B1 * m + (1.0 - ADAM_B1) * g
    v = ADAM_B2 * v + (1.0 - ADAM_B2) * (g * g)
    m_hat = m / (1.0 - ADAM_B1 ** ADAM_STEP)
    v_hat = v / (1.0 - ADAM_B2 ** ADAM_STEP)
    delta = -ADAM_LR * (m_hat / (jnp.sqrt(v_hat) + ADAM_EPS) + ADAM_WD * w)
    return delta, m, v


def _adamw_layer(w, g, m, v, layer, carry):
    k, n = w.shape[1:]
    tr = _row_tile(k, n, ADAM_TILE)
    blk = pl.BlockSpec((None, tr, n), lambda i: (layer, i, 0))
    carry = list(carry or [])
    return _tiled(lambda *a: (*_adamw_math(*a), a[1]), "adamw_layer", (k // tr,), [blk] * 4 + [ANY] * len(carry),
                  [blk] * 4, [jax.ShapeDtypeStruct(w.shape, F32)] * 4, [w, g, m, v] + carry, None,
                  alias={4 + pos: pos for pos in range(len(carry))})


def _adamw(w, g, m, v):
    shape = w.shape
    two_d = lambda a: a.reshape(-1, shape[-1])
    outs = _elementwise(_adamw_math, "adamw", [two_d(w), two_d(g), two_d(m), two_d(v)], [F32, F32, F32])
    return [o.reshape(shape) for o in outs]


ANY = pl.BlockSpec(memory_space=pl.ANY)


def _place():
    x, y, c = lax.axis_index("x"), lax.axis_index("y"), lax.axis_index("c")
    chips = [(1 - x, y), (x, 1 - y), (1 - x, 1 - y)]
    return x, y, c, chips


def _cols(ref, start, size):
    idx = (slice(None),) * (len(ref.shape) - 1) + (pl.ds(pl.multiple_of(start, GW), size),)
    return ref.at[idx]


def _rows_of(ref, start, size):
    nd = len(ref.shape)
    idx = (slice(None),) * (nd - 2) + (pl.ds(pl.multiple_of(start, 16), size), slice(None))
    return ref.at[idx]


def _region(ref, col_sharded, chip, half):
    k, n = ref.shape
    align = 16 if ref.dtype == BF else 8
    if col_sharded:
        return ref.at[pl.ds(pl.multiple_of(half * (k // 2), align), k // 2),
                      pl.ds(pl.multiple_of(chip * (n // NCHIP), GW), n // NCHIP)]
    rows = k // (2 * NCHIP)
    return ref.at[pl.ds(pl.multiple_of((2 * chip + half) * rows, align), rows), :]


def _job_gather(bufs, col_sharded, handoff):
    n = len(bufs)

    def copies(outs, send_sems, recv_sems, stage):
        x, y, c, chips = _place()
        sends, lands = [], []
        for k in range(n):
            for j, chip in enumerate(chips):
                theirs = 2 * chip[0] + chip[1]
                if stage == 0:
                    src, to = _region(outs[k], col_sharded[k], 2 * x + y, c), (*chip, c)
                    land = _region(outs[k], col_sharded[k], theirs, c)
                else:
                    src, to = _region(outs[k], col_sharded[k], theirs, c), (x, y, 1 - c)
                    land = _region(outs[k], col_sharded[k], theirs, 1 - c)
                sem = 3 * n * stage + 3 * k + j
                sems = dict(send_sem=send_sems.at[sem], recv_sem=recv_sems.at[sem], device_id=to, device_id_type=MESH)
                sends.append(pltpu.make_async_remote_copy(src_ref=src, dst_ref=src, **sems))
                lands.append(pltpu.make_async_remote_copy(src_ref=land, dst_ref=land, **sems))
        return sends, lands

    def start(ins, outs, send_sems, recv_sems):
        for cp in copies(outs, send_sems, recv_sems, 0)[0]:
            cp.start()

    def hand_on(ins, outs, send_sems, recv_sems):
        for cp in copies(outs, send_sems, recv_sems, 0)[1]:
            cp.wait_recv()
        for cp in copies(outs, send_sems, recv_sems, 1)[0]:
            cp.start()

    def finish(ins, outs, send_sems, recv_sems):
        sends, lands = copies(outs, send_sems, recv_sems, 1)
        for cp in lands:
            cp.wait_recv()
        for cp in copies(outs, send_sems, recv_sems, 0)[0] + sends:
            cp.wait_send()

    return _Job(bufs, n, [], 6 * n, [(0.0, start), (handoff, hand_on), (1.0, finish)])


def _half(ref, col_sharded, c):
    k, n = ref.shape[-2:]
    return _rows_of(ref, c * (k // 2), k // 2) if col_sharded else _cols(ref, c * (n // 2), n // 2)


def _quarter(ref, col_sharded, j):
    k, n = ref.shape[-2:]
    return _cols(ref, j * (n // NCHIP), n // NCHIP) if col_sharded else _rows_of(ref, j * (k // NCHIP), k // NCHIP)


def _job_pair(grads, col_sharded):
    n = len(grads)

    def half_shape(g, col):
        return (g.shape[0] // 2, g.shape[1]) if col else (g.shape[0], g.shape[1] // 2)

    def copies(ins, got, send_sems, recv_sems):
        x, y, c, _ = _place()
        return [pltpu.make_async_remote_copy(
            src_ref=_half(ins[k], col_sharded[k], 1 - c), dst_ref=got[k], send_sem=send_sems.at[k],
            recv_sem=recv_sems.at[k], device_id=(x, y, 1 - c), device_id_type=MESH) for k in range(n)]

    def start(*refs):
        for cp in copies(*refs):
            cp.start()

    def finish(*refs):
        for cp in copies(*refs):
            cp.wait()

    fresh = [jax.ShapeDtypeStruct(half_shape(g, col), g.dtype) for g, col in zip(grads, col_sharded)]
    return _Job(grads, 0, fresh, n, [(0.0, start), (1.0, finish)])


def _job_chip(halves, col_sharded):
    n = len(halves)

    def quarter_shape(h, col):
        return (3, h.shape[0], h.shape[1] // NCHIP) if col else (3, h.shape[0] // NCHIP, h.shape[1])

    def copies(ins, got, send_sems, recv_sems):
        x, y, c, chips = _place()
        return [pltpu.make_async_remote_copy(
            src_ref=_quarter(ins[k], col_sharded[k], 2 * chip[0] + chip[1]), dst_ref=got[k].at[j],
            send_sem=send_sems.at[3 * k + j], recv_sem=recv_sems.at[3 * k + j], device_id=(*chip, c), device_id_type=MESH)
            for k in range(n) for j, chip in enumerate(chips)]

    def start(*refs):
        for cp in copies(*refs):
            cp.start()

    def finish(*refs):
        for cp in copies(*refs):
            cp.wait()

    fresh = [jax.ShapeDtypeStruct(quarter_shape(h, col), h.dtype) for h, col in zip(halves, col_sharded)]
    return _Job(halves, 0, fresh, 3 * n, [(0.0, start), (1.0, finish)])


def _job_sibling(shards, col_sharded, layers):
    n = len(shards)

    def copies(outs, send_sems, recv_sems):
        x, y, c, _ = _place()
        sends, lands = [], []
        for k in range(n):
            sems = dict(send_sem=send_sems.at[k], recv_sem=recv_sems.at[k], device_id=(x, y, 1 - c), device_id_type=MESH)
            mine = _half(outs[k].at[layers[k]], col_sharded[k], c)
            theirs = _half(outs[k].at[layers[k]], col_sharded[k], 1 - c)
            sends.append(pltpu.make_async_remote_copy(src_ref=mine, dst_ref=mine, **sems))
            lands.append(pltpu.make_async_remote_copy(src_ref=theirs, dst_ref=theirs, **sems))
        return sends, lands

    def start(ins, outs, send_sems, recv_sems):
        for cp in copies(outs, send_sems, recv_sems)[0]:
            cp.start()

    def finish(ins, outs, send_sems, recv_sems):
        sends, lands = copies(outs, send_sems, recv_sems)
        for cp in lands:
            cp.wait_recv()
        for cp in sends:
            cp.wait_send()

    return _Job(shards, n, [], n, [(0.0, start), (1.0, finish)])


def _standalone(jobs, name):
    return _pc(jobs, lambda: None, name=name, in_specs=[], out_specs=[], out_shape=[])()[1]


def _all_reduce_small(buf, jobs=()):
    rows = buf.shape[0]
    per = rows // NDEV
    flips = [(fx, fy, fc) for fx in (0, 1) for fy in (0, 1) for fc in (0, 1)][1:]

    def body(in_ref, out_ref, got_ref, send_sems, recv_sems):
        x, y, c, _ = _place()
        me = 4 * x + 2 * y + c

        def peer(f):
            return tuple(1 - pos if flip else pos for pos, flip in zip((x, y, c), f))

        def block(ref, dev):
            return ref.at[pl.ds(pl.multiple_of(dev * per, 8), per), :]

        scatter = []
        for k, f in enumerate(flips):
            px, py, pc = peer(f)
            scatter.append(pltpu.make_async_remote_copy(
                src_ref=block(in_ref, 4 * px + 2 * py + pc), dst_ref=got_ref.at[k], send_sem=send_sems.at[k],
                recv_sem=recv_sems.at[k], device_id=(px, py, pc), device_id_type=MESH))
        for cp in scatter:
            cp.start()
        for cp in scatter:
            cp.wait()
        total = block(in_ref, me)[...]
        for k in range(len(flips)):
            total = total + got_ref[k]
        block(out_ref, me)[...] = total
        share = []
        for k, f in enumerate(flips):
            share.append(pltpu.make_async_remote_copy(
                src_ref=block(out_ref, me), dst_ref=block(out_ref, me), send_sem=send_sems.at[7 + k],
                recv_sem=recv_sems.at[7 + k], device_id=peer(f), device_id_type=MESH))
        for cp in share:
            cp.start()
        for k, f in enumerate(flips):
            share[k].wait_send()
            px, py, pc = peer(f)
            theirs = block(out_ref, 4 * px + 2 * py + pc)
            pltpu.make_async_remote_copy(
                src_ref=theirs, dst_ref=theirs, send_sem=send_sems.at[7 + k], recv_sem=recv_sems.at[7 + k],
                device_id=(px, py, pc), device_id_type=MESH).wait_recv()

    vmem = pl.BlockSpec(memory_space=pltpu.VMEM)
    return _pc(
        jobs, body, name="all_reduce_small", in_specs=[vmem], out_specs=vmem,
        out_shape=jax.ShapeDtypeStruct((rows, GW), F32),
        scratch_shapes=[pltpu.VMEM((NDEV - 1, per, GW), F32), pltpu.SemaphoreType.DMA((14,)),
                        pltpu.SemaphoreType.DMA((14,))],
    )(buf)


BIG = ("ffn1_w13", "ffn1_w2", "w_in", "w_up", "w_out", "ffn2_w13", "ffn2_w2")
BIG_COL_SHARDED = (True, False, True, True, False, True, False)
SMALL = ("ffn1_norm", "mix_norm", "pool_w", "pool_scale", "sconv_w", "cconv_w", "cconv_ln_g", "cconv_ln_b",
         "sgu_ln_g", "sgu_ln_b", "sgu_w", "sgu_b", "ffn2_norm", "final_norm")
WEIGHTS = ("ffn1_norm", "ffn1_w13", "ffn1_w2", "mix_norm", "w_in", "pool_w", "pool_scale", "sconv_w", "cconv_w",
           "cconv_ln_g", "cconv_ln_b", "sgu_ln_g", "sgu_ln_b", "sgu_w", "sgu_b", "w_up", "w_out", "ffn2_norm",
           "ffn2_w13", "ffn2_w2", "final_norm")


def _pad_rows(a, rows):
    return jnp.pad(a, ((0, 0), (0, rows - a.shape[1]), (0, 0)))


def kernel(x, ffn1_norm, ffn1_w13, ffn1_w2, mix_norm, w_in, pool_w, pool_scale, sconv_w, cconv_w, cconv_ln_g, cconv_ln_b, sgu_ln_g, sgu_ln_b, sgu_w, sgu_b, w_up, w_out, ffn2_norm, ffn2_w13, ffn2_w2, final_norm, loss_target, m_ffn1_norm, m_ffn1_w13, m_ffn1_w2, m_mix_norm, m_w_in, m_pool_w, m_pool_scale, m_sconv_w, m_cconv_w, m_cconv_ln_g, m_cconv_ln_b, m_sgu_ln_g, m_sgu_ln_b, m_sgu_w, m_sgu_b, m_w_up, m_w_out, m_ffn2_norm, m_ffn2_w13, m_ffn2_w2, m_final_norm, v_ffn1_norm, v_ffn1_w13, v_ffn1_w2, v_mix_norm, v_w_in, v_pool_w, v_pool_scale, v_sconv_w, v_cconv_w, v_cconv_ln_g, v_cconv_ln_b, v_sgu_ln_g, v_sgu_ln_b, v_sgu_w, v_sgu_b, v_w_up, v_w_out, v_ffn2_norm, v_ffn2_w13, v_ffn2_w2, v_final_norm):
    args = dict(locals())
    w = {nm: args[nm] for nm in WEIGHTS}
    m = {nm: args["m_" + nm] for nm in WEIGHTS}
    v = {nm: args["v_" + nm] for nm in WEIGHTS}
    depth = ffn1_w13.shape[0]
    chip = 2 * lax.axis_index("x") + lax.axis_index("y")

    sc = jnp.stack([lax.axis_index("c"), chip]).astype(jnp.int32)
    col_of = dict(zip(BIG, BIG_COL_SHARDED), sconv_w=True, cconv_w=True)
    sources = {nm: (w[nm], BF) for nm in BIG}
    sources["w_up"] = (w_up.reshape(depth, NBR * BW, w_up.shape[-1]), BF)
    sources["sconv_w"] = (_pad_rows(sconv_w, 2 * SKP), F32)
    sources["cconv_w"] = (_pad_rows(cconv_w, 2 * CKP), F32)
    first_group = ("ffn1_w13", "ffn1_w2")
    full = [dict() for _ in range(depth)]
    for nm in first_group:
        full[0][nm] = _cast_into(sources[nm][0], 0, col_of[nm], sources[nm][1], sc)

    def gather_job(l, names, handoff):
        return _job_gather([full[l][nm] for nm in names], [col_of[nm] for nm in names], handoff)

    def gather_with(l, names, handoff, call):
        res, job_outs = call([gather_job(l, names, handoff)] if l < depth else [])
        if l < depth:
            full[l].update(zip(names, job_outs[0]))
        return res

    rest = [(nm, l) for l in range(depth) for nm in sources if not (l == 0 and nm in first_group)]
    casted = gather_with(0, first_group, 1.0, lambda jobs: _cast_rest(
        [(sources[nm][0], l, col_of[nm], sources[nm][1]) for nm, l in rest], jobs))
    for (nm, l), arr in zip(rest, casted):
        full[l][nm] = arr

    xs = x[0]
    row = lambda a: a.reshape(1, -1)
    saved = []
    for l in range(depth):
        g1, gm, g2 = row(ffn1_norm[l]), row(mix_norm[l]), row(ffn2_norm[l])
        x1, h1, s1, ab1 = gather_with(l, ("w_in",), 1.0, lambda jobs: _ffn_fwd(
            xs, g1, full[l]["ffn1_w13"], full[l]["ffn1_w2"], jobs))
        hm, p = gather_with(l, ("w_up", "w_out", "sconv_w", "cconv_w", "ffn2_w2"), 0.85, lambda jobs: _mix_in(
            x1, gm, full[l]["w_in"], jobs))
        branch = (pool_w[l], row(pool_scale[l]), full[l]["sconv_w"][:SKP], full[l]["cconv_w"][:CKP], row(cconv_ln_g[l]),
                  row(cconv_ln_b[l]), row(sgu_ln_g[l]), row(sgu_ln_b[l]), sgu_w[l], sgu_b[l].T)
        y, conv_z = gather_with(l, ("ffn2_w13",), 1.0, lambda jobs: _mix_branches_fwd(p, *branch, jobs=jobs))
        w_up_l = full[l]["w_up"].reshape(NBR, BW, D)
        x2, merged, up = gather_with(l + 1, ("ffn1_w2",), 0.8, lambda jobs: _mix_out_fwd(
            x1, y, p, w_up_l, full[l]["w_out"], jobs))
        x3, h2, s2, ab2 = gather_with(l + 1, ("ffn1_w13",), 1.0, lambda jobs: _ffn_fwd(
            x2, g2, full[l]["ffn2_w13"], full[l]["ffn2_w2"], jobs))
        lw = dict(g1=g1, gm=gm, g2=g2, w13a=full[l]["ffn1_w13"], w2a=full[l]["ffn1_w2"], w13b=full[l]["ffn2_w13"],
                  w2b=full[l]["ffn2_w2"], w_in=full[l]["w_in"], w_up=w_up_l, w_out=full[l]["w_out"], branch=branch)
        saved.append(dict(lw=lw, x0=xs, x1=x1, x2=x2, h1=h1, s1=s1, ab1=ab1, hm=hm, p=p, y=y, z=conv_z, merged=merged, up=up, h2=h2,
                          s2=s2, ab2=ab2))
        xs = x3

    (dx, d_final, loss_part), _ = _loss_head(xs, row(final_norm), loss_target[0])
    loss = lax.psum(loss_part[0, 0], ("x", "y", "c"))

    ici_us = dict(ffn1_w13=64, ffn1_w2=32, w_in=93, w_up=23, w_out=12, ffn2_w13=64, ffn2_w2=32)
    parts, pair_sums, reduced, big_updates, pending = {}, {}, {}, {}, []

    def take_jobs(budget_us):
        chosen = []
        for task in list(pending):
            kind, (nm, _) = task
            if kind == "chip":
                if ici_us[nm] > budget_us:
                    continue
                budget_us -= ici_us[nm]
            if kind == "sib" and any(k == "sib" and key[0] == nm for k, key in chosen):
                continue
            chosen.append(task)
            pending.remove(task)
        groups, jobs = [], []
        for kind in ("pair", "sib", "chip"):
            keys = [key for k, key in chosen if k == kind]
            if not keys:
                continue
            cols = [col_of[nm] for nm, _ in keys]
            groups.append((kind, keys))
            if kind == "pair":
                jobs.append(_job_pair([parts[key] for key in keys], cols))
            elif kind == "chip":
                jobs.append(_job_chip([pair_sums[key] for key in keys], cols))
            else:
                jobs.append(_job_sibling([reduced[nm] for nm, _ in keys], cols, [layer for _, layer in keys]))
        return groups, jobs

    def settle(groups, job_outs):
        for (kind, keys), outs in zip(groups, job_outs):
            for key, out in zip(keys, outs):
                nm, layer = key
                if kind == "pair":
                    pair_sums[key] = _pair_sum(parts[key], out, col_of[nm], sc)
                    pending.append(("chip", key))
                elif kind == "chip":
                    assert not any(k == "sib" and other[0] == nm for k, other in pending)
                    reduced[nm] = _chip_sum(pair_sums[key], out, col_of[nm], layer, depth, sc, reduced.get(nm))
                    pending.append(("sib", key))
                else:
                    reduced[nm] = out
                    as3 = lambda a: a.reshape(out.shape)
                    big_updates[nm] = _adamw_layer(as3(w[nm]), out, as3(m[nm]), as3(v[nm]), layer, big_updates.get(nm))

    def run(budget_us, call, carrier=True):
        groups, jobs = take_jobs(budget_us) if carrier else ([], [])
        res, job_outs = call(jobs)
        settle(groups, job_outs)
        return res

    def wgrad_done(key, partial):
        parts[key] = partial
        pending.append(("pair", key))

    small_parts = {nm: [None] * depth for nm in SMALL if nm != "final_norm"}
    for l in reversed(range(depth)):
        sv = saved[l]
        lw = sv["lw"]
        dx, dab, dyh, dg2 = run(100, lambda jobs: _ffn_bwd(
            dx, sv["x2"], lw["g2"], sv["ab2"], lw["w13b"], lw["w2b"], jobs))
        wgrad_done(("ffn2_w13", l), run(58, lambda jobs: _wgrad(sv["h2"], dab, D, 512, "wgrad_w13", jobs), False))
        wgrad_done(("ffn2_w2", l), run(33, lambda jobs: _wgrad(sv["s2"], dyh, 256, D, "wgrad_w2", jobs), False))
        small_parts["ffn2_norm"][l] = dg2

        dy, dp, dup, dxb = run(70, lambda jobs: _mix_out_bwd(dx, sv["up"], sv["p"], lw["w_up"], lw["w_out"], jobs))
        wgrad_done(("w_out", l), run(16, lambda jobs: _wgrad(sv["merged"], dxb, D, 512, "wgrad_w_out", jobs), False))
        wgrad_done(("w_up", l), run(25, lambda jobs: _wgrad_groups(sv["y"], dup, BW, D, "wgrad_w_up", jobs), False))
        (dp, d_pool_w, d_pool_scale, d_sconv, d_cconv, d_clg, d_clb, d_slg, d_slb, d_sgu_w, d_sgu_b) = run(
            130, lambda jobs: _mix_branches_bwd(sv["p"], dy, sv["z"], dp, *lw["branch"], jobs=jobs))
        wgrad_done(("w_in", l), run(80, lambda jobs: _wgrad(sv["hm"], dp, D, 512, "wgrad_w_in", jobs), False))
        dx, dgm = run(91, lambda jobs: _mix_in_bwd(dp, lw["w_in"], sv["x1"], lw["gm"], dx, jobs))
        small_parts["mix_norm"][l] = dgm
        small_parts["pool_w"][l] = d_pool_w
        small_parts["pool_scale"][l] = d_pool_scale
        small_parts["sconv_w"][l] = d_sconv[:SK]
        small_parts["cconv_w"][l] = d_cconv[:CK]
        small_parts["cconv_ln_g"][l] = d_clg
        small_parts["cconv_ln_b"][l] = d_clb
        small_parts["sgu_ln_g"][l] = d_slg
        small_parts["sgu_ln_b"][l] = d_slb
        small_parts["sgu_w"][l] = d_sgu_w
        small_parts["sgu_b"][l] = jnp.sum(d_sgu_b, axis=-1)

        dx, dab, dyh, dg1 = run(100, lambda jobs: _ffn_bwd(
            dx, sv["x0"], lw["g1"], sv["ab1"], lw["w13a"], lw["w2a"], jobs))
        wgrad_done(("ffn1_w2", l), run(33, lambda jobs: _wgrad(sv["s1"], dyh, 256, D, "wgrad_w2", jobs), l == 0))
        wgrad_done(("ffn1_w13", l), run(58, lambda jobs: _wgrad(sv["h1"], dab, D, 512, "wgrad_w13", jobs), l == 0))
        small_parts["ffn1_norm"][l] = dg1
    grad_x = dx[None]

    small_local = {nm: jnp.stack(parts).reshape(depth, *w[nm].shape[1:-1], -1) if nm not in ("sconv_w", "cconv_w")
                   else jnp.stack(parts) for nm, parts in small_parts.items()}
    small_local["final_norm"] = d_final.reshape(-1)
    sizes = [small_local[nm].size for nm in SMALL]
    total = sum(sizes)
    pad_to = NDEV * 8 * GW
    padded = -(-total // pad_to) * pad_to
    packed = jnp.concatenate([small_local[nm].reshape(-1) for nm in SMALL] + [jnp.zeros((padded - total,), F32)])
    groups, jobs = take_jobs(float("inf"))
    summed, job_outs = _all_reduce_small(packed.reshape(-1, GW), jobs)
    settle(groups, job_outs)
    summed = summed.reshape(-1)
    flushes = 0
    while pending:
        groups, jobs = take_jobs(float("inf"))
        settle(groups, _standalone(jobs, "grad_flush_%d" % flushes))
        flushes += 1
    big_grads = {nm: big_updates[nm][3].reshape(w[nm].shape) for nm in BIG}
    small_grads, off = {}, 0
    for nm, size in zip(SMALL, sizes):
        small_grads[nm] = summed[off:off + size].reshape(small_local[nm].shape)
        off += size
    for nm in ("sconv_w", "cconv_w"):
        small_grads[nm] = lax.dynamic_slice_in_dim(small_grads[nm], chip * GW, GW, axis=2)

    grads = {**big_grads, **small_grads}

    delta, new_m, new_v = {}, {}, {}
    for nm in BIG:
        delta[nm], new_m[nm], new_v[nm] = [a.reshape(w[nm].shape) for a in big_updates[nm][:3]]
    s_sizes = [w[nm].size for nm in SMALL]
    s_total = sum(s_sizes)
    s_padded = -(-s_total // (8 * GW)) * (8 * GW)

    def pack(tree):
        return jnp.concatenate([tree[nm].reshape(-1) for nm in SMALL] + [jnp.ones((s_padded - s_total,), F32)]).reshape(-1, GW)

    packed_out = _adamw(pack(w), pack(grads), pack(m), pack(v))
    off = 0
    for nm, size in zip(SMALL, s_sizes):
        for tree, arr in zip((delta, new_m, new_v), packed_out):
            tree[nm] = arr.reshape(-1)[off:off + size].reshape(w[nm].shape)
        off += size

    return (loss, grad_x, *[grads[nm] for nm in WEIGHTS], *[delta[nm] for nm in WEIGHTS],
            *[new_m[nm] for nm in WEIGHTS], *[new_v[nm] for nm in WEIGHTS])
```

```python
import jax
import jax.numpy as jnp
from jax import lax
from jax.experimental import pallas as pl
from jax.experimental.pallas import tpu as pltpu

D = 1024
FF = 2816
BW = 512
NBR = 4
MIXC = 4096
INC = 8192
GW = 128
CHUNK = 64
SK = 3
CK = 31
SKP = 8
CKP = 32
HALO = 32
TOKEN_TILE = 256
BRANCH_TILE = 512
EPS = 1e-6
NCHIP = 4
NDEV = 8

ADAM_LR = 0.001
ADAM_B1 = 0.9
ADAM_B2 = 0.999
ADAM_EPS = 1e-08
ADAM_WD = 0.01
ADAM_STEP = 10

VMEM_LIMIT = 60 * 1024 * 1024

BF = jnp.bfloat16
F32 = jnp.float32
MESH = pl.DeviceIdType.MESH
NT = (((1,), (1,)), ((), ()))
TN = (((0,), (0,)), ((), ()))


def _params(n_axes):
    return pltpu.CompilerParams(dimension_semantics=("arbitrary",) * n_axes, vmem_limit_bytes=VMEM_LIMIT)


class _Job:
    def __init__(self, args, n_inplace, fresh, nsem, phases):
        self.args, self.n_inplace, self.fresh, self.nsem, self.phases = list(args), n_inplace, list(fresh), nsem, phases


def _pc(jobs, body, name, in_specs, out_specs, out_shape, grid=(), scratch_shapes=(), input_output_aliases=None,
        compiler_params=None, grid_spec_scalars=None):
    single = not isinstance(out_shape, (list, tuple))
    core_out_specs = [out_specs] if single else list(out_specs)
    core_out_shape = [out_shape] if single else list(out_shape)
    n_in, n_out, n_scr = len(in_specs), len(core_out_specs), len(scratch_shapes)
    n_pre = 0 if grid_spec_scalars is None else 1
    all_in, all_out, all_shape = list(in_specs), list(core_out_specs), list(core_out_shape)
    all_scr, aliases, extra_args, layout = list(scratch_shapes), dict(input_output_aliases or {}), [], []
    for job in jobs:
        n_job_out = job.n_inplace + len(job.fresh)
        layout.append((len(all_in), len(job.args), len(all_out), n_job_out, len(all_scr)))
        for a in range(job.n_inplace):
            aliases[n_pre + len(all_in) + a] = len(all_out) + a
        all_in += [ANY] * len(job.args)
        extra_args += job.args
        all_out += [ANY] * n_job_out
        all_shape += [jax.ShapeDtypeStruct(a.shape, a.dtype) for a in job.args[:job.n_inplace]] + job.fresh
        all_scr += [pltpu.SemaphoreType.DMA((job.nsem,)), pltpu.SemaphoreType.DMA((job.nsem,))]
    steps = 1
    for extent in grid:
        steps *= extent
    events = []
    for (i0, na, o0, no, s0), job in zip(layout, jobs):
        for frac, fn in job.phases:
            events.append((min(int(frac * steps), steps - 1), frac >= 1.0, len(events), fn, (i0, na, o0, no, s0)))
    events.sort(key=lambda e: e[:3])

    def wrapped(*refs):
        pre, refs = refs[:n_pre], refs[n_pre:]
        ins, outs, scr = refs[:len(all_in)], refs[len(all_in):len(all_in) + len(all_out)], refs[len(all_in) + len(all_out):]
        step = 0
        for axis, extent in enumerate(grid):
            step = step * extent + pl.program_id(axis)

        def emit(event):
            at, _, _, fn, (i0, na, o0, no, s0) = event
            run = lambda: fn(ins[i0:i0 + na], outs[o0:o0 + no], scr[s0], scr[s0 + 1])
            if steps == 1:
                run()
            else:
                pl.when(step == at)(run)

        for event in events:
            if not event[1]:
                emit(event)
        body(*pre, *ins[:n_in], *outs[:n_out], *scr[:n_scr])
        for event in events:
            if event[1]:
                emit(event)

    kwargs = dict(name=name, out_shape=all_shape, input_output_aliases=aliases)
    if compiler_params is not None:
        kwargs["compiler_params"] = compiler_params
    if grid_spec_scalars is not None:
        kwargs["grid_spec"] = pltpu.PrefetchScalarGridSpec(
            num_scalar_prefetch=1, grid=grid, in_specs=all_in, out_specs=all_out, scratch_shapes=all_scr)
    else:
        kwargs.update(in_specs=all_in, out_specs=all_out, scratch_shapes=all_scr)
        if grid:
            kwargs["grid"] = grid
    call = pl.pallas_call(wrapped, **kwargs)

    def run_call(*args):
        pre_args = [] if grid_spec_scalars is None else [grid_spec_scalars]
        res = list(call(*pre_args, *args, *extra_args))
        core = res[0] if single else res[:n_out]
        job_outs = [res[o0:o0 + no] for (_, _, o0, no, _) in layout]
        return core, job_outs

    return run_call


def _full(shape):
    nd = len(shape)
    return pl.BlockSpec(shape, lambda *_: (0,) * nd, pipeline_mode=pl.Buffered(1))


def _rows(ts, width, col=0):
    return pl.BlockSpec((ts, width), lambda i: (i, col))


def _sig(v):
    return jax.nn.sigmoid(v)


def _rms_stats(x):
    r = lax.rsqrt(jnp.mean(x * x, axis=-1, keepdims=True) + EPS)
    return r, x * r


def _rms_bwd(x, g, dh):
    r, xh = _rms_stats(x)
    dg = jnp.sum(dh * xh, axis=0, keepdims=True)
    dxh = dh * g
    dx = r * (dxh - xh * jnp.mean(dxh * xh, axis=-1, keepdims=True))
    return dx, dg


def _accumulate(ref, val, first):
    @pl.when(first)
    def _():
        ref[...] = val

    @pl.when(jnp.logical_not(first))
    def _():
        ref[...] += val


def _ffn_fwd(x, g, w13, w2, jobs=()):
    s_len = x.shape[0]
    ts = min(TOKEN_TILE, s_len)

    def body(x_ref, g_ref, w13_ref, w2_ref, xo_ref, h_ref, s_ref, dsab_ref):
        xv = x_ref[...]
        r, xh = _rms_stats(xv)
        h = (xh * g_ref[...]).astype(BF)
        h_ref[...] = h
        ab = jnp.dot(h, w13_ref[...], preferred_element_type=F32)
        a = ab[:, :FF]
        b = ab[:, FF:]
        sg = _sig(a)
        sil = a * sg
        dsab_ref[:, :FF] = (b * (sg * (1.0 + a * (1.0 - sg)))).astype(BF)
        dsab_ref[:, FF:] = sil.astype(BF)
        s = (sil * b).astype(BF)
        s_ref[...] = s
        xo_ref[...] = xv + 0.5 * jnp.dot(s, w2_ref[...], preferred_element_type=F32)

    return _pc(
        jobs, body, name="ffn_fwd", grid=(s_len // ts,),
        in_specs=[_rows(ts, D), _full((1, D)), _full((D, 2 * FF)), _full((FF, D))],
        out_specs=[_rows(ts, D), _rows(ts, D), _rows(ts, FF), _rows(ts, 2 * FF)],
        out_shape=[jax.ShapeDtypeStruct((s_len, D), F32), jax.ShapeDtypeStruct((s_len, D), BF),
                   jax.ShapeDtypeStruct((s_len, FF), BF), jax.ShapeDtypeStruct((s_len, 2 * FF), BF)],
        compiler_params=_params(1),
    )(x, g, w13, w2)


def _ffn_bwd(dxo, x, g, dsab, w13, w2, jobs=()):
    s_len = x.shape[0]
    ts = min(2 * TOKEN_TILE, s_len)

    def body(dxo_ref, x_ref, g_ref, dsab_ref, w13_ref, w2_ref, dxi_ref, dab_ref, dy_ref, dg_ref):
        i = pl.program_id(0)
        dxo_v = dxo_ref[...]
        dy = (0.5 * dxo_v).astype(BF)
        dy_ref[...] = dy
        ds = lax.dot_general(dy, w2_ref[...], NT, preferred_element_type=F32)
        dab_ref[:, :FF] = (ds * dsab_ref[:, :FF].astype(F32)).astype(BF)
        dab_ref[:, FF:] = (ds * dsab_ref[:, FF:].astype(F32)).astype(BF)
        dh = lax.dot_general(dab_ref[...], w13_ref[...], NT, preferred_element_type=F32)
        dx, dg = _rms_bwd(x_ref[...], g_ref[...], dh)
        dxi_ref[...] = dxo_v + dx
        _accumulate(dg_ref, dg, i == 0)

    return _pc(
        jobs, body, name="ffn_bwd", grid=(s_len // ts,),
        in_specs=[_rows(ts, D), _rows(ts, D), _full((1, D)), _rows(ts, 2 * FF), _full((D, 2 * FF)), _full((FF, D))],
        out_specs=[_rows(ts, D), _rows(ts, 2 * FF), _rows(ts, D), pl.BlockSpec((1, D), lambda i: (0, 0))],
        out_shape=[jax.ShapeDtypeStruct((s_len, D), F32), jax.ShapeDtypeStruct((s_len, 2 * FF), BF),
                   jax.ShapeDtypeStruct((s_len, D), BF), jax.ShapeDtypeStruct((1, D), F32)],
        compiler_params=_params(1),
    )(dxo, x, g, dsab, w13, w2)


def _wgrad(a, b, tk, tn, name, jobs=()):
    s_len, k = a.shape
    n = b.shape[1]

    def body(a_ref, b_ref, o_ref):
        o_ref[...] = lax.dot_general(a_ref[...], b_ref[...], TN, preferred_element_type=F32).astype(BF)

    return _pc(
        jobs, body, name=name, grid=(k // tk, n // tn),
        in_specs=[pl.BlockSpec((s_len, tk), lambda i, j: (0, i)), pl.BlockSpec((s_len, tn), lambda i, j: (0, j))],
        out_specs=pl.BlockSpec((tk, tn), lambda i, j: (i, j)),
        out_shape=jax.ShapeDtypeStruct((k, n), BF),
        compiler_params=_params(2),
    )(a, b)


def _wgrad_groups(a, b, ka, nb, name, jobs=()):
    s_len = a.shape[0]
    groups = a.shape[1] // ka

    def body(a_ref, b_ref, o_ref):
        o_ref[...] = lax.dot_general(a_ref[...], b_ref[...], TN, preferred_element_type=F32).astype(BF)

    return _pc(
        jobs, body, name=name, grid=(groups,),
        in_specs=[pl.BlockSpec((s_len, ka), lambda gi: (0, gi)), pl.BlockSpec((s_len, nb), lambda gi: (0, gi))],
        out_specs=pl.BlockSpec((ka, nb), lambda gi: (gi, 0)),
        out_shape=jax.ShapeDtypeStruct((groups * ka, nb), BF),
        compiler_params=_params(1),
    )(a, b)


def _mix_in(x, g, w_in, jobs=()):
    s_len = x.shape[0]
    ts = min(TOKEN_TILE, s_len)

    def body(x_ref, g_ref, w_ref, h_ref, p_ref):
        _, xh = _rms_stats(x_ref[...])
        h = (xh * g_ref[...]).astype(BF)
        h_ref[...] = h
        p_ref[...] = jnp.dot(h, w_ref[...], preferred_element_type=F32).astype(BF)

    return _pc(
        jobs, body, name="mix_in", grid=(s_len // ts,),
        in_specs=[_rows(ts, D), _full((1, D)), _full((D, INC))],
        out_specs=[_rows(ts, D), _rows(ts, INC)],
        out_shape=[jax.ShapeDtypeStruct((s_len, D), BF), jax.ShapeDtypeStruct((s_len, INC), BF)],
        compiler_params=_params(1),
    )(x, g, w_in)


def _shift(e, j):
    n = e.shape[0]
    j = j % n
    return e if j == 0 else pltpu.roll(e, j, 0)


def _ln_stats(z):
    mu = jnp.mean(z, axis=-1, keepdims=True)
    zc = z - mu
    rs = lax.rsqrt(jnp.mean(zc * zc, axis=-1, keepdims=True) + EPS)
    return rs, zc * rs


def _ln_bwd(rs, zn, dzn):
    return rs * (dzn - jnp.mean(dzn, axis=-1, keepdims=True) - zn * jnp.mean(dzn * zn, axis=-1, keepdims=True))


_GELU_C0 = 0.7978845608028654
_GELU_C1 = 0.044715


def _gelu(p):
    th = jnp.tanh(_GELU_C0 * (p + _GELU_C1 * p * p * p))
    return 0.5 * p * (1.0 + th), th


def _gelu_grad(p, th):
    return 0.5 * (1.0 + th) + 0.5 * p * (1.0 - th * th) * (_GELU_C0 * (1.0 + 3.0 * _GELU_C1 * p * p))


def _pool_diff(a, t, sign):
    outs = []
    for gi in range(NBR):
        win = 2 ** (gi + 1)
        ag = a[:, gi * GW:(gi + 1) * GW]
        cnt = jnp.clip(t + 1, 1, win).astype(F32)
        ws = ag if sign > 0 else ag / cnt
        for s in range(gi + 1):
            ws = ws + _shift(ws, sign * (2 ** s))
        outs.append((ws / cnt if sign > 0 else ws) - ag)
    return outs


def _sgu_mask():
    row = lax.broadcasted_iota(jnp.int32, (GW, GW), 0)
    col = lax.broadcasted_iota(jnp.int32, (GW, GW), 1)
    return (col // CHUNK) <= (row // CHUNK)


def _assemble(pe_ref, prev_ref, cur_ref, next_ref, i, last, ts):
    pe_ref[0:HALO, :] = jnp.where(i > 0, prev_ref[...], jnp.zeros_like(prev_ref))
    pe_ref[HALO:HALO + ts, :] = cur_ref[...]
    if next_ref is not None:
        pe_ref[HALO + ts:, :] = jnp.where(i < last, next_ref[...], jnp.zeros_like(next_ref))


def _halo_specs(ts, width, s_len, with_next):
    per = ts // HALO
    specs = [pl.BlockSpec((HALO, width), lambda i: (jnp.maximum(i * per - 1, 0), 0)),
             pl.BlockSpec((ts, width), lambda i: (i, 0))]
    if with_next:
        specs.append(pl.BlockSpec((HALO, width), lambda i: (jnp.minimum((i + 1) * per, s_len // HALO - 1), 0)))
    return specs


def _branch_weights_specs():
    return [_full((NBR, GW, GW)), _full((1, BW)), _full((SKP, BW)), _full((CKP, BW)), _full((1, BW)), _full((1, BW)),
            _full((1, BW)), _full((1, BW)), _full((NBR, GW, GW)), _full((GW, NBR))]


def _mix_branches_fwd(p, pool_w, pool_scale, sconv_w, cconv_w, cln_g, cln_b, sln_g, sln_b, sgu_w, sgu_bt, jobs=()):
    s_len = p.shape[0]
    ts = min(BRANCH_TILE, s_len)
    ext = HALO + ts

    def body(pp_ref, pc_ref, pw_ref, ps_ref, sw_ref, cw_ref, clg_ref, clb_ref, slg_ref, slb_ref, gw_ref, gb_ref,
             y_ref, z_ref, pe_ref):
        i = pl.program_id(0)
        _assemble(pe_ref, pp_ref, pc_ref, None, i, 0, ts)
        t = i * ts - HALO + lax.broadcasted_iota(jnp.int32, (ext, 1), 0)

        dgs = _pool_diff(pe_ref[:, 0:BW].astype(F32), t, 1)
        for gi in range(NBR):
            e = jnp.dot(dgs[gi][HALO:].astype(BF), pw_ref[gi].astype(BF), preferred_element_type=F32)
            y_ref[:, gi * GW:(gi + 1) * GW] = (e * ps_ref[:, gi * GW:(gi + 1) * GW]).astype(BF)

        xin = pe_ref[:, BW:2 * BW].astype(F32)
        cg = pe_ref[:, 3 * BW:4 * BW].astype(F32)
        q = cg * xin
        cv = sw_ref[2:3, :] * q + sw_ref[1:2, :] * _shift(q, 1) + sw_ref[0:1, :] * _shift(q, 2)
        y_ref[:, BW:2 * BW] = (pe_ref[HALO:, 2 * BW:3 * BW].astype(F32) * cv[HALO:]).astype(BF)

        yg = pe_ref[:, 4 * BW:5 * BW].astype(F32) * _sig(pe_ref[:, 5 * BW:6 * BW].astype(F32))
        z = cw_ref[CK - 1:CK, :] * yg
        for j in range(1, CK):
            z = z + cw_ref[CK - 1 - j:CK - j, :] * _shift(yg, j)
        z_ref[...] = z[HALO:].astype(BF)
        _, zn = _ln_stats(z[HALO:])
        nn = zn * clg_ref[...] + clb_ref[...]
        y_ref[:, 2 * BW:3 * BW] = (nn * _sig(nn)).astype(BF)

        u, _ = _gelu(pc_ref[:, 6 * BW:7 * BW].astype(F32))
        v, _ = _gelu(pc_ref[:, 7 * BW:8 * BW].astype(F32))
        _, vn = _ln_stats(v)
        vn = (vn * slg_ref[...] + slb_ref[...]).astype(BF)
        mask = _sgu_mask()
        for hd in range(NBR):
            wm = jnp.where(mask, gw_ref[hd], 0.0).astype(BF)
            for blk in range(ts // GW):
                rows = slice(blk * GW, (blk + 1) * GW)
                cols = slice(hd * GW, (hd + 1) * GW)
                zz = jnp.dot(wm, vn[rows, cols], preferred_element_type=F32) + gb_ref[:, hd:hd + 1]
                y_ref[rows, 3 * BW + hd * GW:3 * BW + (hd + 1) * GW] = (u[rows, cols] * zz).astype(BF)

    return _pc(
        jobs, body, name="mix_branches_fwd", grid=(s_len // ts,),
        in_specs=_halo_specs(ts, MIXC, s_len, False) + _branch_weights_specs(),
        out_specs=[_rows(ts, NBR * BW), _rows(ts, BW)],
        out_shape=[jax.ShapeDtypeStruct((s_len, NBR * BW), BF), jax.ShapeDtypeStruct((s_len, BW), BF)],
        scratch_shapes=[pltpu.VMEM((ext, MIXC), BF)],
        compiler_params=_params(1),
    )(p, p, pool_w, pool_scale, sconv_w, cconv_w, cln_g, cln_b, sln_g, sln_b, sgu_w, sgu_bt)


def _mix_branches_bwd(p, dy, z, dp, pool_w, pool_scale, sconv_w, cconv_w, cln_g, cln_b, sln_g, sln_b, sgu_w, sgu_bt,
                      jobs=()):
    s_len = p.shape[0]
    ts = min(BRANCH_TILE, s_len)
    ext = ts + 2 * HALO
    last = s_len // ts - 1
    tile = slice(HALO, HALO + ts)
    small_shapes = [(NBR, GW, GW), (1, BW), (SKP, BW), (CKP, BW), (1, BW), (1, BW), (1, BW), (1, BW), (NBR, GW, GW),
                    (NBR, GW, GW)]

    def body(pp_ref, pc_ref, pn_ref, dyp_ref, dyc_ref, dyn_ref, zc_ref, zn_ref, dpin_ref,
             pw_ref, ps_ref, sw_ref, cw_ref, clg_ref, clb_ref, slg_ref, slb_ref, gw_ref, gb_ref,
             dp_ref, dpw_ref, dps_ref, dsw_ref, dcw_ref, dclg_ref, dclb_ref, dslg_ref, dslb_ref, dgw_ref, dgb_ref,
             pe_ref, de_ref):
        del dyp_ref, dpin_ref
        i = pl.program_id(0)
        first = i == 0
        _assemble(pe_ref, pp_ref, pc_ref, pn_ref, i, last, ts)
        de_ref[0:HALO, :] = jnp.zeros((HALO, NBR * BW), BF)
        de_ref[HALO:HALO + ts, :] = dyc_ref[...]
        de_ref[HALO + ts:, :] = jnp.where(i < last, dyn_ref[...], jnp.zeros_like(dyn_ref))
        t = i * ts - HALO + lax.broadcasted_iota(jnp.int32, (ext, 1), 0)

        @pl.when(first)
        def _():
            dsw_ref[...] = jnp.zeros((SKP, BW), F32)
            dcw_ref[...] = jnp.zeros((CKP, BW), F32)

        dgs = _pool_diff(pe_ref[:, 0:BW].astype(F32), t, 1)
        dya = de_ref[:, 0:BW].astype(F32)
        dds = []
        for gi in range(NBR):
            cols = slice(gi * GW, (gi + 1) * GW)
            pw = pw_ref[gi].astype(BF)
            d_t = dgs[gi][tile].astype(BF)
            e = jnp.dot(d_t, pw, preferred_element_type=F32)
            _accumulate(dps_ref.at[:, cols], jnp.sum(dya[tile, cols] * e, axis=0, keepdims=True), first)
            de_g = (dya[:, cols] * ps_ref[:, cols]).astype(BF)
            _accumulate(dpw_ref.at[gi], lax.dot_general(d_t, de_g[tile], TN, preferred_element_type=F32), first)
            dds.append(lax.dot_general(de_g, pw, NT, preferred_element_type=F32))
        das = _pool_diff(jnp.concatenate(dds, axis=1), t, -1)
        for gi in range(NBR):
            dp_ref[:, gi * GW:(gi + 1) * GW] = das[gi][tile].astype(BF)

        xin = pe_ref[:, BW:2 * BW].astype(F32)
        bg = pe_ref[:, 2 * BW:3 * BW].astype(F32)
        cg = pe_ref[:, 3 * BW:4 * BW].astype(F32)
        q = cg * xin
        qs = [q, _shift(q, 1), _shift(q, 2)]
        cv = sw_ref[2:3, :] * qs[0] + sw_ref[1:2, :] * qs[1] + sw_ref[0:1, :] * qs[2]
        dyb = de_ref[:, BW:2 * BW].astype(F32)
        dcv = dyb * bg
        for j in range(SK):
            dsw_ref[SK - 1 - j:SK - j, :] += jnp.sum(dcv[tile] * qs[j][tile], axis=0, keepdims=True)
        dq = sw_ref[2:3, :] * dcv + sw_ref[1:2, :] * _shift(dcv, -1) + sw_ref[0:1, :] * _shift(dcv, -2)
        dp_ref[:, BW:2 * BW] = (dq * cg)[tile].astype(BF)
        dp_ref[:, 2 * BW:3 * BW] = (dyb * cv)[tile].astype(BF)
        dp_ref[:, 3 * BW:4 * BW] = (dq * xin)[tile].astype(BF)

        ca = pc_ref[:, 4 * BW:5 * BW].astype(F32)
        sb = _sig(pc_ref[:, 5 * BW:6 * BW].astype(F32))
        yg_t = ca * sb
        z = jnp.concatenate([zc_ref[...].astype(F32), zn_ref[...].astype(F32)], axis=0)
        rs, zn = _ln_stats(z)
        nn = zn * clg_ref[...] + clb_ref[...]
        sn = _sig(nn)
        dn = de_ref[HALO:, 2 * BW:3 * BW].astype(F32) * (sn * (1.0 + nn * (1.0 - sn)))
        _accumulate(dclg_ref, jnp.sum((dn * zn)[:ts], axis=0, keepdims=True), first)
        _accumulate(dclb_ref, jnp.sum(dn[:ts], axis=0, keepdims=True), first)
        dz = _ln_bwd(rs, zn, dn * clg_ref[...])
        dyg = cw_ref[CK - 1:CK, :] * dz[:ts]
        dcw_ref[CK - 1:CK, :] += jnp.sum(dz[:ts] * yg_t, axis=0, keepdims=True)
        for j in range(1, CK):
            dz_ahead = _shift(dz, -j)[:ts]
            dyg = dyg + cw_ref[CK - 1 - j:CK - j, :] * dz_ahead
            dcw_ref[CK - 1 - j:CK - j, :] += jnp.sum(dz_ahead * yg_t, axis=0, keepdims=True)
        dp_ref[:, 4 * BW:5 * BW] = (dyg * sb).astype(BF)
        dp_ref[:, 5 * BW:6 * BW] = (dyg * ca * sb * (1.0 - sb)).astype(BF)

        pu = pc_ref[:, 6 * BW:7 * BW].astype(F32)
        pv = pc_ref[:, 7 * BW:8 * BW].astype(F32)
        u, thu = _gelu(pu)
        v, thv = _gelu(pv)
        vrs, vn0 = _ln_stats(v)
        vn = (vn0 * slg_ref[...] + slb_ref[...]).astype(BF)
        dyd = dyc_ref[:, 3 * BW:4 * BW].astype(F32)
        dzz = dyd * u
        dzb = dzz.astype(BF)
        mask = _sgu_mask()
        dvn_cols = []
        for hd in range(NBR):
            cols = slice(hd * GW, (hd + 1) * GW)
            wm = jnp.where(mask, gw_ref[hd], 0.0).astype(BF)
            dwm = jnp.zeros((GW, GW), F32)
            dbs = jnp.zeros((GW, GW), F32)
            dvn_rows = []
            for blk in range(ts // GW):
                rows = slice(blk * GW, (blk + 1) * GW)
                zz = jnp.dot(wm, vn[rows, cols], preferred_element_type=F32) + gb_ref[:, hd:hd + 1]
                dp_ref[rows, 6 * BW + hd * GW:6 * BW + (hd + 1) * GW] = (
                    dyd[rows, cols] * zz * _gelu_grad(pu[rows, cols], thu[rows, cols])).astype(BF)
                dwm = dwm + lax.dot_general(dzb[rows, cols], vn[rows, cols], NT, preferred_element_type=F32)
                dbs = dbs + dzz[rows, cols]
                dvn_rows.append(lax.dot_general(wm, dzb[rows, cols], TN, preferred_element_type=F32))
            _accumulate(dgw_ref.at[hd], jnp.where(mask, dwm, 0.0), first)
            _accumulate(dgb_ref.at[hd], dbs, first)
            dvn_cols.append(jnp.concatenate(dvn_rows, axis=0))
        dvn = jnp.concatenate(dvn_cols, axis=1)
        _accumulate(dslg_ref, jnp.sum(dvn * vn0, axis=0, keepdims=True), first)
        _accumulate(dslb_ref, jnp.sum(dvn, axis=0, keepdims=True), first)
        dv = _ln_bwd(vrs, vn0, dvn * slg_ref[...])
        dp_ref[:, 7 * BW:8 * BW] = (dv * _gelu_grad(pv, thv)).astype(BF)

    const = lambda shp: pl.BlockSpec(shp, lambda i: (0,) * len(shp))
    return _pc(
        jobs, body, name="mix_branches_bwd", grid=(s_len // ts,),
        in_specs=(_halo_specs(ts, MIXC, s_len, True) + _halo_specs(ts, NBR * BW, s_len, True)
                  + _halo_specs(ts, BW, s_len, True)[1:] + [pl.BlockSpec(memory_space=pl.ANY)] + _branch_weights_specs()),
        out_specs=[pl.BlockSpec((ts, MIXC), lambda i: (i, 0))] + [const(s) for s in small_shapes],
        out_shape=[jax.ShapeDtypeStruct((s_len, INC), BF)] + [jax.ShapeDtypeStruct(s, F32) for s in small_shapes],
        scratch_shapes=[pltpu.VMEM((ext, MIXC), BF), pltpu.VMEM((ext, NBR * BW), BF)],
        input_output_aliases={8: 0},
        compiler_params=_params(1),
    )(p, p, p, dy, dy, dy, z, z, dp, pool_w, pool_scale, sconv_w, cconv_w, cln_g, cln_b, sln_g, sln_b, sgu_w, sgu_bt)


def _mix_out_fwd(x, y, p, w_up, w_out, jobs=()):
    s_len = x.shape[0]
    ts = min(TOKEN_TILE if jobs else 2 * TOKEN_TILE, s_len)

    def body(x_ref, y_ref, pg_ref, wu_ref, wo_ref, xo_ref, m_ref, up_ref):
        m = jnp.zeros((ts, D), F32)
        for gi in range(NBR):
            up = jnp.dot(y_ref[:, gi * BW:(gi + 1) * BW], wu_ref[gi], preferred_element_type=F32)
            up_ref[:, gi * D:(gi + 1) * D] = up.astype(BF)
            m = m + _sig(pg_ref[:, gi * D:(gi + 1) * D].astype(F32)) * up
        mb = m.astype(BF)
        m_ref[...] = mb
        xo_ref[...] = x_ref[...] + jnp.dot(mb, wo_ref[...], preferred_element_type=F32)

    return _pc(
        jobs, body, name="mix_out_fwd", grid=(s_len // ts,),
        in_specs=[_rows(ts, D), _rows(ts, NBR * BW), _rows(ts, NBR * D, 1), _full((NBR, BW, D)), _full((D, D))],
        out_specs=[_rows(ts, D), _rows(ts, D), _rows(ts, NBR * D)],
        out_shape=[jax.ShapeDtypeStruct((s_len, D), F32), jax.ShapeDtypeStruct((s_len, D), BF),
                   jax.ShapeDtypeStruct((s_len, NBR * D), BF)],
        compiler_params=_params(1),
    )(x, y, p, w_up, w_out)


def _mix_out_bwd(dxo, up, p, w_up, w_out, jobs=()):
    s_len = dxo.shape[0]
    ts = min(TOKEN_TILE, s_len)

    def body(dxo_ref, up_ref, pg_ref, wu_ref, wo_ref, dy_ref, dp_ref, dup_ref, dxb_ref):
        dxb = dxo_ref[...].astype(BF)
        dxb_ref[...] = dxb
        dm = lax.dot_general(dxb, wo_ref[...], NT, preferred_element_type=F32)
        for gi in range(NBR):
            cols = slice(gi * D, (gi + 1) * D)
            gate = _sig(pg_ref[:, cols].astype(F32))
            dp_ref[:, cols] = (dm * up_ref[:, cols].astype(F32) * gate * (1.0 - gate)).astype(BF)
            dup = (dm * gate).astype(BF)
            dup_ref[:, cols] = dup
            dy_ref[:, gi * BW:(gi + 1) * BW] = lax.dot_general(
                dup, wu_ref[gi], NT, preferred_element_type=F32).astype(BF)

    return _pc(
        jobs, body, name="mix_out_bwd", grid=(s_len // ts,),
        in_specs=[_rows(ts, D), _rows(ts, NBR * D), _rows(ts, NBR * D, 1), _full((NBR, BW, D)), _full((D, D))],
        out_specs=[_rows(ts, NBR * BW), _rows(ts, NBR * D, 1), _rows(ts, NBR * D), _rows(ts, D)],
        out_shape=[jax.ShapeDtypeStruct((s_len, NBR * BW), BF), jax.ShapeDtypeStruct((s_len, INC), BF),
                   jax.ShapeDtypeStruct((s_len, NBR * D), BF), jax.ShapeDtypeStruct((s_len, D), BF)],
        compiler_params=_params(1),
    )(dxo, up, p, w_up, w_out)


def _mix_in_bwd(dp, w_in, x, g, dxo, jobs=()):
    s_len = x.shape[0]
    ts = min(TOKEN_TILE, s_len)

    def body(dp_ref, w_ref, x_ref, g_ref, dxo_ref, dxi_ref, dg_ref):
        i = pl.program_id(0)
        dh = lax.dot_general(dp_ref[...], w_ref[...], NT, preferred_element_type=F32)
        dx, dg = _rms_bwd(x_ref[...], g_ref[...], dh)
        dxi_ref[...] = dxo_ref[...] + dx
        _accumulate(dg_ref, dg, i == 0)

    return _pc(
        jobs, body, name="mix_in_bwd", grid=(s_len // ts,),
        in_specs=[_rows(ts, INC), _full((D, INC)), _rows(ts, D), _full((1, D)), _rows(ts, D)],
        out_specs=[_rows(ts, D), pl.BlockSpec((1, D), lambda i: (0, 0))],
        out_shape=[jax.ShapeDtypeStruct((s_len, D), F32), jax.ShapeDtypeStruct((1, D), F32)],
        compiler_params=_params(1),
    )(dp, w_in, x, g, dxo)


def _loss_head(x, g, target, jobs=()):
    s_len = x.shape[0]
    ts = min(512, s_len)

    def body(x_ref, g_ref, t_ref, dx_ref, dg_ref, loss_ref):
        i = pl.program_id(0)
        xv = x_ref[...]
        _, xh = _rms_stats(xv)
        err = xh * g_ref[...] - t_ref[...]
        part = 0.5 * jnp.sum(jnp.mean(err * err, axis=-1, keepdims=True), axis=0, keepdims=True)
        dx, dg = _rms_bwd(xv, g_ref[...], err * (1.0 / D))
        dx_ref[...] = dx
        _accumulate(dg_ref, dg, i == 0)
        _accumulate(loss_ref, jnp.broadcast_to(part, (1, GW)), i == 0)

    return _pc(
        jobs, body, name="loss_head", grid=(s_len // ts,),
        in_specs=[_rows(ts, D), _full((1, D)), _rows(ts, D)],
        out_specs=[_rows(ts, D), pl.BlockSpec((1, D), lambda i: (0, 0)), pl.BlockSpec((1, GW), lambda i: (0, 0))],
        out_shape=[jax.ShapeDtypeStruct((s_len, D), F32), jax.ShapeDtypeStruct((1, D), F32),
                   jax.ShapeDtypeStruct((1, GW), F32)],
        compiler_params=_params(1),
    )(x, g, target)


SUM_TILE = 1 << 20
ADAM_TILE = 1 << 19
CAST_TILE = 1 << 19


def _row_tile(rows, cols, budget=1 << 18):
    tr = rows
    while tr * cols > budget and tr % 16 == 0:
        tr //= 2
    return tr


def _elementwise(fn, name, ins, out_dtypes):
    rows, cols = ins[0].shape
    tr = _row_tile(rows, cols)
    n_in = len(ins)

    def body(*refs):
        res = fn(*[r[...] for r in refs[:n_in]])
        for o_ref, val in zip(refs[n_in:], res):
            o_ref[...] = val.astype(o_ref.dtype)

    outs = pl.pallas_call(
        body, name=name, grid=(rows // tr,),
        in_specs=[_rows(tr, cols)] * n_in, out_specs=[_rows(tr, cols)] * len(out_dtypes),
        out_shape=[jax.ShapeDtypeStruct((rows, cols), dt) for dt in out_dtypes],
        compiler_params=_params(1),
    )(*ins)
    return outs


def _tiled(fn, name, grid, in_specs, out_specs, out_shape, args, scalars=None, alias=None):
    alias = alias or {}
    n_in = len(in_specs) - len(alias)
    n_pre = 0 if scalars is None else 1

    def body(*refs):
        refs = refs[n_pre:]
        res = fn(*[r[...] for r in refs[:n_in]])
        for o_ref, val in zip(refs[len(in_specs):], res):
            o_ref[...] = val.astype(o_ref.dtype)

    aliases = {n_pre + pos: out for pos, out in alias.items()}
    if scalars is None:
        return pl.pallas_call(body, name=name, grid=grid, in_specs=in_specs, out_specs=out_specs, out_shape=out_shape,
                              input_output_aliases=aliases, compiler_params=_params(len(grid)))(*args)
    spec = pltpu.PrefetchScalarGridSpec(num_scalar_prefetch=1, grid=grid, in_specs=in_specs, out_specs=out_specs)
    return pl.pallas_call(body, name=name, grid_spec=spec, out_shape=out_shape, input_output_aliases=aliases,
                          compiler_params=_params(len(grid)))(scalars, *args)


def _cast_into(shard, layer, col, dtype, sc):
    ks, ns = shard.shape[1:]
    tr = _row_tile(ks, ns, SUM_TILE)
    full = (ks, ns * NCHIP) if col else (ks * NCHIP, ns)
    out_idx = (lambda i, s: (i, s[1])) if col else (lambda i, s: (s[1] * (ks // tr) + i, 0))
    return _tiled(lambda v: (v,), "cast_into", (ks // tr,), [pl.BlockSpec((None, tr, ns), lambda i, s: (layer, i, 0))],
                  [pl.BlockSpec((tr, ns), out_idx)], [jax.ShapeDtypeStruct(full, dtype)], [shard], sc)[0]


def _cast_rest(items, jobs=()):
    n = len(items)
    tiles, slot_sets, counts = [], {}, {}
    for k, (shard, _, _, dt) in enumerate(items):
        ks, ns = shard.shape[1:]
        tr = _row_tile(ks, ns, CAST_TILE)
        which = slot_sets.setdefault((tr, ns, shard.dtype, dt), len(slot_sets))
        for t in range(ks // tr):
            tiles.append((k, t, tr, which, counts.get(which, 0) % 2))
            counts[which] = counts.get(which, 0) + 1
    n_sets = len(slot_sets)

    def body(*refs):
        ins, outs = refs[:n], refs[n:2 * n]
        bufs = refs[2 * n:2 * n + 2 * n_sets]
        in_sems, out_sems = refs[2 * n + 2 * n_sets:]
        chip = 2 * lax.axis_index("x") + lax.axis_index("y")
        fetch, store = [], []
        for k, t, tr, which, slot in tiles:
            shard, layer, col, _ = items[k]
            ks, ns = shard.shape[1:]
            if col:
                place = outs[k].at[pl.ds(t * tr, tr), pl.ds(pl.multiple_of(chip * ns, GW), ns)]
            else:
                place = outs[k].at[pl.ds(pl.multiple_of(chip * ks + t * tr, 16), tr), :]
            fetch.append(pltpu.make_async_copy(
                ins[k].at[layer, pl.ds(t * tr, tr), :], bufs[2 * which].at[slot], in_sems.at[which, slot]))
            store.append(pltpu.make_async_copy(bufs[2 * which + 1].at[slot], place, out_sems.at[which, slot]))
        busy = {}
        fetch[0].start()
        for i, (k, t, tr, which, slot) in enumerate(tiles):
            if i + 1 < len(tiles):
                fetch[i + 1].start()
            fetch[i].wait()
            if (which, slot) in busy:
                store[busy[which, slot]].wait()
            bufs[2 * which + 1][slot] = bufs[2 * which][slot].astype(bufs[2 * which + 1].dtype)
            store[i].start()
            busy[which, slot] = i
        for i in busy.values():
            store[i].wait()

    def full_shape(shard, col):
        ks, ns = shard.shape[1:]
        return (ks, ns * NCHIP) if col else (ks * NCHIP, ns)

    scratch = []
    for (tr, ns, dt_in, dt_out) in slot_sets:
        scratch += [pltpu.VMEM((2, tr, ns), dt_in), pltpu.VMEM((2, tr, ns), dt_out)]
    scratch += [pltpu.SemaphoreType.DMA((n_sets, 2)), pltpu.SemaphoreType.DMA((n_sets, 2))]
    return _pc(
        jobs, body, name="cast_rest", in_specs=[ANY] * n, out_specs=[ANY] * n,
        out_shape=[jax.ShapeDtypeStruct(full_shape(shard, col), dt) for shard, _, col, dt in items],
        scratch_shapes=scratch, compiler_params=pltpu.CompilerParams(vmem_limit_bytes=VMEM_LIMIT),
    )(*[shard for shard, _, _, _ in items])


def _pair_sum(g, got, col, sc):
    hk, hn = got.shape
    tr = _row_tile(hk, hn, SUM_TILE)
    g_idx = (lambda i, s: (s[0] * (hk // tr) + i, 0)) if col else (lambda i, s: (i, s[0]))
    plain = pl.BlockSpec((tr, hn), lambda i, s: (i, 0))
    return _tiled(lambda a, b: (a.astype(F32) + b.astype(F32),), "pair_sum", (hk // tr,),
                  [pl.BlockSpec((tr, hn), g_idx), plain], [plain], [jax.ShapeDtypeStruct((hk, hn), BF)], [g, got], sc)[0]


def _chip_sum(ph, got, col, layer, depth, sc, carry):
    qk, qn = got.shape[1:]
    tr = _row_tile(qk, qn, SUM_TILE)
    ph_idx = (lambda i, s: (i, s[1])) if col else (lambda i, s: (s[1] * (qk // tr) + i, 0))
    out_shape = (depth, 2 * qk, qn) if col else (depth, qk, 2 * qn)
    out_idx = (lambda i, s: (layer, s[0] * (qk // tr) + i, 0)) if col else (lambda i, s: (layer, i, s[0]))
    in_specs = [pl.BlockSpec((tr, qn), ph_idx)] + [pl.BlockSpec((None, tr, qn), lambda i, s, j=j: (j, i, 0)) for j in range(3)]
    args = [ph, got, got, got]
    if carry is not None:
        in_specs.append(ANY)
        args.append(carry)
    return _tiled(lambda a, b, c_, d_: (a.astype(F32) + b.astype(F32) + c_.astype(F32) + d_.astype(F32),), "chip_sum",
                  (qk // tr,), in_specs, [pl.BlockSpec((None, tr, qn), out_idx)],
                  [jax.ShapeDtypeStruct(out_shape, F32)], args, sc, alias=None if carry is None else {4: 0})[0]


def _adamw_math(w, g, m, v):
    m = ADAM_B1 * m + (1.0 - ADAM_B1) * g
    v = ADAM_B2 * v + (1.0 - ADAM_B2) * (g * g)
    m_hat = m / (1.0 - ADAM_B1 ** ADAM_STEP)
    v_hat = v / (1.0 - ADAM_B2 ** ADAM_STEP)
    delta = -ADAM_LR * (m_hat / (jnp.sqrt(v_hat) + ADAM_EPS) + ADAM_WD * w)
    return delta, m, v


def _adamw_layer(w, g, m, v, layer, carry):
    k, n = w.shape[1:]
    tr = _row_tile(k, n, ADAM_TILE)
    blk = pl.BlockSpec((None, tr, n), lambda i: (layer, i, 0))
    carry = list(carry or [])
    return _tiled(lambda *a: (*_adamw_math(*a), a[1]), "adamw_layer", (k // tr,), [blk] * 4 + [ANY] * len(carry),
                  [blk] * 4, [jax.ShapeDtypeStruct(w.shape, F32)] * 4, [w, g, m, v] + carry, None,
                  alias={4 + pos: pos for pos in range(len(carry))})


def _adamw(w, g, m, v):
    shape = w.shape
    two_d = lambda a: a.reshape(-1, shape[-1])
    outs = _elementwise(_adamw_math, "adamw", [two_d(w), two_d(g), two_d(m), two_d(v)], [F32, F32, F32])
    return [o.reshape(shape) for o in outs]


ANY = pl.BlockSpec(memory_space=pl.ANY)


def _place():
    x, y, c = lax.axis_index("x"), lax.axis_index("y"), lax.axis_index("c")
    chips = [(1 - x, y), (x, 1 - y), (1 - x, 1 - y)]
    return x, y, c, chips


def _cols(ref, start, size):
    idx = (slice(None),) * (len(ref.shape) - 1) + (pl.ds(pl.multiple_of(start, GW), size),)
    return ref.at[idx]


def _rows_of(ref, start, size):
    nd = len(ref.shape)
    idx = (slice(None),) * (nd - 2) + (pl.ds(pl.multiple_of(start, 16), size), slice(None))
    return ref.at[idx]


def _region(ref, col_sharded, chip, half):
    k, n = ref.shape
    align = 16 if ref.dtype == BF else 8
    if col_sharded:
        return ref.at[pl.ds(pl.multiple_of(half * (k // 2), align), k // 2),
                      pl.ds(pl.multiple_of(chip * (n // NCHIP), GW), n // NCHIP)]
    rows = k // (2 * NCHIP)
    return ref.at[pl.ds(pl.multiple_of((2 * chip + half) * rows, align), rows), :]


def _job_gather(bufs, col_sharded, handoff):
    n = len(bufs)

    def copies(outs, send_sems, recv_sems, stage):
        x, y, c, chips = _place()
        sends, lands = [], []
        for k in range(n):
            for j, chip in enumerate(chips):
                theirs = 2 * chip[0] + chip[1]
                if stage == 0:
                    src, to = _region(outs[k], col_sharded[k], 2 * x + y, c), (*chip, c)
                    land = _region(outs[k], col_sharded[k], theirs, c)
                else:
                    src, to = _region(outs[k], col_sharded[k], theirs, c), (x, y, 1 - c)
                    land = _region(outs[k], col_sharded[k], theirs, 1 - c)
                sem = 3 * n * stage + 3 * k + j
                sems = dict(send_sem=send_sems.at[sem], recv_sem=recv_sems.at[sem], device_id=to, device_id_type=MESH)
                sends.append(pltpu.make_async_remote_copy(src_ref=src, dst_ref=src, **sems))
                lands.append(pltpu.make_async_remote_copy(src_ref=land, dst_ref=land, **sems))
        return sends, lands

    def start(ins, outs, send_sems, recv_sems):
        for cp in copies(outs, send_sems, recv_sems, 0)[0]:
            cp.start()

    def hand_on(ins, outs, send_sems, recv_sems):
        for cp in copies(outs, send_sems, recv_sems, 0)[1]:
            cp.wait_recv()
        for cp in copies(outs, send_sems, recv_sems, 1)[0]:
            cp.start()

    def finish(ins, outs, send_sems, recv_sems):
        sends, lands = copies(outs, send_sems, recv_sems, 1)
        for cp in lands:
            cp.wait_recv()
        for cp in copies(outs, send_sems, recv_sems, 0)[0] + sends:
            cp.wait_send()

    return _Job(bufs, n, [], 6 * n, [(0.0, start), (handoff, hand_on), (1.0, finish)])


def _half(ref, col_sharded, c):
    k, n = ref.shape[-2:]
    return _rows_of(ref, c * (k // 2), k // 2) if col_sharded else _cols(ref, c * (n // 2), n // 2)


def _quarter(ref, col_sharded, j):
    k, n = ref.shape[-2:]
    return _cols(ref, j * (n // NCHIP), n // NCHIP) if col_sharded else _rows_of(ref, j * (k // NCHIP), k // NCHIP)


def _job_pair(grads, col_sharded):
    n = len(grads)

    def half_shape(g, col):
        return (g.shape[0] // 2, g.shape[1]) if col else (g.shape[0], g.shape[1] // 2)

    def copies(ins, got, send_sems, recv_sems):
        x, y, c, _ = _place()
        return [pltpu.make_async_remote_copy(
            src_ref=_half(ins[k], col_sharded[k], 1 - c), dst_ref=got[k], send_sem=send_sems.at[k],
            recv_sem=recv_sems.at[k], device_id=(x, y, 1 - c), device_id_type=MESH) for k in range(n)]

    def start(*refs):
        for cp in copies(*refs):
            cp.start()

    def finish(*refs):
        for cp in copies(*refs):
            cp.wait()

    fresh = [jax.ShapeDtypeStruct(half_shape(g, col), g.dtype) for g, col in zip(grads, col_sharded)]
    return _Job(grads, 0, fresh, n, [(0.0, start), (1.0, finish)])


def _job_chip(halves, col_sharded):
    n = len(halves)

    def quarter_shape(h, col):
        return (3, h.shape[0], h.shape[1] // NCHIP) if col else (3, h.shape[0] // NCHIP, h.shape[1])

    def copies(ins, got, send_sems, recv_sems):
        x, y, c, chips = _place()
        return [pltpu.make_async_remote_copy(
            src_ref=_quarter(ins[k], col_sharded[k], 2 * chip[0] + chip[1]), dst_ref=got[k].at[j],
            send_sem=send_sems.at[3 * k + j], recv_sem=recv_sems.at[3 * k + j], device_id=(*chip, c), device_id_type=MESH)
            for k in range(n) for j, chip in enumerate(chips)]

    def start(*refs):
        for cp in copies(*refs):
            cp.start()

    def finish(*refs):
        for cp in copies(*refs):
            cp.wait()

    fresh = [jax.ShapeDtypeStruct(quarter_shape(h, col), h.dtype) for h, col in zip(halves, col_sharded)]
    return _Job(halves, 0, fresh, 3 * n, [(0.0, start), (1.0, finish)])


def _job_sibling(shards, col_sharded, layers):
    n = len(shards)

    def copies(outs, send_sems, recv_sems):
        x, y, c, _ = _place()
        sends, lands = [], []
        for k in range(n):
            sems = dict(send_sem=send_sems.at[k], recv_sem=recv_sems.at[k], device_id=(x, y, 1 - c), device_id_type=MESH)
            mine = _half(outs[k].at[layers[k]], col_sharded[k], c)
            theirs = _half(outs[k].at[layers[k]], col_sharded[k], 1 - c)
            sends.append(pltpu.make_async_remote_copy(src_ref=mine, dst_ref=mine, **sems))
            lands.append(pltpu.make_async_remote_copy(src_ref=theirs, dst_ref=theirs, **sems))
        return sends, lands

    def start(ins, outs, send_sems, recv_sems):
        for cp in copies(outs, send_sems, recv_sems)[0]:
            cp.start()

    def finish(ins, outs, send_sems, recv_sems):
        sends, lands = copies(outs, send_sems, recv_sems)
        for cp in lands:
            cp.wait_recv()
        for cp in sends:
            cp.wait_send()

    return _Job(shards, n, [], n, [(0.0, start), (1.0, finish)])


def _standalone(jobs, name):
    return _pc(jobs, lambda: None, name=name, in_specs=[], out_specs=[], out_shape=[])()[1]


def _all_reduce_small(buf, jobs=()):
    rows = buf.shape[0]
    per = rows // NDEV
    flips = [(fx, fy, fc) for fx in (0, 1) for fy in (0, 1) for fc in (0, 1)][1:]

    def body(in_ref, out_ref, got_ref, send_sems, recv_sems):
        x, y, c, _ = _place()
        me = 4 * x + 2 * y + c

        def peer(f):
            return tuple(1 - pos if flip else pos for pos, flip in zip((x, y, c), f))

        def block(ref, dev):
            return ref.at[pl.ds(pl.multiple_of(dev * per, 8), per), :]

        scatter = []
        for k, f in enumerate(flips):
            px, py, pc = peer(f)
            scatter.append(pltpu.make_async_remote_copy(
                src_ref=block(in_ref, 4 * px + 2 * py + pc), dst_ref=got_ref.at[k], send_sem=send_sems.at[k],
                recv_sem=recv_sems.at[k], device_id=(px, py, pc), device_id_type=MESH))
        for cp in scatter:
            cp.start()
        for cp in scatter:
            cp.wait()
        total = block(in_ref, me)[...]
        for k in range(len(flips)):
            total = total + got_ref[k]
        block(out_ref, me)[...] = total
        share = []
        for k, f in enumerate(flips):
            share.append(pltpu.make_async_remote_copy(
                src_ref=block(out_ref, me), dst_ref=block(out_ref, me), send_sem=send_sems.at[7 + k],
                recv_sem=recv_sems.at[7 + k], device_id=peer(f), device_id_type=MESH))
        for cp in share:
            cp.start()
        for k, f in enumerate(flips):
            share[k].wait_send()
            px, py, pc = peer(f)
            theirs = block(out_ref, 4 * px + 2 * py + pc)
            pltpu.make_async_remote_copy(
                src_ref=theirs, dst_ref=theirs, send_sem=send_sems.at[7 + k], recv_sem=recv_sems.at[7 + k],
                device_id=(px, py, pc), device_id_type=MESH).wait_recv()

    vmem = pl.BlockSpec(memory_space=pltpu.VMEM)
    return _pc(
        jobs, body, name="all_reduce_small", in_specs=[vmem], out_specs=vmem,
        out_shape=jax.ShapeDtypeStruct((rows, GW), F32),
        scratch_shapes=[pltpu.VMEM((NDEV - 1, per, GW), F32), pltpu.SemaphoreType.DMA((14,)),
                        pltpu.SemaphoreType.DMA((14,))],
    )(buf)


BIG = ("ffn1_w13", "ffn1_w2", "w_in", "w_up", "w_out", "ffn2_w13", "ffn2_w2")
BIG_COL_SHARDED = (True, False, True, True, False, True, False)
SMALL = ("ffn1_norm", "mix_norm", "pool_w", "pool_scale", "sconv_w", "cconv_w", "cconv_ln_g", "cconv_ln_b",
         "sgu_ln_g", "sgu_ln_b", "sgu_w", "sgu_b", "ffn2_norm", "final_norm")
WEIGHTS = ("ffn1_norm", "ffn1_w13", "ffn1_w2", "mix_norm", "w_in", "pool_w", "pool_scale", "sconv_w", "cconv_w",
           "cconv_ln_g", "cconv_ln_b", "sgu_ln_g", "sgu_ln_b", "sgu_w", "sgu_b", "w_up", "w_out", "ffn2_norm",
           "ffn2_w13", "ffn2_w2", "final_norm")


def _pad_rows(a, rows):
    return jnp.pad(a, ((0, 0), (0, rows - a.shape[1]), (0, 0)))


def kernel(x, ffn1_norm, ffn1_w13, ffn1_w2, mix_norm, w_in, pool_w, pool_scale, sconv_w, cconv_w, cconv_ln_g, cconv_ln_b, sgu_ln_g, sgu_ln_b, sgu_w, sgu_b, w_up, w_out, ffn2_norm, ffn2_w13, ffn2_w2, final_norm, loss_target, m_ffn1_norm, m_ffn1_w13, m_ffn1_w2, m_mix_norm, m_w_in, m_pool_w, m_pool_scale, m_sconv_w, m_cconv_w, m_cconv_ln_g, m_cconv_ln_b, m_sgu_ln_g, m_sgu_ln_b, m_sgu_w, m_sgu_b, m_w_up, m_w_out, m_ffn2_norm, m_ffn2_w13, m_ffn2_w2, m_final_norm, v_ffn1_norm, v_ffn1_w13, v_ffn1_w2, v_mix_norm, v_w_in, v_pool_w, v_pool_scale, v_sconv_w, v_cconv_w, v_cconv_ln_g, v_cconv_ln_b, v_sgu_ln_g, v_sgu_ln_b, v_sgu_w, v_sgu_b, v_w_up, v_w_out, v_ffn2_norm, v_ffn2_w13, v_ffn2_w2, v_final_norm):
    args = dict(locals())
    w = {nm: args[nm] for nm in WEIGHTS}
    m = {nm: args["m_" + nm] for nm in WEIGHTS}
    v = {nm: args["v_" + nm] for nm in WEIGHTS}
    depth = ffn1_w13.shape[0]
    chip = 2 * lax.axis_index("x") + lax.axis_index("y")

    sc = jnp.stack([lax.axis_index("c"), chip]).astype(jnp.int32)
    col_of = dict(zip(BIG, BIG_COL_SHARDED), sconv_w=True, cconv_w=True)
    sources = {nm: (w[nm], BF) for nm in BIG}
    sources["w_up"] = (w_up.reshape(depth, NBR * BW, w_up.shape[-1]), BF)
    sources["sconv_w"] = (_pad_rows(sconv_w, 2 * SKP), F32)
    sources["cconv_w"] = (_pad_rows(cconv_w, 2 * CKP), F32)
    first_group = ("ffn1_w13", "ffn1_w2")
    full = [dict() for _ in range(depth)]
    for nm in first_group:
        full[0][nm] = _cast_into(sources[nm][0], 0, col_of[nm], sources[nm][1], sc)

    def gather_job(l, names, handoff):
        return _job_gather([full[l][nm] for nm in names], [col_of[nm] for nm in names], handoff)

    def gather_with(l, names, handoff, call):
        res, job_outs = call([gather_job(l, names, handoff)] if l < depth else [])
        if l < depth:
            full[l].update(zip(names, job_outs[0]))
        return res

    rest = [(nm, l) for l in range(depth) for nm in sources if not (l == 0 and nm in first_group)]
    casted = gather_with(0, first_group, 1.0, lambda jobs: _cast_rest(
        [(sources[nm][0], l, col_of[nm], sources[nm][1]) for nm, l in rest], jobs))
    for (nm, l), arr in zip(rest, casted):
        full[l][nm] = arr

    xs = x[0]
    row = lambda a: a.reshape(1, -1)
    saved = []
    for l in range(depth):
        g1, gm, g2 = row(ffn1_norm[l]), row(mix_norm[l]), row(ffn2_norm[l])
        x1, h1, s1, ab1 = gather_with(l, ("w_in",), 1.0, lambda jobs: _ffn_fwd(
            xs, g1, full[l]["ffn1_w13"], full[l]["ffn1_w2"], jobs))
        hm, p = gather_with(l, ("w_up", "w_out", "sconv_w", "cconv_w", "ffn2_w2"), 0.85, lambda jobs: _mix_in(
            x1, gm, full[l]["w_in"], jobs))
        branch = (pool_w[l], row(pool_scale[l]), full[l]["sconv_w"][:SKP], full[l]["cconv_w"][:CKP], row(cconv_ln_g[l]),
                  row(cconv_ln_b[l]), row(sgu_ln_g[l]), row(sgu_ln_b[l]), sgu_w[l], sgu_b[l].T)
        y, conv_z = gather_with(l, ("ffn2_w13",), 1.0, lambda jobs: _mix_branches_fwd(p, *branch, jobs=jobs))
        w_up_l = full[l]["w_up"].reshape(NBR, BW, D)
        x2, merged, up = gather_with(l + 1, ("ffn1_w2",), 0.8, lambda jobs: _mix_out_fwd(
            x1, y, p, w_up_l, full[l]["w_out"], jobs))
        x3, h2, s2, ab2 = gather_with(l + 1, ("ffn1_w13",), 1.0, lambda jobs: _ffn_fwd(
            x2, g2, full[l]["ffn2_w13"], full[l]["ffn2_w2"], jobs))
        lw = dict(g1=g1, gm=gm, g2=g2, w13a=full[l]["ffn1_w13"], w2a=full[l]["ffn1_w2"], w13b=full[l]["ffn2_w13"],
                  w2b=full[l]["ffn2_w2"], w_in=full[l]["w_in"], w_up=w_up_l, w_out=full[l]["w_out"], branch=branch)
        saved.append(dict(lw=lw, x0=xs, x1=x1, x2=x2, h1=h1, s1=s1, ab1=ab1, hm=hm, p=p, y=y, z=conv_z, merged=merged, up=up, h2=h2,
                          s2=s2, ab2=ab2))
        xs = x3

    (dx, d_final, loss_part), _ = _loss_head(xs, row(final_norm), loss_target[0])
    loss = lax.psum(loss_part[0, 0], ("x", "y", "c"))

    ici_us = dict(ffn1_w13=64, ffn1_w2=32, w_in=93, w_up=23, w_out=12, ffn2_w13=64, ffn2_w2=32)
    parts, pair_sums, reduced, big_updates, pending = {}, {}, {}, {}, []

    def take_jobs(budget_us):
        chosen = []
        for task in list(pending):
            kind, (nm, _) = task
            if kind == "chip":
                if ici_us[nm] > budget_us:
                    continue
                budget_us -= ici_us[nm]
            if kind == "sib" and any(k == "sib" and key[0] == nm for k, key in chosen):
                continue
            chosen.append(task)
            pending.remove(task)
        groups, jobs = [], []
        for kind in ("pair", "sib", "chip"):
            keys = [key for k, key in chosen if k == kind]
            if not keys:
                continue
            cols = [col_of[nm] for nm, _ in keys]
            groups.append((kind, keys))
            if kind == "pair":
                jobs.append(_job_pair([parts[key] for key in keys], cols))
            elif kind == "chip":
                jobs.append(_job_chip([pair_sums[key] for key in keys], cols))
            else:
                jobs.append(_job_sibling([reduced[nm] for nm, _ in keys], cols, [layer for _, layer in keys]))
        return groups, jobs

    def settle(groups, job_outs):
        for (kind, keys), outs in zip(groups, job_outs):
            for key, out in zip(keys, outs):
                nm, layer = key
                if kind == "pair":
                    pair_sums[key] = _pair_sum(parts[key], out, col_of[nm], sc)
                    pending.append(("chip", key))
                elif kind == "chip":
                    assert not any(k == "sib" and other[0] == nm for k, other in pending)
                    reduced[nm] = _chip_sum(pair_sums[key], out, col_of[nm], layer, depth, sc, reduced.get(nm))
                    pending.append(("sib", key))
                else:
                    reduced[nm] = out
                    as3 = lambda a: a.reshape(out.shape)
                    big_updates[nm] = _adamw_layer(as3(w[nm]), out, as3(m[nm]), as3(v[nm]), layer, big_updates.get(nm))

    def run(budget_us, call, carrier=True):
        groups, jobs = take_jobs(budget_us) if carrier else ([], [])
        res, job_outs = call(jobs)
        settle(groups, job_outs)
        return res

    def wgrad_done(key, partial):
        parts[key] = partial
        pending.append(("pair", key))

    small_parts = {nm: [None] * depth for nm in SMALL if nm != "final_norm"}
    for l in reversed(range(depth)):
        sv = saved[l]
        lw = sv["lw"]
        dx, dab, dyh, dg2 = run(100, lambda jobs: _ffn_bwd(
            dx, sv["x2"], lw["g2"], sv["ab2"], lw["w13b"], lw["w2b"], jobs))
        wgrad_done(("ffn2_w13", l), run(58, lambda jobs: _wgrad(sv["h2"], dab, D, 512, "wgrad_w13", jobs), False))
        wgrad_done(("ffn2_w2", l), run(33, lambda jobs: _wgrad(sv["s2"], dyh, 256, D, "wgrad_w2", jobs), False))
        small_parts["ffn2_norm"][l] = dg2

        dy, dp, dup, dxb = run(70, lambda jobs: _mix_out_bwd(dx, sv["up"], sv["p"], lw["w_up"], lw["w_out"], jobs))
        wgrad_done(("w_out", l), run(16, lambda jobs: _wgrad(sv["merged"], dxb, D, 512, "wgrad_w_out", jobs), False))
        wgrad_done(("w_up", l), run(25, lambda jobs: _wgrad_groups(sv["y"], dup, BW, D, "wgrad_w_up", jobs), False))
        (dp, d_pool_w, d_pool_scale, d_sconv, d_cconv, d_clg, d_clb, d_slg, d_slb, d_sgu_w, d_sgu_b) = run(
            130, lambda jobs: _mix_branches_bwd(sv["p"], dy, sv["z"], dp, *lw["branch"], jobs=jobs))
        wgrad_done(("w_in", l), run(80, lambda jobs: _wgrad(sv["hm"], dp, D, 512, "wgrad_w_in", jobs), False))
        dx, dgm = run(91, lambda jobs: _mix_in_bwd(dp, lw["w_in"], sv["x1"], lw["gm"], dx, jobs))
        small_parts["mix_norm"][l] = dgm
        small_parts["pool_w"][l] = d_pool_w
        small_parts["pool_scale"][l] = d_pool_scale
        small_parts["sconv_w"][l] = d_sconv[:SK]
        small_parts["cconv_w"][l] = d_cconv[:CK]
        small_parts["cconv_ln_g"][l] = d_clg
        small_parts["cconv_ln_b"][l] = d_clb
        small_parts["sgu_ln_g"][l] = d_slg
        small_parts["sgu_ln_b"][l] = d_slb
        small_parts["sgu_w"][l] = d_sgu_w
        small_parts["sgu_b"][l] = jnp.sum(d_sgu_b, axis=-1)

        dx, dab, dyh, dg1 = run(100, lambda jobs: _ffn_bwd(
            dx, sv["x0"], lw["g1"], sv["ab1"], lw["w13a"], lw["w2a"], jobs))
        wgrad_done(("ffn1_w2", l), run(33, lambda jobs: _wgrad(sv["s1"], dyh, 256, D, "wgrad_w2", jobs), l == 0))
        wgrad_done(("ffn1_w13", l), run(58, lambda jobs: _wgrad(sv["h1"], dab, D, 512, "wgrad_w13", jobs), l == 0))
        small_parts["ffn1_norm"][l] = dg1
    grad_x = dx[None]

    small_local = {nm: jnp.stack(parts).reshape(depth, *w[nm].shape[1:-1], -1) if nm not in ("sconv_w", "cconv_w")
                   else jnp.stack(parts) for nm, parts in small_parts.items()}
    small_local["final_norm"] = d_final.reshape(-1)
    sizes = [small_local[nm].size for nm in SMALL]
    total = sum(sizes)
    pad_to = NDEV * 8 * GW
    padded = -(-total // pad_to) * pad_to
    packed = jnp.concatenate([small_local[nm].reshape(-1) for nm in SMALL] + [jnp.zeros((padded - total,), F32)])
    groups, jobs = take_jobs(float("inf"))
    summed, job_outs = _all_reduce_small(packed.reshape(-1, GW), jobs)
    settle(groups, job_outs)
    summed = summed.reshape(-1)
    flushes = 0
    while pending:
        groups, jobs = take_jobs(float("inf"))
        settle(groups, _standalone(jobs, "grad_flush_%d" % flushes))
        flushes += 1
    big_grads = {nm: big_updates[nm][3].reshape(w[nm].shape) for nm in BIG}
    small_grads, off = {}, 0
    for nm, size in zip(SMALL, sizes):
        small_grads[nm] = summed[off:off + size].reshape(small_local[nm].shape)
        off += size
    for nm in ("sconv_w", "cconv_w"):
        small_grads[nm] = lax.dynamic_slice_in_dim(small_grads[nm], chip * GW, GW, axis=2)

    grads = {**big_grads, **small_grads}

    delta, new_m, new_v = {}, {}, {}
    for nm in BIG:
        delta[nm], new_m[nm], new_v[nm] = [a.reshape(w[nm].shape) for a in big_updates[nm][:3]]
    s_sizes = [w[nm].size for nm in SMALL]
    s_total = sum(s_sizes)
    s_padded = -(-s_total // (8 * GW)) * (8 * GW)

    def pack(tree):
        return jnp.concatenate([tree[nm].reshape(-1) for nm in SMALL] + [jnp.ones((s_padded - s_total,), F32)]).reshape(-1, GW)

    packed_out = _adamw(pack(w), pack(grads), pack(m), pack(v))
    off = 0
    for nm, size in zip(SMALL, s_sizes):
        for tree, arr in zip((delta, new_m, new_v), packed_out):
            tree[nm] = arr.reshape(-1)[off:off + size].reshape(w[nm].shape)
        off += size

    return (loss, grad_x, *[grads[nm] for nm in WEIGHTS], *[delta[nm] for nm in WEIGHTS],
            *[new_m[nm] for nm in WEIGHTS], *[new_v[nm] for nm in WEIGHTS])
```

```python
import jax
import jax.numpy as jnp
from jax import lax
from jax.experimental import pallas as pl
from jax.experimental.pallas import tpu as pltpu

D = 1024
FF = 2816
BW = 512
NBR = 4
MIXC = 4096
INC = 8192
GW = 128
CHUNK = 64
SK = 3
CK = 31
SKP = 8
CKP = 32
HALO = 32
TOKEN_TILE = 256
BRANCH_TILE = 512
EPS = 1e-6
NCHIP = 4
NDEV = 8

ADAM_LR = 0.001
ADAM_B1 = 0.9
ADAM_B2 = 0.999
ADAM_EPS = 1e-08
ADAM_WD = 0.01
ADAM_STEP = 10

VMEM_LIMIT = 56 * 1024 * 1024

BF = jnp.bfloat16
F32 = jnp.float32
MESH = pl.DeviceIdType.MESH
NT = (((1,), (1,)), ((), ()))
TN = (((0,), (0,)), ((), ()))


def _params(n_axes):
    return pltpu.CompilerParams(dimension_semantics=("arbitrary",) * n_axes, vmem_limit_bytes=VMEM_LIMIT)


class _Job:
    def __init__(self, args, n_inplace, fresh, nsem, phases):
        self.args, self.n_inplace, self.fresh, self.nsem, self.phases = list(args), n_inplace, list(fresh), nsem, phases


def _pc(jobs, body, name, in_specs, out_specs, out_shape, grid=(), scratch_shapes=(), input_output_aliases=None,
        compiler_params=None, grid_spec_scalars=None):
    single = not isinstance(out_shape, (list, tuple))
    core_out_specs = [out_specs] if single else list(out_specs)
    core_out_shape = [out_shape] if single else list(out_shape)
    n_in, n_out, n_scr = len(in_specs), len(core_out_specs), len(scratch_shapes)
    n_pre = 0 if grid_spec_scalars is None else 1
    all_in, all_out, all_shape = list(in_specs), list(core_out_specs), list(core_out_shape)
    all_scr, aliases, extra_args, layout = list(scratch_shapes), dict(input_output_aliases or {}), [], []
    for job in jobs:
        n_job_out = job.n_inplace + len(job.fresh)
        layout.append((len(all_in), len(job.args), len(all_out), n_job_out, len(all_scr)))
        for a in range(job.n_inplace):
            aliases[n_pre + len(all_in) + a] = len(all_out) + a
        all_in += [ANY] * len(job.args)
        extra_args += job.args
        all_out += [ANY] * n_job_out
        all_shape += [jax.ShapeDtypeStruct(a.shape, a.dtype) for a in job.args[:job.n_inplace]] + job.fresh
        all_scr += [pltpu.SemaphoreType.DMA((job.nsem,)), pltpu.SemaphoreType.DMA((job.nsem,))]
    steps = 1
    for extent in grid:
        steps *= extent
    events = []
    for (i0, na, o0, no, s0), job in zip(layout, jobs):
        for frac, fn in job.phases:
            events.append((min(int(frac * steps), steps - 1), frac >= 1.0, len(events), fn, (i0, na, o0, no, s0)))
    events.sort(key=lambda e: e[:3])

    def wrapped(*refs):
        pre, refs = refs[:n_pre], refs[n_pre:]
        ins, outs, scr = refs[:len(all_in)], refs[len(all_in):len(all_in) + len(all_out)], refs[len(all_in) + len(all_out):]
        step = 0
        for axis, extent in enumerate(grid):
            step = step * extent + pl.program_id(axis)

        def emit(event):
            at, _, _, fn, (i0, na, o0, no, s0) = event
            run = lambda: fn(ins[i0:i0 + na], outs[o0:o0 + no], scr[s0], scr[s0 + 1])
            if steps == 1:
                run()
            else:
                pl.when(step == at)(run)

        for event in events:
            if not event[1]:
                emit(event)
        body(*pre, *ins[:n_in], *outs[:n_out], *scr[:n_scr])
        for event in events:
            if event[1]:
                emit(event)

    kwargs = dict(name=name, out_shape=all_shape, input_output_aliases=aliases)
    if compiler_params is not None:
        kwargs["compiler_params"] = compiler_params
    if grid_spec_scalars is not None:
        kwargs["grid_spec"] = pltpu.PrefetchScalarGridSpec(
            num_scalar_prefetch=1, grid=grid, in_specs=all_in, out_specs=all_out, scratch_shapes=all_scr)
    else:
        kwargs.update(in_specs=all_in, out_specs=all_out, scratch_shapes=all_scr)
        if grid:
            kwargs["grid"] = grid
    call = pl.pallas_call(wrapped, **kwargs)

    def run_call(*args):
        pre_args = [] if grid_spec_scalars is None else [grid_spec_scalars]
        res = list(call(*pre_args, *args, *extra_args))
        core = res[0] if single else res[:n_out]
        job_outs = [res[o0:o0 + no] for (_, _, o0, no, _) in layout]
        return core, job_outs

    return run_call


def _full(shape):
    nd = len(shape)
    return pl.BlockSpec(shape, lambda *_: (0,) * nd, pipeline_mode=pl.Buffered(1))


def _rows(ts, width, col=0):
    return pl.BlockSpec((ts, width), lambda i: (i, col))


def _sig(v):
    return jax.nn.sigmoid(v)


def _rms_stats(x):
    r = lax.rsqrt(jnp.mean(x * x, axis=-1, keepdims=True) + EPS)
    return r, x * r


def _rms_bwd(x, g, dh):
    r, xh = _rms_stats(x)
    dg = jnp.sum(dh * xh, axis=0, keepdims=True)
    dxh = dh * g
    dx = r * (dxh - xh * jnp.mean(dxh * xh, axis=-1, keepdims=True))
    return dx, dg


def _accumulate(ref, val, first):
    @pl.when(first)
    def _():
        ref[...] = val

    @pl.when(jnp.logical_not(first))
    def _():
        ref[...] += val


def _ffn_fwd(x, g, w13, w2, jobs=()):
    s_len = x.shape[0]
    ts = min(TOKEN_TILE, s_len)

    def body(x_ref, g_ref, w13_ref, w2_ref, xo_ref, h_ref, s_ref, dsab_ref):
        xv = x_ref[...]
        r, xh = _rms_stats(xv)
        h = (xh * g_ref[...]).astype(BF)
        h_ref[...] = h
        ab = jnp.dot(h, w13_ref[...], preferred_element_type=F32)
        a = ab[:, :FF]
        b = ab[:, FF:]
        sg = _sig(a)
        sil = a * sg
        dsab_ref[:, :FF] = (b * (sg * (1.0 + a * (1.0 - sg)))).astype(BF)
        dsab_ref[:, FF:] = sil.astype(BF)
        s = (sil * b).astype(BF)
        s_ref[...] = s
        xo_ref[...] = xv + 0.5 * jnp.dot(s, w2_ref[...], preferred_element_type=F32)

    return _pc(
        jobs, body, name="ffn_fwd", grid=(s_len // ts,),
        in_specs=[_rows(ts, D), _full((1, D)), _full((D, 2 * FF)), _full((FF, D))],
        out_specs=[_rows(ts, D), _rows(ts, D), _rows(ts, FF), _rows(ts, 2 * FF)],
        out_shape=[jax.ShapeDtypeStruct((s_len, D), F32), jax.ShapeDtypeStruct((s_len, D), BF),
                   jax.ShapeDtypeStruct((s_len, FF), BF), jax.ShapeDtypeStruct((s_len, 2 * FF), BF)],
        compiler_params=_params(1),
    )(x, g, w13, w2)


def _ffn_bwd(dxo, x, g, dsab, w13, w2, jobs=()):
    s_len = x.shape[0]
    ts = min(TOKEN_TILE, s_len)

    def body(dxo_ref, x_ref, g_ref, dsab_ref, w13_ref, w2_ref, dxi_ref, dab_ref, dy_ref, dg_ref):
        i = pl.program_id(0)
        dxo_v = dxo_ref[...]
        dy = (0.5 * dxo_v).astype(BF)
        dy_ref[...] = dy
        ds = lax.dot_general(dy, w2_ref[...], NT, preferred_element_type=F32)
        dab_ref[:, :FF] = (ds * dsab_ref[:, :FF].astype(F32)).astype(BF)
        dab_ref[:, FF:] = (ds * dsab_ref[:, FF:].astype(F32)).astype(BF)
        dh = lax.dot_general(dab_ref[...], w13_ref[...], NT, preferred_element_type=F32)
        dx, dg = _rms_bwd(x_ref[...], g_ref[...], dh)
        dxi_ref[...] = dxo_v + dx
        _accumulate(dg_ref, dg, i == 0)

    return _pc(
        jobs, body, name="ffn_bwd", grid=(s_len // ts,),
        in_specs=[_rows(ts, D), _rows(ts, D), _full((1, D)), _rows(ts, 2 * FF), _full((D, 2 * FF)), _full((FF, D))],
        out_specs=[_rows(ts, D), _rows(ts, 2 * FF), _rows(ts, D), pl.BlockSpec((1, D), lambda i: (0, 0))],
        out_shape=[jax.ShapeDtypeStruct((s_len, D), F32), jax.ShapeDtypeStruct((s_len, 2 * FF), BF),
                   jax.ShapeDtypeStruct((s_len, D), BF), jax.ShapeDtypeStruct((1, D), F32)],
        compiler_params=_params(1),
    )(dxo, x, g, dsab, w13, w2)


def _wgrad(a, b, tk, tn, name, jobs=()):
    s_len, k = a.shape
    n = b.shape[1]

    def body(a_ref, b_ref, o_ref):
        o_ref[...] = lax.dot_general(a_ref[...], b_ref[...], TN, preferred_element_type=F32).astype(BF)

    return _pc(
        jobs, body, name=name, grid=(k // tk, n // tn),
        in_specs=[pl.BlockSpec((s_len, tk), lambda i, j: (0, i)), pl.BlockSpec((s_len, tn), lambda i, j: (0, j))],
        out_specs=pl.BlockSpec((tk, tn), lambda i, j: (i, j)),
        out_shape=jax.ShapeDtypeStruct((k, n), BF),
        compiler_params=_params(2),
    )(a, b)


def _wgrad_groups(a, b, ka, nb, name, jobs=()):
    s_len = a.shape[0]
    groups = a.shape[1] // ka

    def body(a_ref, b_ref, o_ref):
        o_ref[...] = lax.dot_general(a_ref[...], b_ref[...], TN, preferred_element_type=F32).astype(BF)

    return _pc(
        jobs, body, name=name, grid=(groups,),
        in_specs=[pl.BlockSpec((s_len, ka), lambda gi: (0, gi)), pl.BlockSpec((s_len, nb), lambda gi: (0, gi))],
        out_specs=pl.BlockSpec((ka, nb), lambda gi: (gi, 0)),
        out_shape=jax.ShapeDtypeStruct((groups * ka, nb), BF),
        compiler_params=_params(1),
    )(a, b)


def _mix_in(x, g, w_in, jobs=()):
    s_len = x.shape[0]
    ts = min(TOKEN_TILE, s_len)

    def body(x_ref, g_ref, w_ref, h_ref, p_ref):
        _, xh = _rms_stats(x_ref[...])
        h = (xh * g_ref[...]).astype(BF)
        h_ref[...] = h
        p_ref[...] = jnp.dot(h, w_ref[...], preferred_element_type=F32).astype(BF)

    return _pc(
        jobs, body, name="mix_in", grid=(s_len // ts,),
        in_specs=[_rows(ts, D), _full((1, D)), _full((D, INC))],
        out_specs=[_rows(ts, D), _rows(ts, INC)],
        out_shape=[jax.ShapeDtypeStruct((s_len, D), BF), jax.ShapeDtypeStruct((s_len, INC), BF)],
        compiler_params=_params(1),
    )(x, g, w_in)


def _shift(e, j):
    n = e.shape[0]
    j = j % n
    return e if j == 0 else pltpu.roll(e, j, 0)


def _ln_stats(z):
    mu = jnp.mean(z, axis=-1, keepdims=True)
    zc = z - mu
    rs = lax.rsqrt(jnp.mean(zc * zc, axis=-1, keepdims=True) + EPS)
    return rs, zc * rs


def _ln_bwd(rs, zn, dzn):
    return rs * (dzn - jnp.mean(dzn, axis=-1, keepdims=True) - zn * jnp.mean(dzn * zn, axis=-1, keepdims=True))


_GELU_C0 = 0.7978845608028654
_GELU_C1 = 0.044715


def _gelu(p):
    th = jnp.tanh(_GELU_C0 * (p + _GELU_C1 * p * p * p))
    return 0.5 * p * (1.0 + th), th


def _gelu_grad(p, th):
    return 0.5 * (1.0 + th) + 0.5 * p * (1.0 - th * th) * (_GELU_C0 * (1.0 + 3.0 * _GELU_C1 * p * p))


def _pool_diff(a, t, sign):
    outs = []
    for gi in range(NBR):
        win = 2 ** (gi + 1)
        ag = a[:, gi * GW:(gi + 1) * GW]
        cnt = jnp.clip(t + 1, 1, win).astype(F32)
        ws = ag if sign > 0 else ag / cnt
        for s in range(gi + 1):
            ws = ws + _shift(ws, sign * (2 ** s))
        outs.append((ws / cnt if sign > 0 else ws) - ag)
    return outs


def _sgu_mask():
    row = lax.broadcasted_iota(jnp.int32, (GW, GW), 0)
    col = lax.broadcasted_iota(jnp.int32, (GW, GW), 1)
    return (col // CHUNK) <= (row // CHUNK)


def _assemble(pe_ref, prev_ref, cur_ref, next_ref, i, last, ts):
    pe_ref[0:HALO, :] = jnp.where(i > 0, prev_ref[...], jnp.zeros_like(prev_ref))
    pe_ref[HALO:HALO + ts, :] = cur_ref[...]
    if next_ref is not None:
        pe_ref[HALO + ts:, :] = jnp.where(i < last, next_ref[...], jnp.zeros_like(next_ref))


def _halo_specs(ts, width, s_len, with_next):
    per = ts // HALO
    specs = [pl.BlockSpec((HALO, width), lambda i: (jnp.maximum(i * per - 1, 0), 0)),
             pl.BlockSpec((ts, width), lambda i: (i, 0))]
    if with_next:
        specs.append(pl.BlockSpec((HALO, width), lambda i: (jnp.minimum((i + 1) * per, s_len // HALO - 1), 0)))
    return specs


def _branch_weights_specs():
    return [_full((NBR, GW, GW)), _full((1, BW)), _full((SKP, BW)), _full((CKP, BW)), _full((1, BW)), _full((1, BW)),
            _full((1, BW)), _full((1, BW)), _full((NBR, GW, GW)), _full((GW, NBR))]


def _mix_branches_fwd(p, pool_w, pool_scale, sconv_w, cconv_w, cln_g, cln_b, sln_g, sln_b, sgu_w, sgu_bt, jobs=()):
    s_len = p.shape[0]
    ts = min(BRANCH_TILE, s_len)
    ext = HALO + ts

    def body(pp_ref, pc_ref, pw_ref, ps_ref, sw_ref, cw_ref, clg_ref, clb_ref, slg_ref, slb_ref, gw_ref, gb_ref,
             y_ref, z_ref, pe_ref):
        i = pl.program_id(0)
        _assemble(pe_ref, pp_ref, pc_ref, None, i, 0, ts)
        t = i * ts - HALO + lax.broadcasted_iota(jnp.int32, (ext, 1), 0)

        dgs = _pool_diff(pe_ref[:, 0:BW].astype(F32), t, 1)
        for gi in range(NBR):
            e = jnp.dot(dgs[gi][HALO:].astype(BF), pw_ref[gi].astype(BF), preferred_element_type=F32)
            y_ref[:, gi * GW:(gi + 1) * GW] = (e * ps_ref[:, gi * GW:(gi + 1) * GW]).astype(BF)

        xin = pe_ref[:, BW:2 * BW].astype(F32)
        cg = pe_ref[:, 3 * BW:4 * BW].astype(F32)
        q = cg * xin
        cv = sw_ref[2:3, :] * q + sw_ref[1:2, :] * _shift(q, 1) + sw_ref[0:1, :] * _shift(q, 2)
        y_ref[:, BW:2 * BW] = (pe_ref[HALO:, 2 * BW:3 * BW].astype(F32) * cv[HALO:]).astype(BF)

        yg = pe_ref[:, 4 * BW:5 * BW].astype(F32) * _sig(pe_ref[:, 5 * BW:6 * BW].astype(F32))
        z = cw_ref[CK - 1:CK, :] * yg
        for j in range(1, CK):
            z = z + cw_ref[CK - 1 - j:CK - j, :] * _shift(yg, j)
        z_ref[...] = z[HALO:].astype(BF)
        _, zn = _ln_stats(z[HALO:])
        nn = zn * clg_ref[...] + clb_ref[...]
        y_ref[:, 2 * BW:3 * BW] = (nn * _sig(nn)).astype(BF)

        u, _ = _gelu(pc_ref[:, 6 * BW:7 * BW].astype(F32))
        v, _ = _gelu(pc_ref[:, 7 * BW:8 * BW].astype(F32))
        _, vn = _ln_stats(v)
        vn = (vn * slg_ref[...] + slb_ref[...]).astype(BF)
        mask = _sgu_mask()
        for hd in range(NBR):
            wm = jnp.where(mask, gw_ref[hd], 0.0).astype(BF)
            for blk in range(ts // GW):
                rows = slice(blk * GW, (blk + 1) * GW)
                cols = slice(hd * GW, (hd + 1) * GW)
                zz = jnp.dot(wm, vn[rows, cols], preferred_element_type=F32) + gb_ref[:, hd:hd + 1]
                y_ref[rows, 3 * BW + hd * GW:3 * BW + (hd + 1) * GW] = (u[rows, cols] * zz).astype(BF)

    return _pc(
        jobs, body, name="mix_branches_fwd", grid=(s_len // ts,),
        in_specs=_halo_specs(ts, MIXC, s_len, False) + _branch_weights_specs(),
        out_specs=[_rows(ts, NBR * BW), _rows(ts, BW)],
        out_shape=[jax.ShapeDtypeStruct((s_len, NBR * BW), BF), jax.ShapeDtypeStruct((s_len, BW), BF)],
        scratch_shapes=[pltpu.VMEM((ext, MIXC), BF)],
        compiler_params=_params(1),
    )(p, p, pool_w, pool_scale, sconv_w, cconv_w, cln_g, cln_b, sln_g, sln_b, sgu_w, sgu_bt)


def _mix_branches_bwd(p, dy, z, dp, pool_w, pool_scale, sconv_w, cconv_w, cln_g, cln_b, sln_g, sln_b, sgu_w, sgu_bt,
                      jobs=()):
    s_len = p.shape[0]
    ts = min(BRANCH_TILE, s_len)
    ext = ts + 2 * HALO
    last = s_len // ts - 1
    tile = slice(HALO, HALO + ts)
    small_shapes = [(NBR, GW, GW), (1, BW), (SKP, BW), (CKP, BW), (1, BW), (1, BW), (1, BW), (1, BW), (NBR, GW, GW),
                    (NBR, GW, GW)]

    def body(pp_ref, pc_ref, pn_ref, dyp_ref, dyc_ref, dyn_ref, zc_ref, zn_ref, dpin_ref,
             pw_ref, ps_ref, sw_ref, cw_ref, clg_ref, clb_ref, slg_ref, slb_ref, gw_ref, gb_ref,
             dp_ref, dpw_ref, dps_ref, dsw_ref, dcw_ref, dclg_ref, dclb_ref, dslg_ref, dslb_ref, dgw_ref, dgb_ref,
             pe_ref, de_ref):
        del dyp_ref, dpin_ref
        i = pl.program_id(0)
        first = i == 0
        _assemble(pe_ref, pp_ref, pc_ref, pn_ref, i, last, ts)
        de_ref[0:HALO, :] = jnp.zeros((HALO, NBR * BW), BF)
        de_ref[HALO:HALO + ts, :] = dyc_ref[...]
        de_ref[HALO + ts:, :] = jnp.where(i < last, dyn_ref[...], jnp.zeros_like(dyn_ref))
        t = i * ts - HALO + lax.broadcasted_iota(jnp.int32, (ext, 1), 0)

        @pl.when(first)
        def _():
            dsw_ref[...] = jnp.zeros((SKP, BW), F32)
            dcw_ref[...] = jnp.zeros((CKP, BW), F32)

        dgs = _pool_diff(pe_ref[:, 0:BW].astype(F32), t, 1)
        dya = de_ref[:, 0:BW].astype(F32)
        dds = []
        for gi in range(NBR):
            cols = slice(gi * GW, (gi + 1) * GW)
            pw = pw_ref[gi].astype(BF)
            d_t = dgs[gi][tile].astype(BF)
            e = jnp.dot(d_t, pw, preferred_element_type=F32)
            _accumulate(dps_ref.at[:, cols], jnp.sum(dya[tile, cols] * e, axis=0, keepdims=True), first)
            de_g = (dya[:, cols] * ps_ref[:, cols]).astype(BF)
            _accumulate(dpw_ref.at[gi], lax.dot_general(d_t, de_g[tile], TN, preferred_element_type=F32), first)
            dds.append(lax.dot_general(de_g, pw, NT, preferred_element_type=F32))
        das = _pool_diff(jnp.concatenate(dds, axis=1), t, -1)
        for gi in range(NBR):
            dp_ref[:, gi * GW:(gi + 1) * GW] = das[gi][tile].astype(BF)

        xin = pe_ref[:, BW:2 * BW].astype(F32)
        bg = pe_ref[:, 2 * BW:3 * BW].astype(F32)
        cg = pe_ref[:, 3 * BW:4 * BW].astype(F32)
        q = cg * xin
        qs = [q, _shift(q, 1), _shift(q, 2)]
        cv = sw_ref[2:3, :] * qs[0] + sw_ref[1:2, :] * qs[1] + sw_ref[0:1, :] * qs[2]
        dyb = de_ref[:, BW:2 * BW].astype(F32)
        dcv = dyb * bg
        for j in range(SK):
            dsw_ref[SK - 1 - j:SK - j, :] += jnp.sum(dcv[tile] * qs[j][tile], axis=0, keepdims=True)
        dq = sw_ref[2:3, :] * dcv + sw_ref[1:2, :] * _shift(dcv, -1) + sw_ref[0:1, :] * _shift(dcv, -2)
        dp_ref[:, BW:2 * BW] = (dq * cg)[tile].astype(BF)
        dp_ref[:, 2 * BW:3 * BW] = (dyb * cv)[tile].astype(BF)
        dp_ref[:, 3 * BW:4 * BW] = (dq * xin)[tile].astype(BF)

        ca = pc_ref[:, 4 * BW:5 * BW].astype(F32)
        sb = _sig(pc_ref[:, 5 * BW:6 * BW].astype(F32))
        yg_t = ca * sb
        z = jnp.concatenate([zc_ref[...].astype(F32), zn_ref[...].astype(F32)], axis=0)
        rs, zn = _ln_stats(z)
        nn = zn * clg_ref[...] + clb_ref[...]
        sn = _sig(nn)
        dn = de_ref[HALO:, 2 * BW:3 * BW].astype(F32) * (sn * (1.0 + nn * (1.0 - sn)))
        _accumulate(dclg_ref, jnp.sum((dn * zn)[:ts], axis=0, keepdims=True), first)
        _accumulate(dclb_ref, jnp.sum(dn[:ts], axis=0, keepdims=True), first)
        dz = _ln_bwd(rs, zn, dn * clg_ref[...])
        dyg = cw_ref[CK - 1:CK, :] * dz[:ts]
        dcw_ref[CK - 1:CK, :] += jnp.sum(dz[:ts] * yg_t, axis=0, keepdims=True)
        for j in range(1, CK):
            dz_ahead = _shift(dz, -j)[:ts]
            dyg = dyg + cw_ref[CK - 1 - j:CK - j, :] * dz_ahead
            dcw_ref[CK - 1 - j:CK - j, :] += jnp.sum(dz_ahead * yg_t, axis=0, keepdims=True)
        dp_ref[:, 4 * BW:5 * BW] = (dyg * sb).astype(BF)
        dp_ref[:, 5 * BW:6 * BW] = (dyg * ca * sb * (1.0 - sb)).astype(BF)

        pu = pc_ref[:, 6 * BW:7 * BW].astype(F32)
        pv = pc_ref[:, 7 * BW:8 * BW].astype(F32)
        u, thu = _gelu(pu)
        v, thv = _gelu(pv)
        vrs, vn0 = _ln_stats(v)
        vn = (vn0 * slg_ref[...] + slb_ref[...]).astype(BF)
        dyd = dyc_ref[:, 3 * BW:4 * BW].astype(F32)
        dzz = dyd * u
        dzb = dzz.astype(BF)
        mask = _sgu_mask()
        dvn_cols = []
        for hd in range(NBR):
            cols = slice(hd * GW, (hd + 1) * GW)
            wm = jnp.where(mask, gw_ref[hd], 0.0).astype(BF)
            dwm = jnp.zeros((GW, GW), F32)
            dbs = jnp.zeros((GW, GW), F32)
            dvn_rows = []
            for blk in range(ts // GW):
                rows = slice(blk * GW, (blk + 1) * GW)
                zz = jnp.dot(wm, vn[rows, cols], preferred_element_type=F32) + gb_ref[:, hd:hd + 1]
                dp_ref[rows, 6 * BW + hd * GW:6 * BW + (hd + 1) * GW] = (
                    dyd[rows, cols] * zz * _gelu_grad(pu[rows, cols], thu[rows, cols])).astype(BF)
                dwm = dwm + lax.dot_general(dzb[rows, cols], vn[rows, cols], NT, preferred_element_type=F32)
                dbs = dbs + dzz[rows, cols]
                dvn_rows.append(lax.dot_general(wm, dzb[rows, cols], TN, preferred_element_type=F32))
            _accumulate(dgw_ref.at[hd], jnp.where(mask, dwm, 0.0), first)
            _accumulate(dgb_ref.at[hd], dbs, first)
            dvn_cols.append(jnp.concatenate(dvn_rows, axis=0))
        dvn = jnp.concatenate(dvn_cols, axis=1)
        _accumulate(dslg_ref, jnp.sum(dvn * vn0, axis=0, keepdims=True), first)
        _accumulate(dslb_ref, jnp.sum(dvn, axis=0, keepdims=True), first)
        dv = _ln_bwd(vrs, vn0, dvn * slg_ref[...])
        dp_ref[:, 7 * BW:8 * BW] = (dv * _gelu_grad(pv, thv)).astype(BF)

    const = lambda shp: pl.BlockSpec(shp, lambda i: (0,) * len(shp))
    return _pc(
        jobs, body, name="mix_branches_bwd", grid=(s_len // ts,),
        in_specs=(_halo_specs(ts, MIXC, s_len, True) + _halo_specs(ts, NBR * BW, s_len, True)
                  + _halo_specs(ts, BW, s_len, True)[1:] + [pl.BlockSpec(memory_space=pl.ANY)] + _branch_weights_specs()),
        out_specs=[pl.BlockSpec((ts, MIXC), lambda i: (i, 0))] + [const(s) for s in small_shapes],
        out_shape=[jax.ShapeDtypeStruct((s_len, INC), BF)] + [jax.ShapeDtypeStruct(s, F32) for s in small_shapes],
        scratch_shapes=[pltpu.VMEM((ext, MIXC), BF), pltpu.VMEM((ext, NBR * BW), BF)],
        input_output_aliases={8: 0},
        compiler_params=_params(1),
    )(p, p, p, dy, dy, dy, z, z, dp, pool_w, pool_scale, sconv_w, cconv_w, cln_g, cln_b, sln_g, sln_b, sgu_w, sgu_bt)


def _mix_out_fwd(x, y, p, w_up, w_out, jobs=()):
    s_len = x.shape[0]
    ts = min(TOKEN_TILE if jobs else 2 * TOKEN_TILE, s_len)

    def body(x_ref, y_ref, pg_ref, wu_ref, wo_ref, xo_ref, m_ref, up_ref):
        m = jnp.zeros((ts, D), F32)
        for gi in range(NBR):
            up = jnp.dot(y_ref[:, gi * BW:(gi + 1) * BW], wu_ref[gi], preferred_element_type=F32)
            up_ref[:, gi * D:(gi + 1) * D] = up.astype(BF)
            m = m + _sig(pg_ref[:, gi * D:(gi + 1) * D].astype(F32)) * up
        mb = m.astype(BF)
        m_ref[...] = mb
        xo_ref[...] = x_ref[...] + jnp.dot(mb, wo_ref[...], preferred_element_type=F32)

    return _pc(
        jobs, body, name="mix_out_fwd", grid=(s_len // ts,),
        in_specs=[_rows(ts, D), _rows(ts, NBR * BW), _rows(ts, NBR * D, 1), _full((NBR, BW, D)), _full((D, D))],
        out_specs=[_rows(ts, D), _rows(ts, D), _rows(ts, NBR * D)],
        out_shape=[jax.ShapeDtypeStruct((s_len, D), F32), jax.ShapeDtypeStruct((s_len, D), BF),
                   jax.ShapeDtypeStruct((s_len, NBR * D), BF)],
        compiler_params=_params(1),
    )(x, y, p, w_up, w_out)


def _mix_out_bwd(dxo, up, p, w_up, w_out, jobs=()):
    s_len = dxo.shape[0]
    ts = min(TOKEN_TILE, s_len)

    def body(dxo_ref, up_ref, pg_ref, wu_ref, wo_ref, dy_ref, dp_ref, dup_ref, dxb_ref):
        dxb = dxo_ref[...].astype(BF)
        dxb_ref[...] = dxb
        dm = lax.dot_general(dxb, wo_ref[...], NT, preferred_element_type=F32)
        for gi in range(NBR):
            cols = slice(gi * D, (gi + 1) * D)
            gate = _sig(pg_ref[:, cols].astype(F32))
            dp_ref[:, cols] = (dm * up_ref[:, cols].astype(F32) * gate * (1.0 - gate)).astype(BF)
            dup = (dm * gate).astype(BF)
            dup_ref[:, cols] = dup
            dy_ref[:, gi * BW:(gi + 1) * BW] = lax.dot_general(
                dup, wu_ref[gi], NT, preferred_element_type=F32).astype(BF)

    return _pc(
        jobs, body, name="mix_out_bwd", grid=(s_len // ts,),
        in_specs=[_rows(ts, D), _rows(ts, NBR * D), _rows(ts, NBR * D, 1), _full((NBR, BW, D)), _full((D, D))],
        out_specs=[_rows(ts, NBR * BW), _rows(ts, NBR * D, 1), _rows(ts, NBR * D), _rows(ts, D)],
        out_shape=[jax.ShapeDtypeStruct((s_len, NBR * BW), BF), jax.ShapeDtypeStruct((s_len, INC), BF),
                   jax.ShapeDtypeStruct((s_len, NBR * D), BF), jax.ShapeDtypeStruct((s_len, D), BF)],
        compiler_params=_params(1),
    )(dxo, up, p, w_up, w_out)


def _mix_in_bwd(dp, w_in, x, g, dxo, jobs=()):
    s_len = x.shape[0]
    ts = min(TOKEN_TILE, s_len)

    def body(dp_ref, w_ref, x_ref, g_ref, dxo_ref, dxi_ref, dg_ref):
        i = pl.program_id(0)
        dh = lax.dot_general(dp_ref[...], w_ref[...], NT, preferred_element_type=F32)
        dx, dg = _rms_bwd(x_ref[...], g_ref[...], dh)
        dxi_ref[...] = dxo_ref[...] + dx
        _accumulate(dg_ref, dg, i == 0)

    return _pc(
        jobs, body, name="mix_in_bwd", grid=(s_len // ts,),
        in_specs=[_rows(ts, INC), _full((D, INC)), _rows(ts, D), _full((1, D)), _rows(ts, D)],
        out_specs=[_rows(ts, D), pl.BlockSpec((1, D), lambda i: (0, 0))],
        out_shape=[jax.ShapeDtypeStruct((s_len, D), F32), jax.ShapeDtypeStruct((1, D), F32)],
        compiler_params=_params(1),
    )(dp, w_in, x, g, dxo)


def _loss_head(x, g, target, jobs=()):
    s_len = x.shape[0]
    ts = min(512, s_len)

    def body(x_ref, g_ref, t_ref, dx_ref, dg_ref, loss_ref):
        i = pl.program_id(0)
        xv = x_ref[...]
        _, xh = _rms_stats(xv)
        err = xh * g_ref[...] - t_ref[...]
        part = 0.5 * jnp.sum(jnp.mean(err * err, axis=-1, keepdims=True), axis=0, keepdims=True)
        dx, dg = _rms_bwd(xv, g_ref[...], err * (1.0 / D))
        dx_ref[...] = dx
        _accumulate(dg_ref, dg, i == 0)
        _accumulate(loss_ref, jnp.broadcast_to(part, (1, GW)), i == 0)

    return _pc(
        jobs, body, name="loss_head", grid=(s_len // ts,),
        in_specs=[_rows(ts, D), _full((1, D)), _rows(ts, D)],
        out_specs=[_rows(ts, D), pl.BlockSpec((1, D), lambda i: (0, 0)), pl.BlockSpec((1, GW), lambda i: (0, 0))],
        out_shape=[jax.ShapeDtypeStruct((s_len, D), F32), jax.ShapeDtypeStruct((1, D), F32),
                   jax.ShapeDtypeStruct((1, GW), F32)],
        compiler_params=_params(1),
    )(x, g, target)


SUM_TILE = 1 << 20
ADAM_TILE = 1 << 19
CAST_TILE = 1 << 19


def _row_tile(rows, cols, budget=1 << 18):
    tr = rows
    while tr * cols > budget and tr % 16 == 0:
        tr //= 2
    return tr


def _elementwise(fn, name, ins, out_dtypes):
    rows, cols = ins[0].shape
    tr = _row_tile(rows, cols)
    n_in = len(ins)

    def body(*refs):
        res = fn(*[r[...] for r in refs[:n_in]])
        for o_ref, val in zip(refs[n_in:], res):
            o_ref[...] = val.astype(o_ref.dtype)

    outs = pl.pallas_call(
        body, name=name, grid=(rows // tr,),
        in_specs=[_rows(tr, cols)] * n_in, out_specs=[_rows(tr, cols)] * len(out_dtypes),
        out_shape=[jax.ShapeDtypeStruct((rows, cols), dt) for dt in out_dtypes],
        compiler_params=_params(1),
    )(*ins)
    return outs


def _tiled(fn, name, grid, in_specs, out_specs, out_shape, args, scalars=None, alias=None):
    alias = alias or {}
    n_in = len(in_specs) - len(alias)
    n_pre = 0 if scalars is None else 1

    def body(*refs):
        refs = refs[n_pre:]
        res = fn(*[r[...] for r in refs[:n_in]])
        for o_ref, val in zip(refs[len(in_specs):], res):
            o_ref[...] = val.astype(o_ref.dtype)

    aliases = {n_pre + pos: out for pos, out in alias.items()}
    if scalars is None:
        return pl.pallas_call(body, name=name, grid=grid, in_specs=in_specs, out_specs=out_specs, out_shape=out_shape,
                              input_output_aliases=aliases, compiler_params=_params(len(grid)))(*args)
    spec = pltpu.PrefetchScalarGridSpec(num_scalar_prefetch=1, grid=grid, in_specs=in_specs, out_specs=out_specs)
    return pl.pallas_call(body, name=name, grid_spec=spec, out_shape=out_shape, input_output_aliases=aliases,
                          compiler_params=_params(len(grid)))(scalars, *args)


def _cast_into(shard, layer, col, dtype, sc):
    ks, ns = shard.shape[1:]
    tr = _row_tile(ks, ns, SUM_TILE)
    full = (ks, ns * NCHIP) if col else (ks * NCHIP, ns)
    out_idx = (lambda i, s: (i, s[1])) if col else (lambda i, s: (s[1] * (ks // tr) + i, 0))
    return _tiled(lambda v: (v,), "cast_into", (ks // tr,), [pl.BlockSpec((None, tr, ns), lambda i, s: (layer, i, 0))],
                  [pl.BlockSpec((tr, ns), out_idx)], [jax.ShapeDtypeStruct(full, dtype)], [shard], sc)[0]


def _cast_rest(items, jobs=()):
    n = len(items)
    tiles, slot_sets, counts = [], {}, {}
    for k, (shard, _, _, dt) in enumerate(items):
        ks, ns = shard.shape[1:]
        tr = _row_tile(ks, ns, CAST_TILE)
        which = slot_sets.setdefault((tr, ns, shard.dtype, dt), len(slot_sets))
        for t in range(ks // tr):
            tiles.append((k, t, tr, which, counts.get(which, 0) % 2))
            counts[which] = counts.get(which, 0) + 1
    n_sets = len(slot_sets)

    def body(*refs):
        ins, outs = refs[:n], refs[n:2 * n]
        bufs = refs[2 * n:2 * n + 2 * n_sets]
        in_sems, out_sems = refs[2 * n + 2 * n_sets:]
        chip = 2 * lax.axis_index("x") + lax.axis_index("y")
        fetch, store = [], []
        for k, t, tr, which, slot in tiles:
            shard, layer, col, _ = items[k]
            ks, ns = shard.shape[1:]
            if col:
                place = outs[k].at[pl.ds(t * tr, tr), pl.ds(pl.multiple_of(chip * ns, GW), ns)]
            else:
                place = outs[k].at[pl.ds(pl.multiple_of(chip * ks + t * tr, 16), tr), :]
            fetch.append(pltpu.make_async_copy(
                ins[k].at[layer, pl.ds(t * tr, tr), :], bufs[2 * which].at[slot], in_sems.at[which, slot]))
            store.append(pltpu.make_async_copy(bufs[2 * which + 1].at[slot], place, out_sems.at[which, slot]))
        busy = {}
        fetch[0].start()
        for i, (k, t, tr, which, slot) in enumerate(tiles):
            if i + 1 < len(tiles):
                fetch[i + 1].start()
            fetch[i].wait()
            if (which, slot) in busy:
                store[busy[which, slot]].wait()
            bufs[2 * which + 1][slot] = bufs[2 * which][slot].astype(bufs[2 * which + 1].dtype)
            store[i].start()
            busy[which, slot] = i
        for i in busy.values():
            store[i].wait()

    def full_shape(shard, col):
        ks, ns = shard.shape[1:]
        return (ks, ns * NCHIP) if col else (ks * NCHIP, ns)

    scratch = []
    for (tr, ns, dt_in, dt_out) in slot_sets:
        scratch += [pltpu.VMEM((2, tr, ns), dt_in), pltpu.VMEM((2, tr, ns), dt_out)]
    scratch += [pltpu.SemaphoreType.DMA((n_sets, 2)), pltpu.SemaphoreType.DMA((n_sets, 2))]
    return _pc(
        jobs, body, name="cast_rest", in_specs=[ANY] * n, out_specs=[ANY] * n,
        out_shape=[jax.ShapeDtypeStruct(full_shape(shard, col), dt) for shard, _, col, dt in items],
        scratch_shapes=scratch, compiler_params=pltpu.CompilerParams(vmem_limit_bytes=VMEM_LIMIT),
    )(*[shard for shard, _, _, _ in items])


def _pair_sum(g, got, col, sc):
    hk, hn = got.shape
    tr = _row_tile(hk, hn, SUM_TILE)
    g_idx = (lambda i, s: (s[0] * (hk // tr) + i, 0)) if col else (lambda i, s: (i, s[0]))
    plain = pl.BlockSpec((tr, hn), lambda i, s: (i, 0))
    return _tiled(lambda a, b: (a.astype(F32) + b.astype(F32),), "pair_sum", (hk // tr,),
                  [pl.BlockSpec((tr, hn), g_idx), plain], [plain], [jax.ShapeDtypeStruct((hk, hn), BF)], [g, got], sc)[0]


def _chip_sum(ph, got, col, layer, depth, sc, carry):
    qk, qn = got.shape[1:]
    tr = _row_tile(qk, qn, SUM_TILE)
    ph_idx = (lambda i, s: (i, s[1])) if col else (lambda i, s: (s[1] * (qk // tr) + i, 0))
    out_shape = (depth, 2 * qk, qn) if col else (depth, qk, 2 * qn)
    out_idx = (lambda i, s: (layer, s[0] * (qk // tr) + i, 0)) if col else (lambda i, s: (layer, i, s[0]))
    in_specs = [pl.BlockSpec((tr, qn), ph_idx)] + [pl.BlockSpec((None, tr, qn), lambda i, s, j=j: (j, i, 0)) for j in range(3)]
    args = [ph, got, got, got]
    if carry is not None:
        in_specs.append(ANY)
        args.append(carry)
    return _tiled(lambda a, b, c_, d_: (a.astype(F32) + b.astype(F32) + c_.astype(F32) + d_.astype(F32),), "chip_sum",
                  (qk // tr,), in_specs, [pl.BlockSpec((None, tr, qn), out_idx)],
                  [jax.ShapeDtypeStruct(out_shape, F32)], args, sc, alias=None if carry is None else {4: 0})[0]


def _adamw_math(w, g, m, v):
    m = ADAM_B1 * m + (1.0 - ADAM_B1) * g
    v = ADAM_B2 * v + (1.0 - ADAM_B2) * (g * g)
    m_hat = m / (1.0 - ADAM_B1 ** ADAM_STEP)
    v_hat = v / (1.0 - ADAM_B2 ** ADAM_STEP)
    delta = -ADAM_LR * (m_hat / (jnp.sqrt(v_hat) + ADAM_EPS) + ADAM_WD * w)
    return delta, m, v


def _adamw_layer(w, g, m, v, layer, carry):
    k, n = w.shape[1:]
    tr = _row_tile(k, n, ADAM_TILE)
    blk = pl.BlockSpec((None, tr, n), lambda i: (layer, i, 0))
    carry = list(carry or [])
    return _tiled(lambda *a: (*_adamw_math(*a), a[1]), "adamw_layer", (k // tr,), [blk] * 4 + [ANY] * len(carry),
                  [blk] * 4, [jax.ShapeDtypeStruct(w.shape, F32)] * 4, [w, g, m, v] + carry, None,
                  alias={4 + pos: pos for pos in range(len(carry))})


def _adamw(w, g, m, v):
    shape = w.shape
    two_d = lambda a: a.reshape(-1, shape[-1])
    outs = _elementwise(_adamw_math, "adamw", [two_d(w), two_d(g), two_d(m), two_d(v)], [F32, F32, F32])
    return [o.reshape(shape) for o in outs]


ANY = pl.BlockSpec(memory_space=pl.ANY)


def _place():
    x, y, c = lax.axis_index("x"), lax.axis_index("y"), lax.axis_index("c")
    chips = [(1 - x, y), (x, 1 - y), (1 - x, 1 - y)]
    return x, y, c, chips


def _cols(ref, start, size):
    idx = (slice(None),) * (len(ref.shape) - 1) + (pl.ds(pl.multiple_of(start, GW), size),)
    return ref.at[idx]


def _rows_of(ref, start, size):
    nd = len(ref.shape)
    idx = (slice(None),) * (nd - 2) + (pl.ds(pl.multiple_of(start, 16), size), slice(None))
    return ref.at[idx]


def _region(ref, col_sharded, chip, half):
    k, n = ref.shape
    align = 16 if ref.dtype == BF else 8
    if col_sharded:
        return ref.at[pl.ds(pl.multiple_of(half * (k // 2), align), k // 2),
                      pl.ds(pl.multiple_of(chip * (n // NCHIP), GW), n // NCHIP)]
    rows = k // (2 * NCHIP)
    return ref.at[pl.ds(pl.multiple_of((2 * chip + half) * rows, align), rows), :]


def _job_gather(bufs, col_sharded, handoff):
    n = len(bufs)

    def copies(outs, send_sems, recv_sems, stage):
        x, y, c, chips = _place()
        sends, lands = [], []
        for k in range(n):
            for j, chip in enumerate(chips):
                theirs = 2 * chip[0] + chip[1]
                if stage == 0:
                    src, to = _region(outs[k], col_sharded[k], 2 * x + y, c), (*chip, c)
                    land = _region(outs[k], col_sharded[k], theirs, c)
                else:
                    src, to = _region(outs[k], col_sharded[k], theirs, c), (x, y, 1 - c)
                    land = _region(outs[k], col_sharded[k], theirs, 1 - c)
                sem = 3 * n * stage + 3 * k + j
                sems = dict(send_sem=send_sems.at[sem], recv_sem=recv_sems.at[sem], device_id=to, device_id_type=MESH)
                sends.append(pltpu.make_async_remote_copy(src_ref=src, dst_ref=src, **sems))
                lands.append(pltpu.make_async_remote_copy(src_ref=land, dst_ref=land, **sems))
        return sends, lands

    def start(ins, outs, send_sems, recv_sems):
        for cp in copies(outs, send_sems, recv_sems, 0)[0]:
            cp.start()

    def hand_on(ins, outs, send_sems, recv_sems):
        for cp in copies(outs, send_sems, recv_sems, 0)[1]:
            cp.wait_recv()
        for cp in copies(outs, send_sems, recv_sems, 1)[0]:
            cp.start()

    def finish(ins, outs, send_sems, recv_sems):
        sends, lands = copies(outs, send_sems, recv_sems, 1)
        for cp in lands:
            cp.wait_recv()
        for cp in copies(outs, send_sems, recv_sems, 0)[0] + sends:
            cp.wait_send()

    return _Job(bufs, n, [], 6 * n, [(0.0, start), (handoff, hand_on), (1.0, finish)])


def _half(ref, col_sharded, c):
    k, n = ref.shape[-2:]
    return _rows_of(ref, c * (k // 2), k // 2) if col_sharded else _cols(ref, c * (n // 2), n // 2)


def _quarter(ref, col_sharded, j):
    k, n = ref.shape[-2:]
    return _cols(ref, j * (n // NCHIP), n // NCHIP) if col_sharded else _rows_of(ref, j * (k // NCHIP), k // NCHIP)


def _job_pair(grads, col_sharded):
    n = len(grads)

    def half_shape(g, col):
        return (g.shape[0] // 2, g.shape[1]) if col else (g.shape[0], g.shape[1] // 2)

    def copies(ins, got, send_sems, recv_sems):
        x, y, c, _ = _place()
        return [pltpu.make_async_remote_copy(
            src_ref=_half(ins[k], col_sharded[k], 1 - c), dst_ref=got[k], send_sem=send_sems.at[k],
            recv_sem=recv_sems.at[k], device_id=(x, y, 1 - c), device_id_type=MESH) for k in range(n)]

    def start(*refs):
        for cp in copies(*refs):
            cp.start()

    def finish(*refs):
        for cp in copies(*refs):
            cp.wait()

    fresh = [jax.ShapeDtypeStruct(half_shape(g, col), g.dtype) for g, col in zip(grads, col_sharded)]
    return _Job(grads, 0, fresh, n, [(0.0, start), (1.0, finish)])


def _job_chip(halves, col_sharded):
    n = len(halves)

    def quarter_shape(h, col):
        return (3, h.shape[0], h.shape[1] // NCHIP) if col else (3, h.shape[0] // NCHIP, h.shape[1])

    def copies(ins, got, send_sems, recv_sems):
        x, y, c, chips = _place()
        return [pltpu.make_async_remote_copy(
            src_ref=_quarter(ins[k], col_sharded[k], 2 * chip[0] + chip[1]), dst_ref=got[k].at[j],
            send_sem=send_sems.at[3 * k + j], recv_sem=recv_sems.at[3 * k + j], device_id=(*chip, c), device_id_type=MESH)
            for k in range(n) for j, chip in enumerate(chips)]

    def start(*refs):
        for cp in copies(*refs):
            cp.start()

    def finish(*refs):
        for cp in copies(*refs):
            cp.wait()

    fresh = [jax.ShapeDtypeStruct(quarter_shape(h, col), h.dtype) for h, col in zip(halves, col_sharded)]
    return _Job(halves, 0, fresh, 3 * n, [(0.0, start), (1.0, finish)])


def _job_sibling(shards, col_sharded, layers):
    n = len(shards)

    def copies(outs, send_sems, recv_sems):
        x, y, c, _ = _place()
        sends, lands = [], []
        for k in range(n):
            sems = dict(send_sem=send_sems.at[k], recv_sem=recv_sems.at[k], device_id=(x, y, 1 - c), device_id_type=MESH)
            mine = _half(outs[k].at[layers[k]], col_sharded[k], c)
            theirs = _half(outs[k].at[layers[k]], col_sharded[k], 1 - c)
            sends.append(pltpu.make_async_remote_copy(src_ref=mine, dst_ref=mine, **sems))
            lands.append(pltpu.make_async_remote_copy(src_ref=theirs, dst_ref=theirs, **sems))
        return sends, lands

    def start(ins, outs, send_sems, recv_sems):
        for cp in copies(outs, send_sems, recv_sems)[0]:
            cp.start()

    def finish(ins, outs, send_sems, recv_sems):
        sends, lands = copies(outs, send_sems, recv_sems)
        for cp in lands:
            cp.wait_recv()
        for cp in sends:
            cp.wait_send()

    return _Job(shards, n, [], n, [(0.0, start), (1.0, finish)])


def _standalone(jobs, name):
    return _pc(jobs, lambda: None, name=name, in_specs=[], out_specs=[], out_shape=[])()[1]


def _all_reduce_small(buf, jobs=()):
    rows = buf.shape[0]
    per, half = rows // NDEV, rows // 2

    def body(in_ref, out_ref, got_sib, pair_ref, got_chip, send_sems, recv_sems):
        x, y, c, chips = _place()
        sibling = (x, y, 1 - c)

        def part(ref, start, size):
            return ref.at[pl.ds(pl.multiple_of(start, 8), size), :]

        def copy(src, dst, sem, to):
            return pltpu.make_async_remote_copy(src_ref=src, dst_ref=dst, send_sem=send_sems.at[sem],
                                                recv_sem=recv_sems.at[sem], device_id=to, device_id_type=MESH)

        down = copy(part(in_ref, (1 - c) * half, half), got_sib, 0, sibling)
        down.start()
        down.wait()
        pair_ref[...] = part(in_ref, c * half, half)[...] + got_sib[...]
        scatter = [copy(part(pair_ref, (2 * chip[0] + chip[1]) * per, per), got_chip.at[k], 1 + k, (*chip, c))
                   for k, chip in enumerate(chips)]
        for cp in scatter:
            cp.start()
        for cp in scatter:
            cp.wait()
        mine = part(out_ref, c * half + (2 * x + y) * per, per)
        mine[...] = part(pair_ref, (2 * x + y) * per, per)[...] + got_chip[0] + got_chip[1] + got_chip[2]
        share = [copy(mine, mine, 4 + k, (*chip, c)) for k, chip in enumerate(chips)]
        for cp in share:
            cp.start()
        for k, chip in enumerate(chips):
            share[k].wait_send()
            theirs = part(out_ref, c * half + (2 * chip[0] + chip[1]) * per, per)
            copy(theirs, theirs, 4 + k, (*chip, c)).wait_recv()
        up = copy(part(out_ref, c * half, half), part(out_ref, c * half, half), 7, sibling)
        up.start()
        up.wait_send()
        other = part(out_ref, (1 - c) * half, half)
        copy(other, other, 7, sibling).wait_recv()

    vmem = pl.BlockSpec(memory_space=pltpu.VMEM)
    return _pc(
        jobs, body, name="all_reduce_small", in_specs=[vmem], out_specs=vmem,
        out_shape=jax.ShapeDtypeStruct((rows, GW), F32),
        scratch_shapes=[pltpu.VMEM((half, GW), F32), pltpu.VMEM((half, GW), F32), pltpu.VMEM((3, per, GW), F32),
                        pltpu.SemaphoreType.DMA((8,)), pltpu.SemaphoreType.DMA((8,))],
    )(buf)


BIG = ("ffn1_w13", "ffn1_w2", "w_in", "w_up", "w_out", "ffn2_w13", "ffn2_w2")
BIG_COL_SHARDED = (True, False, True, True, False, True, False)
SMALL = ("ffn1_norm", "mix_norm", "pool_w", "pool_scale", "sconv_w", "cconv_w", "cconv_ln_g", "cconv_ln_b",
         "sgu_ln_g", "sgu_ln_b", "sgu_w", "sgu_b", "ffn2_norm", "final_norm")
WEIGHTS = ("ffn1_norm", "ffn1_w13", "ffn1_w2", "mix_norm", "w_in", "pool_w", "pool_scale", "sconv_w", "cconv_w",
           "cconv_ln_g", "cconv_ln_b", "sgu_ln_g", "sgu_ln_b", "sgu_w", "sgu_b", "w_up", "w_out", "ffn2_norm",
           "ffn2_w13", "ffn2_w2", "final_norm")


def _pad_rows(a, rows):
    return jnp.pad(a, ((0, 0), (0, rows - a.shape[1]), (0, 0)))


def kernel(x, ffn1_norm, ffn1_w13, ffn1_w2, mix_norm, w_in, pool_w, pool_scale, sconv_w, cconv_w, cconv_ln_g, cconv_ln_b, sgu_ln_g, sgu_ln_b, sgu_w, sgu_b, w_up, w_out, ffn2_norm, ffn2_w13, ffn2_w2, final_norm, loss_target, m_ffn1_norm, m_ffn1_w13, m_ffn1_w2, m_mix_norm, m_w_in, m_pool_w, m_pool_scale, m_sconv_w, m_cconv_w, m_cconv_ln_g, m_cconv_ln_b, m_sgu_ln_g, m_sgu_ln_b, m_sgu_w, m_sgu_b, m_w_up, m_w_out, m_ffn2_norm, m_ffn2_w13, m_ffn2_w2, m_final_norm, v_ffn1_norm, v_ffn1_w13, v_ffn1_w2, v_mix_norm, v_w_in, v_pool_w, v_pool_scale, v_sconv_w, v_cconv_w, v_cconv_ln_g, v_cconv_ln_b, v_sgu_ln_g, v_sgu_ln_b, v_sgu_w, v_sgu_b, v_w_up, v_w_out, v_ffn2_norm, v_ffn2_w13, v_ffn2_w2, v_final_norm):
    args = dict(locals())
    w = {nm: args[nm] for nm in WEIGHTS}
    m = {nm: args["m_" + nm] for nm in WEIGHTS}
    v = {nm: args["v_" + nm] for nm in WEIGHTS}
    depth = ffn1_w13.shape[0]
    chip = 2 * lax.axis_index("x") + lax.axis_index("y")

    sc = jnp.stack([lax.axis_index("c"), chip]).astype(jnp.int32)
    col_of = dict(zip(BIG, BIG_COL_SHARDED), sconv_w=True, cconv_w=True)
    sources = {nm: (w[nm], BF) for nm in BIG}
    sources["w_up"] = (w_up.reshape(depth, NBR * BW, w_up.shape[-1]), BF)
    sources["sconv_w"] = (_pad_rows(sconv_w, 2 * SKP), F32)
    sources["cconv_w"] = (_pad_rows(cconv_w, 2 * CKP), F32)
    first_group = ("ffn1_w13", "ffn1_w2")
    full = [dict() for _ in range(depth)]
    for nm in first_group:
        full[0][nm] = _cast_into(sources[nm][0], 0, col_of[nm], sources[nm][1], sc)

    def gather_job(l, names, handoff):
        return _job_gather([full[l][nm] for nm in names], [col_of[nm] for nm in names], handoff)

    def gather_with(l, names, handoff, call):
        res, job_outs = call([gather_job(l, names, handoff)] if l < depth else [])
        if l < depth:
            full[l].update(zip(names, job_outs[0]))
        return res

    rest = [(nm, l) for l in range(depth) for nm in sources if not (l == 0 and nm in first_group)]
    casted = gather_with(0, first_group, 1.0, lambda jobs: _cast_rest(
        [(sources[nm][0], l, col_of[nm], sources[nm][1]) for nm, l in rest], jobs))
    for (nm, l), arr in zip(rest, casted):
        full[l][nm] = arr

    xs = x[0]
    row = lambda a: a.reshape(1, -1)
    saved = []
    for l in range(depth):
        g1, gm, g2 = row(ffn1_norm[l]), row(mix_norm[l]), row(ffn2_norm[l])
        x1, h1, s1, ab1 = gather_with(l, ("w_in",), 1.0, lambda jobs: _ffn_fwd(
            xs, g1, full[l]["ffn1_w13"], full[l]["ffn1_w2"], jobs))
        hm, p = gather_with(l, ("w_up", "w_out", "sconv_w", "cconv_w", "ffn2_w2"), 0.85, lambda jobs: _mix_in(
            x1, gm, full[l]["w_in"], jobs))
        branch = (pool_w[l], row(pool_scale[l]), full[l]["sconv_w"][:SKP], full[l]["cconv_w"][:CKP], row(cconv_ln_g[l]),
                  row(cconv_ln_b[l]), row(sgu_ln_g[l]), row(sgu_ln_b[l]), sgu_w[l], sgu_b[l].T)
        y, conv_z = gather_with(l, ("ffn2_w13",), 1.0, lambda jobs: _mix_branches_fwd(p, *branch, jobs=jobs))
        w_up_l = full[l]["w_up"].reshape(NBR, BW, D)
        x2, merged, up = gather_with(l + 1, ("ffn1_w2",), 0.8, lambda jobs: _mix_out_fwd(
            x1, y, p, w_up_l, full[l]["w_out"], jobs))
        x3, h2, s2, ab2 = gather_with(l + 1, ("ffn1_w13",), 1.0, lambda jobs: _ffn_fwd(
            x2, g2, full[l]["ffn2_w13"], full[l]["ffn2_w2"], jobs))
        lw = dict(g1=g1, gm=gm, g2=g2, w13a=full[l]["ffn1_w13"], w2a=full[l]["ffn1_w2"], w13b=full[l]["ffn2_w13"],
                  w2b=full[l]["ffn2_w2"], w_in=full[l]["w_in"], w_up=w_up_l, w_out=full[l]["w_out"], branch=branch)
        saved.append(dict(lw=lw, x0=xs, x1=x1, x2=x2, h1=h1, s1=s1, ab1=ab1, hm=hm, p=p, y=y, z=conv_z, merged=merged, up=up, h2=h2,
                          s2=s2, ab2=ab2))
        xs = x3

    (dx, d_final, loss_part), _ = _loss_head(xs, row(final_norm), loss_target[0])
    loss = lax.psum(loss_part[0, 0], ("x", "y", "c"))

    ici_us = dict(ffn1_w13=64, ffn1_w2=32, w_in=93, w_up=23, w_out=12, ffn2_w13=64, ffn2_w2=32)
    parts, pair_sums, reduced, big_updates, pending = {}, {}, {}, {}, []

    def take_jobs(budget_us):
        chosen = []
        for task in list(pending):
            kind, (nm, _) = task
            if kind == "chip":
                if ici_us[nm] > budget_us:
                    continue
                budget_us -= ici_us[nm]
            if kind == "sib" and any(k == "sib" and key[0] == nm for k, key in chosen):
                continue
            chosen.append(task)
            pending.remove(task)
        groups, jobs = [], []
        for kind in ("pair", "sib", "chip"):
            keys = [key for k, key in chosen if k == kind]
            if not keys:
                continue
            cols = [col_of[nm] for nm, _ in keys]
            groups.append((kind, keys))
            if kind == "pair":
                jobs.append(_job_pair([parts[key] for key in keys], cols))
            elif kind == "chip":
                jobs.append(_job_chip([pair_sums[key] for key in keys], cols))
            else:
                jobs.append(_job_sibling([reduced[nm] for nm, _ in keys], cols, [layer for _, layer in keys]))
        return groups, jobs

    def settle(groups, job_outs):
        for (kind, keys), outs in zip(groups, job_outs):
            for key, out in zip(keys, outs):
                nm, layer = key
                if kind == "pair":
                    pair_sums[key] = _pair_sum(parts[key], out, col_of[nm], sc)
                    pending.append(("chip", key))
                elif kind == "chip":
                    assert not any(k == "sib" and other[0] == nm for k, other in pending)
                    reduced[nm] = _chip_sum(pair_sums[key], out, col_of[nm], layer, depth, sc, reduced.get(nm))
                    pending.append(("sib", key))
                else:
                    reduced[nm] = out
                    as3 = lambda a: a.reshape(out.shape)
                    big_updates[nm] = _adamw_layer(as3(w[nm]), out, as3(m[nm]), as3(v[nm]), layer, big_updates.get(nm))

    def run(budget_us, call, carrier=True):
        groups, jobs = take_jobs(budget_us) if carrier else ([], [])
        res, job_outs = call(jobs)
        settle(groups, job_outs)
        return res

    def wgrad_done(key, partial):
        parts[key] = partial
        pending.append(("pair", key))

    small_parts = {nm: [None] * depth for nm in SMALL if nm != "final_norm"}
    for l in reversed(range(depth)):
        sv = saved[l]
        lw = sv["lw"]
        dx, dab, dyh, dg2 = run(100, lambda jobs: _ffn_bwd(
            dx, sv["x2"], lw["g2"], sv["ab2"], lw["w13b"], lw["w2b"], jobs))
        wgrad_done(("ffn2_w13", l), run(58, lambda jobs: _wgrad(sv["h2"], dab, D, 512, "wgrad_w13", jobs), False))
        wgrad_done(("ffn2_w2", l), run(33, lambda jobs: _wgrad(sv["s2"], dyh, 256, D, "wgrad_w2", jobs), False))
        small_parts["ffn2_norm"][l] = dg2

        dy, dp, dup, dxb = run(70, lambda jobs: _mix_out_bwd(dx, sv["up"], sv["p"], lw["w_up"], lw["w_out"], jobs))
        wgrad_done(("w_out", l), run(16, lambda jobs: _wgrad(sv["merged"], dxb, D, 512, "wgrad_w_out", jobs), False))
        wgrad_done(("w_up", l), run(25, lambda jobs: _wgrad_groups(sv["y"], dup, BW, D, "wgrad_w_up", jobs), False))
        (dp, d_pool_w, d_pool_scale, d_sconv, d_cconv, d_clg, d_clb, d_slg, d_slb, d_sgu_w, d_sgu_b) = run(
            130, lambda jobs: _mix_branches_bwd(sv["p"], dy, sv["z"], dp, *lw["branch"], jobs=jobs))
        wgrad_done(("w_in", l), run(80, lambda jobs: _wgrad(sv["hm"], dp, D, 512, "wgrad_w_in", jobs), False))
        dx, dgm = run(91, lambda jobs: _mix_in_bwd(dp, lw["w_in"], sv["x1"], lw["gm"], dx, jobs))
        small_parts["mix_norm"][l] = dgm
        small_parts["pool_w"][l] = d_pool_w
        small_parts["pool_scale"][l] = d_pool_scale
        small_parts["sconv_w"][l] = d_sconv[:SK]
        small_parts["cconv_w"][l] = d_cconv[:CK]
        small_parts["cconv_ln_g"][l] = d_clg
        small_parts["cconv_ln_b"][l] = d_clb
        small_parts["sgu_ln_g"][l] = d_slg
        small_parts["sgu_ln_b"][l] = d_slb
        small_parts["sgu_w"][l] = d_sgu_w
        small_parts["sgu_b"][l] = jnp.sum(d_sgu_b, axis=-1)

        dx, dab, dyh, dg1 = run(100, lambda jobs: _ffn_bwd(
            dx, sv["x0"], lw["g1"], sv["ab1"], lw["w13a"], lw["w2a"], jobs))
        wgrad_done(("ffn1_w2", l), run(33, lambda jobs: _wgrad(sv["s1"], dyh, 256, D, "wgrad_w2", jobs), l == 0))
        wgrad_done(("ffn1_w13", l), run(58, lambda jobs: _wgrad(sv["h1"], dab, D, 512, "wgrad_w13", jobs), l == 0))
        small_parts["ffn1_norm"][l] = dg1
    grad_x = dx[None]

    small_local = {nm: jnp.stack(parts).reshape(depth, *w[nm].shape[1:-1], -1) if nm not in ("sconv_w", "cconv_w")
                   else jnp.stack(parts) for nm, parts in small_parts.items()}
    small_local["final_norm"] = d_final.reshape(-1)
    sizes = [small_local[nm].size for nm in SMALL]
    total = sum(sizes)
    pad_to = NDEV * 8 * GW
    padded = -(-total // pad_to) * pad_to
    packed = jnp.concatenate([small_local[nm].reshape(-1) for nm in SMALL] + [jnp.zeros((padded - total,), F32)])
    groups, jobs = take_jobs(float("inf"))
    summed, job_outs = _all_reduce_small(packed.reshape(-1, GW), jobs)
    settle(groups, job_outs)
    summed = summed.reshape(-1)
    flushes = 0
    while pending:
        groups, jobs = take_jobs(float("inf"))
        settle(groups, _standalone(jobs, "grad_flush_%d" % flushes))
        flushes += 1
    big_grads = {nm: big_updates[nm][3].reshape(w[nm].shape) for nm in BIG}
    small_grads, off = {}, 0
    for nm, size in zip(SMALL, sizes):
        small_grads[nm] = summed[off:off + size].reshape(small_local[nm].shape)
        off += size
    for nm in ("sconv_w", "cconv_w"):
        small_grads[nm] = lax.dynamic_slice_in_dim(small_grads[nm], chip * GW, GW, axis=2)

    grads = {**big_grads, **small_grads}

    delta, new_m, new_v = {}, {}, {}
    for nm in BIG:
        delta[nm], new_m[nm], new_v[nm] = [a.reshape(w[nm].shape) for a in big_updates[nm][:3]]
    s_sizes = [w[nm].size for nm in SMALL]
    s_total = sum(s_sizes)
    s_padded = -(-s_total // (8 * GW)) * (8 * GW)

    def pack(tree):
        return jnp.concatenate([tree[nm].reshape(-1) for nm in SMALL] + [jnp.ones((s_padded - s_total,), F32)]).reshape(-1, GW)

    packed_out = _adamw(pack(w), pack(grads), pack(m), pack(v))
    off = 0
    for nm, size in zip(SMALL, s_sizes):
        for tree, arr in zip((delta, new_m, new_v), packed_out):
            tree[nm] = arr.reshape(-1)[off:off + size].reshape(w[nm].shape)
        off += size

    return (loss, grad_x, *[grads[nm] for nm in WEIGHTS], *[delta[nm] for nm in WEIGHTS],
            *[new_m[nm] for nm in WEIGHTS], *[new_v[nm] for nm in WEIGHTS])
```

```python
import jax
import jax.numpy as jnp
from jax import lax
from jax.experimental import pallas as pl
from jax.experimental.pallas import tpu as pltpu

D = 1024
FF = 2816
BW = 512
NBR = 4
MIXC = 4096
INC = 8192
GW = 128
CHUNK = 64
SK = 3
CK = 31
SKP = 8
CKP = 32
HALO = 32
TOKEN_TILE = 256
BRANCH_TILE = 512
EPS = 1e-6
NCHIP = 4
NDEV = 8

ADAM_LR = 0.001
ADAM_B1 = 0.9
ADAM_B2 = 0.999
ADAM_EPS = 1e-08
ADAM_WD = 0.01
ADAM_STEP = 10

VMEM_LIMIT = 56 * 1024 * 1024

BF = jnp.bfloat16
F32 = jnp.float32
MESH = pl.DeviceIdType.MESH
NT = (((1,), (1,)), ((), ()))
TN = (((0,), (0,)), ((), ()))


def _params(n_axes):
    return pltpu.CompilerParams(dimension_semantics=("arbitrary",) * n_axes, vmem_limit_bytes=VMEM_LIMIT)


class _Job:
    def __init__(self, args, n_inplace, fresh, nsem, phases):
        self.args, self.n_inplace, self.fresh, self.nsem, self.phases = list(args), n_inplace, list(fresh), nsem, phases


def _pc(jobs, body, name, in_specs, out_specs, out_shape, grid=(), scratch_shapes=(), input_output_aliases=None,
        compiler_params=None, grid_spec_scalars=None):
    single = not isinstance(out_shape, (list, tuple))
    core_out_specs = [out_specs] if single else list(out_specs)
    core_out_shape = [out_shape] if single else list(out_shape)
    n_in, n_out, n_scr = len(in_specs), len(core_out_specs), len(scratch_shapes)
    n_pre = 0 if grid_spec_scalars is None else 1
    all_in, all_out, all_shape = list(in_specs), list(core_out_specs), list(core_out_shape)
    all_scr, aliases, extra_args, layout = list(scratch_shapes), dict(input_output_aliases or {}), [], []
    for job in jobs:
        n_job_out = job.n_inplace + len(job.fresh)
        layout.append((len(all_in), len(job.args), len(all_out), n_job_out, len(all_scr)))
        for a in range(job.n_inplace):
            aliases[n_pre + len(all_in) + a] = len(all_out) + a
        all_in += [ANY] * len(job.args)
        extra_args += job.args
        all_out += [ANY] * n_job_out
        all_shape += [jax.ShapeDtypeStruct(a.shape, a.dtype) for a in job.args[:job.n_inplace]] + job.fresh
        all_scr += [pltpu.SemaphoreType.DMA((job.nsem,)), pltpu.SemaphoreType.DMA((job.nsem,))]
    steps = 1
    for extent in grid:
        steps *= extent
    events = []
    for (i0, na, o0, no, s0), job in zip(layout, jobs):
        for frac, fn in job.phases:
            events.append((min(int(frac * steps), steps - 1), frac >= 1.0, len(events), fn, (i0, na, o0, no, s0)))
    events.sort(key=lambda e: e[:3])

    def wrapped(*refs):
        pre, refs = refs[:n_pre], refs[n_pre:]
        ins, outs, scr = refs[:len(all_in)], refs[len(all_in):len(all_in) + len(all_out)], refs[len(all_in) + len(all_out):]
        step = 0
        for axis, extent in enumerate(grid):
            step = step * extent + pl.program_id(axis)

        def emit(event):
            at, _, _, fn, (i0, na, o0, no, s0) = event
            run = lambda: fn(ins[i0:i0 + na], outs[o0:o0 + no], scr[s0], scr[s0 + 1])
            if steps == 1:
                run()
            else:
                pl.when(step == at)(run)

        for event in events:
            if not event[1]:
                emit(event)
        body(*pre, *ins[:n_in], *outs[:n_out], *scr[:n_scr])
        for event in events:
            if event[1]:
                emit(event)

    kwargs = dict(name=name, out_shape=all_shape, input_output_aliases=aliases)
    if compiler_params is not None:
        kwargs["compiler_params"] = compiler_params
    if grid_spec_scalars is not None:
        kwargs["grid_spec"] = pltpu.PrefetchScalarGridSpec(
            num_scalar_prefetch=1, grid=grid, in_specs=all_in, out_specs=all_out, scratch_shapes=all_scr)
    else:
        kwargs.update(in_specs=all_in, out_specs=all_out, scratch_shapes=all_scr)
        if grid:
            kwargs["grid"] = grid
    call = pl.pallas_call(wrapped, **kwargs)

    def run_call(*args):
        pre_args = [] if grid_spec_scalars is None else [grid_spec_scalars]
        res = list(call(*pre_args, *args, *extra_args))
        core = res[0] if single else res[:n_out]
        job_outs = [res[o0:o0 + no] for (_, _, o0, no, _) in layout]
        return core, job_outs

    return run_call


def _full(shape):
    nd = len(shape)
    return pl.BlockSpec(shape, lambda *_: (0,) * nd, pipeline_mode=pl.Buffered(1))


def _rows(ts, width, col=0):
    return pl.BlockSpec((ts, width), lambda i: (i, col))


def _sig(v):
    return jax.nn.sigmoid(v)


def _rms_stats(x):
    r = lax.rsqrt(jnp.mean(x * x, axis=-1, keepdims=True) + EPS)
    return r, x * r


def _rms_bwd(x, g, dh):
    r, xh = _rms_stats(x)
    dg = jnp.sum(dh * xh, axis=0, keepdims=True)
    dxh = dh * g
    dx = r * (dxh - xh * jnp.mean(dxh * xh, axis=-1, keepdims=True))
    return dx, dg


def _accumulate(ref, val, first):
    @pl.when(first)
    def _():
        ref[...] = val

    @pl.when(jnp.logical_not(first))
    def _():
        ref[...] += val


def _ffn_fwd(x, g, w13, w2, jobs=()):
    s_len = x.shape[0]
    ts = min(TOKEN_TILE, s_len)

    def body(x_ref, g_ref, w13_ref, w2_ref, xo_ref, h_ref, s_ref, dsab_ref):
        xv = x_ref[...]
        r, xh = _rms_stats(xv)
        h = (xh * g_ref[...]).astype(BF)
        h_ref[...] = h
        ab = jnp.dot(h, w13_ref[...], preferred_element_type=F32)
        a = ab[:, :FF]
        b = ab[:, FF:]
        sg = _sig(a)
        sil = a * sg
        dsab_ref[:, :FF] = (b * (sg * (1.0 + a * (1.0 - sg)))).astype(BF)
        dsab_ref[:, FF:] = sil.astype(BF)
        s = (sil * b).astype(BF)
        s_ref[...] = s
        xo_ref[...] = xv + 0.5 * jnp.dot(s, w2_ref[...], preferred_element_type=F32)

    return _pc(
        jobs, body, name="ffn_fwd", grid=(s_len // ts,),
        in_specs=[_rows(ts, D), _full((1, D)), _full((D, 2 * FF)), _full((FF, D))],
        out_specs=[_rows(ts, D), _rows(ts, D), _rows(ts, FF), _rows(ts, 2 * FF)],
        out_shape=[jax.ShapeDtypeStruct((s_len, D), F32), jax.ShapeDtypeStruct((s_len, D), BF),
                   jax.ShapeDtypeStruct((s_len, FF), BF), jax.ShapeDtypeStruct((s_len, 2 * FF), BF)],
        compiler_params=_params(1),
    )(x, g, w13, w2)


def _ffn_bwd(dxo, x, g, dsab, w13, w2, jobs=()):
    s_len = x.shape[0]
    ts = min(TOKEN_TILE, s_len)

    def body(dxo_ref, x_ref, g_ref, dsab_ref, w13_ref, w2_ref, dxi_ref, dab_ref, dy_ref, dg_ref):
        i = pl.program_id(0)
        dxo_v = dxo_ref[...]
        dy = (0.5 * dxo_v).astype(BF)
        dy_ref[...] = dy
        ds = lax.dot_general(dy, w2_ref[...], NT, preferred_element_type=F32)
        dab_ref[:, :FF] = (ds * dsab_ref[:, :FF].astype(F32)).astype(BF)
        dab_ref[:, FF:] = (ds * dsab_ref[:, FF:].astype(F32)).astype(BF)
        dh = lax.dot_general(dab_ref[...], w13_ref[...], NT, preferred_element_type=F32)
        dx, dg = _rms_bwd(x_ref[...], g_ref[...], dh)
        dxi_ref[...] = dxo_v + dx
        _accumulate(dg_ref, dg, i == 0)

    return _pc(
        jobs, body, name="ffn_bwd", grid=(s_len // ts,),
        in_specs=[_rows(ts, D), _rows(ts, D), _full((1, D)), _rows(ts, 2 * FF), _full((D, 2 * FF)), _full((FF, D))],
        out_specs=[_rows(ts, D), _rows(ts, 2 * FF), _rows(ts, D), pl.BlockSpec((1, D), lambda i: (0, 0))],
        out_shape=[jax.ShapeDtypeStruct((s_len, D), F32), jax.ShapeDtypeStruct((s_len, 2 * FF), BF),
                   jax.ShapeDtypeStruct((s_len, D), BF), jax.ShapeDtypeStruct((1, D), F32)],
        compiler_params=_params(1),
    )(dxo, x, g, dsab, w13, w2)


def _wgrad(a, b, tk, tn, name, jobs=()):
    s_len, k = a.shape
    n = b.shape[1]

    def body(a_ref, b_ref, o_ref):
        o_ref[...] = lax.dot_general(a_ref[...], b_ref[...], TN, preferred_element_type=F32).astype(BF)

    return _pc(
        jobs, body, name=name, grid=(k // tk, n // tn),
        in_specs=[pl.BlockSpec((s_len, tk), lambda i, j: (0, i)), pl.BlockSpec((s_len, tn), lambda i, j: (0, j))],
        out_specs=pl.BlockSpec((tk, tn), lambda i, j: (i, j)),
        out_shape=jax.ShapeDtypeStruct((k, n), BF),
        compiler_params=_params(2),
    )(a, b)


def _wgrad_groups(a, b, ka, nb, name, jobs=()):
    s_len = a.shape[0]
    groups = a.shape[1] // ka

    def body(a_ref, b_ref, o_ref):
        o_ref[...] = lax.dot_general(a_ref[...], b_ref[...], TN, preferred_element_type=F32).astype(BF)

    return _pc(
        jobs, body, name=name, grid=(groups,),
        in_specs=[pl.BlockSpec((s_len, ka), lambda gi: (0, gi)), pl.BlockSpec((s_len, nb), lambda gi: (0, gi))],
        out_specs=pl.BlockSpec((ka, nb), lambda gi: (gi, 0)),
        out_shape=jax.ShapeDtypeStruct((groups * ka, nb), BF),
        compiler_params=_params(1),
    )(a, b)


def _mix_in(x, g, w_in, jobs=()):
    s_len = x.shape[0]
    ts = min(TOKEN_TILE, s_len)

    def body(x_ref, g_ref, w_ref, h_ref, p_ref):
        _, xh = _rms_stats(x_ref[...])
        h = (xh * g_ref[...]).astype(BF)
        h_ref[...] = h
        p_ref[...] = jnp.dot(h, w_ref[...], preferred_element_type=F32).astype(BF)

    return _pc(
        jobs, body, name="mix_in", grid=(s_len // ts,),
        in_specs=[_rows(ts, D), _full((1, D)), _full((D, INC))],
        out_specs=[_rows(ts, D), _rows(ts, INC)],
        out_shape=[jax.ShapeDtypeStruct((s_len, D), BF), jax.ShapeDtypeStruct((s_len, INC), BF)],
        compiler_params=_params(1),
    )(x, g, w_in)


def _shift(e, j):
    n = e.shape[0]
    j = j % n
    return e if j == 0 else pltpu.roll(e, j, 0)


def _ln_stats(z):
    mu = jnp.mean(z, axis=-1, keepdims=True)
    zc = z - mu
    rs = lax.rsqrt(jnp.mean(zc * zc, axis=-1, keepdims=True) + EPS)
    return rs, zc * rs


def _ln_bwd(rs, zn, dzn):
    return rs * (dzn - jnp.mean(dzn, axis=-1, keepdims=True) - zn * jnp.mean(dzn * zn, axis=-1, keepdims=True))


_GELU_C0 = 0.7978845608028654
_GELU_C1 = 0.044715


def _gelu(p):
    th = jnp.tanh(_GELU_C0 * (p + _GELU_C1 * p * p * p))
    return 0.5 * p * (1.0 + th), th


def _gelu_grad(p, th):
    return 0.5 * (1.0 + th) + 0.5 * p * (1.0 - th * th) * (_GELU_C0 * (1.0 + 3.0 * _GELU_C1 * p * p))


def _pool_diff(a, t, sign):
    outs = []
    for gi in range(NBR):
        win = 2 ** (gi + 1)
        ag = a[:, gi * GW:(gi + 1) * GW]
        cnt = jnp.clip(t + 1, 1, win).astype(F32)
        ws = ag if sign > 0 else ag / cnt
        for s in range(gi + 1):
            ws = ws + _shift(ws, sign * (2 ** s))
        outs.append((ws / cnt if sign > 0 else ws) - ag)
    return outs


def _sgu_mask():
    row = lax.broadcasted_iota(jnp.int32, (GW, GW), 0)
    col = lax.broadcasted_iota(jnp.int32, (GW, GW), 1)
    return (col // CHUNK) <= (row // CHUNK)


def _assemble(pe_ref, prev_ref, cur_ref, next_ref, i, last, ts):
    pe_ref[0:HALO, :] = jnp.where(i > 0, prev_ref[...], jnp.zeros_like(prev_ref))
    pe_ref[HALO:HALO + ts, :] = cur_ref[...]
    if next_ref is not None:
        pe_ref[HALO + ts:, :] = jnp.where(i < last, next_ref[...], jnp.zeros_like(next_ref))


def _halo_specs(ts, width, s_len, with_next):
    per = ts // HALO
    specs = [pl.BlockSpec((HALO, width), lambda i: (jnp.maximum(i * per - 1, 0), 0)),
             pl.BlockSpec((ts, width), lambda i: (i, 0))]
    if with_next:
        specs.append(pl.BlockSpec((HALO, width), lambda i: (jnp.minimum((i + 1) * per, s_len // HALO - 1), 0)))
    return specs


def _branch_weights_specs():
    return [_full((NBR, GW, GW)), _full((1, BW)), _full((SKP, BW)), _full((CKP, BW)), _full((1, BW)), _full((1, BW)),
            _full((1, BW)), _full((1, BW)), _full((NBR, GW, GW)), _full((GW, NBR))]


def _mix_branches_fwd(p, pool_w, pool_scale, sconv_w, cconv_w, cln_g, cln_b, sln_g, sln_b, sgu_w, sgu_bt, jobs=()):
    s_len = p.shape[0]
    ts = min(BRANCH_TILE, s_len)
    ext = HALO + ts

    def body(pp_ref, pc_ref, pw_ref, ps_ref, sw_ref, cw_ref, clg_ref, clb_ref, slg_ref, slb_ref, gw_ref, gb_ref,
             y_ref, z_ref, pe_ref):
        i = pl.program_id(0)
        _assemble(pe_ref, pp_ref, pc_ref, None, i, 0, ts)
        t = i * ts - HALO + lax.broadcasted_iota(jnp.int32, (ext, 1), 0)

        dgs = _pool_diff(pe_ref[:, 0:BW].astype(F32), t, 1)
        for gi in range(NBR):
            e = jnp.dot(dgs[gi][HALO:].astype(BF), pw_ref[gi].astype(BF), preferred_element_type=F32)
            y_ref[:, gi * GW:(gi + 1) * GW] = (e * ps_ref[:, gi * GW:(gi + 1) * GW]).astype(BF)

        xin = pe_ref[:, BW:2 * BW].astype(F32)
        cg = pe_ref[:, 3 * BW:4 * BW].astype(F32)
        q = cg * xin
        cv = sw_ref[2:3, :] * q + sw_ref[1:2, :] * _shift(q, 1) + sw_ref[0:1, :] * _shift(q, 2)
        y_ref[:, BW:2 * BW] = (pe_ref[HALO:, 2 * BW:3 * BW].astype(F32) * cv[HALO:]).astype(BF)

        yg = pe_ref[:, 4 * BW:5 * BW].astype(F32) * _sig(pe_ref[:, 5 * BW:6 * BW].astype(F32))
        z = cw_ref[CK - 1:CK, :] * yg
        for j in range(1, CK):
            z = z + cw_ref[CK - 1 - j:CK - j, :] * _shift(yg, j)
        z_ref[...] = z[HALO:].astype(BF)
        _, zn = _ln_stats(z[HALO:])
        nn = zn * clg_ref[...] + clb_ref[...]
        y_ref[:, 2 * BW:3 * BW] = (nn * _sig(nn)).astype(BF)

        u, _ = _gelu(pc_ref[:, 6 * BW:7 * BW].astype(F32))
        v, _ = _gelu(pc_ref[:, 7 * BW:8 * BW].astype(F32))
        _, vn = _ln_stats(v)
        vn = (vn * slg_ref[...] + slb_ref[...]).astype(BF)
        mask = _sgu_mask()
        for hd in range(NBR):
            wm = jnp.where(mask, gw_ref[hd], 0.0).astype(BF)
            for blk in range(ts // GW):
                rows = slice(blk * GW, (blk + 1) * GW)
                cols = slice(hd * GW, (hd + 1) * GW)
                zz = jnp.dot(wm, vn[rows, cols], preferred_element_type=F32) + gb_ref[:, hd:hd + 1]
                y_ref[rows, 3 * BW + hd * GW:3 * BW + (hd + 1) * GW] = (u[rows, cols] * zz).astype(BF)

    return _pc(
        jobs, body, name="mix_branches_fwd", grid=(s_len // ts,),
        in_specs=_halo_specs(ts, MIXC, s_len, False) + _branch_weights_specs(),
        out_specs=[_rows(ts, NBR * BW), _rows(ts, BW)],
        out_shape=[jax.ShapeDtypeStruct((s_len, NBR * BW), BF), jax.ShapeDtypeStruct((s_len, BW), BF)],
        scratch_shapes=[pltpu.VMEM((ext, MIXC), BF)],
        compiler_params=_params(1),
    )(p, p, pool_w, pool_scale, sconv_w, cconv_w, cln_g, cln_b, sln_g, sln_b, sgu_w, sgu_bt)


def _mix_branches_bwd(p, dy, z, dp, pool_w, pool_scale, sconv_w, cconv_w, cln_g, cln_b, sln_g, sln_b, sgu_w, sgu_bt,
                      jobs=()):
    s_len = p.shape[0]
    ts = min(BRANCH_TILE, s_len)
    ext = ts + 2 * HALO
    last = s_len // ts - 1
    tile = slice(HALO, HALO + ts)
    small_shapes = [(NBR, GW, GW), (1, BW), (SKP, BW), (CKP, BW), (1, BW), (1, BW), (1, BW), (1, BW), (NBR, GW, GW),
                    (NBR, GW, GW)]

    def body(pp_ref, pc_ref, pn_ref, dyp_ref, dyc_ref, dyn_ref, zc_ref, zn_ref, dpin_ref,
             pw_ref, ps_ref, sw_ref, cw_ref, clg_ref, clb_ref, slg_ref, slb_ref, gw_ref, gb_ref,
             dp_ref, dpw_ref, dps_ref, dsw_ref, dcw_ref, dclg_ref, dclb_ref, dslg_ref, dslb_ref, dgw_ref, dgb_ref,
             pe_ref, de_ref):
        del dyp_ref, dpin_ref
        i = pl.program_id(0)
        first = i == 0
        _assemble(pe_ref, pp_ref, pc_ref, pn_ref, i, last, ts)
        de_ref[0:HALO, :] = jnp.zeros((HALO, NBR * BW), BF)
        de_ref[HALO:HALO + ts, :] = dyc_ref[...]
        de_ref[HALO + ts:, :] = jnp.where(i < last, dyn_ref[...], jnp.zeros_like(dyn_ref))
        t = i * ts - HALO + lax.broadcasted_iota(jnp.int32, (ext, 1), 0)

        @pl.when(first)
        def _():
            dsw_ref[...] = jnp.zeros((SKP, BW), F32)
            dcw_ref[...] = jnp.zeros((CKP, BW), F32)

        dgs = _pool_diff(pe_ref[:, 0:BW].astype(F32), t, 1)
        dya = de_ref[:, 0:BW].astype(F32)
        dds = []
        for gi in range(NBR):
            cols = slice(gi * GW, (gi + 1) * GW)
            pw = pw_ref[gi].astype(BF)
            d_t = dgs[gi][tile].astype(BF)
            e = jnp.dot(d_t, pw, preferred_element_type=F32)
            _accumulate(dps_ref.at[:, cols], jnp.sum(dya[tile, cols] * e, axis=0, keepdims=True), first)
            de_g = (dya[:, cols] * ps_ref[:, cols]).astype(BF)
            _accumulate(dpw_ref.at[gi], lax.dot_general(d_t, de_g[tile], TN, preferred_element_type=F32), first)
            dds.append(lax.dot_general(de_g, pw, NT, preferred_element_type=F32))
        das = _pool_diff(jnp.concatenate(dds, axis=1), t, -1)
        for gi in range(NBR):
            dp_ref[:, gi * GW:(gi + 1) * GW] = das[gi][tile].astype(BF)

        xin = pe_ref[:, BW:2 * BW].astype(F32)
        bg = pe_ref[:, 2 * BW:3 * BW].astype(F32)
        cg = pe_ref[:, 3 * BW:4 * BW].astype(F32)
        q = cg * xin
        qs = [q, _shift(q, 1), _shift(q, 2)]
        cv = sw_ref[2:3, :] * qs[0] + sw_ref[1:2, :] * qs[1] + sw_ref[0:1, :] * qs[2]
        dyb = de_ref[:, BW:2 * BW].astype(F32)
        dcv = dyb * bg
        for j in range(SK):
            dsw_ref[SK - 1 - j:SK - j, :] += jnp.sum(dcv[tile] * qs[j][tile], axis=0, keepdims=True)
        dq = sw_ref[2:3, :] * dcv + sw_ref[1:2, :] * _shift(dcv, -1) + sw_ref[0:1, :] * _shift(dcv, -2)
        dp_ref[:, BW:2 * BW] = (dq * cg)[tile].astype(BF)
        dp_ref[:, 2 * BW:3 * BW] = (dyb * cv)[tile].astype(BF)
        dp_ref[:, 3 * BW:4 * BW] = (dq * xin)[tile].astype(BF)

        ca = pc_ref[:, 4 * BW:5 * BW].astype(F32)
        sb = _sig(pc_ref[:, 5 * BW:6 * BW].astype(F32))
        yg_t = ca * sb
        z = jnp.concatenate([zc_ref[...].astype(F32), zn_ref[...].astype(F32)], axis=0)
        rs, zn = _ln_stats(z)
        nn = zn * clg_ref[...] + clb_ref[...]
        sn = _sig(nn)
        dn = de_ref[HALO:, 2 * BW:3 * BW].astype(F32) * (sn * (1.0 + nn * (1.0 - sn)))
        _accumulate(dclg_ref, jnp.sum((dn * zn)[:ts], axis=0, keepdims=True), first)
        _accumulate(dclb_ref, jnp.sum(dn[:ts], axis=0, keepdims=True), first)
        dz = _ln_bwd(rs, zn, dn * clg_ref[...])
        dyg = cw_ref[CK - 1:CK, :] * dz[:ts]
        dcw_ref[CK - 1:CK, :] += jnp.sum(dz[:ts] * yg_t, axis=0, keepdims=True)
        for j in range(1, CK):
            dz_ahead = _shift(dz, -j)[:ts]
            dyg = dyg + cw_ref[CK - 1 - j:CK - j, :] * dz_ahead
            dcw_ref[CK - 1 - j:CK - j, :] += jnp.sum(dz_ahead * yg_t, axis=0, keepdims=True)
        dp_ref[:, 4 * BW:5 * BW] = (dyg * sb).astype(BF)
        dp_ref[:, 5 * BW:6 * BW] = (dyg * ca * sb * (1.0 - sb)).astype(BF)

        pu = pc_ref[:, 6 * BW:7 * BW].astype(F32)
        pv = pc_ref[:, 7 * BW:8 * BW].astype(F32)
        u, thu = _gelu(pu)
        v, thv = _gelu(pv)
        vrs, vn0 = _ln_stats(v)
        vn = (vn0 * slg_ref[...] + slb_ref[...]).astype(BF)
        dyd = dyc_ref[:, 3 * BW:4 * BW].astype(F32)
        dzz = dyd * u
        dzb = dzz.astype(BF)
        mask = _sgu_mask()
        dvn_cols = []
        for hd in range(NBR):
            cols = slice(hd * GW, (hd + 1) * GW)
            wm = jnp.where(mask, gw_ref[hd], 0.0).astype(BF)
            dwm = jnp.zeros((GW, GW), F32)
            dbs = jnp.zeros((GW, GW), F32)
            dvn_rows = []
            for blk in range(ts // GW):
                rows = slice(blk * GW, (blk + 1) * GW)
                zz = jnp.dot(wm, vn[rows, cols], preferred_element_type=F32) + gb_ref[:, hd:hd + 1]
                dp_ref[rows, 6 * BW + hd * GW:6 * BW + (hd + 1) * GW] = (
                    dyd[rows, cols] * zz * _gelu_grad(pu[rows, cols], thu[rows, cols])).astype(BF)
                dwm = dwm + lax.dot_general(dzb[rows, cols], vn[rows, cols], NT, preferred_element_type=F32)
                dbs = dbs + dzz[rows, cols]
                dvn_rows.append(lax.dot_general(wm, dzb[rows, cols], TN, preferred_element_type=F32))
            _accumulate(dgw_ref.at[hd], jnp.where(mask, dwm, 0.0), first)
            _accumulate(dgb_ref.at[hd], dbs, first)
            dvn_cols.append(jnp.concatenate(dvn_rows, axis=0))
        dvn = jnp.concatenate(dvn_cols, axis=1)
        _accumulate(dslg_ref, jnp.sum(dvn * vn0, axis=0, keepdims=True), first)
        _accumulate(dslb_ref, jnp.sum(dvn, axis=0, keepdims=True), first)
        dv = _ln_bwd(vrs, vn0, dvn * slg_ref[...])
        dp_ref[:, 7 * BW:8 * BW] = (dv * _gelu_grad(pv, thv)).astype(BF)

    const = lambda shp: pl.BlockSpec(shp, lambda i: (0,) * len(shp))
    return _pc(
        jobs, body, name="mix_branches_bwd", grid=(s_len // ts,),
        in_specs=(_halo_specs(ts, MIXC, s_len, True) + _halo_specs(ts, NBR * BW, s_len, True)
                  + _halo_specs(ts, BW, s_len, True)[1:] + [pl.BlockSpec(memory_space=pl.ANY)] + _branch_weights_specs()),
        out_specs=[pl.BlockSpec((ts, MIXC), lambda i: (i, 0))] + [const(s) for s in small_shapes],
        out_shape=[jax.ShapeDtypeStruct((s_len, INC), BF)] + [jax.ShapeDtypeStruct(s, F32) for s in small_shapes],
        scratch_shapes=[pltpu.VMEM((ext, MIXC), BF), pltpu.VMEM((ext, NBR * BW), BF)],
        input_output_aliases={8: 0},
        compiler_params=_params(1),
    )(p, p, p, dy, dy, dy, z, z, dp, pool_w, pool_scale, sconv_w, cconv_w, cln_g, cln_b, sln_g, sln_b, sgu_w, sgu_bt)


def _mix_out_fwd(x, y, p, w_up, w_out, jobs=()):
    s_len = x.shape[0]
    ts = min(TOKEN_TILE if jobs else 2 * TOKEN_TILE, s_len)

    def body(x_ref, y_ref, pg_ref, wu_ref, wo_ref, xo_ref, m_ref, up_ref):
        m = jnp.zeros((ts, D), F32)
        for gi in range(NBR):
            up = jnp.dot(y_ref[:, gi * BW:(gi + 1) * BW], wu_ref[gi], preferred_element_type=F32)
            up_ref[:, gi * D:(gi + 1) * D] = up.astype(BF)
            m = m + _sig(pg_ref[:, gi * D:(gi + 1) * D].astype(F32)) * up
        mb = m.astype(BF)
        m_ref[...] = mb
        xo_ref[...] = x_ref[...] + jnp.dot(mb, wo_ref[...], preferred_element_type=F32)

    return _pc(
        jobs, body, name="mix_out_fwd", grid=(s_len // ts,),
        in_specs=[_rows(ts, D), _rows(ts, NBR * BW), _rows(ts, NBR * D, 1), _full((NBR, BW, D)), _full((D, D))],
        out_specs=[_rows(ts, D), _rows(ts, D), _rows(ts, NBR * D)],
        out_shape=[jax.ShapeDtypeStruct((s_len, D), F32), jax.ShapeDtypeStruct((s_len, D), BF),
                   jax.ShapeDtypeStruct((s_len, NBR * D), BF)],
        compiler_params=_params(1),
    )(x, y, p, w_up, w_out)


def _mix_out_bwd(dxo, up, p, w_up, w_out, jobs=()):
    s_len = dxo.shape[0]
    ts = min(TOKEN_TILE, s_len)
    n_steps = s_len // ts

    def body(dxo_ref, up_hbm, p_hbm, wu_ref, wo_ref, dy_ref, dp_ref, dup_ref, dxb_ref, up_buf, pg_buf, sems):
        i = pl.program_id(0)

        def fetch(step):
            slot = step % 3
            rows = pl.ds(pl.multiple_of(step * ts, ts), ts)
            return (pltpu.make_async_copy(up_hbm.at[rows, :], up_buf.at[slot], sems.at[0, slot]),
                    pltpu.make_async_copy(p_hbm.at[rows, pl.ds(MIXC, NBR * D)], pg_buf.at[slot], sems.at[1, slot]))

        @pl.when(i == 0)
        def _():
            for step in range(min(2, n_steps)):
                for cp in fetch(step):
                    cp.start()

        @pl.when(i + 2 < n_steps)
        def _():
            for cp in fetch(i + 2):
                cp.start()

        for cp in fetch(i):
            cp.wait()
        slot = i % 3
        dxb = dxo_ref[...].astype(BF)
        dxb_ref[...] = dxb
        dm = lax.dot_general(dxb, wo_ref[...], NT, preferred_element_type=F32)
        for gi in range(NBR):
            cols = slice(gi * D, (gi + 1) * D)
            gate = _sig(pg_buf[slot, :, cols].astype(F32))
            dp_ref[:, cols] = (dm * up_buf[slot, :, cols].astype(F32) * gate * (1.0 - gate)).astype(BF)
            dup = (dm * gate).astype(BF)
            dup_ref[:, cols] = dup
            dy_ref[:, gi * BW:(gi + 1) * BW] = lax.dot_general(
                dup, wu_ref[gi], NT, preferred_element_type=F32).astype(BF)

    return _pc(
        jobs, body, name="mix_out_bwd", grid=(n_steps,),
        in_specs=[_rows(ts, D), ANY, ANY, _full((NBR, BW, D)), _full((D, D))],
        out_specs=[_rows(ts, NBR * BW), _rows(ts, NBR * D, 1), _rows(ts, NBR * D), _rows(ts, D)],
        out_shape=[jax.ShapeDtypeStruct((s_len, NBR * BW), BF), jax.ShapeDtypeStruct((s_len, INC), BF),
                   jax.ShapeDtypeStruct((s_len, NBR * D), BF), jax.ShapeDtypeStruct((s_len, D), BF)],
        scratch_shapes=[pltpu.VMEM((3, ts, NBR * D), BF), pltpu.VMEM((3, ts, NBR * D), BF),
                        pltpu.SemaphoreType.DMA((2, 3))],
        compiler_params=_params(1),
    )(dxo, up, p, w_up, w_out)


def _mix_in_bwd(dp, w_in, x, g, dxo, jobs=()):
    s_len = x.shape[0]
    ts = min(TOKEN_TILE, s_len)

    def body(dp_ref, w_ref, x_ref, g_ref, dxo_ref, dxi_ref, dg_ref):
        i = pl.program_id(0)
        dh = lax.dot_general(dp_ref[...], w_ref[...], NT, preferred_element_type=F32)
        dx, dg = _rms_bwd(x_ref[...], g_ref[...], dh)
        dxi_ref[...] = dxo_ref[...] + dx
        _accumulate(dg_ref, dg, i == 0)

    return _pc(
        jobs, body, name="mix_in_bwd", grid=(s_len // ts,),
        in_specs=[_rows(ts, INC), _full((D, INC)), _rows(ts, D), _full((1, D)), _rows(ts, D)],
        out_specs=[_rows(ts, D), pl.BlockSpec((1, D), lambda i: (0, 0))],
        out_shape=[jax.ShapeDtypeStruct((s_len, D), F32), jax.ShapeDtypeStruct((1, D), F32)],
        compiler_params=_params(1),
    )(dp, w_in, x, g, dxo)


def _loss_head(x, g, target, jobs=()):
    s_len = x.shape[0]
    ts = min(512, s_len)

    def body(x_ref, g_ref, t_ref, dx_ref, dg_ref, loss_ref):
        i = pl.program_id(0)
        xv = x_ref[...]
        _, xh = _rms_stats(xv)
        err = xh * g_ref[...] - t_ref[...]
        part = 0.5 * jnp.sum(jnp.mean(err * err, axis=-1, keepdims=True), axis=0, keepdims=True)
        dx, dg = _rms_bwd(xv, g_ref[...], err * (1.0 / D))
        dx_ref[...] = dx
        _accumulate(dg_ref, dg, i == 0)
        _accumulate(loss_ref, jnp.broadcast_to(part, (1, GW)), i == 0)

    return _pc(
        jobs, body, name="loss_head", grid=(s_len // ts,),
        in_specs=[_rows(ts, D), _full((1, D)), _rows(ts, D)],
        out_specs=[_rows(ts, D), pl.BlockSpec((1, D), lambda i: (0, 0)), pl.BlockSpec((1, GW), lambda i: (0, 0))],
        out_shape=[jax.ShapeDtypeStruct((s_len, D), F32), jax.ShapeDtypeStruct((1, D), F32),
                   jax.ShapeDtypeStruct((1, GW), F32)],
        compiler_params=_params(1),
    )(x, g, target)


SUM_TILE = 1 << 20
ADAM_TILE = 1 << 19
CAST_TILE = 1 << 19


def _row_tile(rows, cols, budget=1 << 18):
    tr = rows
    while tr * cols > budget and tr % 16 == 0:
        tr //= 2
    return tr


def _elementwise(fn, name, ins, out_dtypes):
    rows, cols = ins[0].shape
    tr = _row_tile(rows, cols)
    n_in = len(ins)

    def body(*refs):
        res = fn(*[r[...] for r in refs[:n_in]])
        for o_ref, val in zip(refs[n_in:], res):
            o_ref[...] = val.astype(o_ref.dtype)

    outs = pl.pallas_call(
        body, name=name, grid=(rows // tr,),
        in_specs=[_rows(tr, cols)] * n_in, out_specs=[_rows(tr, cols)] * len(out_dtypes),
        out_shape=[jax.ShapeDtypeStruct((rows, cols), dt) for dt in out_dtypes],
        compiler_params=_params(1),
    )(*ins)
    return outs


def _tiled(fn, name, grid, in_specs, out_specs, out_shape, args, scalars=None, alias=None):
    alias = alias or {}
    n_in = len(in_specs) - len(alias)
    n_pre = 0 if scalars is None else 1

    def body(*refs):
        refs = refs[n_pre:]
        res = fn(*[r[...] for r in refs[:n_in]])
        for o_ref, val in zip(refs[len(in_specs):], res):
            o_ref[...] = val.astype(o_ref.dtype)

    aliases = {n_pre + pos: out for pos, out in alias.items()}
    if scalars is None:
        return pl.pallas_call(body, name=name, grid=grid, in_specs=in_specs, out_specs=out_specs, out_shape=out_shape,
                              input_output_aliases=aliases, compiler_params=_params(len(grid)))(*args)
    spec = pltpu.PrefetchScalarGridSpec(num_scalar_prefetch=1, grid=grid, in_specs=in_specs, out_specs=out_specs)
    return pl.pallas_call(body, name=name, grid_spec=spec, out_shape=out_shape, input_output_aliases=aliases,
                          compiler_params=_params(len(grid)))(scalars, *args)


def _cast_into(shard, layer, col, dtype, sc):
    ks, ns = shard.shape[1:]
    tr = _row_tile(ks, ns, SUM_TILE)
    full = (ks, ns * NCHIP) if col else (ks * NCHIP, ns)
    out_idx = (lambda i, s: (i, s[1])) if col else (lambda i, s: (s[1] * (ks // tr) + i, 0))
    return _tiled(lambda v: (v,), "cast_into", (ks // tr,), [pl.BlockSpec((None, tr, ns), lambda i, s: (layer, i, 0))],
                  [pl.BlockSpec((tr, ns), out_idx)], [jax.ShapeDtypeStruct(full, dtype)], [shard], sc)[0]


def _cast_rest(items, jobs=()):
    n = len(items)
    tiles, slot_sets, counts = [], {}, {}
    for k, (shard, _, _, dt) in enumerate(items):
        ks, ns = shard.shape[1:]
        tr = _row_tile(ks, ns, CAST_TILE)
        which = slot_sets.setdefault((tr, ns, shard.dtype, dt), len(slot_sets))
        for t in range(ks // tr):
            tiles.append((k, t, tr, which, counts.get(which, 0) % 2))
            counts[which] = counts.get(which, 0) + 1
    n_sets = len(slot_sets)

    def body(*refs):
        ins, outs = refs[:n], refs[n:2 * n]
        bufs = refs[2 * n:2 * n + 2 * n_sets]
        in_sems, out_sems = refs[2 * n + 2 * n_sets:]
        chip = 2 * lax.axis_index("x") + lax.axis_index("y")
        fetch, store = [], []
        for k, t, tr, which, slot in tiles:
            shard, layer, col, _ = items[k]
            ks, ns = shard.shape[1:]
            if col:
                place = outs[k].at[pl.ds(t * tr, tr), pl.ds(pl.multiple_of(chip * ns, GW), ns)]
            else:
                place = outs[k].at[pl.ds(pl.multiple_of(chip * ks + t * tr, 16), tr), :]
            fetch.append(pltpu.make_async_copy(
                ins[k].at[layer, pl.ds(t * tr, tr), :], bufs[2 * which].at[slot], in_sems.at[which, slot]))
            store.append(pltpu.make_async_copy(bufs[2 * which + 1].at[slot], place, out_sems.at[which, slot]))
        busy = {}
        fetch[0].start()
        for i, (k, t, tr, which, slot) in enumerate(tiles):
            if i + 1 < len(tiles):
                fetch[i + 1].start()
            fetch[i].wait()
            if (which, slot) in busy:
                store[busy[which, slot]].wait()
            bufs[2 * which + 1][slot] = bufs[2 * which][slot].astype(bufs[2 * which + 1].dtype)
            store[i].start()
            busy[which, slot] = i
        for i in busy.values():
            store[i].wait()

    def full_shape(shard, col):
        ks, ns = shard.shape[1:]
        return (ks, ns * NCHIP) if col else (ks * NCHIP, ns)

    scratch = []
    for (tr, ns, dt_in, dt_out) in slot_sets:
        scratch += [pltpu.VMEM((2, tr, ns), dt_in), pltpu.VMEM((2, tr, ns), dt_out)]
    scratch += [pltpu.SemaphoreType.DMA((n_sets, 2)), pltpu.SemaphoreType.DMA((n_sets, 2))]
    return _pc(
        jobs, body, name="cast_rest", in_specs=[ANY] * n, out_specs=[ANY] * n,
        out_shape=[jax.ShapeDtypeStruct(full_shape(shard, col), dt) for shard, _, col, dt in items],
        scratch_shapes=scratch, compiler_params=pltpu.CompilerParams(vmem_limit_bytes=VMEM_LIMIT),
    )(*[shard for shard, _, _, _ in items])


def _pair_sum(g, got, col, sc):
    hk, hn = got.shape
    tr = _row_tile(hk, hn, SUM_TILE)
    g_idx = (lambda i, s: (s[0] * (hk // tr) + i, 0)) if col else (lambda i, s: (i, s[0]))
    plain = pl.BlockSpec((tr, hn), lambda i, s: (i, 0))
    return _tiled(lambda a, b: (a.astype(F32) + b.astype(F32),), "pair_sum", (hk // tr,),
                  [pl.BlockSpec((tr, hn), g_idx), plain], [plain], [jax.ShapeDtypeStruct((hk, hn), BF)], [g, got], sc)[0]


def _chip_sum(ph, got, col, layer, depth, sc, carry):
    qk, qn = got.shape[1:]
    tr = _row_tile(qk, qn, SUM_TILE)
    ph_idx = (lambda i, s: (i, s[1])) if col else (lambda i, s: (s[1] * (qk // tr) + i, 0))
    out_shape = (depth, 2 * qk, qn) if col else (depth, qk, 2 * qn)
    out_idx = (lambda i, s: (layer, s[0] * (qk // tr) + i, 0)) if col else (lambda i, s: (layer, i, s[0]))
    in_specs = [pl.BlockSpec((tr, qn), ph_idx)] + [pl.BlockSpec((None, tr, qn), lambda i, s, j=j: (j, i, 0)) for j in range(3)]
    args = [ph, got, got, got]
    if carry is not None:
        in_specs.append(ANY)
        args.append(carry)
    return _tiled(lambda a, b, c_, d_: (a.astype(F32) + b.astype(F32) + c_.astype(F32) + d_.astype(F32),), "chip_sum",
                  (qk // tr,), in_specs, [pl.BlockSpec((None, tr, qn), out_idx)],
                  [jax.ShapeDtypeStruct(out_shape, F32)], args, sc, alias=None if carry is None else {4: 0})[0]


def _adamw_math(w, g, m, v):
    m = ADAM_B1 * m + (1.0 - ADAM_B1) * g
    v = ADAM_B2 * v + (1.0 - ADAM_B2) * (g * g)
    m_hat = m / (1.0 - ADAM_B1 ** ADAM_STEP)
    v_hat = v / (1.0 - ADAM_B2 ** ADAM_STEP)
    delta = -ADAM_LR * (m_hat / (jnp.sqrt(v_hat) + ADAM_EPS) + ADAM_WD * w)
    return delta, m, v


def _adamw_layer(w, g, m, v, layer, carry):
    k, n = w.shape[1:]
    tr = _row_tile(k, n, ADAM_TILE)
    blk = pl.BlockSpec((None, tr, n), lambda i: (layer, i, 0))
    carry = list(carry or [])
    return _tiled(lambda *a: (*_adamw_math(*a), a[1]), "adamw_layer", (k // tr,), [blk] * 4 + [ANY] * len(carry),
                  [blk] * 4, [jax.ShapeDtypeStruct(w.shape, F32)] * 4, [w, g, m, v] + carry, None,
                  alias={4 + pos: pos for pos in range(len(carry))})


def _adamw(w, g, m, v):
    shape = w.shape
    two_d = lambda a: a.reshape(-1, shape[-1])
    outs = _elementwise(_adamw_math, "adamw", [two_d(w), two_d(g), two_d(m), two_d(v)], [F32, F32, F32])
    return [o.reshape(shape) for o in outs]


ANY = pl.BlockSpec(memory_space=pl.ANY)


def _place():
    x, y, c = lax.axis_index("x"), lax.axis_index("y"), lax.axis_index("c")
    chips = [(1 - x, y), (x, 1 - y), (1 - x, 1 - y)]
    return x, y, c, chips


def _cols(ref, start, size):
    idx = (slice(None),) * (len(ref.shape) - 1) + (pl.ds(pl.multiple_of(start, GW), size),)
    return ref.at[idx]


def _rows_of(ref, start, size):
    nd = len(ref.shape)
    idx = (slice(None),) * (nd - 2) + (pl.ds(pl.multiple_of(start, 16), size), slice(None))
    return ref.at[idx]


def _region(ref, col_sharded, chip, half):
    k, n = ref.shape
    align = 16 if ref.dtype == BF else 8
    if col_sharded:
        return ref.at[pl.ds(pl.multiple_of(half * (k // 2), align), k // 2),
                      pl.ds(pl.multiple_of(chip * (n // NCHIP), GW), n // NCHIP)]
    rows = k // (2 * NCHIP)
    return ref.at[pl.ds(pl.multiple_of((2 * chip + half) * rows, align), rows), :]


def _job_gather(bufs, col_sharded, handoff):
    n = len(bufs)

    def copies(outs, send_sems, recv_sems, stage):
        x, y, c, chips = _place()
        sends, lands = [], []
        for k in range(n):
            for j, chip in enumerate(chips):
                theirs = 2 * chip[0] + chip[1]
                if stage == 0:
                    src, to = _region(outs[k], col_sharded[k], 2 * x + y, c), (*chip, c)
                    land = _region(outs[k], col_sharded[k], theirs, c)
                else:
                    src, to = _region(outs[k], col_sharded[k], theirs, c), (x, y, 1 - c)
                    land = _region(outs[k], col_sharded[k], theirs, 1 - c)
                sem = 3 * n * stage + 3 * k + j
                sems = dict(send_sem=send_sems.at[sem], recv_sem=recv_sems.at[sem], device_id=to, device_id_type=MESH)
                sends.append(pltpu.make_async_remote_copy(src_ref=src, dst_ref=src, **sems))
                lands.append(pltpu.make_async_remote_copy(src_ref=land, dst_ref=land, **sems))
        return sends, lands

    def start(ins, outs, send_sems, recv_sems):
        for cp in copies(outs, send_sems, recv_sems, 0)[0]:
            cp.start()

    def hand_on(ins, outs, send_sems, recv_sems):
        for cp in copies(outs, send_sems, recv_sems, 0)[1]:
            cp.wait_recv()
        for cp in copies(outs, send_sems, recv_sems, 1)[0]:
            cp.start()

    def finish(ins, outs, send_sems, recv_sems):
        sends, lands = copies(outs, send_sems, recv_sems, 1)
        for cp in lands:
            cp.wait_recv()
        for cp in copies(outs, send_sems, recv_sems, 0)[0] + sends:
            cp.wait_send()

    return _Job(bufs, n, [], 6 * n, [(0.0, start), (handoff, hand_on), (1.0, finish)])


def _half(ref, col_sharded, c):
    k, n = ref.shape[-2:]
    return _rows_of(ref, c * (k // 2), k // 2) if col_sharded else _cols(ref, c * (n // 2), n // 2)


def _quarter(ref, col_sharded, j):
    k, n = ref.shape[-2:]
    return _cols(ref, j * (n // NCHIP), n // NCHIP) if col_sharded else _rows_of(ref, j * (k // NCHIP), k // NCHIP)


def _job_pair(grads, col_sharded):
    n = len(grads)

    def half_shape(g, col):
        return (g.shape[0] // 2, g.shape[1]) if col else (g.shape[0], g.shape[1] // 2)

    def copies(ins, got, send_sems, recv_sems):
        x, y, c, _ = _place()
        return [pltpu.make_async_remote_copy(
            src_ref=_half(ins[k], col_sharded[k], 1 - c), dst_ref=got[k], send_sem=send_sems.at[k],
            recv_sem=recv_sems.at[k], device_id=(x, y, 1 - c), device_id_type=MESH) for k in range(n)]

    def start(*refs):
        for cp in copies(*refs):
            cp.start()

    def finish(*refs):
        for cp in copies(*refs):
            cp.wait()

    fresh = [jax.ShapeDtypeStruct(half_shape(g, col), g.dtype) for g, col in zip(grads, col_sharded)]
    return _Job(grads, 0, fresh, n, [(0.0, start), (1.0, finish)])


def _job_chip(halves, col_sharded):
    n = len(halves)

    def quarter_shape(h, col):
        return (3, h.shape[0], h.shape[1] // NCHIP) if col else (3, h.shape[0] // NCHIP, h.shape[1])

    def copies(ins, got, send_sems, recv_sems):
        x, y, c, chips = _place()
        return [pltpu.make_async_remote_copy(
            src_ref=_quarter(ins[k], col_sharded[k], 2 * chip[0] + chip[1]), dst_ref=got[k].at[j],
            send_sem=send_sems.at[3 * k + j], recv_sem=recv_sems.at[3 * k + j], device_id=(*chip, c), device_id_type=MESH)
            for k in range(n) for j, chip in enumerate(chips)]

    def start(*refs):
        for cp in copies(*refs):
            cp.start()

    def finish(*refs):
        for cp in copies(*refs):
            cp.wait()

    fresh = [jax.ShapeDtypeStruct(quarter_shape(h, col), h.dtype) for h, col in zip(halves, col_sharded)]
    return _Job(halves, 0, fresh, 3 * n, [(0.0, start), (1.0, finish)])


def _job_sibling(shards, col_sharded, layers):
    n = len(shards)

    def copies(outs, send_sems, recv_sems):
        x, y, c, _ = _place()
        sends, lands = [], []
        for k in range(n):
            sems = dict(send_sem=send_sems.at[k], recv_sem=recv_sems.at[k], device_id=(x, y, 1 - c), device_id_type=MESH)
            mine = _half(outs[k].at[layers[k]], col_sharded[k], c)
            theirs = _half(outs[k].at[layers[k]], col_sharded[k], 1 - c)
            sends.append(pltpu.make_async_remote_copy(src_ref=mine, dst_ref=mine, **sems))
            lands.append(pltpu.make_async_remote_copy(src_ref=theirs, dst_ref=theirs, **sems))
        return sends, lands

    def start(ins, outs, send_sems, recv_sems):
        for cp in copies(outs, send_sems, recv_sems)[0]:
            cp.start()

    def finish(ins, outs, send_sems, recv_sems):
        sends, lands = copies(outs, send_sems, recv_sems)
        for cp in lands:
            cp.wait_recv()
        for cp in sends:
            cp.wait_send()

    return _Job(shards, n, [], n, [(0.0, start), (1.0, finish)])


def _standalone(jobs, name):
    return _pc(jobs, lambda: None, name=name, in_specs=[], out_specs=[], out_shape=[])()[1]


def _all_reduce_small(buf, jobs=()):
    rows = buf.shape[0]
    per, half = rows // NDEV, rows // 2

    def body(in_ref, out_ref, got_sib, pair_ref, got_chip, send_sems, recv_sems):
        x, y, c, chips = _place()
        sibling = (x, y, 1 - c)

        def part(ref, start, size):
            return ref.at[pl.ds(pl.multiple_of(start, 8), size), :]

        def copy(src, dst, sem, to):
            return pltpu.make_async_remote_copy(src_ref=src, dst_ref=dst, send_sem=send_sems.at[sem],
                                                recv_sem=recv_sems.at[sem], device_id=to, device_id_type=MESH)

        down = copy(part(in_ref, (1 - c) * half, half), got_sib, 0, sibling)
        down.start()
        down.wait()
        pair_ref[...] = part(in_ref, c * half, half)[...] + got_sib[...]
        scatter = [copy(part(pair_ref, (2 * chip[0] + chip[1]) * per, per), got_chip.at[k], 1 + k, (*chip, c))
                   for k, chip in enumerate(chips)]
        for cp in scatter:
            cp.start()
        for cp in scatter:
            cp.wait()
        mine = part(out_ref, c * half + (2 * x + y) * per, per)
        mine[...] = part(pair_ref, (2 * x + y) * per, per)[...] + got_chip[0] + got_chip[1] + got_chip[2]
        share = [copy(mine, mine, 4 + k, (*chip, c)) for k, chip in enumerate(chips)]
        for cp in share:
            cp.start()
        for k, chip in enumerate(chips):
            share[k].wait_send()
            theirs = part(out_ref, c * half + (2 * chip[0] + chip[1]) * per, per)
            copy(theirs, theirs, 4 + k, (*chip, c)).wait_recv()
        up = copy(part(out_ref, c * half, half), part(out_ref, c * half, half), 7, sibling)
        up.start()
        up.wait_send()
        other = part(out_ref, (1 - c) * half, half)
        copy(other, other, 7, sibling).wait_recv()

    vmem = pl.BlockSpec(memory_space=pltpu.VMEM)
    return _pc(
        jobs, body, name="all_reduce_small", in_specs=[vmem], out_specs=vmem,
        out_shape=jax.ShapeDtypeStruct((rows, GW), F32),
        scratch_shapes=[pltpu.VMEM((half, GW), F32), pltpu.VMEM((half, GW), F32), pltpu.VMEM((3, per, GW), F32),
                        pltpu.SemaphoreType.DMA((8,)), pltpu.SemaphoreType.DMA((8,))],
    )(buf)


BIG = ("ffn1_w13", "ffn1_w2", "w_in", "w_up", "w_out", "ffn2_w13", "ffn2_w2")
BIG_COL_SHARDED = (True, False, True, True, False, True, False)
SMALL = ("ffn1_norm", "mix_norm", "pool_w", "pool_scale", "sconv_w", "cconv_w", "cconv_ln_g", "cconv_ln_b",
         "sgu_ln_g", "sgu_ln_b", "sgu_w", "sgu_b", "ffn2_norm", "final_norm")
WEIGHTS = ("ffn1_norm", "ffn1_w13", "ffn1_w2", "mix_norm", "w_in", "pool_w", "pool_scale", "sconv_w", "cconv_w",
           "cconv_ln_g", "cconv_ln_b", "sgu_ln_g", "sgu_ln_b", "sgu_w", "sgu_b", "w_up", "w_out", "ffn2_norm",
           "ffn2_w13", "ffn2_w2", "final_norm")


def _pad_rows(a, rows):
    return jnp.pad(a, ((0, 0), (0, rows - a.shape[1]), (0, 0)))


def kernel(x, ffn1_norm, ffn1_w13, ffn1_w2, mix_norm, w_in, pool_w, pool_scale, sconv_w, cconv_w, cconv_ln_g, cconv_ln_b, sgu_ln_g, sgu_ln_b, sgu_w, sgu_b, w_up, w_out, ffn2_norm, ffn2_w13, ffn2_w2, final_norm, loss_target, m_ffn1_norm, m_ffn1_w13, m_ffn1_w2, m_mix_norm, m_w_in, m_pool_w, m_pool_scale, m_sconv_w, m_cconv_w, m_cconv_ln_g, m_cconv_ln_b, m_sgu_ln_g, m_sgu_ln_b, m_sgu_w, m_sgu_b, m_w_up, m_w_out, m_ffn2_norm, m_ffn2_w13, m_ffn2_w2, m_final_norm, v_ffn1_norm, v_ffn1_w13, v_ffn1_w2, v_mix_norm, v_w_in, v_pool_w, v_pool_scale, v_sconv_w, v_cconv_w, v_cconv_ln_g, v_cconv_ln_b, v_sgu_ln_g, v_sgu_ln_b, v_sgu_w, v_sgu_b, v_w_up, v_w_out, v_ffn2_norm, v_ffn2_w13, v_ffn2_w2, v_final_norm):
    args = dict(locals())
    w = {nm: args[nm] for nm in WEIGHTS}
    m = {nm: args["m_" + nm] for nm in WEIGHTS}
    v = {nm: args["v_" + nm] for nm in WEIGHTS}
    depth = ffn1_w13.shape[0]
    chip = 2 * lax.axis_index("x") + lax.axis_index("y")

    sc = jnp.stack([lax.axis_index("c"), chip]).astype(jnp.int32)
    col_of = dict(zip(BIG, BIG_COL_SHARDED), sconv_w=True, cconv_w=True)
    sources = {nm: (w[nm], BF) for nm in BIG}
    sources["w_up"] = (w_up.reshape(depth, NBR * BW, w_up.shape[-1]), BF)
    sources["sconv_w"] = (_pad_rows(sconv_w, 2 * SKP), F32)
    sources["cconv_w"] = (_pad_rows(cconv_w, 2 * CKP), F32)
    first_group = ("ffn1_w13", "ffn1_w2")
    full = [dict() for _ in range(depth)]
    for nm in first_group:
        full[0][nm] = _cast_into(sources[nm][0], 0, col_of[nm], sources[nm][1], sc)

    def gather_job(l, names, handoff):
        return _job_gather([full[l][nm] for nm in names], [col_of[nm] for nm in names], handoff)

    def gather_with(l, names, handoff, call):
        res, job_outs = call([gather_job(l, names, handoff)] if l < depth else [])
        if l < depth:
            full[l].update(zip(names, job_outs[0]))
        return res

    rest = [(nm, l) for l in range(depth) for nm in sources if not (l == 0 and nm in first_group)]
    casted = gather_with(0, first_group, 1.0, lambda jobs: _cast_rest(
        [(sources[nm][0], l, col_of[nm], sources[nm][1]) for nm, l in rest], jobs))
    for (nm, l), arr in zip(rest, casted):
        full[l][nm] = arr

    xs = x[0]
    row = lambda a: a.reshape(1, -1)
    saved = []
    for l in range(depth):
        g1, gm, g2 = row(ffn1_norm[l]), row(mix_norm[l]), row(ffn2_norm[l])
        x1, h1, s1, ab1 = gather_with(l, ("w_in",), 1.0, lambda jobs: _ffn_fwd(
            xs, g1, full[l]["ffn1_w13"], full[l]["ffn1_w2"], jobs))
        hm, p = gather_with(l, ("w_up", "w_out", "sconv_w", "cconv_w", "ffn2_w2"), 0.85, lambda jobs: _mix_in(
            x1, gm, full[l]["w_in"], jobs))
        branch = (pool_w[l], row(pool_scale[l]), full[l]["sconv_w"][:SKP], full[l]["cconv_w"][:CKP], row(cconv_ln_g[l]),
                  row(cconv_ln_b[l]), row(sgu_ln_g[l]), row(sgu_ln_b[l]), sgu_w[l], sgu_b[l].T)
        y, conv_z = gather_with(l, ("ffn2_w13",), 1.0, lambda jobs: _mix_branches_fwd(p, *branch, jobs=jobs))
        w_up_l = full[l]["w_up"].reshape(NBR, BW, D)
        x2, merged, up = gather_with(l + 1, ("ffn1_w2",), 0.8, lambda jobs: _mix_out_fwd(
            x1, y, p, w_up_l, full[l]["w_out"], jobs))
        x3, h2, s2, ab2 = gather_with(l + 1, ("ffn1_w13",), 1.0, lambda jobs: _ffn_fwd(
            x2, g2, full[l]["ffn2_w13"], full[l]["ffn2_w2"], jobs))
        lw = dict(g1=g1, gm=gm, g2=g2, w13a=full[l]["ffn1_w13"], w2a=full[l]["ffn1_w2"], w13b=full[l]["ffn2_w13"],
                  w2b=full[l]["ffn2_w2"], w_in=full[l]["w_in"], w_up=w_up_l, w_out=full[l]["w_out"], branch=branch)
        saved.append(dict(lw=lw, x0=xs, x1=x1, x2=x2, h1=h1, s1=s1, ab1=ab1, hm=hm, p=p, y=y, z=conv_z, merged=merged, up=up, h2=h2,
                          s2=s2, ab2=ab2))
        xs = x3

    (dx, d_final, loss_part), _ = _loss_head(xs, row(final_norm), loss_target[0])
    loss = lax.psum(loss_part[0, 0], ("x", "y", "c"))

    ici_us = dict(ffn1_w13=64, ffn1_w2=32, w_in=93, w_up=23, w_out=12, ffn2_w13=64, ffn2_w2=32)
    parts, pair_sums, reduced, big_updates, pending = {}, {}, {}, {}, []

    def take_jobs(budget_us):
        chosen = []
        for task in list(pending):
            kind, (nm, _) = task
            if kind == "chip":
                if ici_us[nm] > budget_us:
                    continue
                budget_us -= ici_us[nm]
            if kind == "sib" and any(k == "sib" and key[0] == nm for k, key in chosen):
                continue
            chosen.append(task)
            pending.remove(task)
        groups, jobs = [], []
        for kind in ("pair", "sib", "chip"):
            keys = [key for k, key in chosen if k == kind]
            if not keys:
                continue
            cols = [col_of[nm] for nm, _ in keys]
            groups.append((kind, keys))
            if kind == "pair":
                jobs.append(_job_pair([parts[key] for key in keys], cols))
            elif kind == "chip":
                jobs.append(_job_chip([pair_sums[key] for key in keys], cols))
            else:
                jobs.append(_job_sibling([reduced[nm] for nm, _ in keys], cols, [layer for _, layer in keys]))
        return groups, jobs

    def settle(groups, job_outs):
        for (kind, keys), outs in zip(groups, job_outs):
            for key, out in zip(keys, outs):
                nm, layer = key
                if kind == "pair":
                    pair_sums[key] = _pair_sum(parts[key], out, col_of[nm], sc)
                    pending.append(("chip", key))
                elif kind == "chip":
                    assert not any(k == "sib" and other[0] == nm for k, other in pending)
                    reduced[nm] = _chip_sum(pair_sums[key], out, col_of[nm], layer, depth, sc, reduced.get(nm))
                    pending.append(("sib", key))
                else:
                    reduced[nm] = out
                    as3 = lambda a: a.reshape(out.shape)
                    big_updates[nm] = _adamw_layer(as3(w[nm]), out, as3(m[nm]), as3(v[nm]), layer, big_updates.get(nm))

    def run(budget_us, call, carrier=True):
        groups, jobs = take_jobs(budget_us) if carrier else ([], [])
        res, job_outs = call(jobs)
        settle(groups, job_outs)
        return res

    def wgrad_done(key, partial):
        parts[key] = partial
        pending.append(("pair", key))

    small_parts = {nm: [None] * depth for nm in SMALL if nm != "final_norm"}
    for l in reversed(range(depth)):
        sv = saved[l]
        lw = sv["lw"]
        dx, dab, dyh, dg2 = run(100, lambda jobs: _ffn_bwd(
            dx, sv["x2"], lw["g2"], sv["ab2"], lw["w13b"], lw["w2b"], jobs))
        wgrad_done(("ffn2_w13", l), run(58, lambda jobs: _wgrad(sv["h2"], dab, D, 512, "wgrad_w13", jobs), False))
        wgrad_done(("ffn2_w2", l), run(33, lambda jobs: _wgrad(sv["s2"], dyh, 256, D, "wgrad_w2", jobs), False))
        small_parts["ffn2_norm"][l] = dg2

        dy, dp, dup, dxb = run(70, lambda jobs: _mix_out_bwd(dx, sv["up"], sv["p"], lw["w_up"], lw["w_out"], jobs))
        wgrad_done(("w_out", l), run(16, lambda jobs: _wgrad(sv["merged"], dxb, D, 512, "wgrad_w_out", jobs), False))
        wgrad_done(("w_up", l), run(25, lambda jobs: _wgrad_groups(sv["y"], dup, BW, D, "wgrad_w_up", jobs), False))
        (dp, d_pool_w, d_pool_scale, d_sconv, d_cconv, d_clg, d_clb, d_slg, d_slb, d_sgu_w, d_sgu_b) = run(
            130, lambda jobs: _mix_branches_bwd(sv["p"], dy, sv["z"], dp, *lw["branch"], jobs=jobs))
        wgrad_done(("w_in", l), run(80, lambda jobs: _wgrad(sv["hm"], dp, D, 512, "wgrad_w_in", jobs), False))
        dx, dgm = run(91, lambda jobs: _mix_in_bwd(dp, lw["w_in"], sv["x1"], lw["gm"], dx, jobs))
        small_parts["mix_norm"][l] = dgm
        small_parts["pool_w"][l] = d_pool_w
        small_parts["pool_scale"][l] = d_pool_scale
        small_parts["sconv_w"][l] = d_sconv[:SK]
        small_parts["cconv_w"][l] = d_cconv[:CK]
        small_parts["cconv_ln_g"][l] = d_clg
        small_parts["cconv_ln_b"][l] = d_clb
        small_parts["sgu_ln_g"][l] = d_slg
        small_parts["sgu_ln_b"][l] = d_slb
        small_parts["sgu_w"][l] = d_sgu_w
        small_parts["sgu_b"][l] = jnp.sum(d_sgu_b, axis=-1)

        dx, dab, dyh, dg1 = run(100, lambda jobs: _ffn_bwd(
            dx, sv["x0"], lw["g1"], sv["ab1"], lw["w13a"], lw["w2a"], jobs))
        wgrad_done(("ffn1_w2", l), run(33, lambda jobs: _wgrad(sv["s1"], dyh, 256, D, "wgrad_w2", jobs), l == 0))
        wgrad_done(("ffn1_w13", l), run(58, lambda jobs: _wgrad(sv["h1"], dab, D, 512, "wgrad_w13", jobs), l == 0))
        small_parts["ffn1_norm"][l] = dg1
    grad_x = dx[None]

    small_local = {nm: jnp.stack(parts).reshape(depth, *w[nm].shape[1:-1], -1) if nm not in ("sconv_w", "cconv_w")
                   else jnp.stack(parts) for nm, parts in small_parts.items()}
    small_local["final_norm"] = d_final.reshape(-1)
    sizes = [small_local[nm].size for nm in SMALL]
    total = sum(sizes)
    pad_to = NDEV * 8 * GW
    padded = -(-total // pad_to) * pad_to
    packed = jnp.concatenate([small_local[nm].reshape(-1) for nm in SMALL] + [jnp.zeros((padded - total,), F32)])
    groups, jobs = take_jobs(float("inf"))
    summed, job_outs = _all_reduce_small(packed.reshape(-1, GW), jobs)
    settle(groups, job_outs)
    summed = summed.reshape(-1)
    flushes = 0
    while pending:
        groups, jobs = take_jobs(float("inf"))
        settle(groups, _standalone(jobs, "grad_flush_%d" % flushes))
        flushes += 1
    big_grads = {nm: big_updates[nm][3].reshape(w[nm].shape) for nm in BIG}
    small_grads, off = {}, 0
    for nm, size in zip(SMALL, sizes):
        small_grads[nm] = summed[off:off + size].reshape(small_local[nm].shape)
        off += size
    for nm in ("sconv_w", "cconv_w"):
        small_grads[nm] = lax.dynamic_slice_in_dim(small_grads[nm], chip * GW, GW, axis=2)

    grads = {**big_grads, **small_grads}

    delta, new_m, new_v = {}, {}, {}
    for nm in BIG:
        delta[nm], new_m[nm], new_v[nm] = [a.reshape(w[nm].shape) for a in big_updates[nm][:3]]
    s_sizes = [w[nm].size for nm in SMALL]
    s_total = sum(s_sizes)
    s_padded = -(-s_total // (8 * GW)) * (8 * GW)

    def pack(tree):
        return jnp.concatenate([tree[nm].reshape(-1) for nm in SMALL] + [jnp.ones((s_padded - s_total,), F32)]).reshape(-1, GW)

    packed_out = _adamw(pack(w), pack(grads), pack(m), pack(v))
    off = 0
    for nm, size in zip(SMALL, s_sizes):
        for tree, arr in zip((delta, new_m, new_v), packed_out):
            tree[nm] = arr.reshape(-1)[off:off + size].reshape(w[nm].shape)
        off += size

    return (loss, grad_x, *[grads[nm] for nm in WEIGHTS], *[delta[nm] for nm in WEIGHTS],
            *[new_m[nm] for nm in WEIGHTS], *[new_v[nm] for nm in WEIGHTS])
```

```python
import jax
import jax.numpy as jnp
from jax import lax
from jax.experimental import pallas as pl
from jax.experimental.pallas import tpu as pltpu

D = 1024
FF = 2816
BW = 512
NBR = 4
MIXC = 4096
INC = 8192
GW = 128
CHUNK = 64
SK = 3
CK = 31
SKP = 8
CKP = 32
HALO = 32
TOKEN_TILE = 256
BRANCH_TILE = 512
EPS = 1e-6
NCHIP = 4
NDEV = 8

ADAM_LR = 0.001
ADAM_B1 = 0.9
ADAM_B2 = 0.999
ADAM_EPS = 1e-08
ADAM_WD = 0.01
ADAM_STEP = 10

VMEM_LIMIT = 56 * 1024 * 1024

BF = jnp.bfloat16
F32 = jnp.float32
MESH = pl.DeviceIdType.MESH
NT = (((1,), (1,)), ((), ()))
TN = (((0,), (0,)), ((), ()))


def _params(n_axes):
    return pltpu.CompilerParams(dimension_semantics=("arbitrary",) * n_axes, vmem_limit_bytes=VMEM_LIMIT)


class _Job:
    def __init__(self, args, n_inplace, fresh, nsem, phases):
        self.args, self.n_inplace, self.fresh, self.nsem, self.phases = list(args), n_inplace, list(fresh), nsem, phases


def _pc(jobs, body, name, in_specs, out_specs, out_shape, grid=(), scratch_shapes=(), input_output_aliases=None,
        compiler_params=None, grid_spec_scalars=None):
    single = not isinstance(out_shape, (list, tuple))
    core_out_specs = [out_specs] if single else list(out_specs)
    core_out_shape = [out_shape] if single else list(out_shape)
    n_in, n_out, n_scr = len(in_specs), len(core_out_specs), len(scratch_shapes)
    n_pre = 0 if grid_spec_scalars is None else 1
    all_in, all_out, all_shape = list(in_specs), list(core_out_specs), list(core_out_shape)
    all_scr, aliases, extra_args, layout = list(scratch_shapes), dict(input_output_aliases or {}), [], []
    for job in jobs:
        n_job_out = job.n_inplace + len(job.fresh)
        layout.append((len(all_in), len(job.args), len(all_out), n_job_out, len(all_scr)))
        for a in range(job.n_inplace):
            aliases[n_pre + len(all_in) + a] = len(all_out) + a
        all_in += [ANY] * len(job.args)
        extra_args += job.args
        all_out += [ANY] * n_job_out
        all_shape += [jax.ShapeDtypeStruct(a.shape, a.dtype) for a in job.args[:job.n_inplace]] + job.fresh
        all_scr += [pltpu.SemaphoreType.DMA((job.nsem,)), pltpu.SemaphoreType.DMA((job.nsem,))]
    steps = 1
    for extent in grid:
        steps *= extent
    events = []
    for (i0, na, o0, no, s0), job in zip(layout, jobs):
        for frac, fn in job.phases:
            events.append((min(int(frac * steps), steps - 1), frac >= 1.0, len(events), fn, (i0, na, o0, no, s0)))
    events.sort(key=lambda e: e[:3])

    def wrapped(*refs):
        pre, refs = refs[:n_pre], refs[n_pre:]
        ins, outs, scr = refs[:len(all_in)], refs[len(all_in):len(all_in) + len(all_out)], refs[len(all_in) + len(all_out):]
        step = 0
        for axis, extent in enumerate(grid):
            step = step * extent + pl.program_id(axis)

        def emit(event):
            at, _, _, fn, (i0, na, o0, no, s0) = event
            run = lambda: fn(ins[i0:i0 + na], outs[o0:o0 + no], scr[s0], scr[s0 + 1])
            if steps == 1:
                run()
            else:
                pl.when(step == at)(run)

        for event in events:
            if not event[1]:
                emit(event)
        body(*pre, *ins[:n_in], *outs[:n_out], *scr[:n_scr])
        for event in events:
            if event[1]:
                emit(event)

    kwargs = dict(name=name, out_shape=all_shape, input_output_aliases=aliases)
    if compiler_params is not None:
        kwargs["compiler_params"] = compiler_params
    if grid_spec_scalars is not None:
        kwargs["grid_spec"] = pltpu.PrefetchScalarGridSpec(
            num_scalar_prefetch=1, grid=grid, in_specs=all_in, out_specs=all_out, scratch_shapes=all_scr)
    else:
        kwargs.update(in_specs=all_in, out_specs=all_out, scratch_shapes=all_scr)
        if grid:
            kwargs["grid"] = grid
    call = pl.pallas_call(wrapped, **kwargs)

    def run_call(*args):
        pre_args = [] if grid_spec_scalars is None else [grid_spec_scalars]
        res = list(call(*pre_args, *args, *extra_args))
        core = res[0] if single else res[:n_out]
        job_outs = [res[o0:o0 + no] for (_, _, o0, no, _) in layout]
        return core, job_outs

    return run_call


def _full(shape):
    nd = len(shape)
    return pl.BlockSpec(shape, lambda *_: (0,) * nd, pipeline_mode=pl.Buffered(1))


def _rows(ts, width, col=0):
    return pl.BlockSpec((ts, width), lambda i: (i, col))


def _sig(v):
    return jax.nn.sigmoid(v)


def _rms_stats(x):
    r = lax.rsqrt(jnp.mean(x * x, axis=-1, keepdims=True) + EPS)
    return r, x * r


def _rms_bwd(x, g, dh):
    r, xh = _rms_stats(x)
    dg = jnp.sum(dh * xh, axis=0, keepdims=True)
    dxh = dh * g
    dx = r * (dxh - xh * jnp.mean(dxh * xh, axis=-1, keepdims=True))
    return dx, dg


def _accumulate(ref, val, first):
    @pl.when(first)
    def _():
        ref[...] = val

    @pl.when(jnp.logical_not(first))
    def _():
        ref[...] += val


def _ffn_fwd(x, g, w13, w2, jobs=()):
    s_len = x.shape[0]
    ts = min(TOKEN_TILE, s_len)

    def body(x_ref, g_ref, w13_ref, w2_ref, xo_ref, h_ref, s_ref, dsab_ref):
        xv = x_ref[...]
        r, xh = _rms_stats(xv)
        h = (xh * g_ref[...]).astype(BF)
        h_ref[...] = h
        ab = jnp.dot(h, w13_ref[...], preferred_element_type=F32)
        a = ab[:, :FF]
        b = ab[:, FF:]
        sg = _sig(a)
        sil = a * sg
        dsab_ref[:, :FF] = (b * (sg * (1.0 + a * (1.0 - sg)))).astype(BF)
        dsab_ref[:, FF:] = sil.astype(BF)
        s = (sil * b).astype(BF)
        s_ref[...] = s
        xo_ref[...] = xv + 0.5 * jnp.dot(s, w2_ref[...], preferred_element_type=F32)

    return _pc(
        jobs, body, name="ffn_fwd", grid=(s_len // ts,),
        in_specs=[_rows(ts, D), _full((1, D)), _full((D, 2 * FF)), _full((FF, D))],
        out_specs=[_rows(ts, D), _rows(ts, D), _rows(ts, FF), _rows(ts, 2 * FF)],
        out_shape=[jax.ShapeDtypeStruct((s_len, D), F32), jax.ShapeDtypeStruct((s_len, D), BF),
                   jax.ShapeDtypeStruct((s_len, FF), BF), jax.ShapeDtypeStruct((s_len, 2 * FF), BF)],
        compiler_params=_params(1),
    )(x, g, w13, w2)


def _ffn_bwd(dxo, x, g, dsab, w13, w2, jobs=()):
    s_len = x.shape[0]
    ts = min(TOKEN_TILE, s_len)

    def body(dxo_ref, x_ref, g_ref, dsab_ref, w13_ref, w2_ref, dxi_ref, dab_ref, dy_ref, dg_ref):
        i = pl.program_id(0)
        dxo_v = dxo_ref[...]
        dy = (0.5 * dxo_v).astype(BF)
        dy_ref[...] = dy
        ds = lax.dot_general(dy, w2_ref[...], NT, preferred_element_type=F32)
        dab_ref[:, :FF] = (ds * dsab_ref[:, :FF].astype(F32)).astype(BF)
        dab_ref[:, FF:] = (ds * dsab_ref[:, FF:].astype(F32)).astype(BF)
        dh = lax.dot_general(dab_ref[...], w13_ref[...], NT, preferred_element_type=F32)
        dx, dg = _rms_bwd(x_ref[...], g_ref[...], dh)
        dxi_ref[...] = dxo_v + dx
        _accumulate(dg_ref, dg, i == 0)

    return _pc(
        jobs, body, name="ffn_bwd", grid=(s_len // ts,),
        in_specs=[_rows(ts, D), _rows(ts, D), _full((1, D)), _rows(ts, 2 * FF), _full((D, 2 * FF)), _full((FF, D))],
        out_specs=[_rows(ts, D), _rows(ts, 2 * FF), _rows(ts, D), pl.BlockSpec((1, D), lambda i: (0, 0))],
        out_shape=[jax.ShapeDtypeStruct((s_len, D), F32), jax.ShapeDtypeStruct((s_len, 2 * FF), BF),
                   jax.ShapeDtypeStruct((s_len, D), BF), jax.ShapeDtypeStruct((1, D), F32)],
        compiler_params=_params(1),
    )(dxo, x, g, dsab, w13, w2)


def _wgrad(a, b, tk, tn, name, jobs=()):
    s_len, k = a.shape
    n = b.shape[1]

    def body(a_ref, b_ref, o_ref):
        o_ref[...] = lax.dot_general(a_ref[...], b_ref[...], TN, preferred_element_type=F32).astype(BF)

    return _pc(
        jobs, body, name=name, grid=(k // tk, n // tn),
        in_specs=[pl.BlockSpec((s_len, tk), lambda i, j: (0, i)), pl.BlockSpec((s_len, tn), lambda i, j: (0, j))],
        out_specs=pl.BlockSpec((tk, tn), lambda i, j: (i, j)),
        out_shape=jax.ShapeDtypeStruct((k, n), BF),
        compiler_params=_params(2),
    )(a, b)


def _wgrad_groups(a, b, ka, nb, name, jobs=()):
    s_len = a.shape[0]
    groups = a.shape[1] // ka

    def body(a_ref, b_ref, o_ref):
        o_ref[...] = lax.dot_general(a_ref[...], b_ref[...], TN, preferred_element_type=F32).astype(BF)

    return _pc(
        jobs, body, name=name, grid=(groups,),
        in_specs=[pl.BlockSpec((s_len, ka), lambda gi: (0, gi)), pl.BlockSpec((s_len, nb), lambda gi: (0, gi))],
        out_specs=pl.BlockSpec((ka, nb), lambda gi: (gi, 0)),
        out_shape=jax.ShapeDtypeStruct((groups * ka, nb), BF),
        compiler_params=_params(1),
    )(a, b)


def _mix_in(x, g, w_in, jobs=()):
    s_len = x.shape[0]
    ts = min(TOKEN_TILE, s_len)

    def body(x_ref, g_ref, w_ref, h_ref, p_ref):
        _, xh = _rms_stats(x_ref[...])
        h = (xh * g_ref[...]).astype(BF)
        h_ref[...] = h
        p_ref[...] = jnp.dot(h, w_ref[...], preferred_element_type=F32).astype(BF)

    return _pc(
        jobs, body, name="mix_in", grid=(s_len // ts,),
        in_specs=[_rows(ts, D), _full((1, D)), _full((D, INC))],
        out_specs=[_rows(ts, D), _rows(ts, INC)],
        out_shape=[jax.ShapeDtypeStruct((s_len, D), BF), jax.ShapeDtypeStruct((s_len, INC), BF)],
        compiler_params=_params(1),
    )(x, g, w_in)


def _shift(e, j):
    n = e.shape[0]
    j = j % n
    return e if j == 0 else pltpu.roll(e, j, 0)


def _ln_stats(z):
    mu = jnp.mean(z, axis=-1, keepdims=True)
    zc = z - mu
    rs = lax.rsqrt(jnp.mean(zc * zc, axis=-1, keepdims=True) + EPS)
    return rs, zc * rs


def _ln_bwd(rs, zn, dzn):
    return rs * (dzn - jnp.mean(dzn, axis=-1, keepdims=True) - zn * jnp.mean(dzn * zn, axis=-1, keepdims=True))


_GELU_C0 = 0.7978845608028654
_GELU_C1 = 0.044715


def _gelu(p):
    th = jnp.tanh(_GELU_C0 * (p + _GELU_C1 * p * p * p))
    return 0.5 * p * (1.0 + th), th


def _gelu_grad(p, th):
    return 0.5 * (1.0 + th) + 0.5 * p * (1.0 - th * th) * (_GELU_C0 * (1.0 + 3.0 * _GELU_C1 * p * p))


def _pool_diff(a, t, sign):
    outs = []
    for gi in range(NBR):
        win = 2 ** (gi + 1)
        ag = a[:, gi * GW:(gi + 1) * GW]
        cnt = jnp.clip(t + 1, 1, win).astype(F32)
        ws = ag if sign > 0 else ag / cnt
        for s in range(gi + 1):
            ws = ws + _shift(ws, sign * (2 ** s))
        outs.append((ws / cnt if sign > 0 else ws) - ag)
    return outs


def _sgu_mask():
    row = lax.broadcasted_iota(jnp.int32, (GW, GW), 0)
    col = lax.broadcasted_iota(jnp.int32, (GW, GW), 1)
    return (col // CHUNK) <= (row // CHUNK)


def _assemble(pe_ref, prev_ref, cur_ref, next_ref, i, last, ts):
    pe_ref[0:HALO, :] = jnp.where(i > 0, prev_ref[...], jnp.zeros_like(prev_ref))
    pe_ref[HALO:HALO + ts, :] = cur_ref[...]
    if next_ref is not None:
        pe_ref[HALO + ts:, :] = jnp.where(i < last, next_ref[...], jnp.zeros_like(next_ref))


def _halo_specs(ts, width, s_len, with_next):
    per = ts // HALO
    specs = [pl.BlockSpec((HALO, width), lambda i: (jnp.maximum(i * per - 1, 0), 0)),
             pl.BlockSpec((ts, width), lambda i: (i, 0))]
    if with_next:
        specs.append(pl.BlockSpec((HALO, width), lambda i: (jnp.minimum((i + 1) * per, s_len // HALO - 1), 0)))
    return specs


def _branch_weights_specs():
    return [_full((NBR, GW, GW)), _full((1, BW)), _full((SKP, BW)), _full((CKP, BW)), _full((1, BW)), _full((1, BW)),
            _full((1, BW)), _full((1, BW)), _full((NBR, GW, GW)), _full((GW, NBR))]


def _mix_branches_fwd(p, pool_w, pool_scale, sconv_w, cconv_w, cln_g, cln_b, sln_g, sln_b, sgu_w, sgu_bt, jobs=()):
    s_len = p.shape[0]
    ts = min(BRANCH_TILE, s_len)
    ext = HALO + ts

    def body(pp_ref, pc_ref, pw_ref, ps_ref, sw_ref, cw_ref, clg_ref, clb_ref, slg_ref, slb_ref, gw_ref, gb_ref,
             y_ref, z_ref, pe_ref):
        i = pl.program_id(0)
        _assemble(pe_ref, pp_ref, pc_ref, None, i, 0, ts)
        t = i * ts - HALO + lax.broadcasted_iota(jnp.int32, (ext, 1), 0)

        dgs = _pool_diff(pe_ref[:, 0:BW].astype(F32), t, 1)
        for gi in range(NBR):
            e = jnp.dot(dgs[gi][HALO:].astype(BF), pw_ref[gi].astype(BF), preferred_element_type=F32)
            y_ref[:, gi * GW:(gi + 1) * GW] = (e * ps_ref[:, gi * GW:(gi + 1) * GW]).astype(BF)

        xin = pe_ref[:, BW:2 * BW].astype(F32)
        cg = pe_ref[:, 3 * BW:4 * BW].astype(F32)
        q = cg * xin
        cv = sw_ref[2:3, :] * q + sw_ref[1:2, :] * _shift(q, 1) + sw_ref[0:1, :] * _shift(q, 2)
        y_ref[:, BW:2 * BW] = (pe_ref[HALO:, 2 * BW:3 * BW].astype(F32) * cv[HALO:]).astype(BF)

        yg = pe_ref[:, 4 * BW:5 * BW].astype(F32) * _sig(pe_ref[:, 5 * BW:6 * BW].astype(F32))
        z = cw_ref[CK - 1:CK, :] * yg
        for j in range(1, CK):
            z = z + cw_ref[CK - 1 - j:CK - j, :] * _shift(yg, j)
        z_ref[...] = z[HALO:].astype(BF)
        _, zn = _ln_stats(z[HALO:])
        nn = zn * clg_ref[...] + clb_ref[...]
        y_ref[:, 2 * BW:3 * BW] = (nn * _sig(nn)).astype(BF)

        u, _ = _gelu(pc_ref[:, 6 * BW:7 * BW].astype(F32))
        v, _ = _gelu(pc_ref[:, 7 * BW:8 * BW].astype(F32))
        _, vn = _ln_stats(v)
        vn = (vn * slg_ref[...] + slb_ref[...]).astype(BF)
        mask = _sgu_mask()
        for hd in range(NBR):
            wm = jnp.where(mask, gw_ref[hd], 0.0).astype(BF)
            for blk in range(ts // GW):
                rows = slice(blk * GW, (blk + 1) * GW)
                cols = slice(hd * GW, (hd + 1) * GW)
                zz = jnp.dot(wm, vn[rows, cols], preferred_element_type=F32) + gb_ref[:, hd:hd + 1]
                y_ref[rows, 3 * BW + hd * GW:3 * BW + (hd + 1) * GW] = (u[rows, cols] * zz).astype(BF)

    return _pc(
        jobs, body, name="mix_branches_fwd", grid=(s_len // ts,),
        in_specs=_halo_specs(ts, MIXC, s_len, False) + _branch_weights_specs(),
        out_specs=[_rows(ts, NBR * BW), _rows(ts, BW)],
        out_shape=[jax.ShapeDtypeStruct((s_len, NBR * BW), BF), jax.ShapeDtypeStruct((s_len, BW), BF)],
        scratch_shapes=[pltpu.VMEM((ext, MIXC), BF)],
        compiler_params=_params(1),
    )(p, p, pool_w, pool_scale, sconv_w, cconv_w, cln_g, cln_b, sln_g, sln_b, sgu_w, sgu_bt)


def _mix_branches_bwd(p, dy, z, dp, pool_w, pool_scale, sconv_w, cconv_w, cln_g, cln_b, sln_g, sln_b, sgu_w, sgu_bt,
                      jobs=()):
    s_len = p.shape[0]
    ts = min(BRANCH_TILE, s_len)
    ext = ts + 2 * HALO
    last = s_len // ts - 1
    tile = slice(HALO, HALO + ts)
    small_shapes = [(NBR, GW, GW), (1, BW), (SKP, BW), (CKP, BW), (1, BW), (1, BW), (1, BW), (1, BW), (NBR, GW, GW),
                    (NBR, GW, GW)]

    def body(pp_ref, pc_ref, pn_ref, dyp_ref, dyc_ref, dyn_ref, zc_ref, zn_ref, dpin_ref,
             pw_ref, ps_ref, sw_ref, cw_ref, clg_ref, clb_ref, slg_ref, slb_ref, gw_ref, gb_ref,
             dp_ref, dpw_ref, dps_ref, dsw_ref, dcw_ref, dclg_ref, dclb_ref, dslg_ref, dslb_ref, dgw_ref, dgb_ref,
             pe_ref, de_ref):
        del dyp_ref, dpin_ref
        i = pl.program_id(0)
        first = i == 0
        _assemble(pe_ref, pp_ref, pc_ref, pn_ref, i, last, ts)
        de_ref[0:HALO, :] = jnp.zeros((HALO, NBR * BW), BF)
        de_ref[HALO:HALO + ts, :] = dyc_ref[...]
        de_ref[HALO + ts:, :] = jnp.where(i < last, dyn_ref[...], jnp.zeros_like(dyn_ref))
        t = i * ts - HALO + lax.broadcasted_iota(jnp.int32, (ext, 1), 0)

        @pl.when(first)
        def _():
            dsw_ref[...] = jnp.zeros((SKP, BW), F32)
            dcw_ref[...] = jnp.zeros((CKP, BW), F32)

        dgs = _pool_diff(pe_ref[:, 0:BW].astype(F32), t, 1)
        dya = de_ref[:, 0:BW].astype(F32)
        dds = []
        for gi in range(NBR):
            cols = slice(gi * GW, (gi + 1) * GW)
            pw = pw_ref[gi].astype(BF)
            d_t = dgs[gi][tile].astype(BF)
            e = jnp.dot(d_t, pw, preferred_element_type=F32)
            _accumulate(dps_ref.at[:, cols], jnp.sum(dya[tile, cols] * e, axis=0, keepdims=True), first)
            de_g = (dya[:, cols] * ps_ref[:, cols]).astype(BF)
            _accumulate(dpw_ref.at[gi], lax.dot_general(d_t, de_g[tile], TN, preferred_element_type=F32), first)
            dds.append(lax.dot_general(de_g, pw, NT, preferred_element_type=F32))
        das = _pool_diff(jnp.concatenate(dds, axis=1), t, -1)
        for gi in range(NBR):
            dp_ref[:, gi * GW:(gi + 1) * GW] = das[gi][tile].astype(BF)

        xin = pe_ref[:, BW:2 * BW].astype(F32)
        bg = pe_ref[:, 2 * BW:3 * BW].astype(F32)
        cg = pe_ref[:, 3 * BW:4 * BW].astype(F32)
        q = cg * xin
        qs = [q, _shift(q, 1), _shift(q, 2)]
        cv = sw_ref[2:3, :] * qs[0] + sw_ref[1:2, :] * qs[1] + sw_ref[0:1, :] * qs[2]
        dyb = de_ref[:, BW:2 * BW].astype(F32)
        dcv = dyb * bg
        for j in range(SK):
            dsw_ref[SK - 1 - j:SK - j, :] += jnp.sum(dcv[tile] * qs[j][tile], axis=0, keepdims=True)
        dq = sw_ref[2:3, :] * dcv + sw_ref[1:2, :] * _shift(dcv, -1) + sw_ref[0:1, :] * _shift(dcv, -2)
        dp_ref[:, BW:2 * BW] = (dq * cg)[tile].astype(BF)
        dp_ref[:, 2 * BW:3 * BW] = (dyb * cv)[tile].astype(BF)
        dp_ref[:, 3 * BW:4 * BW] = (dq * xin)[tile].astype(BF)

        ca = pc_ref[:, 4 * BW:5 * BW].astype(F32)
        sb = _sig(pc_ref[:, 5 * BW:6 * BW].astype(F32))
        yg_t = ca * sb
        z = jnp.concatenate([zc_ref[...].astype(F32), zn_ref[...].astype(F32)], axis=0)
        rs, zn = _ln_stats(z)
        nn = zn * clg_ref[...] + clb_ref[...]
        sn = _sig(nn)
        dn = de_ref[HALO:, 2 * BW:3 * BW].astype(F32) * (sn * (1.0 + nn * (1.0 - sn)))
        _accumulate(dclg_ref, jnp.sum((dn * zn)[:ts], axis=0, keepdims=True), first)
        _accumulate(dclb_ref, jnp.sum(dn[:ts], axis=0, keepdims=True), first)
        dz = _ln_bwd(rs, zn, dn * clg_ref[...])
        dyg = cw_ref[CK - 1:CK, :] * dz[:ts]
        dcw_ref[CK - 1:CK, :] += jnp.sum(dz[:ts] * yg_t, axis=0, keepdims=True)
        for j in range(1, CK):
            dz_ahead = _shift(dz, -j)[:ts]
            dyg = dyg + cw_ref[CK - 1 - j:CK - j, :] * dz_ahead
            dcw_ref[CK - 1 - j:CK - j, :] += jnp.sum(dz_ahead * yg_t, axis=0, keepdims=True)
        dp_ref[:, 4 * BW:5 * BW] = (dyg * sb).astype(BF)
        dp_ref[:, 5 * BW:6 * BW] = (dyg * ca * sb * (1.0 - sb)).astype(BF)

        pu = pc_ref[:, 6 * BW:7 * BW].astype(F32)
        pv = pc_ref[:, 7 * BW:8 * BW].astype(F32)
        u, thu = _gelu(pu)
        v, thv = _gelu(pv)
        vrs, vn0 = _ln_stats(v)
        vn = (vn0 * slg_ref[...] + slb_ref[...]).astype(BF)
        dyd = dyc_ref[:, 3 * BW:4 * BW].astype(F32)
        dzz = dyd * u
        dzb = dzz.astype(BF)
        mask = _sgu_mask()
        dvn_cols = []
        for hd in range(NBR):
            cols = slice(hd * GW, (hd + 1) * GW)
            wm = jnp.where(mask, gw_ref[hd], 0.0).astype(BF)
            dwm = jnp.zeros((GW, GW), F32)
            dbs = jnp.zeros((GW, GW), F32)
            dvn_rows = []
            for blk in range(ts // GW):
                rows = slice(blk * GW, (blk + 1) * GW)
                zz = jnp.dot(wm, vn[rows, cols], preferred_element_type=F32) + gb_ref[:, hd:hd + 1]
                dp_ref[rows, 6 * BW + hd * GW:6 * BW + (hd + 1) * GW] = (
                    dyd[rows, cols] * zz * _gelu_grad(pu[rows, cols], thu[rows, cols])).astype(BF)
                dwm = dwm + lax.dot_general(dzb[rows, cols], vn[rows, cols], NT, preferred_element_type=F32)
                dbs = dbs + dzz[rows, cols]
                dvn_rows.append(lax.dot_general(wm, dzb[rows, cols], TN, preferred_element_type=F32))
            _accumulate(dgw_ref.at[hd], jnp.where(mask, dwm, 0.0), first)
            _accumulate(dgb_ref.at[hd], dbs, first)
            dvn_cols.append(jnp.concatenate(dvn_rows, axis=0))
        dvn = jnp.concatenate(dvn_cols, axis=1)
        _accumulate(dslg_ref, jnp.sum(dvn * vn0, axis=0, keepdims=True), first)
        _accumulate(dslb_ref, jnp.sum(dvn, axis=0, keepdims=True), first)
        dv = _ln_bwd(vrs, vn0, dvn * slg_ref[...])
        dp_ref[:, 7 * BW:8 * BW] = (dv * _gelu_grad(pv, thv)).astype(BF)

    const = lambda shp: pl.BlockSpec(shp, lambda i: (0,) * len(shp))
    return _pc(
        jobs, body, name="mix_branches_bwd", grid=(s_len // ts,),
        in_specs=(_halo_specs(ts, MIXC, s_len, True) + _halo_specs(ts, NBR * BW, s_len, True)
                  + _halo_specs(ts, BW, s_len, True)[1:] + [pl.BlockSpec(memory_space=pl.ANY)] + _branch_weights_specs()),
        out_specs=[pl.BlockSpec((ts, MIXC), lambda i: (i, 0))] + [const(s) for s in small_shapes],
        out_shape=[jax.ShapeDtypeStruct((s_len, INC), BF)] + [jax.ShapeDtypeStruct(s, F32) for s in small_shapes],
        scratch_shapes=[pltpu.VMEM((ext, MIXC), BF), pltpu.VMEM((ext, NBR * BW), BF)],
        input_output_aliases={8: 0},
        compiler_params=_params(1),
    )(p, p, p, dy, dy, dy, z, z, dp, pool_w, pool_scale, sconv_w, cconv_w, cln_g, cln_b, sln_g, sln_b, sgu_w, sgu_bt)


def _mix_out_fwd(x, y, p, w_up, w_out, jobs=()):
    s_len = x.shape[0]
    ts = min(TOKEN_TILE if jobs else 2 * TOKEN_TILE, s_len)

    n_steps = s_len // ts

    def body(x_ref, y_ref, p_hbm, wu_ref, wo_ref, xo_ref, m_ref, up_ref, pg_buf, sems):
        i = pl.program_id(0)

        def fetch(step):
            rows = pl.ds(pl.multiple_of(step * ts, ts), ts)
            return pltpu.make_async_copy(p_hbm.at[rows, pl.ds(MIXC, NBR * D)], pg_buf.at[step % 3], sems.at[step % 3])

        @pl.when(i == 0)
        def _():
            for step in range(min(2, n_steps)):
                fetch(step).start()

        @pl.when(i + 2 < n_steps)
        def _():
            fetch(i + 2).start()

        fetch(i).wait()
        slot = i % 3
        m = jnp.zeros((ts, D), F32)
        for gi in range(NBR):
            up = jnp.dot(y_ref[:, gi * BW:(gi + 1) * BW], wu_ref[gi], preferred_element_type=F32)
            up_ref[:, gi * D:(gi + 1) * D] = up.astype(BF)
            m = m + _sig(pg_buf[slot, :, gi * D:(gi + 1) * D].astype(F32)) * up
        mb = m.astype(BF)
        m_ref[...] = mb
        xo_ref[...] = x_ref[...] + jnp.dot(mb, wo_ref[...], preferred_element_type=F32)

    return _pc(
        jobs, body, name="mix_out_fwd", grid=(n_steps,),
        in_specs=[_rows(ts, D), _rows(ts, NBR * BW), ANY, _full((NBR, BW, D)), _full((D, D))],
        out_specs=[_rows(ts, D), _rows(ts, D), _rows(ts, NBR * D)],
        out_shape=[jax.ShapeDtypeStruct((s_len, D), F32), jax.ShapeDtypeStruct((s_len, D), BF),
                   jax.ShapeDtypeStruct((s_len, NBR * D), BF)],
        scratch_shapes=[pltpu.VMEM((3, ts, NBR * D), BF), pltpu.SemaphoreType.DMA((3,))],
        compiler_params=_params(1),
    )(x, y, p, w_up, w_out)


def _mix_out_bwd(dxo, up, p, w_up, w_out, jobs=()):
    s_len = dxo.shape[0]
    ts = min(TOKEN_TILE, s_len)
    n_steps = s_len // ts

    def body(dxo_ref, up_hbm, p_hbm, wu_ref, wo_ref, dy_ref, dp_ref, dup_ref, dxb_ref, up_buf, pg_buf, sems):
        i = pl.program_id(0)

        def fetch(step):
            slot = step % 3
            rows = pl.ds(pl.multiple_of(step * ts, ts), ts)
            return (pltpu.make_async_copy(up_hbm.at[rows, :], up_buf.at[slot], sems.at[0, slot]),
                    pltpu.make_async_copy(p_hbm.at[rows, pl.ds(MIXC, NBR * D)], pg_buf.at[slot], sems.at[1, slot]))

        @pl.when(i == 0)
        def _():
            for step in range(min(2, n_steps)):
                for cp in fetch(step):
                    cp.start()

        @pl.when(i + 2 < n_steps)
        def _():
            for cp in fetch(i + 2):
                cp.start()

        for cp in fetch(i):
            cp.wait()
        slot = i % 3
        dxb = dxo_ref[...].astype(BF)
        dxb_ref[...] = dxb
        dm = lax.dot_general(dxb, wo_ref[...], NT, preferred_element_type=F32)
        for gi in range(NBR):
            cols = slice(gi * D, (gi + 1) * D)
            gate = _sig(pg_buf[slot, :, cols].astype(F32))
            dp_ref[:, cols] = (dm * up_buf[slot, :, cols].astype(F32) * gate * (1.0 - gate)).astype(BF)
            dup = (dm * gate).astype(BF)
            dup_ref[:, cols] = dup
            dy_ref[:, gi * BW:(gi + 1) * BW] = lax.dot_general(
                dup, wu_ref[gi], NT, preferred_element_type=F32).astype(BF)

    return _pc(
        jobs, body, name="mix_out_bwd", grid=(n_steps,),
        in_specs=[_rows(ts, D), ANY, ANY, _full((NBR, BW, D)), _full((D, D))],
        out_specs=[_rows(ts, NBR * BW), _rows(ts, NBR * D, 1), _rows(ts, NBR * D), _rows(ts, D)],
        out_shape=[jax.ShapeDtypeStruct((s_len, NBR * BW), BF), jax.ShapeDtypeStruct((s_len, INC), BF),
                   jax.ShapeDtypeStruct((s_len, NBR * D), BF), jax.ShapeDtypeStruct((s_len, D), BF)],
        scratch_shapes=[pltpu.VMEM((3, ts, NBR * D), BF), pltpu.VMEM((3, ts, NBR * D), BF),
                        pltpu.SemaphoreType.DMA((2, 3))],
        compiler_params=_params(1),
    )(dxo, up, p, w_up, w_out)


def _mix_in_bwd(dp, w_in, x, g, dxo, jobs=()):
    s_len = x.shape[0]
    ts = min(TOKEN_TILE, s_len)

    def body(dp_ref, w_ref, x_ref, g_ref, dxo_ref, dxi_ref, dg_ref):
        i = pl.program_id(0)
        dh = lax.dot_general(dp_ref[...], w_ref[...], NT, preferred_element_type=F32)
        dx, dg = _rms_bwd(x_ref[...], g_ref[...], dh)
        dxi_ref[...] = dxo_ref[...] + dx
        _accumulate(dg_ref, dg, i == 0)

    return _pc(
        jobs, body, name="mix_in_bwd", grid=(s_len // ts,),
        in_specs=[_rows(ts, INC), _full((D, INC)), _rows(ts, D), _full((1, D)), _rows(ts, D)],
        out_specs=[_rows(ts, D), pl.BlockSpec((1, D), lambda i: (0, 0))],
        out_shape=[jax.ShapeDtypeStruct((s_len, D), F32), jax.ShapeDtypeStruct((1, D), F32)],
        compiler_params=_params(1),
    )(dp, w_in, x, g, dxo)


def _loss_head(x, g, target, jobs=()):
    s_len = x.shape[0]
    ts = min(512, s_len)

    def body(x_ref, g_ref, t_ref, dx_ref, dg_ref, loss_ref):
        i = pl.program_id(0)
        xv = x_ref[...]
        _, xh = _rms_stats(xv)
        err = xh * g_ref[...] - t_ref[...]
        part = 0.5 * jnp.sum(jnp.mean(err * err, axis=-1, keepdims=True), axis=0, keepdims=True)
        dx, dg = _rms_bwd(xv, g_ref[...], err * (1.0 / D))
        dx_ref[...] = dx
        _accumulate(dg_ref, dg, i == 0)
        _accumulate(loss_ref, jnp.broadcast_to(part, (1, GW)), i == 0)

    return _pc(
        jobs, body, name="loss_head", grid=(s_len // ts,),
        in_specs=[_rows(ts, D), _full((1, D)), _rows(ts, D)],
        out_specs=[_rows(ts, D), pl.BlockSpec((1, D), lambda i: (0, 0)), pl.BlockSpec((1, GW), lambda i: (0, 0))],
        out_shape=[jax.ShapeDtypeStruct((s_len, D), F32), jax.ShapeDtypeStruct((1, D), F32),
                   jax.ShapeDtypeStruct((1, GW), F32)],
        compiler_params=_params(1),
    )(x, g, target)


SUM_TILE = 1 << 20
ADAM_TILE = 1 << 19
CAST_TILE = 1 << 19


def _row_tile(rows, cols, budget=1 << 18):
    tr = rows
    while tr * cols > budget and tr % 16 == 0:
        tr //= 2
    return tr


def _elementwise(fn, name, ins, out_dtypes):
    rows, cols = ins[0].shape
    tr = _row_tile(rows, cols)
    n_in = len(ins)

    def body(*refs):
        res = fn(*[r[...] for r in refs[:n_in]])
        for o_ref, val in zip(refs[n_in:], res):
            o_ref[...] = val.astype(o_ref.dtype)

    outs = pl.pallas_call(
        body, name=name, grid=(rows // tr,),
        in_specs=[_rows(tr, cols)] * n_in, out_specs=[_rows(tr, cols)] * len(out_dtypes),
        out_shape=[jax.ShapeDtypeStruct((rows, cols), dt) for dt in out_dtypes],
        compiler_params=_params(1),
    )(*ins)
    return outs


def _tiled(fn, name, grid, in_specs, out_specs, out_shape, args, scalars=None, alias=None):
    alias = alias or {}
    n_in = len(in_specs) - len(alias)
    n_pre = 0 if scalars is None else 1

    def body(*refs):
        refs = refs[n_pre:]
        res = fn(*[r[...] for r in refs[:n_in]])
        for o_ref, val in zip(refs[len(in_specs):], res):
            o_ref[...] = val.astype(o_ref.dtype)

    aliases = {n_pre + pos: out for pos, out in alias.items()}
    if scalars is None:
        return pl.pallas_call(body, name=name, grid=grid, in_specs=in_specs, out_specs=out_specs, out_shape=out_shape,
                              input_output_aliases=aliases, compiler_params=_params(len(grid)))(*args)
    spec = pltpu.PrefetchScalarGridSpec(num_scalar_prefetch=1, grid=grid, in_specs=in_specs, out_specs=out_specs)
    return pl.pallas_call(body, name=name, grid_spec=spec, out_shape=out_shape, input_output_aliases=aliases,
                          compiler_params=_params(len(grid)))(scalars, *args)


def _cast_into(shard, layer, col, dtype, sc):
    ks, ns = shard.shape[1:]
    tr = _row_tile(ks, ns, SUM_TILE)
    full = (ks, ns * NCHIP) if col else (ks * NCHIP, ns)
    out_idx = (lambda i, s: (i, s[1])) if col else (lambda i, s: (s[1] * (ks // tr) + i, 0))
    return _tiled(lambda v: (v,), "cast_into", (ks // tr,), [pl.BlockSpec((None, tr, ns), lambda i, s: (layer, i, 0))],
                  [pl.BlockSpec((tr, ns), out_idx)], [jax.ShapeDtypeStruct(full, dtype)], [shard], sc)[0]


def _cast_rest(items, jobs=()):
    n = len(items)
    tiles, slot_sets, counts = [], {}, {}
    for k, (shard, _, _, dt) in enumerate(items):
        ks, ns = shard.shape[1:]
        tr = _row_tile(ks, ns, CAST_TILE)
        which = slot_sets.setdefault((tr, ns, shard.dtype, dt), len(slot_sets))
        for t in range(ks // tr):
            tiles.append((k, t, tr, which, counts.get(which, 0) % 2))
            counts[which] = counts.get(which, 0) + 1
    n_sets = len(slot_sets)

    def body(*refs):
        ins, outs = refs[:n], refs[n:2 * n]
        bufs = refs[2 * n:2 * n + 2 * n_sets]
        in_sems, out_sems = refs[2 * n + 2 * n_sets:]
        chip = 2 * lax.axis_index("x") + lax.axis_index("y")
        fetch, store = [], []
        for k, t, tr, which, slot in tiles:
            shard, layer, col, _ = items[k]
            ks, ns = shard.shape[1:]
            if col:
                place = outs[k].at[pl.ds(t * tr, tr), pl.ds(pl.multiple_of(chip * ns, GW), ns)]
            else:
                place = outs[k].at[pl.ds(pl.multiple_of(chip * ks + t * tr, 16), tr), :]
            fetch.append(pltpu.make_async_copy(
                ins[k].at[layer, pl.ds(t * tr, tr), :], bufs[2 * which].at[slot], in_sems.at[which, slot]))
            store.append(pltpu.make_async_copy(bufs[2 * which + 1].at[slot], place, out_sems.at[which, slot]))
        busy = {}
        fetch[0].start()
        for i, (k, t, tr, which, slot) in enumerate(tiles):
            if i + 1 < len(tiles):
                fetch[i + 1].start()
            fetch[i].wait()
            if (which, slot) in busy:
                store[busy[which, slot]].wait()
            bufs[2 * which + 1][slot] = bufs[2 * which][slot].astype(bufs[2 * which + 1].dtype)
            store[i].start()
            busy[which, slot] = i
        for i in busy.values():
            store[i].wait()

    def full_shape(shard, col):
        ks, ns = shard.shape[1:]
        return (ks, ns * NCHIP) if col else (ks * NCHIP, ns)

    scratch = []
    for (tr, ns, dt_in, dt_out) in slot_sets:
        scratch += [pltpu.VMEM((2, tr, ns), dt_in), pltpu.VMEM((2, tr, ns), dt_out)]
    scratch += [pltpu.SemaphoreType.DMA((n_sets, 2)), pltpu.SemaphoreType.DMA((n_sets, 2))]
    return _pc(
        jobs, body, name="cast_rest", in_specs=[ANY] * n, out_specs=[ANY] * n,
        out_shape=[jax.ShapeDtypeStruct(full_shape(shard, col), dt) for shard, _, col, dt in items],
        scratch_shapes=scratch, compiler_params=pltpu.CompilerParams(vmem_limit_bytes=VMEM_LIMIT),
    )(*[shard for shard, _, _, _ in items])


def _pair_sum(g, got, col, sc):
    hk, hn = got.shape
    tr = _row_tile(hk, hn, SUM_TILE)
    g_idx = (lambda i, s: (s[0] * (hk // tr) + i, 0)) if col else (lambda i, s: (i, s[0]))
    plain = pl.BlockSpec((tr, hn), lambda i, s: (i, 0))
    return _tiled(lambda a, b: (a.astype(F32) + b.astype(F32),), "pair_sum", (hk // tr,),
                  [pl.BlockSpec((tr, hn), g_idx), plain], [plain], [jax.ShapeDtypeStruct((hk, hn), BF)], [g, got], sc)[0]


def _chip_sum(ph, got, col, layer, depth, sc, carry):
    qk, qn = got.shape[1:]
    tr = _row_tile(qk, qn, SUM_TILE)
    ph_idx = (lambda i, s: (i, s[1])) if col else (lambda i, s: (s[1] * (qk // tr) + i, 0))
    out_shape = (depth, 2 * qk, qn) if col else (depth, qk, 2 * qn)
    out_idx = (lambda i, s: (layer, s[0] * (qk // tr) + i, 0)) if col else (lambda i, s: (layer, i, s[0]))
    in_specs = [pl.BlockSpec((tr, qn), ph_idx)] + [pl.BlockSpec((None, tr, qn), lambda i, s, j=j: (j, i, 0)) for j in range(3)]
    args = [ph, got, got, got]
    if carry is not None:
        in_specs.append(ANY)
        args.append(carry)
    return _tiled(lambda a, b, c_, d_: (a.astype(F32) + b.astype(F32) + c_.astype(F32) + d_.astype(F32),), "chip_sum",
                  (qk // tr,), in_specs, [pl.BlockSpec((None, tr, qn), out_idx)],
                  [jax.ShapeDtypeStruct(out_shape, F32)], args, sc, alias=None if carry is None else {4: 0})[0]


def _adamw_math(w, g, m, v):
    m = ADAM_B1 * m + (1.0 - ADAM_B1) * g
    v = ADAM_B2 * v + (1.0 - ADAM_B2) * (g * g)
    m_hat = m / (1.0 - ADAM_B1 ** ADAM_STEP)
    v_hat = v / (1.0 - ADAM_B2 ** ADAM_STEP)
    delta = -ADAM_LR * (m_hat / (jnp.sqrt(v_hat) + ADAM_EPS) + ADAM_WD * w)
    return delta, m, v


def _adamw_layer(w, g, m, v, layer, carry):
    k, n = w.shape[1:]
    tr = _row_tile(k, n, ADAM_TILE)
    blk = pl.BlockSpec((None, tr, n), lambda i: (layer, i, 0))
    carry = list(carry or [])
    return _tiled(lambda *a: (*_adamw_math(*a), a[1]), "adamw_layer", (k // tr,), [blk] * 4 + [ANY] * len(carry),
                  [blk] * 4, [jax.ShapeDtypeStruct(w.shape, F32)] * 4, [w, g, m, v] + carry, None,
                  alias={4 + pos: pos for pos in range(len(carry))})


def _adamw(w, g, m, v):
    shape = w.shape
    two_d = lambda a: a.reshape(-1, shape[-1])
    outs = _elementwise(_adamw_math, "adamw", [two_d(w), two_d(g), two_d(m), two_d(v)], [F32, F32, F32])
    return [o.reshape(shape) for o in outs]


ANY = pl.BlockSpec(memory_space=pl.ANY)


def _place():
    x, y, c = lax.axis_index("x"), lax.axis_index("y"), lax.axis_index("c")
    chips = [(1 - x, y), (x, 1 - y), (1 - x, 1 - y)]
    return x, y, c, chips


def _cols(ref, start, size):
    idx = (slice(None),) * (len(ref.shape) - 1) + (pl.ds(pl.multiple_of(start, GW), size),)
    return ref.at[idx]


def _rows_of(ref, start, size):
    nd = len(ref.shape)
    idx = (slice(None),) * (nd - 2) + (pl.ds(pl.multiple_of(start, 16), size), slice(None))
    return ref.at[idx]


def _region(ref, col_sharded, chip, half):
    k, n = ref.shape
    align = 16 if ref.dtype == BF else 8
    if col_sharded:
        return ref.at[pl.ds(pl.multiple_of(half * (k // 2), align), k // 2),
                      pl.ds(pl.multiple_of(chip * (n // NCHIP), GW), n // NCHIP)]
    rows = k // (2 * NCHIP)
    return ref.at[pl.ds(pl.multiple_of((2 * chip + half) * rows, align), rows), :]


def _job_gather(bufs, col_sharded, handoff):
    n = len(bufs)

    def copies(outs, send_sems, recv_sems, stage):
        x, y, c, chips = _place()
        sends, lands = [], []
        for k in range(n):
            for j, chip in enumerate(chips):
                theirs = 2 * chip[0] + chip[1]
                if stage == 0:
                    src, to = _region(outs[k], col_sharded[k], 2 * x + y, c), (*chip, c)
                    land = _region(outs[k], col_sharded[k], theirs, c)
                else:
                    src, to = _region(outs[k], col_sharded[k], theirs, c), (x, y, 1 - c)
                    land = _region(outs[k], col_sharded[k], theirs, 1 - c)
                sem = 3 * n * stage + 3 * k + j
                sems = dict(send_sem=send_sems.at[sem], recv_sem=recv_sems.at[sem], device_id=to, device_id_type=MESH)
                sends.append(pltpu.make_async_remote_copy(src_ref=src, dst_ref=src, **sems))
                lands.append(pltpu.make_async_remote_copy(src_ref=land, dst_ref=land, **sems))
        return sends, lands

    def start(ins, outs, send_sems, recv_sems):
        for cp in copies(outs, send_sems, recv_sems, 0)[0]:
            cp.start()

    def hand_on(ins, outs, send_sems, recv_sems):
        for cp in copies(outs, send_sems, recv_sems, 0)[1]:
            cp.wait_recv()
        for cp in copies(outs, send_sems, recv_sems, 1)[0]:
            cp.start()

    def finish(ins, outs, send_sems, recv_sems):
        sends, lands = copies(outs, send_sems, recv_sems, 1)
        for cp in lands:
            cp.wait_recv()
        for cp in copies(outs, send_sems, recv_sems, 0)[0] + sends:
            cp.wait_send()

    return _Job(bufs, n, [], 6 * n, [(0.0, start), (handoff, hand_on), (1.0, finish)])


def _half(ref, col_sharded, c):
    k, n = ref.shape[-2:]
    return _rows_of(ref, c * (k // 2), k // 2) if col_sharded else _cols(ref, c * (n // 2), n // 2)


def _quarter(ref, col_sharded, j):
    k, n = ref.shape[-2:]
    return _cols(ref, j * (n // NCHIP), n // NCHIP) if col_sharded else _rows_of(ref, j * (k // NCHIP), k // NCHIP)


def _job_pair(grads, col_sharded):
    n = len(grads)

    def half_shape(g, col):
        return (g.shape[0] // 2, g.shape[1]) if col else (g.shape[0], g.shape[1] // 2)

    def copies(ins, got, send_sems, recv_sems):
        x, y, c, _ = _place()
        return [pltpu.make_async_remote_copy(
            src_ref=_half(ins[k], col_sharded[k], 1 - c), dst_ref=got[k], send_sem=send_sems.at[k],
            recv_sem=recv_sems.at[k], device_id=(x, y, 1 - c), device_id_type=MESH) for k in range(n)]

    def start(*refs):
        for cp in copies(*refs):
            cp.start()

    def finish(*refs):
        for cp in copies(*refs):
            cp.wait()

    fresh = [jax.ShapeDtypeStruct(half_shape(g, col), g.dtype) for g, col in zip(grads, col_sharded)]
    return _Job(grads, 0, fresh, n, [(0.0, start), (1.0, finish)])


def _job_chip(halves, col_sharded):
    n = len(halves)

    def quarter_shape(h, col):
        return (3, h.shape[0], h.shape[1] // NCHIP) if col else (3, h.shape[0] // NCHIP, h.shape[1])

    def copies(ins, got, send_sems, recv_sems):
        x, y, c, chips = _place()
        return [pltpu.make_async_remote_copy(
            src_ref=_quarter(ins[k], col_sharded[k], 2 * chip[0] + chip[1]), dst_ref=got[k].at[j],
            send_sem=send_sems.at[3 * k + j], recv_sem=recv_sems.at[3 * k + j], device_id=(*chip, c), device_id_type=MESH)
            for k in range(n) for j, chip in enumerate(chips)]

    def start(*refs):
        for cp in copies(*refs):
            cp.start()

    def finish(*refs):
        for cp in copies(*refs):
            cp.wait()

    fresh = [jax.ShapeDtypeStruct(quarter_shape(h, col), h.dtype) for h, col in zip(halves, col_sharded)]
    return _Job(halves, 0, fresh, 3 * n, [(0.0, start), (1.0, finish)])


def _job_sibling(shards, col_sharded, layers):
    n = len(shards)

    def copies(outs, send_sems, recv_sems):
        x, y, c, _ = _place()
        sends, lands = [], []
        for k in range(n):
            sems = dict(send_sem=send_sems.at[k], recv_sem=recv_sems.at[k], device_id=(x, y, 1 - c), device_id_type=MESH)
            mine = _half(outs[k].at[layers[k]], col_sharded[k], c)
            theirs = _half(outs[k].at[layers[k]], col_sharded[k], 1 - c)
            sends.append(pltpu.make_async_remote_copy(src_ref=mine, dst_ref=mine, **sems))
            lands.append(pltpu.make_async_remote_copy(src_ref=theirs, dst_ref=theirs, **sems))
        return sends, lands

    def start(ins, outs, send_sems, recv_sems):
        for cp in copies(outs, send_sems, recv_sems)[0]:
            cp.start()

    def finish(ins, outs, send_sems, recv_sems):
        sends, lands = copies(outs, send_sems, recv_sems)
        for cp in lands:
            cp.wait_recv()
        for cp in sends:
            cp.wait_send()

    return _Job(shards, n, [], n, [(0.0, start), (1.0, finish)])


def _standalone(jobs, name):
    return _pc(jobs, lambda: None, name=name, in_specs=[], out_specs=[], out_shape=[])()[1]


def _all_reduce_small(buf, jobs=()):
    rows = buf.shape[0]
    per, half = rows // NDEV, rows // 2

    def body(in_ref, out_ref, got_sib, pair_ref, got_chip, send_sems, recv_sems):
        x, y, c, chips = _place()
        sibling = (x, y, 1 - c)

        def part(ref, start, size):
            return ref.at[pl.ds(pl.multiple_of(start, 8), size), :]

        def copy(src, dst, sem, to):
            return pltpu.make_async_remote_copy(src_ref=src, dst_ref=dst, send_sem=send_sems.at[sem],
                                                recv_sem=recv_sems.at[sem], device_id=to, device_id_type=MESH)

        down = copy(part(in_ref, (1 - c) * half, half), got_sib, 0, sibling)
        down.start()
        down.wait()
        pair_ref[...] = part(in_ref, c * half, half)[...] + got_sib[...]
        scatter = [copy(part(pair_ref, (2 * chip[0] + chip[1]) * per, per), got_chip.at[k], 1 + k, (*chip, c))
                   for k, chip in enumerate(chips)]
        for cp in scatter:
            cp.start()
        for cp in scatter:
            cp.wait()
        mine = part(out_ref, c * half + (2 * x + y) * per, per)
        mine[...] = part(pair_ref, (2 * x + y) * per, per)[...] + got_chip[0] + got_chip[1] + got_chip[2]
        share = [copy(mine, mine, 4 + k, (*chip, c)) for k, chip in enumerate(chips)]
        for cp in share:
            cp.start()
        for k, chip in enumerate(chips):
            share[k].wait_send()
            theirs = part(out_ref, c * half + (2 * chip[0] + chip[1]) * per, per)
            copy(theirs, theirs, 4 + k, (*chip, c)).wait_recv()
        up = copy(part(out_ref, c * half, half), part(out_ref, c * half, half), 7, sibling)
        up.start()
        up.wait_send()
        other = part(out_ref, (1 - c) * half, half)
        copy(other, other, 7, sibling).wait_recv()

    vmem = pl.BlockSpec(memory_space=pltpu.VMEM)
    return _pc(
        jobs, body, name="all_reduce_small", in_specs=[vmem], out_specs=vmem,
        out_shape=jax.ShapeDtypeStruct((rows, GW), F32),
        scratch_shapes=[pltpu.VMEM((half, GW), F32), pltpu.VMEM((half, GW), F32), pltpu.VMEM((3, per, GW), F32),
                        pltpu.SemaphoreType.DMA((8,)), pltpu.SemaphoreType.DMA((8,))],
    )(buf)


BIG = ("ffn1_w13", "ffn1_w2", "w_in", "w_up", "w_out", "ffn2_w13", "ffn2_w2")
BIG_COL_SHARDED = (True, False, True, True, False, True, False)
SMALL = ("ffn1_norm", "mix_norm", "pool_w", "pool_scale", "sconv_w", "cconv_w", "cconv_ln_g", "cconv_ln_b",
         "sgu_ln_g", "sgu_ln_b", "sgu_w", "sgu_b", "ffn2_norm", "final_norm")
WEIGHTS = ("ffn1_norm", "ffn1_w13", "ffn1_w2", "mix_norm", "w_in", "pool_w", "pool_scale", "sconv_w", "cconv_w",
           "cconv_ln_g", "cconv_ln_b", "sgu_ln_g", "sgu_ln_b", "sgu_w", "sgu_b", "w_up", "w_out", "ffn2_norm",
           "ffn2_w13", "ffn2_w2", "final_norm")


def _pad_rows(a, rows):
    return jnp.pad(a, ((0, 0), (0, rows - a.shape[1]), (0, 0)))


def kernel(x, ffn1_norm, ffn1_w13, ffn1_w2, mix_norm, w_in, pool_w, pool_scale, sconv_w, cconv_w, cconv_ln_g, cconv_ln_b, sgu_ln_g, sgu_ln_b, sgu_w, sgu_b, w_up, w_out, ffn2_norm, ffn2_w13, ffn2_w2, final_norm, loss_target, m_ffn1_norm, m_ffn1_w13, m_ffn1_w2, m_mix_norm, m_w_in, m_pool_w, m_pool_scale, m_sconv_w, m_cconv_w, m_cconv_ln_g, m_cconv_ln_b, m_sgu_ln_g, m_sgu_ln_b, m_sgu_w, m_sgu_b, m_w_up, m_w_out, m_ffn2_norm, m_ffn2_w13, m_ffn2_w2, m_final_norm, v_ffn1_norm, v_ffn1_w13, v_ffn1_w2, v_mix_norm, v_w_in, v_pool_w, v_pool_scale, v_sconv_w, v_cconv_w, v_cconv_ln_g, v_cconv_ln_b, v_sgu_ln_g, v_sgu_ln_b, v_sgu_w, v_sgu_b, v_w_up, v_w_out, v_ffn2_norm, v_ffn2_w13, v_ffn2_w2, v_final_norm):
    args = dict(locals())
    w = {nm: args[nm] for nm in WEIGHTS}
    m = {nm: args["m_" + nm] for nm in WEIGHTS}
    v = {nm: args["v_" + nm] for nm in WEIGHTS}
    depth = ffn1_w13.shape[0]
    chip = 2 * lax.axis_index("x") + lax.axis_index("y")

    sc = jnp.stack([lax.axis_index("c"), chip]).astype(jnp.int32)
    col_of = dict(zip(BIG, BIG_COL_SHARDED), sconv_w=True, cconv_w=True)
    sources = {nm: (w[nm], BF) for nm in BIG}
    sources["w_up"] = (w_up.reshape(depth, NBR * BW, w_up.shape[-1]), BF)
    sources["sconv_w"] = (_pad_rows(sconv_w, 2 * SKP), F32)
    sources["cconv_w"] = (_pad_rows(cconv_w, 2 * CKP), F32)
    first_group = ("ffn1_w13", "ffn1_w2")
    full = [dict() for _ in range(depth)]
    for nm in first_group:
        full[0][nm] = _cast_into(sources[nm][0], 0, col_of[nm], sources[nm][1], sc)

    def gather_job(l, names, handoff):
        return _job_gather([full[l][nm] for nm in names], [col_of[nm] for nm in names], handoff)

    def gather_with(l, names, handoff, call):
        res, job_outs = call([gather_job(l, names, handoff)] if l < depth else [])
        if l < depth:
            full[l].update(zip(names, job_outs[0]))
        return res

    rest = [(nm, l) for l in range(depth) for nm in sources if not (l == 0 and nm in first_group)]
    casted = gather_with(0, first_group, 1.0, lambda jobs: _cast_rest(
        [(sources[nm][0], l, col_of[nm], sources[nm][1]) for nm, l in rest], jobs))
    for (nm, l), arr in zip(rest, casted):
        full[l][nm] = arr

    xs = x[0]
    row = lambda a: a.reshape(1, -1)
    saved = []
    for l in range(depth):
        g1, gm, g2 = row(ffn1_norm[l]), row(mix_norm[l]), row(ffn2_norm[l])
        x1, h1, s1, ab1 = gather_with(l, ("w_in",), 1.0, lambda jobs: _ffn_fwd(
            xs, g1, full[l]["ffn1_w13"], full[l]["ffn1_w2"], jobs))
        hm, p = gather_with(l, ("w_up", "w_out", "sconv_w", "cconv_w", "ffn2_w2"), 0.85, lambda jobs: _mix_in(
            x1, gm, full[l]["w_in"], jobs))
        branch = (pool_w[l], row(pool_scale[l]), full[l]["sconv_w"][:SKP], full[l]["cconv_w"][:CKP], row(cconv_ln_g[l]),
                  row(cconv_ln_b[l]), row(sgu_ln_g[l]), row(sgu_ln_b[l]), sgu_w[l], sgu_b[l].T)
        y, conv_z = gather_with(l, ("ffn2_w13",), 1.0, lambda jobs: _mix_branches_fwd(p, *branch, jobs=jobs))
        w_up_l = full[l]["w_up"].reshape(NBR, BW, D)
        x2, merged, up = gather_with(l + 1, ("ffn1_w2",), 0.8, lambda jobs: _mix_out_fwd(
            x1, y, p, w_up_l, full[l]["w_out"], jobs))
        x3, h2, s2, ab2 = gather_with(l + 1, ("ffn1_w13",), 1.0, lambda jobs: _ffn_fwd(
            x2, g2, full[l]["ffn2_w13"], full[l]["ffn2_w2"], jobs))
        lw = dict(g1=g1, gm=gm, g2=g2, w13a=full[l]["ffn1_w13"], w2a=full[l]["ffn1_w2"], w13b=full[l]["ffn2_w13"],
                  w2b=full[l]["ffn2_w2"], w_in=full[l]["w_in"], w_up=w_up_l, w_out=full[l]["w_out"], branch=branch)
        saved.append(dict(lw=lw, x0=xs, x1=x1, x2=x2, h1=h1, s1=s1, ab1=ab1, hm=hm, p=p, y=y, z=conv_z, merged=merged, up=up, h2=h2,
                          s2=s2, ab2=ab2))
        xs = x3

    (dx, d_final, loss_part), _ = _loss_head(xs, row(final_norm), loss_target[0])
    loss = lax.psum(loss_part[0, 0], ("x", "y", "c"))

    ici_us = dict(ffn1_w13=64, ffn1_w2=32, w_in=93, w_up=23, w_out=12, ffn2_w13=64, ffn2_w2=32)
    parts, pair_sums, reduced, big_updates, pending = {}, {}, {}, {}, []

    def take_jobs(budget_us):
        chosen = []
        for task in list(pending):
            kind, (nm, _) = task
            if kind == "chip":
                if ici_us[nm] > budget_us:
                    continue
                budget_us -= ici_us[nm]
            if kind == "sib" and any(k == "sib" and key[0] == nm for k, key in chosen):
                continue
            chosen.append(task)
            pending.remove(task)
        groups, jobs = [], []
        for kind in ("pair", "sib", "chip"):
            keys = [key for k, key in chosen if k == kind]
            if not keys:
                continue
            cols = [col_of[nm] for nm, _ in keys]
            groups.append((kind, keys))
            if kind == "pair":
                jobs.append(_job_pair([parts[key] for key in keys], cols))
            elif kind == "chip":
                jobs.append(_job_chip([pair_sums[key] for key in keys], cols))
            else:
                jobs.append(_job_sibling([reduced[nm] for nm, _ in keys], cols, [layer for _, layer in keys]))
        return groups, jobs

    def settle(groups, job_outs):
        for (kind, keys), outs in zip(groups, job_outs):
            for key, out in zip(keys, outs):
                nm, layer = key
                if kind == "pair":
                    pair_sums[key] = _pair_sum(parts[key], out, col_of[nm], sc)
                    pending.append(("chip", key))
                elif kind == "chip":
                    assert not any(k == "sib" and other[0] == nm for k, other in pending)
                    reduced[nm] = _chip_sum(pair_sums[key], out, col_of[nm], layer, depth, sc, reduced.get(nm))
                    pending.append(("sib", key))
                else:
                    reduced[nm] = out
                    as3 = lambda a: a.reshape(out.shape)
                    big_updates[nm] = _adamw_layer(as3(w[nm]), out, as3(m[nm]), as3(v[nm]), layer, big_updates.get(nm))

    def run(budget_us, call, carrier=True):
        groups, jobs = take_jobs(budget_us) if carrier else ([], [])
        res, job_outs = call(jobs)
        settle(groups, job_outs)
        return res

    def wgrad_done(key, partial):
        parts[key] = partial
        pending.append(("pair", key))

    small_parts = {nm: [None] * depth for nm in SMALL if nm != "final_norm"}
    for l in reversed(range(depth)):
        sv = saved[l]
        lw = sv["lw"]
        dx, dab, dyh, dg2 = run(100, lambda jobs: _ffn_bwd(
            dx, sv["x2"], lw["g2"], sv["ab2"], lw["w13b"], lw["w2b"], jobs))
        wgrad_done(("ffn2_w13", l), run(58, lambda jobs: _wgrad(sv["h2"], dab, D, 512, "wgrad_w13", jobs), False))
        wgrad_done(("ffn2_w2", l), run(33, lambda jobs: _wgrad(sv["s2"], dyh, 256, D, "wgrad_w2", jobs), False))
        small_parts["ffn2_norm"][l] = dg2

        dy, dp, dup, dxb = run(70, lambda jobs: _mix_out_bwd(dx, sv["up"], sv["p"], lw["w_up"], lw["w_out"], jobs))
        wgrad_done(("w_out", l), run(16, lambda jobs: _wgrad(sv["merged"], dxb, D, 512, "wgrad_w_out", jobs), False))
        wgrad_done(("w_up", l), run(25, lambda jobs: _wgrad_groups(sv["y"], dup, BW, D, "wgrad_w_up", jobs), False))
        (dp, d_pool_w, d_pool_scale, d_sconv, d_cconv, d_clg, d_clb, d_slg, d_slb, d_sgu_w, d_sgu_b) = run(
            130, lambda jobs: _mix_branches_bwd(sv["p"], dy, sv["z"], dp, *lw["branch"], jobs=jobs))
        wgrad_done(("w_in", l), run(80, lambda jobs: _wgrad(sv["hm"], dp, D, 512, "wgrad_w_in", jobs), False))
        dx, dgm = run(91, lambda jobs: _mix_in_bwd(dp, lw["w_in"], sv["x1"], lw["gm"], dx, jobs))
        small_parts["mix_norm"][l] = dgm
        small_parts["pool_w"][l] = d_pool_w
        small_parts["pool_scale"][l] = d_pool_scale
        small_parts["sconv_w"][l] = d_sconv[:SK]
        small_parts["cconv_w"][l] = d_cconv[:CK]
        small_parts["cconv_ln_g"][l] = d_clg
        small_parts["cconv_ln_b"][l] = d_clb
        small_parts["sgu_ln_g"][l] = d_slg
        small_parts["sgu_ln_b"][l] = d_slb
        small_parts["sgu_w"][l] = d_sgu_w
        small_parts["sgu_b"][l] = jnp.sum(d_sgu_b, axis=-1)

        dx, dab, dyh, dg1 = run(100, lambda jobs: _ffn_bwd(
            dx, sv["x0"], lw["g1"], sv["ab1"], lw["w13a"], lw["w2a"], jobs))
        wgrad_done(("ffn1_w2", l), run(33, lambda jobs: _wgrad(sv["s1"], dyh, 256, D, "wgrad_w2", jobs), l == 0))
        wgrad_done(("ffn1_w13", l), run(58, lambda jobs: _wgrad(sv["h1"], dab, D, 512, "wgrad_w13", jobs), l == 0))
        small_parts["ffn1_norm"][l] = dg1
    grad_x = dx[None]

    small_local = {nm: jnp.stack(parts).reshape(depth, *w[nm].shape[1:-1], -1) if nm not in ("sconv_w", "cconv_w")
                   else jnp.stack(parts) for nm, parts in small_parts.items()}
    small_local["final_norm"] = d_final.reshape(-1)
    sizes = [small_local[nm].size for nm in SMALL]
    total = sum(sizes)
    pad_to = NDEV * 8 * GW
    padded = -(-total // pad_to) * pad_to
    packed = jnp.concatenate([small_local[nm].reshape(-1) for nm in SMALL] + [jnp.zeros((padded - total,), F32)])
    groups, jobs = take_jobs(float("inf"))
    summed, job_outs = _all_reduce_small(packed.reshape(-1, GW), jobs)
    settle(groups, job_outs)
    summed = summed.reshape(-1)
    flushes = 0
    while pending:
        groups, jobs = take_jobs(float("inf"))
        settle(groups, _standalone(jobs, "grad_flush_%d" % flushes))
        flushes += 1
    big_grads = {nm: big_updates[nm][3].reshape(w[nm].shape) for nm in BIG}
    small_grads, off = {}, 0
    for nm, size in zip(SMALL, sizes):
        small_grads[nm] = summed[off:off + size].reshape(small_local[nm].shape)
        off += size
    for nm in ("sconv_w", "cconv_w"):
        small_grads[nm] = lax.dynamic_slice_in_dim(small_grads[nm], chip * GW, GW, axis=2)

    grads = {**big_grads, **small_grads}

    delta, new_m, new_v = {}, {}, {}
    for nm in BIG:
        delta[nm], new_m[nm], new_v[nm] = [a.reshape(w[nm].shape) for a in big_updates[nm][:3]]
    s_sizes = [w[nm].size for nm in SMALL]
    s_total = sum(s_sizes)
    s_padded = -(-s_total // (8 * GW)) * (8 * GW)

    def pack(tree):
        return jnp.concatenate([tree[nm].reshape(-1) for nm in SMALL] + [jnp.ones((s_padded - s_total,), F32)]).reshape(-1, GW)

    packed_out = _adamw(pack(w), pack(grads), pack(m), pack(v))
    off = 0
    for nm, size in zip(SMALL, s_sizes):
        for tree, arr in zip((delta, new_m, new_v), packed_out):
            tree[nm] = arr.reshape(-1)[off:off + size].reshape(w[nm].shape)
        off += size

    return (loss, grad_x, *[grads[nm] for nm in WEIGHTS], *[delta[nm] for nm in WEIGHTS],
            *[new_m[nm] for nm in WEIGHTS], *[new_v[nm] for nm in WEIGHTS])
```
